```python
import jax, jax.numpy as jnp
from jax import lax
import numpy as np

D_MODEL = 2048
BATCH = 8
SEQ = 8192
DEPTH = 1

CHUNK = 64
EPS = 1e-6

RET_HEADS = 8
RET_DK = 256
RET_DV = 256
RET_QK_WIDTH = RET_HEADS * RET_DK
RET_WIDTH = RET_HEADS * RET_DV
ROPE_THETA = 10000.0

SSD_EXPAND = 2
SSD_WIDTH = SSD_EXPAND * D_MODEL
SSD_HEADDIM = 64
SSD_HEADS = SSD_WIDTH // SSD_HEADDIM
SSD_GROUPS = 8
SSD_HPG = SSD_HEADS // SSD_GROUPS
SSD_STATE = 128
SSD_CONV = 4
SSD_CONV_DIM = SSD_WIDTH + 2 * SSD_GROUPS * SSD_STATE
DT_MIN = 0.001
DT_MAX = 0.1

SPLITS = (RET_QK_WIDTH, RET_QK_WIDTH, RET_WIDTH, RET_WIDTH,
          SSD_WIDTH, SSD_CONV_DIM, SSD_HEADS, D_MODEL, D_MODEL)
IN_PROJ_DIM = sum(SPLITS)

kernel_name = "retention_ssd_gated_hybrid"

f32 = jnp.float32


def rmsnorm(x, w):
    xf = x.astype(f32)
    y = xf * lax.rsqrt(jnp.mean(xf * xf, axis=-1, keepdims=True) + EPS)
    return (y * w.astype(f32)).astype(x.dtype)


def to_chunks(t):
    b, s = t.shape[0], t.shape[1]
    return jnp.moveaxis(t.reshape((b, s // CHUNK, CHUNK) + t.shape[2:]), 1, 0)


def from_chunks(t):
    t = jnp.moveaxis(t, 0, 1)
    return t.reshape((t.shape[0], t.shape[1] * t.shape[2]) + t.shape[3:])


def rope(t, positions):
    half = t.shape[-1] // 2
    inv_freq = ROPE_THETA ** (-jnp.arange(half, dtype=f32) / half)
    ang = positions.astype(f32)[..., None] * inv_freq
    cos, sin = jnp.cos(ang)[:, :, None, :], jnp.sin(ang)[:, :, None, :]
    t1, t2 = t[..., :half], t[..., half:]
    return jnp.concatenate([t1 * cos - t2 * sin, t2 * cos + t1 * sin], axis=-1)


def retention(q, k, v, positions):
    b, s, _ = q.shape
    q = rope(q.reshape(b, s, RET_HEADS, RET_DK).astype(f32), positions)
    k = rope(k.reshape(b, s, RET_HEADS, RET_DK).astype(f32), positions) * (RET_DK ** -0.5)
    v = v.reshape(b, s, RET_HEADS, RET_DV).astype(f32)

    log_gamma = jnp.log1p(-(2.0 ** (-5.0 - jnp.arange(RET_HEADS, dtype=f32))))
    idx = jnp.arange(CHUNK, dtype=f32)
    intra = jnp.exp(jnp.abs(idx[:, None] - idx[None, :]) * log_gamma[:, None, None])
    q_decay = jnp.exp((idx[:, None] + 1.0) * log_gamma[None, :])[None, :, :, None]
    k_decay = jnp.exp((CHUNK - 1.0 - idx[:, None]) * log_gamma[None, :])[None, :, :, None]
    chunk_decay = jnp.exp(CHUNK * log_gamma)[None, :, None, None]

    def step(state, inp):
        qc, kc, vc = inp
        scores = jnp.einsum('blhd,bshd->bhls', qc, kc) * intra
        y = jnp.einsum('bhls,bshv->blhv', scores, vc)
        y = y + jnp.einsum('blhd,bhdv->blhv', qc, state) * q_decay
        state = state * chunk_decay + jnp.einsum('bshd,bshv->bhdv', kc * k_decay, vc)
        return state, y

    state0 = jnp.zeros((b, RET_HEADS, RET_DK, RET_DV), f32)
    _, y = lax.scan(step, state0, (to_chunks(q), to_chunks(k), to_chunks(v)))
    y = from_chunks(y)
    mu = jnp.mean(y, axis=-1, keepdims=True)
    var = jnp.mean(jnp.square(y - mu), axis=-1, keepdims=True)
    y = (y - mu) * lax.rsqrt(var + EPS)
    return y.reshape(b, s, RET_WIDTH)


def ssd(xbc, dt_raw, conv_w, conv_b, dt_bias, a_log, d_skip):
    b, s, _ = xbc.shape
    xbc = lax.conv_general_dilated(
        xbc.astype(f32), conv_w.astype(f32)[:, None, :], window_strides=(1,),
        padding=[(SSD_CONV - 1, 0)], dimension_numbers=('NWC', 'WIO', 'NWC'),
        feature_group_count=SSD_CONV_DIM) + conv_b.astype(f32)
    xbc = jax.nn.silu(xbc)
    gn = SSD_GROUPS * SSD_STATE
    xs = xbc[..., :SSD_WIDTH].reshape(b, s, SSD_GROUPS, SSD_HPG, SSD_HEADDIM)
    bm = xbc[..., SSD_WIDTH:SSD_WIDTH + gn].reshape(b, s, SSD_GROUPS, SSD_STATE)
    cm = xbc[..., SSD_WIDTH + gn:].reshape(b, s, SSD_GROUPS, SSD_STATE)
    dt = jax.nn.softplus(dt_raw.astype(f32) + dt_bias.astype(f32)).reshape(b, s, SSD_GROUPS, SSD_HPG)
    a = dt * (-jnp.exp(a_log.astype(f32))).reshape(SSD_GROUPS, SSD_HPG)
    xdt = xs * dt[..., None]
    causal = jnp.tril(jnp.ones((CHUNK, CHUNK), dtype=bool))[None, :, :, None, None]

    def step(state, inp):
        xc, bc, cc, ac = inp
        acum = jnp.cumsum(ac, axis=1)
        seg = acum[:, :, None] - acum[:, None, :]
        decay = jnp.exp(jnp.where(causal, seg, -jnp.inf))
        cb = jnp.einsum('blgn,bsgn->blsg', cc, bc)
        y = jnp.einsum('blsg,blsgh,bsghp->blghp', cb, decay, xc)
        y = y + jnp.einsum('blgn,bghpn->blghp', cc, state) * jnp.exp(acum)[..., None]
        tail = jnp.exp(acum[:, -1:] - acum)
        state = (state * jnp.exp(acum[:, -1])[..., None, None]
                 + jnp.einsum('bsgn,bsgh,bsghp->bghpn', bc, tail, xc))
        return state, y

    state0 = jnp.zeros((b, SSD_GROUPS, SSD_HPG, SSD_HEADDIM, SSD_STATE), f32)
    _, y = lax.scan(step, state0, (to_chunks(xdt), to_chunks(bm), to_chunks(cm), to_chunks(a)))
    y = from_chunks(y) + d_skip.astype(f32).reshape(SSD_GROUPS, SSD_HPG)[..., None] * xs
    return y.reshape(b, s, SSD_WIDTH)


def _fwd_setup_inputs(seed: int = 0) -> dict:
    key = jax.random.key(seed)
    ks = jax.random.split(key, 16)
    x = jax.random.normal(ks[0], (BATCH, SEQ, D_MODEL), f32)
    offset = jax.random.randint(ks[1], (BATCH, 1), 0, 100000, dtype=jnp.int32)
    positions = offset + jnp.arange(SEQ, dtype=jnp.int32)[None, :]
    norm1_w = 1.0 + 0.02 * jax.random.normal(ks[2], (DEPTH, D_MODEL), f32)
    w_in = jax.random.normal(ks[3], (DEPTH, D_MODEL, IN_PROJ_DIM), f32) * D_MODEL ** -0.5
    conv_w = jax.random.normal(ks[4], (DEPTH, SSD_CONV, SSD_CONV_DIM), f32) * SSD_CONV ** -0.5
    conv_b = 0.01 * jax.random.normal(ks[5], (DEPTH, SSD_CONV_DIM), f32)
    u = jax.random.uniform(ks[6], (DEPTH, SSD_HEADS), f32)
    dt0 = jnp.exp(u * (np.log(DT_MAX) - np.log(DT_MIN)) + np.log(DT_MIN))
    dt_bias = dt0 + jnp.log(-jnp.expm1(-dt0))
    a_log = jnp.log(jax.random.uniform(ks[7], (DEPTH, SSD_HEADS), f32, minval=1.0, maxval=16.0))
    d_skip = 1.0 + 0.1 * jax.random.normal(ks[8], (DEPTH, SSD_HEADS), f32)
    ssd_norm_w = 1.0 + 0.02 * jax.random.normal(ks[9], (DEPTH, SSD_WIDTH), f32)
    w_br_ret = jax.random.normal(ks[10], (DEPTH, RET_WIDTH, D_MODEL), f32) * RET_WIDTH ** -0.5
    w_br_ssd = jax.random.normal(ks[11], (DEPTH, SSD_WIDTH, D_MODEL), f32) * SSD_WIDTH ** -0.5
    w_out = jax.random.normal(ks[12], (DEPTH, D_MODEL, D_MODEL), f32) * D_MODEL ** -0.5
    norm_f_w = 1.0 + 0.02 * jax.random.normal(ks[13], (D_MODEL,), f32)
    return {"x": x, "positions": positions, "norm1_w": norm1_w, "w_in": w_in,
            "conv_w": conv_w, "conv_b": conv_b, "dt_bias": dt_bias, "a_log": a_log,
            "d_skip": d_skip, "ssd_norm_w": ssd_norm_w, "w_br_ret": w_br_ret,
            "w_br_ssd": w_br_ssd, "w_out": w_out, "norm_f_w": norm_f_w}


def _fwd_reference(x, positions, norm1_w, w_in, conv_w, conv_b, dt_bias, a_log, d_skip,
              ssd_norm_w, w_br_ret, w_br_ssd, w_out, norm_f_w):
    offsets = [int(o) for o in np.cumsum(SPLITS)[:-1]]
    for l in range(DEPTH):
        h = rmsnorm(x, norm1_w[l])
        proj = jnp.einsum('bsd,de->bse', h, w_in[l])
        q, k, v, g_ret, z, xbc, dt_raw, gate_r, gate_s = jnp.split(proj, offsets, axis=-1)
        y_r = (retention(q, k, v, positions) * jax.nn.silu(g_ret.astype(f32))).astype(x.dtype)
        y_s = ssd(xbc, dt_raw, conv_w[l], conv_b[l], dt_bias[l], a_log[l], d_skip[l])
        y_s = rmsnorm((y_s * jax.nn.silu(z.astype(f32))).astype(x.dtype), ssd_norm_w[l])
        p_r = jnp.einsum('bse,ed->bsd', y_r, w_br_ret[l])
        p_s = jnp.einsum('bse,ed->bsd', y_s, w_br_ssd[l])
        merged = jax.nn.sigmoid(gate_r) * p_r + jax.nn.sigmoid(gate_s) * p_s
        x = x + jnp.einsum('bsd,de->bse', merged, w_out[l])
    return rmsnorm(x, norm_f_w)


import jax as _jax
import jax.numpy as _jnp

TWIN_FORMAT = 'train_step'
FWD_PARAMS = ['x', 'positions', 'norm1_w', 'w_in', 'conv_w', 'conv_b', 'dt_bias', 'a_log', 'd_skip', 'ssd_norm_w', 'w_br_ret', 'w_br_ssd', 'w_out', 'norm_f_w']
TWIN_WEIGHTS = ['norm1_w', 'w_in', 'conv_w', 'conv_b', 'dt_bias', 'a_log', 'd_skip', 'ssd_norm_w', 'w_br_ret', 'w_br_ssd', 'w_out', 'norm_f_w']
TWIN_DIFF_INPUT = 'x'
TWIN_INPUTS = ['x', 'positions', 'norm1_w', 'w_in', 'conv_w', 'conv_b', 'dt_bias', 'a_log', 'd_skip', 'ssd_norm_w', 'w_br_ret', 'w_br_ssd', 'w_out', 'norm_f_w', 'loss_target', 'm_norm1_w', 'm_w_in', 'm_conv_w', 'm_conv_b', 'm_dt_bias', 'm_a_log', 'm_d_skip', 'm_ssd_norm_w', 'm_w_br_ret', 'm_w_br_ssd', 'm_w_out', 'm_norm_f_w', 'v_norm1_w', 'v_w_in', 'v_conv_w', 'v_conv_b', 'v_dt_bias', 'v_a_log', 'v_d_skip', 'v_ssd_norm_w', 'v_w_br_ret', 'v_w_br_ssd', 'v_w_out', 'v_norm_f_w']
TWIN_OUTPUTS = ['loss', 'grad_x', 'grad_norm1_w', 'grad_w_in', 'grad_conv_w', 'grad_conv_b', 'grad_dt_bias', 'grad_a_log', 'grad_d_skip', 'grad_ssd_norm_w', 'grad_w_br_ret', 'grad_w_br_ssd', 'grad_w_out', 'grad_norm_f_w', 'delta_norm1_w', 'delta_w_in', 'delta_conv_w', 'delta_conv_b', 'delta_dt_bias', 'delta_a_log', 'delta_d_skip', 'delta_ssd_norm_w', 'delta_w_br_ret', 'delta_w_br_ssd', 'delta_w_out', 'delta_norm_f_w', 'new_m_norm1_w', 'new_m_w_in', 'new_m_conv_w', 'new_m_conv_b', 'new_m_dt_bias', 'new_m_a_log', 'new_m_d_skip', 'new_m_ssd_norm_w', 'new_m_w_br_ret', 'new_m_w_br_ssd', 'new_m_w_out', 'new_m_norm_f_w', 'new_v_norm1_w', 'new_v_w_in', 'new_v_conv_w', 'new_v_conv_b', 'new_v_dt_bias', 'new_v_a_log', 'new_v_d_skip', 'new_v_ssd_norm_w', 'new_v_w_br_ret', 'new_v_w_br_ssd', 'new_v_w_out', 'new_v_norm_f_w']
TWIN_LEAF_KINDS = {'loss': 'loss', 'grad_x': 'grad_x', 'grad_norm1_w': 'grad_w', 'grad_w_in': 'grad_w', 'grad_conv_w': 'grad_w', 'grad_conv_b': 'grad_w', 'grad_dt_bias': 'grad_w', 'grad_a_log': 'grad_w', 'grad_d_skip': 'grad_w', 'grad_ssd_norm_w': 'grad_w', 'grad_w_br_ret': 'grad_w', 'grad_w_br_ssd': 'grad_w', 'grad_w_out': 'grad_w', 'grad_norm_f_w': 'grad_w', 'delta_norm1_w': 'delta_w', 'delta_w_in': 'delta_w', 'delta_conv_w': 'delta_w', 'delta_conv_b': 'delta_w', 'delta_dt_bias': 'delta_w', 'delta_a_log': 'delta_w', 'delta_d_skip': 'delta_w', 'delta_ssd_norm_w': 'delta_w', 'delta_w_br_ret': 'delta_w', 'delta_w_br_ssd': 'delta_w', 'delta_w_out': 'delta_w', 'delta_norm_f_w': 'delta_w', 'new_m_norm1_w': 'new_m', 'new_m_w_in': 'new_m', 'new_m_conv_w': 'new_m', 'new_m_conv_b': 'new_m', 'new_m_dt_bias': 'new_m', 'new_m_a_log': 'new_m', 'new_m_d_skip': 'new_m', 'new_m_ssd_norm_w': 'new_m', 'new_m_w_br_ret': 'new_m', 'new_m_w_br_ssd': 'new_m', 'new_m_w_out': 'new_m', 'new_m_norm_f_w': 'new_m', 'new_v_norm1_w': 'new_v', 'new_v_w_in': 'new_v', 'new_v_conv_w': 'new_v', 'new_v_conv_b': 'new_v', 'new_v_dt_bias': 'new_v', 'new_v_a_log': 'new_v', 'new_v_d_skip': 'new_v', 'new_v_ssd_norm_w': 'new_v', 'new_v_w_br_ret': 'new_v', 'new_v_w_br_ssd': 'new_v', 'new_v_w_out': 'new_v', 'new_v_norm_f_w': 'new_v'}


def _forward(args):
    return _fwd_reference(*[args[k] for k in FWD_PARAMS])


def _output_shape():
    def fwd():
        inp = _fwd_setup_inputs(0)
        return _fwd_reference(*[inp[k] for k in FWD_PARAMS])
    out = _jax.eval_shape(fwd)
    return out.shape, out.dtype

N_MICROBATCH = 1
ADAM_LR = 0.001
ADAM_B1 = 0.9
ADAM_B2 = 0.999
ADAM_EPS = 1e-08
ADAM_WD = 0.01
ADAM_STEP = 10
PER_EXAMPLE_BATCH_AXIS = {'x': 0, 'positions': 0, 'loss_target': 0}
SHARED_INPUTS = []
_WEIGHT_DTYPES = {'norm1_w': _jnp.float32, 'w_in': _jnp.float32, 'conv_w': _jnp.float32, 'conv_b': _jnp.float32, 'dt_bias': _jnp.float32, 'a_log': _jnp.float32, 'd_skip': _jnp.float32, 'ssd_norm_w': _jnp.float32, 'w_br_ret': _jnp.float32, 'w_br_ssd': _jnp.float32, 'w_out': _jnp.float32, 'norm_f_w': _jnp.float32}
MOMENT_SCALE = {'norm1_w': 1.142111e-01, 'w_in': 3.467376e-02, 'conv_w': 3.568331e-02, 'conv_b': 4.889146e-02, 'dt_bias': 1.050616e-01, 'a_log': 1.206475e-01, 'd_skip': 2.196655e-01, 'ssd_norm_w': 4.026348e-02, 'w_br_ret': 3.426528e-02, 'w_br_ssd': 5.780387e-02, 'w_out': 6.721335e-02, 'norm_f_w': 3.198289e+01}


def _to_microbatches(a, axis):
    t = _jnp.moveaxis(a, axis, 0)
    t = t.reshape((N_MICROBATCH, t.shape[0] // N_MICROBATCH) + t.shape[1:])
    return _jnp.moveaxis(t, 1, axis + 1)


def setup_inputs(seed: int = 0) -> dict:
    inp = _fwd_setup_inputs(seed)
    key = _jax.random.fold_in(_jax.random.key(seed), 7919)
    shape, _ = _output_shape()
    out = dict(inp)
    out["loss_target"] = _jax.random.normal(_jax.random.fold_in(key, 0), shape, _jnp.float32)
    for i, name in enumerate(TWIN_WEIGHTS):
        w = inp[name].astype(_jnp.float32)
        if MOMENT_SCALE is None:
            s = _jnp.sqrt(_jnp.mean(_jnp.square(w)) + 1e-30)
        else:
            s = MOMENT_SCALE[name]
        km, kv = _jax.random.split(_jax.random.fold_in(key, i + 1))
        out[name] = w
        out["m_" + name] = s * _jax.random.normal(km, w.shape, _jnp.float32)
        out["v_" + name] = (s * s) * _jax.random.uniform(kv, w.shape, _jnp.float32, 0.5, 1.5)
    if N_MICROBATCH > 1:
        for name, axis in PER_EXAMPLE_BATCH_AXIS.items():
            out[name] = _to_microbatches(out[name], axis)
    return {'x': out['x'], 'positions': out['positions'], 'norm1_w': out['norm1_w'], 'w_in': out['w_in'], 'conv_w': out['conv_w'], 'conv_b': out['conv_b'], 'dt_bias': out['dt_bias'], 'a_log': out['a_log'], 'd_skip': out['d_skip'], 'ssd_norm_w': out['ssd_norm_w'], 'w_br_ret': out['w_br_ret'], 'w_br_ssd': out['w_br_ssd'], 'w_out': out['w_out'], 'norm_f_w': out['norm_f_w'], 'loss_target': out['loss_target'], 'm_norm1_w': out['m_norm1_w'], 'm_w_in': out['m_w_in'], 'm_conv_w': out['m_conv_w'], 'm_conv_b': out['m_conv_b'], 'm_dt_bias': out['m_dt_bias'], 'm_a_log': out['m_a_log'], 'm_d_skip': out['m_d_skip'], 'm_ssd_norm_w': out['m_ssd_norm_w'], 'm_w_br_ret': out['m_w_br_ret'], 'm_w_br_ssd': out['m_w_br_ssd'], 'm_w_out': out['m_w_out'], 'm_norm_f_w': out['m_norm_f_w'], 'v_norm1_w': out['v_norm1_w'], 'v_w_in': out['v_w_in'], 'v_conv_w': out['v_conv_w'], 'v_conv_b': out['v_conv_b'], 'v_dt_bias': out['v_dt_bias'], 'v_a_log': out['v_a_log'], 'v_d_skip': out['v_d_skip'], 'v_ssd_norm_w': out['v_ssd_norm_w'], 'v_w_br_ret': out['v_w_br_ret'], 'v_w_br_ssd': out['v_w_br_ssd'], 'v_w_out': out['v_w_out'], 'v_norm_f_w': out['v_norm_f_w']}


def _loss(weights, diff, rest, loss_target):
    with _jax.named_scope("forward"):
        args = {**rest, TWIN_DIFF_INPUT: diff, **{k: w.astype(_WEIGHT_DTYPES[k]) for k, w in weights.items()}}
        y = _forward(args)
    with _jax.named_scope("loss_head"):
        err = _jnp.square(y.astype(_jnp.float32) - loss_target)
        return 0.5 * _jnp.sum(_jnp.mean(err, axis=-1)) if err.ndim else 0.5 * err


def _adamw(w, g, m, v):
    m = ADAM_B1 * m + (1.0 - ADAM_B1) * g
    v = ADAM_B2 * v + (1.0 - ADAM_B2) * _jnp.square(g)
    m_hat = m / (1.0 - ADAM_B1 ** ADAM_STEP)
    v_hat = v / (1.0 - ADAM_B2 ** ADAM_STEP)
    delta = -ADAM_LR * (m_hat / (_jnp.sqrt(v_hat) + ADAM_EPS) + ADAM_WD * w)
    return delta, m, v


def reference(x, positions, norm1_w, w_in, conv_w, conv_b, dt_bias, a_log, d_skip, ssd_norm_w, w_br_ret, w_br_ssd, w_out, norm_f_w, loss_target, m_norm1_w, m_w_in, m_conv_w, m_conv_b, m_dt_bias, m_a_log, m_d_skip, m_ssd_norm_w, m_w_br_ret, m_w_br_ssd, m_w_out, m_norm_f_w, v_norm1_w, v_w_in, v_conv_w, v_conv_b, v_dt_bias, v_a_log, v_d_skip, v_ssd_norm_w, v_w_br_ret, v_w_br_ssd, v_w_out, v_norm_f_w):
    given = dict(x=x, positions=positions, norm1_w=norm1_w, w_in=w_in, conv_w=conv_w, conv_b=conv_b, dt_bias=dt_bias, a_log=a_log, d_skip=d_skip, ssd_norm_w=ssd_norm_w, w_br_ret=w_br_ret, w_br_ssd=w_br_ssd, w_out=w_out, norm_f_w=norm_f_w, loss_target=loss_target, m_norm1_w=m_norm1_w, m_w_in=m_w_in, m_conv_w=m_conv_w, m_conv_b=m_conv_b, m_dt_bias=m_dt_bias, m_a_log=m_a_log, m_d_skip=m_d_skip, m_ssd_norm_w=m_ssd_norm_w, m_w_br_ret=m_w_br_ret, m_w_br_ssd=m_w_br_ssd, m_w_out=m_w_out, m_norm_f_w=m_norm_f_w, v_norm1_w=v_norm1_w, v_w_in=v_w_in, v_conv_w=v_conv_w, v_conv_b=v_conv_b, v_dt_bias=v_dt_bias, v_a_log=v_a_log, v_d_skip=v_d_skip, v_ssd_norm_w=v_ssd_norm_w, v_w_br_ret=v_w_br_ret, v_w_br_ssd=v_w_br_ssd, v_w_out=v_w_out, v_norm_f_w=v_norm_f_w)
    weights = {n: given[n] for n in TWIN_WEIGHTS}
    shared = {n: given[n] for n in SHARED_INPUTS}
    per_example = {n: given[n] for n in ['x', 'positions']}
    grad_fn = _jax.value_and_grad(_loss, argnums=(0, 1))

    def one_microbatch(ex, loss_target):
        ex = dict(ex)
        diff = ex.pop(TWIN_DIFF_INPUT)
        return grad_fn(weights, diff, {**shared, **ex}, loss_target)

    if N_MICROBATCH == 1:
        loss, (grad_w, grad_x) = one_microbatch(per_example, given["loss_target"])
    else:
        def body(carry, xs):
            loss_sum, grad_sum = carry
            l_k, (gw_k, gx_k) = one_microbatch(xs[0], xs[1])
            with _jax.named_scope("update"):
                return (loss_sum + l_k, _jax.tree.map(_jnp.add, grad_sum, gw_k)), gx_k

        init = (_jnp.zeros((), _jnp.float32), _jax.tree.map(_jnp.zeros_like, weights))
        (loss, grad_w), grad_x = _jax.lax.scan(body, init, (per_example, given["loss_target"]))
    with _jax.named_scope("update"):
        delta_w, new_m, new_v = {}, {}, {}
        for n in TWIN_WEIGHTS:
            delta_w[n], new_m[n], new_v[n] = _adamw(weights[n], grad_w[n], given["m_" + n], given["v_" + n])
    return (loss, grad_x, *[grad_w[n] for n in TWIN_WEIGHTS], *[delta_w[n] for n in TWIN_WEIGHTS],
            *[new_m[n] for n in TWIN_WEIGHTS], *[new_v[n] for n in TWIN_WEIGHTS])
```

```python
import functools

import numpy as np
import jax
import jax.numpy as jnp
from jax import lax
from jax.experimental import pallas as pl
from jax.experimental.pallas import tpu as pltpu

F32 = jnp.float32
BF16 = jnp.bfloat16

D_MODEL = 2048
CHUNK = 64
EPS = 1e-6
N_DEV = 8

RET_HEADS = 8
RET_DK = 256
RET_W = RET_HEADS * RET_DK
ROPE_THETA = 10000.0
ROPE_HALF = RET_DK // 2

SSD_W = 4096
SSD_P = 64
SSD_HEADS = 64
SSD_GROUPS = 8
SSD_N = 128
SSD_GW = SSD_W // SSD_GROUPS
SSD_QW = 256
SSD_CONV = 4
SSD_CD = SSD_W + 2 * SSD_GROUPS * SSD_N
HEAD_PAD = 128
B_OFF = SSD_W
C_OFF = SSD_W + SSD_GROUPS * SSD_N

ADAM_LR = 0.001
ADAM_B1 = 0.9
ADAM_B2 = 0.999
ADAM_EPS = 1e-08
ADAM_WD = 0.01
ADAM_STEP = 10

SPLITS = (RET_W, RET_W, RET_W, RET_W, SSD_W, SSD_CD, SSD_HEADS, D_MODEL, D_MODEL)
IN_PROJ = sum(SPLITS)
OFF_Z = 4 * RET_W
OFF_XBC = OFF_Z + SSD_W
OFF_DT = OFF_XBC + SSD_CD
OFF_G = OFF_DT + SSD_HEADS

ROW_TILE = 256
CONV_TILE = 128
MM_TILE = 1024
MM_TK = 512

MESH = pl.DeviceIdType.MESH


def _dot(a, b):
    return lax.dot_general(a, b, (((1,), (0,)), ((), ())), preferred_element_type=F32)


def _dot_nt(a, b):
    return lax.dot_general(a, b, (((1,), (1,)), ((), ())), preferred_element_type=F32)


def _dot_tn(a, b):
    return lax.dot_general(a, b, (((0,), (0,)), ((), ())), preferred_element_type=F32)


def _bf(x):
    return x.astype(BF16)


def _split3(x):
    hi = x.astype(BF16)
    r = x - hi.astype(F32)
    mid = r.astype(BF16)
    lo = (r - mid.astype(F32)).astype(BF16)
    return hi, mid, lo


def _dot_exact_l(x, sel):
    hi, mid, lo = _split3(x)
    return _dot(hi, sel) + _dot(mid, sel) + _dot(lo, sel)


def _dot_exact_r(sel, x):
    hi, mid, lo = _split3(x)
    return _dot(sel, hi) + _dot(sel, mid) + _dot(sel, lo)


def _sigmoid(x):
    return 1.0 / (1.0 + jnp.exp(-x))


def _softplus(x):
    return jnp.maximum(x, 0.0) + jnp.log(1.0 + jnp.exp(-jnp.abs(x)))


def _iota(shape, axis):
    return lax.broadcasted_iota(jnp.int32, shape, axis)


def _params(*sem):
    return pltpu.CompilerParams(dimension_semantics=sem)


def _matmul(a, b, mode, out_dtype, name, add=None):
    if mode == "nn":
        (m, k), (k2, n) = a.shape, b.shape
    elif mode == "nt":
        (m, k), (n, k2) = a.shape, b.shape
    else:
        (k, m), (k2, n) = a.shape, b.shape
    assert k == k2, (a.shape, b.shape, mode)
    tm, tn, tk = min(m, MM_TILE), min(n, MM_TILE), min(k, MM_TK)
    assert m % tm == 0 and n % tn == 0 and k % tk == 0, (m, n, k)
    nk = k // tk
    dot = {"nn": _dot, "nt": _dot_nt, "tn": _dot_tn}[mode]

    def body(*refs):
        if add is None:
            a_ref, b_ref, o_ref, acc_ref = refs
            add_ref = None
        else:
            a_ref, b_ref, add_ref, o_ref, acc_ref = refs
        kk = pl.program_id(2)

        @pl.when(kk == 0)
        def _():
            acc_ref[...] = jnp.zeros_like(acc_ref)

        acc_ref[...] += dot(_bf(a_ref[...]), _bf(b_ref[...]))

        @pl.when(kk == nk - 1)
        def _():
            r = acc_ref[...]
            if add_ref is not None:
                r = r + add_ref[...]
            o_ref[...] = r.astype(o_ref.dtype)

    if mode == "nn":
        a_spec = pl.BlockSpec((tm, tk), lambda i, j, kk: (i, kk))
        b_spec = pl.BlockSpec((tk, tn), lambda i, j, kk: (kk, j))
    elif mode == "nt":
        a_spec = pl.BlockSpec((tm, tk), lambda i, j, kk: (i, kk))
        b_spec = pl.BlockSpec((tn, tk), lambda i, j, kk: (j, kk))
    else:
        a_spec = pl.BlockSpec((tk, tm), lambda i, j, kk: (kk, i))
        b_spec = pl.BlockSpec((tk, tn), lambda i, j, kk: (kk, j))
    o_spec = pl.BlockSpec((tm, tn), lambda i, j, kk: (i, j))
    in_specs = [a_spec, b_spec] + ([o_spec] if add is not None else [])
    args = (a, b) + ((add,) if add is not None else ())
    return pl.pallas_call(
        body,
        out_shape=jax.ShapeDtypeStruct((m, n), out_dtype),
        grid=(m // tm, n // tn, nk),
        in_specs=in_specs,
        out_specs=o_spec,
        scratch_shapes=[pltpu.VMEM((tm, tn), F32)],
        compiler_params=_params("parallel", "parallel", "arbitrary"),
        name=name,
    )(*args)


def _row_spec(width, tile=ROW_TILE):
    return pl.BlockSpec((tile, width), lambda i: (i, 0))


def _full_spec(shape):
    nd = len(shape)
    return pl.BlockSpec(shape, lambda *_: (0,) * nd)


def _cast_bf16(x, name):
    r, c = x.shape
    tr = min(r, ROW_TILE)

    def body(x_ref, o_ref):
        o_ref[...] = _bf(x_ref[...])

    return pl.pallas_call(
        body, out_shape=jax.ShapeDtypeStruct((r, c), BF16), grid=(r // tr,),
        in_specs=[_row_spec(c, tr)], out_specs=_row_spec(c, tr),
        compiler_params=_params("parallel"), name=name)(x)


def _rmsnorm_fwd(x, w, name):
    s, d = x.shape

    def body(x_ref, w_ref, o_ref):
        xv = x_ref[...]
        rstd = lax.rsqrt(jnp.mean(xv * xv, axis=1, keepdims=True) + EPS)
        o_ref[...] = _bf(xv * rstd * w_ref[...])

    return pl.pallas_call(
        body, out_shape=jax.ShapeDtypeStruct((s, d), BF16), grid=(s // ROW_TILE,),
        in_specs=[_row_spec(d), _full_spec((1, d))], out_specs=_row_spec(d),
        compiler_params=_params("parallel"), name=name)(x, w)


def _rmsnorm_bwd(x, w, dh, dres, name):
    s, d = x.shape

    def body(x_ref, w_ref, dh_ref, dres_ref, dx_ref, dw_ref):
        @pl.when(pl.program_id(0) == 0)
        def _():
            dw_ref[...] = jnp.zeros_like(dw_ref)

        xv = x_ref[...]
        rstd = lax.rsqrt(jnp.mean(xv * xv, axis=1, keepdims=True) + EPS)
        xhat = xv * rstd
        dhv = dh_ref[...]
        dxhat = dhv * w_ref[...]
        dx = rstd * (dxhat - xhat * jnp.mean(dxhat * xhat, axis=1, keepdims=True))
        dx_ref[...] = dx + dres_ref[...]
        dw_ref[...] += jnp.sum(dhv * xhat, axis=0, keepdims=True)

    return pl.pallas_call(
        body,
        out_shape=(jax.ShapeDtypeStruct((s, d), F32), jax.ShapeDtypeStruct((1, d), F32)),
        grid=(s // ROW_TILE,),
        in_specs=[_row_spec(d), _full_spec((1, d)), _row_spec(d), _row_spec(d)],
        out_specs=(_row_spec(d), _full_spec((1, d))),
        compiler_params=_params("arbitrary"), name=name)(x, w, dh, dres)


def _rope_tables(pos_col, inv_freq, name):
    s = pos_col.shape[0]

    def body(p_ref, f_ref, cos_ref, sin_ref):
        ang = p_ref[...].astype(F32) * f_ref[...]
        cos_ref[...] = jnp.cos(ang)
        sin_ref[...] = jnp.sin(ang)

    out = jax.ShapeDtypeStruct((s, ROPE_HALF), F32)
    return pl.pallas_call(
        body, out_shape=(out, out), grid=(s // ROW_TILE,),
        in_specs=[_row_spec(1), _full_spec((1, ROPE_HALF))],
        out_specs=(_row_spec(ROPE_HALF), _row_spec(ROPE_HALF)),
        compiler_params=_params("parallel"), name=name)(pos_col, inv_freq)


def _merge_fwd(pg, p_r, p_s, name):
    s = pg.shape[0]

    def body(g_ref, r_ref, s_ref, o_ref):
        g = g_ref[...]
        o_ref[...] = _bf(_sigmoid(g[:, :D_MODEL]) * r_ref[...] + _sigmoid(g[:, D_MODEL:]) * s_ref[...])

    return pl.pallas_call(
        body, out_shape=jax.ShapeDtypeStruct((s, D_MODEL), BF16), grid=(s // ROW_TILE,),
        in_specs=[_row_spec(2 * D_MODEL), _row_spec(D_MODEL), _row_spec(D_MODEL)],
        out_specs=_row_spec(D_MODEL), compiler_params=_params("parallel"), name=name)(pg, p_r, p_s)


def _merge_bwd(pg, p_r, p_s, dm, name):
    s = pg.shape[0]

    def body(g_ref, r_ref, s_ref, dm_ref, dr_ref, ds_ref, dg_ref):
        g = g_ref[...]
        sr, ss = _sigmoid(g[:, :D_MODEL]), _sigmoid(g[:, D_MODEL:])
        d = dm_ref[...]
        dr_ref[...] = _bf(d * sr)
        ds_ref[...] = _bf(d * ss)
        dg_ref[:, :D_MODEL] = _bf(d * r_ref[...] * sr * (1.0 - sr))
        dg_ref[:, D_MODEL:] = _bf(d * s_ref[...] * ss * (1.0 - ss))

    o = jax.ShapeDtypeStruct((s, D_MODEL), BF16)
    return pl.pallas_call(
        body, out_shape=(o, o, jax.ShapeDtypeStruct((s, 2 * D_MODEL), BF16)), grid=(s // ROW_TILE,),
        in_specs=[_row_spec(2 * D_MODEL), _row_spec(D_MODEL), _row_spec(D_MODEL), _row_spec(D_MODEL)],
        out_specs=(_row_spec(D_MODEL), _row_spec(D_MODEL), _row_spec(2 * D_MODEL)),
        compiler_params=_params("parallel"), name=name)(pg, p_r, p_s, dm)


def _final_fwd_bwd(x, o, w, target, name):
    s, d = x.shape

    def body(x_ref, o_ref, w_ref, t_ref, dx_ref, dw_ref, loss_ref):
        @pl.when(pl.program_id(0) == 0)
        def _():
            dw_ref[...] = jnp.zeros_like(dw_ref)
            loss_ref[...] = jnp.zeros_like(loss_ref)

        x2 = x_ref[...] + o_ref[...]
        rstd = lax.rsqrt(jnp.mean(x2 * x2, axis=1, keepdims=True) + EPS)
        xhat = x2 * rstd
        wv = w_ref[...]
        err = xhat * wv - t_ref[...]
        loss_ref[...] += jnp.sum(jnp.sum(err * err, axis=1, keepdims=True), axis=0, keepdims=True) * (0.5 / d)
        dy = err * (1.0 / d)
        dw_ref[...] += jnp.sum(dy * xhat, axis=0, keepdims=True)
        dxhat = dy * wv
        dx_ref[...] = rstd * (dxhat - xhat * jnp.mean(dxhat * xhat, axis=1, keepdims=True))

    return pl.pallas_call(
        body,
        out_shape=(jax.ShapeDtypeStruct((s, d), F32), jax.ShapeDtypeStruct((1, d), F32),
                   jax.ShapeDtypeStruct((8, 128), F32)),
        grid=(s // ROW_TILE,),
        in_specs=[_row_spec(d), _row_spec(d), _full_spec((1, d)), _row_spec(d)],
        out_specs=(_row_spec(d), _full_spec((1, d)), _full_spec((8, 128))),
        compiler_params=_params("arbitrary"), name=name)(x, o, w, target)


def _adamw(w, g, m, v, name):
    r, c = w.shape
    tr = min(r, ROW_TILE)
    c1 = 1.0 / (1.0 - ADAM_B1 ** ADAM_STEP)
    c2 = 1.0 / (1.0 - ADAM_B2 ** ADAM_STEP)

    def body(w_ref, g_ref, m_ref, v_ref, d_ref, nm_ref, nv_ref):
        gv = g_ref[...]
        nm = ADAM_B1 * m_ref[...] + (1.0 - ADAM_B1) * gv
        nv = ADAM_B2 * v_ref[...] + (1.0 - ADAM_B2) * (gv * gv)
        d_ref[...] = -ADAM_LR * ((nm * c1) / (jnp.sqrt(nv * c2) + ADAM_EPS) + ADAM_WD * w_ref[...])
        nm_ref[...] = nm
        nv_ref[...] = nv

    o = jax.ShapeDtypeStruct((r, c), F32)
    sp = _row_spec(c, tr)
    return pl.pallas_call(
        body, out_shape=(o, o, o), grid=(r // tr,), in_specs=[sp, sp, sp, sp], out_specs=(sp, sp, sp),
        compiler_params=_params("parallel"), name=name)(w, g, m, v)


def _sum_slots(land, name):
    _, r, c = land.shape
    tr = min(r, ROW_TILE)

    def body(l_ref, o_ref):
        acc = l_ref[0].astype(F32)
        for i in range(1, N_DEV):
            acc = acc + l_ref[i].astype(F32)
        o_ref[...] = acc

    return pl.pallas_call(
        body, out_shape=jax.ShapeDtypeStruct((r, c), F32), grid=(r // tr,),
        in_specs=[pl.BlockSpec((N_DEV, tr, c), lambda i: (0, i, 0))], out_specs=_row_spec(c, tr),
        compiler_params=_params("parallel"), name=name)(land)


def _retention_tables():
    lg = np.log1p(-(2.0 ** (-5.0 - np.arange(RET_HEADS, dtype=np.float64))))
    idx = np.arange(CHUNK, dtype=np.float64)
    intra = np.exp(np.abs(idx[:, None] - idx[None, :])[None] * lg[:, None, None])
    qd = np.exp((idx[None, :] + 1.0) * lg[:, None])
    kd = np.exp((CHUNK - 1.0 - idx[None, :]) * lg[:, None])
    cd = np.exp(CHUNK * lg)
    ones = np.ones((1, 1, RET_DK))
    return (jnp.asarray(intra, F32), jnp.asarray(qd[:, :, None] * ones, F32),
            jnp.asarray(kd[:, :, None] * ones, F32), jnp.asarray(cd[:, None, None] * ones, F32))


def _rope(t, cos, sin):
    t1, t2 = t[:, :ROPE_HALF], t[:, ROPE_HALF:]
    return jnp.concatenate([t1 * cos - t2 * sin, t2 * cos + t1 * sin], axis=1)


def _rope_t(d, cos, sin):
    d1, d2 = d[:, :ROPE_HALF], d[:, ROPE_HALF:]
    return jnp.concatenate([d1 * cos + d2 * sin, d2 * cos - d1 * sin], axis=1)


def _ret_head_fwd(q, k, v, cos, sin, intra, qd, st):
    qr = _rope(q, cos, sin)
    kr = _rope(k, cos, sin) * (RET_DK ** -0.5)
    sc = _dot_nt(_bf(qr), _bf(kr)) * intra
    y = _dot(_bf(sc), _bf(v)) + _dot(_bf(qr), _bf(st)) * qd
    return qr, kr, sc, y


def _ret_specs(nc, rev):
    cidx = (lambda c: nc - 1 - c) if rev else (lambda c: c)
    return dict(
        proj=pl.BlockSpec((CHUNK, 4 * RET_W), lambda c: (cidx(c), 0)),
        half=pl.BlockSpec((CHUNK, ROPE_HALF), lambda c: (cidx(c), 0)),
        wide=pl.BlockSpec((CHUNK, RET_W), lambda c: (cidx(c), 0)),
        state=pl.BlockSpec((1, RET_HEADS, RET_DK, RET_DK), lambda c: (cidx(c), 0, 0, 0)),
        intra=_full_spec((RET_HEADS, CHUNK, CHUNK)),
        dec=_full_spec((RET_HEADS, CHUNK, RET_DK)),
        cd=_full_spec((RET_HEADS, 1, RET_DK)),
    )


def _retention_fwd(p_r, cos, sin, name):
    s = p_r.shape[0]
    nc = s // CHUNK
    intra_t, qd_t, kd_t, cd_t = _retention_tables()

    def body(p_ref, cos_ref, sin_ref, intra_ref, qd_ref, kd_ref, cd_ref, y_ref, st_ref, state):
        @pl.when(pl.program_id(0) == 0)
        def _():
            state[...] = jnp.zeros_like(state)

        cos, sin = cos_ref[...], sin_ref[...]
        for h in range(RET_HEADS):
            lo = h * RET_DK
            q = p_ref[:, lo:lo + RET_DK]
            k = p_ref[:, RET_W + lo:RET_W + lo + RET_DK]
            v = p_ref[:, 2 * RET_W + lo:2 * RET_W + lo + RET_DK]
            g = p_ref[:, 3 * RET_W + lo:3 * RET_W + lo + RET_DK]
            st = state[h]
            st_ref[0, h] = st
            _, kr, _, y = _ret_head_fwd(q, k, v, cos, sin, intra_ref[h], qd_ref[h], st)
            state[h] = st * cd_ref[h] + _dot_tn(_bf(kr * kd_ref[h]), _bf(v))
            mu = jnp.mean(y, axis=1, keepdims=True)
            yc = y - mu
            rstd = lax.rsqrt(jnp.mean(yc * yc, axis=1, keepdims=True) + EPS)
            y_ref[:, lo:lo + RET_DK] = _bf(yc * rstd * (g * _sigmoid(g)))

    sp = _ret_specs(nc, False)
    return pl.pallas_call(
        body,
        out_shape=(jax.ShapeDtypeStruct((s, RET_W), BF16),
                   jax.ShapeDtypeStruct((nc, RET_HEADS, RET_DK, RET_DK), F32)),
        grid=(nc,),
        in_specs=[sp["proj"], sp["half"], sp["half"], sp["intra"], sp["dec"], sp["dec"], sp["cd"]],
        out_specs=(sp["wide"], sp["state"]),
        scratch_shapes=[pltpu.VMEM((RET_HEADS, RET_DK, RET_DK), F32)],
        compiler_params=_params("arbitrary"), name=name)(p_r, cos, sin, intra_t, qd_t, kd_t, cd_t)


def _retention_bwd(p_r, cos, sin, states, dy_r, name):
    s = p_r.shape[0]
    nc = s // CHUNK
    intra_t, qd_t, kd_t, cd_t = _retention_tables()

    def body(p_ref, cos_ref, sin_ref, intra_ref, qd_ref, kd_ref, cd_ref, st_ref, dy_ref, dp_ref, dstate):
        @pl.when(pl.program_id(0) == 0)
        def _():
            dstate[...] = jnp.zeros_like(dstate)

        cos, sin = cos_ref[...], sin_ref[...]
        for h in range(RET_HEADS):
            lo = h * RET_DK
            q = p_ref[:, lo:lo + RET_DK]
            k = p_ref[:, RET_W + lo:RET_W + lo + RET_DK]
            v = p_ref[:, 2 * RET_W + lo:2 * RET_W + lo + RET_DK]
            g = p_ref[:, 3 * RET_W + lo:3 * RET_W + lo + RET_DK]
            st = st_ref[0, h]
            intra, qd, kd = intra_ref[h], qd_ref[h], kd_ref[h]
            qr, kr, sc, y = _ret_head_fwd(q, k, v, cos, sin, intra, qd, st)
            mu = jnp.mean(y, axis=1, keepdims=True)
            yc = y - mu
            rstd = lax.rsqrt(jnp.mean(yc * yc, axis=1, keepdims=True) + EPS)
            yn = yc * rstd
            sg = _sigmoid(g)
            dyr = dy_ref[:, lo:lo + RET_DK]
            dyn = dyr * (g * sg)
            dg = dyr * yn * (sg * (1.0 + g * (1.0 - sg)))
            dy = rstd * (dyn - jnp.mean(dyn, axis=1, keepdims=True)
                         - yn * jnp.mean(dyn * yn, axis=1, keepdims=True))
            dyb, vb, qrb = _bf(dy), _bf(v), _bf(qr)
            dsn = dstate[h]
            dsnb = _bf(dsn)
            ds = _bf(_dot_nt(dyb, vb) * intra)
            t = _bf(dy * qd)
            dv = _dot_tn(_bf(sc), dyb) + _dot(_bf(kr * kd), dsnb)
            dqr = _dot(ds, _bf(kr)) + _dot_nt(t, _bf(st))
            dkr = _dot_tn(ds, qrb) + _dot_nt(vb, dsnb) * kd
            dstate[h] = dsn * cd_ref[h] + _dot_tn(qrb, t)
            dp_ref[:, lo:lo + RET_DK] = _bf(_rope_t(dqr, cos, sin))
            dp_ref[:, RET_W + lo:RET_W + lo + RET_DK] = _bf(_rope_t(dkr, cos, sin) * (RET_DK ** -0.5))
            dp_ref[:, 2 * RET_W + lo:2 * RET_W + lo + RET_DK] = _bf(dv)
            dp_ref[:, 3 * RET_W + lo:3 * RET_W + lo + RET_DK] = _bf(dg)

    sp = _ret_specs(nc, True)
    return pl.pallas_call(
        body,
        out_shape=jax.ShapeDtypeStruct((s, 4 * RET_W), BF16),
        grid=(nc,),
        in_specs=[sp["proj"], sp["half"], sp["half"], sp["intra"], sp["dec"], sp["dec"], sp["cd"],
                  sp["state"], sp["wide"]],
        out_specs=sp["proj"],
        scratch_shapes=[pltpu.VMEM((RET_HEADS, RET_DK, RET_DK), F32)],
        compiler_params=_params("arbitrary"), name=name)(p_r, cos, sin, intra_t, qd_t, kd_t, cd_t, states, dy_r)


def _conv_taps(ext, w):
    acc = w[SSD_CONV - 1:SSD_CONV] * ext
    for j in range(SSD_CONV - 1):
        acc = acc + w[j:j + 1] * pltpu.roll(ext, SSD_CONV - 1 - j, axis=0)
    return acc


def _conv_fwd(xbc_raw, conv_w, conv_b, name):
    s = xbc_raw.shape[0]
    t8 = CONV_TILE // 8

    def body(cur_ref, prev_ref, w_ref, b_ref, o_ref):
        keep = (pl.program_id(0) > 0).astype(F32)
        ext = jnp.concatenate([prev_ref[...] * keep, cur_ref[...]], axis=0)
        u = _conv_taps(ext, w_ref[...])[8:] + b_ref[...]
        o_ref[...] = u * _sigmoid(u)

    return pl.pallas_call(
        body, out_shape=jax.ShapeDtypeStruct((s, SSD_CD), F32), grid=(s // CONV_TILE,),
        in_specs=[_row_spec(SSD_CD, CONV_TILE),
                  pl.BlockSpec((8, SSD_CD), lambda i: (jnp.maximum(i * t8 - 1, 0), 0)),
                  _full_spec((SSD_CONV, SSD_CD)), _full_spec((1, SSD_CD))],
        out_specs=_row_spec(SSD_CD, CONV_TILE),
        compiler_params=_params("parallel"), name=name)(xbc_raw, xbc_raw, conv_w, conv_b)


def _conv_bwd(xbc_raw, conv_w, conv_b, dact, name):
    s = xbc_raw.shape[0]
    nt = s // CONV_TILE
    t8 = CONV_TILE // 8
    rows = CONV_TILE + 8

    def body(cur_ref, prev_ref, next_ref, w_ref, b_ref, d_ref, dnext_ref, dx_ref, dw_ref, db_ref):
        i = pl.program_id(0)

        @pl.when(i == 0)
        def _():
            dw_ref[...] = jnp.zeros_like(dw_ref)
            db_ref[...] = jnp.zeros_like(db_ref)

        keep_prev = (i > 0).astype(F32)
        keep_next = (i < nt - 1).astype(F32)
        w = w_ref[...]
        ext = jnp.concatenate([prev_ref[...] * keep_prev, cur_ref[...], next_ref[...]], axis=0)
        shifted = [pltpu.roll(ext, SSD_CONV - 1 - j, axis=0)[8:] for j in range(SSD_CONV - 1)] + [ext[8:]]
        u = b_ref[...]
        for j in range(SSD_CONV):
            u = u + w[j:j + 1] * shifted[j]
        sg = _sigmoid(u)
        dact = jnp.concatenate([d_ref[...], dnext_ref[...] * keep_next], axis=0)
        du = dact * (sg * (1.0 + u * (1.0 - sg)))
        dx = w[SSD_CONV - 1:SSD_CONV] * du
        for j in range(SSD_CONV - 1):
            dx = dx + w[j:j + 1] * pltpu.roll(du, rows - (SSD_CONV - 1 - j), axis=0)
        dx_ref[...] = _bf(dx[:CONV_TILE])
        duc = du[:CONV_TILE]
        for j in range(SSD_CONV):
            dw_ref[j:j + 1, :] += jnp.sum(duc * shifted[j][:CONV_TILE], axis=0, keepdims=True)
        db_ref[...] += jnp.sum(duc, axis=0, keepdims=True)

    row = _row_spec(SSD_CD, CONV_TILE)
    prev = pl.BlockSpec((8, SSD_CD), lambda i: (jnp.maximum(i * t8 - 1, 0), 0))
    nxt = pl.BlockSpec((8, SSD_CD), lambda i: (jnp.minimum((i + 1) * t8, s // 8 - 1), 0))
    return pl.pallas_call(
        body,
        out_shape=(jax.ShapeDtypeStruct((s, SSD_CD), BF16), jax.ShapeDtypeStruct((SSD_CONV, SSD_CD), F32),
                   jax.ShapeDtypeStruct((1, SSD_CD), F32)),
        grid=(nt,),
        in_specs=[row, prev, nxt, _full_spec((SSD_CONV, SSD_CD)), _full_spec((1, SSD_CD)), row, nxt],
        out_specs=(row, _full_spec((SSD_CONV, SSD_CD)), _full_spec((1, SSD_CD))),
        compiler_params=_params("arbitrary"), name=name)(xbc_raw, xbc_raw, xbc_raw, conv_w, conv_b, dact, dact)


def _head_select():
    e = np.zeros((HEAD_PAD, SSD_W), np.float32)
    for h in range(SSD_HEADS):
        e[h, h * SSD_P:(h + 1) * SSD_P] = 1.0
    return jnp.asarray(e, BF16), jnp.asarray(e.T, BF16)


def _pad_heads(v):
    return jnp.pad(v.reshape(1, SSD_HEADS).astype(F32), ((0, 0), (0, HEAD_PAD - SSD_HEADS)))


def _ssd_masks():
    r = _iota((CHUNK, SSD_QW), 0)
    c = _iota((CHUNK, SSD_QW), 1) % SSD_P
    itile = (r == c).astype(F32)
    ctile = r >= c
    rb = _iota((SSD_QW, SSD_QW), 0) // SSD_P
    cb = _iota((SSD_QW, SSD_QW), 1) // SSD_P
    return itile, ctile, rb == cb


def _ssd_prep(xbc_ref, dtr_ref, dtb_ref, alog_ref, e_ref):
    xs = xbc_ref[:, :SSD_W]
    u = dtr_ref[...] + dtb_ref[...]
    dt = _softplus(u)
    nexp = -jnp.exp(alog_ref[...])
    a = dt * nexp
    tril = _bf((_iota((CHUNK, CHUNK), 0) >= _iota((CHUNK, CHUNK), 1)).astype(F32))
    acum = _dot_exact_r(tril, a)
    e = e_ref[...]
    acol = _dot_exact_l(acum, e)
    dte = _dot_exact_l(dt, e)
    alast = acol[CHUNK - 1:CHUNK, :]
    return dict(xs=xs, u=u, dt=dt, nexp=nexp, a=a, acol=acol, dte=dte, xdt=xs * dte,
                ea=jnp.exp(acol), tail=jnp.exp(alast - acol), eal=jnp.exp(alast))


def _tile4(x):
    return jnp.concatenate([x, x, x, x], axis=0)


def _fold4(x):
    return x[0:CHUNK] + x[CHUNK:2 * CHUNK] + x[2 * CHUNK:3 * CHUNK] + x[3 * CHUNK:4 * CHUNK]


def _ssd_specs(nc, rev):
    cidx = (lambda c: nc - 1 - c) if rev else (lambda c: c)
    return dict(
        xbc=pl.BlockSpec((CHUNK, SSD_CD), lambda c: (cidx(c), 0)),
        dt=pl.BlockSpec((CHUNK, HEAD_PAD), lambda c: (cidx(c), 0)),
        wide=pl.BlockSpec((CHUNK, SSD_W), lambda c: (cidx(c), 0)),
        state=pl.BlockSpec((1, SSD_N, SSD_W), lambda c: (cidx(c), 0, 0)),
        head=_full_spec((1, HEAD_PAD)),
        roww=_full_spec((1, SSD_W)),
        e=_full_spec((HEAD_PAD, SSD_W)),
        et=_full_spec((SSD_W, HEAD_PAD)),
    )


def _ssd_fwd(xbc, dt_raw, z, dt_bias, a_log, d_skip, norm_w, name):
    s = xbc.shape[0]
    nc = s // CHUNK
    e_sel, _ = _head_select()

    def body(xbc_ref, dtr_ref, z_ref, dtb_ref, alog_ref, dsk_ref, nw_ref, e_ref,
             yraw_ref, ys_ref, st_ref, state):
        @pl.when(pl.program_id(0) == 0)
        def _():
            state[...] = jnp.zeros_like(state)

        p = _ssd_prep(xbc_ref, dtr_ref, dtb_ref, alog_ref, e_ref)
        itile, ctile, bdmask = _ssd_masks()
        dske = _dot_exact_l(jnp.broadcast_to(dsk_ref[...], (8, HEAD_PAD)), e_ref[...])[0:1]
        st_ref[0] = state[...]
        for g in range(SSD_GROUPS):
            gs = slice(g * SSD_GW, (g + 1) * SSD_GW)
            bg = _bf(xbc_ref[:, B_OFF + g * SSD_N:B_OFF + (g + 1) * SSD_N])
            cg = _bf(xbc_ref[:, C_OFF + g * SSD_N:C_OFF + (g + 1) * SSD_N])
            cbt = _dot_nt(cg, _tile4(bg))
            stg = state[:, gs]
            ys = _dot(cg, _bf(stg))
            for qd in range(SSD_GW // SSD_QW):
                ql = slice(g * SSD_GW + qd * SSD_QW, g * SSD_GW + (qd + 1) * SSD_QW)
                aq = p["acol"][:, ql]
                arow = jnp.sum(aq * itile, axis=0, keepdims=True)
                dq = jnp.exp(jnp.where(ctile, aq - arow, -jnp.inf))
                xbd = jnp.where(bdmask, _tile4(_bf(p["xdt"][:, ql])), jnp.zeros((), BF16))
                yq = _dot(_bf(cbt * dq), xbd) + ys[:, qd * SSD_QW:(qd + 1) * SSD_QW] * p["ea"][:, ql]
                yraw_ref[:, ql] = yq + dske[:, ql] * p["xs"][:, ql]
            wt = _bf(p["xdt"][:, gs] * p["tail"][:, gs])
            state[:, gs] = stg * p["eal"][:, gs] + _dot_tn(bg, wt)
        zv = z_ref[...]
        t = yraw_ref[...] * (zv * _sigmoid(zv))
        rstd = lax.rsqrt(jnp.mean(t * t, axis=1, keepdims=True) + EPS)
        ys_ref[...] = _bf(t * rstd * nw_ref[...])

    sp = _ssd_specs(nc, False)
    return pl.pallas_call(
        body,
        out_shape=(jax.ShapeDtypeStruct((s, SSD_W), F32), jax.ShapeDtypeStruct((s, SSD_W), BF16),
                   jax.ShapeDtypeStruct((nc, SSD_N, SSD_W), F32)),
        grid=(nc,),
        in_specs=[sp["xbc"], sp["dt"], sp["wide"], sp["head"], sp["head"], sp["head"], sp["roww"], sp["e"]],
        out_specs=(sp["wide"], sp["wide"], sp["state"]),
        scratch_shapes=[pltpu.VMEM((SSD_N, SSD_W), F32)],
        compiler_params=_params("arbitrary"), name=name,
    )(xbc, dt_raw, z, dt_bias, a_log, d_skip, norm_w, e_sel)


def _ssd_bwd(xbc, dt_raw, z, yraw, states, dys, dt_bias, a_log, d_skip, norm_w, name):
    s = xbc.shape[0]
    nc = s // CHUNK
    e_sel, et_sel = _head_select()

    def body(xbc_ref, dtr_ref, z_ref, yraw_ref, st_ref, dys_ref, dtb_ref, alog_ref, dsk_ref, nw_ref,
             e_ref, et_ref, dxbc_ref, dz_ref, ddtr_ref, ddtb_ref, dalog_ref, ddsk_ref, dnw_ref,
             dstate, dacol_s, ddte_s):
        @pl.when(pl.program_id(0) == 0)
        def _():
            dstate[...] = jnp.zeros_like(dstate)
            ddtb_ref[...] = jnp.zeros_like(ddtb_ref)
            dalog_ref[...] = jnp.zeros_like(dalog_ref)
            ddsk_ref[...] = jnp.zeros_like(ddsk_ref)
            dnw_ref[...] = jnp.zeros_like(dnw_ref)

        p = _ssd_prep(xbc_ref, dtr_ref, dtb_ref, alog_ref, e_ref)
        itile, ctile, bdmask = _ssd_masks()
        et = et_ref[...]
        dske = _dot_exact_l(jnp.broadcast_to(dsk_ref[...], (8, HEAD_PAD)), e_ref[...])[0:1]
        last_row = (_iota((CHUNK, 1), 0) == CHUNK - 1).astype(F32)

        zv, y, dysv = z_ref[...], yraw_ref[...], dys_ref[...]
        sz = _sigmoid(zv)
        silz = zv * sz
        t = y * silz
        rstd = lax.rsqrt(jnp.mean(t * t, axis=1, keepdims=True) + EPS)
        that = t * rstd
        dthat = dysv * nw_ref[...]
        dt_ = rstd * (dthat - that * jnp.mean(dthat * that, axis=1, keepdims=True))
        dnw_ref[...] += jnp.sum(dysv * that, axis=0, keepdims=True)
        dz_ref[...] = _bf(dt_ * y * (sz * (1.0 + zv * (1.0 - sz))))
        dy = dt_ * silz
        dsk_row = jnp.sum(dy * p["xs"], axis=0, keepdims=True)
        ddsk_ref[...] += _dot_exact_l(jnp.broadcast_to(dsk_row, (8, SSD_W)), et)[0:1]

        for g in range(SSD_GROUPS):
            gs = slice(g * SSD_GW, (g + 1) * SSD_GW)
            bg = _bf(xbc_ref[:, B_OFF + g * SSD_N:B_OFF + (g + 1) * SSD_N])
            cg = _bf(xbc_ref[:, C_OFF + g * SSD_N:C_OFF + (g + 1) * SSD_N])
            btile = _tile4(bg)
            cbt = _dot_nt(cg, btile)
            stg = st_ref[0, :, gs]
            stgb = _bf(stg)
            dyg = dy[:, gs]
            eag, tailg, xdtg, ealg = p["ea"][:, gs], p["tail"][:, gs], p["xdt"][:, gs], p["eal"][:, gs]
            dys_g = _bf(dyg * eag)
            dacol_g = dyg * (_dot(cg, stgb) * eag)
            dc = _dot_nt(dys_g, stgb)
            dst = _dot_tn(cg, dys_g)
            dsn = dstate[:, gs]
            dsnb = _bf(dsn)
            dwt = _dot(bg, dsnb)
            db = _dot_nt(_bf(xdtg * tailg), dsnb)
            dtl = dwt * xdtg * tailg
            dacol_g = dacol_g - dtl
            dal_row = jnp.sum(dtl, axis=0, keepdims=True) + jnp.sum(dsn * stg, axis=0, keepdims=True) * ealg
            dstate[:, gs] = dst + dsn * ealg
            dxdt_q = []
            dacol_q = []
            for qd in range(SSD_GW // SSD_QW):
                ql = slice(g * SSD_GW + qd * SSD_QW, g * SSD_GW + (qd + 1) * SSD_QW)
                aq = p["acol"][:, ql]
                arow = jnp.sum(aq * itile, axis=0, keepdims=True)
                dq = jnp.exp(jnp.where(ctile, aq - arow, -jnp.inf))
                mq = cbt * dq
                xbd = jnp.where(bdmask, _tile4(_bf(p["xdt"][:, ql])), jnp.zeros((), BF16))
                dyq = _bf(dy[:, ql])
                dm = _dot_nt(dyq, xbd)
                dxbd = jnp.where(bdmask, _dot_tn(_bf(mq), dyq), 0.0)
                dxdt_q.append(_fold4(dxbd))
                eq = dm * mq
                dacol_q.append(eq - itile * jnp.sum(eq, axis=0, keepdims=True))
                dcbt = _bf(dm * dq)
                dc = dc + _dot(dcbt, btile)
                db = db + _fold4(_dot_tn(dcbt, cg))
            dxdt_g = dwt * tailg + jnp.concatenate(dxdt_q, axis=1)
            dacol_s[:, gs] = dacol_g + jnp.concatenate(dacol_q, axis=1) + last_row * dal_row
            ddte_s[:, gs] = dxdt_g * p["xs"][:, gs]
            dxbc_ref[:, gs] = dyg * dske[:, gs] + dxdt_g * p["dte"][:, gs]
            dxbc_ref[:, B_OFF + g * SSD_N:B_OFF + (g + 1) * SSD_N] = db
            dxbc_ref[:, C_OFF + g * SSD_N:C_OFF + (g + 1) * SSD_N] = dc

        dacum = _dot_exact_l(dacol_s[...], et)
        triu = _bf((_iota((CHUNK, CHUNK), 1) >= _iota((CHUNK, CHUNK), 0)).astype(F32))
        da = _dot_exact_r(triu, dacum)
        ddt = _dot_exact_l(ddte_s[...], et) + da * p["nexp"]
        dalog_ref[...] += jnp.sum(da * p["a"], axis=0, keepdims=True)
        du = ddt * _sigmoid(p["u"])
        ddtr_ref[...] = _bf(du)
        ddtb_ref[...] += jnp.sum(du, axis=0, keepdims=True)

    sp = _ssd_specs(nc, True)
    return pl.pallas_call(
        body,
        out_shape=(jax.ShapeDtypeStruct((s, SSD_CD), F32), jax.ShapeDtypeStruct((s, SSD_W), BF16),
                   jax.ShapeDtypeStruct((s, HEAD_PAD), BF16), jax.ShapeDtypeStruct((1, HEAD_PAD), F32),
                   jax.ShapeDtypeStruct((1, HEAD_PAD), F32), jax.ShapeDtypeStruct((1, HEAD_PAD), F32),
                   jax.ShapeDtypeStruct((1, SSD_W), F32)),
        grid=(nc,),
        in_specs=[sp["xbc"], sp["dt"], sp["wide"], sp["wide"], sp["state"], sp["wide"],
                  sp["head"], sp["head"], sp["head"], sp["roww"], sp["e"], sp["et"]],
        out_specs=(sp["xbc"], sp["wide"], sp["dt"], sp["head"], sp["head"], sp["head"], sp["roww"]),
        scratch_shapes=[pltpu.VMEM((SSD_N, SSD_W), F32), pltpu.VMEM((CHUNK, SSD_W), F32),
                        pltpu.VMEM((CHUNK, SSD_W), F32)],
        compiler_params=_params("arbitrary"), name=name,
    )(xbc, dt_raw, z, yraw, states, dys, dt_bias, a_log, d_skip, norm_w, e_sel, et_sel)


def _local_step(x, pos_col, target, norm1_w, conv_w, conv_b, dt_bias, a_log, d_skip, ssd_norm_w, norm_f_w,
                w_r, w_z, w_xbc, w_dt, w_g, w_br_ret, w_br_ssd, w_out):
    inv_freq = jnp.asarray(ROPE_THETA ** (-np.arange(ROPE_HALF, dtype=np.float64) / ROPE_HALF), F32).reshape(1, ROPE_HALF)
    dtb, alog, dsk = _pad_heads(dt_bias), _pad_heads(a_log), _pad_heads(d_skip)

    h = _rmsnorm_fwd(x, norm1_w, "rmsnorm1_fwd")
    p_r = _matmul(h, w_r, "nn", F32, "proj_ret")
    p_z = _matmul(h, w_z, "nn", F32, "proj_z")
    p_xbc = _matmul(h, w_xbc, "nn", F32, "proj_xbc")
    p_dt = _matmul(h, w_dt, "nn", F32, "proj_dt")
    p_g = _matmul(h, w_g, "nn", F32, "proj_gates")
    cos, sin = _rope_tables(pos_col, inv_freq, "rope_tables")
    y_r, ret_states = _retention_fwd(p_r, cos, sin, "retention_fwd")
    xbc_act = _conv_fwd(p_xbc, conv_w, conv_b, "conv_fwd")
    y_raw, y_s, ssd_states = _ssd_fwd(xbc_act, p_dt, p_z, dtb, alog, dsk, ssd_norm_w, "ssd_fwd")
    pr = _matmul(y_r, w_br_ret, "nn", F32, "branch_ret")
    ps = _matmul(y_s, w_br_ssd, "nn", F32, "branch_ssd")
    merged = _merge_fwd(p_g, pr, ps, "merge_fwd")
    o = _matmul(merged, w_out, "nn", F32, "out_proj")
    dx2, g_norm_f, loss_acc = _final_fwd_bwd(x, o, norm_f_w, target, "final_norm_loss")

    g_w_out = _matmul(merged, dx2, "tn", BF16, "grad_w_out")
    dmerged = _matmul(dx2, w_out, "nt", F32, "d_merged")
    dpr, dps, dp_g = _merge_bwd(p_g, pr, ps, dmerged, "merge_bwd")
    g_w_br_ret = _matmul(y_r, dpr, "tn", BF16, "grad_w_br_ret")
    g_w_br_ssd = _matmul(y_s, dps, "tn", BF16, "grad_w_br_ssd")
    dy_r = _matmul(dpr, w_br_ret, "nt", F32, "d_y_ret")
    dy_s = _matmul(dps, w_br_ssd, "nt", F32, "d_y_ssd")
    dxbc_act, dp_z, dp_dt, g_dtb, g_alog, g_dsk, g_ssd_norm = _ssd_bwd(
        xbc_act, p_dt, p_z, y_raw, ssd_states, dy_s, dtb, alog, dsk, ssd_norm_w, "ssd_bwd")
    dp_xbc, g_conv_w, g_conv_b = _conv_bwd(p_xbc, conv_w, conv_b, dxbc_act, "conv_bwd")
    dp_r = _retention_bwd(p_r, cos, sin, ret_states, dy_r, "retention_bwd")
    dh = _matmul(dp_r, w_r, "nt", F32, "d_h_ret")
    dh = _matmul(dp_z, w_z, "nt", F32, "d_h_z", add=dh)
    dh = _matmul(dp_xbc, w_xbc, "nt", F32, "d_h_xbc", add=dh)
    dh = _matmul(dp_dt, w_dt, "nt", F32, "d_h_dt", add=dh)
    dh = _matmul(dp_g, w_g, "nt", F32, "d_h_gates", add=dh)
    g_w_in = jnp.concatenate([
        _matmul(h, dp_r, "tn", BF16, "grad_w_ret"),
        _matmul(h, dp_z, "tn", BF16, "grad_w_z"),
        _matmul(h, dp_xbc, "tn", BF16, "grad_w_xbc"),
        _matmul(h, dp_dt, "tn", BF16, "grad_w_dt")[:, :SSD_HEADS],
        _matmul(h, dp_g, "tn", BF16, "grad_w_gates"),
    ], axis=1)
    grad_x, g_norm1 = _rmsnorm_bwd(x, norm1_w, dh, dx2, "rmsnorm1_bwd")
    small = dict(norm1_w=g_norm1, conv_w=g_conv_w, conv_b=g_conv_b, dt_bias=g_dtb[:, :SSD_HEADS],
                 a_log=g_alog[:, :SSD_HEADS], d_skip=g_dsk[:, :SSD_HEADS], ssd_norm_w=g_ssd_norm,
                 norm_f_w=g_norm_f)
    big = dict(w_in=g_w_in, w_br_ret=g_w_br_ret, w_br_ssd=g_w_br_ssd, w_out=g_w_out)
    return loss_acc[0, 0], grad_x, small, big


def _mesh_pos():
    return lax.axis_index("x"), lax.axis_index("y"), lax.axis_index("c")


def _all_gather(shard, name):
    r, c = shard.shape

    def body(x_ref, out_ref, send_sems, recv_sems, local_sem):
        x, y, cc = _mesh_pos()
        me, sibling = (x, y, cc), (x, y, 1 - cc)
        chips = [(1 - x, y), (x, 1 - y), (1 - x, 1 - y)]

        def slot(px, py, pc):
            return out_ref.at[4 * px + 2 * py + pc]

        def copy(k, block, to, src=None):
            return pltpu.make_async_remote_copy(
                src_ref=slot(*block) if src is None else src, dst_ref=slot(*block),
                send_sem=send_sems.at[k], recv_sem=recv_sems.at[k], device_id=to, device_id_type=MESH)

        mine = pltpu.make_async_copy(x_ref, slot(*me), local_sem)
        mine.start()
        first = [copy(0, me, sibling, src=x_ref)]
        first += [copy(1 + j, me, (*chip, cc), src=x_ref) for j, chip in enumerate(chips)]
        for cp in first:
            cp.start()
        passed = [copy(4 + j, (*chip, cc), sibling) for j, chip in enumerate(chips)]
        for j, chip in enumerate(chips):
            copy(1 + j, (*chip, cc), me).wait_recv()
            passed[j].start()
        copy(0, sibling, me).wait_recv()
        for j, chip in enumerate(chips):
            copy(4 + j, (*chip, 1 - cc), me).wait_recv()
        for cp in first + passed:
            cp.wait_send()
        mine.wait()

    return pl.pallas_call(
        body,
        out_shape=jax.ShapeDtypeStruct((N_DEV, r, c), shard.dtype),
        in_specs=[pl.BlockSpec(memory_space=pl.ANY)],
        out_specs=pl.BlockSpec(memory_space=pl.ANY),
        scratch_shapes=[pltpu.SemaphoreType.DMA((7,)), pltpu.SemaphoreType.DMA((7,)), pltpu.SemaphoreType.DMA],
        name=name)(shard)


def _scatter_blocks(blocks, name):
    _, r, c = blocks.shape

    def body(x_ref, out_ref, send_sems, recv_sems, local_sem):
        x, y, cc = _mesh_pos()
        my_idx = 4 * x + 2 * y + cc
        mine = pltpu.make_async_copy(x_ref.at[my_idx], out_ref.at[my_idx], local_sem)
        mine.start()
        copies = []
        for k in range(1, N_DEV):
            px, py, pc = x ^ (k >> 2), y ^ ((k >> 1) & 1), cc ^ (k & 1)
            cp = pltpu.make_async_remote_copy(
                src_ref=x_ref.at[4 * px + 2 * py + pc], dst_ref=out_ref.at[my_idx],
                send_sem=send_sems.at[k - 1], recv_sem=recv_sems.at[k - 1],
                device_id=(px, py, pc), device_id_type=MESH)
            cp.start()
            copies.append(cp)
        for k in range(1, N_DEV):
            px, py, pc = x ^ (k >> 2), y ^ ((k >> 1) & 1), cc ^ (k & 1)
            pltpu.make_async_remote_copy(
                src_ref=x_ref.at[my_idx], dst_ref=out_ref.at[4 * px + 2 * py + pc],
                send_sem=send_sems.at[k - 1], recv_sem=recv_sems.at[k - 1],
                device_id=(px, py, pc), device_id_type=MESH).wait_recv()
        for cp in copies:
            cp.wait_send()
        mine.wait()

    return pl.pallas_call(
        body,
        out_shape=jax.ShapeDtypeStruct(blocks.shape, blocks.dtype),
        in_specs=[pl.BlockSpec(memory_space=pl.ANY)],
        out_specs=pl.BlockSpec(memory_space=pl.ANY),
        scratch_shapes=[pltpu.SemaphoreType.DMA((7,)), pltpu.SemaphoreType.DMA((7,)), pltpu.SemaphoreType.DMA],
        name=name)(blocks)


def _all_reduce_small(vec, name):
    r, c = vec.shape

    def body(x_ref, out_ref, land, send_sems, recv_sems):
        x, y, cc = _mesh_pos()
        my_idx = 4 * x + 2 * y + cc
        land[my_idx] = x_ref[...]
        copies = []
        for k in range(1, N_DEV):
            px, py, pc = x ^ (k >> 2), y ^ ((k >> 1) & 1), cc ^ (k & 1)
            cp = pltpu.make_async_remote_copy(
                src_ref=x_ref, dst_ref=land.at[my_idx],
                send_sem=send_sems.at[k - 1], recv_sem=recv_sems.at[k - 1],
                device_id=(px, py, pc), device_id_type=MESH)
            cp.start()
            copies.append(cp)
        for k in range(1, N_DEV):
            px, py, pc = x ^ (k >> 2), y ^ ((k >> 1) & 1), cc ^ (k & 1)
            pltpu.make_async_remote_copy(
                src_ref=x_ref, dst_ref=land.at[4 * px + 2 * py + pc],
                send_sem=send_sems.at[k - 1], recv_sem=recv_sems.at[k - 1],
                device_id=(px, py, pc), device_id_type=MESH).wait_recv()
        for cp in copies:
            cp.wait_send()
        acc = land[0]
        for i in range(1, N_DEV):
            acc = acc + land[i]
        out_ref[...] = acc

    return pl.pallas_call(
        body,
        out_shape=jax.ShapeDtypeStruct((r, c), F32),
        in_specs=[pl.BlockSpec(memory_space=pltpu.VMEM)],
        out_specs=pl.BlockSpec(memory_space=pltpu.VMEM),
        scratch_shapes=[pltpu.VMEM((N_DEV, r, c), F32), pltpu.SemaphoreType.DMA((7,)),
                        pltpu.SemaphoreType.DMA((7,))],
        name=name)(vec)


def _reduce_scatter(full_grad, axis, name):
    r, c = full_grad.shape
    if axis == 0:
        blocks = full_grad.reshape(N_DEV, r // N_DEV, c)
    else:
        blocks = full_grad.reshape(r, N_DEV, c // N_DEV).transpose(1, 0, 2)
    return _sum_slots(_scatter_blocks(blocks, name + "_exchange"), name + "_sum")


_SMALL = ("norm1_w", "conv_w", "conv_b", "dt_bias", "a_log", "d_skip", "ssd_norm_w", "norm_f_w")
_SMALL_COLS = 128
_WEIGHTS = ("norm1_w", "w_in", "conv_w", "conv_b", "dt_bias", "a_log", "d_skip", "ssd_norm_w",
            "w_br_ret", "w_br_ssd", "w_out", "norm_f_w")


def _pack(parts):
    flat = jnp.concatenate([p.reshape(-1).astype(F32) for p in parts])
    rows = -(-flat.shape[0] // (8 * _SMALL_COLS)) * 8
    return jnp.pad(flat, (0, rows * _SMALL_COLS - flat.shape[0])).reshape(rows, _SMALL_COLS)


def _unpack(packed, shapes):
    flat = packed.reshape(-1)
    out, off = [], 0
    for shp in shapes:
        n = int(np.prod(shp))
        out.append(flat[off:off + n].reshape(shp))
        off += n
    return out


def kernel(x, positions, norm1_w, w_in, conv_w, conv_b, dt_bias, a_log, d_skip, ssd_norm_w, w_br_ret, w_br_ssd, w_out, norm_f_w, loss_target, m_norm1_w, m_w_in, m_conv_w, m_conv_b, m_dt_bias, m_a_log, m_d_skip, m_ssd_norm_w, m_w_br_ret, m_w_br_ssd, m_w_out, m_norm_f_w, v_norm1_w, v_w_in, v_conv_w, v_conv_b, v_dt_bias, v_a_log, v_d_skip, v_ssd_norm_w, v_w_br_ret, v_w_br_ssd, v_w_out, v_norm_f_w):
    w = dict(norm1_w=norm1_w, w_in=w_in, conv_w=conv_w, conv_b=conv_b, dt_bias=dt_bias, a_log=a_log,
             d_skip=d_skip, ssd_norm_w=ssd_norm_w, w_br_ret=w_br_ret, w_br_ssd=w_br_ssd, w_out=w_out,
             norm_f_w=norm_f_w)
    m = dict(norm1_w=m_norm1_w, w_in=m_w_in, conv_w=m_conv_w, conv_b=m_conv_b, dt_bias=m_dt_bias,
             a_log=m_a_log, d_skip=m_d_skip, ssd_norm_w=m_ssd_norm_w, w_br_ret=m_w_br_ret,
             w_br_ssd=m_w_br_ssd, w_out=m_w_out, norm_f_w=m_norm_f_w)
    v = dict(norm1_w=v_norm1_w, w_in=v_w_in, conv_w=v_conv_w, conv_b=v_conv_b, dt_bias=v_dt_bias,
             a_log=v_a_log, d_skip=v_d_skip, ssd_norm_w=v_ssd_norm_w, w_br_ret=v_w_br_ret,
             w_br_ssd=v_w_br_ssd, w_out=v_w_out, norm_f_w=v_norm_f_w)
    s = x.shape[1]
    my_idx = 4 * lax.axis_index("x") + 2 * lax.axis_index("y") + lax.axis_index("c")

    w_in_all = _all_gather(_cast_bf16(w_in[0], "cast_w_in"), "gather_w_in")
    w_in_full = w_in_all.transpose(1, 0, 2).reshape(D_MODEL, IN_PROJ)
    w_dt = jnp.pad(w_in_full[:, OFF_DT:OFF_G], ((0, 0), (0, HEAD_PAD - SSD_HEADS)))
    w_br_ret_full = _all_gather(_cast_bf16(w_br_ret[0], "cast_w_br_ret"), "gather_w_br_ret").reshape(RET_W, D_MODEL)
    w_br_ssd_full = _all_gather(_cast_bf16(w_br_ssd[0], "cast_w_br_ssd"), "gather_w_br_ssd").reshape(SSD_W, D_MODEL)
    w_out_full = _all_gather(_cast_bf16(w_out[0], "cast_w_out"), "gather_w_out").reshape(D_MODEL, D_MODEL)
    conv_w_full = _all_gather(conv_w[0], "gather_conv_w").transpose(1, 0, 2).reshape(SSD_CONV, SSD_CD)

    loss_part, grad_x, g_small, g_big = _local_step(
        x[0], positions.reshape(s, 1), loss_target[0], norm1_w, conv_w_full, conv_b, dt_bias, a_log, d_skip,
        ssd_norm_w, norm_f_w.reshape(1, D_MODEL),
        w_in_full[:, :OFF_Z], w_in_full[:, OFF_Z:OFF_XBC], w_in_full[:, OFF_XBC:OFF_DT], w_dt,
        w_in_full[:, OFF_G:], w_br_ret_full, w_br_ssd_full, w_out_full)

    loss = lax.psum(loss_part, ("x", "y", "c"))

    small_shapes = [g_small[n].shape for n in _SMALL]
    summed = _unpack(_all_reduce_small(_pack([g_small[n] for n in _SMALL]), "allreduce_small"), small_shapes)
    grads = dict(zip(_SMALL, summed))
    conv_cols = SSD_CD // N_DEV
    grads["conv_w"] = lax.dynamic_slice_in_dim(grads["conv_w"], my_idx * conv_cols, conv_cols, axis=1)
    grads["norm_f_w"] = grads["norm_f_w"].reshape(D_MODEL)
    grads["w_in"] = _reduce_scatter(g_big["w_in"], 1, "rs_w_in")
    grads["w_br_ret"] = _reduce_scatter(g_big["w_br_ret"], 0, "rs_w_br_ret")
    grads["w_br_ssd"] = _reduce_scatter(g_big["w_br_ssd"], 0, "rs_w_br_ssd")
    grads["w_out"] = _reduce_scatter(g_big["w_out"], 0, "rs_w_out")
    for n in ("norm1_w", "conv_w", "conv_b", "dt_bias", "a_log", "d_skip", "ssd_norm_w"):
        grads[n] = grads[n].reshape(w[n].shape)
    for n in ("w_in", "w_br_ret", "w_br_ssd", "w_out"):
        grads[n] = grads[n].reshape(w[n].shape)

    delta, new_m, new_v = {}, {}, {}
    for n in ("w_in", "w_br_ret", "w_br_ssd", "w_out"):
        shp = w[n].shape
        two_d = (shp[1], shp[2])
        d_, m_, v_ = _adamw(w[n].reshape(two_d), grads[n].reshape(two_d), m[n].reshape(two_d),
                            v[n].reshape(two_d), "adamw_" + n)
        delta[n], new_m[n], new_v[n] = d_.reshape(shp), m_.reshape(shp), v_.reshape(shp)
    shapes = [w[n].shape for n in _SMALL]
    packed = _adamw(_pack([w[n] for n in _SMALL]), _pack([grads[n] for n in _SMALL]),
                    _pack([m[n] for n in _SMALL]), _pack([v[n] for n in _SMALL]), "adamw_small")
    for res, dst in zip(packed, (delta, new_m, new_v)):
        for n, a in zip(_SMALL, _unpack(res, shapes)):
            dst[n] = a

    return (loss, grad_x.reshape(x.shape), *[grads[n] for n in _WEIGHTS], *[delta[n] for n in _WEIGHTS],
            *[new_m[n] for n in _WEIGHTS], *[new_v[n] for n in _WEIGHTS])
```

```python
import functools

import numpy as np
import jax
import jax.numpy as jnp
from jax import lax
from jax.experimental import pallas as pl
from jax.experimental.pallas import tpu as pltpu

F32 = jnp.float32
BF16 = jnp.bfloat16

D_MODEL = 2048
CHUNK = 64
EPS = 1e-6
N_DEV = 8

RET_HEADS = 8
RET_DK = 256
RET_W = RET_HEADS * RET_DK
ROPE_THETA = 10000.0
ROPE_HALF = RET_DK // 2

SSD_W = 4096
SSD_P = 64
SSD_HEADS = 64
SSD_GROUPS = 8
SSD_N = 128
SSD_GW = SSD_W // SSD_GROUPS
SSD_QW = 256
SSD_CONV = 4
SSD_CD = SSD_W + 2 * SSD_GROUPS * SSD_N
HEAD_PAD = 128
B_OFF = SSD_W
C_OFF = SSD_W + SSD_GROUPS * SSD_N

ADAM_LR = 0.001
ADAM_B1 = 0.9
ADAM_B2 = 0.999
ADAM_EPS = 1e-08
ADAM_WD = 0.01
ADAM_STEP = 10

SPLITS = (RET_W, RET_W, RET_W, RET_W, SSD_W, SSD_CD, SSD_HEADS, D_MODEL, D_MODEL)
IN_PROJ = sum(SPLITS)
OFF_Z = 4 * RET_W
OFF_XBC = OFF_Z + SSD_W
OFF_DT = OFF_XBC + SSD_CD
OFF_G = OFF_DT + SSD_HEADS

ROW_TILE = 256
CONV_TILE = 128
MM_TILE = 1024
MM_TK = 2048

MESH = pl.DeviceIdType.MESH


def _dot(a, b):
    return lax.dot_general(a, b, (((1,), (0,)), ((), ())), preferred_element_type=F32)


def _dot_nt(a, b):
    return lax.dot_general(a, b, (((1,), (1,)), ((), ())), preferred_element_type=F32)


def _dot_tn(a, b):
    return lax.dot_general(a, b, (((0,), (0,)), ((), ())), preferred_element_type=F32)


def _bf(x):
    return x.astype(BF16)


def _split3(x):
    hi = x.astype(BF16)
    r = x - hi.astype(F32)
    mid = r.astype(BF16)
    lo = (r - mid.astype(F32)).astype(BF16)
    return hi, mid, lo


def _dot_exact_l(x, sel):
    hi, mid, lo = _split3(x)
    return _dot(hi, sel) + _dot(mid, sel) + _dot(lo, sel)


def _dot_exact_r(sel, x):
    hi, mid, lo = _split3(x)
    return _dot(sel, hi) + _dot(sel, mid) + _dot(sel, lo)


def _sigmoid(x):
    return 1.0 / (1.0 + jnp.exp(-x))


def _softplus(x):
    return jnp.maximum(x, 0.0) + jnp.log(1.0 + jnp.exp(-jnp.abs(x)))


def _iota(shape, axis):
    return lax.broadcasted_iota(jnp.int32, shape, axis)


def _params(*sem):
    return pltpu.CompilerParams(dimension_semantics=sem)


def _matmul(a, b, mode, out_dtype, name, add=None):
    if mode == "nn":
        (m, k), (k2, n) = a.shape, b.shape
    elif mode == "nt":
        (m, k), (n, k2) = a.shape, b.shape
    else:
        (k, m), (k2, n) = a.shape, b.shape
    assert k == k2, (a.shape, b.shape, mode)
    tm, tn, tk = min(m, MM_TILE), min(n, MM_TILE), min(k, MM_TK)
    assert m % tm == 0 and n % tn == 0 and k % tk == 0, (m, n, k)
    nk = k // tk
    dot = {"nn": _dot, "nt": _dot_nt, "tn": _dot_tn}[mode]

    def body(*refs):
        a_ref, b_ref = refs[:2]
        add_ref = refs[2] if add is not None else None
        o_ref = refs[3] if add is not None else refs[2]

        def finish(r):
            if add_ref is not None:
                r = r + add_ref[...]
            o_ref[...] = r.astype(o_ref.dtype)

        if nk == 1:
            finish(dot(_bf(a_ref[...]), _bf(b_ref[...])))
            return
        acc_ref = refs[-1]
        kk = pl.program_id(2)

        @pl.when(kk == 0)
        def _():
            acc_ref[...] = dot(_bf(a_ref[...]), _bf(b_ref[...]))

        @pl.when(jnp.logical_and(kk > 0, kk < nk - 1))
        def _():
            acc_ref[...] += dot(_bf(a_ref[...]), _bf(b_ref[...]))

        @pl.when(kk == nk - 1)
        def _():
            finish(acc_ref[...] + dot(_bf(a_ref[...]), _bf(b_ref[...])))

    if mode == "nn":
        a_spec = pl.BlockSpec((tm, tk), lambda i, j, kk: (i, kk))
        b_spec = pl.BlockSpec((tk, tn), lambda i, j, kk: (kk, j))
    elif mode == "nt":
        a_spec = pl.BlockSpec((tm, tk), lambda i, j, kk: (i, kk))
        b_spec = pl.BlockSpec((tn, tk), lambda i, j, kk: (j, kk))
    else:
        a_spec = pl.BlockSpec((tk, tm), lambda i, j, kk: (kk, i))
        b_spec = pl.BlockSpec((tk, tn), lambda i, j, kk: (kk, j))
    o_spec = pl.BlockSpec((tm, tn), lambda i, j, kk: (i, j))
    in_specs = [a_spec, b_spec] + ([o_spec] if add is not None else [])
    args = (a, b) + ((add,) if add is not None else ())
    block_bytes = (tm * tk * a.dtype.itemsize + tk * tn * b.dtype.itemsize
                   + tm * tn * (jnp.dtype(out_dtype).itemsize + (4 if add is not None else 0)))
    vmem = 2 * block_bytes + 2 * tm * tn * 4 + 2 * (tm + tn) * tk + (4 << 20)
    return pl.pallas_call(
        body,
        out_shape=jax.ShapeDtypeStruct((m, n), out_dtype),
        grid=(m // tm, n // tn, nk),
        in_specs=in_specs,
        out_specs=o_spec,
        scratch_shapes=[pltpu.VMEM((tm, tn), F32)] if nk > 1 else [],
        compiler_params=pltpu.CompilerParams(
            dimension_semantics=("parallel", "parallel", "arbitrary"), vmem_limit_bytes=int(vmem)),
        name=name,
    )(*args)


def _row_spec(width, tile=ROW_TILE):
    return pl.BlockSpec((tile, width), lambda i: (i, 0))


def _full_spec(shape):
    nd = len(shape)
    return pl.BlockSpec(shape, lambda *_: (0,) * nd)


def _cast_bf16(x, name):
    r, c = x.shape
    tr = min(r, ROW_TILE)

    def body(x_ref, o_ref):
        o_ref[...] = _bf(x_ref[...])

    return pl.pallas_call(
        body, out_shape=jax.ShapeDtypeStruct((r, c), BF16), grid=(r // tr,),
        in_specs=[_row_spec(c, tr)], out_specs=_row_spec(c, tr),
        compiler_params=_params("parallel"), name=name)(x)


def _rmsnorm_fwd(x, w, name):
    s, d = x.shape

    def body(x_ref, w_ref, o_ref):
        xv = x_ref[...]
        rstd = lax.rsqrt(jnp.mean(xv * xv, axis=1, keepdims=True) + EPS)
        o_ref[...] = _bf(xv * rstd * w_ref[...])

    return pl.pallas_call(
        body, out_shape=jax.ShapeDtypeStruct((s, d), BF16), grid=(s // ROW_TILE,),
        in_specs=[_row_spec(d), _full_spec((1, d))], out_specs=_row_spec(d),
        compiler_params=_params("parallel"), name=name)(x, w)


def _rmsnorm_bwd(x, w, dh, dres, name):
    s, d = x.shape

    def body(x_ref, w_ref, dh_ref, dres_ref, dx_ref, dw_ref):
        @pl.when(pl.program_id(0) == 0)
        def _():
            dw_ref[...] = jnp.zeros_like(dw_ref)

        xv = x_ref[...]
        rstd = lax.rsqrt(jnp.mean(xv * xv, axis=1, keepdims=True) + EPS)
        xhat = xv * rstd
        dhv = dh_ref[...]
        dxhat = dhv * w_ref[...]
        dx = rstd * (dxhat - xhat * jnp.mean(dxhat * xhat, axis=1, keepdims=True))
        dx_ref[...] = dx + dres_ref[...]
        dw_ref[...] += jnp.sum(dhv * xhat, axis=0, keepdims=True)

    return pl.pallas_call(
        body,
        out_shape=(jax.ShapeDtypeStruct((s, d), F32), jax.ShapeDtypeStruct((1, d), F32)),
        grid=(s // ROW_TILE,),
        in_specs=[_row_spec(d), _full_spec((1, d)), _row_spec(d), _row_spec(d)],
        out_specs=(_row_spec(d), _full_spec((1, d))),
        compiler_params=_params("arbitrary"), name=name)(x, w, dh, dres)


def _rope_tables(pos_col, inv_freq, name):
    s = pos_col.shape[0]

    def body(p_ref, f_ref, cos_ref, sin_ref):
        ang = p_ref[...].astype(F32) * f_ref[...]
        cos_ref[...] = jnp.cos(ang)
        sin_ref[...] = jnp.sin(ang)

    out = jax.ShapeDtypeStruct((s, ROPE_HALF), F32)
    return pl.pallas_call(
        body, out_shape=(out, out), grid=(s // ROW_TILE,),
        in_specs=[_row_spec(1), _full_spec((1, ROPE_HALF))],
        out_specs=(_row_spec(ROPE_HALF), _row_spec(ROPE_HALF)),
        compiler_params=_params("parallel"), name=name)(pos_col, inv_freq)


def _merge_fwd(pg, p_r, p_s, name):
    s = pg.shape[0]

    def body(g_ref, r_ref, s_ref, o_ref):
        g = g_ref[...]
        o_ref[...] = _bf(_sigmoid(g[:, :D_MODEL]) * r_ref[...] + _sigmoid(g[:, D_MODEL:]) * s_ref[...])

    return pl.pallas_call(
        body, out_shape=jax.ShapeDtypeStruct((s, D_MODEL), BF16), grid=(s // ROW_TILE,),
        in_specs=[_row_spec(2 * D_MODEL), _row_spec(D_MODEL), _row_spec(D_MODEL)],
        out_specs=_row_spec(D_MODEL), compiler_params=_params("parallel"), name=name)(pg, p_r, p_s)


def _merge_bwd(pg, p_r, p_s, dm, name):
    s = pg.shape[0]

    def body(g_ref, r_ref, s_ref, dm_ref, dr_ref, ds_ref, dg_ref):
        g = g_ref[...]
        sr, ss = _sigmoid(g[:, :D_MODEL]), _sigmoid(g[:, D_MODEL:])
        d = dm_ref[...]
        dr_ref[...] = _bf(d * sr)
        ds_ref[...] = _bf(d * ss)
        dg_ref[:, :D_MODEL] = _bf(d * r_ref[...] * sr * (1.0 - sr))
        dg_ref[:, D_MODEL:] = _bf(d * s_ref[...] * ss * (1.0 - ss))

    o = jax.ShapeDtypeStruct((s, D_MODEL), BF16)
    return pl.pallas_call(
        body, out_shape=(o, o, jax.ShapeDtypeStruct((s, 2 * D_MODEL), BF16)), grid=(s // ROW_TILE,),
        in_specs=[_row_spec(2 * D_MODEL), _row_spec(D_MODEL), _row_spec(D_MODEL), _row_spec(D_MODEL)],
        out_specs=(_row_spec(D_MODEL), _row_spec(D_MODEL), _row_spec(2 * D_MODEL)),
        compiler_params=_params("parallel"), name=name)(pg, p_r, p_s, dm)


def _final_fwd_bwd(x, o, w, target, name):
    s, d = x.shape

    def body(x_ref, o_ref, w_ref, t_ref, dx_ref, dw_ref, loss_ref):
        @pl.when(pl.program_id(0) == 0)
        def _():
            dw_ref[...] = jnp.zeros_like(dw_ref)
            loss_ref[...] = jnp.zeros_like(loss_ref)

        x2 = x_ref[...] + o_ref[...]
        rstd = lax.rsqrt(jnp.mean(x2 * x2, axis=1, keepdims=True) + EPS)
        xhat = x2 * rstd
        wv = w_ref[...]
        err = xhat * wv - t_ref[...]
        loss_ref[...] += jnp.sum(jnp.sum(err * err, axis=1, keepdims=True), axis=0, keepdims=True) * (0.5 / d)
        dy = err * (1.0 / d)
        dw_ref[...] += jnp.sum(dy * xhat, axis=0, keepdims=True)
        dxhat = dy * wv
        dx_ref[...] = rstd * (dxhat - xhat * jnp.mean(dxhat * xhat, axis=1, keepdims=True))

    return pl.pallas_call(
        body,
        out_shape=(jax.ShapeDtypeStruct((s, d), F32), jax.ShapeDtypeStruct((1, d), F32),
                   jax.ShapeDtypeStruct((8, 128), F32)),
        grid=(s // ROW_TILE,),
        in_specs=[_row_spec(d), _row_spec(d), _full_spec((1, d)), _row_spec(d)],
        out_specs=(_row_spec(d), _full_spec((1, d)), _full_spec((8, 128))),
        compiler_params=_params("arbitrary"), name=name)(x, o, w, target)


def _adamw(w, g, m, v, name):
    r, c = w.shape
    tr = min(r, ROW_TILE)
    c1 = 1.0 / (1.0 - ADAM_B1 ** ADAM_STEP)
    c2 = 1.0 / (1.0 - ADAM_B2 ** ADAM_STEP)

    def body(w_ref, g_ref, m_ref, v_ref, d_ref, nm_ref, nv_ref):
        gv = g_ref[...]
        nm = ADAM_B1 * m_ref[...] + (1.0 - ADAM_B1) * gv
        nv = ADAM_B2 * v_ref[...] + (1.0 - ADAM_B2) * (gv * gv)
        d_ref[...] = -ADAM_LR * ((nm * c1) / (jnp.sqrt(nv * c2) + ADAM_EPS) + ADAM_WD * w_ref[...])
        nm_ref[...] = nm
        nv_ref[...] = nv

    o = jax.ShapeDtypeStruct((r, c), F32)
    sp = _row_spec(c, tr)
    return pl.pallas_call(
        body, out_shape=(o, o, o), grid=(r // tr,), in_specs=[sp, sp, sp, sp], out_specs=(sp, sp, sp),
        compiler_params=_params("parallel"), name=name)(w, g, m, v)


def _sum_slots(land, name):
    _, r, c = land.shape
    tr = min(r, ROW_TILE)

    def body(l_ref, o_ref):
        acc = l_ref[0].astype(F32)
        for i in range(1, N_DEV):
            acc = acc + l_ref[i].astype(F32)
        o_ref[...] = acc

    return pl.pallas_call(
        body, out_shape=jax.ShapeDtypeStruct((r, c), F32), grid=(r // tr,),
        in_specs=[pl.BlockSpec((N_DEV, tr, c), lambda i: (0, i, 0))], out_specs=_row_spec(c, tr),
        compiler_params=_params("parallel"), name=name)(land)


def _retention_tables():
    lg = np.log1p(-(2.0 ** (-5.0 - np.arange(RET_HEADS, dtype=np.float64))))
    idx = np.arange(CHUNK, dtype=np.float64)
    intra = np.exp(np.abs(idx[:, None] - idx[None, :])[None] * lg[:, None, None])
    qd = np.exp((idx[None, :] + 1.0) * lg[:, None])
    kd = np.exp((CHUNK - 1.0 - idx[None, :]) * lg[:, None])
    cd = np.exp(CHUNK * lg)
    ones = np.ones((1, 1, RET_DK))
    return (jnp.asarray(intra, F32), jnp.asarray(qd[:, :, None] * ones, F32),
            jnp.asarray(kd[:, :, None] * ones, F32), jnp.asarray(cd[:, None, None] * ones, F32))


def _rope(t, cos, sin):
    t1, t2 = t[:, :ROPE_HALF], t[:, ROPE_HALF:]
    return jnp.concatenate([t1 * cos - t2 * sin, t2 * cos + t1 * sin], axis=1)


def _rope_t(d, cos, sin):
    d1, d2 = d[:, :ROPE_HALF], d[:, ROPE_HALF:]
    return jnp.concatenate([d1 * cos + d2 * sin, d2 * cos - d1 * sin], axis=1)


def _ret_head_fwd(q, k, v, cos, sin, intra, qd, st):
    qr = _rope(q, cos, sin)
    kr = _rope(k, cos, sin) * (RET_DK ** -0.5)
    sc = _dot_nt(_bf(qr), _bf(kr)) * intra
    y = _dot(_bf(sc), _bf(v)) + _dot(_bf(qr), _bf(st)) * qd
    return qr, kr, sc, y


def _ret_specs(nc, rev):
    cidx = (lambda c: nc - 1 - c) if rev else (lambda c: c)
    return dict(
        proj=pl.BlockSpec((CHUNK, 4 * RET_W), lambda c: (cidx(c), 0)),
        half=pl.BlockSpec((CHUNK, ROPE_HALF), lambda c: (cidx(c), 0)),
        wide=pl.BlockSpec((CHUNK, RET_W), lambda c: (cidx(c), 0)),
        state=pl.BlockSpec((1, RET_HEADS, RET_DK, RET_DK), lambda c: (cidx(c), 0, 0, 0)),
        intra=_full_spec((RET_HEADS, CHUNK, CHUNK)),
        dec=_full_spec((RET_HEADS, CHUNK, RET_DK)),
        cd=_full_spec((RET_HEADS, 1, RET_DK)),
    )


def _retention_fwd(p_r, cos, sin, name):
    s = p_r.shape[0]
    nc = s // CHUNK
    intra_t, qd_t, kd_t, cd_t = _retention_tables()

    def body(p_ref, cos_ref, sin_ref, intra_ref, qd_ref, kd_ref, cd_ref, y_ref, st_ref, state):
        @pl.when(pl.program_id(0) == 0)
        def _():
            state[...] = jnp.zeros_like(state)

        cos, sin = cos_ref[...], sin_ref[...]
        for h in range(RET_HEADS):
            lo = h * RET_DK
            q = p_ref[:, lo:lo + RET_DK]
            k = p_ref[:, RET_W + lo:RET_W + lo + RET_DK]
            v = p_ref[:, 2 * RET_W + lo:2 * RET_W + lo + RET_DK]
            g = p_ref[:, 3 * RET_W + lo:3 * RET_W + lo + RET_DK]
            st = state[h]
            st_ref[0, h] = st
            _, kr, _, y = _ret_head_fwd(q, k, v, cos, sin, intra_ref[h], qd_ref[h], st)
            state[h] = st * cd_ref[h] + _dot_tn(_bf(kr * kd_ref[h]), _bf(v))
            mu = jnp.mean(y, axis=1, keepdims=True)
            yc = y - mu
            rstd = lax.rsqrt(jnp.mean(yc * yc, axis=1, keepdims=True) + EPS)
            y_ref[:, lo:lo + RET_DK] = _bf(yc * rstd * (g * _sigmoid(g)))

    sp = _ret_specs(nc, False)
    return pl.pallas_call(
        body,
        out_shape=(jax.ShapeDtypeStruct((s, RET_W), BF16),
                   jax.ShapeDtypeStruct((nc, RET_HEADS, RET_DK, RET_DK), F32)),
        grid=(nc,),
        in_specs=[sp["proj"], sp["half"], sp["half"], sp["intra"], sp["dec"], sp["dec"], sp["cd"]],
        out_specs=(sp["wide"], sp["state"]),
        scratch_shapes=[pltpu.VMEM((RET_HEADS, RET_DK, RET_DK), F32)],
        compiler_params=_params("arbitrary"), name=name)(p_r, cos, sin, intra_t, qd_t, kd_t, cd_t)


def _retention_bwd(p_r, cos, sin, states, dy_r, name):
    s = p_r.shape[0]
    nc = s // CHUNK
    intra_t, qd_t, kd_t, cd_t = _retention_tables()

    def body(p_ref, cos_ref, sin_ref, intra_ref, qd_ref, kd_ref, cd_ref, st_ref, dy_ref, dp_ref, dstate):
        @pl.when(pl.program_id(0) == 0)
        def _():
            dstate[...] = jnp.zeros_like(dstate)

        cos, sin = cos_ref[...], sin_ref[...]
        for h in range(RET_HEADS):
            lo = h * RET_DK
            q = p_ref[:, lo:lo + RET_DK]
            k = p_ref[:, RET_W + lo:RET_W + lo + RET_DK]
            v = p_ref[:, 2 * RET_W + lo:2 * RET_W + lo + RET_DK]
            g = p_ref[:, 3 * RET_W + lo:3 * RET_W + lo + RET_DK]
            st = st_ref[0, h]
            intra, qd, kd = intra_ref[h], qd_ref[h], kd_ref[h]
            qr, kr, sc, y = _ret_head_fwd(q, k, v, cos, sin, intra, qd, st)
            mu = jnp.mean(y, axis=1, keepdims=True)
            yc = y - mu
            rstd = lax.rsqrt(jnp.mean(yc * yc, axis=1, keepdims=True) + EPS)
            yn = yc * rstd
            sg = _sigmoid(g)
            dyr = dy_ref[:, lo:lo + RET_DK]
            dyn = dyr * (g * sg)
            dg = dyr * yn * (sg * (1.0 + g * (1.0 - sg)))
            dy = rstd * (dyn - jnp.mean(dyn, axis=1, keepdims=True)
                         - yn * jnp.mean(dyn * yn, axis=1, keepdims=True))
            dyb, vb, qrb = _bf(dy), _bf(v), _bf(qr)
            dsn = dstate[h]
            dsnb = _bf(dsn)
            ds = _bf(_dot_nt(dyb, vb) * intra)
            t = _bf(dy * qd)
            dv = _dot_tn(_bf(sc), dyb) + _dot(_bf(kr * kd), dsnb)
            dqr = _dot(ds, _bf(kr)) + _dot_nt(t, _bf(st))
            dkr = _dot_tn(ds, qrb) + _dot_nt(vb, dsnb) * kd
            dstate[h] = dsn * cd_ref[h] + _dot_tn(qrb, t)
            dp_ref[:, lo:lo + RET_DK] = _bf(_rope_t(dqr, cos, sin))
            dp_ref[:, RET_W + lo:RET_W + lo + RET_DK] = _bf(_rope_t(dkr, cos, sin) * (RET_DK ** -0.5))
            dp_ref[:, 2 * RET_W + lo:2 * RET_W + lo + RET_DK] = _bf(dv)
            dp_ref[:, 3 * RET_W + lo:3 * RET_W + lo + RET_DK] = _bf(dg)

    sp = _ret_specs(nc, True)
    return pl.pallas_call(
        body,
        out_shape=jax.ShapeDtypeStruct((s, 4 * RET_W), BF16),
        grid=(nc,),
        in_specs=[sp["proj"], sp["half"], sp["half"], sp["intra"], sp["dec"], sp["dec"], sp["cd"],
                  sp["state"], sp["wide"]],
        out_specs=sp["proj"],
        scratch_shapes=[pltpu.VMEM((RET_HEADS, RET_DK, RET_DK), F32)],
        compiler_params=_params("arbitrary"), name=name)(p_r, cos, sin, intra_t, qd_t, kd_t, cd_t, states, dy_r)


def _conv_taps(ext, w):
    acc = w[SSD_CONV - 1:SSD_CONV] * ext
    for j in range(SSD_CONV - 1):
        acc = acc + w[j:j + 1] * pltpu.roll(ext, SSD_CONV - 1 - j, axis=0)
    return acc


def _conv_fwd(xbc_raw, conv_w, conv_b, name):
    s = xbc_raw.shape[0]
    t8 = CONV_TILE // 8

    def body(cur_ref, prev_ref, w_ref, b_ref, o_ref):
        keep = (pl.program_id(0) > 0).astype(F32)
        ext = jnp.concatenate([prev_ref[...] * keep, cur_ref[...]], axis=0)
        u = _conv_taps(ext, w_ref[...])[8:] + b_ref[...]
        o_ref[...] = u * _sigmoid(u)

    return pl.pallas_call(
        body, out_shape=jax.ShapeDtypeStruct((s, SSD_CD), F32), grid=(s // CONV_TILE,),
        in_specs=[_row_spec(SSD_CD, CONV_TILE),
                  pl.BlockSpec((8, SSD_CD), lambda i: (jnp.maximum(i * t8 - 1, 0), 0)),
                  _full_spec((SSD_CONV, SSD_CD)), _full_spec((1, SSD_CD))],
        out_specs=_row_spec(SSD_CD, CONV_TILE),
        compiler_params=_params("parallel"), name=name)(xbc_raw, xbc_raw, conv_w, conv_b)


def _conv_bwd(xbc_raw, conv_w, conv_b, dact, name):
    s = xbc_raw.shape[0]
    nt = s // CONV_TILE
    t8 = CONV_TILE // 8
    rows = CONV_TILE + 8

    def body(cur_ref, prev_ref, next_ref, w_ref, b_ref, d_ref, dnext_ref, dx_ref, dw_ref, db_ref):
        i = pl.program_id(0)

        @pl.when(i == 0)
        def _():
            dw_ref[...] = jnp.zeros_like(dw_ref)
            db_ref[...] = jnp.zeros_like(db_ref)

        keep_prev = (i > 0).astype(F32)
        keep_next = (i < nt - 1).astype(F32)
        w = w_ref[...]
        ext = jnp.concatenate([prev_ref[...] * keep_prev, cur_ref[...], next_ref[...]], axis=0)
        shifted = [pltpu.roll(ext, SSD_CONV - 1 - j, axis=0)[8:] for j in range(SSD_CONV - 1)] + [ext[8:]]
        u = b_ref[...]
        for j in range(SSD_CONV):
            u = u + w[j:j + 1] * shifted[j]
        sg = _sigmoid(u)
        dact = jnp.concatenate([d_ref[...], dnext_ref[...] * keep_next], axis=0)
        du = dact * (sg * (1.0 + u * (1.0 - sg)))
        dx = w[SSD_CONV - 1:SSD_CONV] * du
        for j in range(SSD_CONV - 1):
            dx = dx + w[j:j + 1] * pltpu.roll(du, rows - (SSD_CONV - 1 - j), axis=0)
        dx_ref[...] = _bf(dx[:CONV_TILE])
        duc = du[:CONV_TILE]
        for j in range(SSD_CONV):
            dw_ref[j:j + 1, :] += jnp.sum(duc * shifted[j][:CONV_TILE], axis=0, keepdims=True)
        db_ref[...] += jnp.sum(duc, axis=0, keepdims=True)

    row = _row_spec(SSD_CD, CONV_TILE)
    prev = pl.BlockSpec((8, SSD_CD), lambda i: (jnp.maximum(i * t8 - 1, 0), 0))
    nxt = pl.BlockSpec((8, SSD_CD), lambda i: (jnp.minimum((i + 1) * t8, s // 8 - 1), 0))
    return pl.pallas_call(
        body,
        out_shape=(jax.ShapeDtypeStruct((s, SSD_CD), BF16), jax.ShapeDtypeStruct((SSD_CONV, SSD_CD), F32),
                   jax.ShapeDtypeStruct((1, SSD_CD), F32)),
        grid=(nt,),
        in_specs=[row, prev, nxt, _full_spec((SSD_CONV, SSD_CD)), _full_spec((1, SSD_CD)), row, nxt],
        out_specs=(row, _full_spec((SSD_CONV, SSD_CD)), _full_spec((1, SSD_CD))),
        compiler_params=_params("arbitrary"), name=name)(xbc_raw, xbc_raw, xbc_raw, conv_w, conv_b, dact, dact)


def _head_select():
    e = np.zeros((HEAD_PAD, SSD_W), np.float32)
    for h in range(SSD_HEADS):
        e[h, h * SSD_P:(h + 1) * SSD_P] = 1.0
    return jnp.asarray(e, BF16), jnp.asarray(e.T, BF16)


def _pad_heads(v):
    return jnp.pad(v.reshape(1, SSD_HEADS).astype(F32), ((0, 0), (0, HEAD_PAD - SSD_HEADS)))


def _ssd_masks():
    r = _iota((CHUNK, SSD_QW), 0)
    c = _iota((CHUNK, SSD_QW), 1) % SSD_P
    itile = (r == c).astype(F32)
    ctile = r >= c
    rb = _iota((SSD_QW, SSD_QW), 0) // SSD_P
    cb = _iota((SSD_QW, SSD_QW), 1) // SSD_P
    return itile, ctile, rb == cb


def _ssd_prep(xbc_ref, dtr_ref, dtb_ref, alog_ref, e_ref):
    xs = xbc_ref[:, :SSD_W]
    u = dtr_ref[...] + dtb_ref[...]
    dt = _softplus(u)
    nexp = -jnp.exp(alog_ref[...])
    a = dt * nexp
    tril = _bf((_iota((CHUNK, CHUNK), 0) >= _iota((CHUNK, CHUNK), 1)).astype(F32))
    acum = _dot_exact_r(tril, a)
    e = e_ref[...]
    acol = _dot_exact_l(acum, e)
    dte = _dot_exact_l(dt, e)
    alast = acol[CHUNK - 1:CHUNK, :]
    return dict(xs=xs, u=u, dt=dt, nexp=nexp, a=a, acol=acol, dte=dte, xdt=xs * dte,
                ea=jnp.exp(acol), tail=jnp.exp(alast - acol), eal=jnp.exp(alast))


def _tile4(x):
    return jnp.concatenate([x, x, x, x], axis=0)


def _fold4(x):
    return x[0:CHUNK] + x[CHUNK:2 * CHUNK] + x[2 * CHUNK:3 * CHUNK] + x[3 * CHUNK:4 * CHUNK]


def _ssd_specs(nc, rev):
    cidx = (lambda c: nc - 1 - c) if rev else (lambda c: c)
    return dict(
        xbc=pl.BlockSpec((CHUNK, SSD_CD), lambda c: (cidx(c), 0)),
        dt=pl.BlockSpec((CHUNK, HEAD_PAD), lambda c: (cidx(c), 0)),
        wide=pl.BlockSpec((CHUNK, SSD_W), lambda c: (cidx(c), 0)),
        state=pl.BlockSpec((1, SSD_N, SSD_W), lambda c: (cidx(c), 0, 0)),
        head=_full_spec((1, HEAD_PAD)),
        roww=_full_spec((1, SSD_W)),
        e=_full_spec((HEAD_PAD, SSD_W)),
        et=_full_spec((SSD_W, HEAD_PAD)),
    )


def _ssd_fwd(xbc, dt_raw, z, dt_bias, a_log, d_skip, norm_w, name):
    s = xbc.shape[0]
    nc = s // CHUNK
    e_sel, _ = _head_select()

    def body(xbc_ref, dtr_ref, z_ref, dtb_ref, alog_ref, dsk_ref, nw_ref, e_ref,
             yraw_ref, ys_ref, st_ref, state):
        @pl.when(pl.program_id(0) == 0)
        def _():
            state[...] = jnp.zeros_like(state)

        p = _ssd_prep(xbc_ref, dtr_ref, dtb_ref, alog_ref, e_ref)
        itile, ctile, bdmask = _ssd_masks()
        dske = _dot_exact_l(jnp.broadcast_to(dsk_ref[...], (8, HEAD_PAD)), e_ref[...])[0:1]
        st_ref[0] = state[...]
        for g in range(SSD_GROUPS):
            gs = slice(g * SSD_GW, (g + 1) * SSD_GW)
            bg = _bf(xbc_ref[:, B_OFF + g * SSD_N:B_OFF + (g + 1) * SSD_N])
            cg = _bf(xbc_ref[:, C_OFF + g * SSD_N:C_OFF + (g + 1) * SSD_N])
            cbt = _dot_nt(cg, _tile4(bg))
            stg = state[:, gs]
            ys = _dot(cg, _bf(stg))
            for qd in range(SSD_GW // SSD_QW):
                ql = slice(g * SSD_GW + qd * SSD_QW, g * SSD_GW + (qd + 1) * SSD_QW)
                aq = p["acol"][:, ql]
                arow = jnp.sum(aq * itile, axis=0, keepdims=True)
                dq = jnp.exp(jnp.where(ctile, aq - arow, -jnp.inf))
                xbd = jnp.where(bdmask, _tile4(_bf(p["xdt"][:, ql])), jnp.zeros((), BF16))
                yq = _dot(_bf(cbt * dq), xbd) + ys[:, qd * SSD_QW:(qd + 1) * SSD_QW] * p["ea"][:, ql]
                yraw_ref[:, ql] = yq + dske[:, ql] * p["xs"][:, ql]
            wt = _bf(p["xdt"][:, gs] * p["tail"][:, gs])
            state[:, gs] = stg * p["eal"][:, gs] + _dot_tn(bg, wt)
        zv = z_ref[...]
        t = yraw_ref[...] * (zv * _sigmoid(zv))
        rstd = lax.rsqrt(jnp.mean(t * t, axis=1, keepdims=True) + EPS)
        ys_ref[...] = _bf(t * rstd * nw_ref[...])

    sp = _ssd_specs(nc, False)
    return pl.pallas_call(
        body,
        out_shape=(jax.ShapeDtypeStruct((s, SSD_W), F32), jax.ShapeDtypeStruct((s, SSD_W), BF16),
                   jax.ShapeDtypeStruct((nc, SSD_N, SSD_W), F32)),
        grid=(nc,),
        in_specs=[sp["xbc"], sp["dt"], sp["wide"], sp["head"], sp["head"], sp["head"], sp["roww"], sp["e"]],
        out_specs=(sp["wide"], sp["wide"], sp["state"]),
        scratch_shapes=[pltpu.VMEM((SSD_N, SSD_W), F32)],
        compiler_params=_params("arbitrary"), name=name,
    )(xbc, dt_raw, z, dt_bias, a_log, d_skip, norm_w, e_sel)


def _ssd_bwd(xbc, dt_raw, z, yraw, states, dys, dt_bias, a_log, d_skip, norm_w, name):
    s = xbc.shape[0]
    nc = s // CHUNK
    e_sel, et_sel = _head_select()

    def body(xbc_ref, dtr_ref, z_ref, yraw_ref, st_ref, dys_ref, dtb_ref, alog_ref, dsk_ref, nw_ref,
             e_ref, et_ref, dxbc_ref, dz_ref, ddtr_ref, ddtb_ref, dalog_ref, ddsk_ref, dnw_ref,
             dstate, dacol_s, ddte_s):
        @pl.when(pl.program_id(0) == 0)
        def _():
            dstate[...] = jnp.zeros_like(dstate)
            ddtb_ref[...] = jnp.zeros_like(ddtb_ref)
            dalog_ref[...] = jnp.zeros_like(dalog_ref)
            ddsk_ref[...] = jnp.zeros_like(ddsk_ref)
            dnw_ref[...] = jnp.zeros_like(dnw_ref)

        p = _ssd_prep(xbc_ref, dtr_ref, dtb_ref, alog_ref, e_ref)
        itile, ctile, bdmask = _ssd_masks()
        et = et_ref[...]
        dske = _dot_exact_l(jnp.broadcast_to(dsk_ref[...], (8, HEAD_PAD)), e_ref[...])[0:1]
        last_row = (_iota((CHUNK, 1), 0) == CHUNK - 1).astype(F32)

        zv, y, dysv = z_ref[...], yraw_ref[...], dys_ref[...]
        sz = _sigmoid(zv)
        silz = zv * sz
        t = y * silz
        rstd = lax.rsqrt(jnp.mean(t * t, axis=1, keepdims=True) + EPS)
        that = t * rstd
        dthat = dysv * nw_ref[...]
        dt_ = rstd * (dthat - that * jnp.mean(dthat * that, axis=1, keepdims=True))
        dnw_ref[...] += jnp.sum(dysv * that, axis=0, keepdims=True)
        dz_ref[...] = _bf(dt_ * y * (sz * (1.0 + zv * (1.0 - sz))))
        dy = dt_ * silz
        dsk_row = jnp.sum(dy * p["xs"], axis=0, keepdims=True)
        ddsk_ref[...] += _dot_exact_l(jnp.broadcast_to(dsk_row, (8, SSD_W)), et)[0:1]

        for g in range(SSD_GROUPS):
            gs = slice(g * SSD_GW, (g + 1) * SSD_GW)
            bg = _bf(xbc_ref[:, B_OFF + g * SSD_N:B_OFF + (g + 1) * SSD_N])
            cg = _bf(xbc_ref[:, C_OFF + g * SSD_N:C_OFF + (g + 1) * SSD_N])
            btile = _tile4(bg)
            cbt = _dot_nt(cg, btile)
            stg = st_ref[0, :, gs]
            stgb = _bf(stg)
            dyg = dy[:, gs]
            eag, tailg, xdtg, ealg = p["ea"][:, gs], p["tail"][:, gs], p["xdt"][:, gs], p["eal"][:, gs]
            dys_g = _bf(dyg * eag)
            dacol_g = dyg * (_dot(cg, stgb) * eag)
            dc = _dot_nt(dys_g, stgb)
            dst = _dot_tn(cg, dys_g)
            dsn = dstate[:, gs]
            dsnb = _bf(dsn)
            dwt = _dot(bg, dsnb)
            db = _dot_nt(_bf(xdtg * tailg), dsnb)
            dtl = dwt * xdtg * tailg
            dacol_g = dacol_g - dtl
            dal_row = jnp.sum(dtl, axis=0, keepdims=True) + jnp.sum(dsn * stg, axis=0, keepdims=True) * ealg
            dstate[:, gs] = dst + dsn * ealg
            dxdt_q = []
            dacol_q = []
            for qd in range(SSD_GW // SSD_QW):
                ql = slice(g * SSD_GW + qd * SSD_QW, g * SSD_GW + (qd + 1) * SSD_QW)
                aq = p["acol"][:, ql]
                arow = jnp.sum(aq * itile, axis=0, keepdims=True)
                dq = jnp.exp(jnp.where(ctile, aq - arow, -jnp.inf))
                mq = cbt * dq
                xbd = jnp.where(bdmask, _tile4(_bf(p["xdt"][:, ql])), jnp.zeros((), BF16))
                dyq = _bf(dy[:, ql])
                dm = _dot_nt(dyq, xbd)
                dxbd = jnp.where(bdmask, _dot_tn(_bf(mq), dyq), 0.0)
                dxdt_q.append(_fold4(dxbd))
                eq = dm * mq
                dacol_q.append(eq - itile * jnp.sum(eq, axis=0, keepdims=True))
                dcbt = _bf(dm * dq)
                dc = dc + _dot(dcbt, btile)
                db = db + _fold4(_dot_tn(dcbt, cg))
            dxdt_g = dwt * tailg + jnp.concatenate(dxdt_q, axis=1)
            dacol_s[:, gs] = dacol_g + jnp.concatenate(dacol_q, axis=1) + last_row * dal_row
            ddte_s[:, gs] = dxdt_g * p["xs"][:, gs]
            dxbc_ref[:, gs] = dyg * dske[:, gs] + dxdt_g * p["dte"][:, gs]
            dxbc_ref[:, B_OFF + g * SSD_N:B_OFF + (g + 1) * SSD_N] = db
            dxbc_ref[:, C_OFF + g * SSD_N:C_OFF + (g + 1) * SSD_N] = dc

        dacum = _dot_exact_l(dacol_s[...], et)
        triu = _bf((_iota((CHUNK, CHUNK), 1) >= _iota((CHUNK, CHUNK), 0)).astype(F32))
        da = _dot_exact_r(triu, dacum)
        ddt = _dot_exact_l(ddte_s[...], et) + da * p["nexp"]
        dalog_ref[...] += jnp.sum(da * p["a"], axis=0, keepdims=True)
        du = ddt * _sigmoid(p["u"])
        ddtr_ref[...] = _bf(du)
        ddtb_ref[...] += jnp.sum(du, axis=0, keepdims=True)

    sp = _ssd_specs(nc, True)
    return pl.pallas_call(
        body,
        out_shape=(jax.ShapeDtypeStruct((s, SSD_CD), F32), jax.ShapeDtypeStruct((s, SSD_W), BF16),
                   jax.ShapeDtypeStruct((s, HEAD_PAD), BF16), jax.ShapeDtypeStruct((1, HEAD_PAD), F32),
                   jax.ShapeDtypeStruct((1, HEAD_PAD), F32), jax.ShapeDtypeStruct((1, HEAD_PAD), F32),
                   jax.ShapeDtypeStruct((1, SSD_W), F32)),
        grid=(nc,),
        in_specs=[sp["xbc"], sp["dt"], sp["wide"], sp["wide"], sp["state"], sp["wide"],
                  sp["head"], sp["head"], sp["head"], sp["roww"], sp["e"], sp["et"]],
        out_specs=(sp["xbc"], sp["wide"], sp["dt"], sp["head"], sp["head"], sp["head"], sp["roww"]),
        scratch_shapes=[pltpu.VMEM((SSD_N, SSD_W), F32), pltpu.VMEM((CHUNK, SSD_W), F32),
                        pltpu.VMEM((CHUNK, SSD_W), F32)],
        compiler_params=_params("arbitrary"), name=name,
    )(xbc, dt_raw, z, yraw, states, dys, dt_bias, a_log, d_skip, norm_w, e_sel, et_sel)


def _local_step(x, pos_col, target, norm1_w, conv_w, conv_b, dt_bias, a_log, d_skip, ssd_norm_w, norm_f_w,
                w_r, w_z, w_xbc, w_dt, w_g, w_br_ret, w_br_ssd, w_out):
    inv_freq = jnp.asarray(ROPE_THETA ** (-np.arange(ROPE_HALF, dtype=np.float64) / ROPE_HALF), F32).reshape(1, ROPE_HALF)
    dtb, alog, dsk = _pad_heads(dt_bias), _pad_heads(a_log), _pad_heads(d_skip)

    h = _rmsnorm_fwd(x, norm1_w, "rmsnorm1_fwd")
    p_r = _matmul(h, w_r, "nn", F32, "proj_ret")
    p_z = _matmul(h, w_z, "nn", F32, "proj_z")
    p_xbc = _matmul(h, w_xbc, "nn", F32, "proj_xbc")
    p_dt = _matmul(h, w_dt, "nn", F32, "proj_dt")
    p_g = _matmul(h, w_g, "nn", F32, "proj_gates")
    cos, sin = _rope_tables(pos_col, inv_freq, "rope_tables")
    y_r, ret_states = _retention_fwd(p_r, cos, sin, "retention_fwd")
    xbc_act = _conv_fwd(p_xbc, conv_w, conv_b, "conv_fwd")
    y_raw, y_s, ssd_states = _ssd_fwd(xbc_act, p_dt, p_z, dtb, alog, dsk, ssd_norm_w, "ssd_fwd")
    pr = _matmul(y_r, w_br_ret, "nn", F32, "branch_ret")
    ps = _matmul(y_s, w_br_ssd, "nn", F32, "branch_ssd")
    merged = _merge_fwd(p_g, pr, ps, "merge_fwd")
    o = _matmul(merged, w_out, "nn", F32, "out_proj")
    dx2, g_norm_f, loss_acc = _final_fwd_bwd(x, o, norm_f_w, target, "final_norm_loss")

    g_w_out = _matmul(merged, dx2, "tn", BF16, "grad_w_out")
    dmerged = _matmul(dx2, w_out, "nt", F32, "d_merged")
    dpr, dps, dp_g = _merge_bwd(p_g, pr, ps, dmerged, "merge_bwd")
    g_w_br_ret = _matmul(y_r, dpr, "tn", BF16, "grad_w_br_ret")
    g_w_br_ssd = _matmul(y_s, dps, "tn", BF16, "grad_w_br_ssd")
    dy_r = _matmul(dpr, w_br_ret, "nt", F32, "d_y_ret")
    dy_s = _matmul(dps, w_br_ssd, "nt", F32, "d_y_ssd")
    dxbc_act, dp_z, dp_dt, g_dtb, g_alog, g_dsk, g_ssd_norm = _ssd_bwd(
        xbc_act, p_dt, p_z, y_raw, ssd_states, dy_s, dtb, alog, dsk, ssd_norm_w, "ssd_bwd")
    dp_xbc, g_conv_w, g_conv_b = _conv_bwd(p_xbc, conv_w, conv_b, dxbc_act, "conv_bwd")
    dp_r = _retention_bwd(p_r, cos, sin, ret_states, dy_r, "retention_bwd")
    dh = _matmul(dp_r, w_r, "nt", F32, "d_h_ret")
    dh = _matmul(dp_z, w_z, "nt", F32, "d_h_z", add=dh)
    dh = _matmul(dp_xbc, w_xbc, "nt", F32, "d_h_xbc", add=dh)
    dh = _matmul(dp_dt, w_dt, "nt", F32, "d_h_dt", add=dh)
    dh = _matmul(dp_g, w_g, "nt", F32, "d_h_gates", add=dh)
    g_w_in = jnp.concatenate([
        _matmul(h, dp_r, "tn", BF16, "grad_w_ret"),
        _matmul(h, dp_z, "tn", BF16, "grad_w_z"),
        _matmul(h, dp_xbc, "tn", BF16, "grad_w_xbc"),
        _matmul(h, dp_dt, "tn", BF16, "grad_w_dt")[:, :SSD_HEADS],
        _matmul(h, dp_g, "tn", BF16, "grad_w_gates"),
    ], axis=1)
    grad_x, g_norm1 = _rmsnorm_bwd(x, norm1_w, dh, dx2, "rmsnorm1_bwd")
    small = dict(norm1_w=g_norm1, conv_w=g_conv_w, conv_b=g_conv_b, dt_bias=g_dtb[:, :SSD_HEADS],
                 a_log=g_alog[:, :SSD_HEADS], d_skip=g_dsk[:, :SSD_HEADS], ssd_norm_w=g_ssd_norm,
                 norm_f_w=g_norm_f)
    big = dict(w_in=g_w_in, w_br_ret=g_w_br_ret, w_br_ssd=g_w_br_ssd, w_out=g_w_out)
    return loss_acc[0, 0], grad_x, small, big


def _mesh_pos():
    return lax.axis_index("x"), lax.axis_index("y"), lax.axis_index("c")


def _all_gather(shard, name):
    r, c = shard.shape

    def body(x_ref, out_ref, send_sems, recv_sems, local_sem):
        x, y, cc = _mesh_pos()
        me, sibling = (x, y, cc), (x, y, 1 - cc)
        chips = [(1 - x, y), (x, 1 - y), (1 - x, 1 - y)]

        def slot(px, py, pc):
            return out_ref.at[4 * px + 2 * py + pc]

        def copy(k, block, to, src=None):
            return pltpu.make_async_remote_copy(
                src_ref=slot(*block) if src is None else src, dst_ref=slot(*block),
                send_sem=send_sems.at[k], recv_sem=recv_sems.at[k], device_id=to, device_id_type=MESH)

        mine = pltpu.make_async_copy(x_ref, slot(*me), local_sem)
        mine.start()
        first = [copy(0, me, sibling, src=x_ref)]
        first += [copy(1 + j, me, (*chip, cc), src=x_ref) for j, chip in enumerate(chips)]
        for cp in first:
            cp.start()
        passed = [copy(4 + j, (*chip, cc), sibling) for j, chip in enumerate(chips)]
        for j, chip in enumerate(chips):
            copy(1 + j, (*chip, cc), me).wait_recv()
            passed[j].start()
        copy(0, sibling, me).wait_recv()
        for j, chip in enumerate(chips):
            copy(4 + j, (*chip, 1 - cc), me).wait_recv()
        for cp in first + passed:
            cp.wait_send()
        mine.wait()

    return pl.pallas_call(
        body,
        out_shape=jax.ShapeDtypeStruct((N_DEV, r, c), shard.dtype),
        in_specs=[pl.BlockSpec(memory_space=pl.ANY)],
        out_specs=pl.BlockSpec(memory_space=pl.ANY),
        scratch_shapes=[pltpu.SemaphoreType.DMA((7,)), pltpu.SemaphoreType.DMA((7,)), pltpu.SemaphoreType.DMA],
        name=name)(shard)


def _scatter_blocks(blocks, name):
    _, r, c = blocks.shape

    def body(x_ref, out_ref, send_sems, recv_sems, local_sem):
        x, y, cc = _mesh_pos()
        my_idx = 4 * x + 2 * y + cc
        mine = pltpu.make_async_copy(x_ref.at[my_idx], out_ref.at[my_idx], local_sem)
        mine.start()
        copies = []
        for k in range(1, N_DEV):
            px, py, pc = x ^ (k >> 2), y ^ ((k >> 1) & 1), cc ^ (k & 1)
            cp = pltpu.make_async_remote_copy(
                src_ref=x_ref.at[4 * px + 2 * py + pc], dst_ref=out_ref.at[my_idx],
                send_sem=send_sems.at[k - 1], recv_sem=recv_sems.at[k - 1],
                device_id=(px, py, pc), device_id_type=MESH)
            cp.start()
            copies.append(cp)
        for k in range(1, N_DEV):
            px, py, pc = x ^ (k >> 2), y ^ ((k >> 1) & 1), cc ^ (k & 1)
            pltpu.make_async_remote_copy(
                src_ref=x_ref.at[my_idx], dst_ref=out_ref.at[4 * px + 2 * py + pc],
                send_sem=send_sems.at[k - 1], recv_sem=recv_sems.at[k - 1],
                device_id=(px, py, pc), device_id_type=MESH).wait_recv()
        for cp in copies:
            cp.wait_send()
        mine.wait()

    return pl.pallas_call(
        body,
        out_shape=jax.ShapeDtypeStruct(blocks.shape, blocks.dtype),
        in_specs=[pl.BlockSpec(memory_space=pl.ANY)],
        out_specs=pl.BlockSpec(memory_space=pl.ANY),
        scratch_shapes=[pltpu.SemaphoreType.DMA((7,)), pltpu.SemaphoreType.DMA((7,)), pltpu.SemaphoreType.DMA],
        name=name)(blocks)


def _all_reduce_small(vec, name):
    r, c = vec.shape

    def body(x_ref, out_ref, land, send_sems, recv_sems):
        x, y, cc = _mesh_pos()
        my_idx = 4 * x + 2 * y + cc
        land[my_idx] = x_ref[...]
        copies = []
        for k in range(1, N_DEV):
            px, py, pc = x ^ (k >> 2), y ^ ((k >> 1) & 1), cc ^ (k & 1)
            cp = pltpu.make_async_remote_copy(
                src_ref=x_ref, dst_ref=land.at[my_idx],
                send_sem=send_sems.at[k - 1], recv_sem=recv_sems.at[k - 1],
                device_id=(px, py, pc), device_id_type=MESH)
            cp.start()
            copies.append(cp)
        for k in range(1, N_DEV):
            px, py, pc = x ^ (k >> 2), y ^ ((k >> 1) & 1), cc ^ (k & 1)
            pltpu.make_async_remote_copy(
                src_ref=x_ref, dst_ref=land.at[4 * px + 2 * py + pc],
                send_sem=send_sems.at[k - 1], recv_sem=recv_sems.at[k - 1],
                device_id=(px, py, pc), device_id_type=MESH).wait_recv()
        for cp in copies:
            cp.wait_send()
        acc = land[0]
        for i in range(1, N_DEV):
            acc = acc + land[i]
        out_ref[...] = acc

    return pl.pallas_call(
        body,
        out_shape=jax.ShapeDtypeStruct((r, c), F32),
        in_specs=[pl.BlockSpec(memory_space=pltpu.VMEM)],
        out_specs=pl.BlockSpec(memory_space=pltpu.VMEM),
        scratch_shapes=[pltpu.VMEM((N_DEV, r, c), F32), pltpu.SemaphoreType.DMA((7,)),
                        pltpu.SemaphoreType.DMA((7,))],
        name=name)(vec)


def _reduce_scatter(full_grad, axis, name):
    r, c = full_grad.shape
    if axis == 0:
        blocks = full_grad.reshape(N_DEV, r // N_DEV, c)
    else:
        blocks = full_grad.reshape(r, N_DEV, c // N_DEV).transpose(1, 0, 2)
    return _sum_slots(_scatter_blocks(blocks, name + "_exchange"), name + "_sum")


_SMALL = ("norm1_w", "conv_w", "conv_b", "dt_bias", "a_log", "d_skip", "ssd_norm_w", "norm_f_w")
_SMALL_COLS = 128
_WEIGHTS = ("norm1_w", "w_in", "conv_w", "conv_b", "dt_bias", "a_log", "d_skip", "ssd_norm_w",
            "w_br_ret", "w_br_ssd", "w_out", "norm_f_w")


def _pack(parts):
    flat = jnp.concatenate([p.reshape(-1).astype(F32) for p in parts])
    rows = -(-flat.shape[0] // (8 * _SMALL_COLS)) * 8
    return jnp.pad(flat, (0, rows * _SMALL_COLS - flat.shape[0])).reshape(rows, _SMALL_COLS)


def _unpack(packed, shapes):
    flat = packed.reshape(-1)
    out, off = [], 0
    for shp in shapes:
        n = int(np.prod(shp))
        out.append(flat[off:off + n].reshape(shp))
        off += n
    return out


def kernel(x, positions, norm1_w, w_in, conv_w, conv_b, dt_bias, a_log, d_skip, ssd_norm_w, w_br_ret, w_br_ssd, w_out, norm_f_w, loss_target, m_norm1_w, m_w_in, m_conv_w, m_conv_b, m_dt_bias, m_a_log, m_d_skip, m_ssd_norm_w, m_w_br_ret, m_w_br_ssd, m_w_out, m_norm_f_w, v_norm1_w, v_w_in, v_conv_w, v_conv_b, v_dt_bias, v_a_log, v_d_skip, v_ssd_norm_w, v_w_br_ret, v_w_br_ssd, v_w_out, v_norm_f_w):
    w = dict(norm1_w=norm1_w, w_in=w_in, conv_w=conv_w, conv_b=conv_b, dt_bias=dt_bias, a_log=a_log,
             d_skip=d_skip, ssd_norm_w=ssd_norm_w, w_br_ret=w_br_ret, w_br_ssd=w_br_ssd, w_out=w_out,
             norm_f_w=norm_f_w)
    m = dict(norm1_w=m_norm1_w, w_in=m_w_in, conv_w=m_conv_w, conv_b=m_conv_b, dt_bias=m_dt_bias,
             a_log=m_a_log, d_skip=m_d_skip, ssd_norm_w=m_ssd_norm_w, w_br_ret=m_w_br_ret,
             w_br_ssd=m_w_br_ssd, w_out=m_w_out, norm_f_w=m_norm_f_w)
    v = dict(norm1_w=v_norm1_w, w_in=v_w_in, conv_w=v_conv_w, conv_b=v_conv_b, dt_bias=v_dt_bias,
             a_log=v_a_log, d_skip=v_d_skip, ssd_norm_w=v_ssd_norm_w, w_br_ret=v_w_br_ret,
             w_br_ssd=v_w_br_ssd, w_out=v_w_out, norm_f_w=v_norm_f_w)
    s = x.shape[1]
    my_idx = 4 * lax.axis_index("x") + 2 * lax.axis_index("y") + lax.axis_index("c")

    w_in_all = _all_gather(_cast_bf16(w_in[0], "cast_w_in"), "gather_w_in")
    w_in_full = w_in_all.transpose(1, 0, 2).reshape(D_MODEL, IN_PROJ)
    w_dt = jnp.pad(w_in_full[:, OFF_DT:OFF_G], ((0, 0), (0, HEAD_PAD - SSD_HEADS)))
    w_br_ret_full = _all_gather(_cast_bf16(w_br_ret[0], "cast_w_br_ret"), "gather_w_br_ret").reshape(RET_W, D_MODEL)
    w_br_ssd_full = _all_gather(_cast_bf16(w_br_ssd[0], "cast_w_br_ssd"), "gather_w_br_ssd").reshape(SSD_W, D_MODEL)
    w_out_full = _all_gather(_cast_bf16(w_out[0], "cast_w_out"), "gather_w_out").reshape(D_MODEL, D_MODEL)
    conv_w_full = _all_gather(conv_w[0], "gather_conv_w").transpose(1, 0, 2).reshape(SSD_CONV, SSD_CD)

    loss_part, grad_x, g_small, g_big = _local_step(
        x[0], positions.reshape(s, 1), loss_target[0], norm1_w, conv_w_full, conv_b, dt_bias, a_log, d_skip,
        ssd_norm_w, norm_f_w.reshape(1, D_MODEL),
        w_in_full[:, :OFF_Z], w_in_full[:, OFF_Z:OFF_XBC], w_in_full[:, OFF_XBC:OFF_DT], w_dt,
        w_in_full[:, OFF_G:], w_br_ret_full, w_br_ssd_full, w_out_full)

    loss = lax.psum(loss_part, ("x", "y", "c"))

    small_shapes = [g_small[n].shape for n in _SMALL]
    summed = _unpack(_all_reduce_small(_pack([g_small[n] for n in _SMALL]), "allreduce_small"), small_shapes)
    grads = dict(zip(_SMALL, summed))
    conv_cols = SSD_CD // N_DEV
    grads["conv_w"] = lax.dynamic_slice_in_dim(grads["conv_w"], my_idx * conv_cols, conv_cols, axis=1)
    grads["norm_f_w"] = grads["norm_f_w"].reshape(D_MODEL)
    grads["w_in"] = _reduce_scatter(g_big["w_in"], 1, "rs_w_in")
    grads["w_br_ret"] = _reduce_scatter(g_big["w_br_ret"], 0, "rs_w_br_ret")
    grads["w_br_ssd"] = _reduce_scatter(g_big["w_br_ssd"], 0, "rs_w_br_ssd")
    grads["w_out"] = _reduce_scatter(g_big["w_out"], 0, "rs_w_out")
    for n in ("norm1_w", "conv_w", "conv_b", "dt_bias", "a_log", "d_skip", "ssd_norm_w"):
        grads[n] = grads[n].reshape(w[n].shape)
    for n in ("w_in", "w_br_ret", "w_br_ssd", "w_out"):
        grads[n] = grads[n].reshape(w[n].shape)

    delta, new_m, new_v = {}, {}, {}
    for n in ("w_in", "w_br_ret", "w_br_ssd", "w_out"):
        shp = w[n].shape
        two_d = (shp[1], shp[2])
        d_, m_, v_ = _adamw(w[n].reshape(two_d), grads[n].reshape(two_d), m[n].reshape(two_d),
                            v[n].reshape(two_d), "adamw_" + n)
        delta[n], new_m[n], new_v[n] = d_.reshape(shp), m_.reshape(shp), v_.reshape(shp)
    shapes = [w[n].shape for n in _SMALL]
    packed = _adamw(_pack([w[n] for n in _SMALL]), _pack([grads[n] for n in _SMALL]),
                    _pack([m[n] for n in _SMALL]), _pack([v[n] for n in _SMALL]), "adamw_small")
    for res, dst in zip(packed, (delta, new_m, new_v)):
        for n, a in zip(_SMALL, _unpack(res, shapes)):
            dst[n] = a

    return (loss, grad_x.reshape(x.shape), *[grads[n] for n in _WEIGHTS], *[delta[n] for n in _WEIGHTS],
            *[new_m[n] for n in _WEIGHTS], *[new_v[n] for n in _WEIGHTS])
```

```python
import functools
import math

import numpy as np
import jax
import jax.numpy as jnp
from jax import lax
from jax.experimental import pallas as pl
from jax.experimental.pallas import tpu as pltpu

F32 = jnp.float32
BF16 = jnp.bfloat16

D_MODEL = 2048
CHUNK = 64
EPS = 1e-6
N_DEV = 8

RET_HEADS = 8
RET_DK = 256
RET_W = RET_HEADS * RET_DK
ROPE_THETA = 10000.0
ROPE_HALF = RET_DK // 2

SSD_W = 4096
SSD_P = 64
SSD_HEADS = 64
SSD_GROUPS = 8
SSD_N = 128
SSD_GW = SSD_W // SSD_GROUPS
SSD_QW = 256
SSD_CONV = 4
SSD_CD = SSD_W + 2 * SSD_GROUPS * SSD_N
HEAD_PAD = 128
B_OFF = SSD_W
C_OFF = SSD_W + SSD_GROUPS * SSD_N

ADAM_LR = 0.001
ADAM_B1 = 0.9
ADAM_B2 = 0.999
ADAM_EPS = 1e-08
ADAM_WD = 0.01
ADAM_STEP = 10

SPLITS = (RET_W, RET_W, RET_W, RET_W, SSD_W, SSD_CD, SSD_HEADS, D_MODEL, D_MODEL)
IN_PROJ = sum(SPLITS)
OFF_Z = 4 * RET_W
OFF_XBC = OFF_Z + SSD_W
OFF_DT = OFF_XBC + SSD_CD
OFF_G = OFF_DT + SSD_HEADS

ROW_TILE = 256
CONV_TILE = 128
MM_TILE = 1024
MM_TK = 2048

MESH = pl.DeviceIdType.MESH


def _dot(a, b):
    return lax.dot_general(a, b, (((1,), (0,)), ((), ())), preferred_element_type=F32)


def _dot_nt(a, b):
    return lax.dot_general(a, b, (((1,), (1,)), ((), ())), preferred_element_type=F32)


def _dot_tn(a, b):
    return lax.dot_general(a, b, (((0,), (0,)), ((), ())), preferred_element_type=F32)


def _bf(x):
    return x.astype(BF16)


def _split3(x):
    hi = x.astype(BF16)
    r = x - hi.astype(F32)
    mid = r.astype(BF16)
    lo = (r - mid.astype(F32)).astype(BF16)
    return hi, mid, lo


def _dot_exact_l(x, sel):
    hi, mid, lo = _split3(x)
    return _dot(hi, sel) + _dot(mid, sel) + _dot(lo, sel)


def _dot_exact_r(sel, x):
    hi, mid, lo = _split3(x)
    return _dot(sel, hi) + _dot(sel, mid) + _dot(sel, lo)


def _sigmoid(x):
    return 1.0 / (1.0 + jnp.exp(-x))


def _softplus(x):
    return jnp.maximum(x, 0.0) + jnp.log(1.0 + jnp.exp(-jnp.abs(x)))


def _iota(shape, axis):
    return lax.broadcasted_iota(jnp.int32, shape, axis)


def _params(*sem):
    return pltpu.CompilerParams(dimension_semantics=sem)


class _Rider:
    def __init__(self, inputs, out_shapes, n_sems, phases):
        self.inputs, self.out_shapes, self.n_sems, self.phases = tuple(inputs), tuple(out_shapes), n_sems, phases

    def in_specs(self):
        return [pl.BlockSpec(memory_space=pl.ANY)] * len(self.inputs)

    def out_specs(self):
        return [pl.BlockSpec(memory_space=pl.ANY)] * len(self.out_shapes)

    def scratch(self):
        return [pltpu.SemaphoreType.DMA((self.n_sems,)), pltpu.SemaphoreType.DMA((self.n_sems,))]

    def run(self, step, n_steps, ins, outs, send_sems, recv_sems, last):
        for frac, fn in self.phases:
            if (frac >= 1.0) != last:
                continue
            at = min(int(frac * n_steps), n_steps - 1)

            @pl.when(step == at)
            def _(fn=fn):
                fn(ins, outs, send_sems, recv_sems)


def _mesh_pos():
    return lax.axis_index("x"), lax.axis_index("y"), lax.axis_index("c")


def _gather_rider(shards, forward_at):
    n = len(shards)

    def tools(a, ins, outs, send_sems, recv_sems):
        x, y, cc = _mesh_pos()
        chips = [(1 - x, y), (x, 1 - y), (1 - x, 1 - y)]

        def slot(px, py, pc):
            return outs[a].at[4 * px + 2 * py + pc]

        def copy(k, block, to, src=None):
            return pltpu.make_async_remote_copy(
                src_ref=slot(*block) if src is None else src, dst_ref=slot(*block),
                send_sem=send_sems.at[8 * a + k], recv_sem=recv_sems.at[8 * a + k], device_id=to, device_id_type=MESH)

        me, sibling = (x, y, cc), (x, y, 1 - cc)
        return dict(
            mine=lambda: pltpu.make_async_copy(ins[a], slot(*me), send_sems.at[8 * a + 7]),
            first=lambda: [copy(0, me, sibling, src=ins[a])] + [copy(1 + j, me, (*chip, cc), src=ins[a])
                                                                for j, chip in enumerate(chips)],
            passed=lambda: [copy(4 + j, (*chip, cc), sibling) for j, chip in enumerate(chips)],
            landed=lambda: [copy(1 + j, (*chip, cc), me) for j, chip in enumerate(chips)],
            from_sibling=lambda: [copy(0, sibling, me)] + [copy(4 + j, (*chip, 1 - cc), me)
                                                           for j, chip in enumerate(chips)])

    def start(*refs):
        for a in range(n):
            t = tools(a, *refs)
            t["mine"]().start()
            for cp in t["first"]():
                cp.start()

    def forward(*refs):
        for a in range(n):
            t = tools(a, *refs)
            for got, cp in zip(t["landed"](), t["passed"]()):
                got.wait_recv()
                cp.start()

    def finish(*refs):
        for a in range(n):
            t = tools(a, *refs)
            for cp in t["from_sibling"]():
                cp.wait_recv()
            for cp in t["first"]() + t["passed"]():
                cp.wait_send()
            t["mine"]().wait()

    outs = [jax.ShapeDtypeStruct((N_DEV,) + s.shape, s.dtype) for s in shards]
    return _Rider(shards, outs, 8 * n, ((0.0, start), (forward_at, forward), (1.0, finish)))


def _scatter_rider(blocks):
    n = len(blocks)

    def copies(a, ins, outs, send_sems, recv_sems, with_back):
        x, y, cc = _mesh_pos()
        my_idx = 4 * x + 2 * y + cc
        mine = pltpu.make_async_copy(ins[a].at[my_idx], outs[a].at[my_idx], send_sems.at[8 * a + 7])
        out, back = [], []
        for k in range(1, N_DEV):
            px, py, pc = x ^ (k >> 2), y ^ ((k >> 1) & 1), cc ^ (k & 1)
            peer_idx = 4 * px + 2 * py + pc
            sems = dict(send_sem=send_sems.at[8 * a + k - 1], recv_sem=recv_sems.at[8 * a + k - 1],
                        device_id=(px, py, pc), device_id_type=MESH)
            out.append(pltpu.make_async_remote_copy(src_ref=ins[a].at[peer_idx], dst_ref=outs[a].at[my_idx], **sems))
            if with_back:
                back.append(pltpu.make_async_remote_copy(src_ref=ins[a].at[my_idx], dst_ref=outs[a].at[peer_idx], **sems))
        return mine, out, back

    def start(*refs):
        for a in range(n):
            mine, out, _ = copies(a, *refs, False)
            mine.start()
            for cp in out:
                cp.start()

    def finish(*refs):
        for a in range(n):
            mine, out, back = copies(a, *refs, True)
            for cp in back:
                cp.wait_recv()
            for cp in out:
                cp.wait_send()
            mine.wait()

    outs = [jax.ShapeDtypeStruct(b.shape, b.dtype) for b in blocks]
    return _Rider(blocks, outs, 8 * n, ((0.0, start), (1.0, finish)))


def _sibling_rider(blocks):
    _, r, c = blocks.shape

    def copies(ins, outs, send_sems, recv_sems):
        x, y, cc = _mesh_pos()
        return [pltpu.make_async_remote_copy(
            src_ref=ins[0].at[2 * j + 1 - cc], dst_ref=outs[0].at[j], send_sem=send_sems.at[j],
            recv_sem=recv_sems.at[j], device_id=(x, y, 1 - cc), device_id_type=MESH) for j in range(4)]

    def start(*refs):
        for cp in copies(*refs):
            cp.start()

    def finish(*refs):
        for cp in copies(*refs):
            cp.wait_recv()
        for cp in copies(*refs):
            cp.wait_send()

    return _Rider([blocks], [jax.ShapeDtypeStruct((4, r, c), blocks.dtype)], 4, ((0.0, start), (1.0, finish)))


def _chip_rider(partial, row0, rows):
    _, _, c = partial.shape

    def copies(ins, outs, send_sems, recv_sems, with_back):
        x, y, cc = _mesh_pos()
        my_chip = 2 * x + y

        def rows_of(j):
            return ins[0].at[j, pl.ds(row0, rows)]

        mine = pltpu.make_async_copy(rows_of(my_chip), outs[0].at[my_chip], send_sems.at[3])
        out, back = [], []
        for k in range(1, 4):
            px, py = x ^ (k >> 1), y ^ (k & 1)
            peer = 2 * px + py
            sems = dict(send_sem=send_sems.at[k - 1], recv_sem=recv_sems.at[k - 1], device_id=(px, py, cc),
                        device_id_type=MESH)
            out.append(pltpu.make_async_remote_copy(src_ref=rows_of(peer), dst_ref=outs[0].at[my_chip], **sems))
            if with_back:
                back.append(pltpu.make_async_remote_copy(src_ref=rows_of(my_chip), dst_ref=outs[0].at[peer], **sems))
        return mine, out, back

    def start(*refs):
        mine, out, _ = copies(*refs, False)
        mine.start()
        for cp in out:
            cp.start()

    def finish(*refs):
        mine, out, back = copies(*refs, True)
        for cp in back:
            cp.wait_recv()
        for cp in out:
            cp.wait_send()
        mine.wait()

    return _Rider([partial], [jax.ShapeDtypeStruct((4, rows, c), partial.dtype)], 4, ((0.0, start), (1.0, finish)))


def _matmul(a, b, mode, out_dtype, name, add=None, rider=None):
    if mode == "nn":
        (m, k), (k2, n) = a.shape, b.shape
    elif mode == "nt":
        (m, k), (n, k2) = a.shape, b.shape
    else:
        (k, m), (k2, n) = a.shape, b.shape
    assert k == k2, (a.shape, b.shape, mode)
    tm, tn, tk = min(m, MM_TILE), min(n, MM_TILE), min(k, MM_TK)
    assert m % tm == 0 and n % tn == 0 and k % tk == 0, (m, n, k)
    nk = k // tk
    dot = {"nn": _dot, "nt": _dot_nt, "tn": _dot_tn}[mode]

    gm, gn = m // tm, n // tn
    n_rin = len(rider.inputs) if rider else 0
    n_rout = len(rider.out_shapes) if rider else 0
    n_add = 1 if add is not None else 0

    def body(*refs):
        a_ref, b_ref = refs[:2]
        add_ref = refs[2] if n_add else None
        r_ins = refs[2 + n_add:2 + n_add + n_rin]
        o_ref = refs[2 + n_add + n_rin]
        r_outs = refs[3 + n_add + n_rin:3 + n_add + n_rin + n_rout]
        scratch = refs[3 + n_add + n_rin + n_rout:]
        acc_ref = scratch[0] if nk > 1 else None
        kk = pl.program_id(2)
        if rider:
            step = (pl.program_id(0) * gn + pl.program_id(1)) * nk + kk
            ride = functools.partial(rider.run, step, gm * gn * nk, r_ins, r_outs, scratch[-2], scratch[-1])
            ride(last=False)

        def finish(r):
            if add_ref is not None:
                r = r + add_ref[...]
            o_ref[...] = r.astype(o_ref.dtype)

        if nk == 1:
            finish(dot(_bf(a_ref[...]), _bf(b_ref[...])))
        else:
            @pl.when(kk == 0)
            def _():
                acc_ref[...] = dot(_bf(a_ref[...]), _bf(b_ref[...]))

            @pl.when(jnp.logical_and(kk > 0, kk < nk - 1))
            def _():
                acc_ref[...] += dot(_bf(a_ref[...]), _bf(b_ref[...]))

            @pl.when(kk == nk - 1)
            def _():
                finish(acc_ref[...] + dot(_bf(a_ref[...]), _bf(b_ref[...])))
        if rider:
            ride(last=True)

    if mode == "nn":
        a_spec = pl.BlockSpec((tm, tk), lambda i, j, kk: (i, kk))
        b_spec = pl.BlockSpec((tk, tn), lambda i, j, kk: (kk, j))
    elif mode == "nt":
        a_spec = pl.BlockSpec((tm, tk), lambda i, j, kk: (i, kk))
        b_spec = pl.BlockSpec((tn, tk), lambda i, j, kk: (j, kk))
    else:
        a_spec = pl.BlockSpec((tk, tm), lambda i, j, kk: (kk, i))
        b_spec = pl.BlockSpec((tk, tn), lambda i, j, kk: (kk, j))
    o_spec = pl.BlockSpec((tm, tn), lambda i, j, kk: (i, j))
    in_specs = [a_spec, b_spec] + ([o_spec] if add is not None else [])
    args = (a, b) + ((add,) if add is not None else ())
    block_bytes = (tm * tk * a.dtype.itemsize + tk * tn * b.dtype.itemsize
                   + tm * tn * (jnp.dtype(out_dtype).itemsize + (4 if add is not None else 0)))
    vmem = 2 * block_bytes + 2 * tm * tn * 4 + 2 * (tm + tn) * tk + (4 << 20)
    out_shape = jax.ShapeDtypeStruct((m, n), out_dtype)
    scratch = [pltpu.VMEM((tm, tn), F32)] if nk > 1 else []
    if rider:
        in_specs = in_specs + rider.in_specs()
        args = args + rider.inputs
        out_shape = (out_shape, *rider.out_shapes)
        o_spec = (o_spec, *rider.out_specs())
        scratch = scratch + rider.scratch()
    sem = ("arbitrary",) * 3 if rider else ("parallel", "parallel", "arbitrary")
    return pl.pallas_call(
        body,
        out_shape=out_shape,
        grid=(gm, gn, nk),
        in_specs=in_specs,
        out_specs=o_spec,
        scratch_shapes=scratch,
        compiler_params=pltpu.CompilerParams(dimension_semantics=sem, vmem_limit_bytes=int(vmem)),
        name=name,
    )(*args)


def _row_spec(width, tile=ROW_TILE):
    return pl.BlockSpec((tile, width), lambda i: (i, 0))


def _full_spec(shape):
    nd = len(shape)
    return pl.BlockSpec(shape, lambda *_: (0,) * nd)


def _cast_bf16(x, name):
    r, c = x.shape
    tr = min(r, ROW_TILE)

    def body(x_ref, o_ref):
        o_ref[...] = _bf(x_ref[...])

    return pl.pallas_call(
        body, out_shape=jax.ShapeDtypeStruct((r, c), BF16), grid=(r // tr,),
        in_specs=[_row_spec(c, tr)], out_specs=_row_spec(c, tr),
        compiler_params=_params("parallel"), name=name)(x)


def _rmsnorm_fwd(x, w, name):
    s, d = x.shape

    def body(x_ref, w_ref, o_ref):
        xv = x_ref[...]
        rstd = lax.rsqrt(jnp.mean(xv * xv, axis=1, keepdims=True) + EPS)
        o_ref[...] = _bf(xv * rstd * w_ref[...])

    return pl.pallas_call(
        body, out_shape=jax.ShapeDtypeStruct((s, d), BF16), grid=(s // ROW_TILE,),
        in_specs=[_row_spec(d), _full_spec((1, d))], out_specs=_row_spec(d),
        compiler_params=_params("parallel"), name=name)(x, w)


def _rmsnorm_bwd(x, w, dh, dres, name):
    s, d = x.shape

    def body(x_ref, w_ref, dh_ref, dres_ref, dx_ref, dw_ref):
        @pl.when(pl.program_id(0) == 0)
        def _():
            dw_ref[...] = jnp.zeros_like(dw_ref)

        xv = x_ref[...]
        rstd = lax.rsqrt(jnp.mean(xv * xv, axis=1, keepdims=True) + EPS)
        xhat = xv * rstd
        dhv = dh_ref[...]
        dxhat = dhv * w_ref[...]
        dx = rstd * (dxhat - xhat * jnp.mean(dxhat * xhat, axis=1, keepdims=True))
        dx_ref[...] = dx + dres_ref[...]
        dw_ref[...] += jnp.sum(dhv * xhat, axis=0, keepdims=True)

    return pl.pallas_call(
        body,
        out_shape=(jax.ShapeDtypeStruct((s, d), F32), jax.ShapeDtypeStruct((1, d), F32)),
        grid=(s // ROW_TILE,),
        in_specs=[_row_spec(d), _full_spec((1, d)), _row_spec(d), _row_spec(d)],
        out_specs=(_row_spec(d), _full_spec((1, d))),
        compiler_params=_params("arbitrary"), name=name)(x, w, dh, dres)


def _rope_tables(pos_col, inv_freq, name):
    s = pos_col.shape[0]

    def body(p_ref, f_ref, cos_ref, sin_ref):
        ang = p_ref[...].astype(F32) * f_ref[...]
        cos_ref[...] = jnp.cos(ang)
        sin_ref[...] = jnp.sin(ang)

    out = jax.ShapeDtypeStruct((s, ROPE_HALF), F32)
    return pl.pallas_call(
        body, out_shape=(out, out), grid=(s // ROW_TILE,),
        in_specs=[_row_spec(1), _full_spec((1, ROPE_HALF))],
        out_specs=(_row_spec(ROPE_HALF), _row_spec(ROPE_HALF)),
        compiler_params=_params("parallel"), name=name)(pos_col, inv_freq)


def _merge_fwd(pg, p_r, p_s, name):
    s = pg.shape[0]

    def body(g_ref, r_ref, s_ref, o_ref):
        g = g_ref[...]
        o_ref[...] = _bf(_sigmoid(g[:, :D_MODEL]) * r_ref[...] + _sigmoid(g[:, D_MODEL:]) * s_ref[...])

    return pl.pallas_call(
        body, out_shape=jax.ShapeDtypeStruct((s, D_MODEL), BF16), grid=(s // ROW_TILE,),
        in_specs=[_row_spec(2 * D_MODEL), _row_spec(D_MODEL), _row_spec(D_MODEL)],
        out_specs=_row_spec(D_MODEL), compiler_params=_params("parallel"), name=name)(pg, p_r, p_s)


def _merge_bwd(pg, p_r, p_s, dm, name):
    s = pg.shape[0]

    def body(g_ref, r_ref, s_ref, dm_ref, dr_ref, ds_ref, dg_ref):
        g = g_ref[...]
        sr, ss = _sigmoid(g[:, :D_MODEL]), _sigmoid(g[:, D_MODEL:])
        d = dm_ref[...]
        dr_ref[...] = _bf(d * sr)
        ds_ref[...] = _bf(d * ss)
        dg_ref[:, :D_MODEL] = _bf(d * r_ref[...] * sr * (1.0 - sr))
        dg_ref[:, D_MODEL:] = _bf(d * s_ref[...] * ss * (1.0 - ss))

    o = jax.ShapeDtypeStruct((s, D_MODEL), BF16)
    return pl.pallas_call(
        body, out_shape=(o, o, jax.ShapeDtypeStruct((s, 2 * D_MODEL), BF16)), grid=(s // ROW_TILE,),
        in_specs=[_row_spec(2 * D_MODEL), _row_spec(D_MODEL), _row_spec(D_MODEL), _row_spec(D_MODEL)],
        out_specs=(_row_spec(D_MODEL), _row_spec(D_MODEL), _row_spec(2 * D_MODEL)),
        compiler_params=_params("parallel"), name=name)(pg, p_r, p_s, dm)


def _final_fwd_bwd(x, o, w, target, name):
    s, d = x.shape

    def body(x_ref, o_ref, w_ref, t_ref, dx_ref, dw_ref, loss_ref):
        @pl.when(pl.program_id(0) == 0)
        def _():
            dw_ref[...] = jnp.zeros_like(dw_ref)
            loss_ref[...] = jnp.zeros_like(loss_ref)

        x2 = x_ref[...] + o_ref[...]
        rstd = lax.rsqrt(jnp.mean(x2 * x2, axis=1, keepdims=True) + EPS)
        xhat = x2 * rstd
        wv = w_ref[...]
        err = xhat * wv - t_ref[...]
        loss_ref[...] += jnp.sum(jnp.sum(err * err, axis=1, keepdims=True), axis=0, keepdims=True) * (0.5 / d)
        dy = err * (1.0 / d)
        dw_ref[...] += jnp.sum(dy * xhat, axis=0, keepdims=True)
        dxhat = dy * wv
        dx_ref[...] = rstd * (dxhat - xhat * jnp.mean(dxhat * xhat, axis=1, keepdims=True))

    return pl.pallas_call(
        body,
        out_shape=(jax.ShapeDtypeStruct((s, d), F32), jax.ShapeDtypeStruct((1, d), F32),
                   jax.ShapeDtypeStruct((8, 128), F32)),
        grid=(s // ROW_TILE,),
        in_specs=[_row_spec(d), _row_spec(d), _full_spec((1, d)), _row_spec(d)],
        out_specs=(_row_spec(d), _full_spec((1, d)), _full_spec((8, 128))),
        compiler_params=_params("arbitrary"), name=name)(x, o, w, target)


def _adamw(w, g, m, v, name):
    r, c = w.shape
    tr = min(r, ROW_TILE)
    c1 = 1.0 / (1.0 - ADAM_B1 ** ADAM_STEP)
    c2 = 1.0 / (1.0 - ADAM_B2 ** ADAM_STEP)

    def body(w_ref, g_ref, m_ref, v_ref, d_ref, nm_ref, nv_ref):
        gv = g_ref[...]
        nm = ADAM_B1 * m_ref[...] + (1.0 - ADAM_B1) * gv
        nv = ADAM_B2 * v_ref[...] + (1.0 - ADAM_B2) * (gv * gv)
        d_ref[...] = -ADAM_LR * ((nm * c1) / (jnp.sqrt(nv * c2) + ADAM_EPS) + ADAM_WD * w_ref[...])
        nm_ref[...] = nm
        nv_ref[...] = nv

    o = jax.ShapeDtypeStruct((r, c), F32)
    sp = _row_spec(c, tr)
    return pl.pallas_call(
        body, out_shape=(o, o, o), grid=(r // tr,), in_specs=[sp, sp, sp, sp], out_specs=(sp, sp, sp),
        compiler_params=_params("parallel"), name=name)(w, g, m, v)


def _sum_slots(land, name):
    n, r, c = land.shape
    tr = math.gcd(r, ROW_TILE)

    def body(l_ref, o_ref):
        acc = l_ref[0].astype(F32)
        for i in range(1, n):
            acc = acc + l_ref[i].astype(F32)
        o_ref[...] = acc

    return pl.pallas_call(
        body, out_shape=jax.ShapeDtypeStruct((r, c), F32), grid=(r // tr,),
        in_specs=[pl.BlockSpec((n, tr, c), lambda i: (0, i, 0))], out_specs=_row_spec(c, tr),
        compiler_params=_params("parallel"), name=name)(land)


def _add_bf16(a, b, name):
    r, c = a.shape

    def body(a_ref, b_ref, o_ref):
        o_ref[...] = _bf(a_ref[...].astype(F32) + b_ref[...].astype(F32))

    sp = _row_spec(c)
    return pl.pallas_call(
        body, out_shape=jax.ShapeDtypeStruct((r, c), BF16), grid=(r // ROW_TILE,), in_specs=[sp, sp], out_specs=sp,
        compiler_params=_params("parallel"), name=name)(a, b)


def _retention_tables():
    lg = np.log1p(-(2.0 ** (-5.0 - np.arange(RET_HEADS, dtype=np.float64))))
    idx = np.arange(CHUNK, dtype=np.float64)
    intra = np.exp(np.abs(idx[:, None] - idx[None, :])[None] * lg[:, None, None])
    qd = np.exp((idx[None, :] + 1.0) * lg[:, None])
    kd = np.exp((CHUNK - 1.0 - idx[None, :]) * lg[:, None])
    cd = np.exp(CHUNK * lg)
    ones = np.ones((1, 1, RET_DK))
    return (jnp.asarray(intra, F32), jnp.asarray(qd[:, :, None] * ones, F32),
            jnp.asarray(kd[:, :, None] * ones, F32), jnp.asarray(cd[:, None, None] * ones, F32))


def _rope(t, cos, sin):
    t1, t2 = t[:, :ROPE_HALF], t[:, ROPE_HALF:]
    return jnp.concatenate([t1 * cos - t2 * sin, t2 * cos + t1 * sin], axis=1)


def _rope_t(d, cos, sin):
    d1, d2 = d[:, :ROPE_HALF], d[:, ROPE_HALF:]
    return jnp.concatenate([d1 * cos + d2 * sin, d2 * cos - d1 * sin], axis=1)


def _ret_head_fwd(q, k, v, cos, sin, intra, qd, st):
    qr = _rope(q, cos, sin)
    kr = _rope(k, cos, sin) * (RET_DK ** -0.5)
    sc = _dot_nt(_bf(qr), _bf(kr)) * intra
    y = _dot(_bf(sc), _bf(v)) + _dot(_bf(qr), _bf(st)) * qd
    return qr, kr, sc, y


def _ret_specs(nc, rev):
    cidx = (lambda c: nc - 1 - c) if rev else (lambda c: c)
    return dict(
        proj=pl.BlockSpec((CHUNK, 4 * RET_W), lambda c: (cidx(c), 0)),
        half=pl.BlockSpec((CHUNK, ROPE_HALF), lambda c: (cidx(c), 0)),
        wide=pl.BlockSpec((CHUNK, RET_W), lambda c: (cidx(c), 0)),
        state=pl.BlockSpec((1, RET_HEADS, RET_DK, RET_DK), lambda c: (cidx(c), 0, 0, 0)),
        intra=_full_spec((RET_HEADS, CHUNK, CHUNK)),
        dec=_full_spec((RET_HEADS, CHUNK, RET_DK)),
        cd=_full_spec((RET_HEADS, 1, RET_DK)),
    )


def _retention_fwd(p_r, cos, sin, name):
    s = p_r.shape[0]
    nc = s // CHUNK
    intra_t, qd_t, kd_t, cd_t = _retention_tables()

    def body(p_ref, cos_ref, sin_ref, intra_ref, qd_ref, kd_ref, cd_ref, y_ref, st_ref, state):
        @pl.when(pl.program_id(0) == 0)
        def _():
            state[...] = jnp.zeros_like(state)

        cos, sin = cos_ref[...], sin_ref[...]
        for h in range(RET_HEADS):
            lo = h * RET_DK
            q = p_ref[:, lo:lo + RET_DK]
            k = p_ref[:, RET_W + lo:RET_W + lo + RET_DK]
            v = p_ref[:, 2 * RET_W + lo:2 * RET_W + lo + RET_DK]
            g = p_ref[:, 3 * RET_W + lo:3 * RET_W + lo + RET_DK]
            st = state[h]
            st_ref[0, h] = st
            _, kr, _, y = _ret_head_fwd(q, k, v, cos, sin, intra_ref[h], qd_ref[h], st)
            state[h] = st * cd_ref[h] + _dot_tn(_bf(kr * kd_ref[h]), _bf(v))
            mu = jnp.mean(y, axis=1, keepdims=True)
            yc = y - mu
            rstd = lax.rsqrt(jnp.mean(yc * yc, axis=1, keepdims=True) + EPS)
            y_ref[:, lo:lo + RET_DK] = _bf(yc * rstd * (g * _sigmoid(g)))

    sp = _ret_specs(nc, False)
    return pl.pallas_call(
        body,
        out_shape=(jax.ShapeDtypeStruct((s, RET_W), BF16),
                   jax.ShapeDtypeStruct((nc, RET_HEADS, RET_DK, RET_DK), F32)),
        grid=(nc,),
        in_specs=[sp["proj"], sp["half"], sp["half"], sp["intra"], sp["dec"], sp["dec"], sp["cd"]],
        out_specs=(sp["wide"], sp["state"]),
        scratch_shapes=[pltpu.VMEM((RET_HEADS, RET_DK, RET_DK), F32)],
        compiler_params=_params("arbitrary"), name=name)(p_r, cos, sin, intra_t, qd_t, kd_t, cd_t)


def _retention_bwd(p_r, cos, sin, states, dy_r, name):
    s = p_r.shape[0]
    nc = s // CHUNK
    intra_t, qd_t, kd_t, cd_t = _retention_tables()

    def body(p_ref, cos_ref, sin_ref, intra_ref, qd_ref, kd_ref, cd_ref, st_ref, dy_ref, dp_ref, dstate):
        @pl.when(pl.program_id(0) == 0)
        def _():
            dstate[...] = jnp.zeros_like(dstate)

        cos, sin = cos_ref[...], sin_ref[...]
        for h in range(RET_HEADS):
            lo = h * RET_DK
            q = p_ref[:, lo:lo + RET_DK]
            k = p_ref[:, RET_W + lo:RET_W + lo + RET_DK]
            v = p_ref[:, 2 * RET_W + lo:2 * RET_W + lo + RET_DK]
            g = p_ref[:, 3 * RET_W + lo:3 * RET_W + lo + RET_DK]
            st = st_ref[0, h]
            intra, qd, kd = intra_ref[h], qd_ref[h], kd_ref[h]
            qr, kr, sc, y = _ret_head_fwd(q, k, v, cos, sin, intra, qd, st)
            mu = jnp.mean(y, axis=1, keepdims=True)
            yc = y - mu
            rstd = lax.rsqrt(jnp.mean(yc * yc, axis=1, keepdims=True) + EPS)
            yn = yc * rstd
            sg = _sigmoid(g)
            dyr = dy_ref[:, lo:lo + RET_DK]
            dyn = dyr * (g * sg)
            dg = dyr * yn * (sg * (1.0 + g * (1.0 - sg)))
            dy = rstd * (dyn - jnp.mean(dyn, axis=1, keepdims=True)
                         - yn * jnp.mean(dyn * yn, axis=1, keepdims=True))
            dyb, vb, qrb = _bf(dy), _bf(v), _bf(qr)
            dsn = dstate[h]
            dsnb = _bf(dsn)
            ds = _bf(_dot_nt(dyb, vb) * intra)
            t = _bf(dy * qd)
            dv = _dot_tn(_bf(sc), dyb) + _dot(_bf(kr * kd), dsnb)
            dqr = _dot(ds, _bf(kr)) + _dot_nt(t, _bf(st))
            dkr = _dot_tn(ds, qrb) + _dot_nt(vb, dsnb) * kd
            dstate[h] = dsn * cd_ref[h] + _dot_tn(qrb, t)
            dp_ref[:, lo:lo + RET_DK] = _bf(_rope_t(dqr, cos, sin))
            dp_ref[:, RET_W + lo:RET_W + lo + RET_DK] = _bf(_rope_t(dkr, cos, sin) * (RET_DK ** -0.5))
            dp_ref[:, 2 * RET_W + lo:2 * RET_W + lo + RET_DK] = _bf(dv)
            dp_ref[:, 3 * RET_W + lo:3 * RET_W + lo + RET_DK] = _bf(dg)

    sp = _ret_specs(nc, True)
    return pl.pallas_call(
        body,
        out_shape=jax.ShapeDtypeStruct((s, 4 * RET_W), BF16),
        grid=(nc,),
        in_specs=[sp["proj"], sp["half"], sp["half"], sp["intra"], sp["dec"], sp["dec"], sp["cd"],
                  sp["state"], sp["wide"]],
        out_specs=sp["proj"],
        scratch_shapes=[pltpu.VMEM((RET_HEADS, RET_DK, RET_DK), F32)],
        compiler_params=_params("arbitrary"), name=name)(p_r, cos, sin, intra_t, qd_t, kd_t, cd_t, states, dy_r)


def _conv_taps(ext, w):
    acc = w[SSD_CONV - 1:SSD_CONV] * ext
    for j in range(SSD_CONV - 1):
        acc = acc + w[j:j + 1] * pltpu.roll(ext, SSD_CONV - 1 - j, axis=0)
    return acc


def _conv_fwd(xbc_raw, conv_w, conv_b, name):
    s = xbc_raw.shape[0]
    t8 = CONV_TILE // 8

    def body(cur_ref, prev_ref, w_ref, b_ref, o_ref):
        keep = (pl.program_id(0) > 0).astype(F32)
        ext = jnp.concatenate([prev_ref[...] * keep, cur_ref[...]], axis=0)
        u = _conv_taps(ext, w_ref[...])[8:] + b_ref[...]
        o_ref[...] = u * _sigmoid(u)

    return pl.pallas_call(
        body, out_shape=jax.ShapeDtypeStruct((s, SSD_CD), F32), grid=(s // CONV_TILE,),
        in_specs=[_row_spec(SSD_CD, CONV_TILE),
                  pl.BlockSpec((8, SSD_CD), lambda i: (jnp.maximum(i * t8 - 1, 0), 0)),
                  _full_spec((SSD_CONV, SSD_CD)), _full_spec((1, SSD_CD))],
        out_specs=_row_spec(SSD_CD, CONV_TILE),
        compiler_params=_params("parallel"), name=name)(xbc_raw, xbc_raw, conv_w, conv_b)


def _conv_bwd(xbc_raw, conv_w, conv_b, dact, name):
    s = xbc_raw.shape[0]
    nt = s // CONV_TILE
    t8 = CONV_TILE // 8
    rows = CONV_TILE + 8

    def body(cur_ref, prev_ref, next_ref, w_ref, b_ref, d_ref, dnext_ref, dx_ref, dw_ref, db_ref):
        i = pl.program_id(0)

        @pl.when(i == 0)
        def _():
            dw_ref[...] = jnp.zeros_like(dw_ref)
            db_ref[...] = jnp.zeros_like(db_ref)

        keep_prev = (i > 0).astype(F32)
        keep_next = (i < nt - 1).astype(F32)
        w = w_ref[...]
        ext = jnp.concatenate([prev_ref[...] * keep_prev, cur_ref[...], next_ref[...]], axis=0)
        shifted = [pltpu.roll(ext, SSD_CONV - 1 - j, axis=0)[8:] for j in range(SSD_CONV - 1)] + [ext[8:]]
        u = b_ref[...]
        for j in range(SSD_CONV):
            u = u + w[j:j + 1] * shifted[j]
        sg = _sigmoid(u)
        dact = jnp.concatenate([d_ref[...], dnext_ref[...] * keep_next], axis=0)
        du = dact * (sg * (1.0 + u * (1.0 - sg)))
        dx = w[SSD_CONV - 1:SSD_CONV] * du
        for j in range(SSD_CONV - 1):
            dx = dx + w[j:j + 1] * pltpu.roll(du, rows - (SSD_CONV - 1 - j), axis=0)
        dx_ref[...] = _bf(dx[:CONV_TILE])
        duc = du[:CONV_TILE]
        for j in range(SSD_CONV):
            dw_ref[j:j + 1, :] += jnp.sum(duc * shifted[j][:CONV_TILE], axis=0, keepdims=True)
        db_ref[...] += jnp.sum(duc, axis=0, keepdims=True)

    row = _row_spec(SSD_CD, CONV_TILE)
    prev = pl.BlockSpec((8, SSD_CD), lambda i: (jnp.maximum(i * t8 - 1, 0), 0))
    nxt = pl.BlockSpec((8, SSD_CD), lambda i: (jnp.minimum((i + 1) * t8, s // 8 - 1), 0))
    return pl.pallas_call(
        body,
        out_shape=(jax.ShapeDtypeStruct((s, SSD_CD), BF16), jax.ShapeDtypeStruct((SSD_CONV, SSD_CD), F32),
                   jax.ShapeDtypeStruct((1, SSD_CD), F32)),
        grid=(nt,),
        in_specs=[row, prev, nxt, _full_spec((SSD_CONV, SSD_CD)), _full_spec((1, SSD_CD)), row, nxt],
        out_specs=(row, _full_spec((SSD_CONV, SSD_CD)), _full_spec((1, SSD_CD))),
        compiler_params=_params("arbitrary"), name=name)(xbc_raw, xbc_raw, xbc_raw, conv_w, conv_b, dact, dact)


def _head_select():
    e = np.zeros((HEAD_PAD, SSD_W), np.float32)
    for h in range(SSD_HEADS):
        e[h, h * SSD_P:(h + 1) * SSD_P] = 1.0
    return jnp.asarray(e, BF16), jnp.asarray(e.T, BF16)


def _pad_heads(v):
    return jnp.pad(v.reshape(1, SSD_HEADS).astype(F32), ((0, 0), (0, HEAD_PAD - SSD_HEADS)))


def _ssd_masks():
    r = _iota((CHUNK, SSD_QW), 0)
    c = _iota((CHUNK, SSD_QW), 1) % SSD_P
    itile = (r == c).astype(F32)
    ctile = r >= c
    rb = _iota((SSD_QW, SSD_QW), 0) // SSD_P
    cb = _iota((SSD_QW, SSD_QW), 1) // SSD_P
    return itile, ctile, rb == cb


def _ssd_prep(xbc_ref, dtr_ref, dtb_ref, alog_ref, e_ref):
    xs = xbc_ref[:, :SSD_W]
    u = dtr_ref[...] + dtb_ref[...]
    dt = _softplus(u)
    nexp = -jnp.exp(alog_ref[...])
    a = dt * nexp
    tril = _bf((_iota((CHUNK, CHUNK), 0) >= _iota((CHUNK, CHUNK), 1)).astype(F32))
    acum = _dot_exact_r(tril, a)
    e = e_ref[...]
    acol = _dot_exact_l(acum, e)
    dte = _dot_exact_l(dt, e)
    alast = acol[CHUNK - 1:CHUNK, :]
    return dict(xs=xs, u=u, dt=dt, nexp=nexp, a=a, acol=acol, dte=dte, xdt=xs * dte,
                ea=jnp.exp(acol), tail=jnp.exp(alast - acol), eal=jnp.exp(alast))


def _tile4(x):
    return jnp.concatenate([x, x, x, x], axis=0)


def _fold4(x):
    return x[0:CHUNK] + x[CHUNK:2 * CHUNK] + x[2 * CHUNK:3 * CHUNK] + x[3 * CHUNK:4 * CHUNK]


def _ssd_specs(nc, rev):
    cidx = (lambda c: nc - 1 - c) if rev else (lambda c: c)
    return dict(
        xbc=pl.BlockSpec((CHUNK, SSD_CD), lambda c: (cidx(c), 0)),
        dt=pl.BlockSpec((CHUNK, HEAD_PAD), lambda c: (cidx(c), 0)),
        wide=pl.BlockSpec((CHUNK, SSD_W), lambda c: (cidx(c), 0)),
        state=pl.BlockSpec((1, SSD_N, SSD_W), lambda c: (cidx(c), 0, 0)),
        head=_full_spec((1, HEAD_PAD)),
        roww=_full_spec((1, SSD_W)),
        e=_full_spec((HEAD_PAD, SSD_W)),
        et=_full_spec((SSD_W, HEAD_PAD)),
    )


def _ssd_fwd(xbc, dt_raw, z, dt_bias, a_log, d_skip, norm_w, name):
    s = xbc.shape[0]
    nc = s // CHUNK
    e_sel, _ = _head_select()

    def body(xbc_ref, dtr_ref, z_ref, dtb_ref, alog_ref, dsk_ref, nw_ref, e_ref,
             yraw_ref, ys_ref, st_ref, state):
        @pl.when(pl.program_id(0) == 0)
        def _():
            state[...] = jnp.zeros_like(state)

        p = _ssd_prep(xbc_ref, dtr_ref, dtb_ref, alog_ref, e_ref)
        itile, ctile, bdmask = _ssd_masks()
        dske = _dot_exact_l(jnp.broadcast_to(dsk_ref[...], (8, HEAD_PAD)), e_ref[...])[0:1]
        st_ref[0] = state[...]
        for g in range(SSD_GROUPS):
            gs = slice(g * SSD_GW, (g + 1) * SSD_GW)
            bg = _bf(xbc_ref[:, B_OFF + g * SSD_N:B_OFF + (g + 1) * SSD_N])
            cg = _bf(xbc_ref[:, C_OFF + g * SSD_N:C_OFF + (g + 1) * SSD_N])
            cbt = _dot_nt(cg, _tile4(bg))
            stg = state[:, gs]
            ys = _dot(cg, _bf(stg))
            for qd in range(SSD_GW // SSD_QW):
                ql = slice(g * SSD_GW + qd * SSD_QW, g * SSD_GW + (qd + 1) * SSD_QW)
                aq = p["acol"][:, ql]
                arow = jnp.sum(aq * itile, axis=0, keepdims=True)
                dq = jnp.exp(jnp.where(ctile, aq - arow, -jnp.inf))
                xbd = jnp.where(bdmask, _tile4(_bf(p["xdt"][:, ql])), jnp.zeros((), BF16))
                yq = _dot(_bf(cbt * dq), xbd) + ys[:, qd * SSD_QW:(qd + 1) * SSD_QW] * p["ea"][:, ql]
                yraw_ref[:, ql] = yq + dske[:, ql] * p["xs"][:, ql]
            wt = _bf(p["xdt"][:, gs] * p["tail"][:, gs])
            state[:, gs] = stg * p["eal"][:, gs] + _dot_tn(bg, wt)
        zv = z_ref[...]
        t = yraw_ref[...] * (zv * _sigmoid(zv))
        rstd = lax.rsqrt(jnp.mean(t * t, axis=1, keepdims=True) + EPS)
        ys_ref[...] = _bf(t * rstd * nw_ref[...])

    sp = _ssd_specs(nc, False)
    return pl.pallas_call(
        body,
        out_shape=(jax.ShapeDtypeStruct((s, SSD_W), F32), jax.ShapeDtypeStruct((s, SSD_W), BF16),
                   jax.ShapeDtypeStruct((nc, SSD_N, SSD_W), F32)),
        grid=(nc,),
        in_specs=[sp["xbc"], sp["dt"], sp["wide"], sp["head"], sp["head"], sp["head"], sp["roww"], sp["e"]],
        out_specs=(sp["wide"], sp["wide"], sp["state"]),
        scratch_shapes=[pltpu.VMEM((SSD_N, SSD_W), F32)],
        compiler_params=_params("arbitrary"), name=name,
    )(xbc, dt_raw, z, dt_bias, a_log, d_skip, norm_w, e_sel)


def _ssd_bwd(xbc, dt_raw, z, yraw, states, dys, dt_bias, a_log, d_skip, norm_w, name, rider):
    s = xbc.shape[0]
    nc = s // CHUNK
    e_sel, et_sel = _head_select()
    n_in, n_out, n_scratch = 12, 7, 3
    n_rin, n_rout = len(rider.inputs), len(rider.out_shapes)

    def body(*refs):
        ins, refs = refs[:n_in], refs[n_in:]
        r_ins, refs = refs[:n_rin], refs[n_rin:]
        outs, refs = refs[:n_out], refs[n_out:]
        r_outs, refs = refs[:n_rout], refs[n_rout:]
        ride = functools.partial(rider.run, pl.program_id(0), nc, r_ins, r_outs, refs[n_scratch], refs[n_scratch + 1])
        ride(last=False)
        compute(*ins, *outs, *refs[:n_scratch])
        ride(last=True)

    def compute(xbc_ref, dtr_ref, z_ref, yraw_ref, st_ref, dys_ref, dtb_ref, alog_ref, dsk_ref, nw_ref,
                e_ref, et_ref, dxbc_ref, dz_ref, ddtr_ref, ddtb_ref, dalog_ref, ddsk_ref, dnw_ref,
                dstate, dacol_s, ddte_s):
        @pl.when(pl.program_id(0) == 0)
        def _():
            dstate[...] = jnp.zeros_like(dstate)
            ddtb_ref[...] = jnp.zeros_like(ddtb_ref)
            dalog_ref[...] = jnp.zeros_like(dalog_ref)
            ddsk_ref[...] = jnp.zeros_like(ddsk_ref)
            dnw_ref[...] = jnp.zeros_like(dnw_ref)

        p = _ssd_prep(xbc_ref, dtr_ref, dtb_ref, alog_ref, e_ref)
        itile, ctile, bdmask = _ssd_masks()
        et = et_ref[...]
        dske = _dot_exact_l(jnp.broadcast_to(dsk_ref[...], (8, HEAD_PAD)), e_ref[...])[0:1]
        last_row = (_iota((CHUNK, 1), 0) == CHUNK - 1).astype(F32)

        zv, y, dysv = z_ref[...], yraw_ref[...], dys_ref[...]
        sz = _sigmoid(zv)
        silz = zv * sz
        t = y * silz
        rstd = lax.rsqrt(jnp.mean(t * t, axis=1, keepdims=True) + EPS)
        that = t * rstd
        dthat = dysv * nw_ref[...]
        dt_ = rstd * (dthat - that * jnp.mean(dthat * that, axis=1, keepdims=True))
        dnw_ref[...] += jnp.sum(dysv * that, axis=0, keepdims=True)
        dz_ref[...] = _bf(dt_ * y * (sz * (1.0 + zv * (1.0 - sz))))
        dy = dt_ * silz
        dsk_row = jnp.sum(dy * p["xs"], axis=0, keepdims=True)
        ddsk_ref[...] += _dot_exact_l(jnp.broadcast_to(dsk_row, (8, SSD_W)), et)[0:1]

        for g in range(SSD_GROUPS):
            gs = slice(g * SSD_GW, (g + 1) * SSD_GW)
            bg = _bf(xbc_ref[:, B_OFF + g * SSD_N:B_OFF + (g + 1) * SSD_N])
            cg = _bf(xbc_ref[:, C_OFF + g * SSD_N:C_OFF + (g + 1) * SSD_N])
            btile = _tile4(bg)
            cbt = _dot_nt(cg, btile)
            stg = st_ref[0, :, gs]
            stgb = _bf(stg)
            dyg = dy[:, gs]
            eag, tailg, xdtg, ealg = p["ea"][:, gs], p["tail"][:, gs], p["xdt"][:, gs], p["eal"][:, gs]
            dys_g = _bf(dyg * eag)
            dacol_g = dyg * (_dot(cg, stgb) * eag)
            dc = _dot_nt(dys_g, stgb)
            dst = _dot_tn(cg, dys_g)
            dsn = dstate[:, gs]
            dsnb = _bf(dsn)
            dwt = _dot(bg, dsnb)
            db = _dot_nt(_bf(xdtg * tailg), dsnb)
            dtl = dwt * xdtg * tailg
            dacol_g = dacol_g - dtl
            dal_row = jnp.sum(dtl, axis=0, keepdims=True) + jnp.sum(dsn * stg, axis=0, keepdims=True) * ealg
            dstate[:, gs] = dst + dsn * ealg
            dxdt_q = []
            dacol_q = []
            for qd in range(SSD_GW // SSD_QW):
                ql = slice(g * SSD_GW + qd * SSD_QW, g * SSD_GW + (qd + 1) * SSD_QW)
                aq = p["acol"][:, ql]
                arow = jnp.sum(aq * itile, axis=0, keepdims=True)
                dq = jnp.exp(jnp.where(ctile, aq - arow, -jnp.inf))
                mq = cbt * dq
                xbd = jnp.where(bdmask, _tile4(_bf(p["xdt"][:, ql])), jnp.zeros((), BF16))
                dyq = _bf(dy[:, ql])
                dm = _dot_nt(dyq, xbd)
                dxbd = jnp.where(bdmask, _dot_tn(_bf(mq), dyq), 0.0)
                dxdt_q.append(_fold4(dxbd))
                eq = dm * mq
                dacol_q.append(eq - itile * jnp.sum(eq, axis=0, keepdims=True))
                dcbt = _bf(dm * dq)
                dc = dc + _dot(dcbt, btile)
                db = db + _fold4(_dot_tn(dcbt, cg))
            dxdt_g = dwt * tailg + jnp.concatenate(dxdt_q, axis=1)
            dacol_s[:, gs] = dacol_g + jnp.concatenate(dacol_q, axis=1) + last_row * dal_row
            ddte_s[:, gs] = dxdt_g * p["xs"][:, gs]
            dxbc_ref[:, gs] = dyg * dske[:, gs] + dxdt_g * p["dte"][:, gs]
            dxbc_ref[:, B_OFF + g * SSD_N:B_OFF + (g + 1) * SSD_N] = db
            dxbc_ref[:, C_OFF + g * SSD_N:C_OFF + (g + 1) * SSD_N] = dc

        dacum = _dot_exact_l(dacol_s[...], et)
        triu = _bf((_iota((CHUNK, CHUNK), 1) >= _iota((CHUNK, CHUNK), 0)).astype(F32))
        da = _dot_exact_r(triu, dacum)
        ddt = _dot_exact_l(ddte_s[...], et) + da * p["nexp"]
        dalog_ref[...] += jnp.sum(da * p["a"], axis=0, keepdims=True)
        du = ddt * _sigmoid(p["u"])
        ddtr_ref[...] = _bf(du)
        ddtb_ref[...] += jnp.sum(du, axis=0, keepdims=True)

    sp = _ssd_specs(nc, True)
    return pl.pallas_call(
        body,
        out_shape=(jax.ShapeDtypeStruct((s, SSD_CD), F32), jax.ShapeDtypeStruct((s, SSD_W), BF16),
                   jax.ShapeDtypeStruct((s, HEAD_PAD), BF16), jax.ShapeDtypeStruct((1, HEAD_PAD), F32),
                   jax.ShapeDtypeStruct((1, HEAD_PAD), F32), jax.ShapeDtypeStruct((1, HEAD_PAD), F32),
                   jax.ShapeDtypeStruct((1, SSD_W), F32), *rider.out_shapes),
        grid=(nc,),
        in_specs=[sp["xbc"], sp["dt"], sp["wide"], sp["wide"], sp["state"], sp["wide"],
                  sp["head"], sp["head"], sp["head"], sp["roww"], sp["e"], sp["et"], *rider.in_specs()],
        out_specs=(sp["xbc"], sp["wide"], sp["dt"], sp["head"], sp["head"], sp["head"], sp["roww"],
                   *rider.out_specs()),
        scratch_shapes=[pltpu.VMEM((SSD_N, SSD_W), F32), pltpu.VMEM((CHUNK, SSD_W), F32),
                        pltpu.VMEM((CHUNK, SSD_W), F32), *rider.scratch()],
        compiler_params=_params("arbitrary"), name=name,
    )(xbc, dt_raw, z, yraw, states, dys, dt_bias, a_log, d_skip, norm_w, e_sel, et_sel, *rider.inputs)


W_IN_SHARD = IN_PROJ // N_DEV
W_IN_ROW_PARTS = ((0, 832), (832, 608), (1440, 608))


def _device_step(x, pos_col, target, norm1_w, conv_w_shard, conv_b, dt_bias, a_log, d_skip, ssd_norm_w, norm_f_w,
                 w_r, w_z, w_xbc, w_dt, w_g, br_ret_shard, br_ssd_shard, out_shard):
    inv_freq = jnp.asarray(ROPE_THETA ** (-np.arange(ROPE_HALF, dtype=np.float64) / ROPE_HALF), F32).reshape(1, ROPE_HALF)
    dtb, alog, dsk = _pad_heads(dt_bias), _pad_heads(a_log), _pad_heads(d_skip)
    my_core = lax.axis_index("c")

    h = _rmsnorm_fwd(x, norm1_w, "rmsnorm1_fwd")
    gather = _gather_rider([br_ret_shard, br_ssd_shard, out_shard, conv_w_shard], forward_at=0.7)
    p_r, all_ret, all_ssd, all_out, all_conv = _matmul(h, w_r, "nn", F32, "proj_ret", rider=gather)
    w_br_ret = all_ret.reshape(RET_W, D_MODEL)
    w_br_ssd = all_ssd.reshape(SSD_W, D_MODEL)
    w_out = all_out.reshape(D_MODEL, D_MODEL)
    conv_w = all_conv.transpose(1, 0, 2).reshape(SSD_CONV, SSD_CD)
    p_z = _matmul(h, w_z, "nn", F32, "proj_z")
    p_xbc = _matmul(h, w_xbc, "nn", F32, "proj_xbc")
    p_dt = _matmul(h, w_dt, "nn", F32, "proj_dt")
    p_g = _matmul(h, w_g, "nn", F32, "proj_gates")
    cos, sin = _rope_tables(pos_col, inv_freq, "rope_tables")
    y_r, ret_states = _retention_fwd(p_r, cos, sin, "retention_fwd")
    xbc_act = _conv_fwd(p_xbc, conv_w, conv_b, "conv_fwd")
    y_raw, y_s, ssd_states = _ssd_fwd(xbc_act, p_dt, p_z, dtb, alog, dsk, ssd_norm_w, "ssd_fwd")
    pr = _matmul(y_r, w_br_ret, "nn", F32, "branch_ret")
    ps = _matmul(y_s, w_br_ssd, "nn", F32, "branch_ssd")
    merged = _merge_fwd(p_g, pr, ps, "merge_fwd")
    o = _matmul(merged, w_out, "nn", F32, "out_proj")
    dx2, g_norm_f, loss_acc = _final_fwd_bwd(x, o, norm_f_w, target, "final_norm_loss")

    g_w_out = _matmul(merged, dx2, "tn", BF16, "grad_w_out")
    dmerged = _matmul(dx2, w_out, "nt", F32, "d_merged")
    dpr, dps, dp_g = _merge_bwd(p_g, pr, ps, dmerged, "merge_bwd")
    g_w_br_ret = _matmul(y_r, dpr, "tn", BF16, "grad_w_br_ret")
    g_w_br_ssd = _matmul(y_s, dps, "tn", BF16, "grad_w_br_ssd")
    dy_r = _matmul(dpr, w_br_ret, "nt", F32, "d_y_ret")
    dy_s = _matmul(dps, w_br_ssd, "nt", F32, "d_y_ssd")
    scatter = _scatter_rider([g_w_out.reshape(N_DEV, -1, D_MODEL), g_w_br_ret.reshape(N_DEV, -1, D_MODEL),
                              g_w_br_ssd.reshape(N_DEV, -1, D_MODEL)])
    dxbc_act, dp_z, dp_dt, g_dtb, g_alog, g_dsk, g_ssd_norm, got_out, got_ret, got_ssd = _ssd_bwd(
        xbc_act, p_dt, p_z, y_raw, ssd_states, dy_s, dtb, alog, dsk, ssd_norm_w, "ssd_bwd", scatter)
    dp_xbc, g_conv_w, g_conv_b = _conv_bwd(p_xbc, conv_w, conv_b, dxbc_act, "conv_bwd")
    dp_r = _retention_bwd(p_r, cos, sin, ret_states, dy_r, "retention_bwd")
    g_w_in = jnp.concatenate([
        _matmul(h, dp_r, "tn", BF16, "grad_w_ret"),
        _matmul(h, dp_z, "tn", BF16, "grad_w_z"),
        _matmul(h, dp_xbc, "tn", BF16, "grad_w_xbc"),
        _matmul(h, dp_dt, "tn", BF16, "grad_w_dt")[:, :SSD_HEADS],
        _matmul(h, dp_g, "tn", BF16, "grad_w_gates"),
    ], axis=1)
    blocks = g_w_in.reshape(D_MODEL, N_DEV, W_IN_SHARD).transpose(1, 0, 2)
    dh, from_sibling = _matmul(dp_r, w_r, "nt", F32, "d_h_ret", rider=_sibling_rider(blocks))
    own = lax.dynamic_index_in_dim(blocks.reshape(4, 2, D_MODEL, W_IN_SHARD), my_core, axis=1, keepdims=False)
    chip_sum = _add_bf16(own.reshape(4 * D_MODEL, W_IN_SHARD), from_sibling.reshape(4 * D_MODEL, W_IN_SHARD),
                         "w_in_chip_sum").reshape(4, D_MODEL, W_IN_SHARD)
    carriers = (("d_h_xbc", dp_xbc, w_xbc), ("d_h_gates", dp_g, w_g), ("d_h_z", dp_z, w_z))
    parts = []
    for (row0, rows), (nm, dp, w) in zip(W_IN_ROW_PARTS, carriers):
        dh, landed = _matmul(dp, w, "nt", F32, nm, add=dh, rider=_chip_rider(chip_sum, row0, rows))
        parts.append(_sum_slots(landed, "w_in_sum_rows_%d" % row0))
    dh = _matmul(dp_dt, w_dt, "nt", F32, "d_h_dt", add=dh)
    grad_x, g_norm1 = _rmsnorm_bwd(x, norm1_w, dh, dx2, "rmsnorm1_bwd")
    small = dict(norm1_w=g_norm1, conv_w=g_conv_w, conv_b=g_conv_b, dt_bias=g_dtb[:, :SSD_HEADS],
                 a_log=g_alog[:, :SSD_HEADS], d_skip=g_dsk[:, :SSD_HEADS], ssd_norm_w=g_ssd_norm,
                 norm_f_w=g_norm_f)
    big = dict(w_in=jnp.concatenate(parts, axis=0), w_br_ret=_sum_slots(got_ret, "w_br_ret_sum"),
               w_br_ssd=_sum_slots(got_ssd, "w_br_ssd_sum"), w_out=_sum_slots(got_out, "w_out_sum"))
    return loss_acc[0, 0], grad_x, small, big


def _all_gather(shard, name):
    rider = _gather_rider([shard], 0.0)

    def body(x_ref, out_ref, send_sems, recv_sems):
        for _, fn in rider.phases:
            fn((x_ref,), (out_ref,), send_sems, recv_sems)

    return pl.pallas_call(
        body, out_shape=rider.out_shapes[0], in_specs=rider.in_specs(), out_specs=rider.out_specs()[0],
        scratch_shapes=rider.scratch(), name=name)(shard)


def _all_reduce_small(vec, name):
    r, c = vec.shape

    def body(x_ref, out_ref, land, send_sems, recv_sems):
        x, y, cc = _mesh_pos()
        my_idx = 4 * x + 2 * y + cc
        land[my_idx] = x_ref[...]
        copies = []
        for k in range(1, N_DEV):
            px, py, pc = x ^ (k >> 2), y ^ ((k >> 1) & 1), cc ^ (k & 1)
            cp = pltpu.make_async_remote_copy(
                src_ref=x_ref, dst_ref=land.at[my_idx],
                send_sem=send_sems.at[k - 1], recv_sem=recv_sems.at[k - 1],
                device_id=(px, py, pc), device_id_type=MESH)
            cp.start()
            copies.append(cp)
        for k in range(1, N_DEV):
            px, py, pc = x ^ (k >> 2), y ^ ((k >> 1) & 1), cc ^ (k & 1)
            pltpu.make_async_remote_copy(
                src_ref=x_ref, dst_ref=land.at[4 * px + 2 * py + pc],
                send_sem=send_sems.at[k - 1], recv_sem=recv_sems.at[k - 1],
                device_id=(px, py, pc), device_id_type=MESH).wait_recv()
        for cp in copies:
            cp.wait_send()
        acc = land[0]
        for i in range(1, N_DEV):
            acc = acc + land[i]
        out_ref[...] = acc

    return pl.pallas_call(
        body,
        out_shape=jax.ShapeDtypeStruct((r, c), F32),
        in_specs=[pl.BlockSpec(memory_space=pltpu.VMEM)],
        out_specs=pl.BlockSpec(memory_space=pltpu.VMEM),
        scratch_shapes=[pltpu.VMEM((N_DEV, r, c), F32), pltpu.SemaphoreType.DMA((7,)),
                        pltpu.SemaphoreType.DMA((7,))],
        name=name)(vec)


_SMALL = ("norm1_w", "conv_w", "conv_b", "dt_bias", "a_log", "d_skip", "ssd_norm_w", "norm_f_w")
_SMALL_COLS = 128
_WEIGHTS = ("norm1_w", "w_in", "conv_w", "conv_b", "dt_bias", "a_log", "d_skip", "ssd_norm_w",
            "w_br_ret", "w_br_ssd", "w_out", "norm_f_w")


def _pack(parts):
    flat = jnp.concatenate([p.reshape(-1).astype(F32) for p in parts])
    rows = -(-flat.shape[0] // (8 * _SMALL_COLS)) * 8
    return jnp.pad(flat, (0, rows * _SMALL_COLS - flat.shape[0])).reshape(rows, _SMALL_COLS)


def _unpack(packed, shapes):
    flat = packed.reshape(-1)
    out, off = [], 0
    for shp in shapes:
        n = int(np.prod(shp))
        out.append(flat[off:off + n].reshape(shp))
        off += n
    return out


def kernel(x, positions, norm1_w, w_in, conv_w, conv_b, dt_bias, a_log, d_skip, ssd_norm_w, w_br_ret, w_br_ssd, w_out, norm_f_w, loss_target, m_norm1_w, m_w_in, m_conv_w, m_conv_b, m_dt_bias, m_a_log, m_d_skip, m_ssd_norm_w, m_w_br_ret, m_w_br_ssd, m_w_out, m_norm_f_w, v_norm1_w, v_w_in, v_conv_w, v_conv_b, v_dt_bias, v_a_log, v_d_skip, v_ssd_norm_w, v_w_br_ret, v_w_br_ssd, v_w_out, v_norm_f_w):
    w = dict(norm1_w=norm1_w, w_in=w_in, conv_w=conv_w, conv_b=conv_b, dt_bias=dt_bias, a_log=a_log,
             d_skip=d_skip, ssd_norm_w=ssd_norm_w, w_br_ret=w_br_ret, w_br_ssd=w_br_ssd, w_out=w_out,
             norm_f_w=norm_f_w)
    m = dict(norm1_w=m_norm1_w, w_in=m_w_in, conv_w=m_conv_w, conv_b=m_conv_b, dt_bias=m_dt_bias,
             a_log=m_a_log, d_skip=m_d_skip, ssd_norm_w=m_ssd_norm_w, w_br_ret=m_w_br_ret,
             w_br_ssd=m_w_br_ssd, w_out=m_w_out, norm_f_w=m_norm_f_w)
    v = dict(norm1_w=v_norm1_w, w_in=v_w_in, conv_w=v_conv_w, conv_b=v_conv_b, dt_bias=v_dt_bias,
             a_log=v_a_log, d_skip=v_d_skip, ssd_norm_w=v_ssd_norm_w, w_br_ret=v_w_br_ret,
             w_br_ssd=v_w_br_ssd, w_out=v_w_out, norm_f_w=v_norm_f_w)
    s = x.shape[1]
    my_idx = 4 * lax.axis_index("x") + 2 * lax.axis_index("y") + lax.axis_index("c")

    w_in_all = _all_gather(_cast_bf16(w_in[0], "cast_w_in"), "gather_w_in")
    w_in_full = w_in_all.transpose(1, 0, 2).reshape(D_MODEL, IN_PROJ)
    w_dt = jnp.pad(w_in_full[:, OFF_DT:OFF_G], ((0, 0), (0, HEAD_PAD - SSD_HEADS)))

    loss_part, grad_x, g_small, g_big = _device_step(
        x[0], positions.reshape(s, 1), loss_target[0], norm1_w, conv_w[0], conv_b, dt_bias, a_log, d_skip,
        ssd_norm_w, norm_f_w.reshape(1, D_MODEL),
        w_in_full[:, :OFF_Z], w_in_full[:, OFF_Z:OFF_XBC], w_in_full[:, OFF_XBC:OFF_DT], w_dt,
        w_in_full[:, OFF_G:], _cast_bf16(w_br_ret[0], "cast_w_br_ret"), _cast_bf16(w_br_ssd[0], "cast_w_br_ssd"),
        _cast_bf16(w_out[0], "cast_w_out"))

    loss = lax.psum(loss_part, ("x", "y", "c"))

    small_shapes = [g_small[n].shape for n in _SMALL]
    summed = _unpack(_all_reduce_small(_pack([g_small[n] for n in _SMALL]), "allreduce_small"), small_shapes)
    grads = dict(zip(_SMALL, summed))
    conv_cols = SSD_CD // N_DEV
    grads["conv_w"] = lax.dynamic_slice_in_dim(grads["conv_w"], my_idx * conv_cols, conv_cols, axis=1)
    grads["norm_f_w"] = grads["norm_f_w"].reshape(D_MODEL)
    grads.update(g_big)
    for n in ("norm1_w", "w_in", "conv_w", "conv_b", "dt_bias", "a_log", "d_skip", "ssd_norm_w", "w_br_ret",
              "w_br_ssd", "w_out"):
        grads[n] = grads[n].reshape(w[n].shape)

    delta, new_m, new_v = {}, {}, {}
    for n in ("w_in", "w_br_ret", "w_br_ssd", "w_out"):
        shp = w[n].shape
        two_d = (shp[1], shp[2])
        d_, m_, v_ = _adamw(w[n].reshape(two_d), grads[n].reshape(two_d), m[n].reshape(two_d),
                            v[n].reshape(two_d), "adamw_" + n)
        delta[n], new_m[n], new_v[n] = d_.reshape(shp), m_.reshape(shp), v_.reshape(shp)
    shapes = [w[n].shape for n in _SMALL]
    packed = _adamw(_pack([w[n] for n in _SMALL]), _pack([grads[n] for n in _SMALL]),
                    _pack([m[n] for n in _SMALL]), _pack([v[n] for n in _SMALL]), "adamw_small")
    for res, dst in zip(packed, (delta, new_m, new_v)):
        for n, a in zip(_SMALL, _unpack(res, shapes)):
            dst[n] = a

    return (loss, grad_x.reshape(x.shape), *[grads[n] for n in _WEIGHTS], *[delta[n] for n in _WEIGHTS],
            *[new_m[n] for n in _WEIGHTS], *[new_v[n] for n in _WEIGHTS])
```

```python
import functools

import numpy as np
import jax
import jax.numpy as jnp
from jax import lax
from jax.experimental import pallas as pl
from jax.experimental.pallas import tpu as pltpu

F32 = jnp.float32
BF16 = jnp.bfloat16

D_MODEL = 2048
CHUNK = 64
EPS = 1e-6
N_DEV = 8

RET_HEADS = 8
RET_DK = 256
RET_W = RET_HEADS * RET_DK
ROPE_THETA = 10000.0
ROPE_HALF = RET_DK // 2

SSD_W = 4096
SSD_P = 64
SSD_HEADS = 64
SSD_GROUPS = 8
SSD_N = 128
SSD_GW = SSD_W // SSD_GROUPS
SSD_QW = 256
SSD_CONV = 4
SSD_CD = SSD_W + 2 * SSD_GROUPS * SSD_N
HEAD_PAD = 128
B_OFF = SSD_W
C_OFF = SSD_W + SSD_GROUPS * SSD_N

ADAM_LR = 0.001
ADAM_B1 = 0.9
ADAM_B2 = 0.999
ADAM_EPS = 1e-08
ADAM_WD = 0.01
ADAM_STEP = 10

SPLITS = (RET_W, RET_W, RET_W, RET_W, SSD_W, SSD_CD, SSD_HEADS, D_MODEL, D_MODEL)
IN_PROJ = sum(SPLITS)
OFF_Z = 4 * RET_W
OFF_XBC = OFF_Z + SSD_W
OFF_DT = OFF_XBC + SSD_CD
OFF_G = OFF_DT + SSD_HEADS

ROW_TILE = 256
CONV_TILE = 128
MM_TILE = 1024
MM_TK = 2048

MESH = pl.DeviceIdType.MESH


def _dot(a, b):
    return lax.dot_general(a, b, (((1,), (0,)), ((), ())), preferred_element_type=F32)


def _dot_nt(a, b):
    return lax.dot_general(a, b, (((1,), (1,)), ((), ())), preferred_element_type=F32)


def _dot_tn(a, b):
    return lax.dot_general(a, b, (((0,), (0,)), ((), ())), preferred_element_type=F32)


def _bf(x):
    return x.astype(BF16)


def _split3(x):
    hi = x.astype(BF16)
    r = x - hi.astype(F32)
    mid = r.astype(BF16)
    lo = (r - mid.astype(F32)).astype(BF16)
    return hi, mid, lo


def _dot_exact_l(x, sel):
    hi, mid, lo = _split3(x)
    return _dot(hi, sel) + _dot(mid, sel) + _dot(lo, sel)


def _dot_exact_r(sel, x):
    hi, mid, lo = _split3(x)
    return _dot(sel, hi) + _dot(sel, mid) + _dot(sel, lo)


def _sigmoid(x):
    return 1.0 / (1.0 + jnp.exp(-x))


def _softplus(x):
    return jnp.maximum(x, 0.0) + jnp.log(1.0 + jnp.exp(-jnp.abs(x)))


def _iota(shape, axis):
    return lax.broadcasted_iota(jnp.int32, shape, axis)


def _params(*sem):
    return pltpu.CompilerParams(dimension_semantics=sem)


class _Rider:
    def __init__(self, inputs, out_shapes, n_sems, phases):
        self.inputs, self.out_shapes, self.n_sems, self.phases = tuple(inputs), tuple(out_shapes), n_sems, phases

    def in_specs(self):
        return [pl.BlockSpec(memory_space=pl.ANY)] * len(self.inputs)

    def out_specs(self):
        return [pl.BlockSpec(memory_space=pl.ANY)] * len(self.out_shapes)

    def scratch(self):
        return [pltpu.SemaphoreType.DMA((self.n_sems,)), pltpu.SemaphoreType.DMA((self.n_sems,))]

    def run(self, step, n_steps, ins, outs, send_sems, recv_sems, last):
        for frac, fn in self.phases:
            if (frac >= 1.0) != last:
                continue
            at = min(int(frac * n_steps), n_steps - 1)

            @pl.when(step == at)
            def _(fn=fn):
                fn(ins, outs, send_sems, recv_sems)


def _mesh_pos():
    return lax.axis_index("x"), lax.axis_index("y"), lax.axis_index("c")


def _gather_rider(shards, forward_at):
    n = len(shards)

    def tools(a, ins, outs, send_sems, recv_sems):
        x, y, cc = _mesh_pos()
        chips = [(1 - x, y), (x, 1 - y), (1 - x, 1 - y)]

        def slot(px, py, pc):
            return outs[a].at[4 * px + 2 * py + pc]

        def copy(k, block, to, src=None):
            return pltpu.make_async_remote_copy(
                src_ref=slot(*block) if src is None else src, dst_ref=slot(*block),
                send_sem=send_sems.at[8 * a + k], recv_sem=recv_sems.at[8 * a + k], device_id=to, device_id_type=MESH)

        me, sibling = (x, y, cc), (x, y, 1 - cc)
        return dict(
            mine=lambda: pltpu.make_async_copy(ins[a], slot(*me), send_sems.at[8 * a + 7]),
            first=lambda: [copy(0, me, sibling, src=ins[a])] + [copy(1 + j, me, (*chip, cc), src=ins[a])
                                                                for j, chip in enumerate(chips)],
            passed=lambda: [copy(4 + j, (*chip, cc), sibling) for j, chip in enumerate(chips)],
            landed=lambda: [copy(1 + j, (*chip, cc), me) for j, chip in enumerate(chips)],
            from_sibling=lambda: [copy(0, sibling, me)] + [copy(4 + j, (*chip, 1 - cc), me)
                                                           for j, chip in enumerate(chips)])

    def start(*refs):
        for a in range(n):
            t = tools(a, *refs)
            t["mine"]().start()
            for cp in t["first"]():
                cp.start()

    def forward(*refs):
        for a in range(n):
            t = tools(a, *refs)
            for got, cp in zip(t["landed"](), t["passed"]()):
                got.wait_recv()
                cp.start()

    def finish(*refs):
        for a in range(n):
            t = tools(a, *refs)
            for cp in t["from_sibling"]():
                cp.wait_recv()
            for cp in t["first"]() + t["passed"]():
                cp.wait_send()
            t["mine"]().wait()

    outs = [jax.ShapeDtypeStruct((N_DEV,) + s.shape, s.dtype) for s in shards]
    return _Rider(shards, outs, 8 * n, ((0.0, start), (forward_at, forward), (1.0, finish)))


def _scatter_rider(blocks):
    n = len(blocks)

    def copies(a, ins, outs, send_sems, recv_sems, with_back):
        x, y, cc = _mesh_pos()
        my_idx = 4 * x + 2 * y + cc
        mine = pltpu.make_async_copy(ins[a].at[my_idx], outs[a].at[my_idx], send_sems.at[8 * a + 7])
        out, back = [], []
        for k in range(1, N_DEV):
            px, py, pc = x ^ (k >> 2), y ^ ((k >> 1) & 1), cc ^ (k & 1)
            peer_idx = 4 * px + 2 * py + pc
            sems = dict(send_sem=send_sems.at[8 * a + k - 1], recv_sem=recv_sems.at[8 * a + k - 1],
                        device_id=(px, py, pc), device_id_type=MESH)
            out.append(pltpu.make_async_remote_copy(src_ref=ins[a].at[peer_idx], dst_ref=outs[a].at[my_idx], **sems))
            if with_back:
                back.append(pltpu.make_async_remote_copy(src_ref=ins[a].at[my_idx], dst_ref=outs[a].at[peer_idx], **sems))
        return mine, out, back

    def start(*refs):
        for a in range(n):
            mine, out, _ = copies(a, *refs, False)
            mine.start()
            for cp in out:
                cp.start()

    def finish(*refs):
        for a in range(n):
            mine, out, back = copies(a, *refs, True)
            for cp in back:
                cp.wait_recv()
            for cp in out:
                cp.wait_send()
            mine.wait()

    outs = [jax.ShapeDtypeStruct(b.shape, b.dtype) for b in blocks]
    return _Rider(blocks, outs, 8 * n, ((0.0, start), (1.0, finish)))


def _sibling_rider(blocks):
    _, r, c = blocks.shape

    def copies(ins, outs, send_sems, recv_sems):
        x, y, cc = _mesh_pos()
        return [pltpu.make_async_remote_copy(
            src_ref=ins[0].at[2 * j + 1 - cc], dst_ref=outs[0].at[j], send_sem=send_sems.at[j],
            recv_sem=recv_sems.at[j], device_id=(x, y, 1 - cc), device_id_type=MESH) for j in range(4)]

    def start(*refs):
        for cp in copies(*refs):
            cp.start()

    def finish(*refs):
        for cp in copies(*refs):
            cp.wait_recv()
        for cp in copies(*refs):
            cp.wait_send()

    return _Rider([blocks], [jax.ShapeDtypeStruct((4, r, c), blocks.dtype)], 4, ((0.0, start), (1.0, finish)))


def _chip_rider(partial, row0, rows):
    _, _, c = partial.shape

    def copies(ins, outs, send_sems, recv_sems, with_back):
        x, y, cc = _mesh_pos()
        my_chip = 2 * x + y

        def rows_of(j):
            return ins[0].at[j, pl.ds(row0, rows)]

        mine = pltpu.make_async_copy(rows_of(my_chip), outs[0].at[my_chip], send_sems.at[3])
        out, back = [], []
        for k in range(1, 4):
            px, py = x ^ (k >> 1), y ^ (k & 1)
            peer = 2 * px + py
            sems = dict(send_sem=send_sems.at[k - 1], recv_sem=recv_sems.at[k - 1], device_id=(px, py, cc),
                        device_id_type=MESH)
            out.append(pltpu.make_async_remote_copy(src_ref=rows_of(peer), dst_ref=outs[0].at[my_chip], **sems))
            if with_back:
                back.append(pltpu.make_async_remote_copy(src_ref=rows_of(my_chip), dst_ref=outs[0].at[peer], **sems))
        return mine, out, back

    def start(*refs):
        mine, out, _ = copies(*refs, False)
        mine.start()
        for cp in out:
            cp.start()

    def finish(*refs):
        mine, out, back = copies(*refs, True)
        for cp in back:
            cp.wait_recv()
        for cp in out:
            cp.wait_send()
        mine.wait()

    return _Rider([partial], [jax.ShapeDtypeStruct((4, rows, c), partial.dtype)], 4, ((0.0, start), (1.0, finish)))


def _matmul(a, b, mode, out_dtype, name, add=None, rider=None):
    if mode == "nn":
        (m, k), (k2, n) = a.shape, b.shape
    elif mode == "nt":
        (m, k), (n, k2) = a.shape, b.shape
    else:
        (k, m), (k2, n) = a.shape, b.shape
    assert k == k2, (a.shape, b.shape, mode)
    tm, tn, tk = min(m, MM_TILE), min(n, MM_TILE), min(k, MM_TK)
    assert m % tm == 0 and n % tn == 0 and k % tk == 0, (m, n, k)
    nk = k // tk
    dot = {"nn": _dot, "nt": _dot_nt, "tn": _dot_tn}[mode]

    gm, gn = m // tm, n // tn
    n_rin = len(rider.inputs) if rider else 0
    n_rout = len(rider.out_shapes) if rider else 0
    n_add = 1 if add is not None else 0

    def body(*refs):
        a_ref, b_ref = refs[:2]
        add_ref = refs[2] if n_add else None
        r_ins = refs[2 + n_add:2 + n_add + n_rin]
        o_ref = refs[2 + n_add + n_rin]
        r_outs = refs[3 + n_add + n_rin:3 + n_add + n_rin + n_rout]
        scratch = refs[3 + n_add + n_rin + n_rout:]
        acc_ref = scratch[0] if nk > 1 else None
        kk = pl.program_id(2)
        if rider:
            step = (pl.program_id(0) * gn + pl.program_id(1)) * nk + kk
            ride = functools.partial(rider.run, step, gm * gn * nk, r_ins, r_outs, scratch[-2], scratch[-1])
            ride(last=False)

        def finish(r):
            if add_ref is not None:
                r = r + add_ref[...]
            o_ref[...] = r.astype(o_ref.dtype)

        if nk == 1:
            finish(dot(_bf(a_ref[...]), _bf(b_ref[...])))
        else:
            @pl.when(kk == 0)
            def _():
                acc_ref[...] = dot(_bf(a_ref[...]), _bf(b_ref[...]))

            @pl.when(jnp.logical_and(kk > 0, kk < nk - 1))
            def _():
                acc_ref[...] += dot(_bf(a_ref[...]), _bf(b_ref[...]))

            @pl.when(kk == nk - 1)
            def _():
                finish(acc_ref[...] + dot(_bf(a_ref[...]), _bf(b_ref[...])))
        if rider:
            ride(last=True)

    if mode == "nn":
        a_spec = pl.BlockSpec((tm, tk), lambda i, j, kk: (i, kk))
        b_spec = pl.BlockSpec((tk, tn), lambda i, j, kk: (kk, j))
    elif mode == "nt":
        a_spec = pl.BlockSpec((tm, tk), lambda i, j, kk: (i, kk))
        b_spec = pl.BlockSpec((tn, tk), lambda i, j, kk: (j, kk))
    else:
        a_spec = pl.BlockSpec((tk, tm), lambda i, j, kk: (kk, i))
        b_spec = pl.BlockSpec((tk, tn), lambda i, j, kk: (kk, j))
    o_spec = pl.BlockSpec((tm, tn), lambda i, j, kk: (i, j))
    in_specs = [a_spec, b_spec] + ([o_spec] if add is not None else [])
    args = (a, b) + ((add,) if add is not None else ())
    block_bytes = (tm * tk * a.dtype.itemsize + tk * tn * b.dtype.itemsize
                   + tm * tn * (jnp.dtype(out_dtype).itemsize + (4 if add is not None else 0)))
    vmem = 2 * block_bytes + 2 * tm * tn * 4 + 2 * (tm + tn) * tk + (4 << 20)
    out_shape = jax.ShapeDtypeStruct((m, n), out_dtype)
    scratch = [pltpu.VMEM((tm, tn), F32)] if nk > 1 else []
    if rider:
        in_specs = in_specs + rider.in_specs()
        args = args + rider.inputs
        out_shape = (out_shape, *rider.out_shapes)
        o_spec = (o_spec, *rider.out_specs())
        scratch = scratch + rider.scratch()
    sem = ("arbitrary",) * 3 if rider else ("parallel", "parallel", "arbitrary")
    return pl.pallas_call(
        body,
        out_shape=out_shape,
        grid=(gm, gn, nk),
        in_specs=in_specs,
        out_specs=o_spec,
        scratch_shapes=scratch,
        compiler_params=pltpu.CompilerParams(dimension_semantics=sem, vmem_limit_bytes=int(vmem)),
        name=name,
    )(*args)


def _row_spec(width, tile=ROW_TILE):
    return pl.BlockSpec((tile, width), lambda i: (i, 0))


def _full_spec(shape):
    nd = len(shape)
    return pl.BlockSpec(shape, lambda *_: (0,) * nd)


def _tiling_2d(r, c):
    if r <= ROW_TILE or r % ROW_TILE == 0:
        tr = min(r, ROW_TILE)
        return (tr, c), r // tr, (lambda i: (i, 0))
    tc = 128 if r > 4096 else 256
    assert c % tc == 0, (r, c)
    return (r, tc), c // tc, (lambda i: (0, i))


def _spec_2d(r, c):
    blk, grid, idx = _tiling_2d(r, c)
    return pl.BlockSpec(blk, idx), grid


def _cast_bf16(x, name):
    r, c = x.shape
    sp, grid = _spec_2d(r, c)

    def body(x_ref, o_ref):
        o_ref[...] = _bf(x_ref[...])

    return pl.pallas_call(
        body, out_shape=jax.ShapeDtypeStruct((r, c), BF16), grid=(grid,), in_specs=[sp], out_specs=sp,
        compiler_params=_params("parallel"), name=name)(x)


def _rmsnorm_fwd(x, w, name):
    s, d = x.shape

    def body(x_ref, w_ref, o_ref):
        xv = x_ref[...]
        rstd = lax.rsqrt(jnp.mean(xv * xv, axis=1, keepdims=True) + EPS)
        o_ref[...] = _bf(xv * rstd * w_ref[...])

    return pl.pallas_call(
        body, out_shape=jax.ShapeDtypeStruct((s, d), BF16), grid=(s // ROW_TILE,),
        in_specs=[_row_spec(d), _full_spec((1, d))], out_specs=_row_spec(d),
        compiler_params=_params("parallel"), name=name)(x, w)


def _rmsnorm_bwd(x, w, dh, dres, name):
    s, d = x.shape

    def body(x_ref, w_ref, dh_ref, dres_ref, dx_ref, dw_ref):
        @pl.when(pl.program_id(0) == 0)
        def _():
            dw_ref[...] = jnp.zeros_like(dw_ref)

        xv = x_ref[...]
        rstd = lax.rsqrt(jnp.mean(xv * xv, axis=1, keepdims=True) + EPS)
        xhat = xv * rstd
        dhv = dh_ref[...]
        dxhat = dhv * w_ref[...]
        dx = rstd * (dxhat - xhat * jnp.mean(dxhat * xhat, axis=1, keepdims=True))
        dx_ref[...] = dx + dres_ref[...]
        dw_ref[...] += jnp.sum(dhv * xhat, axis=0, keepdims=True)

    return pl.pallas_call(
        body,
        out_shape=(jax.ShapeDtypeStruct((s, d), F32), jax.ShapeDtypeStruct((1, d), F32)),
        grid=(s // ROW_TILE,),
        in_specs=[_row_spec(d), _full_spec((1, d)), _row_spec(d), _row_spec(d)],
        out_specs=(_row_spec(d), _full_spec((1, d))),
        compiler_params=_params("arbitrary"), name=name)(x, w, dh, dres)


def _rope_tables(pos_col, inv_freq, name):
    s = pos_col.shape[0]

    def body(p_ref, f_ref, cos_ref, sin_ref):
        ang = p_ref[...].astype(F32) * f_ref[...]
        cos_ref[...] = jnp.cos(ang)
        sin_ref[...] = jnp.sin(ang)

    out = jax.ShapeDtypeStruct((s, ROPE_HALF), F32)
    return pl.pallas_call(
        body, out_shape=(out, out), grid=(s // ROW_TILE,),
        in_specs=[_row_spec(1), _full_spec((1, ROPE_HALF))],
        out_specs=(_row_spec(ROPE_HALF), _row_spec(ROPE_HALF)),
        compiler_params=_params("parallel"), name=name)(pos_col, inv_freq)


def _merge_fwd(pg, p_r, p_s, name):
    s = pg.shape[0]

    def body(g_ref, r_ref, s_ref, o_ref):
        g = g_ref[...]
        o_ref[...] = _bf(_sigmoid(g[:, :D_MODEL]) * r_ref[...] + _sigmoid(g[:, D_MODEL:]) * s_ref[...])

    return pl.pallas_call(
        body, out_shape=jax.ShapeDtypeStruct((s, D_MODEL), BF16), grid=(s // ROW_TILE,),
        in_specs=[_row_spec(2 * D_MODEL), _row_spec(D_MODEL), _row_spec(D_MODEL)],
        out_specs=_row_spec(D_MODEL), compiler_params=_params("parallel"), name=name)(pg, p_r, p_s)


def _merge_bwd(pg, p_r, p_s, dm, name):
    s = pg.shape[0]

    def body(g_ref, r_ref, s_ref, dm_ref, dr_ref, ds_ref, dg_ref):
        g = g_ref[...]
        sr, ss = _sigmoid(g[:, :D_MODEL]), _sigmoid(g[:, D_MODEL:])
        d = dm_ref[...]
        dr_ref[...] = _bf(d * sr)
        ds_ref[...] = _bf(d * ss)
        dg_ref[:, :D_MODEL] = _bf(d * r_ref[...] * sr * (1.0 - sr))
        dg_ref[:, D_MODEL:] = _bf(d * s_ref[...] * ss * (1.0 - ss))

    o = jax.ShapeDtypeStruct((s, D_MODEL), BF16)
    return pl.pallas_call(
        body, out_shape=(o, o, jax.ShapeDtypeStruct((s, 2 * D_MODEL), BF16)), grid=(s // ROW_TILE,),
        in_specs=[_row_spec(2 * D_MODEL), _row_spec(D_MODEL), _row_spec(D_MODEL), _row_spec(D_MODEL)],
        out_specs=(_row_spec(D_MODEL), _row_spec(D_MODEL), _row_spec(2 * D_MODEL)),
        compiler_params=_params("parallel"), name=name)(pg, p_r, p_s, dm)


def _final_fwd_bwd(x, o, w, target, name):
    s, d = x.shape

    def body(x_ref, o_ref, w_ref, t_ref, dx_ref, dw_ref, loss_ref):
        @pl.when(pl.program_id(0) == 0)
        def _():
            dw_ref[...] = jnp.zeros_like(dw_ref)
            loss_ref[...] = jnp.zeros_like(loss_ref)

        x2 = x_ref[...] + o_ref[...]
        rstd = lax.rsqrt(jnp.mean(x2 * x2, axis=1, keepdims=True) + EPS)
        xhat = x2 * rstd
        wv = w_ref[...]
        err = xhat * wv - t_ref[...]
        loss_ref[...] += jnp.sum(jnp.sum(err * err, axis=1, keepdims=True), axis=0, keepdims=True) * (0.5 / d)
        dy = err * (1.0 / d)
        dw_ref[...] += jnp.sum(dy * xhat, axis=0, keepdims=True)
        dxhat = dy * wv
        dx_ref[...] = rstd * (dxhat - xhat * jnp.mean(dxhat * xhat, axis=1, keepdims=True))

    return pl.pallas_call(
        body,
        out_shape=(jax.ShapeDtypeStruct((s, d), F32), jax.ShapeDtypeStruct((1, d), F32),
                   jax.ShapeDtypeStruct((8, 128), F32)),
        grid=(s // ROW_TILE,),
        in_specs=[_row_spec(d), _row_spec(d), _full_spec((1, d)), _row_spec(d)],
        out_specs=(_row_spec(d), _full_spec((1, d)), _full_spec((8, 128))),
        compiler_params=_params("arbitrary"), name=name)(x, o, w, target)


def _adamw(w, g, m, v, name):
    r, c = w.shape
    sp, grid = _spec_2d(r, c)
    c1 = 1.0 / (1.0 - ADAM_B1 ** ADAM_STEP)
    c2 = 1.0 / (1.0 - ADAM_B2 ** ADAM_STEP)

    def body(w_ref, g_ref, m_ref, v_ref, d_ref, nm_ref, nv_ref):
        gv = g_ref[...]
        nm = ADAM_B1 * m_ref[...] + (1.0 - ADAM_B1) * gv
        nv = ADAM_B2 * v_ref[...] + (1.0 - ADAM_B2) * (gv * gv)
        d_ref[...] = -ADAM_LR * ((nm * c1) / (jnp.sqrt(nv * c2) + ADAM_EPS) + ADAM_WD * w_ref[...])
        nm_ref[...] = nm
        nv_ref[...] = nv

    o = jax.ShapeDtypeStruct((r, c), F32)
    return pl.pallas_call(
        body, out_shape=(o, o, o), grid=(grid,), in_specs=[sp, sp, sp, sp], out_specs=(sp, sp, sp),
        compiler_params=_params("parallel"), name=name)(w, g, m, v)


def _sum_slots(land, name):
    n, r, c = land.shape
    blk, grid, idx = _tiling_2d(r, c)

    def body(l_ref, o_ref):
        acc = l_ref[0].astype(F32)
        for i in range(1, n):
            acc = acc + l_ref[i].astype(F32)
        o_ref[...] = acc

    return pl.pallas_call(
        body, out_shape=jax.ShapeDtypeStruct((r, c), F32), grid=(grid,),
        in_specs=[pl.BlockSpec((n, *blk), lambda i: (0, *idx(i)))], out_specs=pl.BlockSpec(blk, idx),
        compiler_params=_params("parallel"), name=name)(land)


def _add_bf16(a, b, name):
    r, c = a.shape
    sp, grid = _spec_2d(r, c)

    def body(a_ref, b_ref, o_ref):
        o_ref[...] = _bf(a_ref[...].astype(F32) + b_ref[...].astype(F32))

    return pl.pallas_call(
        body, out_shape=jax.ShapeDtypeStruct((r, c), BF16), grid=(grid,), in_specs=[sp, sp], out_specs=sp,
        compiler_params=_params("parallel"), name=name)(a, b)


def _retention_tables():
    lg = np.log1p(-(2.0 ** (-5.0 - np.arange(RET_HEADS, dtype=np.float64))))
    idx = np.arange(CHUNK, dtype=np.float64)
    intra = np.exp(np.abs(idx[:, None] - idx[None, :])[None] * lg[:, None, None])
    qd = np.exp((idx[None, :] + 1.0) * lg[:, None])
    kd = np.exp((CHUNK - 1.0 - idx[None, :]) * lg[:, None])
    cd = np.exp(CHUNK * lg)
    ones = np.ones((1, 1, RET_DK))
    return (jnp.asarray(intra, F32), jnp.asarray(qd[:, :, None] * ones, F32),
            jnp.asarray(kd[:, :, None] * ones, F32), jnp.asarray(cd[:, None, None] * ones, F32))


def _rope(t, cos, sin):
    t1, t2 = t[:, :ROPE_HALF], t[:, ROPE_HALF:]
    return jnp.concatenate([t1 * cos - t2 * sin, t2 * cos + t1 * sin], axis=1)


def _rope_t(d, cos, sin):
    d1, d2 = d[:, :ROPE_HALF], d[:, ROPE_HALF:]
    return jnp.concatenate([d1 * cos + d2 * sin, d2 * cos - d1 * sin], axis=1)


def _ret_head_fwd(q, k, v, cos, sin, intra, qd, st):
    qr = _rope(q, cos, sin)
    kr = _rope(k, cos, sin) * (RET_DK ** -0.5)
    sc = _dot_nt(_bf(qr), _bf(kr)) * intra
    y = _dot(_bf(sc), _bf(v)) + _dot(_bf(qr), _bf(st)) * qd
    return qr, kr, sc, y


def _ret_specs(nc, rev):
    cidx = (lambda c: nc - 1 - c) if rev else (lambda c: c)
    return dict(
        proj=pl.BlockSpec((CHUNK, 4 * RET_W), lambda c: (cidx(c), 0)),
        half=pl.BlockSpec((CHUNK, ROPE_HALF), lambda c: (cidx(c), 0)),
        wide=pl.BlockSpec((CHUNK, RET_W), lambda c: (cidx(c), 0)),
        state=pl.BlockSpec((1, RET_HEADS, RET_DK, RET_DK), lambda c: (cidx(c), 0, 0, 0)),
        intra=_full_spec((RET_HEADS, CHUNK, CHUNK)),
        dec=_full_spec((RET_HEADS, CHUNK, RET_DK)),
        cd=_full_spec((RET_HEADS, 1, RET_DK)),
    )


def _retention_fwd(p_r, cos, sin, name):
    s = p_r.shape[0]
    nc = s // CHUNK
    intra_t, qd_t, kd_t, cd_t = _retention_tables()

    def body(p_ref, cos_ref, sin_ref, intra_ref, qd_ref, kd_ref, cd_ref, y_ref, st_ref, state):
        @pl.when(pl.program_id(0) == 0)
        def _():
            state[...] = jnp.zeros_like(state)

        cos, sin = cos_ref[...], sin_ref[...]
        for h in range(RET_HEADS):
            lo = h * RET_DK
            q = p_ref[:, lo:lo + RET_DK]
            k = p_ref[:, RET_W + lo:RET_W + lo + RET_DK]
            v = p_ref[:, 2 * RET_W + lo:2 * RET_W + lo + RET_DK]
            g = p_ref[:, 3 * RET_W + lo:3 * RET_W + lo + RET_DK]
            st = state[h]
            st_ref[0, h] = st
            _, kr, _, y = _ret_head_fwd(q, k, v, cos, sin, intra_ref[h], qd_ref[h], st)
            state[h] = st * cd_ref[h] + _dot_tn(_bf(kr * kd_ref[h]), _bf(v))
            mu = jnp.mean(y, axis=1, keepdims=True)
            yc = y - mu
            rstd = lax.rsqrt(jnp.mean(yc * yc, axis=1, keepdims=True) + EPS)
            y_ref[:, lo:lo + RET_DK] = _bf(yc * rstd * (g * _sigmoid(g)))

    sp = _ret_specs(nc, False)
    return pl.pallas_call(
        body,
        out_shape=(jax.ShapeDtypeStruct((s, RET_W), BF16),
                   jax.ShapeDtypeStruct((nc, RET_HEADS, RET_DK, RET_DK), F32)),
        grid=(nc,),
        in_specs=[sp["proj"], sp["half"], sp["half"], sp["intra"], sp["dec"], sp["dec"], sp["cd"]],
        out_specs=(sp["wide"], sp["state"]),
        scratch_shapes=[pltpu.VMEM((RET_HEADS, RET_DK, RET_DK), F32)],
        compiler_params=_params("arbitrary"), name=name)(p_r, cos, sin, intra_t, qd_t, kd_t, cd_t)


def _retention_bwd(p_r, cos, sin, states, dy_r, name):
    s = p_r.shape[0]
    nc = s // CHUNK
    intra_t, qd_t, kd_t, cd_t = _retention_tables()

    def body(p_ref, cos_ref, sin_ref, intra_ref, qd_ref, kd_ref, cd_ref, st_ref, dy_ref, dp_ref, dstate):
        @pl.when(pl.program_id(0) == 0)
        def _():
            dstate[...] = jnp.zeros_like(dstate)

        cos, sin = cos_ref[...], sin_ref[...]
        for h in range(RET_HEADS):
            lo = h * RET_DK
            q = p_ref[:, lo:lo + RET_DK]
            k = p_ref[:, RET_W + lo:RET_W + lo + RET_DK]
            v = p_ref[:, 2 * RET_W + lo:2 * RET_W + lo + RET_DK]
            g = p_ref[:, 3 * RET_W + lo:3 * RET_W + lo + RET_DK]
            st = st_ref[0, h]
            intra, qd, kd = intra_ref[h], qd_ref[h], kd_ref[h]
            qr, kr, sc, y = _ret_head_fwd(q, k, v, cos, sin, intra, qd, st)
            mu = jnp.mean(y, axis=1, keepdims=True)
            yc = y - mu
            rstd = lax.rsqrt(jnp.mean(yc * yc, axis=1, keepdims=True) + EPS)
            yn = yc * rstd
            sg = _sigmoid(g)
            dyr = dy_ref[:, lo:lo + RET_DK]
            dyn = dyr * (g * sg)
            dg = dyr * yn * (sg * (1.0 + g * (1.0 - sg)))
            dy = rstd * (dyn - jnp.mean(dyn, axis=1, keepdims=True)
                         - yn * jnp.mean(dyn * yn, axis=1, keepdims=True))
            dyb, vb, qrb = _bf(dy), _bf(v), _bf(qr)
            dsn = dstate[h]
            dsnb = _bf(dsn)
            ds = _bf(_dot_nt(dyb, vb) * intra)
            t = _bf(dy * qd)
            dv = _dot_tn(_bf(sc), dyb) + _dot(_bf(kr * kd), dsnb)
            dqr = _dot(ds, _bf(kr)) + _dot_nt(t, _bf(st))
            dkr = _dot_tn(ds, qrb) + _dot_nt(vb, dsnb) * kd
            dstate[h] = dsn * cd_ref[h] + _dot_tn(qrb, t)
            dp_ref[:, lo:lo + RET_DK] = _bf(_rope_t(dqr, cos, sin))
            dp_ref[:, RET_W + lo:RET_W + lo + RET_DK] = _bf(_rope_t(dkr, cos, sin) * (RET_DK ** -0.5))
            dp_ref[:, 2 * RET_W + lo:2 * RET_W + lo + RET_DK] = _bf(dv)
            dp_ref[:, 3 * RET_W + lo:3 * RET_W + lo + RET_DK] = _bf(dg)

    sp = _ret_specs(nc, True)
    return pl.pallas_call(
        body,
        out_shape=jax.ShapeDtypeStruct((s, 4 * RET_W), BF16),
        grid=(nc,),
        in_specs=[sp["proj"], sp["half"], sp["half"], sp["intra"], sp["dec"], sp["dec"], sp["cd"],
                  sp["state"], sp["wide"]],
        out_specs=sp["proj"],
        scratch_shapes=[pltpu.VMEM((RET_HEADS, RET_DK, RET_DK), F32)],
        compiler_params=_params("arbitrary"), name=name)(p_r, cos, sin, intra_t, qd_t, kd_t, cd_t, states, dy_r)


def _conv_taps(ext, w):
    acc = w[SSD_CONV - 1:SSD_CONV] * ext
    for j in range(SSD_CONV - 1):
        acc = acc + w[j:j + 1] * pltpu.roll(ext, SSD_CONV - 1 - j, axis=0)
    return acc


def _conv_fwd(xbc_raw, conv_w, conv_b, name):
    s = xbc_raw.shape[0]
    t8 = CONV_TILE // 8

    def body(cur_ref, prev_ref, w_ref, b_ref, o_ref):
        keep = (pl.program_id(0) > 0).astype(F32)
        ext = jnp.concatenate([prev_ref[...] * keep, cur_ref[...]], axis=0)
        u = _conv_taps(ext, w_ref[...])[8:] + b_ref[...]
        o_ref[...] = u * _sigmoid(u)

    return pl.pallas_call(
        body, out_shape=jax.ShapeDtypeStruct((s, SSD_CD), F32), grid=(s // CONV_TILE,),
        in_specs=[_row_spec(SSD_CD, CONV_TILE),
                  pl.BlockSpec((8, SSD_CD), lambda i: (jnp.maximum(i * t8 - 1, 0), 0)),
                  _full_spec((SSD_CONV, SSD_CD)), _full_spec((1, SSD_CD))],
        out_specs=_row_spec(SSD_CD, CONV_TILE),
        compiler_params=_params("parallel"), name=name)(xbc_raw, xbc_raw, conv_w, conv_b)


def _conv_bwd(xbc_raw, conv_w, conv_b, dact, name):
    s = xbc_raw.shape[0]
    nt = s // CONV_TILE
    t8 = CONV_TILE // 8
    rows = CONV_TILE + 8

    def body(cur_ref, prev_ref, next_ref, w_ref, b_ref, d_ref, dnext_ref, dx_ref, dw_ref, db_ref):
        i = pl.program_id(0)

        @pl.when(i == 0)
        def _():
            dw_ref[...] = jnp.zeros_like(dw_ref)
            db_ref[...] = jnp.zeros_like(db_ref)

        keep_prev = (i > 0).astype(F32)
        keep_next = (i < nt - 1).astype(F32)
        w = w_ref[...]
        ext = jnp.concatenate([prev_ref[...] * keep_prev, cur_ref[...], next_ref[...]], axis=0)
        shifted = [pltpu.roll(ext, SSD_CONV - 1 - j, axis=0)[8:] for j in range(SSD_CONV - 1)] + [ext[8:]]
        u = b_ref[...]
        for j in range(SSD_CONV):
            u = u + w[j:j + 1] * shifted[j]
        sg = _sigmoid(u)
        dact = jnp.concatenate([d_ref[...], dnext_ref[...] * keep_next], axis=0)
        du = dact * (sg * (1.0 + u * (1.0 - sg)))
        dx = w[SSD_CONV - 1:SSD_CONV] * du
        for j in range(SSD_CONV - 1):
            dx = dx + w[j:j + 1] * pltpu.roll(du, rows - (SSD_CONV - 1 - j), axis=0)
        dx_ref[...] = _bf(dx[:CONV_TILE])
        duc = du[:CONV_TILE]
        for j in range(SSD_CONV):
            dw_ref[j:j + 1, :] += jnp.sum(duc * shifted[j][:CONV_TILE], axis=0, keepdims=True)
        db_ref[...] += jnp.sum(duc, axis=0, keepdims=True)

    row = _row_spec(SSD_CD, CONV_TILE)
    prev = pl.BlockSpec((8, SSD_CD), lambda i: (jnp.maximum(i * t8 - 1, 0), 0))
    nxt = pl.BlockSpec((8, SSD_CD), lambda i: (jnp.minimum((i + 1) * t8, s // 8 - 1), 0))
    return pl.pallas_call(
        body,
        out_shape=(jax.ShapeDtypeStruct((s, SSD_CD), BF16), jax.ShapeDtypeStruct((SSD_CONV, SSD_CD), F32),
                   jax.ShapeDtypeStruct((1, SSD_CD), F32)),
        grid=(nt,),
        in_specs=[row, prev, nxt, _full_spec((SSD_CONV, SSD_CD)), _full_spec((1, SSD_CD)), row, nxt],
        out_specs=(row, _full_spec((SSD_CONV, SSD_CD)), _full_spec((1, SSD_CD))),
        compiler_params=_params("arbitrary"), name=name)(xbc_raw, xbc_raw, xbc_raw, conv_w, conv_b, dact, dact)


def _head_select():
    e = np.zeros((HEAD_PAD, SSD_W), np.float32)
    for h in range(SSD_HEADS):
        e[h, h * SSD_P:(h + 1) * SSD_P] = 1.0
    return jnp.asarray(e, BF16), jnp.asarray(e.T, BF16)


def _pad_heads(v):
    return jnp.pad(v.reshape(1, SSD_HEADS).astype(F32), ((0, 0), (0, HEAD_PAD - SSD_HEADS)))


def _ssd_masks():
    r = _iota((CHUNK, SSD_QW), 0)
    c = _iota((CHUNK, SSD_QW), 1) % SSD_P
    itile = (r == c).astype(F32)
    ctile = r >= c
    rb = _iota((SSD_QW, SSD_QW), 0) // SSD_P
    cb = _iota((SSD_QW, SSD_QW), 1) // SSD_P
    return itile, ctile, rb == cb


def _ssd_prep(xbc_ref, dtr_ref, dtb_ref, alog_ref, e_ref):
    xs = xbc_ref[:, :SSD_W]
    u = dtr_ref[...] + dtb_ref[...]
    dt = _softplus(u)
    nexp = -jnp.exp(alog_ref[...])
    a = dt * nexp
    tril = _bf((_iota((CHUNK, CHUNK), 0) >= _iota((CHUNK, CHUNK), 1)).astype(F32))
    acum = _dot_exact_r(tril, a)
    e = e_ref[...]
    acol = _dot_exact_l(acum, e)
    dte = _dot_exact_l(dt, e)
    alast = acol[CHUNK - 1:CHUNK, :]
    return dict(xs=xs, u=u, dt=dt, nexp=nexp, a=a, acol=acol, dte=dte, xdt=xs * dte,
                ea=jnp.exp(acol), tail=jnp.exp(alast - acol), eal=jnp.exp(alast))


def _tile4(x):
    return jnp.concatenate([x, x, x, x], axis=0)


def _fold4(x):
    return x[0:CHUNK] + x[CHUNK:2 * CHUNK] + x[2 * CHUNK:3 * CHUNK] + x[3 * CHUNK:4 * CHUNK]


def _ssd_specs(nc, rev):
    cidx = (lambda c: nc - 1 - c) if rev else (lambda c: c)
    return dict(
        xbc=pl.BlockSpec((CHUNK, SSD_CD), lambda c: (cidx(c), 0)),
        dt=pl.BlockSpec((CHUNK, HEAD_PAD), lambda c: (cidx(c), 0)),
        wide=pl.BlockSpec((CHUNK, SSD_W), lambda c: (cidx(c), 0)),
        state=pl.BlockSpec((1, SSD_N, SSD_W), lambda c: (cidx(c), 0, 0)),
        head=_full_spec((1, HEAD_PAD)),
        roww=_full_spec((1, SSD_W)),
        e=_full_spec((HEAD_PAD, SSD_W)),
        et=_full_spec((SSD_W, HEAD_PAD)),
    )


def _ssd_fwd(xbc, dt_raw, z, dt_bias, a_log, d_skip, norm_w, name):
    s = xbc.shape[0]
    nc = s // CHUNK
    e_sel, _ = _head_select()

    def body(xbc_ref, dtr_ref, z_ref, dtb_ref, alog_ref, dsk_ref, nw_ref, e_ref,
             yraw_ref, ys_ref, st_ref, state):
        @pl.when(pl.program_id(0) == 0)
        def _():
            state[...] = jnp.zeros_like(state)

        p = _ssd_prep(xbc_ref, dtr_ref, dtb_ref, alog_ref, e_ref)
        itile, ctile, bdmask = _ssd_masks()
        dske = _dot_exact_l(jnp.broadcast_to(dsk_ref[...], (8, HEAD_PAD)), e_ref[...])[0:1]
        st_ref[0] = state[...]
        for g in range(SSD_GROUPS):
            gs = slice(g * SSD_GW, (g + 1) * SSD_GW)
            bg = _bf(xbc_ref[:, B_OFF + g * SSD_N:B_OFF + (g + 1) * SSD_N])
            cg = _bf(xbc_ref[:, C_OFF + g * SSD_N:C_OFF + (g + 1) * SSD_N])
            cbt = _dot_nt(cg, _tile4(bg))
            stg = state[:, gs]
            ys = _dot(cg, _bf(stg))
            for qd in range(SSD_GW // SSD_QW):
                ql = slice(g * SSD_GW + qd * SSD_QW, g * SSD_GW + (qd + 1) * SSD_QW)
                aq = p["acol"][:, ql]
                arow = jnp.sum(aq * itile, axis=0, keepdims=True)
                dq = jnp.exp(jnp.where(ctile, aq - arow, -jnp.inf))
                xbd = jnp.where(bdmask, _tile4(_bf(p["xdt"][:, ql])), jnp.zeros((), BF16))
                yq = _dot(_bf(cbt * dq), xbd) + ys[:, qd * SSD_QW:(qd + 1) * SSD_QW] * p["ea"][:, ql]
                yraw_ref[:, ql] = yq + dske[:, ql] * p["xs"][:, ql]
            wt = _bf(p["xdt"][:, gs] * p["tail"][:, gs])
            state[:, gs] = stg * p["eal"][:, gs] + _dot_tn(bg, wt)
        zv = z_ref[...]
        t = yraw_ref[...] * (zv * _sigmoid(zv))
        rstd = lax.rsqrt(jnp.mean(t * t, axis=1, keepdims=True) + EPS)
        ys_ref[...] = _bf(t * rstd * nw_ref[...])

    sp = _ssd_specs(nc, False)
    return pl.pallas_call(
        body,
        out_shape=(jax.ShapeDtypeStruct((s, SSD_W), F32), jax.ShapeDtypeStruct((s, SSD_W), BF16),
                   jax.ShapeDtypeStruct((nc, SSD_N, SSD_W), F32)),
        grid=(nc,),
        in_specs=[sp["xbc"], sp["dt"], sp["wide"], sp["head"], sp["head"], sp["head"], sp["roww"], sp["e"]],
        out_specs=(sp["wide"], sp["wide"], sp["state"]),
        scratch_shapes=[pltpu.VMEM((SSD_N, SSD_W), F32)],
        compiler_params=_params("arbitrary"), name=name,
    )(xbc, dt_raw, z, dt_bias, a_log, d_skip, norm_w, e_sel)


def _ssd_bwd(xbc, dt_raw, z, yraw, states, dys, dt_bias, a_log, d_skip, norm_w, name, rider):
    s = xbc.shape[0]
    nc = s // CHUNK
    e_sel, et_sel = _head_select()
    n_in, n_out, n_scratch = 12, 7, 3
    n_rin, n_rout = len(rider.inputs), len(rider.out_shapes)

    def body(*refs):
        ins, refs = refs[:n_in], refs[n_in:]
        r_ins, refs = refs[:n_rin], refs[n_rin:]
        outs, refs = refs[:n_out], refs[n_out:]
        r_outs, refs = refs[:n_rout], refs[n_rout:]
        ride = functools.partial(rider.run, pl.program_id(0), nc, r_ins, r_outs, refs[n_scratch], refs[n_scratch + 1])
        ride(last=False)
        compute(*ins, *outs, *refs[:n_scratch])
        ride(last=True)

    def compute(xbc_ref, dtr_ref, z_ref, yraw_ref, st_ref, dys_ref, dtb_ref, alog_ref, dsk_ref, nw_ref,
                e_ref, et_ref, dxbc_ref, dz_ref, ddtr_ref, ddtb_ref, dalog_ref, ddsk_ref, dnw_ref,
                dstate, dacol_s, ddte_s):
        @pl.when(pl.program_id(0) == 0)
        def _():
            dstate[...] = jnp.zeros_like(dstate)
            ddtb_ref[...] = jnp.zeros_like(ddtb_ref)
            dalog_ref[...] = jnp.zeros_like(dalog_ref)
            ddsk_ref[...] = jnp.zeros_like(ddsk_ref)
            dnw_ref[...] = jnp.zeros_like(dnw_ref)

        p = _ssd_prep(xbc_ref, dtr_ref, dtb_ref, alog_ref, e_ref)
        itile, ctile, bdmask = _ssd_masks()
        et = et_ref[...]
        dske = _dot_exact_l(jnp.broadcast_to(dsk_ref[...], (8, HEAD_PAD)), e_ref[...])[0:1]
        last_row = (_iota((CHUNK, 1), 0) == CHUNK - 1).astype(F32)

        zv, y, dysv = z_ref[...], yraw_ref[...], dys_ref[...]
        sz = _sigmoid(zv)
        silz = zv * sz
        t = y * silz
        rstd = lax.rsqrt(jnp.mean(t * t, axis=1, keepdims=True) + EPS)
        that = t * rstd
        dthat = dysv * nw_ref[...]
        dt_ = rstd * (dthat - that * jnp.mean(dthat * that, axis=1, keepdims=True))
        dnw_ref[...] += jnp.sum(dysv * that, axis=0, keepdims=True)
        dz_ref[...] = _bf(dt_ * y * (sz * (1.0 + zv * (1.0 - sz))))
        dy = dt_ * silz
        dsk_row = jnp.sum(dy * p["xs"], axis=0, keepdims=True)
        ddsk_ref[...] += _dot_exact_l(jnp.broadcast_to(dsk_row, (8, SSD_W)), et)[0:1]

        for g in range(SSD_GROUPS):
            gs = slice(g * SSD_GW, (g + 1) * SSD_GW)
            bg = _bf(xbc_ref[:, B_OFF + g * SSD_N:B_OFF + (g + 1) * SSD_N])
            cg = _bf(xbc_ref[:, C_OFF + g * SSD_N:C_OFF + (g + 1) * SSD_N])
            btile = _tile4(bg)
            cbt = _dot_nt(cg, btile)
            stg = st_ref[0, :, gs]
            stgb = _bf(stg)
            dyg = dy[:, gs]
            eag, tailg, xdtg, ealg = p["ea"][:, gs], p["tail"][:, gs], p["xdt"][:, gs], p["eal"][:, gs]
            dys_g = _bf(dyg * eag)
            dacol_g = dyg * (_dot(cg, stgb) * eag)
            dc = _dot_nt(dys_g, stgb)
            dst = _dot_tn(cg, dys_g)
            dsn = dstate[:, gs]
            dsnb = _bf(dsn)
            dwt = _dot(bg, dsnb)
            db = _dot_nt(_bf(xdtg * tailg), dsnb)
            dtl = dwt * xdtg * tailg
            dacol_g = dacol_g - dtl
            dal_row = jnp.sum(dtl, axis=0, keepdims=True) + jnp.sum(dsn * stg, axis=0, keepdims=True) * ealg
            dstate[:, gs] = dst + dsn * ealg
            dxdt_q = []
            dacol_q = []
            for qd in range(SSD_GW // SSD_QW):
                ql = slice(g * SSD_GW + qd * SSD_QW, g * SSD_GW + (qd + 1) * SSD_QW)
                aq = p["acol"][:, ql]
                arow = jnp.sum(aq * itile, axis=0, keepdims=True)
                dq = jnp.exp(jnp.where(ctile, aq - arow, -jnp.inf))
                mq = cbt * dq
                xbd = jnp.where(bdmask, _tile4(_bf(p["xdt"][:, ql])), jnp.zeros((), BF16))
                dyq = _bf(dy[:, ql])
                dm = _dot_nt(dyq, xbd)
                dxbd = jnp.where(bdmask, _dot_tn(_bf(mq), dyq), 0.0)
                dxdt_q.append(_fold4(dxbd))
                eq = dm * mq
                dacol_q.append(eq - itile * jnp.sum(eq, axis=0, keepdims=True))
                dcbt = _bf(dm * dq)
                dc = dc + _dot(dcbt, btile)
                db = db + _fold4(_dot_tn(dcbt, cg))
            dxdt_g = dwt * tailg + jnp.concatenate(dxdt_q, axis=1)
            dacol_s[:, gs] = dacol_g + jnp.concatenate(dacol_q, axis=1) + last_row * dal_row
            ddte_s[:, gs] = dxdt_g * p["xs"][:, gs]
            dxbc_ref[:, gs] = dyg * dske[:, gs] + dxdt_g * p["dte"][:, gs]
            dxbc_ref[:, B_OFF + g * SSD_N:B_OFF + (g + 1) * SSD_N] = db
            dxbc_ref[:, C_OFF + g * SSD_N:C_OFF + (g + 1) * SSD_N] = dc

        dacum = _dot_exact_l(dacol_s[...], et)
        triu = _bf((_iota((CHUNK, CHUNK), 1) >= _iota((CHUNK, CHUNK), 0)).astype(F32))
        da = _dot_exact_r(triu, dacum)
        ddt = _dot_exact_l(ddte_s[...], et) + da * p["nexp"]
        dalog_ref[...] += jnp.sum(da * p["a"], axis=0, keepdims=True)
        du = ddt * _sigmoid(p["u"])
        ddtr_ref[...] = _bf(du)
        ddtb_ref[...] += jnp.sum(du, axis=0, keepdims=True)

    sp = _ssd_specs(nc, True)
    return pl.pallas_call(
        body,
        out_shape=(jax.ShapeDtypeStruct((s, SSD_CD), F32), jax.ShapeDtypeStruct((s, SSD_W), BF16),
                   jax.ShapeDtypeStruct((s, HEAD_PAD), BF16), jax.ShapeDtypeStruct((1, HEAD_PAD), F32),
                   jax.ShapeDtypeStruct((1, HEAD_PAD), F32), jax.ShapeDtypeStruct((1, HEAD_PAD), F32),
                   jax.ShapeDtypeStruct((1, SSD_W), F32), *rider.out_shapes),
        grid=(nc,),
        in_specs=[sp["xbc"], sp["dt"], sp["wide"], sp["wide"], sp["state"], sp["wide"],
                  sp["head"], sp["head"], sp["head"], sp["roww"], sp["e"], sp["et"], *rider.in_specs()],
        out_specs=(sp["xbc"], sp["wide"], sp["dt"], sp["head"], sp["head"], sp["head"], sp["roww"],
                   *rider.out_specs()),
        scratch_shapes=[pltpu.VMEM((SSD_N, SSD_W), F32), pltpu.VMEM((CHUNK, SSD_W), F32),
                        pltpu.VMEM((CHUNK, SSD_W), F32), *rider.scratch()],
        compiler_params=_params("arbitrary"), name=name,
    )(xbc, dt_raw, z, yraw, states, dys, dt_bias, a_log, d_skip, norm_w, e_sel, et_sel, *rider.inputs)


W_IN_SHARD = IN_PROJ // N_DEV
W_IN_ROW_PARTS = ((0, 1152), (1152, 832), (1984, 840))


def _device_step(x, pos_col, target, norm1_w, conv_w_shard, conv_b, dt_bias, a_log, d_skip, ssd_norm_w, norm_f_w,
                 w_r, w_z, w_xbc, w_dt, w_g, br_ret_shard, br_ssd_shard, out_shard):
    inv_freq = jnp.asarray(ROPE_THETA ** (-np.arange(ROPE_HALF, dtype=np.float64) / ROPE_HALF), F32).reshape(1, ROPE_HALF)
    dtb, alog, dsk = _pad_heads(dt_bias), _pad_heads(a_log), _pad_heads(d_skip)
    my_core = lax.axis_index("c")

    h = _rmsnorm_fwd(x, norm1_w, "rmsnorm1_fwd")
    gather = _gather_rider([br_ret_shard, br_ssd_shard, out_shard, conv_w_shard], forward_at=0.7)
    p_r, all_ret, all_ssd, all_out, all_conv = _matmul(h, w_r, "nt", F32, "proj_ret", rider=gather)
    w_br_ret = all_ret.reshape(RET_W, D_MODEL)
    w_br_ssd = all_ssd.reshape(SSD_W, D_MODEL)
    w_out = all_out.reshape(D_MODEL, D_MODEL)
    conv_w = all_conv.transpose(1, 0, 2).reshape(SSD_CONV, SSD_CD)
    p_z = _matmul(h, w_z, "nt", F32, "proj_z")
    p_xbc = _matmul(h, w_xbc, "nt", F32, "proj_xbc")
    p_dt = _matmul(h, w_dt, "nt", F32, "proj_dt")
    p_g = _matmul(h, w_g, "nt", F32, "proj_gates")
    cos, sin = _rope_tables(pos_col, inv_freq, "rope_tables")
    y_r, ret_states = _retention_fwd(p_r, cos, sin, "retention_fwd")
    xbc_act = _conv_fwd(p_xbc, conv_w, conv_b, "conv_fwd")
    y_raw, y_s, ssd_states = _ssd_fwd(xbc_act, p_dt, p_z, dtb, alog, dsk, ssd_norm_w, "ssd_fwd")
    pr = _matmul(y_r, w_br_ret, "nn", F32, "branch_ret")
    ps = _matmul(y_s, w_br_ssd, "nn", F32, "branch_ssd")
    merged = _merge_fwd(p_g, pr, ps, "merge_fwd")
    o = _matmul(merged, w_out, "nn", F32, "out_proj")
    dx2, g_norm_f, loss_acc = _final_fwd_bwd(x, o, norm_f_w, target, "final_norm_loss")

    g_w_out = _matmul(merged, dx2, "tn", BF16, "grad_w_out")
    dmerged = _matmul(dx2, w_out, "nt", F32, "d_merged")
    dpr, dps, dp_g = _merge_bwd(p_g, pr, ps, dmerged, "merge_bwd")
    g_w_br_ret = _matmul(y_r, dpr, "tn", BF16, "grad_w_br_ret")
    g_w_br_ssd = _matmul(y_s, dps, "tn", BF16, "grad_w_br_ssd")
    dy_r = _matmul(dpr, w_br_ret, "nt", F32, "d_y_ret")
    dy_s = _matmul(dps, w_br_ssd, "nt", F32, "d_y_ssd")
    scatter = _scatter_rider([g_w_out.reshape(N_DEV, -1, D_MODEL), g_w_br_ret.reshape(N_DEV, -1, D_MODEL),
                              g_w_br_ssd.reshape(N_DEV, -1, D_MODEL)])
    dxbc_act, dp_z, dp_dt, g_dtb, g_alog, g_dsk, g_ssd_norm, got_out, got_ret, got_ssd = _ssd_bwd(
        xbc_act, p_dt, p_z, y_raw, ssd_states, dy_s, dtb, alog, dsk, ssd_norm_w, "ssd_bwd", scatter)
    dp_xbc, g_conv_w, g_conv_b = _conv_bwd(p_xbc, conv_w, conv_b, dxbc_act, "conv_bwd")
    dp_r = _retention_bwd(p_r, cos, sin, ret_states, dy_r, "retention_bwd")
    g_w_in = jnp.concatenate([
        _matmul(dp_r, h, "tn", BF16, "grad_w_ret"),
        _matmul(dp_z, h, "tn", BF16, "grad_w_z"),
        _matmul(dp_xbc, h, "tn", BF16, "grad_w_xbc"),
        _matmul(dp_dt, h, "tn", BF16, "grad_w_dt")[:SSD_HEADS],
        _matmul(dp_g, h, "tn", BF16, "grad_w_gates"),
    ], axis=0)
    blocks = g_w_in.reshape(N_DEV, W_IN_SHARD, D_MODEL)
    dh, from_sibling = _matmul(dp_r, w_r, "nn", F32, "d_h_ret", rider=_sibling_rider(blocks))
    own = lax.dynamic_index_in_dim(blocks.reshape(4, 2, W_IN_SHARD, D_MODEL), my_core, axis=1, keepdims=False)
    chip_sum = _add_bf16(own.reshape(4 * W_IN_SHARD, D_MODEL), from_sibling.reshape(4 * W_IN_SHARD, D_MODEL),
                         "w_in_chip_sum").reshape(4, W_IN_SHARD, D_MODEL)
    carriers = (("d_h_xbc", dp_xbc, w_xbc), ("d_h_gates", dp_g, w_g), ("d_h_z", dp_z, w_z))
    parts = []
    for (row0, rows), (nm, dp, w) in zip(W_IN_ROW_PARTS, carriers):
        dh, landed = _matmul(dp, w, "nn", F32, nm, add=dh, rider=_chip_rider(chip_sum, row0, rows))
        parts.append(_sum_slots(landed, "w_in_sum_rows_%d" % row0))
    dh = _matmul(dp_dt, w_dt, "nn", F32, "d_h_dt", add=dh)
    grad_x, g_norm1 = _rmsnorm_bwd(x, norm1_w, dh, dx2, "rmsnorm1_bwd")
    small = dict(norm1_w=g_norm1, conv_w=g_conv_w, conv_b=g_conv_b, dt_bias=g_dtb[:, :SSD_HEADS],
                 a_log=g_alog[:, :SSD_HEADS], d_skip=g_dsk[:, :SSD_HEADS], ssd_norm_w=g_ssd_norm,
                 norm_f_w=g_norm_f)
    big = dict(w_in=jnp.concatenate(parts, axis=0), w_br_ret=_sum_slots(got_ret, "w_br_ret_sum"),
               w_br_ssd=_sum_slots(got_ssd, "w_br_ssd_sum"), w_out=_sum_slots(got_out, "w_out_sum"))
    return loss_acc[0, 0], grad_x, small, big


def _all_gather(shard, name):
    rider = _gather_rider([shard], 0.0)

    def body(x_ref, out_ref, send_sems, recv_sems):
        for _, fn in rider.phases:
            fn((x_ref,), (out_ref,), send_sems, recv_sems)

    return pl.pallas_call(
        body, out_shape=rider.out_shapes[0], in_specs=rider.in_specs(), out_specs=rider.out_specs()[0],
        scratch_shapes=rider.scratch(), name=name)(shard)


def _all_reduce_small(vec, name):
    r, c = vec.shape

    def body(x_ref, out_ref, land, send_sems, recv_sems):
        x, y, cc = _mesh_pos()
        my_idx = 4 * x + 2 * y + cc
        land[my_idx] = x_ref[...]
        copies = []
        for k in range(1, N_DEV):
            px, py, pc = x ^ (k >> 2), y ^ ((k >> 1) & 1), cc ^ (k & 1)
            cp = pltpu.make_async_remote_copy(
                src_ref=x_ref, dst_ref=land.at[my_idx],
                send_sem=send_sems.at[k - 1], recv_sem=recv_sems.at[k - 1],
                device_id=(px, py, pc), device_id_type=MESH)
            cp.start()
            copies.append(cp)
        for k in range(1, N_DEV):
            px, py, pc = x ^ (k >> 2), y ^ ((k >> 1) & 1), cc ^ (k & 1)
            pltpu.make_async_remote_copy(
                src_ref=x_ref, dst_ref=land.at[4 * px + 2 * py + pc],
                send_sem=send_sems.at[k - 1], recv_sem=recv_sems.at[k - 1],
                device_id=(px, py, pc), device_id_type=MESH).wait_recv()
        for cp in copies:
            cp.wait_send()
        acc = land[0]
        for i in range(1, N_DEV):
            acc = acc + land[i]
        out_ref[...] = acc

    return pl.pallas_call(
        body,
        out_shape=jax.ShapeDtypeStruct((r, c), F32),
        in_specs=[pl.BlockSpec(memory_space=pltpu.VMEM)],
        out_specs=pl.BlockSpec(memory_space=pltpu.VMEM),
        scratch_shapes=[pltpu.VMEM((N_DEV, r, c), F32), pltpu.SemaphoreType.DMA((7,)),
                        pltpu.SemaphoreType.DMA((7,))],
        name=name)(vec)


_SMALL = ("norm1_w", "conv_w", "conv_b", "dt_bias", "a_log", "d_skip", "ssd_norm_w", "norm_f_w")
_SMALL_COLS = 128
_W_IN = "w_in"
_WEIGHTS = ("norm1_w", "w_in", "conv_w", "conv_b", "dt_bias", "a_log", "d_skip", "ssd_norm_w",
            "w_br_ret", "w_br_ssd", "w_out", "norm_f_w")


def _pack(parts):
    flat = jnp.concatenate([p.reshape(-1).astype(F32) for p in parts])
    rows = -(-flat.shape[0] // (8 * _SMALL_COLS)) * 8
    return jnp.pad(flat, (0, rows * _SMALL_COLS - flat.shape[0])).reshape(rows, _SMALL_COLS)


def _unpack(packed, shapes):
    flat = packed.reshape(-1)
    out, off = [], 0
    for shp in shapes:
        n = int(np.prod(shp))
        out.append(flat[off:off + n].reshape(shp))
        off += n
    return out


def kernel(x, positions, norm1_w, w_in, conv_w, conv_b, dt_bias, a_log, d_skip, ssd_norm_w, w_br_ret, w_br_ssd, w_out, norm_f_w, loss_target, m_norm1_w, m_w_in, m_conv_w, m_conv_b, m_dt_bias, m_a_log, m_d_skip, m_ssd_norm_w, m_w_br_ret, m_w_br_ssd, m_w_out, m_norm_f_w, v_norm1_w, v_w_in, v_conv_w, v_conv_b, v_dt_bias, v_a_log, v_d_skip, v_ssd_norm_w, v_w_br_ret, v_w_br_ssd, v_w_out, v_norm_f_w):
    w = dict(norm1_w=norm1_w, w_in=w_in, conv_w=conv_w, conv_b=conv_b, dt_bias=dt_bias, a_log=a_log,
             d_skip=d_skip, ssd_norm_w=ssd_norm_w, w_br_ret=w_br_ret, w_br_ssd=w_br_ssd, w_out=w_out,
             norm_f_w=norm_f_w)
    m = dict(norm1_w=m_norm1_w, w_in=m_w_in, conv_w=m_conv_w, conv_b=m_conv_b, dt_bias=m_dt_bias,
             a_log=m_a_log, d_skip=m_d_skip, ssd_norm_w=m_ssd_norm_w, w_br_ret=m_w_br_ret,
             w_br_ssd=m_w_br_ssd, w_out=m_w_out, norm_f_w=m_norm_f_w)
    v = dict(norm1_w=v_norm1_w, w_in=v_w_in, conv_w=v_conv_w, conv_b=v_conv_b, dt_bias=v_dt_bias,
             a_log=v_a_log, d_skip=v_d_skip, ssd_norm_w=v_ssd_norm_w, w_br_ret=v_w_br_ret,
             w_br_ssd=v_w_br_ssd, w_out=v_w_out, norm_f_w=v_norm_f_w)
    s = x.shape[1]
    my_idx = 4 * lax.axis_index("x") + 2 * lax.axis_index("y") + lax.axis_index("c")

    w[_W_IN], m[_W_IN], v[_W_IN] = w_in[0].T, m_w_in[0].T, v_w_in[0].T
    w_in_all = _all_gather(_cast_bf16(w[_W_IN], "cast_w_in"), "gather_w_in").reshape(IN_PROJ, D_MODEL)
    w_dt = jnp.pad(w_in_all[OFF_DT:OFF_G], ((0, HEAD_PAD - SSD_HEADS), (0, 0)))

    loss_part, grad_x, g_small, g_big = _device_step(
        x[0], positions.reshape(s, 1), loss_target[0], norm1_w, conv_w[0], conv_b, dt_bias, a_log, d_skip,
        ssd_norm_w, norm_f_w.reshape(1, D_MODEL),
        w_in_all[:OFF_Z], w_in_all[OFF_Z:OFF_XBC], w_in_all[OFF_XBC:OFF_DT], w_dt, w_in_all[OFF_G:],
        _cast_bf16(w_br_ret[0], "cast_w_br_ret"), _cast_bf16(w_br_ssd[0], "cast_w_br_ssd"),
        _cast_bf16(w_out[0], "cast_w_out"))

    loss = lax.psum(loss_part, ("x", "y", "c"))

    small_shapes = [g_small[n].shape for n in _SMALL]
    summed = _unpack(_all_reduce_small(_pack([g_small[n] for n in _SMALL]), "allreduce_small"), small_shapes)
    grads = dict(zip(_SMALL, summed))
    conv_cols = SSD_CD // N_DEV
    grads["conv_w"] = lax.dynamic_slice_in_dim(grads["conv_w"], my_idx * conv_cols, conv_cols, axis=1)
    grads["norm_f_w"] = grads["norm_f_w"].reshape(D_MODEL)
    for n in ("norm1_w", "conv_w", "conv_b", "dt_bias", "a_log", "d_skip", "ssd_norm_w"):
        grads[n] = grads[n].reshape(w[n].shape)

    delta, new_m, new_v = {}, {}, {}
    for n in ("w_br_ret", "w_br_ssd", "w_out"):
        w[n], m[n], v[n] = w[n][0], m[n][0], v[n][0]
    for n in (_W_IN, "w_br_ret", "w_br_ssd", "w_out"):
        back = (lambda a: a.T[None]) if n == _W_IN else (lambda a: a[None])
        res = _adamw(w[n], g_big[n], m[n], v[n], "adamw_" + n)
        grads[n] = back(g_big[n])
        delta[n], new_m[n], new_v[n] = (back(a) for a in res)
    shapes = [w[n].shape for n in _SMALL]
    packed = _adamw(_pack([w[n] for n in _SMALL]), _pack([grads[n] for n in _SMALL]),
                    _pack([m[n] for n in _SMALL]), _pack([v[n] for n in _SMALL]), "adamw_small")
    for res, dst in zip(packed, (delta, new_m, new_v)):
        for n, a in zip(_SMALL, _unpack(res, shapes)):
            dst[n] = a

    return (loss, grad_x.reshape(x.shape), *[grads[n] for n in _WEIGHTS], *[delta[n] for n in _WEIGHTS],
            *[new_m[n] for n in _WEIGHTS], *[new_v[n] for n in _WEIGHTS])
```

```python
import functools

import numpy as np
import jax
import jax.numpy as jnp
from jax import lax
from jax.experimental import pallas as pl
from jax.experimental.pallas import tpu as pltpu

F32 = jnp.float32
BF16 = jnp.bfloat16

D_MODEL = 2048
CHUNK = 64
EPS = 1e-6
N_DEV = 8

RET_HEADS = 8
RET_DK = 256
RET_W = RET_HEADS * RET_DK
ROPE_THETA = 10000.0
ROPE_HALF = RET_DK // 2

SSD_W = 4096
SSD_P = 64
SSD_HEADS = 64
SSD_GROUPS = 8
SSD_N = 128
SSD_GW = SSD_W // SSD_GROUPS
SSD_QW = 256
SSD_CONV = 4
SSD_CD = SSD_W + 2 * SSD_GROUPS * SSD_N
HEAD_PAD = 128
B_OFF = SSD_W
C_OFF = SSD_W + SSD_GROUPS * SSD_N

ADAM_LR = 0.001
ADAM_B1 = 0.9
ADAM_B2 = 0.999
ADAM_EPS = 1e-08
ADAM_WD = 0.01
ADAM_STEP = 10

SPLITS = (RET_W, RET_W, RET_W, RET_W, SSD_W, SSD_CD, SSD_HEADS, D_MODEL, D_MODEL)
IN_PROJ = sum(SPLITS)
OFF_Z = 4 * RET_W
OFF_XBC = OFF_Z + SSD_W
OFF_DT = OFF_XBC + SSD_CD
OFF_G = OFF_DT + SSD_HEADS

ROW_TILE = 256
CONV_TILE = 128
MM_TILE = 1024
MM_TK = 2048

MESH = pl.DeviceIdType.MESH


def _dot(a, b):
    return lax.dot_general(a, b, (((1,), (0,)), ((), ())), preferred_element_type=F32)


def _dot_nt(a, b):
    return lax.dot_general(a, b, (((1,), (1,)), ((), ())), preferred_element_type=F32)


def _dot_tn(a, b):
    return lax.dot_general(a, b, (((0,), (0,)), ((), ())), preferred_element_type=F32)


def _bf(x):
    return x.astype(BF16)


def _split3(x):
    hi = x.astype(BF16)
    r = x - hi.astype(F32)
    mid = r.astype(BF16)
    lo = (r - mid.astype(F32)).astype(BF16)
    return hi, mid, lo


def _dot_exact_l(x, sel):
    hi, mid, lo = _split3(x)
    return _dot(hi, sel) + _dot(mid, sel) + _dot(lo, sel)


def _dot_exact_r(sel, x):
    hi, mid, lo = _split3(x)
    return _dot(sel, hi) + _dot(sel, mid) + _dot(sel, lo)


def _sigmoid(x):
    return 1.0 / (1.0 + jnp.exp(-x))


def _softplus(x):
    return jnp.maximum(x, 0.0) + jnp.log(1.0 + jnp.exp(-jnp.abs(x)))


def _iota(shape, axis):
    return lax.broadcasted_iota(jnp.int32, shape, axis)


def _params(*sem):
    return pltpu.CompilerParams(dimension_semantics=sem)


class _Rider:
    def __init__(self, inputs, out_shapes, n_sems, phases):
        self.inputs, self.out_shapes, self.n_sems, self.phases = tuple(inputs), tuple(out_shapes), n_sems, phases

    def in_specs(self):
        return [pl.BlockSpec(memory_space=pl.ANY)] * len(self.inputs)

    def out_specs(self):
        return [pl.BlockSpec(memory_space=pl.ANY)] * len(self.out_shapes)

    def scratch(self):
        return [pltpu.SemaphoreType.DMA((self.n_sems,)), pltpu.SemaphoreType.DMA((self.n_sems,))]

    def run(self, step, n_steps, ins, outs, send_sems, recv_sems, last):
        for frac, fn in self.phases:
            if (frac >= 1.0) != last:
                continue
            at = min(int(frac * n_steps), n_steps - 1)

            @pl.when(step == at)
            def _(fn=fn):
                fn(ins, outs, send_sems, recv_sems)


def _mesh_pos():
    return lax.axis_index("x"), lax.axis_index("y"), lax.axis_index("c")


def _gather_rider(shards, forward_at):
    n = len(shards)

    def tools(a, ins, outs, send_sems, recv_sems):
        x, y, cc = _mesh_pos()
        chips = [(1 - x, y), (x, 1 - y), (1 - x, 1 - y)]

        def slot(px, py, pc):
            return outs[a].at[4 * px + 2 * py + pc]

        def copy(k, block, to, src=None):
            return pltpu.make_async_remote_copy(
                src_ref=slot(*block) if src is None else src, dst_ref=slot(*block),
                send_sem=send_sems.at[8 * a + k], recv_sem=recv_sems.at[8 * a + k], device_id=to, device_id_type=MESH)

        me, sibling = (x, y, cc), (x, y, 1 - cc)
        return dict(
            mine=lambda: pltpu.make_async_copy(ins[a], slot(*me), send_sems.at[8 * a + 7]),
            first=lambda: [copy(0, me, sibling, src=ins[a])] + [copy(1 + j, me, (*chip, cc), src=ins[a])
                                                                for j, chip in enumerate(chips)],
            passed=lambda: [copy(4 + j, (*chip, cc), sibling) for j, chip in enumerate(chips)],
            landed=lambda: [copy(1 + j, (*chip, cc), me) for j, chip in enumerate(chips)],
            from_sibling=lambda: [copy(0, sibling, me)] + [copy(4 + j, (*chip, 1 - cc), me)
                                                           for j, chip in enumerate(chips)])

    def start(*refs):
        for a in range(n):
            t = tools(a, *refs)
            t["mine"]().start()
            for cp in t["first"]():
                cp.start()

    def forward(*refs):
        for a in range(n):
            t = tools(a, *refs)
            for got, cp in zip(t["landed"](), t["passed"]()):
                got.wait_recv()
                cp.start()

    def finish(*refs):
        for a in range(n):
            t = tools(a, *refs)
            for cp in t["from_sibling"]():
                cp.wait_recv()
            for cp in t["first"]() + t["passed"]():
                cp.wait_send()
            t["mine"]().wait()

    outs = [jax.ShapeDtypeStruct((N_DEV,) + s.shape, s.dtype) for s in shards]
    return _Rider(shards, outs, 8 * n, ((0.0, start), (forward_at, forward), (1.0, finish)))


def _scatter_rider(blocks):
    n = len(blocks)

    def copies(a, ins, outs, send_sems, recv_sems, with_back):
        x, y, cc = _mesh_pos()
        my_idx = 4 * x + 2 * y + cc
        mine = pltpu.make_async_copy(ins[a].at[my_idx], outs[a].at[my_idx], send_sems.at[8 * a + 7])
        out, back = [], []
        for k in range(1, N_DEV):
            px, py, pc = x ^ (k >> 2), y ^ ((k >> 1) & 1), cc ^ (k & 1)
            peer_idx = 4 * px + 2 * py + pc
            sems = dict(send_sem=send_sems.at[8 * a + k - 1], recv_sem=recv_sems.at[8 * a + k - 1],
                        device_id=(px, py, pc), device_id_type=MESH)
            out.append(pltpu.make_async_remote_copy(src_ref=ins[a].at[peer_idx], dst_ref=outs[a].at[my_idx], **sems))
            if with_back:
                back.append(pltpu.make_async_remote_copy(src_ref=ins[a].at[my_idx], dst_ref=outs[a].at[peer_idx], **sems))
        return mine, out, back

    def start(*refs):
        for a in range(n):
            mine, out, _ = copies(a, *refs, False)
            mine.start()
            for cp in out:
                cp.start()

    def finish(*refs):
        for a in range(n):
            mine, out, back = copies(a, *refs, True)
            for cp in back:
                cp.wait_recv()
            for cp in out:
                cp.wait_send()
            mine.wait()

    outs = [jax.ShapeDtypeStruct(b.shape, b.dtype) for b in blocks]
    return _Rider(blocks, outs, 8 * n, ((0.0, start), (1.0, finish)))


def _sibling_rider(blocks):
    _, r, c = blocks.shape

    def copies(ins, outs, send_sems, recv_sems):
        x, y, cc = _mesh_pos()
        return [pltpu.make_async_remote_copy(
            src_ref=ins[0].at[2 * j + 1 - cc], dst_ref=outs[0].at[j], send_sem=send_sems.at[j],
            recv_sem=recv_sems.at[j], device_id=(x, y, 1 - cc), device_id_type=MESH) for j in range(4)]

    def start(*refs):
        for cp in copies(*refs):
            cp.start()

    def finish(*refs):
        for cp in copies(*refs):
            cp.wait_recv()
        for cp in copies(*refs):
            cp.wait_send()

    return _Rider([blocks], [jax.ShapeDtypeStruct((4, r, c), blocks.dtype)], 4, ((0.0, start), (1.0, finish)))


def _chip_rider(partial, row0, rows):
    _, _, c = partial.shape

    def copies(ins, outs, send_sems, recv_sems, with_back):
        x, y, cc = _mesh_pos()
        my_chip = 2 * x + y

        def rows_of(j):
            return ins[0].at[j, pl.ds(row0, rows)]

        mine = pltpu.make_async_copy(rows_of(my_chip), outs[0].at[my_chip], send_sems.at[3])
        out, back = [], []
        for k in range(1, 4):
            px, py = x ^ (k >> 1), y ^ (k & 1)
            peer = 2 * px + py
            sems = dict(send_sem=send_sems.at[k - 1], recv_sem=recv_sems.at[k - 1], device_id=(px, py, cc),
                        device_id_type=MESH)
            out.append(pltpu.make_async_remote_copy(src_ref=rows_of(peer), dst_ref=outs[0].at[my_chip], **sems))
            if with_back:
                back.append(pltpu.make_async_remote_copy(src_ref=rows_of(my_chip), dst_ref=outs[0].at[peer], **sems))
        return mine, out, back

    def start(*refs):
        mine, out, _ = copies(*refs, False)
        mine.start()
        for cp in out:
            cp.start()

    def finish(*refs):
        mine, out, back = copies(*refs, True)
        for cp in back:
            cp.wait_recv()
        for cp in out:
            cp.wait_send()
        mine.wait()

    return _Rider([partial], [jax.ShapeDtypeStruct((4, rows, c), partial.dtype)], 4, ((0.0, start), (1.0, finish)))


def _matmul(a, b, mode, out_dtype, name, add=None, rider=None):
    if mode == "nn":
        (m, k), (k2, n) = a.shape, b.shape
    elif mode == "nt":
        (m, k), (n, k2) = a.shape, b.shape
    else:
        (k, m), (k2, n) = a.shape, b.shape
    assert k == k2, (a.shape, b.shape, mode)
    tm, tn, tk = min(m, MM_TILE), min(n, MM_TILE), min(k, MM_TK)
    assert m % tm == 0 and n % tn == 0 and k % tk == 0, (m, n, k)
    nk = k // tk
    dot = {"nn": _dot, "nt": _dot_nt, "tn": _dot_tn}[mode]

    gm, gn = m // tm, n // tn
    n_rin = len(rider.inputs) if rider else 0
    n_rout = len(rider.out_shapes) if rider else 0
    n_add = 1 if add is not None else 0

    def body(*refs):
        a_ref, b_ref = refs[:2]
        add_ref = refs[2] if n_add else None
        r_ins = refs[2 + n_add:2 + n_add + n_rin]
        o_ref = refs[2 + n_add + n_rin]
        r_outs = refs[3 + n_add + n_rin:3 + n_add + n_rin + n_rout]
        scratch = refs[3 + n_add + n_rin + n_rout:]
        acc_ref = scratch[0] if nk > 1 else None
        kk = pl.program_id(2)
        if rider:
            step = (pl.program_id(0) * gn + pl.program_id(1)) * nk + kk
            ride = functools.partial(rider.run, step, gm * gn * nk, r_ins, r_outs, scratch[-2], scratch[-1])
            ride(last=False)

        def finish(r):
            if add_ref is not None:
                r = r + add_ref[...]
            o_ref[...] = r.astype(o_ref.dtype)

        if nk == 1:
            finish(dot(_bf(a_ref[...]), _bf(b_ref[...])))
        else:
            @pl.when(kk == 0)
            def _():
                acc_ref[...] = dot(_bf(a_ref[...]), _bf(b_ref[...]))

            @pl.when(jnp.logical_and(kk > 0, kk < nk - 1))
            def _():
                acc_ref[...] += dot(_bf(a_ref[...]), _bf(b_ref[...]))

            @pl.when(kk == nk - 1)
            def _():
                finish(acc_ref[...] + dot(_bf(a_ref[...]), _bf(b_ref[...])))
        if rider:
            ride(last=True)

    if mode == "nn":
        a_spec = pl.BlockSpec((tm, tk), lambda i, j, kk: (i, kk))
        b_spec = pl.BlockSpec((tk, tn), lambda i, j, kk: (kk, j))
    elif mode == "nt":
        a_spec = pl.BlockSpec((tm, tk), lambda i, j, kk: (i, kk))
        b_spec = pl.BlockSpec((tn, tk), lambda i, j, kk: (j, kk))
    else:
        a_spec = pl.BlockSpec((tk, tm), lambda i, j, kk: (kk, i))
        b_spec = pl.BlockSpec((tk, tn), lambda i, j, kk: (kk, j))
    o_spec = pl.BlockSpec((tm, tn), lambda i, j, kk: (i, j))
    in_specs = [a_spec, b_spec] + ([o_spec] if add is not None else [])
    args = (a, b) + ((add,) if add is not None else ())
    block_bytes = (tm * tk * a.dtype.itemsize + tk * tn * b.dtype.itemsize
                   + tm * tn * (jnp.dtype(out_dtype).itemsize + (4 if add is not None else 0)))
    vmem = 2 * block_bytes + 2 * tm * tn * 4 + 2 * (tm + tn) * tk + (4 << 20)
    out_shape = jax.ShapeDtypeStruct((m, n), out_dtype)
    scratch = [pltpu.VMEM((tm, tn), F32)] if nk > 1 else []
    if rider:
        in_specs = in_specs + rider.in_specs()
        args = args + rider.inputs
        out_shape = (out_shape, *rider.out_shapes)
        o_spec = (o_spec, *rider.out_specs())
        scratch = scratch + rider.scratch()
    sem = ("arbitrary",) * 3 if rider else ("parallel", "parallel", "arbitrary")
    return pl.pallas_call(
        body,
        out_shape=out_shape,
        grid=(gm, gn, nk),
        in_specs=in_specs,
        out_specs=o_spec,
        scratch_shapes=scratch,
        compiler_params=pltpu.CompilerParams(dimension_semantics=sem, vmem_limit_bytes=int(vmem)),
        name=name,
    )(*args)


def _row_spec(width, tile=ROW_TILE):
    return pl.BlockSpec((tile, width), lambda i: (i, 0))


def _full_spec(shape):
    nd = len(shape)
    return pl.BlockSpec(shape, lambda *_: (0,) * nd)


def _tiling_2d(r, c):
    if r <= ROW_TILE or r % ROW_TILE == 0:
        tr = min(r, ROW_TILE)
        return (tr, c), r // tr, (lambda i: (i, 0))
    tc = 128 if r > 4096 else 256
    assert c % tc == 0, (r, c)
    return (r, tc), c // tc, (lambda i: (0, i))


def _spec_2d(r, c):
    blk, grid, idx = _tiling_2d(r, c)
    return pl.BlockSpec(blk, idx), grid


def _cast_bf16(x, name):
    r, c = x.shape
    sp, grid = _spec_2d(r, c)

    def body(x_ref, o_ref):
        o_ref[...] = _bf(x_ref[...])

    return pl.pallas_call(
        body, out_shape=jax.ShapeDtypeStruct((r, c), BF16), grid=(grid,), in_specs=[sp], out_specs=sp,
        compiler_params=_params("parallel"), name=name)(x)


def _rmsnorm_fwd(x, w, name):
    s, d = x.shape

    def body(x_ref, w_ref, o_ref):
        xv = x_ref[...]
        rstd = lax.rsqrt(jnp.mean(xv * xv, axis=1, keepdims=True) + EPS)
        o_ref[...] = _bf(xv * rstd * w_ref[...])

    return pl.pallas_call(
        body, out_shape=jax.ShapeDtypeStruct((s, d), BF16), grid=(s // ROW_TILE,),
        in_specs=[_row_spec(d), _full_spec((1, d))], out_specs=_row_spec(d),
        compiler_params=_params("parallel"), name=name)(x, w)


def _rmsnorm_bwd(x, w, dh, dres, name):
    s, d = x.shape

    def body(x_ref, w_ref, dh_ref, dres_ref, dx_ref, dw_ref):
        @pl.when(pl.program_id(0) == 0)
        def _():
            dw_ref[...] = jnp.zeros_like(dw_ref)

        xv = x_ref[...]
        rstd = lax.rsqrt(jnp.mean(xv * xv, axis=1, keepdims=True) + EPS)
        xhat = xv * rstd
        dhv = dh_ref[...]
        dxhat = dhv * w_ref[...]
        dx = rstd * (dxhat - xhat * jnp.mean(dxhat * xhat, axis=1, keepdims=True))
        dx_ref[...] = dx + dres_ref[...]
        dw_ref[...] += jnp.sum(dhv * xhat, axis=0, keepdims=True)

    return pl.pallas_call(
        body,
        out_shape=(jax.ShapeDtypeStruct((s, d), F32), jax.ShapeDtypeStruct((1, d), F32)),
        grid=(s // ROW_TILE,),
        in_specs=[_row_spec(d), _full_spec((1, d)), _row_spec(d), _row_spec(d)],
        out_specs=(_row_spec(d), _full_spec((1, d))),
        compiler_params=_params("arbitrary"), name=name)(x, w, dh, dres)


def _rope_tables(pos_col, inv_freq, name):
    s = pos_col.shape[0]

    def body(p_ref, f_ref, cos_ref, sin_ref):
        ang = p_ref[...].astype(F32) * f_ref[...]
        cos_ref[...] = jnp.cos(ang)
        sin_ref[...] = jnp.sin(ang)

    out = jax.ShapeDtypeStruct((s, ROPE_HALF), F32)
    return pl.pallas_call(
        body, out_shape=(out, out), grid=(s // ROW_TILE,),
        in_specs=[_row_spec(1), _full_spec((1, ROPE_HALF))],
        out_specs=(_row_spec(ROPE_HALF), _row_spec(ROPE_HALF)),
        compiler_params=_params("parallel"), name=name)(pos_col, inv_freq)


def _merge_fwd(pg, p_r, p_s, name):
    s = pg.shape[0]

    def body(g_ref, r_ref, s_ref, o_ref):
        g = g_ref[...].astype(F32)
        o_ref[...] = _bf(_sigmoid(g[:, :D_MODEL]) * r_ref[...].astype(F32)
                         + _sigmoid(g[:, D_MODEL:]) * s_ref[...].astype(F32))

    return pl.pallas_call(
        body, out_shape=jax.ShapeDtypeStruct((s, D_MODEL), BF16), grid=(s // ROW_TILE,),
        in_specs=[_row_spec(2 * D_MODEL), _row_spec(D_MODEL), _row_spec(D_MODEL)],
        out_specs=_row_spec(D_MODEL), compiler_params=_params("parallel"), name=name)(pg, p_r, p_s)


def _merge_bwd(pg, p_r, p_s, dm, name):
    s = pg.shape[0]

    def body(g_ref, r_ref, s_ref, dm_ref, dr_ref, ds_ref, dg_ref):
        g = g_ref[...].astype(F32)
        sr, ss = _sigmoid(g[:, :D_MODEL]), _sigmoid(g[:, D_MODEL:])
        d = dm_ref[...]
        dr_ref[...] = _bf(d * sr)
        ds_ref[...] = _bf(d * ss)
        dg_ref[:, :D_MODEL] = _bf(d * r_ref[...].astype(F32) * sr * (1.0 - sr))
        dg_ref[:, D_MODEL:] = _bf(d * s_ref[...].astype(F32) * ss * (1.0 - ss))

    o = jax.ShapeDtypeStruct((s, D_MODEL), BF16)
    return pl.pallas_call(
        body, out_shape=(o, o, jax.ShapeDtypeStruct((s, 2 * D_MODEL), BF16)), grid=(s // ROW_TILE,),
        in_specs=[_row_spec(2 * D_MODEL), _row_spec(D_MODEL), _row_spec(D_MODEL), _row_spec(D_MODEL)],
        out_specs=(_row_spec(D_MODEL), _row_spec(D_MODEL), _row_spec(2 * D_MODEL)),
        compiler_params=_params("parallel"), name=name)(pg, p_r, p_s, dm)


def _final_fwd_bwd(x, o, w, target, name):
    s, d = x.shape

    def body(x_ref, o_ref, w_ref, t_ref, dx_ref, dw_ref, loss_ref):
        @pl.when(pl.program_id(0) == 0)
        def _():
            dw_ref[...] = jnp.zeros_like(dw_ref)
            loss_ref[...] = jnp.zeros_like(loss_ref)

        x2 = x_ref[...] + o_ref[...]
        rstd = lax.rsqrt(jnp.mean(x2 * x2, axis=1, keepdims=True) + EPS)
        xhat = x2 * rstd
        wv = w_ref[...]
        err = xhat * wv - t_ref[...]
        loss_ref[...] += jnp.sum(jnp.sum(err * err, axis=1, keepdims=True), axis=0, keepdims=True) * (0.5 / d)
        dy = err * (1.0 / d)
        dw_ref[...] += jnp.sum(dy * xhat, axis=0, keepdims=True)
        dxhat = dy * wv
        dx_ref[...] = rstd * (dxhat - xhat * jnp.mean(dxhat * xhat, axis=1, keepdims=True))

    return pl.pallas_call(
        body,
        out_shape=(jax.ShapeDtypeStruct((s, d), F32), jax.ShapeDtypeStruct((1, d), F32),
                   jax.ShapeDtypeStruct((8, 128), F32)),
        grid=(s // ROW_TILE,),
        in_specs=[_row_spec(d), _row_spec(d), _full_spec((1, d)), _row_spec(d)],
        out_specs=(_row_spec(d), _full_spec((1, d)), _full_spec((8, 128))),
        compiler_params=_params("arbitrary"), name=name)(x, o, w, target)


def _adamw(w, g, m, v, name):
    r, c = w.shape
    sp, grid = _spec_2d(r, c)
    c1 = 1.0 / (1.0 - ADAM_B1 ** ADAM_STEP)
    c2 = 1.0 / (1.0 - ADAM_B2 ** ADAM_STEP)

    def body(w_ref, g_ref, m_ref, v_ref, d_ref, nm_ref, nv_ref):
        gv = g_ref[...]
        nm = ADAM_B1 * m_ref[...] + (1.0 - ADAM_B1) * gv
        nv = ADAM_B2 * v_ref[...] + (1.0 - ADAM_B2) * (gv * gv)
        d_ref[...] = -ADAM_LR * ((nm * c1) / (jnp.sqrt(nv * c2) + ADAM_EPS) + ADAM_WD * w_ref[...])
        nm_ref[...] = nm
        nv_ref[...] = nv

    o = jax.ShapeDtypeStruct((r, c), F32)
    return pl.pallas_call(
        body, out_shape=(o, o, o), grid=(grid,), in_specs=[sp, sp, sp, sp], out_specs=(sp, sp, sp),
        compiler_params=_params("parallel"), name=name)(w, g, m, v)


def _sum_slots(land, name):
    n, r, c = land.shape
    blk, grid, idx = _tiling_2d(r, c)

    def body(l_ref, o_ref):
        acc = l_ref[0].astype(F32)
        for i in range(1, n):
            acc = acc + l_ref[i].astype(F32)
        o_ref[...] = acc

    return pl.pallas_call(
        body, out_shape=jax.ShapeDtypeStruct((r, c), F32), grid=(grid,),
        in_specs=[pl.BlockSpec((n, *blk), lambda i: (0, *idx(i)))], out_specs=pl.BlockSpec(blk, idx),
        compiler_params=_params("parallel"), name=name)(land)


def _add_bf16(a, b, name):
    r, c = a.shape
    sp, grid = _spec_2d(r, c)

    def body(a_ref, b_ref, o_ref):
        o_ref[...] = _bf(a_ref[...].astype(F32) + b_ref[...].astype(F32))

    return pl.pallas_call(
        body, out_shape=jax.ShapeDtypeStruct((r, c), BF16), grid=(grid,), in_specs=[sp, sp], out_specs=sp,
        compiler_params=_params("parallel"), name=name)(a, b)


def _retention_tables():
    lg = np.log1p(-(2.0 ** (-5.0 - np.arange(RET_HEADS, dtype=np.float64))))
    idx = np.arange(CHUNK, dtype=np.float64)
    intra = np.exp(np.abs(idx[:, None] - idx[None, :])[None] * lg[:, None, None])
    qd = np.exp((idx[None, :] + 1.0) * lg[:, None])
    kd = np.exp((CHUNK - 1.0 - idx[None, :]) * lg[:, None])
    cd = np.exp(CHUNK * lg)
    ones = np.ones((1, 1, RET_DK))
    return (jnp.asarray(intra, F32), jnp.asarray(qd[:, :, None] * ones, F32),
            jnp.asarray(kd[:, :, None] * ones, F32), jnp.asarray(cd[:, None, None] * ones, F32))


def _rope(t, cos, sin):
    t1, t2 = t[:, :ROPE_HALF], t[:, ROPE_HALF:]
    return jnp.concatenate([t1 * cos - t2 * sin, t2 * cos + t1 * sin], axis=1)


def _rope_t(d, cos, sin):
    d1, d2 = d[:, :ROPE_HALF], d[:, ROPE_HALF:]
    return jnp.concatenate([d1 * cos + d2 * sin, d2 * cos - d1 * sin], axis=1)


_HEADS = range(RET_HEADS)


def _ret_chunk_fwd(p_ref, cos, sin, intra_ref, qd_ref, st):
    def seg(i, h):
        return p_ref[:, i * RET_W + h * RET_DK:i * RET_W + (h + 1) * RET_DK]

    v = [seg(2, h) for h in _HEADS]
    qr = [_rope(seg(0, h), cos, sin) for h in _HEADS]
    kr = [_rope(seg(1, h), cos, sin) * (RET_DK ** -0.5) for h in _HEADS]
    qrb, vb = [_bf(a) for a in qr], [_bf(a) for a in v]
    sc = [_dot_nt(qrb[h], _bf(kr[h])) * intra_ref[h] for h in _HEADS]
    qs = [_dot(qrb[h], _bf(st[h])) for h in _HEADS]
    y = [_dot(_bf(sc[h]), vb[h]) + qs[h] * qd_ref[h] for h in _HEADS]
    g = [seg(3, h) for h in _HEADS]
    return dict(v=v, vb=vb, qr=qr, qrb=qrb, kr=kr, sc=sc, y=y, g=g)


def _group_norm(y):
    mu = jnp.mean(y, axis=1, keepdims=True)
    yc = y - mu
    rstd = lax.rsqrt(jnp.mean(yc * yc, axis=1, keepdims=True) + EPS)
    return yc * rstd, rstd


def _ret_specs(nc, rev):
    cidx = (lambda c: nc - 1 - c) if rev else (lambda c: c)
    return dict(
        proj=pl.BlockSpec((CHUNK, 4 * RET_W), lambda c: (cidx(c), 0)),
        half=pl.BlockSpec((CHUNK, ROPE_HALF), lambda c: (cidx(c), 0)),
        wide=pl.BlockSpec((CHUNK, RET_W), lambda c: (cidx(c), 0)),
        state=pl.BlockSpec((1, RET_HEADS, RET_DK, RET_DK), lambda c: (cidx(c), 0, 0, 0)),
        intra=_full_spec((RET_HEADS, CHUNK, CHUNK)),
        dec=_full_spec((RET_HEADS, CHUNK, RET_DK)),
        cd=_full_spec((RET_HEADS, 1, RET_DK)),
    )


def _retention_fwd(p_r, cos, sin, name):
    s = p_r.shape[0]
    nc = s // CHUNK
    intra_t, qd_t, kd_t, cd_t = _retention_tables()

    def body(p_ref, cos_ref, sin_ref, intra_ref, qd_ref, kd_ref, cd_ref, y_ref, st_ref, state):
        @pl.when(pl.program_id(0) == 0)
        def _():
            state[...] = jnp.zeros_like(state)

        st = [state[h] for h in _HEADS]
        f = _ret_chunk_fwd(p_ref, cos_ref[...], sin_ref[...], intra_ref, qd_ref, st)
        new_st = [st[h] * cd_ref[h] + _dot_tn(_bf(f["kr"][h] * kd_ref[h]), f["vb"][h]) for h in _HEADS]
        out = [_bf(_group_norm(f["y"][h])[0] * (f["g"][h] * _sigmoid(f["g"][h]))) for h in _HEADS]
        for h in _HEADS:
            st_ref[0, h] = st[h]
            state[h] = new_st[h]
            y_ref[:, h * RET_DK:(h + 1) * RET_DK] = out[h]

    sp = _ret_specs(nc, False)
    return pl.pallas_call(
        body,
        out_shape=(jax.ShapeDtypeStruct((s, RET_W), BF16),
                   jax.ShapeDtypeStruct((nc, RET_HEADS, RET_DK, RET_DK), F32)),
        grid=(nc,),
        in_specs=[sp["proj"], sp["half"], sp["half"], sp["intra"], sp["dec"], sp["dec"], sp["cd"]],
        out_specs=(sp["wide"], sp["state"]),
        scratch_shapes=[pltpu.VMEM((RET_HEADS, RET_DK, RET_DK), F32)],
        compiler_params=_params("arbitrary"), name=name)(p_r, cos, sin, intra_t, qd_t, kd_t, cd_t)


def _retention_bwd(p_r, cos, sin, states, dy_r, name):
    s = p_r.shape[0]
    nc = s // CHUNK
    intra_t, qd_t, kd_t, cd_t = _retention_tables()

    def body(p_ref, cos_ref, sin_ref, intra_ref, qd_ref, kd_ref, cd_ref, st_ref, dy_ref, dp_ref, dstate):
        @pl.when(pl.program_id(0) == 0)
        def _():
            dstate[...] = jnp.zeros_like(dstate)

        cos, sin = cos_ref[...], sin_ref[...]
        st = [st_ref[0, h] for h in _HEADS]
        dsn = [dstate[h] for h in _HEADS]
        f = _ret_chunk_fwd(p_ref, cos, sin, intra_ref, qd_ref, st)
        vb, qrb, kr, g = f["vb"], f["qrb"], f["kr"], f["g"]
        norm = [_group_norm(f["y"][h]) for h in _HEADS]
        sg = [_sigmoid(g[h]) for h in _HEADS]
        dyr = [dy_ref[:, h * RET_DK:(h + 1) * RET_DK] for h in _HEADS]
        dyn = [dyr[h] * (g[h] * sg[h]) for h in _HEADS]
        dg = [dyr[h] * norm[h][0] * (sg[h] * (1.0 + g[h] * (1.0 - sg[h]))) for h in _HEADS]
        dy = [norm[h][1] * (dyn[h] - jnp.mean(dyn[h], axis=1, keepdims=True)
                            - norm[h][0] * jnp.mean(dyn[h] * norm[h][0], axis=1, keepdims=True)) for h in _HEADS]
        dyb, dsnb = [_bf(a) for a in dy], [_bf(a) for a in dsn]
        ds = [_bf(_dot_nt(dyb[h], vb[h]) * intra_ref[h]) for h in _HEADS]
        t = [_bf(dy[h] * qd_ref[h]) for h in _HEADS]
        dv = [_dot_tn(_bf(f["sc"][h]), dyb[h]) + _dot(_bf(kr[h] * kd_ref[h]), dsnb[h]) for h in _HEADS]
        dqr = [_dot(ds[h], _bf(kr[h])) + _dot_nt(t[h], _bf(st[h])) for h in _HEADS]
        dkr = [_dot_tn(ds[h], qrb[h]) + _dot_nt(vb[h], dsnb[h]) * kd_ref[h] for h in _HEADS]
        new_ds = [dsn[h] * cd_ref[h] + _dot_tn(qrb[h], t[h]) for h in _HEADS]
        for h in _HEADS:
            lo = h * RET_DK
            dstate[h] = new_ds[h]
            dp_ref[:, lo:lo + RET_DK] = _bf(_rope_t(dqr[h], cos, sin))
            dp_ref[:, RET_W + lo:RET_W + lo + RET_DK] = _bf(_rope_t(dkr[h], cos, sin) * (RET_DK ** -0.5))
            dp_ref[:, 2 * RET_W + lo:2 * RET_W + lo + RET_DK] = _bf(dv[h])
            dp_ref[:, 3 * RET_W + lo:3 * RET_W + lo + RET_DK] = _bf(dg[h])

    sp = _ret_specs(nc, True)
    return pl.pallas_call(
        body,
        out_shape=jax.ShapeDtypeStruct((s, 4 * RET_W), BF16),
        grid=(nc,),
        in_specs=[sp["proj"], sp["half"], sp["half"], sp["intra"], sp["dec"], sp["dec"], sp["cd"],
                  sp["state"], sp["wide"]],
        out_specs=sp["proj"],
        scratch_shapes=[pltpu.VMEM((RET_HEADS, RET_DK, RET_DK), F32)],
        compiler_params=_params("arbitrary"), name=name)(p_r, cos, sin, intra_t, qd_t, kd_t, cd_t, states, dy_r)


def _conv_taps(ext, w):
    acc = w[SSD_CONV - 1:SSD_CONV] * ext
    for j in range(SSD_CONV - 1):
        acc = acc + w[j:j + 1] * pltpu.roll(ext, SSD_CONV - 1 - j, axis=0)
    return acc


def _conv_fwd(xbc_raw, conv_w, conv_b, name):
    s = xbc_raw.shape[0]
    t8 = CONV_TILE // 8

    def body(cur_ref, prev_ref, w_ref, b_ref, o_ref):
        keep = (pl.program_id(0) > 0).astype(F32)
        ext = jnp.concatenate([prev_ref[...] * keep, cur_ref[...]], axis=0)
        u = _conv_taps(ext, w_ref[...])[8:] + b_ref[...]
        o_ref[...] = u * _sigmoid(u)

    return pl.pallas_call(
        body, out_shape=jax.ShapeDtypeStruct((s, SSD_CD), F32), grid=(s // CONV_TILE,),
        in_specs=[_row_spec(SSD_CD, CONV_TILE),
                  pl.BlockSpec((8, SSD_CD), lambda i: (jnp.maximum(i * t8 - 1, 0), 0)),
                  _full_spec((SSD_CONV, SSD_CD)), _full_spec((1, SSD_CD))],
        out_specs=_row_spec(SSD_CD, CONV_TILE),
        compiler_params=_params("parallel"), name=name)(xbc_raw, xbc_raw, conv_w, conv_b)


def _conv_bwd(xbc_raw, conv_w, conv_b, dact, name):
    s = xbc_raw.shape[0]
    nt = s // CONV_TILE
    t8 = CONV_TILE // 8
    rows = CONV_TILE + 8

    def body(cur_ref, prev_ref, next_ref, w_ref, b_ref, d_ref, dnext_ref, dx_ref, dw_ref, db_ref):
        i = pl.program_id(0)

        @pl.when(i == 0)
        def _():
            dw_ref[...] = jnp.zeros_like(dw_ref)
            db_ref[...] = jnp.zeros_like(db_ref)

        keep_prev = (i > 0).astype(F32)
        keep_next = (i < nt - 1).astype(F32)
        w = w_ref[...]
        ext = jnp.concatenate([prev_ref[...] * keep_prev, cur_ref[...], next_ref[...]], axis=0)
        shifted = [pltpu.roll(ext, SSD_CONV - 1 - j, axis=0)[8:] for j in range(SSD_CONV - 1)] + [ext[8:]]
        u = b_ref[...]
        for j in range(SSD_CONV):
            u = u + w[j:j + 1] * shifted[j]
        sg = _sigmoid(u)
        dact = jnp.concatenate([d_ref[...], dnext_ref[...] * keep_next], axis=0)
        du = dact * (sg * (1.0 + u * (1.0 - sg)))
        dx = w[SSD_CONV - 1:SSD_CONV] * du
        for j in range(SSD_CONV - 1):
            dx = dx + w[j:j + 1] * pltpu.roll(du, rows - (SSD_CONV - 1 - j), axis=0)
        dx_ref[...] = _bf(dx[:CONV_TILE])
        duc = du[:CONV_TILE]
        for j in range(SSD_CONV):
            dw_ref[j:j + 1, :] += jnp.sum(duc * shifted[j][:CONV_TILE], axis=0, keepdims=True)
        db_ref[...] += jnp.sum(duc, axis=0, keepdims=True)

    row = _row_spec(SSD_CD, CONV_TILE)
    prev = pl.BlockSpec((8, SSD_CD), lambda i: (jnp.maximum(i * t8 - 1, 0), 0))
    nxt = pl.BlockSpec((8, SSD_CD), lambda i: (jnp.minimum((i + 1) * t8, s // 8 - 1), 0))
    return pl.pallas_call(
        body,
        out_shape=(jax.ShapeDtypeStruct((s, SSD_CD), BF16), jax.ShapeDtypeStruct((SSD_CONV, SSD_CD), F32),
                   jax.ShapeDtypeStruct((1, SSD_CD), F32)),
        grid=(nt,),
        in_specs=[row, prev, nxt, _full_spec((SSD_CONV, SSD_CD)), _full_spec((1, SSD_CD)), row, nxt],
        out_specs=(row, _full_spec((SSD_CONV, SSD_CD)), _full_spec((1, SSD_CD))),
        compiler_params=_params("arbitrary"), name=name)(xbc_raw, xbc_raw, xbc_raw, conv_w, conv_b, dact, dact)


def _head_select():
    e = np.zeros((HEAD_PAD, SSD_W), np.float32)
    for h in range(SSD_HEADS):
        e[h, h * SSD_P:(h + 1) * SSD_P] = 1.0
    return jnp.asarray(e, BF16), jnp.asarray(e.T, BF16)


def _pad_heads(v):
    return jnp.pad(v.reshape(1, SSD_HEADS).astype(F32), ((0, 0), (0, HEAD_PAD - SSD_HEADS)))


def _ssd_masks():
    r = _iota((CHUNK, SSD_QW), 0)
    c = _iota((CHUNK, SSD_QW), 1) % SSD_P
    itile = (r == c).astype(F32)
    ctile = r >= c
    rb = _iota((SSD_QW, SSD_QW), 0) // SSD_P
    cb = _iota((SSD_QW, SSD_QW), 1) // SSD_P
    return itile, ctile, rb == cb


def _ssd_prep(xbc_ref, dtr_ref, dtb_ref, alog_ref, e_ref=None, acol_ref=None, dte_ref=None):
    xs = xbc_ref[:, :SSD_W]
    u = dtr_ref[...] + dtb_ref[...]
    dt = _softplus(u)
    nexp = -jnp.exp(alog_ref[...])
    a = dt * nexp
    if e_ref is not None:
        tril = _bf((_iota((CHUNK, CHUNK), 0) >= _iota((CHUNK, CHUNK), 1)).astype(F32))
        e = e_ref[...]
        acol = _dot_exact_l(_dot_exact_r(tril, a), e)
        dte = _dot_exact_l(dt, e)
    else:
        acol, dte = acol_ref[...], dte_ref[...]
    alast = acol[CHUNK - 1:CHUNK, :]
    return dict(xs=xs, u=u, dt=dt, nexp=nexp, a=a, acol=acol, dte=dte, xdt=xs * dte,
                ea=jnp.exp(acol), tail=jnp.exp(alast - acol), eal=jnp.exp(alast))


def _tile4(x):
    return jnp.concatenate([x, x, x, x], axis=0)


def _fold4(x):
    return x[0:CHUNK] + x[CHUNK:2 * CHUNK] + x[2 * CHUNK:3 * CHUNK] + x[3 * CHUNK:4 * CHUNK]


def _ssd_specs(nc, rev):
    cidx = (lambda c: nc - 1 - c) if rev else (lambda c: c)
    return dict(
        xbc=pl.BlockSpec((CHUNK, SSD_CD), lambda c: (cidx(c), 0)),
        dt=pl.BlockSpec((CHUNK, HEAD_PAD), lambda c: (cidx(c), 0)),
        wide=pl.BlockSpec((CHUNK, SSD_W), lambda c: (cidx(c), 0)),
        state=pl.BlockSpec((1, SSD_N, SSD_W), lambda c: (cidx(c), 0, 0)),
        head=_full_spec((1, HEAD_PAD)),
        roww=_full_spec((1, SSD_W)),
        e=_full_spec((HEAD_PAD, SSD_W)),
        et=_full_spec((SSD_W, HEAD_PAD)),
    )


_GROUPS = range(SSD_GROUPS)
_GROUP_LANES = [slice(g * SSD_GW, (g + 1) * SSD_GW) for g in _GROUPS]
_QUADS = [(g, slice(g * SSD_GW + q * SSD_QW, g * SSD_GW + (q + 1) * SSD_QW), slice(q * SSD_QW, (q + 1) * SSD_QW))
          for g in _GROUPS for q in range(SSD_GW // SSD_QW)]


def _ssd_bc(xbc_ref):
    bg = [_bf(xbc_ref[:, B_OFF + g * SSD_N:B_OFF + (g + 1) * SSD_N]) for g in _GROUPS]
    cg = [_bf(xbc_ref[:, C_OFF + g * SSD_N:C_OFF + (g + 1) * SSD_N]) for g in _GROUPS]
    return bg, cg


def _ssd_decay(aq, itile, ctile):
    arow = jnp.sum(aq * itile, axis=0, keepdims=True)
    return jnp.exp(jnp.where(ctile, aq - arow, -jnp.inf))


def _ssd_blockdiag(xq, bdmask):
    return jnp.where(bdmask, _tile4(_bf(xq)), jnp.zeros((), BF16))


def _ssd_fwd(xbc, dt_raw, z, dt_bias, a_log, dske, norm_w, name):
    s = xbc.shape[0]
    nc = s // CHUNK
    e_sel, _ = _head_select()

    def body(xbc_ref, dtr_ref, z_ref, dtb_ref, alog_ref, dske_ref, nw_ref, e_ref,
             yraw_ref, ys_ref, st_ref, acol_ref, dte_ref, state):
        @pl.when(pl.program_id(0) == 0)
        def _():
            state[...] = jnp.zeros_like(state)

        p = _ssd_prep(xbc_ref, dtr_ref, dtb_ref, alog_ref, e_ref=e_ref)
        acol_ref[...] = p["acol"]
        dte_ref[...] = p["dte"]
        itile, ctile, bdmask = _ssd_masks()
        dske = dske_ref[...]
        st_all = state[...]
        st_ref[0] = st_all
        stg = [st_all[:, gs] for gs in _GROUP_LANES]
        bg, cg = _ssd_bc(xbc_ref)
        cbt = [_dot_nt(cg[g], _tile4(bg[g])) for g in _GROUPS]
        ys = [_dot(cg[g], _bf(stg[g])) for g in _GROUPS]
        dq = [_ssd_decay(p["acol"][:, ql], itile, ctile) for _, ql, _ in _QUADS]
        xbd = [_ssd_blockdiag(p["xdt"][:, ql], bdmask) for _, ql, _ in _QUADS]
        yq = [_dot(_bf(cbt[g] * dq[i]), xbd[i]) + ys[g][:, qs] * p["ea"][:, ql] + dske[:, ql] * p["xs"][:, ql]
              for i, (g, ql, qs) in enumerate(_QUADS)]
        new_st = [stg[g] * p["eal"][:, gs] + _dot_tn(bg[g], _bf(p["xdt"][:, gs] * p["tail"][:, gs]))
                  for g, gs in enumerate(_GROUP_LANES)]
        y = jnp.concatenate(yq, axis=1)
        state[...] = jnp.concatenate(new_st, axis=1)
        yraw_ref[...] = y
        zv = z_ref[...]
        t = y * (zv * _sigmoid(zv))
        rstd = lax.rsqrt(jnp.mean(t * t, axis=1, keepdims=True) + EPS)
        ys_ref[...] = _bf(t * rstd * nw_ref[...])

    sp = _ssd_specs(nc, False)
    wide = jax.ShapeDtypeStruct((s, SSD_W), F32)
    return pl.pallas_call(
        body,
        out_shape=(wide, jax.ShapeDtypeStruct((s, SSD_W), BF16), jax.ShapeDtypeStruct((nc, SSD_N, SSD_W), F32),
                   wide, wide),
        grid=(nc,),
        in_specs=[sp["xbc"], sp["dt"], sp["wide"], sp["head"], sp["head"], sp["roww"], sp["roww"], sp["e"]],
        out_specs=(sp["wide"], sp["wide"], sp["state"], sp["wide"], sp["wide"]),
        scratch_shapes=[pltpu.VMEM((SSD_N, SSD_W), F32)],
        compiler_params=_params("arbitrary"), name=name,
    )(xbc, dt_raw, z, dt_bias, a_log, dske, norm_w, e_sel)


def _ssd_bwd(xbc, dt_raw, z, yraw, states, dys, acol, dte, dt_bias, a_log, dske, norm_w, name, rider):
    s = xbc.shape[0]
    nc = s // CHUNK
    _, et_sel = _head_select()
    n_in, n_out, n_scratch = 13, 7, 2
    n_rin, n_rout = len(rider.inputs), len(rider.out_shapes)

    def body(*refs):
        ins, refs = refs[:n_in], refs[n_in:]
        r_ins, refs = refs[:n_rin], refs[n_rin:]
        outs, refs = refs[:n_out], refs[n_out:]
        r_outs, refs = refs[:n_rout], refs[n_rout:]
        ride = functools.partial(rider.run, pl.program_id(0), nc, r_ins, r_outs, refs[n_scratch], refs[n_scratch + 1])
        ride(last=False)
        compute(*ins, *outs, *refs[:n_scratch])
        ride(last=True)

    def compute(xbc_ref, dtr_ref, z_ref, yraw_ref, st_ref, dys_ref, acol_ref, dte_ref, dtb_ref, alog_ref,
                dske_ref, nw_ref, et_ref, dxbc_ref, dz_ref, ddtr_ref, ddtb_ref, dalog_ref, ddsk_ref, dnw_ref,
                dstate, dsk_acc):
        @pl.when(pl.program_id(0) == 0)
        def _():
            dstate[...] = jnp.zeros_like(dstate)
            dsk_acc[...] = jnp.zeros_like(dsk_acc)
            ddtb_ref[...] = jnp.zeros_like(ddtb_ref)
            dalog_ref[...] = jnp.zeros_like(dalog_ref)
            dnw_ref[...] = jnp.zeros_like(dnw_ref)

        p = _ssd_prep(xbc_ref, dtr_ref, dtb_ref, alog_ref, acol_ref=acol_ref, dte_ref=dte_ref)
        itile, ctile, bdmask = _ssd_masks()
        et = et_ref[...]
        dske = dske_ref[...]
        last_row = (_iota((CHUNK, 1), 0) == CHUNK - 1).astype(F32)

        zv, y, dysv = z_ref[...], yraw_ref[...], dys_ref[...]
        sz = _sigmoid(zv)
        silz = zv * sz
        t = y * silz
        rstd = lax.rsqrt(jnp.mean(t * t, axis=1, keepdims=True) + EPS)
        that = t * rstd
        dthat = dysv * nw_ref[...]
        dt_ = rstd * (dthat - that * jnp.mean(dthat * that, axis=1, keepdims=True))
        dnw_ref[...] += jnp.sum(dysv * that, axis=0, keepdims=True)
        dz_ref[...] = _bf(dt_ * y * (sz * (1.0 + zv * (1.0 - sz))))
        dy = dt_ * silz
        dsk_acc[...] += jnp.sum(dy * p["xs"], axis=0, keepdims=True)

        @pl.when(pl.program_id(0) == nc - 1)
        def _():
            ddsk_ref[...] = _dot_exact_l(jnp.broadcast_to(dsk_acc[...], (8, SSD_W)), et)[0:1]

        gl = _GROUP_LANES
        st_all, dsn_all = st_ref[0], dstate[...]
        stg = [st_all[:, gs] for gs in gl]
        dsn = [dsn_all[:, gs] for gs in gl]
        stgb, dsnb = [_bf(a) for a in stg], [_bf(a) for a in dsn]
        bg, cg = _ssd_bc(xbc_ref)
        btile = [_tile4(a) for a in bg]
        cbt = [_dot_nt(cg[g], btile[g]) for g in _GROUPS]
        dyg = [dy[:, gs] for gs in gl]
        eag, tailg = [p["ea"][:, gs] for gs in gl], [p["tail"][:, gs] for gs in gl]
        xdtg, ealg = [p["xdt"][:, gs] for gs in gl], [p["eal"][:, gs] for gs in gl]
        dys_g = [_bf(dyg[g] * eag[g]) for g in _GROUPS]
        ysg = [_dot(cg[g], stgb[g]) for g in _GROUPS]
        dc0 = [_dot_nt(dys_g[g], stgb[g]) for g in _GROUPS]
        dst = [_dot_tn(cg[g], dys_g[g]) for g in _GROUPS]
        dwt = [_dot(bg[g], dsnb[g]) for g in _GROUPS]
        db0 = [_dot_nt(_bf(xdtg[g] * tailg[g]), dsnb[g]) for g in _GROUPS]
        dtl = [dwt[g] * xdtg[g] * tailg[g] for g in _GROUPS]
        dal_row = [jnp.sum(dtl[g], axis=0, keepdims=True) + jnp.sum(dsn[g] * stg[g], axis=0, keepdims=True) * ealg[g]
                   for g in _GROUPS]
        dstate[...] = jnp.concatenate([dst[g] + dsn[g] * ealg[g] for g in _GROUPS], axis=1)
        dq = [_ssd_decay(p["acol"][:, ql], itile, ctile) for _, ql, _ in _QUADS]
        mq = [cbt[g] * dq[i] for i, (g, _, _) in enumerate(_QUADS)]
        xbd = [_ssd_blockdiag(p["xdt"][:, ql], bdmask) for _, ql, _ in _QUADS]
        dyq = [_bf(dy[:, ql]) for _, ql, _ in _QUADS]
        dm = [_dot_nt(dyq[i], xbd[i]) for i in range(len(_QUADS))]
        dxdt_q = [_fold4(jnp.where(bdmask, _dot_tn(_bf(mq[i]), dyq[i]), 0.0)) for i in range(len(_QUADS))]
        eq = [dm[i] * mq[i] for i in range(len(_QUADS))]
        dacol_q = [eq[i] - itile * jnp.sum(eq[i], axis=0, keepdims=True) for i in range(len(_QUADS))]
        dcbt = [_bf(dm[i] * dq[i]) for i in range(len(_QUADS))]
        dc_q = [_dot(dcbt[i], btile[g]) for i, (g, _, _) in enumerate(_QUADS)]
        db_q = [_fold4(_dot_tn(dcbt[i], cg[g])) for i, (g, _, _) in enumerate(_QUADS)]
        dxdt = jnp.concatenate(dxdt_q, axis=1) + jnp.concatenate([dwt[g] * tailg[g] for g in _GROUPS], axis=1)
        dacol = (jnp.concatenate(dacol_q, axis=1)
                 + jnp.concatenate([dyg[g] * (ysg[g] * eag[g]) - dtl[g] + last_row * dal_row[g] for g in _GROUPS],
                                   axis=1))
        dxbc_ref[:, :SSD_W] = dy * dske + dxdt * p["dte"]
        dxbc_ref[:, B_OFF:C_OFF] = jnp.concatenate([db0[g] + db_q[2 * g] + db_q[2 * g + 1] for g in _GROUPS], axis=1)
        dxbc_ref[:, C_OFF:] = jnp.concatenate([dc0[g] + dc_q[2 * g] + dc_q[2 * g + 1] for g in _GROUPS], axis=1)

        dacum = _dot_exact_l(dacol, et)
        triu = _bf((_iota((CHUNK, CHUNK), 1) >= _iota((CHUNK, CHUNK), 0)).astype(F32))
        da = _dot_exact_r(triu, dacum)
        ddt = _dot_exact_l(dxdt * p["xs"], et) + da * p["nexp"]
        dalog_ref[...] += jnp.sum(da * p["a"], axis=0, keepdims=True)
        du = ddt * _sigmoid(p["u"])
        ddtr_ref[...] = _bf(du)
        ddtb_ref[...] += jnp.sum(du, axis=0, keepdims=True)

    sp = _ssd_specs(nc, True)
    return pl.pallas_call(
        body,
        out_shape=(jax.ShapeDtypeStruct((s, SSD_CD), F32), jax.ShapeDtypeStruct((s, SSD_W), BF16),
                   jax.ShapeDtypeStruct((s, HEAD_PAD), BF16), jax.ShapeDtypeStruct((1, HEAD_PAD), F32),
                   jax.ShapeDtypeStruct((1, HEAD_PAD), F32), jax.ShapeDtypeStruct((1, HEAD_PAD), F32),
                   jax.ShapeDtypeStruct((1, SSD_W), F32), *rider.out_shapes),
        grid=(nc,),
        in_specs=[sp["xbc"], sp["dt"], sp["wide"], sp["wide"], sp["state"], sp["wide"], sp["wide"], sp["wide"],
                  sp["head"], sp["head"], sp["roww"], sp["roww"], sp["et"], *rider.in_specs()],
        out_specs=(sp["xbc"], sp["wide"], sp["dt"], sp["head"], sp["head"], sp["head"], sp["roww"],
                   *rider.out_specs()),
        scratch_shapes=[pltpu.VMEM((SSD_N, SSD_W), F32), pltpu.VMEM((1, SSD_W), F32), *rider.scratch()],
        compiler_params=_params("arbitrary"), name=name,
    )(xbc, dt_raw, z, yraw, states, dys, acol, dte, dt_bias, a_log, dske, norm_w, et_sel, *rider.inputs)


W_IN_SHARD = IN_PROJ // N_DEV
W_IN_ROW_PARTS = ((0, 1152), (1152, 832), (1984, 840))


def _device_step(x, pos_col, target, norm1_w, conv_w_shard, conv_b, dt_bias, a_log, d_skip, ssd_norm_w, norm_f_w,
                 w_r, w_z, w_xbc, w_dt, w_g, br_ret_shard, br_ssd_shard, out_shard):
    inv_freq = jnp.asarray(ROPE_THETA ** (-np.arange(ROPE_HALF, dtype=np.float64) / ROPE_HALF), F32).reshape(1, ROPE_HALF)
    dtb, alog = _pad_heads(dt_bias), _pad_heads(a_log)
    dske = jnp.repeat(d_skip.reshape(SSD_HEADS).astype(F32), SSD_P).reshape(1, SSD_W)
    my_core = lax.axis_index("c")

    h = _rmsnorm_fwd(x, norm1_w, "rmsnorm1_fwd")
    gather = _gather_rider([br_ret_shard, br_ssd_shard, out_shard, conv_w_shard], forward_at=0.7)
    p_r, all_ret, all_ssd, all_out, all_conv = _matmul(h, w_r, "nt", F32, "proj_ret", rider=gather)
    w_br_ret = all_ret.reshape(RET_W, D_MODEL)
    w_br_ssd = all_ssd.reshape(SSD_W, D_MODEL)
    w_out = all_out.reshape(D_MODEL, D_MODEL)
    conv_w = all_conv.transpose(1, 0, 2).reshape(SSD_CONV, SSD_CD)
    p_z = _matmul(h, w_z, "nt", F32, "proj_z")
    p_xbc = _matmul(h, w_xbc, "nt", F32, "proj_xbc")
    p_dt = _matmul(h, w_dt, "nt", F32, "proj_dt")
    p_g = _matmul(h, w_g, "nt", BF16, "proj_gates")
    cos, sin = _rope_tables(pos_col, inv_freq, "rope_tables")
    y_r, ret_states = _retention_fwd(p_r, cos, sin, "retention_fwd")
    xbc_act = _conv_fwd(p_xbc, conv_w, conv_b, "conv_fwd")
    y_raw, y_s, ssd_states, acol, dte = _ssd_fwd(xbc_act, p_dt, p_z, dtb, alog, dske, ssd_norm_w, "ssd_fwd")
    pr = _matmul(y_r, w_br_ret, "nn", BF16, "branch_ret")
    ps = _matmul(y_s, w_br_ssd, "nn", BF16, "branch_ssd")
    merged = _merge_fwd(p_g, pr, ps, "merge_fwd")
    o = _matmul(merged, w_out, "nn", F32, "out_proj")
    dx2, g_norm_f, loss_acc = _final_fwd_bwd(x, o, norm_f_w, target, "final_norm_loss")

    g_w_out = _matmul(merged, dx2, "tn", BF16, "grad_w_out")
    dmerged = _matmul(dx2, w_out, "nt", F32, "d_merged")
    dpr, dps, dp_g = _merge_bwd(p_g, pr, ps, dmerged, "merge_bwd")
    g_w_br_ret = _matmul(y_r, dpr, "tn", BF16, "grad_w_br_ret")
    g_w_br_ssd = _matmul(y_s, dps, "tn", BF16, "grad_w_br_ssd")
    dy_r = _matmul(dpr, w_br_ret, "nt", F32, "d_y_ret")
    dy_s = _matmul(dps, w_br_ssd, "nt", F32, "d_y_ssd")
    scatter = _scatter_rider([g_w_out.reshape(N_DEV, -1, D_MODEL), g_w_br_ret.reshape(N_DEV, -1, D_MODEL),
                              g_w_br_ssd.reshape(N_DEV, -1, D_MODEL)])
    dxbc_act, dp_z, dp_dt, g_dtb, g_alog, g_dsk, g_ssd_norm, got_out, got_ret, got_ssd = _ssd_bwd(
        xbc_act, p_dt, p_z, y_raw, ssd_states, dy_s, acol, dte, dtb, alog, dske, ssd_norm_w, "ssd_bwd", scatter)
    dp_xbc, g_conv_w, g_conv_b = _conv_bwd(p_xbc, conv_w, conv_b, dxbc_act, "conv_bwd")
    dp_r = _retention_bwd(p_r, cos, sin, ret_states, dy_r, "retention_bwd")
    g_w_in = jnp.concatenate([
        _matmul(dp_r, h, "tn", BF16, "grad_w_ret"),
        _matmul(dp_z, h, "tn", BF16, "grad_w_z"),
        _matmul(dp_xbc, h, "tn", BF16, "grad_w_xbc"),
        _matmul(dp_dt, h, "tn", BF16, "grad_w_dt")[:SSD_HEADS],
        _matmul(dp_g, h, "tn", BF16, "grad_w_gates"),
    ], axis=0)
    blocks = g_w_in.reshape(N_DEV, W_IN_SHARD, D_MODEL)
    dh, from_sibling = _matmul(dp_r, w_r, "nn", F32, "d_h_ret", rider=_sibling_rider(blocks))
    own = lax.dynamic_index_in_dim(blocks.reshape(4, 2, W_IN_SHARD, D_MODEL), my_core, axis=1, keepdims=False)
    chip_sum = _add_bf16(own.reshape(4 * W_IN_SHARD, D_MODEL), from_sibling.reshape(4 * W_IN_SHARD, D_MODEL),
                         "w_in_chip_sum").reshape(4, W_IN_SHARD, D_MODEL)
    carriers = (("d_h_xbc", dp_xbc, w_xbc), ("d_h_gates", dp_g, w_g), ("d_h_z", dp_z, w_z))
    parts = []
    for (row0, rows), (nm, dp, w) in zip(W_IN_ROW_PARTS, carriers):
        dh, landed = _matmul(dp, w, "nn", F32, nm, add=dh, rider=_chip_rider(chip_sum, row0, rows))
        parts.append(_sum_slots(landed, "w_in_sum_rows_%d" % row0))
    dh = _matmul(dp_dt, w_dt, "nn", F32, "d_h_dt", add=dh)
    grad_x, g_norm1 = _rmsnorm_bwd(x, norm1_w, dh, dx2, "rmsnorm1_bwd")
    small = dict(norm1_w=g_norm1, conv_w=g_conv_w, conv_b=g_conv_b, dt_bias=g_dtb[:, :SSD_HEADS],
                 a_log=g_alog[:, :SSD_HEADS], d_skip=g_dsk[:, :SSD_HEADS], ssd_norm_w=g_ssd_norm,
                 norm_f_w=g_norm_f)
    big = dict(w_in=jnp.concatenate(parts, axis=0), w_br_ret=_sum_slots(got_ret, "w_br_ret_sum"),
               w_br_ssd=_sum_slots(got_ssd, "w_br_ssd_sum"), w_out=_sum_slots(got_out, "w_out_sum"))
    return loss_acc[0, 0], grad_x, small, big


def _all_gather(shard, name):
    rider = _gather_rider([shard], 0.0)

    def body(x_ref, out_ref, send_sems, recv_sems):
        for _, fn in rider.phases:
            fn((x_ref,), (out_ref,), send_sems, recv_sems)

    return pl.pallas_call(
        body, out_shape=rider.out_shapes[0], in_specs=rider.in_specs(), out_specs=rider.out_specs()[0],
        scratch_shapes=rider.scratch(), name=name)(shard)


def _all_reduce_small(vec, name):
    r, c = vec.shape

    def body(x_ref, out_ref, land, send_sems, recv_sems):
        x, y, cc = _mesh_pos()
        my_idx = 4 * x + 2 * y + cc
        land[my_idx] = x_ref[...]
        copies = []
        for k in range(1, N_DEV):
            px, py, pc = x ^ (k >> 2), y ^ ((k >> 1) & 1), cc ^ (k & 1)
            cp = pltpu.make_async_remote_copy(
                src_ref=x_ref, dst_ref=land.at[my_idx],
                send_sem=send_sems.at[k - 1], recv_sem=recv_sems.at[k - 1],
                device_id=(px, py, pc), device_id_type=MESH)
            cp.start()
            copies.append(cp)
        for k in range(1, N_DEV):
            px, py, pc = x ^ (k >> 2), y ^ ((k >> 1) & 1), cc ^ (k & 1)
            pltpu.make_async_remote_copy(
                src_ref=x_ref, dst_ref=land.at[4 * px + 2 * py + pc],
                send_sem=send_sems.at[k - 1], recv_sem=recv_sems.at[k - 1],
                device_id=(px, py, pc), device_id_type=MESH).wait_recv()
        for cp in copies:
            cp.wait_send()
        acc = land[0]
        for i in range(1, N_DEV):
            acc = acc + land[i]
        out_ref[...] = acc

    return pl.pallas_call(
        body,
        out_shape=jax.ShapeDtypeStruct((r, c), F32),
        in_specs=[pl.BlockSpec(memory_space=pltpu.VMEM)],
        out_specs=pl.BlockSpec(memory_space=pltpu.VMEM),
        scratch_shapes=[pltpu.VMEM((N_DEV, r, c), F32), pltpu.SemaphoreType.DMA((7,)),
                        pltpu.SemaphoreType.DMA((7,))],
        name=name)(vec)


_SMALL = ("norm1_w", "conv_w", "conv_b", "dt_bias", "a_log", "d_skip", "ssd_norm_w", "norm_f_w")
_SMALL_COLS = 128
_W_IN = "w_in"
_WEIGHTS = ("norm1_w", "w_in", "conv_w", "conv_b", "dt_bias", "a_log", "d_skip", "ssd_norm_w",
            "w_br_ret", "w_br_ssd", "w_out", "norm_f_w")


def _pack(parts):
    flat = jnp.concatenate([p.reshape(-1).astype(F32) for p in parts])
    rows = -(-flat.shape[0] // (8 * _SMALL_COLS)) * 8
    return jnp.pad(flat, (0, rows * _SMALL_COLS - flat.shape[0])).reshape(rows, _SMALL_COLS)


def _unpack(packed, shapes):
    flat = packed.reshape(-1)
    out, off = [], 0
    for shp in shapes:
        n = int(np.prod(shp))
        out.append(flat[off:off + n].reshape(shp))
        off += n
    return out


def kernel(x, positions, norm1_w, w_in, conv_w, conv_b, dt_bias, a_log, d_skip, ssd_norm_w, w_br_ret, w_br_ssd, w_out, norm_f_w, loss_target, m_norm1_w, m_w_in, m_conv_w, m_conv_b, m_dt_bias, m_a_log, m_d_skip, m_ssd_norm_w, m_w_br_ret, m_w_br_ssd, m_w_out, m_norm_f_w, v_norm1_w, v_w_in, v_conv_w, v_conv_b, v_dt_bias, v_a_log, v_d_skip, v_ssd_norm_w, v_w_br_ret, v_w_br_ssd, v_w_out, v_norm_f_w):
    w = dict(norm1_w=norm1_w, w_in=w_in, conv_w=conv_w, conv_b=conv_b, dt_bias=dt_bias, a_log=a_log,
             d_skip=d_skip, ssd_norm_w=ssd_norm_w, w_br_ret=w_br_ret, w_br_ssd=w_br_ssd, w_out=w_out,
             norm_f_w=norm_f_w)
    m = dict(norm1_w=m_norm1_w, w_in=m_w_in, conv_w=m_conv_w, conv_b=m_conv_b, dt_bias=m_dt_bias,
             a_log=m_a_log, d_skip=m_d_skip, ssd_norm_w=m_ssd_norm_w, w_br_ret=m_w_br_ret,
             w_br_ssd=m_w_br_ssd, w_out=m_w_out, norm_f_w=m_norm_f_w)
    v = dict(norm1_w=v_norm1_w, w_in=v_w_in, conv_w=v_conv_w, conv_b=v_conv_b, dt_bias=v_dt_bias,
             a_log=v_a_log, d_skip=v_d_skip, ssd_norm_w=v_ssd_norm_w, w_br_ret=v_w_br_ret,
             w_br_ssd=v_w_br_ssd, w_out=v_w_out, norm_f_w=v_norm_f_w)
    s = x.shape[1]
    my_idx = 4 * lax.axis_index("x") + 2 * lax.axis_index("y") + lax.axis_index("c")

    w[_W_IN], m[_W_IN], v[_W_IN] = w_in[0].T, m_w_in[0].T, v_w_in[0].T
    w_in_all = _all_gather(_cast_bf16(w[_W_IN], "cast_w_in"), "gather_w_in").reshape(IN_PROJ, D_MODEL)
    w_dt = jnp.pad(w_in_all[OFF_DT:OFF_G], ((0, HEAD_PAD - SSD_HEADS), (0, 0)))

    loss_part, grad_x, g_small, g_big = _device_step(
        x[0], positions.reshape(s, 1), loss_target[0], norm1_w, conv_w[0], conv_b, dt_bias, a_log, d_skip,
        ssd_norm_w, norm_f_w.reshape(1, D_MODEL),
        w_in_all[:OFF_Z], w_in_all[OFF_Z:OFF_XBC], w_in_all[OFF_XBC:OFF_DT], w_dt, w_in_all[OFF_G:],
        _cast_bf16(w_br_ret[0], "cast_w_br_ret"), _cast_bf16(w_br_ssd[0], "cast_w_br_ssd"),
        _cast_bf16(w_out[0], "cast_w_out"))

    loss = lax.psum(loss_part, ("x", "y", "c"))

    small_shapes = [g_small[n].shape for n in _SMALL]
    summed = _unpack(_all_reduce_small(_pack([g_small[n] for n in _SMALL]), "allreduce_small"), small_shapes)
    grads = dict(zip(_SMALL, summed))
    conv_cols = SSD_CD // N_DEV
    grads["conv_w"] = lax.dynamic_slice_in_dim(grads["conv_w"], my_idx * conv_cols, conv_cols, axis=1)
    grads["norm_f_w"] = grads["norm_f_w"].reshape(D_MODEL)
    for n in ("norm1_w", "conv_w", "conv_b", "dt_bias", "a_log", "d_skip", "ssd_norm_w"):
        grads[n] = grads[n].reshape(w[n].shape)

    delta, new_m, new_v = {}, {}, {}
    for n in ("w_br_ret", "w_br_ssd", "w_out"):
        w[n], m[n], v[n] = w[n][0], m[n][0], v[n][0]
    for n in (_W_IN, "w_br_ret", "w_br_ssd", "w_out"):
        back = (lambda a: a.T[None]) if n == _W_IN else (lambda a: a[None])
        res = _adamw(w[n], g_big[n], m[n], v[n], "adamw_" + n)
        grads[n] = back(g_big[n])
        delta[n], new_m[n], new_v[n] = (back(a) for a in res)
    shapes = [w[n].shape for n in _SMALL]
    packed = _adamw(_pack([w[n] for n in _SMALL]), _pack([grads[n] for n in _SMALL]),
                    _pack([m[n] for n in _SMALL]), _pack([v[n] for n in _SMALL]), "adamw_small")
    for res, dst in zip(packed, (delta, new_m, new_v)):
        for n, a in zip(_SMALL, _unpack(res, shapes)):
            dst[n] = a

    return (loss, grad_x.reshape(x.shape), *[grads[n] for n in _WEIGHTS], *[delta[n] for n in _WEIGHTS],
            *[new_m[n] for n in _WEIGHTS], *[new_v[n] for n in _WEIGHTS])
```

```python
import functools

import numpy as np
import jax
import jax.numpy as jnp
from jax import lax
from jax.experimental import pallas as pl
from jax.experimental.pallas import tpu as pltpu

F32 = jnp.float32
BF16 = jnp.bfloat16

D_MODEL = 2048
CHUNK = 64
EPS = 1e-6
N_DEV = 8

RET_HEADS = 8
RET_DK = 256
RET_W = RET_HEADS * RET_DK
ROPE_THETA = 10000.0
ROPE_HALF = RET_DK // 2

SSD_W = 4096
SSD_P = 64
SSD_HEADS = 64
SSD_GROUPS = 8
SSD_N = 128
SSD_GW = SSD_W // SSD_GROUPS
SSD_QW = 256
SSD_CONV = 4
SSD_CD = SSD_W + 2 * SSD_GROUPS * SSD_N
HEAD_PAD = 128
B_OFF = SSD_W
C_OFF = SSD_W + SSD_GROUPS * SSD_N

ADAM_LR = 0.001
ADAM_B1 = 0.9
ADAM_B2 = 0.999
ADAM_EPS = 1e-08
ADAM_WD = 0.01
ADAM_STEP = 10

SPLITS = (RET_W, RET_W, RET_W, RET_W, SSD_W, SSD_CD, SSD_HEADS, D_MODEL, D_MODEL)
IN_PROJ = sum(SPLITS)
OFF_Z = 4 * RET_W
OFF_XBC = OFF_Z + SSD_W
OFF_DT = OFF_XBC + SSD_CD
OFF_G = OFF_DT + SSD_HEADS

ROW_TILE = 256
CONV_TILE = 128
MM_TILE = 1024
MM_TK = 2048

MESH = pl.DeviceIdType.MESH


def _dot(a, b):
    return lax.dot_general(a, b, (((1,), (0,)), ((), ())), preferred_element_type=F32)


def _dot_nt(a, b):
    return lax.dot_general(a, b, (((1,), (1,)), ((), ())), preferred_element_type=F32)


def _dot_tn(a, b):
    return lax.dot_general(a, b, (((0,), (0,)), ((), ())), preferred_element_type=F32)


def _bf(x):
    return x.astype(BF16)


def _split3(x):
    hi = x.astype(BF16)
    r = x - hi.astype(F32)
    mid = r.astype(BF16)
    lo = (r - mid.astype(F32)).astype(BF16)
    return hi, mid, lo


def _dot_exact_l(x, sel, pieces=3):
    hi, mid, lo = _split3(x)
    r = _dot(hi, sel) + _dot(mid, sel)
    return r + _dot(lo, sel) if pieces == 3 else r


def _dot_exact_r(sel, x):
    hi, mid, lo = _split3(x)
    return _dot(sel, hi) + _dot(sel, mid) + _dot(sel, lo)


def _sigmoid(x):
    return 1.0 / (1.0 + jnp.exp(-x))


def _softplus(x):
    return jnp.maximum(x, 0.0) + jnp.log(1.0 + jnp.exp(-jnp.abs(x)))


def _iota(shape, axis):
    return lax.broadcasted_iota(jnp.int32, shape, axis)


def _params(*sem):
    return pltpu.CompilerParams(dimension_semantics=sem)


class _Rider:
    def __init__(self, inputs, out_shapes, n_sems, phases):
        self.inputs, self.out_shapes, self.n_sems, self.phases = tuple(inputs), tuple(out_shapes), n_sems, phases

    def in_specs(self):
        return [pl.BlockSpec(memory_space=pl.ANY)] * len(self.inputs)

    def out_specs(self):
        return [pl.BlockSpec(memory_space=pl.ANY)] * len(self.out_shapes)

    def scratch(self):
        return [pltpu.SemaphoreType.DMA((self.n_sems,)), pltpu.SemaphoreType.DMA((self.n_sems,))]

    def run(self, step, n_steps, ins, outs, send_sems, recv_sems, last):
        for frac, fn in self.phases:
            if (frac >= 1.0) != last:
                continue
            at = min(int(frac * n_steps), n_steps - 1)

            @pl.when(step == at)
            def _(fn=fn):
                fn(ins, outs, send_sems, recv_sems)


def _mesh_pos():
    return lax.axis_index("x"), lax.axis_index("y"), lax.axis_index("c")


def _gather_rider(shards, relay_at, pass_at):
    n = len(shards)

    def tools(a, ins, outs, send_sems, recv_sems):
        x, y, cc = _mesh_pos()
        nbrs = [(1 - x, y), (x, 1 - y)]
        diag = (1 - x, 1 - y)
        relay_from, relay_to = (x ^ cc, y ^ (1 - cc)), (x ^ (1 - cc), y ^ cc)

        def slot(px, py, pc):
            return outs[a].at[4 * px + 2 * py + pc]

        def copy(k, block, to, src=None):
            return pltpu.make_async_remote_copy(
                src_ref=slot(*block) if src is None else src, dst_ref=slot(*block),
                send_sem=send_sems.at[8 * a + k], recv_sem=recv_sems.at[8 * a + k], device_id=to, device_id_type=MESH)

        me, sibling = (x, y, cc), (x, y, 1 - cc)
        return dict(
            mine=lambda: pltpu.make_async_copy(ins[a], slot(*me), send_sems.at[8 * a + 7]),
            first=lambda: [copy(0, me, sibling, src=ins[a])] + [copy(1 + j, me, (*chip, cc), src=ins[a])
                                                                for j, chip in enumerate(nbrs)],
            landed=lambda: [copy(1 + j, (*chip, cc), me) for j, chip in enumerate(nbrs)],
            relay=lambda: copy(3, (*relay_from, cc), (*relay_to, cc)),
            relayed_in=lambda: copy(3, (*diag, cc), me),
            passed=lambda: [copy(4 + j, (*chip, cc), sibling) for j, chip in enumerate(nbrs)],
            passed_diag=lambda: copy(6, (*diag, cc), sibling),
            from_sibling=lambda: [copy(0, sibling, me)] + [copy(4 + j, (*chip, 1 - cc), me)
                                                           for j, chip in enumerate(nbrs + [diag])])

    def start(*refs):
        for a in range(n):
            t = tools(a, *refs)
            t["mine"]().start()
            for cp in t["first"]():
                cp.start()

    def relay(*refs):
        for a in range(n):
            t = tools(a, *refs)
            for got in t["landed"]():
                got.wait_recv()
            t["relay"]().start()
            for cp in t["passed"]():
                cp.start()

    def pass_diag(*refs):
        for a in range(n):
            t = tools(a, *refs)
            t["relayed_in"]().wait_recv()
            t["passed_diag"]().start()

    def finish(*refs):
        for a in range(n):
            t = tools(a, *refs)
            for cp in t["from_sibling"]():
                cp.wait_recv()
            for cp in t["first"]() + [t["relay"]()] + t["passed"]() + [t["passed_diag"]()]:
                cp.wait_send()
            t["mine"]().wait()

    outs = [jax.ShapeDtypeStruct((N_DEV,) + s.shape, s.dtype) for s in shards]
    return _Rider(shards, outs, 8 * n, ((0.0, start), (relay_at, relay), (pass_at, pass_diag), (1.0, finish)))


def _scatter_rider(blocks):
    n = len(blocks)

    def copies(a, ins, outs, send_sems, recv_sems, with_back):
        x, y, cc = _mesh_pos()
        my_idx = 4 * x + 2 * y + cc
        mine = pltpu.make_async_copy(ins[a].at[my_idx], outs[a].at[my_idx], send_sems.at[8 * a + 7])
        out, back = [], []
        for k in range(1, N_DEV):
            px, py, pc = x ^ (k >> 2), y ^ ((k >> 1) & 1), cc ^ (k & 1)
            peer_idx = 4 * px + 2 * py + pc
            sems = dict(send_sem=send_sems.at[8 * a + k - 1], recv_sem=recv_sems.at[8 * a + k - 1],
                        device_id=(px, py, pc), device_id_type=MESH)
            out.append(pltpu.make_async_remote_copy(src_ref=ins[a].at[peer_idx], dst_ref=outs[a].at[my_idx], **sems))
            if with_back:
                back.append(pltpu.make_async_remote_copy(src_ref=ins[a].at[my_idx], dst_ref=outs[a].at[peer_idx], **sems))
        return mine, out, back

    def start(*refs):
        for a in range(n):
            mine, out, _ = copies(a, *refs, False)
            mine.start()
            for cp in out:
                cp.start()

    def finish(*refs):
        for a in range(n):
            mine, out, back = copies(a, *refs, True)
            for cp in back:
                cp.wait_recv()
            for cp in out:
                cp.wait_send()
            mine.wait()

    outs = [jax.ShapeDtypeStruct(b.shape, b.dtype) for b in blocks]
    return _Rider(blocks, outs, 8 * n, ((0.0, start), (1.0, finish)))


def _sibling_rider(blocks):
    _, r, c = blocks.shape

    def copies(ins, outs, send_sems, recv_sems):
        x, y, cc = _mesh_pos()
        return [pltpu.make_async_remote_copy(
            src_ref=ins[0].at[2 * j + 1 - cc], dst_ref=outs[0].at[j], send_sem=send_sems.at[j],
            recv_sem=recv_sems.at[j], device_id=(x, y, 1 - cc), device_id_type=MESH) for j in range(4)]

    def start(*refs):
        for cp in copies(*refs):
            cp.start()

    def finish(*refs):
        for cp in copies(*refs):
            cp.wait_recv()
        for cp in copies(*refs):
            cp.wait_send()

    return _Rider([blocks], [jax.ShapeDtypeStruct((4, r, c), blocks.dtype)], 4, ((0.0, start), (1.0, finish)))


def _chip_rider(partial, row0, rows):
    _, _, c = partial.shape

    def copies(ins, outs, send_sems, recv_sems, with_back):
        x, y, cc = _mesh_pos()
        my_chip = 2 * x + y

        def rows_of(j):
            return ins[0].at[j, pl.ds(row0, rows)]

        mine = pltpu.make_async_copy(rows_of(my_chip), outs[0].at[my_chip], send_sems.at[3])
        out, back = [], []
        for k in range(1, 4):
            px, py = x ^ (k >> 1), y ^ (k & 1)
            peer = 2 * px + py
            sems = dict(send_sem=send_sems.at[k - 1], recv_sem=recv_sems.at[k - 1], device_id=(px, py, cc),
                        device_id_type=MESH)
            out.append(pltpu.make_async_remote_copy(src_ref=rows_of(peer), dst_ref=outs[0].at[my_chip], **sems))
            if with_back:
                back.append(pltpu.make_async_remote_copy(src_ref=rows_of(my_chip), dst_ref=outs[0].at[peer], **sems))
        return mine, out, back

    def start(*refs):
        mine, out, _ = copies(*refs, False)
        mine.start()
        for cp in out:
            cp.start()

    def finish(*refs):
        mine, out, back = copies(*refs, True)
        for cp in back:
            cp.wait_recv()
        for cp in out:
            cp.wait_send()
        mine.wait()

    return _Rider([partial], [jax.ShapeDtypeStruct((4, rows, c), partial.dtype)], 4, ((0.0, start), (1.0, finish)))


def _matmul(a, b, mode, out_dtype, name, add=None, rider=None):
    if mode == "nn":
        (m, k), (k2, n) = a.shape, b.shape
    elif mode == "nt":
        (m, k), (n, k2) = a.shape, b.shape
    else:
        (k, m), (k2, n) = a.shape, b.shape
    assert k == k2, (a.shape, b.shape, mode)
    tm, tn, tk = min(m, MM_TILE), min(n, MM_TILE), min(k, MM_TK)
    assert m % tm == 0 and n % tn == 0 and k % tk == 0, (m, n, k)
    nk = k // tk
    dot = {"nn": _dot, "nt": _dot_nt, "tn": _dot_tn}[mode]

    gm, gn = m // tm, n // tn
    n_rin = len(rider.inputs) if rider else 0
    n_rout = len(rider.out_shapes) if rider else 0
    n_add = 1 if add is not None else 0

    def body(*refs):
        a_ref, b_ref = refs[:2]
        add_ref = refs[2] if n_add else None
        r_ins = refs[2 + n_add:2 + n_add + n_rin]
        o_ref = refs[2 + n_add + n_rin]
        r_outs = refs[3 + n_add + n_rin:3 + n_add + n_rin + n_rout]
        scratch = refs[3 + n_add + n_rin + n_rout:]
        acc_ref = scratch[0] if nk > 1 else None
        kk = pl.program_id(2)
        if rider:
            step = (pl.program_id(0) * gn + pl.program_id(1)) * nk + kk
            ride = functools.partial(rider.run, step, gm * gn * nk, r_ins, r_outs, scratch[-2], scratch[-1])
            ride(last=False)

        def finish(r):
            if add_ref is not None:
                r = r + add_ref[...]
            o_ref[...] = r.astype(o_ref.dtype)

        if nk == 1:
            finish(dot(_bf(a_ref[...]), _bf(b_ref[...])))
        else:
            @pl.when(kk == 0)
            def _():
                acc_ref[...] = dot(_bf(a_ref[...]), _bf(b_ref[...]))

            @pl.when(jnp.logical_and(kk > 0, kk < nk - 1))
            def _():
                acc_ref[...] += dot(_bf(a_ref[...]), _bf(b_ref[...]))

            @pl.when(kk == nk - 1)
            def _():
                finish(acc_ref[...] + dot(_bf(a_ref[...]), _bf(b_ref[...])))
        if rider:
            ride(last=True)

    if mode == "nn":
        a_spec = pl.BlockSpec((tm, tk), lambda i, j, kk: (i, kk))
        b_spec = pl.BlockSpec((tk, tn), lambda i, j, kk: (kk, j))
    elif mode == "nt":
        a_spec = pl.BlockSpec((tm, tk), lambda i, j, kk: (i, kk))
        b_spec = pl.BlockSpec((tn, tk), lambda i, j, kk: (j, kk))
    else:
        a_spec = pl.BlockSpec((tk, tm), lambda i, j, kk: (kk, i))
        b_spec = pl.BlockSpec((tk, tn), lambda i, j, kk: (kk, j))
    o_spec = pl.BlockSpec((tm, tn), lambda i, j, kk: (i, j))
    in_specs = [a_spec, b_spec] + ([o_spec] if add is not None else [])
    args = (a, b) + ((add,) if add is not None else ())
    block_bytes = (tm * tk * a.dtype.itemsize + tk * tn * b.dtype.itemsize
                   + tm * tn * (jnp.dtype(out_dtype).itemsize + (4 if add is not None else 0)))
    vmem = 2 * block_bytes + 2 * tm * tn * 4 + 2 * (tm + tn) * tk + (4 << 20)
    out_shape = jax.ShapeDtypeStruct((m, n), out_dtype)
    scratch = [pltpu.VMEM((tm, tn), F32)] if nk > 1 else []
    if rider:
        in_specs = in_specs + rider.in_specs()
        args = args + rider.inputs
        out_shape = (out_shape, *rider.out_shapes)
        o_spec = (o_spec, *rider.out_specs())
        scratch = scratch + rider.scratch()
    sem = ("arbitrary",) * 3 if rider else ("parallel", "parallel", "arbitrary")
    return pl.pallas_call(
        body,
        out_shape=out_shape,
        grid=(gm, gn, nk),
        in_specs=in_specs,
        out_specs=o_spec,
        scratch_shapes=scratch,
        compiler_params=pltpu.CompilerParams(dimension_semantics=sem, vmem_limit_bytes=int(vmem)),
        name=name,
    )(*args)


def _row_spec(width, tile=ROW_TILE):
    return pl.BlockSpec((tile, width), lambda i: (i, 0))


def _full_spec(shape):
    nd = len(shape)
    return pl.BlockSpec(shape, lambda *_: (0,) * nd)


def _tiling_2d(r, c):
    if r <= ROW_TILE or r % ROW_TILE == 0:
        tr = min(r, ROW_TILE)
        return (tr, c), r // tr, (lambda i: (i, 0))
    tc = 128 if r > 4096 else 256
    assert c % tc == 0, (r, c)
    return (r, tc), c // tc, (lambda i: (0, i))


def _spec_2d(r, c):
    blk, grid, idx = _tiling_2d(r, c)
    return pl.BlockSpec(blk, idx), grid


def _cast_bf16(x, name):
    r, c = x.shape
    sp, grid = _spec_2d(r, c)

    def body(x_ref, o_ref):
        o_ref[...] = _bf(x_ref[...])

    return pl.pallas_call(
        body, out_shape=jax.ShapeDtypeStruct((r, c), BF16), grid=(grid,), in_specs=[sp], out_specs=sp,
        compiler_params=_params("parallel"), name=name)(x)


def _rmsnorm_fwd(x, w, name):
    s, d = x.shape

    def body(x_ref, w_ref, o_ref):
        xv = x_ref[...]
        rstd = lax.rsqrt(jnp.mean(xv * xv, axis=1, keepdims=True) + EPS)
        o_ref[...] = _bf(xv * rstd * w_ref[...])

    return pl.pallas_call(
        body, out_shape=jax.ShapeDtypeStruct((s, d), BF16), grid=(s // ROW_TILE,),
        in_specs=[_row_spec(d), _full_spec((1, d))], out_specs=_row_spec(d),
        compiler_params=_params("parallel"), name=name)(x, w)


def _rmsnorm_bwd(x, w, dh, dres, name):
    s, d = x.shape

    def body(x_ref, w_ref, dh_ref, dres_ref, dx_ref, dw_ref):
        @pl.when(pl.program_id(0) == 0)
        def _():
            dw_ref[...] = jnp.zeros_like(dw_ref)

        xv = x_ref[...]
        rstd = lax.rsqrt(jnp.mean(xv * xv, axis=1, keepdims=True) + EPS)
        xhat = xv * rstd
        dhv = dh_ref[...]
        dxhat = dhv * w_ref[...]
        dx = rstd * (dxhat - xhat * jnp.mean(dxhat * xhat, axis=1, keepdims=True))
        dx_ref[...] = dx + dres_ref[...]
        dw_ref[...] += jnp.sum(dhv * xhat, axis=0, keepdims=True)

    return pl.pallas_call(
        body,
        out_shape=(jax.ShapeDtypeStruct((s, d), F32), jax.ShapeDtypeStruct((1, d), F32)),
        grid=(s // ROW_TILE,),
        in_specs=[_row_spec(d), _full_spec((1, d)), _row_spec(d), _row_spec(d)],
        out_specs=(_row_spec(d), _full_spec((1, d))),
        compiler_params=_params("arbitrary"), name=name)(x, w, dh, dres)


def _rope_tables(pos_col, inv_freq, name):
    s = pos_col.shape[0]

    def body(p_ref, f_ref, cos_ref, sin_ref):
        ang = p_ref[...].astype(F32) * f_ref[...]
        cos_ref[...] = jnp.cos(ang)
        sin_ref[...] = jnp.sin(ang)

    out = jax.ShapeDtypeStruct((s, ROPE_HALF), F32)
    return pl.pallas_call(
        body, out_shape=(out, out), grid=(s // ROW_TILE,),
        in_specs=[_row_spec(1), _full_spec((1, ROPE_HALF))],
        out_specs=(_row_spec(ROPE_HALF), _row_spec(ROPE_HALF)),
        compiler_params=_params("parallel"), name=name)(pos_col, inv_freq)


def _merge_fwd(pg, p_r, p_s, name):
    s = pg.shape[0]

    def body(g_ref, r_ref, s_ref, o_ref):
        g = g_ref[...].astype(F32)
        o_ref[...] = _bf(_sigmoid(g[:, :D_MODEL]) * r_ref[...].astype(F32)
                         + _sigmoid(g[:, D_MODEL:]) * s_ref[...].astype(F32))

    return pl.pallas_call(
        body, out_shape=jax.ShapeDtypeStruct((s, D_MODEL), BF16), grid=(s // ROW_TILE,),
        in_specs=[_row_spec(2 * D_MODEL), _row_spec(D_MODEL), _row_spec(D_MODEL)],
        out_specs=_row_spec(D_MODEL), compiler_params=_params("parallel"), name=name)(pg, p_r, p_s)


def _merge_bwd(pg, p_r, p_s, dm, name):
    s = pg.shape[0]

    def body(g_ref, r_ref, s_ref, dm_ref, dr_ref, ds_ref, dg_ref):
        g = g_ref[...].astype(F32)
        sr, ss = _sigmoid(g[:, :D_MODEL]), _sigmoid(g[:, D_MODEL:])
        d = dm_ref[...]
        dr_ref[...] = _bf(d * sr)
        ds_ref[...] = _bf(d * ss)
        dg_ref[:, :D_MODEL] = _bf(d * r_ref[...].astype(F32) * sr * (1.0 - sr))
        dg_ref[:, D_MODEL:] = _bf(d * s_ref[...].astype(F32) * ss * (1.0 - ss))

    o = jax.ShapeDtypeStruct((s, D_MODEL), BF16)
    return pl.pallas_call(
        body, out_shape=(o, o, jax.ShapeDtypeStruct((s, 2 * D_MODEL), BF16)), grid=(s // ROW_TILE,),
        in_specs=[_row_spec(2 * D_MODEL), _row_spec(D_MODEL), _row_spec(D_MODEL), _row_spec(D_MODEL)],
        out_specs=(_row_spec(D_MODEL), _row_spec(D_MODEL), _row_spec(2 * D_MODEL)),
        compiler_params=_params("parallel"), name=name)(pg, p_r, p_s, dm)


def _final_fwd_bwd(x, o, w, target, name):
    s, d = x.shape

    def body(x_ref, o_ref, w_ref, t_ref, dx_ref, dw_ref, loss_ref):
        @pl.when(pl.program_id(0) == 0)
        def _():
            dw_ref[...] = jnp.zeros_like(dw_ref)
            loss_ref[...] = jnp.zeros_like(loss_ref)

        x2 = x_ref[...] + o_ref[...]
        rstd = lax.rsqrt(jnp.mean(x2 * x2, axis=1, keepdims=True) + EPS)
        xhat = x2 * rstd
        wv = w_ref[...]
        err = xhat * wv - t_ref[...]
        loss_ref[...] += jnp.sum(jnp.sum(err * err, axis=1, keepdims=True), axis=0, keepdims=True) * (0.5 / d)
        dy = err * (1.0 / d)
        dw_ref[...] += jnp.sum(dy * xhat, axis=0, keepdims=True)
        dxhat = dy * wv
        dx_ref[...] = rstd * (dxhat - xhat * jnp.mean(dxhat * xhat, axis=1, keepdims=True))

    return pl.pallas_call(
        body,
        out_shape=(jax.ShapeDtypeStruct((s, d), F32), jax.ShapeDtypeStruct((1, d), F32),
                   jax.ShapeDtypeStruct((8, 128), F32)),
        grid=(s // ROW_TILE,),
        in_specs=[_row_spec(d), _row_spec(d), _full_spec((1, d)), _row_spec(d)],
        out_specs=(_row_spec(d), _full_spec((1, d)), _full_spec((8, 128))),
        compiler_params=_params("arbitrary"), name=name)(x, o, w, target)


def _adamw(w, g, m, v, name):
    r, c = w.shape
    sp, grid = _spec_2d(r, c)
    c1 = 1.0 / (1.0 - ADAM_B1 ** ADAM_STEP)
    c2 = 1.0 / (1.0 - ADAM_B2 ** ADAM_STEP)

    def body(w_ref, g_ref, m_ref, v_ref, d_ref, nm_ref, nv_ref):
        gv = g_ref[...]
        nm = ADAM_B1 * m_ref[...] + (1.0 - ADAM_B1) * gv
        nv = ADAM_B2 * v_ref[...] + (1.0 - ADAM_B2) * (gv * gv)
        d_ref[...] = -ADAM_LR * ((nm * c1) / (jnp.sqrt(nv * c2) + ADAM_EPS) + ADAM_WD * w_ref[...])
        nm_ref[...] = nm
        nv_ref[...] = nv

    o = jax.ShapeDtypeStruct((r, c), F32)
    return pl.pallas_call(
        body, out_shape=(o, o, o), grid=(grid,), in_specs=[sp, sp, sp, sp], out_specs=(sp, sp, sp),
        compiler_params=_params("parallel"), name=name)(w, g, m, v)


def _sum_slots(land, name):
    n, r, c = land.shape
    blk, grid, idx = _tiling_2d(r, c)

    def body(l_ref, o_ref):
        acc = l_ref[0].astype(F32)
        for i in range(1, n):
            acc = acc + l_ref[i].astype(F32)
        o_ref[...] = acc

    return pl.pallas_call(
        body, out_shape=jax.ShapeDtypeStruct((r, c), F32), grid=(grid,),
        in_specs=[pl.BlockSpec((n, *blk), lambda i: (0, *idx(i)))], out_specs=pl.BlockSpec(blk, idx),
        compiler_params=_params("parallel"), name=name)(land)


def _add_bf16(a, b, name):
    r, c = a.shape
    sp, grid = _spec_2d(r, c)

    def body(a_ref, b_ref, o_ref):
        o_ref[...] = _bf(a_ref[...].astype(F32) + b_ref[...].astype(F32))

    return pl.pallas_call(
        body, out_shape=jax.ShapeDtypeStruct((r, c), BF16), grid=(grid,), in_specs=[sp, sp], out_specs=sp,
        compiler_params=_params("parallel"), name=name)(a, b)


def _retention_tables():
    lg = np.log1p(-(2.0 ** (-5.0 - np.arange(RET_HEADS, dtype=np.float64))))
    idx = np.arange(CHUNK, dtype=np.float64)
    intra = np.exp(np.abs(idx[:, None] - idx[None, :])[None] * lg[:, None, None])
    qd = np.exp((idx[None, :] + 1.0) * lg[:, None])
    kd = np.exp((CHUNK - 1.0 - idx[None, :]) * lg[:, None])
    cd = np.exp(CHUNK * lg)
    ones = np.ones((1, 1, RET_DK))
    return (jnp.asarray(intra, F32), jnp.asarray(qd[:, :, None] * ones, F32),
            jnp.asarray(kd[:, :, None] * ones, F32), jnp.asarray(cd[:, None, None] * ones, F32))


def _rope(t, cos, sin):
    t1, t2 = t[:, :ROPE_HALF], t[:, ROPE_HALF:]
    return jnp.concatenate([t1 * cos - t2 * sin, t2 * cos + t1 * sin], axis=1)


def _rope_t(d, cos, sin):
    d1, d2 = d[:, :ROPE_HALF], d[:, ROPE_HALF:]
    return jnp.concatenate([d1 * cos + d2 * sin, d2 * cos - d1 * sin], axis=1)


_HEADS = range(RET_HEADS)


def _ret_chunk_fwd(p_ref, cos, sin, intra_ref, qd_ref, st):
    def seg(i, h):
        return p_ref[:, i * RET_W + h * RET_DK:i * RET_W + (h + 1) * RET_DK]

    v = [seg(2, h) for h in _HEADS]
    qr = [_rope(seg(0, h), cos, sin) for h in _HEADS]
    kr = [_rope(seg(1, h), cos, sin) * (RET_DK ** -0.5) for h in _HEADS]
    qrb, vb = [_bf(a) for a in qr], [_bf(a) for a in v]
    sc = [_dot_nt(qrb[h], _bf(kr[h])) * intra_ref[h] for h in _HEADS]
    qs = [_dot(qrb[h], _bf(st[h])) for h in _HEADS]
    y = [_dot(_bf(sc[h]), vb[h]) + qs[h] * qd_ref[h] for h in _HEADS]
    g = [seg(3, h) for h in _HEADS]
    return dict(v=v, vb=vb, qr=qr, qrb=qrb, kr=kr, sc=sc, y=y, g=g)


def _group_norm(y):
    mu = jnp.mean(y, axis=1, keepdims=True)
    yc = y - mu
    rstd = lax.rsqrt(jnp.mean(yc * yc, axis=1, keepdims=True) + EPS)
    return yc * rstd, rstd


def _ret_specs(nc, rev):
    cidx = (lambda c: nc - 1 - c) if rev else (lambda c: c)
    return dict(
        proj=pl.BlockSpec((CHUNK, 4 * RET_W), lambda c: (cidx(c), 0)),
        half=pl.BlockSpec((CHUNK, ROPE_HALF), lambda c: (cidx(c), 0)),
        wide=pl.BlockSpec((CHUNK, RET_W), lambda c: (cidx(c), 0)),
        state=pl.BlockSpec((1, RET_HEADS, RET_DK, RET_DK), lambda c: (cidx(c), 0, 0, 0)),
        intra=_full_spec((RET_HEADS, CHUNK, CHUNK)),
        dec=_full_spec((RET_HEADS, CHUNK, RET_DK)),
        cd=_full_spec((RET_HEADS, 1, RET_DK)),
    )


def _retention_fwd(p_r, cos, sin, name):
    s = p_r.shape[0]
    nc = s // CHUNK
    intra_t, qd_t, kd_t, cd_t = _retention_tables()

    def body(p_ref, cos_ref, sin_ref, intra_ref, qd_ref, kd_ref, cd_ref, y_ref, st_ref, state):
        @pl.when(pl.program_id(0) == 0)
        def _():
            state[...] = jnp.zeros_like(state)

        st = [state[h] for h in _HEADS]
        f = _ret_chunk_fwd(p_ref, cos_ref[...], sin_ref[...], intra_ref, qd_ref, st)
        new_st = [st[h] * cd_ref[h] + _dot_tn(_bf(f["kr"][h] * kd_ref[h]), f["vb"][h]) for h in _HEADS]
        out = [_bf(_group_norm(f["y"][h])[0] * (f["g"][h] * _sigmoid(f["g"][h]))) for h in _HEADS]
        for h in _HEADS:
            st_ref[0, h] = _bf(st[h])
            state[h] = new_st[h]
            y_ref[:, h * RET_DK:(h + 1) * RET_DK] = out[h]

    sp = _ret_specs(nc, False)
    return pl.pallas_call(
        body,
        out_shape=(jax.ShapeDtypeStruct((s, RET_W), BF16),
                   jax.ShapeDtypeStruct((nc, RET_HEADS, RET_DK, RET_DK), BF16)),
        grid=(nc,),
        in_specs=[sp["proj"], sp["half"], sp["half"], sp["intra"], sp["dec"], sp["dec"], sp["cd"]],
        out_specs=(sp["wide"], sp["state"]),
        scratch_shapes=[pltpu.VMEM((RET_HEADS, RET_DK, RET_DK), F32)],
        compiler_params=_params("arbitrary"), name=name)(p_r, cos, sin, intra_t, qd_t, kd_t, cd_t)


def _retention_bwd(p_r, cos, sin, states, dy_r, name):
    s = p_r.shape[0]
    nc = s // CHUNK
    intra_t, qd_t, kd_t, cd_t = _retention_tables()

    def body(p_ref, cos_ref, sin_ref, intra_ref, qd_ref, kd_ref, cd_ref, st_ref, dy_ref, dp_ref, dstate):
        @pl.when(pl.program_id(0) == 0)
        def _():
            dstate[...] = jnp.zeros_like(dstate)

        cos, sin = cos_ref[...], sin_ref[...]
        st = [st_ref[0, h] for h in _HEADS]
        dsn = [dstate[h] for h in _HEADS]
        f = _ret_chunk_fwd(p_ref, cos, sin, intra_ref, qd_ref, st)
        vb, qrb, kr, g = f["vb"], f["qrb"], f["kr"], f["g"]
        norm = [_group_norm(f["y"][h]) for h in _HEADS]
        sg = [_sigmoid(g[h]) for h in _HEADS]
        dyr = [dy_ref[:, h * RET_DK:(h + 1) * RET_DK] for h in _HEADS]
        dyn = [dyr[h] * (g[h] * sg[h]) for h in _HEADS]
        dg = [dyr[h] * norm[h][0] * (sg[h] * (1.0 + g[h] * (1.0 - sg[h]))) for h in _HEADS]
        dy = [norm[h][1] * (dyn[h] - jnp.mean(dyn[h], axis=1, keepdims=True)
                            - norm[h][0] * jnp.mean(dyn[h] * norm[h][0], axis=1, keepdims=True)) for h in _HEADS]
        dyb, dsnb = [_bf(a) for a in dy], [_bf(a) for a in dsn]
        ds = [_bf(_dot_nt(dyb[h], vb[h]) * intra_ref[h]) for h in _HEADS]
        t = [_bf(dy[h] * qd_ref[h]) for h in _HEADS]
        dv = [_dot_tn(_bf(f["sc"][h]), dyb[h]) + _dot(_bf(kr[h] * kd_ref[h]), dsnb[h]) for h in _HEADS]
        dqr = [_dot(ds[h], _bf(kr[h])) + _dot_nt(t[h], _bf(st[h])) for h in _HEADS]
        dkr = [_dot_tn(ds[h], qrb[h]) + _dot_nt(vb[h], dsnb[h]) * kd_ref[h] for h in _HEADS]
        new_ds = [dsn[h] * cd_ref[h] + _dot_tn(qrb[h], t[h]) for h in _HEADS]
        for h in _HEADS:
            lo = h * RET_DK
            dstate[h] = new_ds[h]
            dp_ref[:, lo:lo + RET_DK] = _bf(_rope_t(dqr[h], cos, sin))
            dp_ref[:, RET_W + lo:RET_W + lo + RET_DK] = _bf(_rope_t(dkr[h], cos, sin) * (RET_DK ** -0.5))
            dp_ref[:, 2 * RET_W + lo:2 * RET_W + lo + RET_DK] = _bf(dv[h])
            dp_ref[:, 3 * RET_W + lo:3 * RET_W + lo + RET_DK] = _bf(dg[h])

    sp = _ret_specs(nc, True)
    return pl.pallas_call(
        body,
        out_shape=jax.ShapeDtypeStruct((s, 4 * RET_W), BF16),
        grid=(nc,),
        in_specs=[sp["proj"], sp["half"], sp["half"], sp["intra"], sp["dec"], sp["dec"], sp["cd"],
                  sp["state"], sp["wide"]],
        out_specs=sp["proj"],
        scratch_shapes=[pltpu.VMEM((RET_HEADS, RET_DK, RET_DK), F32)],
        compiler_params=_params("arbitrary"), name=name)(p_r, cos, sin, intra_t, qd_t, kd_t, cd_t, states, dy_r)


def _conv_taps(ext, w):
    acc = w[SSD_CONV - 1:SSD_CONV] * ext
    for j in range(SSD_CONV - 1):
        acc = acc + w[j:j + 1] * pltpu.roll(ext, SSD_CONV - 1 - j, axis=0)
    return acc


def _conv_fwd(xbc_raw, conv_w, conv_b, name):
    s = xbc_raw.shape[0]
    t8 = CONV_TILE // 8

    def body(cur_ref, prev_ref, w_ref, b_ref, o_ref, slope_ref):
        keep = (pl.program_id(0) > 0).astype(F32)
        ext = jnp.concatenate([prev_ref[...] * keep, cur_ref[...]], axis=0)
        u = _conv_taps(ext, w_ref[...])[8:] + b_ref[...]
        sg = _sigmoid(u)
        o_ref[...] = u * sg
        slope_ref[...] = sg * (1.0 + u * (1.0 - sg))

    out = jax.ShapeDtypeStruct((s, SSD_CD), F32)
    return pl.pallas_call(
        body, out_shape=(out, out), grid=(s // CONV_TILE,),
        in_specs=[_row_spec(SSD_CD, CONV_TILE),
                  pl.BlockSpec((8, SSD_CD), lambda i: (jnp.maximum(i * t8 - 1, 0), 0)),
                  _full_spec((SSD_CONV, SSD_CD)), _full_spec((1, SSD_CD))],
        out_specs=(_row_spec(SSD_CD, CONV_TILE), _row_spec(SSD_CD, CONV_TILE)),
        compiler_params=_params("parallel"), name=name)(xbc_raw, xbc_raw, conv_w, conv_b)


def _conv_bwd(xbc_raw, conv_w, slope, dact, name):
    s = xbc_raw.shape[0]
    nt = s // CONV_TILE
    t8 = CONV_TILE // 8
    rows = CONV_TILE + 8

    def body(cur_ref, prev_ref, w_ref, s_ref, snext_ref, d_ref, dnext_ref, dx_ref, dw_ref, db_ref):
        i = pl.program_id(0)

        @pl.when(i == 0)
        def _():
            dw_ref[...] = jnp.zeros_like(dw_ref)
            db_ref[...] = jnp.zeros_like(db_ref)

        keep_prev = (i > 0).astype(F32)
        keep_next = (i < nt - 1).astype(F32)
        w = w_ref[...]
        cur = cur_ref[...]
        ext = jnp.concatenate([prev_ref[...] * keep_prev, cur], axis=0)
        duc = d_ref[...] * s_ref[...]
        du = jnp.concatenate([duc, dnext_ref[...] * snext_ref[...] * keep_next], axis=0)
        dx = w[SSD_CONV - 1:SSD_CONV] * du
        for j in range(SSD_CONV - 1):
            dx = dx + w[j:j + 1] * pltpu.roll(du, rows - (SSD_CONV - 1 - j), axis=0)
        dx_ref[...] = _bf(dx[:CONV_TILE])
        for j in range(SSD_CONV - 1):
            shifted = pltpu.roll(ext, SSD_CONV - 1 - j, axis=0)[8:]
            dw_ref[j:j + 1, :] += jnp.sum(duc * shifted, axis=0, keepdims=True)
        dw_ref[SSD_CONV - 1:SSD_CONV, :] += jnp.sum(duc * cur, axis=0, keepdims=True)
        db_ref[...] += jnp.sum(duc, axis=0, keepdims=True)

    row = _row_spec(SSD_CD, CONV_TILE)
    prev = pl.BlockSpec((8, SSD_CD), lambda i: (jnp.maximum(i * t8 - 1, 0), 0))
    nxt = pl.BlockSpec((8, SSD_CD), lambda i: (jnp.minimum((i + 1) * t8, s // 8 - 1), 0))
    return pl.pallas_call(
        body,
        out_shape=(jax.ShapeDtypeStruct((s, SSD_CD), BF16), jax.ShapeDtypeStruct((SSD_CONV, SSD_CD), F32),
                   jax.ShapeDtypeStruct((1, SSD_CD), F32)),
        grid=(nt,),
        in_specs=[row, prev, _full_spec((SSD_CONV, SSD_CD)), row, nxt, row, nxt],
        out_specs=(row, _full_spec((SSD_CONV, SSD_CD)), _full_spec((1, SSD_CD))),
        compiler_params=_params("arbitrary"), name=name)(xbc_raw, xbc_raw, conv_w, slope, slope, dact, dact)


def _head_select():
    e = np.zeros((HEAD_PAD, SSD_W), np.float32)
    for h in range(SSD_HEADS):
        e[h, h * SSD_P:(h + 1) * SSD_P] = 1.0
    return jnp.asarray(e, BF16), jnp.asarray(e.T, BF16)


def _pad_heads(v):
    return jnp.pad(v.reshape(1, SSD_HEADS).astype(F32), ((0, 0), (0, HEAD_PAD - SSD_HEADS)))


def _ssd_masks():
    r = _iota((CHUNK, SSD_QW), 0)
    c = _iota((CHUNK, SSD_QW), 1) % SSD_P
    itile = (r == c).astype(F32)
    ctile = r >= c
    rb = _iota((SSD_QW, SSD_QW), 0) // SSD_P
    cb = _iota((SSD_QW, SSD_QW), 1) // SSD_P
    return itile, ctile, rb == cb


def _ssd_prep(xbc_ref, dtr_ref, dtb_ref, alog_ref, e_ref=None, acol_ref=None, dte_ref=None):
    xs = xbc_ref[:, :SSD_W]
    u = dtr_ref[...] + dtb_ref[...]
    dt = _softplus(u)
    nexp = -jnp.exp(alog_ref[...])
    a = dt * nexp
    if e_ref is not None:
        tril = _bf((_iota((CHUNK, CHUNK), 0) >= _iota((CHUNK, CHUNK), 1)).astype(F32))
        e = e_ref[...]
        acol = _dot_exact_l(_dot_exact_r(tril, a), e)
        dte = _dot_exact_l(dt, e)
    else:
        acol, dte = acol_ref[...], dte_ref[...]
    alast = acol[CHUNK - 1:CHUNK, :]
    return dict(xs=xs, u=u, dt=dt, nexp=nexp, a=a, acol=acol, dte=dte, xdt=xs * dte,
                ea=jnp.exp(acol), tail=jnp.exp(alast - acol), eal=jnp.exp(alast))


def _tile4(x):
    return jnp.concatenate([x, x, x, x], axis=0)


def _fold4(x):
    return x[0:CHUNK] + x[CHUNK:2 * CHUNK] + x[2 * CHUNK:3 * CHUNK] + x[3 * CHUNK:4 * CHUNK]


def _ssd_specs(nc, rev):
    cidx = (lambda c: nc - 1 - c) if rev else (lambda c: c)
    return dict(
        xbc=pl.BlockSpec((CHUNK, SSD_CD), lambda c: (cidx(c), 0)),
        dt=pl.BlockSpec((CHUNK, HEAD_PAD), lambda c: (cidx(c), 0)),
        wide=pl.BlockSpec((CHUNK, SSD_W), lambda c: (cidx(c), 0)),
        state=pl.BlockSpec((1, SSD_N, SSD_W), lambda c: (cidx(c), 0, 0)),
        head=_full_spec((1, HEAD_PAD)),
        roww=_full_spec((1, SSD_W)),
        e=_full_spec((HEAD_PAD, SSD_W)),
        et=_full_spec((SSD_W, HEAD_PAD)),
    )


_GROUPS = range(SSD_GROUPS)
_GROUP_LANES = [slice(g * SSD_GW, (g + 1) * SSD_GW) for g in _GROUPS]
_QUADS = [(g, slice(g * SSD_GW + q * SSD_QW, g * SSD_GW + (q + 1) * SSD_QW), slice(q * SSD_QW, (q + 1) * SSD_QW))
          for g in _GROUPS for q in range(SSD_GW // SSD_QW)]


def _ssd_bc(xbc_ref):
    bg = [_bf(xbc_ref[:, B_OFF + g * SSD_N:B_OFF + (g + 1) * SSD_N]) for g in _GROUPS]
    cg = [_bf(xbc_ref[:, C_OFF + g * SSD_N:C_OFF + (g + 1) * SSD_N]) for g in _GROUPS]
    return bg, cg


def _ssd_decay(aq, itile, ctile):
    arow = jnp.sum(aq * itile, axis=0, keepdims=True)
    return jnp.exp(jnp.where(ctile, aq - arow, -jnp.inf))


def _ssd_blockdiag(xq, bdmask):
    return jnp.where(bdmask, _tile4(_bf(xq)), jnp.zeros((), BF16))


def _ssd_fwd(xbc, dt_raw, z, dt_bias, a_log, dske, norm_w, name):
    s = xbc.shape[0]
    nc = s // CHUNK
    e_sel, _ = _head_select()

    def body(xbc_ref, dtr_ref, z_ref, dtb_ref, alog_ref, dske_ref, nw_ref, e_ref,
             yraw_ref, ys_ref, st_ref, acol_ref, dte_ref, state):
        @pl.when(pl.program_id(0) == 0)
        def _():
            state[...] = jnp.zeros_like(state)

        p = _ssd_prep(xbc_ref, dtr_ref, dtb_ref, alog_ref, e_ref=e_ref)
        acol_ref[...] = p["acol"]
        dte_ref[...] = p["dte"]
        itile, ctile, bdmask = _ssd_masks()
        dske = dske_ref[...]
        st_all = state[...]
        st_ref[0] = st_all
        stg = [st_all[:, gs] for gs in _GROUP_LANES]
        bg, cg = _ssd_bc(xbc_ref)
        cbt = [_dot_nt(cg[g], _tile4(bg[g])) for g in _GROUPS]
        ys = [_dot(cg[g], _bf(stg[g])) for g in _GROUPS]
        dq = [_ssd_decay(p["acol"][:, ql], itile, ctile) for _, ql, _ in _QUADS]
        xbd = [_ssd_blockdiag(p["xdt"][:, ql], bdmask) for _, ql, _ in _QUADS]
        yq = [_dot(_bf(cbt[g] * dq[i]), xbd[i]) + ys[g][:, qs] * p["ea"][:, ql] + dske[:, ql] * p["xs"][:, ql]
              for i, (g, ql, qs) in enumerate(_QUADS)]
        new_st = [stg[g] * p["eal"][:, gs] + _dot_tn(bg[g], _bf(p["xdt"][:, gs] * p["tail"][:, gs]))
                  for g, gs in enumerate(_GROUP_LANES)]
        y = jnp.concatenate(yq, axis=1)
        state[...] = jnp.concatenate(new_st, axis=1)
        yraw_ref[...] = y
        zv = z_ref[...]
        t = y * (zv * _sigmoid(zv))
        rstd = lax.rsqrt(jnp.mean(t * t, axis=1, keepdims=True) + EPS)
        ys_ref[...] = _bf(t * rstd * nw_ref[...])

    sp = _ssd_specs(nc, False)
    wide = jax.ShapeDtypeStruct((s, SSD_W), F32)
    return pl.pallas_call(
        body,
        out_shape=(wide, jax.ShapeDtypeStruct((s, SSD_W), BF16), jax.ShapeDtypeStruct((nc, SSD_N, SSD_W), F32),
                   wide, wide),
        grid=(nc,),
        in_specs=[sp["xbc"], sp["dt"], sp["wide"], sp["head"], sp["head"], sp["roww"], sp["roww"], sp["e"]],
        out_specs=(sp["wide"], sp["wide"], sp["state"], sp["wide"], sp["wide"]),
        scratch_shapes=[pltpu.VMEM((SSD_N, SSD_W), F32)],
        compiler_params=_params("arbitrary"), name=name,
    )(xbc, dt_raw, z, dt_bias, a_log, dske, norm_w, e_sel)


def _ssd_bwd(xbc, dt_raw, z, yraw, states, dys, acol, dte, dt_bias, a_log, dske, norm_w, name, rider):
    s = xbc.shape[0]
    nc = s // CHUNK
    _, et_sel = _head_select()
    n_in, n_out, n_scratch = 13, 7, 2
    n_rin, n_rout = len(rider.inputs), len(rider.out_shapes)

    def body(*refs):
        ins, refs = refs[:n_in], refs[n_in:]
        r_ins, refs = refs[:n_rin], refs[n_rin:]
        outs, refs = refs[:n_out], refs[n_out:]
        r_outs, refs = refs[:n_rout], refs[n_rout:]
        ride = functools.partial(rider.run, pl.program_id(0), nc, r_ins, r_outs, refs[n_scratch], refs[n_scratch + 1])
        ride(last=False)
        compute(*ins, *outs, *refs[:n_scratch])
        ride(last=True)

    def compute(xbc_ref, dtr_ref, z_ref, yraw_ref, st_ref, dys_ref, acol_ref, dte_ref, dtb_ref, alog_ref,
                dske_ref, nw_ref, et_ref, dxbc_ref, dz_ref, ddtr_ref, ddtb_ref, dalog_ref, ddsk_ref, dnw_ref,
                dstate, dsk_acc):
        @pl.when(pl.program_id(0) == 0)
        def _():
            dstate[...] = jnp.zeros_like(dstate)
            dsk_acc[...] = jnp.zeros_like(dsk_acc)
            ddtb_ref[...] = jnp.zeros_like(ddtb_ref)
            dalog_ref[...] = jnp.zeros_like(dalog_ref)
            dnw_ref[...] = jnp.zeros_like(dnw_ref)

        p = _ssd_prep(xbc_ref, dtr_ref, dtb_ref, alog_ref, acol_ref=acol_ref, dte_ref=dte_ref)
        itile, ctile, bdmask = _ssd_masks()
        et = et_ref[...]
        dske = dske_ref[...]
        last_row = (_iota((CHUNK, 1), 0) == CHUNK - 1).astype(F32)

        zv, y, dysv = z_ref[...], yraw_ref[...], dys_ref[...]
        sz = _sigmoid(zv)
        silz = zv * sz
        t = y * silz
        rstd = lax.rsqrt(jnp.mean(t * t, axis=1, keepdims=True) + EPS)
        that = t * rstd
        dthat = dysv * nw_ref[...]
        dt_ = rstd * (dthat - that * jnp.mean(dthat * that, axis=1, keepdims=True))
        dnw_ref[...] += jnp.sum(dysv * that, axis=0, keepdims=True)
        dz_ref[...] = _bf(dt_ * y * (sz * (1.0 + zv * (1.0 - sz))))
        dy = dt_ * silz
        dsk_acc[...] += jnp.sum(dy * p["xs"], axis=0, keepdims=True)

        @pl.when(pl.program_id(0) == nc - 1)
        def _():
            ddsk_ref[...] = _dot_exact_l(jnp.broadcast_to(dsk_acc[...], (8, SSD_W)), et)[0:1]

        gl = _GROUP_LANES
        st_all, dsn_all = st_ref[0], dstate[...]
        stg = [st_all[:, gs] for gs in gl]
        dsn = [dsn_all[:, gs] for gs in gl]
        stgb, dsnb = [_bf(a) for a in stg], [_bf(a) for a in dsn]
        bg, cg = _ssd_bc(xbc_ref)
        btile = [_tile4(a) for a in bg]
        cbt = [_dot_nt(cg[g], btile[g]) for g in _GROUPS]
        dyg = [dy[:, gs] for gs in gl]
        eag, tailg = [p["ea"][:, gs] for gs in gl], [p["tail"][:, gs] for gs in gl]
        xdtg, ealg = [p["xdt"][:, gs] for gs in gl], [p["eal"][:, gs] for gs in gl]
        dys_g = [_bf(dyg[g] * eag[g]) for g in _GROUPS]
        ysg = [_dot(cg[g], stgb[g]) for g in _GROUPS]
        dc0 = [_dot_nt(dys_g[g], stgb[g]) for g in _GROUPS]
        dst = [_dot_tn(cg[g], dys_g[g]) for g in _GROUPS]
        dwt = [_dot(bg[g], dsnb[g]) for g in _GROUPS]
        db0 = [_dot_nt(_bf(xdtg[g] * tailg[g]), dsnb[g]) for g in _GROUPS]
        dtl = [dwt[g] * xdtg[g] * tailg[g] for g in _GROUPS]
        dal_row = [jnp.sum(dtl[g], axis=0, keepdims=True) + jnp.sum(dsn[g] * stg[g], axis=0, keepdims=True) * ealg[g]
                   for g in _GROUPS]
        dstate[...] = jnp.concatenate([dst[g] + dsn[g] * ealg[g] for g in _GROUPS], axis=1)
        dq = [_ssd_decay(p["acol"][:, ql], itile, ctile) for _, ql, _ in _QUADS]
        mq = [cbt[g] * dq[i] for i, (g, _, _) in enumerate(_QUADS)]
        xbd = [_ssd_blockdiag(p["xdt"][:, ql], bdmask) for _, ql, _ in _QUADS]
        dyq = [_bf(dy[:, ql]) for _, ql, _ in _QUADS]
        dm = [_dot_nt(dyq[i], xbd[i]) for i in range(len(_QUADS))]
        dxdt_q = [_fold4(jnp.where(bdmask, _dot_tn(_bf(mq[i]), dyq[i]), 0.0)) for i in range(len(_QUADS))]
        eq = [dm[i] * mq[i] for i in range(len(_QUADS))]
        dacol_q = [eq[i] - itile * jnp.sum(eq[i], axis=0, keepdims=True) for i in range(len(_QUADS))]
        dcbt = [_bf(dm[i] * dq[i]) for i in range(len(_QUADS))]
        dc_q = [_dot(dcbt[i], btile[g]) for i, (g, _, _) in enumerate(_QUADS)]
        db_q = [_fold4(_dot_tn(dcbt[i], cg[g])) for i, (g, _, _) in enumerate(_QUADS)]
        dxdt = jnp.concatenate(dxdt_q, axis=1) + jnp.concatenate([dwt[g] * tailg[g] for g in _GROUPS], axis=1)
        dacol = (jnp.concatenate(dacol_q, axis=1)
                 + jnp.concatenate([dyg[g] * (ysg[g] * eag[g]) - dtl[g] + last_row * dal_row[g] for g in _GROUPS],
                                   axis=1))
        dxbc_ref[:, :SSD_W] = dy * dske + dxdt * p["dte"]
        dxbc_ref[:, B_OFF:C_OFF] = jnp.concatenate([db0[g] + db_q[2 * g] + db_q[2 * g + 1] for g in _GROUPS], axis=1)
        dxbc_ref[:, C_OFF:] = jnp.concatenate([dc0[g] + dc_q[2 * g] + dc_q[2 * g + 1] for g in _GROUPS], axis=1)

        dacum = _dot_exact_l(dacol, et, pieces=2)
        triu = _bf((_iota((CHUNK, CHUNK), 1) >= _iota((CHUNK, CHUNK), 0)).astype(F32))
        da = _dot_exact_r(triu, dacum)
        ddt = _dot_exact_l(dxdt * p["xs"], et, pieces=2) + da * p["nexp"]
        dalog_ref[...] += jnp.sum(da * p["a"], axis=0, keepdims=True)
        du = ddt * _sigmoid(p["u"])
        ddtr_ref[...] = _bf(du)
        ddtb_ref[...] += jnp.sum(du, axis=0, keepdims=True)

    sp = _ssd_specs(nc, True)
    return pl.pallas_call(
        body,
        out_shape=(jax.ShapeDtypeStruct((s, SSD_CD), F32), jax.ShapeDtypeStruct((s, SSD_W), BF16),
                   jax.ShapeDtypeStruct((s, HEAD_PAD), BF16), jax.ShapeDtypeStruct((1, HEAD_PAD), F32),
                   jax.ShapeDtypeStruct((1, HEAD_PAD), F32), jax.ShapeDtypeStruct((1, HEAD_PAD), F32),
                   jax.ShapeDtypeStruct((1, SSD_W), F32), *rider.out_shapes),
        grid=(nc,),
        in_specs=[sp["xbc"], sp["dt"], sp["wide"], sp["wide"], sp["state"], sp["wide"], sp["wide"], sp["wide"],
                  sp["head"], sp["head"], sp["roww"], sp["roww"], sp["et"], *rider.in_specs()],
        out_specs=(sp["xbc"], sp["wide"], sp["dt"], sp["head"], sp["head"], sp["head"], sp["roww"],
                   *rider.out_specs()),
        scratch_shapes=[pltpu.VMEM((SSD_N, SSD_W), F32), pltpu.VMEM((1, SSD_W), F32), *rider.scratch()],
        compiler_params=_params("arbitrary"), name=name,
    )(xbc, dt_raw, z, yraw, states, dys, acol, dte, dt_bias, a_log, dske, norm_w, et_sel, *rider.inputs)


W_IN_SHARD = IN_PROJ // N_DEV
W_IN_ROW_PARTS = ((0, 1152), (1152, 832), (1984, 840))


def _device_step(x, pos_col, target, norm1_w, conv_w_shard, conv_b, dt_bias, a_log, d_skip, ssd_norm_w, norm_f_w,
                 w_r, w_z, w_xbc, w_dt, w_g, br_ret_shard, br_ssd_shard, out_shard):
    inv_freq = jnp.asarray(ROPE_THETA ** (-np.arange(ROPE_HALF, dtype=np.float64) / ROPE_HALF), F32).reshape(1, ROPE_HALF)
    dtb, alog = _pad_heads(dt_bias), _pad_heads(a_log)
    dske = jnp.repeat(d_skip.reshape(SSD_HEADS).astype(F32), SSD_P).reshape(1, SSD_W)
    my_core = lax.axis_index("c")

    h = _rmsnorm_fwd(x, norm1_w, "rmsnorm1_fwd")
    gather = _gather_rider([br_ret_shard, br_ssd_shard, out_shard, conv_w_shard], relay_at=0.35, pass_at=0.6)
    p_r, all_ret, all_ssd, all_out, all_conv = _matmul(h, w_r, "nt", F32, "proj_ret", rider=gather)
    w_br_ret = all_ret.reshape(RET_W, D_MODEL)
    w_br_ssd = all_ssd.reshape(SSD_W, D_MODEL)
    w_out = all_out.reshape(D_MODEL, D_MODEL)
    conv_w = all_conv.transpose(1, 0, 2).reshape(SSD_CONV, SSD_CD)
    p_z = _matmul(h, w_z, "nt", F32, "proj_z")
    p_xbc = _matmul(h, w_xbc, "nt", F32, "proj_xbc")
    p_dt = _matmul(h, w_dt, "nt", F32, "proj_dt")
    p_g = _matmul(h, w_g, "nt", BF16, "proj_gates")
    cos, sin = _rope_tables(pos_col, inv_freq, "rope_tables")
    y_r, ret_states = _retention_fwd(p_r, cos, sin, "retention_fwd")
    xbc_act, silu_slope = _conv_fwd(p_xbc, conv_w, conv_b, "conv_fwd")
    y_raw, y_s, ssd_states, acol, dte = _ssd_fwd(xbc_act, p_dt, p_z, dtb, alog, dske, ssd_norm_w, "ssd_fwd")
    pr = _matmul(y_r, w_br_ret, "nn", BF16, "branch_ret")
    ps = _matmul(y_s, w_br_ssd, "nn", BF16, "branch_ssd")
    merged = _merge_fwd(p_g, pr, ps, "merge_fwd")
    o = _matmul(merged, w_out, "nn", F32, "out_proj")
    dx2, g_norm_f, loss_acc = _final_fwd_bwd(x, o, norm_f_w, target, "final_norm_loss")

    g_w_out = _matmul(merged, dx2, "tn", BF16, "grad_w_out")
    dmerged = _matmul(dx2, w_out, "nt", F32, "d_merged")
    dpr, dps, dp_g = _merge_bwd(p_g, pr, ps, dmerged, "merge_bwd")
    g_w_br_ret = _matmul(y_r, dpr, "tn", BF16, "grad_w_br_ret")
    g_w_br_ssd = _matmul(y_s, dps, "tn", BF16, "grad_w_br_ssd")
    dy_r = _matmul(dpr, w_br_ret, "nt", F32, "d_y_ret")
    dy_s = _matmul(dps, w_br_ssd, "nt", F32, "d_y_ssd")
    scatter = _scatter_rider([g_w_out.reshape(N_DEV, -1, D_MODEL), g_w_br_ret.reshape(N_DEV, -1, D_MODEL),
                              g_w_br_ssd.reshape(N_DEV, -1, D_MODEL)])
    dxbc_act, dp_z, dp_dt, g_dtb, g_alog, g_dsk, g_ssd_norm, got_out, got_ret, got_ssd = _ssd_bwd(
        xbc_act, p_dt, p_z, y_raw, ssd_states, dy_s, acol, dte, dtb, alog, dske, ssd_norm_w, "ssd_bwd", scatter)
    dp_xbc, g_conv_w, g_conv_b = _conv_bwd(p_xbc, conv_w, silu_slope, dxbc_act, "conv_bwd")
    dp_r = _retention_bwd(p_r, cos, sin, ret_states, dy_r, "retention_bwd")
    g_w_in = jnp.concatenate([
        _matmul(dp_r, h, "tn", BF16, "grad_w_ret"),
        _matmul(dp_z, h, "tn", BF16, "grad_w_z"),
        _matmul(dp_xbc, h, "tn", BF16, "grad_w_xbc"),
        _matmul(dp_dt, h, "tn", BF16, "grad_w_dt")[:SSD_HEADS],
        _matmul(dp_g, h, "tn", BF16, "grad_w_gates"),
    ], axis=0)
    blocks = g_w_in.reshape(N_DEV, W_IN_SHARD, D_MODEL)
    dh, from_sibling = _matmul(dp_r, w_r, "nn", F32, "d_h_ret", rider=_sibling_rider(blocks))
    own = lax.dynamic_index_in_dim(blocks.reshape(4, 2, W_IN_SHARD, D_MODEL), my_core, axis=1, keepdims=False)
    chip_sum = _add_bf16(own.reshape(4 * W_IN_SHARD, D_MODEL), from_sibling.reshape(4 * W_IN_SHARD, D_MODEL),
                         "w_in_chip_sum").reshape(4, W_IN_SHARD, D_MODEL)
    carriers = (("d_h_xbc", dp_xbc, w_xbc), ("d_h_gates", dp_g, w_g), ("d_h_z", dp_z, w_z))
    parts = []
    for (row0, rows), (nm, dp, w) in zip(W_IN_ROW_PARTS, carriers):
        dh, landed = _matmul(dp, w, "nn", F32, nm, add=dh, rider=_chip_rider(chip_sum, row0, rows))
        parts.append(_sum_slots(landed, "w_in_sum_rows_%d" % row0))
    dh = _matmul(dp_dt, w_dt, "nn", F32, "d_h_dt", add=dh)
    grad_x, g_norm1 = _rmsnorm_bwd(x, norm1_w, dh, dx2, "rmsnorm1_bwd")
    small = dict(norm1_w=g_norm1, conv_w=g_conv_w, conv_b=g_conv_b, dt_bias=g_dtb[:, :SSD_HEADS],
                 a_log=g_alog[:, :SSD_HEADS], d_skip=g_dsk[:, :SSD_HEADS], ssd_norm_w=g_ssd_norm,
                 norm_f_w=g_norm_f)
    big = dict(w_in=jnp.concatenate(parts, axis=0), w_br_ret=_sum_slots(got_ret, "w_br_ret_sum"),
               w_br_ssd=_sum_slots(got_ssd, "w_br_ssd_sum"), w_out=_sum_slots(got_out, "w_out_sum"))
    return loss_acc[0, 0], grad_x, small, big


def _all_gather(shard, name):
    rider = _gather_rider([shard], 0.0, 0.0)

    def body(x_ref, out_ref, send_sems, recv_sems):
        for _, fn in rider.phases:
            fn((x_ref,), (out_ref,), send_sems, recv_sems)

    return pl.pallas_call(
        body, out_shape=rider.out_shapes[0], in_specs=rider.in_specs(), out_specs=rider.out_specs()[0],
        scratch_shapes=rider.scratch(), name=name)(shard)


def _all_reduce_small(vec, name):
    r, c = vec.shape

    def body(x_ref, out_ref, land, send_sems, recv_sems):
        x, y, cc = _mesh_pos()
        my_idx = 4 * x + 2 * y + cc
        land[my_idx] = x_ref[...]
        copies = []
        for k in range(1, N_DEV):
            px, py, pc = x ^ (k >> 2), y ^ ((k >> 1) & 1), cc ^ (k & 1)
            cp = pltpu.make_async_remote_copy(
                src_ref=x_ref, dst_ref=land.at[my_idx],
                send_sem=send_sems.at[k - 1], recv_sem=recv_sems.at[k - 1],
                device_id=(px, py, pc), device_id_type=MESH)
            cp.start()
            copies.append(cp)
        for k in range(1, N_DEV):
            px, py, pc = x ^ (k >> 2), y ^ ((k >> 1) & 1), cc ^ (k & 1)
            pltpu.make_async_remote_copy(
                src_ref=x_ref, dst_ref=land.at[4 * px + 2 * py + pc],
                send_sem=send_sems.at[k - 1], recv_sem=recv_sems.at[k - 1],
                device_id=(px, py, pc), device_id_type=MESH).wait_recv()
        for cp in copies:
            cp.wait_send()
        acc = land[0]
        for i in range(1, N_DEV):
            acc = acc + land[i]
        out_ref[...] = acc

    return pl.pallas_call(
        body,
        out_shape=jax.ShapeDtypeStruct((r, c), F32),
        in_specs=[pl.BlockSpec(memory_space=pltpu.VMEM)],
        out_specs=pl.BlockSpec(memory_space=pltpu.VMEM),
        scratch_shapes=[pltpu.VMEM((N_DEV, r, c), F32), pltpu.SemaphoreType.DMA((7,)),
                        pltpu.SemaphoreType.DMA((7,))],
        name=name)(vec)


_SMALL = ("norm1_w", "conv_w", "conv_b", "dt_bias", "a_log", "d_skip", "ssd_norm_w", "norm_f_w")
_SMALL_COLS = 128
_W_IN = "w_in"
_WEIGHTS = ("norm1_w", "w_in", "conv_w", "conv_b", "dt_bias", "a_log", "d_skip", "ssd_norm_w",
            "w_br_ret", "w_br_ssd", "w_out", "norm_f_w")


def _pack(parts):
    flat = jnp.concatenate([p.reshape(-1).astype(F32) for p in parts])
    rows = -(-flat.shape[0] // (8 * _SMALL_COLS)) * 8
    return jnp.pad(flat, (0, rows * _SMALL_COLS - flat.shape[0])).reshape(rows, _SMALL_COLS)


def _unpack(packed, shapes):
    flat = packed.reshape(-1)
    out, off = [], 0
    for shp in shapes:
        n = int(np.prod(shp))
        out.append(flat[off:off + n].reshape(shp))
        off += n
    return out


def kernel(x, positions, norm1_w, w_in, conv_w, conv_b, dt_bias, a_log, d_skip, ssd_norm_w, w_br_ret, w_br_ssd, w_out, norm_f_w, loss_target, m_norm1_w, m_w_in, m_conv_w, m_conv_b, m_dt_bias, m_a_log, m_d_skip, m_ssd_norm_w, m_w_br_ret, m_w_br_ssd, m_w_out, m_norm_f_w, v_norm1_w, v_w_in, v_conv_w, v_conv_b, v_dt_bias, v_a_log, v_d_skip, v_ssd_norm_w, v_w_br_ret, v_w_br_ssd, v_w_out, v_norm_f_w):
    w = dict(norm1_w=norm1_w, w_in=w_in, conv_w=conv_w, conv_b=conv_b, dt_bias=dt_bias, a_log=a_log,
             d_skip=d_skip, ssd_norm_w=ssd_norm_w, w_br_ret=w_br_ret, w_br_ssd=w_br_ssd, w_out=w_out,
             norm_f_w=norm_f_w)
    m = dict(norm1_w=m_norm1_w, w_in=m_w_in, conv_w=m_conv_w, conv_b=m_conv_b, dt_bias=m_dt_bias,
             a_log=m_a_log, d_skip=m_d_skip, ssd_norm_w=m_ssd_norm_w, w_br_ret=m_w_br_ret,
             w_br_ssd=m_w_br_ssd, w_out=m_w_out, norm_f_w=m_norm_f_w)
    v = dict(norm1_w=v_norm1_w, w_in=v_w_in, conv_w=v_conv_w, conv_b=v_conv_b, dt_bias=v_dt_bias,
             a_log=v_a_log, d_skip=v_d_skip, ssd_norm_w=v_ssd_norm_w, w_br_ret=v_w_br_ret,
             w_br_ssd=v_w_br_ssd, w_out=v_w_out, norm_f_w=v_norm_f_w)
    s = x.shape[1]
    my_idx = 4 * lax.axis_index("x") + 2 * lax.axis_index("y") + lax.axis_index("c")

    w[_W_IN], m[_W_IN], v[_W_IN] = w_in[0].T, m_w_in[0].T, v_w_in[0].T
    w_in_all = _all_gather(_cast_bf16(w[_W_IN], "cast_w_in"), "gather_w_in").reshape(IN_PROJ, D_MODEL)
    w_dt = jnp.pad(w_in_all[OFF_DT:OFF_G], ((0, HEAD_PAD - SSD_HEADS), (0, 0)))

    loss_part, grad_x, g_small, g_big = _device_step(
        x[0], positions.reshape(s, 1), loss_target[0], norm1_w, conv_w[0], conv_b, dt_bias, a_log, d_skip,
        ssd_norm_w, norm_f_w.reshape(1, D_MODEL),
        w_in_all[:OFF_Z], w_in_all[OFF_Z:OFF_XBC], w_in_all[OFF_XBC:OFF_DT], w_dt, w_in_all[OFF_G:],
        _cast_bf16(w_br_ret[0], "cast_w_br_ret"), _cast_bf16(w_br_ssd[0], "cast_w_br_ssd"),
        _cast_bf16(w_out[0], "cast_w_out"))

    loss = lax.psum(loss_part, ("x", "y", "c"))

    small_shapes = [g_small[n].shape for n in _SMALL]
    summed = _unpack(_all_reduce_small(_pack([g_small[n] for n in _SMALL]), "allreduce_small"), small_shapes)
    grads = dict(zip(_SMALL, summed))
    conv_cols = SSD_CD // N_DEV
    grads["conv_w"] = lax.dynamic_slice_in_dim(grads["conv_w"], my_idx * conv_cols, conv_cols, axis=1)
    grads["norm_f_w"] = grads["norm_f_w"].reshape(D_MODEL)
    for n in ("norm1_w", "conv_w", "conv_b", "dt_bias", "a_log", "d_skip", "ssd_norm_w"):
        grads[n] = grads[n].reshape(w[n].shape)

    delta, new_m, new_v = {}, {}, {}
    for n in ("w_br_ret", "w_br_ssd", "w_out"):
        w[n], m[n], v[n] = w[n][0], m[n][0], v[n][0]
    for n in (_W_IN, "w_br_ret", "w_br_ssd", "w_out"):
        back = (lambda a: a.T[None]) if n == _W_IN else (lambda a: a[None])
        res = _adamw(w[n], g_big[n], m[n], v[n], "adamw_" + n)
        grads[n] = back(g_big[n])
        delta[n], new_m[n], new_v[n] = (back(a) for a in res)
    shapes = [w[n].shape for n in _SMALL]
    packed = _adamw(_pack([w[n] for n in _SMALL]), _pack([grads[n] for n in _SMALL]),
                    _pack([m[n] for n in _SMALL]), _pack([v[n] for n in _SMALL]), "adamw_small")
    for res, dst in zip(packed, (delta, new_m, new_v)):
        for n, a in zip(_SMALL, _unpack(res, shapes)):
            dst[n] = a

    return (loss, grad_x.reshape(x.shape), *[grads[n] for n in _WEIGHTS], *[delta[n] for n in _WEIGHTS],
            *[new_m[n] for n in _WEIGHTS], *[new_v[n] for n in _WEIGHTS])
```

```python
import functools

import numpy as np
import jax
import jax.numpy as jnp
from jax import lax
from jax.experimental import pallas as pl
from jax.experimental.pallas import tpu as pltpu

F32 = jnp.float32
BF16 = jnp.bfloat16

D_MODEL = 2048
CHUNK = 64
EPS = 1e-6
N_DEV = 8

RET_HEADS = 8
RET_DK = 256
RET_W = RET_HEADS * RET_DK
ROPE_THETA = 10000.0
ROPE_HALF = RET_DK // 2

SSD_W = 4096
SSD_P = 64
SSD_HEADS = 64
SSD_GROUPS = 8
SSD_N = 128
SSD_GW = SSD_W // SSD_GROUPS
SSD_QW = 256
SSD_BLOCK_GROUPS = 2
SSD_CONV = 4
SSD_CD = SSD_W + 2 * SSD_GROUPS * SSD_N
HEAD_PAD = 128
B_OFF = SSD_W
C_OFF = SSD_W + SSD_GROUPS * SSD_N

ADAM_LR = 0.001
ADAM_B1 = 0.9
ADAM_B2 = 0.999
ADAM_EPS = 1e-08
ADAM_WD = 0.01
ADAM_STEP = 10

SPLITS = (RET_W, RET_W, RET_W, RET_W, SSD_W, SSD_CD, SSD_HEADS, D_MODEL, D_MODEL)
IN_PROJ = sum(SPLITS)
OFF_Z = 4 * RET_W
OFF_XBC = OFF_Z + SSD_W
OFF_DT = OFF_XBC + SSD_CD
OFF_G = OFF_DT + SSD_HEADS

ROW_TILE = 256
CONV_TILE = 128
MM_TILE = 1024
MM_TK = 2048

MESH = pl.DeviceIdType.MESH


def _dot(a, b):
    return lax.dot_general(a, b, (((1,), (0,)), ((), ())), preferred_element_type=F32)


def _dot_nt(a, b):
    return lax.dot_general(a, b, (((1,), (1,)), ((), ())), preferred_element_type=F32)


def _dot_tn(a, b):
    return lax.dot_general(a, b, (((0,), (0,)), ((), ())), preferred_element_type=F32)


def _bf(x):
    return x.astype(BF16)


def _split3(x):
    hi = x.astype(BF16)
    r = x - hi.astype(F32)
    mid = r.astype(BF16)
    lo = (r - mid.astype(F32)).astype(BF16)
    return hi, mid, lo


def _dot_exact_l(x, sel, pieces=3):
    hi, mid, lo = _split3(x)
    r = _dot(hi, sel) + _dot(mid, sel)
    return r + _dot(lo, sel) if pieces == 3 else r


def _dot_exact_r(sel, x):
    hi, mid, lo = _split3(x)
    return _dot(sel, hi) + _dot(sel, mid) + _dot(sel, lo)


def _sigmoid(x):
    return 1.0 / (1.0 + jnp.exp(-x))


def _softplus(x):
    return jnp.maximum(x, 0.0) + jnp.log(1.0 + jnp.exp(-jnp.abs(x)))


def _iota(shape, axis):
    return lax.broadcasted_iota(jnp.int32, shape, axis)


def _params(*sem):
    return pltpu.CompilerParams(dimension_semantics=sem)


class _Rider:
    def __init__(self, inputs, out_shapes, n_sems, phases):
        self.inputs, self.out_shapes, self.n_sems, self.phases = tuple(inputs), tuple(out_shapes), n_sems, phases

    def in_specs(self):
        return [pl.BlockSpec(memory_space=pl.ANY)] * len(self.inputs)

    def out_specs(self):
        return [pl.BlockSpec(memory_space=pl.ANY)] * len(self.out_shapes)

    def scratch(self):
        return [pltpu.SemaphoreType.DMA((self.n_sems,)), pltpu.SemaphoreType.DMA((self.n_sems,))]

    def run(self, step, n_steps, ins, outs, send_sems, recv_sems, last):
        for frac, fn in self.phases:
            if (frac >= 1.0) != last:
                continue
            at = min(int(frac * n_steps), n_steps - 1)

            @pl.when(step == at)
            def _(fn=fn):
                fn(ins, outs, send_sems, recv_sems)


def _mesh_pos():
    return lax.axis_index("x"), lax.axis_index("y"), lax.axis_index("c")


def _gather_rider(shards, relay_at, pass_at):
    n = len(shards)

    def tools(a, ins, outs, send_sems, recv_sems):
        x, y, cc = _mesh_pos()
        nbrs = [(1 - x, y), (x, 1 - y)]
        diag = (1 - x, 1 - y)
        relay_from, relay_to = (x ^ cc, y ^ (1 - cc)), (x ^ (1 - cc), y ^ cc)

        def slot(px, py, pc):
            return outs[a].at[4 * px + 2 * py + pc]

        def copy(k, block, to, src=None):
            return pltpu.make_async_remote_copy(
                src_ref=slot(*block) if src is None else src, dst_ref=slot(*block),
                send_sem=send_sems.at[8 * a + k], recv_sem=recv_sems.at[8 * a + k], device_id=to, device_id_type=MESH)

        me, sibling = (x, y, cc), (x, y, 1 - cc)
        return dict(
            mine=lambda: pltpu.make_async_copy(ins[a], slot(*me), send_sems.at[8 * a + 7]),
            first=lambda: [copy(0, me, sibling, src=ins[a])] + [copy(1 + j, me, (*chip, cc), src=ins[a])
                                                                for j, chip in enumerate(nbrs)],
            landed=lambda: [copy(1 + j, (*chip, cc), me) for j, chip in enumerate(nbrs)],
            relay=lambda: copy(3, (*relay_from, cc), (*relay_to, cc)),
            relayed_in=lambda: copy(3, (*diag, cc), me),
            passed=lambda: [copy(4 + j, (*chip, cc), sibling) for j, chip in enumerate(nbrs)],
            passed_diag=lambda: copy(6, (*diag, cc), sibling),
            from_sibling=lambda: [copy(0, sibling, me)] + [copy(4 + j, (*chip, 1 - cc), me)
                                                           for j, chip in enumerate(nbrs + [diag])])

    def start(*refs):
        for a in range(n):
            t = tools(a, *refs)
            t["mine"]().start()
            for cp in t["first"]():
                cp.start()

    def relay(*refs):
        for a in range(n):
            t = tools(a, *refs)
            for got in t["landed"]():
                got.wait_recv()
            t["relay"]().start()
            for cp in t["passed"]():
                cp.start()

    def pass_diag(*refs):
        for a in range(n):
            t = tools(a, *refs)
            t["relayed_in"]().wait_recv()
            t["passed_diag"]().start()

    def finish(*refs):
        for a in range(n):
            t = tools(a, *refs)
            for cp in t["from_sibling"]():
                cp.wait_recv()
            for cp in t["first"]() + [t["relay"]()] + t["passed"]() + [t["passed_diag"]()]:
                cp.wait_send()
            t["mine"]().wait()

    outs = [jax.ShapeDtypeStruct((N_DEV,) + s.shape, s.dtype) for s in shards]
    return _Rider(shards, outs, 8 * n, ((0.0, start), (relay_at, relay), (pass_at, pass_diag), (1.0, finish)))


def _scatter_rider(blocks):
    n = len(blocks)

    def copies(a, ins, outs, send_sems, recv_sems, with_back):
        x, y, cc = _mesh_pos()
        my_idx = 4 * x + 2 * y + cc
        mine = pltpu.make_async_copy(ins[a].at[my_idx], outs[a].at[my_idx], send_sems.at[8 * a + 7])
        out, back = [], []
        for k in range(1, N_DEV):
            px, py, pc = x ^ (k >> 2), y ^ ((k >> 1) & 1), cc ^ (k & 1)
            peer_idx = 4 * px + 2 * py + pc
            sems = dict(send_sem=send_sems.at[8 * a + k - 1], recv_sem=recv_sems.at[8 * a + k - 1],
                        device_id=(px, py, pc), device_id_type=MESH)
            out.append(pltpu.make_async_remote_copy(src_ref=ins[a].at[peer_idx], dst_ref=outs[a].at[my_idx], **sems))
            if with_back:
                back.append(pltpu.make_async_remote_copy(src_ref=ins[a].at[my_idx], dst_ref=outs[a].at[peer_idx], **sems))
        return mine, out, back

    def start(*refs):
        for a in range(n):
            mine, out, _ = copies(a, *refs, False)
            mine.start()
            for cp in out:
                cp.start()

    def finish(*refs):
        for a in range(n):
            mine, out, back = copies(a, *refs, True)
            for cp in back:
                cp.wait_recv()
            for cp in out:
                cp.wait_send()
            mine.wait()

    outs = [jax.ShapeDtypeStruct(b.shape, b.dtype) for b in blocks]
    return _Rider(blocks, outs, 8 * n, ((0.0, start), (1.0, finish)))


def _sibling_rider(blocks):
    _, r, c = blocks.shape

    def copies(ins, outs, send_sems, recv_sems):
        x, y, cc = _mesh_pos()
        return [pltpu.make_async_remote_copy(
            src_ref=ins[0].at[2 * j + 1 - cc], dst_ref=outs[0].at[j], send_sem=send_sems.at[j],
            recv_sem=recv_sems.at[j], device_id=(x, y, 1 - cc), device_id_type=MESH) for j in range(4)]

    def start(*refs):
        for cp in copies(*refs):
            cp.start()

    def finish(*refs):
        for cp in copies(*refs):
            cp.wait_recv()
        for cp in copies(*refs):
            cp.wait_send()

    return _Rider([blocks], [jax.ShapeDtypeStruct((4, r, c), blocks.dtype)], 4, ((0.0, start), (1.0, finish)))


def _chip_rider(partial, row0, rows):
    _, _, c = partial.shape

    def copies(ins, outs, send_sems, recv_sems, with_back):
        x, y, cc = _mesh_pos()
        my_chip = 2 * x + y

        def rows_of(j):
            return ins[0].at[j, pl.ds(row0, rows)]

        mine = pltpu.make_async_copy(rows_of(my_chip), outs[0].at[my_chip], send_sems.at[3])
        out, back = [], []
        for k in range(1, 4):
            px, py = x ^ (k >> 1), y ^ (k & 1)
            peer = 2 * px + py
            sems = dict(send_sem=send_sems.at[k - 1], recv_sem=recv_sems.at[k - 1], device_id=(px, py, cc),
                        device_id_type=MESH)
            out.append(pltpu.make_async_remote_copy(src_ref=rows_of(peer), dst_ref=outs[0].at[my_chip], **sems))
            if with_back:
                back.append(pltpu.make_async_remote_copy(src_ref=rows_of(my_chip), dst_ref=outs[0].at[peer], **sems))
        return mine, out, back

    def start(*refs):
        mine, out, _ = copies(*refs, False)
        mine.start()
        for cp in out:
            cp.start()

    def finish(*refs):
        mine, out, back = copies(*refs, True)
        for cp in back:
            cp.wait_recv()
        for cp in out:
            cp.wait_send()
        mine.wait()

    return _Rider([partial], [jax.ShapeDtypeStruct((4, rows, c), partial.dtype)], 4, ((0.0, start), (1.0, finish)))


def _matmul(a, b, mode, out_dtype, name, add=None, rider=None, b_rows=None):
    b_shape = b.shape if b_rows is None else (b_rows[1], b.shape[1])
    if mode == "nn":
        (m, k), (k2, n) = a.shape, b_shape
    elif mode == "nt":
        (m, k), (n, k2) = a.shape, b_shape
    else:
        (k, m), (k2, n) = a.shape, b_shape
    assert k == k2, (a.shape, b_shape, mode)
    tm, tn, tk = min(m, MM_TILE), min(n, MM_TILE), min(k, MM_TK)
    assert m % tm == 0 and n % tn == 0 and k % tk == 0, (m, n, k)
    nk = k // tk
    b_tile_rows = tn if mode == "nt" else tk
    b_off = 0 if b_rows is None else b_rows[0] // b_tile_rows
    assert b_rows is None or b_rows[0] % b_tile_rows == 0, (b_rows, b_tile_rows)
    dot = {"nn": _dot, "nt": _dot_nt, "tn": _dot_tn}[mode]

    gm, gn = m // tm, n // tn
    n_rin = len(rider.inputs) if rider else 0
    n_rout = len(rider.out_shapes) if rider else 0
    n_add = 1 if add is not None else 0

    def body(*refs):
        a_ref, b_ref = refs[:2]
        add_ref = refs[2] if n_add else None
        r_ins = refs[2 + n_add:2 + n_add + n_rin]
        o_ref = refs[2 + n_add + n_rin]
        r_outs = refs[3 + n_add + n_rin:3 + n_add + n_rin + n_rout]
        scratch = refs[3 + n_add + n_rin + n_rout:]
        acc_ref = scratch[0] if nk > 1 else None
        kk = pl.program_id(2)
        if rider:
            step = (pl.program_id(0) * gn + pl.program_id(1)) * nk + kk
            ride = functools.partial(rider.run, step, gm * gn * nk, r_ins, r_outs, scratch[-2], scratch[-1])
            ride(last=False)

        def finish(r):
            if add_ref is not None:
                r = r + add_ref[...]
            o_ref[...] = r.astype(o_ref.dtype)

        if nk == 1:
            finish(dot(_bf(a_ref[...]), _bf(b_ref[...])))
        else:
            @pl.when(kk == 0)
            def _():
                acc_ref[...] = dot(_bf(a_ref[...]), _bf(b_ref[...]))

            @pl.when(jnp.logical_and(kk > 0, kk < nk - 1))
            def _():
                acc_ref[...] += dot(_bf(a_ref[...]), _bf(b_ref[...]))

            @pl.when(kk == nk - 1)
            def _():
                finish(acc_ref[...] + dot(_bf(a_ref[...]), _bf(b_ref[...])))
        if rider:
            ride(last=True)

    if mode == "nn":
        a_spec = pl.BlockSpec((tm, tk), lambda i, j, kk: (i, kk))
        b_spec = pl.BlockSpec((tk, tn), lambda i, j, kk: (kk + b_off, j))
    elif mode == "nt":
        a_spec = pl.BlockSpec((tm, tk), lambda i, j, kk: (i, kk))
        b_spec = pl.BlockSpec((tn, tk), lambda i, j, kk: (j + b_off, kk))
    else:
        a_spec = pl.BlockSpec((tk, tm), lambda i, j, kk: (kk, i))
        b_spec = pl.BlockSpec((tk, tn), lambda i, j, kk: (kk + b_off, j))
    o_spec = pl.BlockSpec((tm, tn), lambda i, j, kk: (i, j))
    in_specs = [a_spec, b_spec] + ([o_spec] if add is not None else [])
    args = (a, b) + ((add,) if add is not None else ())
    block_bytes = (tm * tk * a.dtype.itemsize + tk * tn * b.dtype.itemsize
                   + tm * tn * (jnp.dtype(out_dtype).itemsize + (4 if add is not None else 0)))
    vmem = 2 * block_bytes + 2 * tm * tn * 4 + 2 * (tm + tn) * tk + (4 << 20)
    out_shape = jax.ShapeDtypeStruct((m, n), out_dtype)
    scratch = [pltpu.VMEM((tm, tn), F32)] if nk > 1 else []
    if rider:
        in_specs = in_specs + rider.in_specs()
        args = args + rider.inputs
        out_shape = (out_shape, *rider.out_shapes)
        o_spec = (o_spec, *rider.out_specs())
        scratch = scratch + rider.scratch()
    sem = ("arbitrary",) * 3 if rider else ("parallel", "parallel", "arbitrary")
    return pl.pallas_call(
        body,
        out_shape=out_shape,
        grid=(gm, gn, nk),
        in_specs=in_specs,
        out_specs=o_spec,
        scratch_shapes=scratch,
        compiler_params=pltpu.CompilerParams(dimension_semantics=sem, vmem_limit_bytes=int(vmem)),
        name=name,
    )(*args)


def _row_spec(width, tile=ROW_TILE):
    return pl.BlockSpec((tile, width), lambda i: (i, 0))


def _full_spec(shape):
    nd = len(shape)
    return pl.BlockSpec(shape, lambda *_: (0,) * nd)


def _tiling_2d(r, c):
    if r <= ROW_TILE or r % ROW_TILE == 0:
        tr = min(r, ROW_TILE)
        return (tr, c), r // tr, (lambda i: (i, 0))
    tc = 128 if r > 4096 else 256
    assert c % tc == 0, (r, c)
    return (r, tc), c // tc, (lambda i: (0, i))


def _spec_2d(r, c):
    blk, grid, idx = _tiling_2d(r, c)
    return pl.BlockSpec(blk, idx), grid


def _cast_bf16(x, name):
    r, c = x.shape
    sp, grid = _spec_2d(r, c)

    def body(x_ref, o_ref):
        o_ref[...] = _bf(x_ref[...])

    return pl.pallas_call(
        body, out_shape=jax.ShapeDtypeStruct((r, c), BF16), grid=(grid,), in_specs=[sp], out_specs=sp,
        compiler_params=_params("parallel"), name=name)(x)


def _rmsnorm_fwd(x, w, name, rider):
    s, d = x.shape
    steps = s // ROW_TILE
    n_rin, n_rout = len(rider.inputs), len(rider.out_shapes)

    def body(x_ref, w_ref, *refs):
        r_ins, o_ref, r_outs = refs[:n_rin], refs[n_rin], refs[n_rin + 1:n_rin + 1 + n_rout]
        ride = functools.partial(rider.run, pl.program_id(0), steps, r_ins, r_outs, refs[-2], refs[-1])
        ride(last=False)
        xv = x_ref[...]
        rstd = lax.rsqrt(jnp.mean(xv * xv, axis=1, keepdims=True) + EPS)
        o_ref[...] = _bf(xv * rstd * w_ref[...])
        ride(last=True)

    return pl.pallas_call(
        body, out_shape=(jax.ShapeDtypeStruct((s, d), BF16), *rider.out_shapes), grid=(steps,),
        in_specs=[_row_spec(d), _full_spec((1, d)), *rider.in_specs()],
        out_specs=(_row_spec(d), *rider.out_specs()), scratch_shapes=rider.scratch(),
        compiler_params=_params("arbitrary"), name=name)(x, w, *rider.inputs)


def _rmsnorm_bwd(x, w, dh, dres, name):
    s, d = x.shape

    def body(x_ref, w_ref, dh_ref, dres_ref, dx_ref, dw_ref):
        @pl.when(pl.program_id(0) == 0)
        def _():
            dw_ref[...] = jnp.zeros_like(dw_ref)

        xv = x_ref[...]
        rstd = lax.rsqrt(jnp.mean(xv * xv, axis=1, keepdims=True) + EPS)
        xhat = xv * rstd
        dhv = dh_ref[...]
        dxhat = dhv * w_ref[...]
        dx = rstd * (dxhat - xhat * jnp.mean(dxhat * xhat, axis=1, keepdims=True))
        dx_ref[...] = dx + dres_ref[...]
        dw_ref[...] += jnp.sum(dhv * xhat, axis=0, keepdims=True)

    return pl.pallas_call(
        body,
        out_shape=(jax.ShapeDtypeStruct((s, d), F32), jax.ShapeDtypeStruct((1, d), F32)),
        grid=(s // ROW_TILE,),
        in_specs=[_row_spec(d), _full_spec((1, d)), _row_spec(d), _row_spec(d)],
        out_specs=(_row_spec(d), _full_spec((1, d))),
        compiler_params=_params("arbitrary"), name=name)(x, w, dh, dres)


def _rope_tables(pos_col, inv_freq, name):
    s = pos_col.shape[0]

    def body(p_ref, f_ref, cos_ref, sin_ref):
        ang = p_ref[...].astype(F32) * f_ref[...]
        cos_ref[...] = jnp.cos(ang)
        sin_ref[...] = jnp.sin(ang)

    out = jax.ShapeDtypeStruct((s, ROPE_HALF), F32)
    return pl.pallas_call(
        body, out_shape=(out, out), grid=(s // ROW_TILE,),
        in_specs=[_row_spec(1), _full_spec((1, ROPE_HALF))],
        out_specs=(_row_spec(ROPE_HALF), _row_spec(ROPE_HALF)),
        compiler_params=_params("parallel"), name=name)(pos_col, inv_freq)


def _merge_fwd(pg, p_r, p_s, name):
    s = pg.shape[0]

    def body(g_ref, r_ref, s_ref, o_ref):
        g = g_ref[...].astype(F32)
        o_ref[...] = _bf(_sigmoid(g[:, :D_MODEL]) * r_ref[...].astype(F32)
                         + _sigmoid(g[:, D_MODEL:]) * s_ref[...].astype(F32))

    return pl.pallas_call(
        body, out_shape=jax.ShapeDtypeStruct((s, D_MODEL), BF16), grid=(s // ROW_TILE,),
        in_specs=[_row_spec(2 * D_MODEL), _row_spec(D_MODEL), _row_spec(D_MODEL)],
        out_specs=_row_spec(D_MODEL), compiler_params=_params("parallel"), name=name)(pg, p_r, p_s)


def _merge_bwd(pg, p_r, p_s, dm, name):
    s = pg.shape[0]

    def body(g_ref, r_ref, s_ref, dm_ref, dr_ref, ds_ref, dg_ref):
        g = g_ref[...].astype(F32)
        sr, ss = _sigmoid(g[:, :D_MODEL]), _sigmoid(g[:, D_MODEL:])
        d = dm_ref[...]
        dr_ref[...] = _bf(d * sr)
        ds_ref[...] = _bf(d * ss)
        dg_ref[:, :D_MODEL] = _bf(d * r_ref[...].astype(F32) * sr * (1.0 - sr))
        dg_ref[:, D_MODEL:] = _bf(d * s_ref[...].astype(F32) * ss * (1.0 - ss))

    o = jax.ShapeDtypeStruct((s, D_MODEL), BF16)
    return pl.pallas_call(
        body, out_shape=(o, o, jax.ShapeDtypeStruct((s, 2 * D_MODEL), BF16)), grid=(s // ROW_TILE,),
        in_specs=[_row_spec(2 * D_MODEL), _row_spec(D_MODEL), _row_spec(D_MODEL), _row_spec(D_MODEL)],
        out_specs=(_row_spec(D_MODEL), _row_spec(D_MODEL), _row_spec(2 * D_MODEL)),
        compiler_params=_params("parallel"), name=name)(pg, p_r, p_s, dm)


def _final_fwd_bwd(x, o, w, target, name):
    s, d = x.shape

    def body(x_ref, o_ref, w_ref, t_ref, dx_ref, dw_ref, loss_ref):
        @pl.when(pl.program_id(0) == 0)
        def _():
            dw_ref[...] = jnp.zeros_like(dw_ref)
            loss_ref[...] = jnp.zeros_like(loss_ref)

        x2 = x_ref[...] + o_ref[...]
        rstd = lax.rsqrt(jnp.mean(x2 * x2, axis=1, keepdims=True) + EPS)
        xhat = x2 * rstd
        wv = w_ref[...]
        err = xhat * wv - t_ref[...]
        loss_ref[...] += jnp.sum(jnp.sum(err * err, axis=1, keepdims=True), axis=0, keepdims=True) * (0.5 / d)
        dy = err * (1.0 / d)
        dw_ref[...] += jnp.sum(dy * xhat, axis=0, keepdims=True)
        dxhat = dy * wv
        dx_ref[...] = rstd * (dxhat - xhat * jnp.mean(dxhat * xhat, axis=1, keepdims=True))

    return pl.pallas_call(
        body,
        out_shape=(jax.ShapeDtypeStruct((s, d), F32), jax.ShapeDtypeStruct((1, d), F32),
                   jax.ShapeDtypeStruct((8, 128), F32)),
        grid=(s // ROW_TILE,),
        in_specs=[_row_spec(d), _row_spec(d), _full_spec((1, d)), _row_spec(d)],
        out_specs=(_row_spec(d), _full_spec((1, d)), _full_spec((8, 128))),
        compiler_params=_params("arbitrary"), name=name)(x, o, w, target)


def _adamw(w, g, m, v, name):
    r, c = w.shape
    sp, grid = _spec_2d(r, c)
    c1 = 1.0 / (1.0 - ADAM_B1 ** ADAM_STEP)
    c2 = 1.0 / (1.0 - ADAM_B2 ** ADAM_STEP)

    def body(w_ref, g_ref, m_ref, v_ref, d_ref, nm_ref, nv_ref):
        gv = g_ref[...]
        nm = ADAM_B1 * m_ref[...] + (1.0 - ADAM_B1) * gv
        nv = ADAM_B2 * v_ref[...] + (1.0 - ADAM_B2) * (gv * gv)
        d_ref[...] = -ADAM_LR * ((nm * c1) / (jnp.sqrt(nv * c2) + ADAM_EPS) + ADAM_WD * w_ref[...])
        nm_ref[...] = nm
        nv_ref[...] = nv

    o = jax.ShapeDtypeStruct((r, c), F32)
    return pl.pallas_call(
        body, out_shape=(o, o, o), grid=(grid,), in_specs=[sp, sp, sp, sp], out_specs=(sp, sp, sp),
        compiler_params=_params("parallel"), name=name)(w, g, m, v)


def _sum_slots(land, name):
    n, r, c = land.shape
    blk, grid, idx = _tiling_2d(r, c)

    def body(l_ref, o_ref):
        acc = l_ref[0].astype(F32)
        for i in range(1, n):
            acc = acc + l_ref[i].astype(F32)
        o_ref[...] = acc

    return pl.pallas_call(
        body, out_shape=jax.ShapeDtypeStruct((r, c), F32), grid=(grid,),
        in_specs=[pl.BlockSpec((n, *blk), lambda i: (0, *idx(i)))], out_specs=pl.BlockSpec(blk, idx),
        compiler_params=_params("parallel"), name=name)(land)


SLAB_COLS = 256


def _chip_sum(blocks, from_sibling, my_core, name):
    _, r, c = blocks.shape

    def body(core_ref, a_ref, b_ref, o_ref):
        o_ref[...] = _bf(a_ref[...].astype(F32) + b_ref[...].astype(F32))

    def slab(chip_of):
        return pl.BlockSpec((1, r, SLAB_COLS), lambda j, i, core: (chip_of(j, core), 0, i))

    grid_spec = pltpu.PrefetchScalarGridSpec(
        num_scalar_prefetch=1, grid=(4, c // SLAB_COLS),
        in_specs=[slab(lambda j, core: 2 * j + core[0]), slab(lambda j, core: j)],
        out_specs=slab(lambda j, core: j))
    return pl.pallas_call(
        body, out_shape=jax.ShapeDtypeStruct((4, r, c), BF16), grid_spec=grid_spec,
        compiler_params=_params("parallel", "parallel"), name=name,
    )(my_core.astype(jnp.int32).reshape(1), blocks, from_sibling)


def _sum_row_parts(parts, name):
    c = parts[0].shape[2]
    rows = [p.shape[1] for p in parts]

    def body(*refs):
        o_ref, off = refs[-1], 0
        for p_ref, n in zip(refs[:-1], rows):
            acc = p_ref[0].astype(F32)
            for i in range(1, p_ref.shape[0]):
                acc = acc + p_ref[i].astype(F32)
            o_ref[off:off + n, :] = acc
            off += n

    return pl.pallas_call(
        body, out_shape=jax.ShapeDtypeStruct((sum(rows), c), F32), grid=(c // SLAB_COLS,),
        in_specs=[pl.BlockSpec((p.shape[0], p.shape[1], SLAB_COLS), lambda i: (0, 0, i)) for p in parts],
        out_specs=pl.BlockSpec((sum(rows), SLAB_COLS), lambda i: (0, i)),
        compiler_params=_params("parallel"), name=name)(*parts)


def _retention_tables():
    lg = np.log1p(-(2.0 ** (-5.0 - np.arange(RET_HEADS, dtype=np.float64))))
    idx = np.arange(CHUNK, dtype=np.float64)
    intra = np.exp(np.abs(idx[:, None] - idx[None, :])[None] * lg[:, None, None])
    qd = np.exp((idx[None, :] + 1.0) * lg[:, None])
    kd = np.exp((CHUNK - 1.0 - idx[None, :]) * lg[:, None])
    cd = np.exp(CHUNK * lg)
    ones = np.ones((1, 1, RET_DK))
    return (jnp.asarray(intra, F32), jnp.asarray(qd[:, :, None] * ones, F32),
            jnp.asarray(kd[:, :, None] * ones, F32), jnp.asarray(cd[:, None, None] * ones, F32))


def _rope(t, cos, sin):
    t1, t2 = t[:, :ROPE_HALF], t[:, ROPE_HALF:]
    return jnp.concatenate([t1 * cos - t2 * sin, t2 * cos + t1 * sin], axis=1)


def _rope_t(d, cos, sin):
    d1, d2 = d[:, :ROPE_HALF], d[:, ROPE_HALF:]
    return jnp.concatenate([d1 * cos + d2 * sin, d2 * cos - d1 * sin], axis=1)


_HEADS = range(RET_HEADS)


def _ret_chunk_fwd(p_ref, cos, sin, intra_ref, qd_ref, st):
    def seg(i, h):
        return p_ref[:, i * RET_W + h * RET_DK:i * RET_W + (h + 1) * RET_DK]

    v = [seg(2, h) for h in _HEADS]
    qr = [_rope(seg(0, h), cos, sin) for h in _HEADS]
    kr = [_rope(seg(1, h), cos, sin) * (RET_DK ** -0.5) for h in _HEADS]
    qrb, vb = [_bf(a) for a in qr], [_bf(a) for a in v]
    sc = [_dot_nt(qrb[h], _bf(kr[h])) * intra_ref[h] for h in _HEADS]
    qs = [_dot(qrb[h], _bf(st[h])) for h in _HEADS]
    y = [_dot(_bf(sc[h]), vb[h]) + qs[h] * qd_ref[h] for h in _HEADS]
    g = [seg(3, h) for h in _HEADS]
    return dict(v=v, vb=vb, qr=qr, qrb=qrb, kr=kr, sc=sc, y=y, g=g)


def _group_norm(y):
    mu = jnp.mean(y, axis=1, keepdims=True)
    yc = y - mu
    rstd = lax.rsqrt(jnp.mean(yc * yc, axis=1, keepdims=True) + EPS)
    return yc * rstd, rstd


def _ret_specs(nc, rev):
    cidx = (lambda c: nc - 1 - c) if rev else (lambda c: c)
    return dict(
        proj=pl.BlockSpec((CHUNK, 4 * RET_W), lambda c: (cidx(c), 0)),
        half=pl.BlockSpec((CHUNK, ROPE_HALF), lambda c: (cidx(c), 0)),
        wide=pl.BlockSpec((CHUNK, RET_W), lambda c: (cidx(c), 0)),
        state=pl.BlockSpec((1, RET_HEADS, RET_DK, RET_DK), lambda c: (cidx(c), 0, 0, 0)),
        intra=_full_spec((RET_HEADS, CHUNK, CHUNK)),
        dec=_full_spec((RET_HEADS, CHUNK, RET_DK)),
        cd=_full_spec((RET_HEADS, 1, RET_DK)),
    )


def _retention_fwd(p_r, cos, sin, name):
    s = p_r.shape[0]
    nc = s // CHUNK
    intra_t, qd_t, kd_t, cd_t = _retention_tables()

    def body(p_ref, cos_ref, sin_ref, intra_ref, qd_ref, kd_ref, cd_ref, y_ref, st_ref, state):
        @pl.when(pl.program_id(0) == 0)
        def _():
            state[...] = jnp.zeros_like(state)

        st = [state[h] for h in _HEADS]
        f = _ret_chunk_fwd(p_ref, cos_ref[...], sin_ref[...], intra_ref, qd_ref, st)
        new_st = [st[h] * cd_ref[h] + _dot_tn(_bf(f["kr"][h] * kd_ref[h]), f["vb"][h]) for h in _HEADS]
        out = [_bf(_group_norm(f["y"][h])[0] * (f["g"][h] * _sigmoid(f["g"][h]))) for h in _HEADS]
        for h in _HEADS:
            st_ref[0, h] = _bf(st[h])
            state[h] = new_st[h]
            y_ref[:, h * RET_DK:(h + 1) * RET_DK] = out[h]

    sp = _ret_specs(nc, False)
    return pl.pallas_call(
        body,
        out_shape=(jax.ShapeDtypeStruct((s, RET_W), BF16),
                   jax.ShapeDtypeStruct((nc, RET_HEADS, RET_DK, RET_DK), BF16)),
        grid=(nc,),
        in_specs=[sp["proj"], sp["half"], sp["half"], sp["intra"], sp["dec"], sp["dec"], sp["cd"]],
        out_specs=(sp["wide"], sp["state"]),
        scratch_shapes=[pltpu.VMEM((RET_HEADS, RET_DK, RET_DK), F32)],
        compiler_params=_params("arbitrary"), name=name)(p_r, cos, sin, intra_t, qd_t, kd_t, cd_t)


def _retention_bwd(p_r, cos, sin, states, dy_r, name):
    s = p_r.shape[0]
    nc = s // CHUNK
    intra_t, qd_t, kd_t, cd_t = _retention_tables()

    def body(p_ref, cos_ref, sin_ref, intra_ref, qd_ref, kd_ref, cd_ref, st_ref, dy_ref, dp_ref, dstate):
        @pl.when(pl.program_id(0) == 0)
        def _():
            dstate[...] = jnp.zeros_like(dstate)

        cos, sin = cos_ref[...], sin_ref[...]
        st = [st_ref[0, h] for h in _HEADS]
        dsn = [dstate[h] for h in _HEADS]
        f = _ret_chunk_fwd(p_ref, cos, sin, intra_ref, qd_ref, st)
        vb, qrb, kr, g = f["vb"], f["qrb"], f["kr"], f["g"]
        norm = [_group_norm(f["y"][h]) for h in _HEADS]
        sg = [_sigmoid(g[h]) for h in _HEADS]
        dyr = [dy_ref[:, h * RET_DK:(h + 1) * RET_DK] for h in _HEADS]
        dyn = [dyr[h] * (g[h] * sg[h]) for h in _HEADS]
        dg = [dyr[h] * norm[h][0] * (sg[h] * (1.0 + g[h] * (1.0 - sg[h]))) for h in _HEADS]
        dy = [norm[h][1] * (dyn[h] - jnp.mean(dyn[h], axis=1, keepdims=True)
                            - norm[h][0] * jnp.mean(dyn[h] * norm[h][0], axis=1, keepdims=True)) for h in _HEADS]
        dyb, dsnb = [_bf(a) for a in dy], [_bf(a) for a in dsn]
        ds = [_bf(_dot_nt(dyb[h], vb[h]) * intra_ref[h]) for h in _HEADS]
        t = [_bf(dy[h] * qd_ref[h]) for h in _HEADS]
        dv = [_dot_tn(_bf(f["sc"][h]), dyb[h]) + _dot(_bf(kr[h] * kd_ref[h]), dsnb[h]) for h in _HEADS]
        dqr = [_dot(ds[h], _bf(kr[h])) + _dot_nt(t[h], _bf(st[h])) for h in _HEADS]
        dkr = [_dot_tn(ds[h], qrb[h]) + _dot_nt(vb[h], dsnb[h]) * kd_ref[h] for h in _HEADS]
        new_ds = [dsn[h] * cd_ref[h] + _dot_tn(qrb[h], t[h]) for h in _HEADS]
        for h in _HEADS:
            lo = h * RET_DK
            dstate[h] = new_ds[h]
            dp_ref[:, lo:lo + RET_DK] = _bf(_rope_t(dqr[h], cos, sin))
            dp_ref[:, RET_W + lo:RET_W + lo + RET_DK] = _bf(_rope_t(dkr[h], cos, sin) * (RET_DK ** -0.5))
            dp_ref[:, 2 * RET_W + lo:2 * RET_W + lo + RET_DK] = _bf(dv[h])
            dp_ref[:, 3 * RET_W + lo:3 * RET_W + lo + RET_DK] = _bf(dg[h])

    sp = _ret_specs(nc, True)
    return pl.pallas_call(
        body,
        out_shape=jax.ShapeDtypeStruct((s, 4 * RET_W), BF16),
        grid=(nc,),
        in_specs=[sp["proj"], sp["half"], sp["half"], sp["intra"], sp["dec"], sp["dec"], sp["cd"],
                  sp["state"], sp["wide"]],
        out_specs=sp["proj"],
        scratch_shapes=[pltpu.VMEM((RET_HEADS, RET_DK, RET_DK), F32)],
        compiler_params=_params("arbitrary"), name=name)(p_r, cos, sin, intra_t, qd_t, kd_t, cd_t, states, dy_r)


def _conv_taps(ext, w):
    acc = w[SSD_CONV - 1:SSD_CONV] * ext
    for j in range(SSD_CONV - 1):
        acc = acc + w[j:j + 1] * pltpu.roll(ext, SSD_CONV - 1 - j, axis=0)
    return acc


def _conv_fwd(xbc_raw, conv_w, conv_b, name):
    s = xbc_raw.shape[0]
    t8 = CONV_TILE // 8

    def body(cur_ref, prev_ref, w_ref, b_ref, o_ref, slope_ref):
        keep = (pl.program_id(0) > 0).astype(F32)
        ext = jnp.concatenate([prev_ref[...] * keep, cur_ref[...]], axis=0)
        u = _conv_taps(ext, w_ref[...])[8:] + b_ref[...]
        sg = _sigmoid(u)
        o_ref[...] = u * sg
        slope_ref[...] = sg * (1.0 + u * (1.0 - sg))

    out = jax.ShapeDtypeStruct((s, SSD_CD), F32)
    return pl.pallas_call(
        body, out_shape=(out, out), grid=(s // CONV_TILE,),
        in_specs=[_row_spec(SSD_CD, CONV_TILE),
                  pl.BlockSpec((8, SSD_CD), lambda i: (jnp.maximum(i * t8 - 1, 0), 0)),
                  _full_spec((SSD_CONV, SSD_CD)), _full_spec((1, SSD_CD))],
        out_specs=(_row_spec(SSD_CD, CONV_TILE), _row_spec(SSD_CD, CONV_TILE)),
        compiler_params=_params("parallel"), name=name)(xbc_raw, xbc_raw, conv_w, conv_b)


def _conv_bwd(xbc_raw, conv_w, slope, dact, name):
    s = xbc_raw.shape[0]
    nt = s // CONV_TILE
    t8 = CONV_TILE // 8
    rows = CONV_TILE + 8

    def body(cur_ref, prev_ref, w_ref, s_ref, snext_ref, d_ref, dnext_ref, dx_ref, dw_ref, db_ref):
        i = pl.program_id(0)

        @pl.when(i == 0)
        def _():
            dw_ref[...] = jnp.zeros_like(dw_ref)
            db_ref[...] = jnp.zeros_like(db_ref)

        keep_prev = (i > 0).astype(F32)
        keep_next = (i < nt - 1).astype(F32)
        w = w_ref[...]
        cur = cur_ref[...]
        ext = jnp.concatenate([prev_ref[...] * keep_prev, cur], axis=0)
        duc = d_ref[...] * s_ref[...]
        du = jnp.concatenate([duc, dnext_ref[...] * snext_ref[...] * keep_next], axis=0)
        dx = w[SSD_CONV - 1:SSD_CONV] * du
        for j in range(SSD_CONV - 1):
            dx = dx + w[j:j + 1] * pltpu.roll(du, rows - (SSD_CONV - 1 - j), axis=0)
        dx_ref[...] = _bf(dx[:CONV_TILE])
        for j in range(SSD_CONV - 1):
            shifted = pltpu.roll(ext, SSD_CONV - 1 - j, axis=0)[8:]
            dw_ref[j:j + 1, :] += jnp.sum(duc * shifted, axis=0, keepdims=True)
        dw_ref[SSD_CONV - 1:SSD_CONV, :] += jnp.sum(duc * cur, axis=0, keepdims=True)
        db_ref[...] += jnp.sum(duc, axis=0, keepdims=True)

    row = _row_spec(SSD_CD, CONV_TILE)
    prev = pl.BlockSpec((8, SSD_CD), lambda i: (jnp.maximum(i * t8 - 1, 0), 0))
    nxt = pl.BlockSpec((8, SSD_CD), lambda i: (jnp.minimum((i + 1) * t8, s // 8 - 1), 0))
    return pl.pallas_call(
        body,
        out_shape=(jax.ShapeDtypeStruct((s, SSD_CD), BF16), jax.ShapeDtypeStruct((SSD_CONV, SSD_CD), F32),
                   jax.ShapeDtypeStruct((1, SSD_CD), F32)),
        grid=(nt,),
        in_specs=[row, prev, _full_spec((SSD_CONV, SSD_CD)), row, nxt, row, nxt],
        out_specs=(row, _full_spec((SSD_CONV, SSD_CD)), _full_spec((1, SSD_CD))),
        compiler_params=_params("arbitrary"), name=name)(xbc_raw, xbc_raw, conv_w, slope, slope, dact, dact)


def _head_select():
    e = np.zeros((HEAD_PAD, SSD_W), np.float32)
    for h in range(SSD_HEADS):
        e[h, h * SSD_P:(h + 1) * SSD_P] = 1.0
    return jnp.asarray(e, BF16), jnp.asarray(e.T, BF16)


def _pad_heads(v):
    return jnp.pad(v.reshape(1, SSD_HEADS).astype(F32), ((0, 0), (0, HEAD_PAD - SSD_HEADS)))


def _ssd_masks():
    r = _iota((CHUNK, SSD_QW), 0)
    c = _iota((CHUNK, SSD_QW), 1) % SSD_P
    itile = (r == c).astype(F32)
    ctile = r >= c
    rb = _iota((SSD_QW, SSD_QW), 0) // SSD_P
    cb = _iota((SSD_QW, SSD_QW), 1) // SSD_P
    return itile, ctile, rb == cb


_TILES = [slice(i * SSD_QW, (i + 1) * SSD_QW) for i in range(SSD_W // SSD_QW)]


def _ssd_heads(dtr_ref, dtb_ref, alog_ref):
    u = dtr_ref[...] + dtb_ref[...]
    dt = _softplus(u)
    nexp = -jnp.exp(alog_ref[...])
    return u, dt, nexp, dt * nexp


def _ssd_group(xbc_ref, acol_ref, dte_ref, gs):
    acol = acol_ref[:, gs]
    alast = acol[CHUNK - 1:CHUNK, :]
    xs, dte = xbc_ref[:, gs], dte_ref[:, gs]
    return dict(xs=xs, dte=dte, xdt=xs * dte, ea=jnp.exp(acol), tail=jnp.exp(alast - acol), eal=jnp.exp(alast))


def _silu_gate(z_ref, y_ref, tl):
    zv = z_ref[:, tl]
    sz = _sigmoid(zv)
    return zv, sz, y_ref[:, tl] * (zv * sz)


def _tile4(x):
    return jnp.concatenate([x, x, x, x], axis=0)


def _fold4(x):
    return x[0:CHUNK] + x[CHUNK:2 * CHUNK] + x[2 * CHUNK:3 * CHUNK] + x[3 * CHUNK:4 * CHUNK]


def _ssd_specs(nc, rev):
    cidx = (lambda c: nc - 1 - c) if rev else (lambda c: c)
    return dict(
        xbc=pl.BlockSpec((CHUNK, SSD_CD), lambda c: (cidx(c), 0)),
        dt=pl.BlockSpec((CHUNK, HEAD_PAD), lambda c: (cidx(c), 0)),
        wide=pl.BlockSpec((CHUNK, SSD_W), lambda c: (cidx(c), 0)),
        state=pl.BlockSpec((1, SSD_N, SSD_W), lambda c: (cidx(c), 0, 0)),
        head=_full_spec((1, HEAD_PAD)),
        roww=_full_spec((1, SSD_W)),
        e=_full_spec((HEAD_PAD, SSD_W)),
        et=_full_spec((SSD_W, HEAD_PAD)),
    )


_GROUPS = range(SSD_GROUPS)
_GROUP_LANES = [slice(g * SSD_GW, (g + 1) * SSD_GW) for g in _GROUPS]
_QUADS = [(g, slice(g * SSD_GW + q * SSD_QW, g * SSD_GW + (q + 1) * SSD_QW), slice(q * SSD_QW, (q + 1) * SSD_QW))
          for g in _GROUPS for q in range(SSD_GW // SSD_QW)]


def _ssd_bc(xbc_ref):
    bg = [_bf(xbc_ref[:, B_OFF + g * SSD_N:B_OFF + (g + 1) * SSD_N]) for g in _GROUPS]
    cg = [_bf(xbc_ref[:, C_OFF + g * SSD_N:C_OFF + (g + 1) * SSD_N]) for g in _GROUPS]
    return bg, cg


def _ssd_decay(aq, itile, ctile):
    arow = jnp.sum(aq * itile, axis=0, keepdims=True)
    return jnp.exp(jnp.where(ctile, aq - arow, -jnp.inf))


def _ssd_blockdiag(xq, bdmask):
    return jnp.where(bdmask, _tile4(_bf(xq)), jnp.zeros((), BF16))


def _ssd_fwd(xbc, dt_raw, z, dt_bias, a_log, dske, norm_w, name):
    s = xbc.shape[0]
    nc = s // CHUNK
    e_sel, _ = _head_select()

    def body(xbc_ref, dtr_ref, z_ref, dtb_ref, alog_ref, dske_ref, nw_ref, e_ref,
             yraw_ref, ys_ref, st_ref, acol_ref, dte_ref, state):
        @pl.when(pl.program_id(0) == 0)
        def _():
            state[...] = jnp.zeros_like(state)

        _, dt, _, a = _ssd_heads(dtr_ref, dtb_ref, alog_ref)
        tril = _bf((_iota((CHUNK, CHUNK), 0) >= _iota((CHUNK, CHUNK), 1)).astype(F32))
        ac3, dt3 = _split3(_dot_exact_r(tril, a)), _split3(dt)
        for tl in _TILES:
            e_t = e_ref[:, tl]
            acol_ref[:, tl] = _dot(ac3[0], e_t) + _dot(ac3[1], e_t) + _dot(ac3[2], e_t)
            dte_ref[:, tl] = _dot(dt3[0], e_t) + _dot(dt3[1], e_t) + _dot(dt3[2], e_t)

        itile, ctile, bdmask = _ssd_masks()
        st_ref[0] = state[...]
        bg, cg = _ssd_bc(xbc_ref)
        for g0 in range(0, SSD_GROUPS, SSD_BLOCK_GROUPS):
            gids = range(g0, g0 + SSD_BLOCK_GROUPS)
            quads = [(g, ql, qs) for g, ql, qs in _QUADS if g in gids]
            q = {g: _ssd_group(xbc_ref, acol_ref, dte_ref, _GROUP_LANES[g]) for g in gids}
            stg = {g: state[:, _GROUP_LANES[g]] for g in gids}
            cbt = {g: _dot_nt(cg[g], _tile4(bg[g])) for g in gids}
            ys = {g: _dot(cg[g], _bf(stg[g])) for g in gids}
            dq = [_ssd_decay(acol_ref[:, ql], itile, ctile) for _, ql, _ in quads]
            xbd = [_ssd_blockdiag(q[g]["xdt"][:, qs], bdmask) for g, _, qs in quads]
            yq = [_dot(_bf(cbt[g] * dq[i]), xbd[i]) + ys[g][:, qs] * q[g]["ea"][:, qs]
                  + dske_ref[:, ql] * q[g]["xs"][:, qs] for i, (g, ql, qs) in enumerate(quads)]
            new_st = {g: stg[g] * q[g]["eal"] + _dot_tn(bg[g], _bf(q[g]["xdt"] * q[g]["tail"])) for g in gids}
            for i, (_, ql, _) in enumerate(quads):
                yraw_ref[:, ql] = yq[i]
            for g in gids:
                state[:, _GROUP_LANES[g]] = new_st[g]

        sq = jnp.zeros((CHUNK, SSD_QW), F32)
        for tl in _TILES:
            t = _silu_gate(z_ref, yraw_ref, tl)[2]
            sq = sq + t * t
        rstd = lax.rsqrt(jnp.sum(sq, axis=1, keepdims=True) * (1.0 / SSD_W) + EPS)
        for tl in _TILES:
            ys_ref[:, tl] = _bf(_silu_gate(z_ref, yraw_ref, tl)[2] * rstd * nw_ref[:, tl])

    sp = _ssd_specs(nc, False)
    wide = jax.ShapeDtypeStruct((s, SSD_W), F32)
    return pl.pallas_call(
        body,
        out_shape=(wide, jax.ShapeDtypeStruct((s, SSD_W), BF16), jax.ShapeDtypeStruct((nc, SSD_N, SSD_W), F32),
                   wide, wide),
        grid=(nc,),
        in_specs=[sp["xbc"], sp["dt"], sp["wide"], sp["head"], sp["head"], sp["roww"], sp["roww"], sp["e"]],
        out_specs=(sp["wide"], sp["wide"], sp["state"], sp["wide"], sp["wide"]),
        scratch_shapes=[pltpu.VMEM((SSD_N, SSD_W), F32)],
        compiler_params=_params("arbitrary"), name=name,
    )(xbc, dt_raw, z, dt_bias, a_log, dske, norm_w, e_sel)


def _ssd_bwd(xbc, dt_raw, z, yraw, states, dys, acol, dte, dt_bias, a_log, dske, norm_w, name, rider):
    s = xbc.shape[0]
    nc = s // CHUNK
    _, et_sel = _head_select()
    n_in, n_out, n_scratch = 13, 7, 5
    n_rin, n_rout = len(rider.inputs), len(rider.out_shapes)

    def body(*refs):
        ins, refs = refs[:n_in], refs[n_in:]
        r_ins, refs = refs[:n_rin], refs[n_rin:]
        outs, refs = refs[:n_out], refs[n_out:]
        r_outs, refs = refs[:n_rout], refs[n_rout:]
        ride = functools.partial(rider.run, pl.program_id(0), nc, r_ins, r_outs, refs[n_scratch], refs[n_scratch + 1])
        ride(last=False)
        compute(*ins, *outs, *refs[:n_scratch])
        ride(last=True)

    def compute(xbc_ref, dtr_ref, z_ref, yraw_ref, st_ref, dys_ref, acol_ref, dte_ref, dtb_ref, alog_ref,
                dske_ref, nw_ref, et_ref, dxbc_ref, dz_ref, ddtr_ref, ddtb_ref, dalog_ref, ddsk_ref, dnw_ref,
                dstate, dsk_acc, dy_s, dacol_s, ddte_s):
        @pl.when(pl.program_id(0) == 0)
        def _():
            dstate[...] = jnp.zeros_like(dstate)
            dsk_acc[...] = jnp.zeros_like(dsk_acc)
            ddtb_ref[...] = jnp.zeros_like(ddtb_ref)
            dalog_ref[...] = jnp.zeros_like(dalog_ref)
            dnw_ref[...] = jnp.zeros_like(dnw_ref)

        itile, ctile, bdmask = _ssd_masks()
        last_row = (_iota((CHUNK, 1), 0) == CHUNK - 1).astype(F32)

        s1 = jnp.zeros((CHUNK, SSD_QW), F32)
        s2 = jnp.zeros((CHUNK, SSD_QW), F32)
        for tl in _TILES:
            t = _silu_gate(z_ref, yraw_ref, tl)[2]
            s1 = s1 + t * t
            s2 = s2 + dys_ref[:, tl] * nw_ref[:, tl] * t
        rstd = lax.rsqrt(jnp.sum(s1, axis=1, keepdims=True) * (1.0 / SSD_W) + EPS)
        back = rstd * rstd * rstd * jnp.sum(s2, axis=1, keepdims=True) * (1.0 / SSD_W)
        for tl in _TILES:
            zv, sz, t = _silu_gate(z_ref, yraw_ref, tl)
            dysv = dys_ref[:, tl]
            dt_ = rstd * (dysv * nw_ref[:, tl]) - t * back
            dnw_ref[:, tl] += jnp.sum(dysv * t * rstd, axis=0, keepdims=True)
            dz_ref[:, tl] = _bf(dt_ * yraw_ref[:, tl] * (sz * (1.0 + zv * (1.0 - sz))))
            dy_t = dt_ * (zv * sz)
            dy_s[:, tl] = dy_t
            dsk_acc[:, tl] += jnp.sum(dy_t * xbc_ref[:, tl], axis=0, keepdims=True)

        @pl.when(pl.program_id(0) == nc - 1)
        def _():
            ddsk_ref[...] = _dot_exact_l(jnp.broadcast_to(dsk_acc[...], (8, SSD_W)), et_ref[...])[0:1]

        gl = _GROUP_LANES
        bg, cg = _ssd_bc(xbc_ref)
        for g0 in range(0, SSD_GROUPS, SSD_BLOCK_GROUPS):
            gids = range(g0, g0 + SSD_BLOCK_GROUPS)
            quads = [(i, g, ql, qs) for i, (g, ql, qs) in enumerate(_QUADS) if g in gids]
            q = {g: _ssd_group(xbc_ref, acol_ref, dte_ref, gl[g]) for g in gids}
            stg = {g: st_ref[0, :, gl[g]] for g in gids}
            dsn = {g: dstate[:, gl[g]] for g in gids}
            stgb, dsnb = {g: _bf(stg[g]) for g in gids}, {g: _bf(dsn[g]) for g in gids}
            btile = {g: _tile4(bg[g]) for g in gids}
            cbt = {g: _dot_nt(cg[g], btile[g]) for g in gids}
            dyg = {g: dy_s[:, gl[g]] for g in gids}
            eag, tailg = {g: q[g]["ea"] for g in gids}, {g: q[g]["tail"] for g in gids}
            xdtg, ealg = {g: q[g]["xdt"] for g in gids}, {g: q[g]["eal"] for g in gids}
            dys_g = {g: _bf(dyg[g] * eag[g]) for g in gids}
            ysg = {g: _dot(cg[g], stgb[g]) for g in gids}
            dc0 = {g: _dot_nt(dys_g[g], stgb[g]) for g in gids}
            dst = {g: _dot_tn(cg[g], dys_g[g]) for g in gids}
            dwt = {g: _dot(bg[g], dsnb[g]) for g in gids}
            db0 = {g: _dot_nt(_bf(xdtg[g] * tailg[g]), dsnb[g]) for g in gids}
            dtl = {g: dwt[g] * xdtg[g] * tailg[g] for g in gids}
            dal_row = {g: jnp.sum(dtl[g], axis=0, keepdims=True)
                       + jnp.sum(dsn[g] * stg[g], axis=0, keepdims=True) * ealg[g] for g in gids}
            for g in gids:
                dstate[:, gl[g]] = dst[g] + dsn[g] * ealg[g]
            dq = {i: _ssd_decay(acol_ref[:, ql], itile, ctile) for i, _, ql, _ in quads}
            mq = {i: cbt[g] * dq[i] for i, g, _, _ in quads}
            xbd = {i: _ssd_blockdiag(xdtg[g][:, qs], bdmask) for i, g, _, qs in quads}
            dyq = {i: _bf(dyg[g][:, qs]) for i, g, _, qs in quads}
            dm = {i: _dot_nt(dyq[i], xbd[i]) for i, _, _, _ in quads}
            dxdt_q = {i: _fold4(jnp.where(bdmask, _dot_tn(_bf(mq[i]), dyq[i]), 0.0)) for i, _, _, _ in quads}
            eq = {i: dm[i] * mq[i] for i, _, _, _ in quads}
            dacol_q = {i: eq[i] - itile * jnp.sum(eq[i], axis=0, keepdims=True) for i, _, _, _ in quads}
            dcbt = {i: _bf(dm[i] * dq[i]) for i, _, _, _ in quads}
            dc_q = {i: _dot(dcbt[i], btile[g]) for i, g, _, _ in quads}
            db_q = {i: _fold4(_dot_tn(dcbt[i], cg[g])) for i, g, _, _ in quads}
            for g in gids:
                dxdt_g = jnp.concatenate([dxdt_q[2 * g], dxdt_q[2 * g + 1]], axis=1) + dwt[g] * tailg[g]
                dxbc_ref[:, gl[g]] = dyg[g] * dske_ref[:, gl[g]] + dxdt_g * q[g]["dte"]
                ddte_s[:, gl[g]] = dxdt_g * q[g]["xs"]
                dacol_s[:, gl[g]] = (jnp.concatenate([dacol_q[2 * g], dacol_q[2 * g + 1]], axis=1)
                                     + dyg[g] * (ysg[g] * eag[g]) - dtl[g] + last_row * dal_row[g])
                dxbc_ref[:, B_OFF + g * SSD_N:B_OFF + (g + 1) * SSD_N] = db0[g] + db_q[2 * g] + db_q[2 * g + 1]
                dxbc_ref[:, C_OFF + g * SSD_N:C_OFF + (g + 1) * SSD_N] = dc0[g] + dc_q[2 * g] + dc_q[2 * g + 1]

        dacum = jnp.zeros((CHUNK, HEAD_PAD), F32)
        ddt = jnp.zeros((CHUNK, HEAD_PAD), F32)
        for i, tl in enumerate(_TILES):
            et_t = et_ref[i * SSD_QW:(i + 1) * SSD_QW, :]
            dacum = dacum + _dot_exact_l(dacol_s[:, tl], et_t, pieces=2)
            ddt = ddt + _dot_exact_l(ddte_s[:, tl], et_t, pieces=2)
        u, _, nexp, a = _ssd_heads(dtr_ref, dtb_ref, alog_ref)
        triu = _bf((_iota((CHUNK, CHUNK), 1) >= _iota((CHUNK, CHUNK), 0)).astype(F32))
        da = _dot_exact_r(triu, dacum)
        ddt = ddt + da * nexp
        dalog_ref[...] += jnp.sum(da * a, axis=0, keepdims=True)
        du = ddt * _sigmoid(u)
        ddtr_ref[...] = _bf(du)
        ddtb_ref[...] += jnp.sum(du, axis=0, keepdims=True)

    sp = _ssd_specs(nc, True)
    return pl.pallas_call(
        body,
        out_shape=(jax.ShapeDtypeStruct((s, SSD_CD), F32), jax.ShapeDtypeStruct((s, SSD_W), BF16),
                   jax.ShapeDtypeStruct((s, HEAD_PAD), BF16), jax.ShapeDtypeStruct((1, HEAD_PAD), F32),
                   jax.ShapeDtypeStruct((1, HEAD_PAD), F32), jax.ShapeDtypeStruct((1, HEAD_PAD), F32),
                   jax.ShapeDtypeStruct((1, SSD_W), F32), *rider.out_shapes),
        grid=(nc,),
        in_specs=[sp["xbc"], sp["dt"], sp["wide"], sp["wide"], sp["state"], sp["wide"], sp["wide"], sp["wide"],
                  sp["head"], sp["head"], sp["roww"], sp["roww"], sp["et"], *rider.in_specs()],
        out_specs=(sp["xbc"], sp["wide"], sp["dt"], sp["head"], sp["head"], sp["head"], sp["roww"],
                   *rider.out_specs()),
        scratch_shapes=[pltpu.VMEM((SSD_N, SSD_W), F32), pltpu.VMEM((1, SSD_W), F32),
                        *[pltpu.VMEM((CHUNK, SSD_W), F32)] * 3, *rider.scratch()],
        compiler_params=_params("arbitrary"), name=name,
    )(xbc, dt_raw, z, yraw, states, dys, acol, dte, dt_bias, a_log, dske, norm_w, et_sel, *rider.inputs)


W_IN_SHARD = IN_PROJ // N_DEV
W_IN_ROW_PARTS = ((0, 1152), (1152, 832), (1984, 840))


def _device_step(x, pos_col, target, norm1_w, conv_w_shard, conv_b, dt_bias, a_log, d_skip, ssd_norm_w, norm_f_w,
                 w_in_shard, br_ret_shard, br_ssd_shard, out_shard):
    inv_freq = jnp.asarray(ROPE_THETA ** (-np.arange(ROPE_HALF, dtype=np.float64) / ROPE_HALF), F32).reshape(1, ROPE_HALF)
    dtb, alog = _pad_heads(dt_bias), _pad_heads(a_log)
    dske = jnp.repeat(d_skip.reshape(SSD_HEADS).astype(F32), SSD_P).reshape(1, SSD_W)
    my_core = lax.axis_index("c")

    h, w_all = _rmsnorm_fwd(x, norm1_w, "rmsnorm1_fwd", _gather_rider([w_in_shard], 1.0, 1.0))
    w_all = w_all.reshape(IN_PROJ, D_MODEL)
    w_dt = jnp.pad(w_all[OFF_DT:OFF_G], ((0, HEAD_PAD - SSD_HEADS), (0, 0)))
    w_g = w_all[OFF_G:]
    rows_r, rows_z, rows_xbc = (0, OFF_Z), (OFF_Z, SSD_W), (OFF_XBC, SSD_CD)
    gather = _gather_rider([br_ret_shard, br_ssd_shard, out_shard, conv_w_shard], relay_at=0.35, pass_at=0.6)
    p_r, all_ret, all_ssd, all_out, all_conv = _matmul(h, w_all, "nt", F32, "proj_ret", rider=gather, b_rows=rows_r)
    w_br_ret = all_ret.reshape(RET_W, D_MODEL)
    w_br_ssd = all_ssd.reshape(SSD_W, D_MODEL)
    w_out = all_out.reshape(D_MODEL, D_MODEL)
    conv_w = all_conv.transpose(1, 0, 2).reshape(SSD_CONV, SSD_CD)
    p_z = _matmul(h, w_all, "nt", F32, "proj_z", b_rows=rows_z)
    p_xbc = _matmul(h, w_all, "nt", F32, "proj_xbc", b_rows=rows_xbc)
    p_dt = _matmul(h, w_dt, "nt", F32, "proj_dt")
    p_g = _matmul(h, w_g, "nt", BF16, "proj_gates")
    cos, sin = _rope_tables(pos_col, inv_freq, "rope_tables")
    y_r, ret_states = _retention_fwd(p_r, cos, sin, "retention_fwd")
    xbc_act, silu_slope = _conv_fwd(p_xbc, conv_w, conv_b, "conv_fwd")
    y_raw, y_s, ssd_states, acol, dte = _ssd_fwd(xbc_act, p_dt, p_z, dtb, alog, dske, ssd_norm_w, "ssd_fwd")
    pr = _matmul(y_r, w_br_ret, "nn", BF16, "branch_ret")
    ps = _matmul(y_s, w_br_ssd, "nn", BF16, "branch_ssd")
    merged = _merge_fwd(p_g, pr, ps, "merge_fwd")
    o = _matmul(merged, w_out, "nn", F32, "out_proj")
    dx2, g_norm_f, loss_acc = _final_fwd_bwd(x, o, norm_f_w, target, "final_norm_loss")

    g_w_out = _matmul(merged, dx2, "tn", BF16, "grad_w_out")
    dmerged = _matmul(dx2, w_out, "nt", F32, "d_merged")
    dpr, dps, dp_g = _merge_bwd(p_g, pr, ps, dmerged, "merge_bwd")
    g_w_br_ret = _matmul(y_r, dpr, "tn", BF16, "grad_w_br_ret")
    g_w_br_ssd = _matmul(y_s, dps, "tn", BF16, "grad_w_br_ssd")
    dy_r = _matmul(dpr, w_br_ret, "nt", F32, "d_y_ret")
    dy_s = _matmul(dps, w_br_ssd, "nt", F32, "d_y_ssd")
    scatter = _scatter_rider([g_w_out.reshape(N_DEV, -1, D_MODEL), g_w_br_ret.reshape(N_DEV, -1, D_MODEL),
                              g_w_br_ssd.reshape(N_DEV, -1, D_MODEL)])
    dxbc_act, dp_z, dp_dt, g_dtb, g_alog, g_dsk, g_ssd_norm, got_out, got_ret, got_ssd = _ssd_bwd(
        xbc_act, p_dt, p_z, y_raw, ssd_states, dy_s, acol, dte, dtb, alog, dske, ssd_norm_w, "ssd_bwd", scatter)
    dp_xbc, g_conv_w, g_conv_b = _conv_bwd(p_xbc, conv_w, silu_slope, dxbc_act, "conv_bwd")
    dp_r = _retention_bwd(p_r, cos, sin, ret_states, dy_r, "retention_bwd")
    g_w_in = jnp.concatenate([
        _matmul(dp_r, h, "tn", BF16, "grad_w_ret"),
        _matmul(dp_z, h, "tn", BF16, "grad_w_z"),
        _matmul(dp_xbc, h, "tn", BF16, "grad_w_xbc"),
        _matmul(dp_dt, h, "tn", BF16, "grad_w_dt")[:SSD_HEADS],
        _matmul(dp_g, h, "tn", BF16, "grad_w_gates"),
    ], axis=0)
    blocks = g_w_in.reshape(N_DEV, W_IN_SHARD, D_MODEL)
    dh, from_sibling = _matmul(dp_r, w_all, "nn", F32, "d_h_ret", rider=_sibling_rider(blocks), b_rows=rows_r)
    chip_sum = _chip_sum(blocks, from_sibling, my_core, "w_in_chip_sum")
    carriers = (("d_h_xbc", dp_xbc, w_all, rows_xbc), ("d_h_gates", dp_g, w_g, None), ("d_h_z", dp_z, w_all, rows_z))
    landed = []
    for (row0, rows), (nm, dp, w, b_rows) in zip(W_IN_ROW_PARTS, carriers):
        dh, got = _matmul(dp, w, "nn", F32, nm, add=dh, rider=_chip_rider(chip_sum, row0, rows), b_rows=b_rows)
        landed.append(got)
    dh = _matmul(dp_dt, w_dt, "nn", F32, "d_h_dt", add=dh)
    grad_x, g_norm1 = _rmsnorm_bwd(x, norm1_w, dh, dx2, "rmsnorm1_bwd")
    small = dict(norm1_w=g_norm1, conv_w=g_conv_w, conv_b=g_conv_b, dt_bias=g_dtb[:, :SSD_HEADS],
                 a_log=g_alog[:, :SSD_HEADS], d_skip=g_dsk[:, :SSD_HEADS], ssd_norm_w=g_ssd_norm,
                 norm_f_w=g_norm_f)
    big = dict(w_in=_sum_row_parts(landed, "w_in_sum"), w_br_ret=_sum_slots(got_ret, "w_br_ret_sum"),
               w_br_ssd=_sum_slots(got_ssd, "w_br_ssd_sum"), w_out=_sum_slots(got_out, "w_out_sum"))
    return loss_acc[0, 0], grad_x, small, big


def _all_reduce_small(vec, name):
    r, c = vec.shape

    def body(x_ref, out_ref, land, send_sems, recv_sems):
        x, y, cc = _mesh_pos()
        my_idx = 4 * x + 2 * y + cc
        land[my_idx] = x_ref[...]
        copies = []
        for k in range(1, N_DEV):
            px, py, pc = x ^ (k >> 2), y ^ ((k >> 1) & 1), cc ^ (k & 1)
            cp = pltpu.make_async_remote_copy(
                src_ref=x_ref, dst_ref=land.at[my_idx],
                send_sem=send_sems.at[k - 1], recv_sem=recv_sems.at[k - 1],
                device_id=(px, py, pc), device_id_type=MESH)
            cp.start()
            copies.append(cp)
        for k in range(1, N_DEV):
            px, py, pc = x ^ (k >> 2), y ^ ((k >> 1) & 1), cc ^ (k & 1)
            pltpu.make_async_remote_copy(
                src_ref=x_ref, dst_ref=land.at[4 * px + 2 * py + pc],
                send_sem=send_sems.at[k - 1], recv_sem=recv_sems.at[k - 1],
                device_id=(px, py, pc), device_id_type=MESH).wait_recv()
        for cp in copies:
            cp.wait_send()
        acc = land[0]
        for i in range(1, N_DEV):
            acc = acc + land[i]
        out_ref[...] = acc

    return pl.pallas_call(
        body,
        out_shape=jax.ShapeDtypeStruct((r, c), F32),
        in_specs=[pl.BlockSpec(memory_space=pltpu.VMEM)],
        out_specs=pl.BlockSpec(memory_space=pltpu.VMEM),
        scratch_shapes=[pltpu.VMEM((N_DEV, r, c), F32), pltpu.SemaphoreType.DMA((7,)),
                        pltpu.SemaphoreType.DMA((7,))],
        name=name)(vec)


_SMALL = ("norm1_w", "conv_w", "conv_b", "dt_bias", "a_log", "d_skip", "ssd_norm_w", "norm_f_w")
_SMALL_COLS = 128
_W_IN = "w_in"
_WEIGHTS = ("norm1_w", "w_in", "conv_w", "conv_b", "dt_bias", "a_log", "d_skip", "ssd_norm_w",
            "w_br_ret", "w_br_ssd", "w_out", "norm_f_w")


def _pack(parts):
    flat = jnp.concatenate([p.reshape(-1).astype(F32) for p in parts])
    rows = -(-flat.shape[0] // (8 * _SMALL_COLS)) * 8
    return jnp.pad(flat, (0, rows * _SMALL_COLS - flat.shape[0])).reshape(rows, _SMALL_COLS)


def _unpack(packed, shapes):
    flat = packed.reshape(-1)
    out, off = [], 0
    for shp in shapes:
        n = int(np.prod(shp))
        out.append(flat[off:off + n].reshape(shp))
        off += n
    return out


def kernel(x, positions, norm1_w, w_in, conv_w, conv_b, dt_bias, a_log, d_skip, ssd_norm_w, w_br_ret, w_br_ssd, w_out, norm_f_w, loss_target, m_norm1_w, m_w_in, m_conv_w, m_conv_b, m_dt_bias, m_a_log, m_d_skip, m_ssd_norm_w, m_w_br_ret, m_w_br_ssd, m_w_out, m_norm_f_w, v_norm1_w, v_w_in, v_conv_w, v_conv_b, v_dt_bias, v_a_log, v_d_skip, v_ssd_norm_w, v_w_br_ret, v_w_br_ssd, v_w_out, v_norm_f_w):
    w = dict(norm1_w=norm1_w, w_in=w_in, conv_w=conv_w, conv_b=conv_b, dt_bias=dt_bias, a_log=a_log,
             d_skip=d_skip, ssd_norm_w=ssd_norm_w, w_br_ret=w_br_ret, w_br_ssd=w_br_ssd, w_out=w_out,
             norm_f_w=norm_f_w)
    m = dict(norm1_w=m_norm1_w, w_in=m_w_in, conv_w=m_conv_w, conv_b=m_conv_b, dt_bias=m_dt_bias,
             a_log=m_a_log, d_skip=m_d_skip, ssd_norm_w=m_ssd_norm_w, w_br_ret=m_w_br_ret,
             w_br_ssd=m_w_br_ssd, w_out=m_w_out, norm_f_w=m_norm_f_w)
    v = dict(norm1_w=v_norm1_w, w_in=v_w_in, conv_w=v_conv_w, conv_b=v_conv_b, dt_bias=v_dt_bias,
             a_log=v_a_log, d_skip=v_d_skip, ssd_norm_w=v_ssd_norm_w, w_br_ret=v_w_br_ret,
             w_br_ssd=v_w_br_ssd, w_out=v_w_out, norm_f_w=v_norm_f_w)
    s = x.shape[1]
    my_idx = 4 * lax.axis_index("x") + 2 * lax.axis_index("y") + lax.axis_index("c")

    w[_W_IN], m[_W_IN], v[_W_IN] = w_in[0].T, m_w_in[0].T, v_w_in[0].T

    loss_part, grad_x, g_small, g_big = _device_step(
        x[0], positions.reshape(s, 1), loss_target[0], norm1_w, conv_w[0], conv_b, dt_bias, a_log, d_skip,
        ssd_norm_w, norm_f_w.reshape(1, D_MODEL), _cast_bf16(w[_W_IN], "cast_w_in"),
        _cast_bf16(w_br_ret[0], "cast_w_br_ret"), _cast_bf16(w_br_ssd[0], "cast_w_br_ssd"),
        _cast_bf16(w_out[0], "cast_w_out"))

    loss = lax.psum(loss_part, ("x", "y", "c"))

    small_shapes = [g_small[n].shape for n in _SMALL]
    summed = _unpack(_all_reduce_small(_pack([g_small[n] for n in _SMALL]), "allreduce_small"), small_shapes)
    grads = dict(zip(_SMALL, summed))
    conv_cols = SSD_CD // N_DEV
    grads["conv_w"] = lax.dynamic_slice_in_dim(grads["conv_w"], my_idx * conv_cols, conv_cols, axis=1)
    grads["norm_f_w"] = grads["norm_f_w"].reshape(D_MODEL)
    for n in ("norm1_w", "conv_w", "conv_b", "dt_bias", "a_log", "d_skip", "ssd_norm_w"):
        grads[n] = grads[n].reshape(w[n].shape)

    delta, new_m, new_v = {}, {}, {}
    for n in ("w_br_ret", "w_br_ssd", "w_out"):
        w[n], m[n], v[n] = w[n][0], m[n][0], v[n][0]
    for n in (_W_IN, "w_br_ret", "w_br_ssd", "w_out"):
        back = (lambda a: a.T[None]) if n == _W_IN else (lambda a: a[None])
        res = _adamw(w[n], g_big[n], m[n], v[n], "adamw_" + n)
        grads[n] = back(g_big[n])
        delta[n], new_m[n], new_v[n] = (back(a) for a in res)
    shapes = [w[n].shape for n in _SMALL]
    packed = _adamw(_pack([w[n] for n in _SMALL]), _pack([grads[n] for n in _SMALL]),
                    _pack([m[n] for n in _SMALL]), _pack([v[n] for n in _SMALL]), "adamw_small")
    for res, dst in zip(packed, (delta, new_m, new_v)):
        for n, a in zip(_SMALL, _unpack(res, shapes)):
            dst[n] = a

    return (loss, grad_x.reshape(x.shape), *[grads[n] for n in _WEIGHTS], *[delta[n] for n in _WEIGHTS],
            *[new_m[n] for n in _WEIGHTS], *[new_v[n] for n in _WEIGHTS])
```

```python
import functools

import numpy as np
import jax
import jax.numpy as jnp
from jax import lax
from jax.experimental import pallas as pl
from jax.experimental.pallas import tpu as pltpu

F32 = jnp.float32
BF16 = jnp.bfloat16

D_MODEL = 2048
CHUNK = 64
CHUNKS_PER_STEP = 2
STEP_ROWS = CHUNK * CHUNKS_PER_STEP
EPS = 1e-6
N_DEV = 8

RET_HEADS = 8
RET_DK = 256
RET_W = RET_HEADS * RET_DK
ROPE_THETA = 10000.0
ROPE_HALF = RET_DK // 2

SSD_W = 4096
SSD_P = 64
SSD_HEADS = 64
SSD_GROUPS = 8
SSD_N = 128
SSD_GW = SSD_W // SSD_GROUPS
SSD_QW = 256
SSD_BLOCK_GROUPS = 2
SSD_CONV = 4
SSD_CD = SSD_W + 2 * SSD_GROUPS * SSD_N
HEAD_PAD = 128
B_OFF = SSD_W
C_OFF = SSD_W + SSD_GROUPS * SSD_N

ADAM_LR = 0.001
ADAM_B1 = 0.9
ADAM_B2 = 0.999
ADAM_EPS = 1e-08
ADAM_WD = 0.01
ADAM_STEP = 10

SPLITS = (RET_W, RET_W, RET_W, RET_W, SSD_W, SSD_CD, SSD_HEADS, D_MODEL, D_MODEL)
IN_PROJ = sum(SPLITS)
OFF_Z = 4 * RET_W
OFF_XBC = OFF_Z + SSD_W
OFF_DT = OFF_XBC + SSD_CD
OFF_G = OFF_DT + SSD_HEADS

ROW_TILE = 256
CONV_TILE = 128
MM_TILE = 1024
MM_TILE_WIDE = 2048
MM_TK = 2048
VMEM_CAP = 60 << 20

MESH = pl.DeviceIdType.MESH


def _dot(a, b):
    return lax.dot_general(a, b, (((1,), (0,)), ((), ())), preferred_element_type=F32)


def _dot_nt(a, b):
    return lax.dot_general(a, b, (((1,), (1,)), ((), ())), preferred_element_type=F32)


def _dot_tn(a, b):
    return lax.dot_general(a, b, (((0,), (0,)), ((), ())), preferred_element_type=F32)


def _bf(x):
    return x.astype(BF16)


def _split3(x):
    hi = x.astype(BF16)
    r = x - hi.astype(F32)
    mid = r.astype(BF16)
    lo = (r - mid.astype(F32)).astype(BF16)
    return hi, mid, lo


def _dot_exact_l(x, sel, pieces=3):
    hi, mid, lo = _split3(x)
    r = _dot(hi, sel) + _dot(mid, sel)
    return r + _dot(lo, sel) if pieces == 3 else r


def _dot_exact_r(sel, x):
    hi, mid, lo = _split3(x)
    return _dot(sel, hi) + _dot(sel, mid) + _dot(sel, lo)


def _sigmoid(x):
    return 1.0 / (1.0 + jnp.exp(-x))


def _softplus(x):
    return jnp.maximum(x, 0.0) + jnp.log(1.0 + jnp.exp(-jnp.abs(x)))


def _iota(shape, axis):
    return lax.broadcasted_iota(jnp.int32, shape, axis)


def _params(*sem):
    return pltpu.CompilerParams(dimension_semantics=sem)


class _Rider:
    def __init__(self, inputs, out_shapes, n_sems, phases):
        self.inputs, self.out_shapes, self.n_sems, self.phases = tuple(inputs), tuple(out_shapes), n_sems, phases

    def in_specs(self):
        return [pl.BlockSpec(memory_space=pl.ANY)] * len(self.inputs)

    def out_specs(self):
        return [pl.BlockSpec(memory_space=pl.ANY)] * len(self.out_shapes)

    def scratch(self):
        return [pltpu.SemaphoreType.DMA((self.n_sems,)), pltpu.SemaphoreType.DMA((self.n_sems,))]

    def run(self, step, n_steps, ins, outs, send_sems, recv_sems, last):
        for frac, fn in self.phases:
            if (frac >= 1.0) != last:
                continue
            at = min(int(frac * n_steps), n_steps - 1)

            @pl.when(step == at)
            def _(fn=fn):
                fn(ins, outs, send_sems, recv_sems)


def _mesh_pos():
    return lax.axis_index("x"), lax.axis_index("y"), lax.axis_index("c")


def _gather_rider(shards, relay_at, pass_at):
    n = len(shards)

    def tools(a, ins, outs, send_sems, recv_sems):
        x, y, cc = _mesh_pos()
        nbrs = [(1 - x, y), (x, 1 - y)]
        diag = (1 - x, 1 - y)
        relay_from, relay_to = (x ^ cc, y ^ (1 - cc)), (x ^ (1 - cc), y ^ cc)

        def slot(px, py, pc):
            return outs[a].at[4 * px + 2 * py + pc]

        def copy(k, block, to, src=None):
            return pltpu.make_async_remote_copy(
                src_ref=slot(*block) if src is None else src, dst_ref=slot(*block),
                send_sem=send_sems.at[8 * a + k], recv_sem=recv_sems.at[8 * a + k], device_id=to, device_id_type=MESH)

        me, sibling = (x, y, cc), (x, y, 1 - cc)
        return dict(
            mine=lambda: pltpu.make_async_copy(ins[a], slot(*me), send_sems.at[8 * a + 7]),
            first=lambda: [copy(0, me, sibling, src=ins[a])] + [copy(1 + j, me, (*chip, cc), src=ins[a])
                                                                for j, chip in enumerate(nbrs)],
            landed=lambda: [copy(1 + j, (*chip, cc), me) for j, chip in enumerate(nbrs)],
            relay=lambda: copy(3, (*relay_from, cc), (*relay_to, cc)),
            relayed_in=lambda: copy(3, (*diag, cc), me),
            passed=lambda: [copy(4 + j, (*chip, cc), sibling) for j, chip in enumerate(nbrs)],
            passed_diag=lambda: copy(6, (*diag, cc), sibling),
            from_sibling=lambda: [copy(0, sibling, me)] + [copy(4 + j, (*chip, 1 - cc), me)
                                                           for j, chip in enumerate(nbrs + [diag])])

    def start(*refs):
        for a in range(n):
            t = tools(a, *refs)
            t["mine"]().start()
            for cp in t["first"]():
                cp.start()

    def relay(*refs):
        for a in range(n):
            t = tools(a, *refs)
            for got in t["landed"]():
                got.wait_recv()
            t["relay"]().start()
            for cp in t["passed"]():
                cp.start()

    def pass_diag(*refs):
        for a in range(n):
            t = tools(a, *refs)
            t["relayed_in"]().wait_recv()
            t["passed_diag"]().start()

    def finish(*refs):
        for a in range(n):
            t = tools(a, *refs)
            for cp in t["from_sibling"]():
                cp.wait_recv()
            for cp in t["first"]() + [t["relay"]()] + t["passed"]() + [t["passed_diag"]()]:
                cp.wait_send()
            t["mine"]().wait()

    outs = [jax.ShapeDtypeStruct((N_DEV,) + s.shape, s.dtype) for s in shards]
    return _Rider(shards, outs, 8 * n, ((0.0, start), (relay_at, relay), (pass_at, pass_diag), (1.0, finish)))


def _scatter_rider(blocks):
    n = len(blocks)

    def copies(a, ins, outs, send_sems, recv_sems, with_back):
        x, y, cc = _mesh_pos()
        my_idx = 4 * x + 2 * y + cc
        mine = pltpu.make_async_copy(ins[a].at[my_idx], outs[a].at[my_idx], send_sems.at[8 * a + 7])
        out, back = [], []
        for k in range(1, N_DEV):
            px, py, pc = x ^ (k >> 2), y ^ ((k >> 1) & 1), cc ^ (k & 1)
            peer_idx = 4 * px + 2 * py + pc
            sems = dict(send_sem=send_sems.at[8 * a + k - 1], recv_sem=recv_sems.at[8 * a + k - 1],
                        device_id=(px, py, pc), device_id_type=MESH)
            out.append(pltpu.make_async_remote_copy(src_ref=ins[a].at[peer_idx], dst_ref=outs[a].at[my_idx], **sems))
            if with_back:
                back.append(pltpu.make_async_remote_copy(src_ref=ins[a].at[my_idx], dst_ref=outs[a].at[peer_idx], **sems))
        return mine, out, back

    def start(*refs):
        for a in range(n):
            mine, out, _ = copies(a, *refs, False)
            mine.start()
            for cp in out:
                cp.start()

    def finish(*refs):
        for a in range(n):
            mine, out, back = copies(a, *refs, True)
            for cp in back:
                cp.wait_recv()
            for cp in out:
                cp.wait_send()
            mine.wait()

    outs = [jax.ShapeDtypeStruct(b.shape, b.dtype) for b in blocks]
    return _Rider(blocks, outs, 8 * n, ((0.0, start), (1.0, finish)))


def _sibling_rider(blocks):
    _, r, c = blocks.shape

    def copies(ins, outs, send_sems, recv_sems):
        x, y, cc = _mesh_pos()
        return [pltpu.make_async_remote_copy(
            src_ref=ins[0].at[2 * j + 1 - cc], dst_ref=outs[0].at[j], send_sem=send_sems.at[j],
            recv_sem=recv_sems.at[j], device_id=(x, y, 1 - cc), device_id_type=MESH) for j in range(4)]

    def start(*refs):
        for cp in copies(*refs):
            cp.start()

    def finish(*refs):
        for cp in copies(*refs):
            cp.wait_recv()
        for cp in copies(*refs):
            cp.wait_send()

    return _Rider([blocks], [jax.ShapeDtypeStruct((4, r, c), blocks.dtype)], 4, ((0.0, start), (1.0, finish)))


def _chip_rider(partial, row0, rows):
    _, _, c = partial.shape

    def copies(ins, outs, send_sems, recv_sems, with_back):
        x, y, cc = _mesh_pos()
        my_chip = 2 * x + y

        def rows_of(j):
            return ins[0].at[j, pl.ds(row0, rows)]

        mine = pltpu.make_async_copy(rows_of(my_chip), outs[0].at[my_chip], send_sems.at[3])
        out, back = [], []
        for k in range(1, 4):
            px, py = x ^ (k >> 1), y ^ (k & 1)
            peer = 2 * px + py
            sems = dict(send_sem=send_sems.at[k - 1], recv_sem=recv_sems.at[k - 1], device_id=(px, py, cc),
                        device_id_type=MESH)
            out.append(pltpu.make_async_remote_copy(src_ref=rows_of(peer), dst_ref=outs[0].at[my_chip], **sems))
            if with_back:
                back.append(pltpu.make_async_remote_copy(src_ref=rows_of(my_chip), dst_ref=outs[0].at[peer], **sems))
        return mine, out, back

    def start(*refs):
        mine, out, _ = copies(*refs, False)
        mine.start()
        for cp in out:
            cp.start()

    def finish(*refs):
        mine, out, back = copies(*refs, True)
        for cp in back:
            cp.wait_recv()
        for cp in out:
            cp.wait_send()
        mine.wait()

    return _Rider([partial], [jax.ShapeDtypeStruct((4, rows, c), partial.dtype)], 4, ((0.0, start), (1.0, finish)))


def _matmul(a, b, mode, out_dtype, name, add=None, rider=None, b_rows=None):
    b_shape = b.shape if b_rows is None else (b_rows[1], b.shape[1])
    if mode == "nn":
        (m, k), (k2, n) = a.shape, b_shape
    elif mode == "nt":
        (m, k), (n, k2) = a.shape, b_shape
    else:
        (k, m), (k2, n) = a.shape, b_shape
    assert k == k2, (a.shape, b_shape, mode)
    tm, tn, tk = min(m, MM_TILE), min(n, MM_TILE), min(k, MM_TK)
    if n % MM_TILE_WIDE == 0 and (k <= MM_TK or (add is None and jnp.dtype(out_dtype).itemsize == 2)):
        tn = MM_TILE_WIDE
    assert m % tm == 0 and n % tn == 0 and k % tk == 0, (m, n, k)
    nk = k // tk
    b_tile_rows = tn if mode == "nt" else tk
    b_off = 0 if b_rows is None else b_rows[0] // b_tile_rows
    assert b_rows is None or b_rows[0] % b_tile_rows == 0, (b_rows, b_tile_rows)
    dot = {"nn": _dot, "nt": _dot_nt, "tn": _dot_tn}[mode]

    gm, gn = m // tm, n // tn
    n_rin = len(rider.inputs) if rider else 0
    n_rout = len(rider.out_shapes) if rider else 0
    n_add = 1 if add is not None else 0

    def body(*refs):
        a_ref, b_ref = refs[:2]
        add_ref = refs[2] if n_add else None
        r_ins = refs[2 + n_add:2 + n_add + n_rin]
        o_ref = refs[2 + n_add + n_rin]
        r_outs = refs[3 + n_add + n_rin:3 + n_add + n_rin + n_rout]
        scratch = refs[3 + n_add + n_rin + n_rout:]
        acc_ref = scratch[0] if nk > 1 else None
        kk = pl.program_id(2)
        if rider:
            step = (pl.program_id(0) * gn + pl.program_id(1)) * nk + kk
            ride = functools.partial(rider.run, step, gm * gn * nk, r_ins, r_outs, scratch[-2], scratch[-1])
            ride(last=False)

        def finish(r):
            if add_ref is not None:
                r = r + add_ref[...]
            o_ref[...] = r.astype(o_ref.dtype)

        if nk == 1:
            finish(dot(_bf(a_ref[...]), _bf(b_ref[...])))
        else:
            @pl.when(kk == 0)
            def _():
                acc_ref[...] = dot(_bf(a_ref[...]), _bf(b_ref[...]))

            @pl.when(jnp.logical_and(kk > 0, kk < nk - 1))
            def _():
                acc_ref[...] += dot(_bf(a_ref[...]), _bf(b_ref[...]))

            @pl.when(kk == nk - 1)
            def _():
                finish(acc_ref[...] + dot(_bf(a_ref[...]), _bf(b_ref[...])))
        if rider:
            ride(last=True)

    if mode == "nn":
        a_spec = pl.BlockSpec((tm, tk), lambda i, j, kk: (i, kk))
        b_spec = pl.BlockSpec((tk, tn), lambda i, j, kk: (kk + b_off, j))
    elif mode == "nt":
        a_spec = pl.BlockSpec((tm, tk), lambda i, j, kk: (i, kk))
        b_spec = pl.BlockSpec((tn, tk), lambda i, j, kk: (j + b_off, kk))
    else:
        a_spec = pl.BlockSpec((tk, tm), lambda i, j, kk: (kk, i))
        b_spec = pl.BlockSpec((tk, tn), lambda i, j, kk: (kk + b_off, j))
    o_spec = pl.BlockSpec((tm, tn), lambda i, j, kk: (i, j))
    in_specs = [a_spec, b_spec] + ([o_spec] if add is not None else [])
    args = (a, b) + ((add,) if add is not None else ())
    block_bytes = (tm * tk * a.dtype.itemsize + tk * tn * b.dtype.itemsize
                   + tm * tn * (jnp.dtype(out_dtype).itemsize + (4 if add is not None else 0)))
    vmem = min(2 * block_bytes + 2 * tm * tn * 4 + 2 * (tm + tn) * tk + (4 << 20), VMEM_CAP)
    out_shape = jax.ShapeDtypeStruct((m, n), out_dtype)
    scratch = [pltpu.VMEM((tm, tn), F32)] if nk > 1 else []
    if rider:
        in_specs = in_specs + rider.in_specs()
        args = args + rider.inputs
        out_shape = (out_shape, *rider.out_shapes)
        o_spec = (o_spec, *rider.out_specs())
        scratch = scratch + rider.scratch()
    sem = ("arbitrary",) * 3 if rider else ("parallel", "parallel", "arbitrary")
    return pl.pallas_call(
        body,
        out_shape=out_shape,
        grid=(gm, gn, nk),
        in_specs=in_specs,
        out_specs=o_spec,
        scratch_shapes=scratch,
        compiler_params=pltpu.CompilerParams(dimension_semantics=sem, vmem_limit_bytes=int(vmem)),
        name=name,
    )(*args)


def _row_spec(width, tile=ROW_TILE):
    return pl.BlockSpec((tile, width), lambda i: (i, 0))


def _full_spec(shape):
    nd = len(shape)
    return pl.BlockSpec(shape, lambda *_: (0,) * nd)


def _tiling_2d(r, c):
    if r <= ROW_TILE or r % ROW_TILE == 0:
        tr = min(r, ROW_TILE)
        return (tr, c), r // tr, (lambda i: (i, 0))
    tc = 128 if r > 4096 else 256
    assert c % tc == 0, (r, c)
    return (r, tc), c // tc, (lambda i: (0, i))


def _spec_2d(r, c):
    blk, grid, idx = _tiling_2d(r, c)
    return pl.BlockSpec(blk, idx), grid


def _cast_bf16(x, name):
    r, c = x.shape
    sp, grid = _spec_2d(r, c)

    def body(x_ref, o_ref):
        o_ref[...] = _bf(x_ref[...])

    return pl.pallas_call(
        body, out_shape=jax.ShapeDtypeStruct((r, c), BF16), grid=(grid,), in_specs=[sp], out_specs=sp,
        compiler_params=_params("parallel"), name=name)(x)


def _rmsnorm_fwd(x, w, name, rider):
    s, d = x.shape
    steps = s // ROW_TILE
    n_rin, n_rout = len(rider.inputs), len(rider.out_shapes)

    def body(x_ref, w_ref, *refs):
        r_ins, o_ref, r_outs = refs[:n_rin], refs[n_rin], refs[n_rin + 1:n_rin + 1 + n_rout]
        ride = functools.partial(rider.run, pl.program_id(0), steps, r_ins, r_outs, refs[-2], refs[-1])
        ride(last=False)
        xv = x_ref[...]
        rstd = lax.rsqrt(jnp.mean(xv * xv, axis=1, keepdims=True) + EPS)
        o_ref[...] = _bf(xv * rstd * w_ref[...])
        ride(last=True)

    return pl.pallas_call(
        body, out_shape=(jax.ShapeDtypeStruct((s, d), BF16), *rider.out_shapes), grid=(steps,),
        in_specs=[_row_spec(d), _full_spec((1, d)), *rider.in_specs()],
        out_specs=(_row_spec(d), *rider.out_specs()), scratch_shapes=rider.scratch(),
        compiler_params=_params("arbitrary"), name=name)(x, w, *rider.inputs)


def _rmsnorm_bwd(x, w, dh, dres, name):
    s, d = x.shape

    def body(x_ref, w_ref, dh_ref, dres_ref, dx_ref, dw_ref):
        @pl.when(pl.program_id(0) == 0)
        def _():
            dw_ref[...] = jnp.zeros_like(dw_ref)

        xv = x_ref[...]
        rstd = lax.rsqrt(jnp.mean(xv * xv, axis=1, keepdims=True) + EPS)
        xhat = xv * rstd
        dhv = dh_ref[...]
        dxhat = dhv * w_ref[...]
        dx = rstd * (dxhat - xhat * jnp.mean(dxhat * xhat, axis=1, keepdims=True))
        dx_ref[...] = dx + dres_ref[...]
        dw_ref[...] += jnp.sum(dhv * xhat, axis=0, keepdims=True)

    return pl.pallas_call(
        body,
        out_shape=(jax.ShapeDtypeStruct((s, d), F32), jax.ShapeDtypeStruct((1, d), F32)),
        grid=(s // ROW_TILE,),
        in_specs=[_row_spec(d), _full_spec((1, d)), _row_spec(d), _row_spec(d)],
        out_specs=(_row_spec(d), _full_spec((1, d))),
        compiler_params=_params("arbitrary"), name=name)(x, w, dh, dres)


def _rope_tables(pos_col, inv_freq, name):
    s = pos_col.shape[0]

    def body(p_ref, f_ref, cos_ref, sin_ref):
        ang = p_ref[...].astype(F32) * f_ref[...]
        cos_ref[...] = jnp.cos(ang)
        sin_ref[...] = jnp.sin(ang)

    out = jax.ShapeDtypeStruct((s, ROPE_HALF), F32)
    return pl.pallas_call(
        body, out_shape=(out, out), grid=(s // ROW_TILE,),
        in_specs=[_row_spec(1), _full_spec((1, ROPE_HALF))],
        out_specs=(_row_spec(ROPE_HALF), _row_spec(ROPE_HALF)),
        compiler_params=_params("parallel"), name=name)(pos_col, inv_freq)


def _merge_fwd(pg, p_r, p_s, name):
    s = pg.shape[0]

    def body(g_ref, r_ref, s_ref, o_ref):
        g = g_ref[...].astype(F32)
        o_ref[...] = _bf(_sigmoid(g[:, :D_MODEL]) * r_ref[...].astype(F32)
                         + _sigmoid(g[:, D_MODEL:]) * s_ref[...].astype(F32))

    return pl.pallas_call(
        body, out_shape=jax.ShapeDtypeStruct((s, D_MODEL), BF16), grid=(s // ROW_TILE,),
        in_specs=[_row_spec(2 * D_MODEL), _row_spec(D_MODEL), _row_spec(D_MODEL)],
        out_specs=_row_spec(D_MODEL), compiler_params=_params("parallel"), name=name)(pg, p_r, p_s)


def _merge_bwd(pg, p_r, p_s, dm, name):
    s = pg.shape[0]

    def body(g_ref, r_ref, s_ref, dm_ref, dr_ref, ds_ref, dg_ref):
        g = g_ref[...].astype(F32)
        sr, ss = _sigmoid(g[:, :D_MODEL]), _sigmoid(g[:, D_MODEL:])
        d = dm_ref[...]
        dr_ref[...] = _bf(d * sr)
        ds_ref[...] = _bf(d * ss)
        dg_ref[:, :D_MODEL] = _bf(d * r_ref[...].astype(F32) * sr * (1.0 - sr))
        dg_ref[:, D_MODEL:] = _bf(d * s_ref[...].astype(F32) * ss * (1.0 - ss))

    o = jax.ShapeDtypeStruct((s, D_MODEL), BF16)
    return pl.pallas_call(
        body, out_shape=(o, o, jax.ShapeDtypeStruct((s, 2 * D_MODEL), BF16)), grid=(s // ROW_TILE,),
        in_specs=[_row_spec(2 * D_MODEL), _row_spec(D_MODEL), _row_spec(D_MODEL), _row_spec(D_MODEL)],
        out_specs=(_row_spec(D_MODEL), _row_spec(D_MODEL), _row_spec(2 * D_MODEL)),
        compiler_params=_params("parallel"), name=name)(pg, p_r, p_s, dm)


def _final_fwd_bwd(x, o, w, target, name):
    s, d = x.shape

    def body(x_ref, o_ref, w_ref, t_ref, dx_ref, dw_ref, loss_ref):
        @pl.when(pl.program_id(0) == 0)
        def _():
            dw_ref[...] = jnp.zeros_like(dw_ref)
            loss_ref[...] = jnp.zeros_like(loss_ref)

        x2 = x_ref[...] + o_ref[...]
        rstd = lax.rsqrt(jnp.mean(x2 * x2, axis=1, keepdims=True) + EPS)
        xhat = x2 * rstd
        wv = w_ref[...]
        err = xhat * wv - t_ref[...]
        loss_ref[...] += jnp.sum(jnp.sum(err * err, axis=1, keepdims=True), axis=0, keepdims=True) * (0.5 / d)
        dy = err * (1.0 / d)
        dw_ref[...] += jnp.sum(dy * xhat, axis=0, keepdims=True)
        dxhat = dy * wv
        dx_ref[...] = rstd * (dxhat - xhat * jnp.mean(dxhat * xhat, axis=1, keepdims=True))

    return pl.pallas_call(
        body,
        out_shape=(jax.ShapeDtypeStruct((s, d), F32), jax.ShapeDtypeStruct((1, d), F32),
                   jax.ShapeDtypeStruct((8, 128), F32)),
        grid=(s // ROW_TILE,),
        in_specs=[_row_spec(d), _row_spec(d), _full_spec((1, d)), _row_spec(d)],
        out_specs=(_row_spec(d), _full_spec((1, d)), _full_spec((8, 128))),
        compiler_params=_params("arbitrary"), name=name)(x, o, w, target)


def _adamw(w, g, m, v, name):
    r, c = w.shape
    sp, grid = _spec_2d(r, c)
    c1 = 1.0 / (1.0 - ADAM_B1 ** ADAM_STEP)
    c2 = 1.0 / (1.0 - ADAM_B2 ** ADAM_STEP)

    def body(w_ref, g_ref, m_ref, v_ref, d_ref, nm_ref, nv_ref):
        gv = g_ref[...]
        nm = ADAM_B1 * m_ref[...] + (1.0 - ADAM_B1) * gv
        nv = ADAM_B2 * v_ref[...] + (1.0 - ADAM_B2) * (gv * gv)
        d_ref[...] = -ADAM_LR * ((nm * c1) / (jnp.sqrt(nv * c2) + ADAM_EPS) + ADAM_WD * w_ref[...])
        nm_ref[...] = nm
        nv_ref[...] = nv

    o = jax.ShapeDtypeStruct((r, c), F32)
    return pl.pallas_call(
        body, out_shape=(o, o, o), grid=(grid,), in_specs=[sp, sp, sp, sp], out_specs=(sp, sp, sp),
        compiler_params=_params("parallel"), name=name)(w, g, m, v)


def _sum_slots(land, name):
    n, r, c = land.shape
    blk, grid, idx = _tiling_2d(r, c)

    def body(l_ref, o_ref):
        acc = l_ref[0].astype(F32)
        for i in range(1, n):
            acc = acc + l_ref[i].astype(F32)
        o_ref[...] = acc

    return pl.pallas_call(
        body, out_shape=jax.ShapeDtypeStruct((r, c), F32), grid=(grid,),
        in_specs=[pl.BlockSpec((n, *blk), lambda i: (0, *idx(i)))], out_specs=pl.BlockSpec(blk, idx),
        compiler_params=_params("parallel"), name=name)(land)


SLAB_COLS = 256


def _chip_sum(blocks, from_sibling, my_core, name):
    _, r, c = blocks.shape

    def body(core_ref, a_ref, b_ref, o_ref):
        o_ref[...] = _bf(a_ref[...].astype(F32) + b_ref[...].astype(F32))

    def slab(chip_of):
        return pl.BlockSpec((1, r, SLAB_COLS), lambda j, i, core: (chip_of(j, core), 0, i))

    grid_spec = pltpu.PrefetchScalarGridSpec(
        num_scalar_prefetch=1, grid=(4, c // SLAB_COLS),
        in_specs=[slab(lambda j, core: 2 * j + core[0]), slab(lambda j, core: j)],
        out_specs=slab(lambda j, core: j))
    return pl.pallas_call(
        body, out_shape=jax.ShapeDtypeStruct((4, r, c), BF16), grid_spec=grid_spec,
        compiler_params=_params("parallel", "parallel"), name=name,
    )(my_core.astype(jnp.int32).reshape(1), blocks, from_sibling)


def _sum_row_parts(parts, name):
    c = parts[0].shape[2]
    rows = [p.shape[1] for p in parts]

    def body(*refs):
        o_ref, off = refs[-1], 0
        for p_ref, n in zip(refs[:-1], rows):
            acc = p_ref[0].astype(F32)
            for i in range(1, p_ref.shape[0]):
                acc = acc + p_ref[i].astype(F32)
            o_ref[off:off + n, :] = acc
            off += n

    return pl.pallas_call(
        body, out_shape=jax.ShapeDtypeStruct((sum(rows), c), F32), grid=(c // SLAB_COLS,),
        in_specs=[pl.BlockSpec((p.shape[0], p.shape[1], SLAB_COLS), lambda i: (0, 0, i)) for p in parts],
        out_specs=pl.BlockSpec((sum(rows), SLAB_COLS), lambda i: (0, i)),
        compiler_params=_params("parallel"), name=name)(*parts)


def _retention_tables():
    lg = np.log1p(-(2.0 ** (-5.0 - np.arange(RET_HEADS, dtype=np.float64))))
    idx = np.arange(CHUNK, dtype=np.float64)
    intra = np.exp(np.abs(idx[:, None] - idx[None, :])[None] * lg[:, None, None])
    qd = np.exp((idx[None, :] + 1.0) * lg[:, None])
    kd = np.exp((CHUNK - 1.0 - idx[None, :]) * lg[:, None])
    cd = np.exp(CHUNK * lg)
    ones = np.ones((1, 1, RET_DK))
    return (jnp.asarray(intra, F32), jnp.asarray(qd[:, :, None] * ones, F32),
            jnp.asarray(kd[:, :, None] * ones, F32), jnp.asarray(cd[:, None, None] * ones, F32))


def _rope(t, cos, sin):
    t1, t2 = t[:, :ROPE_HALF], t[:, ROPE_HALF:]
    return jnp.concatenate([t1 * cos - t2 * sin, t2 * cos + t1 * sin], axis=1)


def _rope_t(d, cos, sin):
    d1, d2 = d[:, :ROPE_HALF], d[:, ROPE_HALF:]
    return jnp.concatenate([d1 * cos + d2 * sin, d2 * cos - d1 * sin], axis=1)


_HEADS = range(RET_HEADS)


def _ret_chunk_fwd(p_ref, cos, sin, intra_ref, qd_ref, st):
    def seg(i, h):
        return p_ref[:, i * RET_W + h * RET_DK:i * RET_W + (h + 1) * RET_DK]

    v = [seg(2, h) for h in _HEADS]
    qr = [_rope(seg(0, h), cos, sin) for h in _HEADS]
    kr = [_rope(seg(1, h), cos, sin) * (RET_DK ** -0.5) for h in _HEADS]
    qrb, vb = [_bf(a) for a in qr], [_bf(a) for a in v]
    sc = [_dot_nt(qrb[h], _bf(kr[h])) * intra_ref[h] for h in _HEADS]
    qs = [_dot(qrb[h], _bf(st[h])) for h in _HEADS]
    y = [_dot(_bf(sc[h]), vb[h]) + qs[h] * qd_ref[h] for h in _HEADS]
    g = [seg(3, h) for h in _HEADS]
    return dict(v=v, vb=vb, qr=qr, qrb=qrb, kr=kr, sc=sc, y=y, g=g)


def _group_norm(y):
    mu = jnp.mean(y, axis=1, keepdims=True)
    yc = y - mu
    rstd = lax.rsqrt(jnp.mean(yc * yc, axis=1, keepdims=True) + EPS)
    return yc * rstd, rstd


def _chunk_rows(sub):
    return pl.ds(sub * CHUNK, CHUNK)


def _ret_specs(steps, rev):
    cidx = (lambda c: steps - 1 - c) if rev else (lambda c: c)
    return dict(
        proj=pl.BlockSpec((STEP_ROWS, 4 * RET_W), lambda c: (cidx(c), 0)),
        half=pl.BlockSpec((STEP_ROWS, ROPE_HALF), lambda c: (cidx(c), 0)),
        wide=pl.BlockSpec((STEP_ROWS, RET_W), lambda c: (cidx(c), 0)),
        state=pl.BlockSpec((CHUNKS_PER_STEP, RET_HEADS, RET_DK, RET_DK), lambda c: (cidx(c), 0, 0, 0)),
        intra=_full_spec((RET_HEADS, CHUNK, CHUNK)),
        dec=_full_spec((RET_HEADS, CHUNK, RET_DK)),
        cd=_full_spec((RET_HEADS, 1, RET_DK)),
    )


def _retention_fwd(p_r, cos, sin, name):
    s = p_r.shape[0]
    nc = s // CHUNK
    intra_t, qd_t, kd_t, cd_t = _retention_tables()

    def body(p_ref, cos_ref, sin_ref, intra_ref, qd_ref, kd_ref, cd_ref, y_ref, st_ref, state):
        @pl.when(pl.program_id(0) == 0)
        def _():
            state[...] = jnp.zeros_like(state)

        for sub in range(CHUNKS_PER_STEP):
            rows = _chunk_rows(sub)
            y_v, st_v = y_ref.at[rows], st_ref.at[sub]
            st = [state[h] for h in _HEADS]
            f = _ret_chunk_fwd(p_ref.at[rows], cos_ref[rows, :], sin_ref[rows, :], intra_ref, qd_ref, st)
            new_st = [st[h] * cd_ref[h] + _dot_tn(_bf(f["kr"][h] * kd_ref[h]), f["vb"][h]) for h in _HEADS]
            out = [_bf(_group_norm(f["y"][h])[0] * (f["g"][h] * _sigmoid(f["g"][h]))) for h in _HEADS]
            for h in _HEADS:
                st_v[h] = _bf(st[h])
                state[h] = new_st[h]
                y_v[:, h * RET_DK:(h + 1) * RET_DK] = out[h]

    steps = nc // CHUNKS_PER_STEP
    sp = _ret_specs(steps, False)
    return pl.pallas_call(
        body,
        out_shape=(jax.ShapeDtypeStruct((s, RET_W), BF16),
                   jax.ShapeDtypeStruct((nc, RET_HEADS, RET_DK, RET_DK), BF16)),
        grid=(steps,),
        in_specs=[sp["proj"], sp["half"], sp["half"], sp["intra"], sp["dec"], sp["dec"], sp["cd"]],
        out_specs=(sp["wide"], sp["state"]),
        scratch_shapes=[pltpu.VMEM((RET_HEADS, RET_DK, RET_DK), F32)],
        compiler_params=_params("arbitrary"), name=name)(p_r, cos, sin, intra_t, qd_t, kd_t, cd_t)


def _retention_bwd(p_r, cos, sin, states, dy_r, name):
    s = p_r.shape[0]
    nc = s // CHUNK
    intra_t, qd_t, kd_t, cd_t = _retention_tables()

    def body(p_ref, cos_ref, sin_ref, intra_ref, qd_ref, kd_ref, cd_ref, st_ref, dy_ref, dp_ref, dstate):
        @pl.when(pl.program_id(0) == 0)
        def _():
            dstate[...] = jnp.zeros_like(dstate)

        for sub in reversed(range(CHUNKS_PER_STEP)):
            rows = _chunk_rows(sub)
            chunk(p_ref.at[rows], cos_ref[rows, :], sin_ref[rows, :], intra_ref, qd_ref, kd_ref, cd_ref,
                  st_ref.at[sub], dy_ref.at[rows], dp_ref.at[rows], dstate)

    def chunk(p_ref, cos, sin, intra_ref, qd_ref, kd_ref, cd_ref, st_ref, dy_ref, dp_ref, dstate):
        st = [st_ref[h] for h in _HEADS]
        dsn = [dstate[h] for h in _HEADS]
        f = _ret_chunk_fwd(p_ref, cos, sin, intra_ref, qd_ref, st)
        vb, qrb, kr, g = f["vb"], f["qrb"], f["kr"], f["g"]
        norm = [_group_norm(f["y"][h]) for h in _HEADS]
        sg = [_sigmoid(g[h]) for h in _HEADS]
        dyr = [dy_ref[:, h * RET_DK:(h + 1) * RET_DK] for h in _HEADS]
        dyn = [dyr[h] * (g[h] * sg[h]) for h in _HEADS]
        dg = [dyr[h] * norm[h][0] * (sg[h] * (1.0 + g[h] * (1.0 - sg[h]))) for h in _HEADS]
        dy = [norm[h][1] * (dyn[h] - jnp.mean(dyn[h], axis=1, keepdims=True)
                            - norm[h][0] * jnp.mean(dyn[h] * norm[h][0], axis=1, keepdims=True)) for h in _HEADS]
        dyb, dsnb = [_bf(a) for a in dy], [_bf(a) for a in dsn]
        ds = [_bf(_dot_nt(dyb[h], vb[h]) * intra_ref[h]) for h in _HEADS]
        t = [_bf(dy[h] * qd_ref[h]) for h in _HEADS]
        dv = [_dot_tn(_bf(f["sc"][h]), dyb[h]) + _dot(_bf(kr[h] * kd_ref[h]), dsnb[h]) for h in _HEADS]
        dqr = [_dot(ds[h], _bf(kr[h])) + _dot_nt(t[h], _bf(st[h])) for h in _HEADS]
        dkr = [_dot_tn(ds[h], qrb[h]) + _dot_nt(vb[h], dsnb[h]) * kd_ref[h] for h in _HEADS]
        new_ds = [dsn[h] * cd_ref[h] + _dot_tn(qrb[h], t[h]) for h in _HEADS]
        for h in _HEADS:
            lo = h * RET_DK
            dstate[h] = new_ds[h]
            dp_ref[:, lo:lo + RET_DK] = _bf(_rope_t(dqr[h], cos, sin))
            dp_ref[:, RET_W + lo:RET_W + lo + RET_DK] = _bf(_rope_t(dkr[h], cos, sin) * (RET_DK ** -0.5))
            dp_ref[:, 2 * RET_W + lo:2 * RET_W + lo + RET_DK] = _bf(dv[h])
            dp_ref[:, 3 * RET_W + lo:3 * RET_W + lo + RET_DK] = _bf(dg[h])

    steps = nc // CHUNKS_PER_STEP
    sp = _ret_specs(steps, True)
    return pl.pallas_call(
        body,
        out_shape=jax.ShapeDtypeStruct((s, 4 * RET_W), BF16),
        grid=(steps,),
        in_specs=[sp["proj"], sp["half"], sp["half"], sp["intra"], sp["dec"], sp["dec"], sp["cd"],
                  sp["state"], sp["wide"]],
        out_specs=sp["proj"],
        scratch_shapes=[pltpu.VMEM((RET_HEADS, RET_DK, RET_DK), F32)],
        compiler_params=_params("arbitrary"), name=name)(p_r, cos, sin, intra_t, qd_t, kd_t, cd_t, states, dy_r)


def _conv_taps(ext, w):
    acc = w[SSD_CONV - 1:SSD_CONV] * ext
    for j in range(SSD_CONV - 1):
        acc = acc + w[j:j + 1] * pltpu.roll(ext, SSD_CONV - 1 - j, axis=0)
    return acc


def _conv_fwd(xbc_raw, conv_w, conv_b, name):
    s = xbc_raw.shape[0]
    t8 = CONV_TILE // 8

    def body(cur_ref, prev_ref, w_ref, b_ref, o_ref, slope_ref):
        keep = (pl.program_id(0) > 0).astype(F32)
        ext = jnp.concatenate([prev_ref[...] * keep, cur_ref[...]], axis=0)
        u = _conv_taps(ext, w_ref[...])[8:] + b_ref[...]
        sg = _sigmoid(u)
        o_ref[...] = u * sg
        slope_ref[...] = sg * (1.0 + u * (1.0 - sg))

    out = jax.ShapeDtypeStruct((s, SSD_CD), F32)
    return pl.pallas_call(
        body, out_shape=(out, out), grid=(s // CONV_TILE,),
        in_specs=[_row_spec(SSD_CD, CONV_TILE),
                  pl.BlockSpec((8, SSD_CD), lambda i: (jnp.maximum(i * t8 - 1, 0), 0)),
                  _full_spec((SSD_CONV, SSD_CD)), _full_spec((1, SSD_CD))],
        out_specs=(_row_spec(SSD_CD, CONV_TILE), _row_spec(SSD_CD, CONV_TILE)),
        compiler_params=_params("parallel"), name=name)(xbc_raw, xbc_raw, conv_w, conv_b)


def _conv_bwd(xbc_raw, conv_w, slope, dact, name):
    s = xbc_raw.shape[0]
    nt = s // CONV_TILE
    t8 = CONV_TILE // 8
    rows = CONV_TILE + 8

    def body(cur_ref, prev_ref, w_ref, s_ref, snext_ref, d_ref, dnext_ref, dx_ref, dw_ref, db_ref):
        i = pl.program_id(0)

        @pl.when(i == 0)
        def _():
            dw_ref[...] = jnp.zeros_like(dw_ref)
            db_ref[...] = jnp.zeros_like(db_ref)

        keep_prev = (i > 0).astype(F32)
        keep_next = (i < nt - 1).astype(F32)
        w = w_ref[...]
        cur = cur_ref[...]
        ext = jnp.concatenate([prev_ref[...] * keep_prev, cur], axis=0)
        duc = d_ref[...] * s_ref[...]
        du = jnp.concatenate([duc, dnext_ref[...] * snext_ref[...] * keep_next], axis=0)
        dx = w[SSD_CONV - 1:SSD_CONV] * du
        for j in range(SSD_CONV - 1):
            dx = dx + w[j:j + 1] * pltpu.roll(du, rows - (SSD_CONV - 1 - j), axis=0)
        dx_ref[...] = _bf(dx[:CONV_TILE])
        for j in range(SSD_CONV - 1):
            shifted = pltpu.roll(ext, SSD_CONV - 1 - j, axis=0)[8:]
            dw_ref[j:j + 1, :] += jnp.sum(duc * shifted, axis=0, keepdims=True)
        dw_ref[SSD_CONV - 1:SSD_CONV, :] += jnp.sum(duc * cur, axis=0, keepdims=True)
        db_ref[...] += jnp.sum(duc, axis=0, keepdims=True)

    row = _row_spec(SSD_CD, CONV_TILE)
    prev = pl.BlockSpec((8, SSD_CD), lambda i: (jnp.maximum(i * t8 - 1, 0), 0))
    nxt = pl.BlockSpec((8, SSD_CD), lambda i: (jnp.minimum((i + 1) * t8, s // 8 - 1), 0))
    return pl.pallas_call(
        body,
        out_shape=(jax.ShapeDtypeStruct((s, SSD_CD), BF16), jax.ShapeDtypeStruct((SSD_CONV, SSD_CD), F32),
                   jax.ShapeDtypeStruct((1, SSD_CD), F32)),
        grid=(nt,),
        in_specs=[row, prev, _full_spec((SSD_CONV, SSD_CD)), row, nxt, row, nxt],
        out_specs=(row, _full_spec((SSD_CONV, SSD_CD)), _full_spec((1, SSD_CD))),
        compiler_params=_params("arbitrary"), name=name)(xbc_raw, xbc_raw, conv_w, slope, slope, dact, dact)


def _head_select():
    e = np.zeros((HEAD_PAD, SSD_W), np.float32)
    for h in range(SSD_HEADS):
        e[h, h * SSD_P:(h + 1) * SSD_P] = 1.0
    return jnp.asarray(e, BF16), jnp.asarray(e.T, BF16)


def _pad_heads(v):
    return jnp.pad(v.reshape(1, SSD_HEADS).astype(F32), ((0, 0), (0, HEAD_PAD - SSD_HEADS)))


def _ssd_masks():
    r = _iota((CHUNK, SSD_QW), 0)
    c = _iota((CHUNK, SSD_QW), 1) % SSD_P
    itile = (r == c).astype(F32)
    ctile = r >= c
    rb = _iota((SSD_QW, SSD_QW), 0) // SSD_P
    cb = _iota((SSD_QW, SSD_QW), 1) // SSD_P
    return itile, ctile, rb == cb


_TILES = [slice(i * SSD_QW, (i + 1) * SSD_QW) for i in range(SSD_W // SSD_QW)]


def _ssd_heads(dtr_ref, dtb_ref, alog_ref):
    u = dtr_ref[...] + dtb_ref[...]
    dt = _softplus(u)
    nexp = -jnp.exp(alog_ref[...])
    return u, dt, nexp, dt * nexp


def _ssd_group(xbc_ref, acol_ref, dte_ref, gs):
    acol = acol_ref[:, gs]
    alast = acol[CHUNK - 1:CHUNK, :]
    xs, dte = xbc_ref[:, gs], dte_ref[:, gs]
    return dict(xs=xs, dte=dte, xdt=xs * dte, ea=jnp.exp(acol), tail=jnp.exp(alast - acol), eal=jnp.exp(alast))


def _silu_gate(z_ref, y_ref, tl):
    zv = z_ref[:, tl]
    sz = _sigmoid(zv)
    return zv, sz, y_ref[:, tl] * (zv * sz)


def _tile4(x):
    return jnp.concatenate([x, x, x, x], axis=0)


def _fold4(x):
    return x[0:CHUNK] + x[CHUNK:2 * CHUNK] + x[2 * CHUNK:3 * CHUNK] + x[3 * CHUNK:4 * CHUNK]


def _ssd_specs(steps, rev):
    cidx = (lambda c: steps - 1 - c) if rev else (lambda c: c)
    return dict(
        xbc=pl.BlockSpec((STEP_ROWS, SSD_CD), lambda c: (cidx(c), 0)),
        dt=pl.BlockSpec((STEP_ROWS, HEAD_PAD), lambda c: (cidx(c), 0)),
        wide=pl.BlockSpec((STEP_ROWS, SSD_W), lambda c: (cidx(c), 0)),
        state=pl.BlockSpec((CHUNKS_PER_STEP, SSD_N, SSD_W), lambda c: (cidx(c), 0, 0)),
        head=_full_spec((1, HEAD_PAD)),
        roww=_full_spec((1, SSD_W)),
        e=_full_spec((HEAD_PAD, SSD_W)),
        et=_full_spec((SSD_W, HEAD_PAD)),
    )


_GROUPS = range(SSD_GROUPS)
_GROUP_LANES = [slice(g * SSD_GW, (g + 1) * SSD_GW) for g in _GROUPS]
_QUADS = [(g, slice(g * SSD_GW + q * SSD_QW, g * SSD_GW + (q + 1) * SSD_QW), slice(q * SSD_QW, (q + 1) * SSD_QW))
          for g in _GROUPS for q in range(SSD_GW // SSD_QW)]


def _ssd_bc(xbc_ref):
    bg = [_bf(xbc_ref[:, B_OFF + g * SSD_N:B_OFF + (g + 1) * SSD_N]) for g in _GROUPS]
    cg = [_bf(xbc_ref[:, C_OFF + g * SSD_N:C_OFF + (g + 1) * SSD_N]) for g in _GROUPS]
    return bg, cg


def _ssd_decay(aq, itile, ctile):
    arow = jnp.sum(aq * itile, axis=0, keepdims=True)
    return jnp.exp(jnp.where(ctile, aq - arow, -jnp.inf))


def _ssd_blockdiag(xq, bdmask):
    return jnp.where(bdmask, _tile4(_bf(xq)), jnp.zeros((), BF16))


def _ssd_fwd(xbc, dt_raw, z, dt_bias, a_log, dske, norm_w, name):
    s = xbc.shape[0]
    nc = s // CHUNK
    e_sel, _ = _head_select()

    def body(xbc_ref, dtr_ref, z_ref, dtb_ref, alog_ref, dske_ref, nw_ref, e_ref,
             yraw_ref, ys_ref, st_ref, acol_ref, dte_ref, state):
        @pl.when(pl.program_id(0) == 0)
        def _():
            state[...] = jnp.zeros_like(state)

        for sub in range(CHUNKS_PER_STEP):
            rows = _chunk_rows(sub)
            chunk(xbc_ref.at[rows], dtr_ref.at[rows], z_ref.at[rows], dtb_ref, alog_ref, dske_ref, nw_ref, e_ref,
                  yraw_ref.at[rows], ys_ref.at[rows], st_ref.at[sub], acol_ref.at[rows], dte_ref.at[rows], state)

    def chunk(xbc_ref, dtr_ref, z_ref, dtb_ref, alog_ref, dske_ref, nw_ref, e_ref,
              yraw_ref, ys_ref, st_ref, acol_ref, dte_ref, state):
        _, dt, _, a = _ssd_heads(dtr_ref, dtb_ref, alog_ref)
        tril = _bf((_iota((CHUNK, CHUNK), 0) >= _iota((CHUNK, CHUNK), 1)).astype(F32))
        ac3, dt3 = _split3(_dot_exact_r(tril, a)), _split3(dt)
        for tl in _TILES:
            e_t = e_ref[:, tl]
            acol_ref[:, tl] = _dot(ac3[0], e_t) + _dot(ac3[1], e_t) + _dot(ac3[2], e_t)
            dte_ref[:, tl] = _dot(dt3[0], e_t) + _dot(dt3[1], e_t) + _dot(dt3[2], e_t)

        itile, ctile, bdmask = _ssd_masks()
        st_ref[...] = state[...]
        bg, cg = _ssd_bc(xbc_ref)
        for g0 in range(0, SSD_GROUPS, SSD_BLOCK_GROUPS):
            gids = range(g0, g0 + SSD_BLOCK_GROUPS)
            quads = [(g, ql, qs) for g, ql, qs in _QUADS if g in gids]
            q = {g: _ssd_group(xbc_ref, acol_ref, dte_ref, _GROUP_LANES[g]) for g in gids}
            stg = {g: state[:, _GROUP_LANES[g]] for g in gids}
            cbt = {g: _dot_nt(cg[g], _tile4(bg[g])) for g in gids}
            ys = {g: _dot(cg[g], _bf(stg[g])) for g in gids}
            dq = [_ssd_decay(acol_ref[:, ql], itile, ctile) for _, ql, _ in quads]
            xbd = [_ssd_blockdiag(q[g]["xdt"][:, qs], bdmask) for g, _, qs in quads]
            yq = [_dot(_bf(cbt[g] * dq[i]), xbd[i]) + ys[g][:, qs] * q[g]["ea"][:, qs]
                  + dske_ref[:, ql] * q[g]["xs"][:, qs] for i, (g, ql, qs) in enumerate(quads)]
            new_st = {g: stg[g] * q[g]["eal"] + _dot_tn(bg[g], _bf(q[g]["xdt"] * q[g]["tail"])) for g in gids}
            for i, (_, ql, _) in enumerate(quads):
                yraw_ref[:, ql] = yq[i]
            for g in gids:
                state[:, _GROUP_LANES[g]] = new_st[g]

        sq = jnp.zeros((CHUNK, SSD_QW), F32)
        for tl in _TILES:
            t = _silu_gate(z_ref, yraw_ref, tl)[2]
            sq = sq + t * t
        rstd = lax.rsqrt(jnp.sum(sq, axis=1, keepdims=True) * (1.0 / SSD_W) + EPS)
        for tl in _TILES:
            ys_ref[:, tl] = _bf(_silu_gate(z_ref, yraw_ref, tl)[2] * rstd * nw_ref[:, tl])

    steps = nc // CHUNKS_PER_STEP
    sp = _ssd_specs(steps, False)
    wide = jax.ShapeDtypeStruct((s, SSD_W), F32)
    return pl.pallas_call(
        body,
        out_shape=(wide, jax.ShapeDtypeStruct((s, SSD_W), BF16), jax.ShapeDtypeStruct((nc, SSD_N, SSD_W), F32),
                   wide, wide),
        grid=(steps,),
        in_specs=[sp["xbc"], sp["dt"], sp["wide"], sp["head"], sp["head"], sp["roww"], sp["roww"], sp["e"]],
        out_specs=(sp["wide"], sp["wide"], sp["state"], sp["wide"], sp["wide"]),
        scratch_shapes=[pltpu.VMEM((SSD_N, SSD_W), F32)],
        compiler_params=_params("arbitrary"), name=name,
    )(xbc, dt_raw, z, dt_bias, a_log, dske, norm_w, e_sel)


def _ssd_bwd(xbc, dt_raw, z, yraw, states, dys, acol, dte, dt_bias, a_log, dske, norm_w, name, rider):
    s = xbc.shape[0]
    nc = s // CHUNK
    steps = nc // CHUNKS_PER_STEP
    _, et_sel = _head_select()
    n_in, n_out, n_scratch = 13, 7, 5
    n_rin, n_rout = len(rider.inputs), len(rider.out_shapes)

    def body(*refs):
        ins, refs = refs[:n_in], refs[n_in:]
        r_ins, refs = refs[:n_rin], refs[n_rin:]
        outs, refs = refs[:n_out], refs[n_out:]
        r_outs, refs = refs[:n_rout], refs[n_rout:]
        ride = functools.partial(rider.run, pl.program_id(0), steps, r_ins, r_outs, refs[n_scratch],
                                 refs[n_scratch + 1])
        ride(last=False)
        compute(*ins, *outs, *refs[:n_scratch])
        ride(last=True)

    def compute(xbc_ref, dtr_ref, z_ref, yraw_ref, st_ref, dys_ref, acol_ref, dte_ref, dtb_ref, alog_ref,
                dske_ref, nw_ref, et_ref, dxbc_ref, dz_ref, ddtr_ref, ddtb_ref, dalog_ref, ddsk_ref, dnw_ref,
                dstate, dsk_acc, dy_s, dacol_s, ddte_s):
        @pl.when(pl.program_id(0) == 0)
        def _():
            dstate[...] = jnp.zeros_like(dstate)
            dsk_acc[...] = jnp.zeros_like(dsk_acc)
            ddtb_ref[...] = jnp.zeros_like(ddtb_ref)
            dalog_ref[...] = jnp.zeros_like(dalog_ref)
            dnw_ref[...] = jnp.zeros_like(dnw_ref)

        for sub in reversed(range(CHUNKS_PER_STEP)):
            rows = _chunk_rows(sub)
            chunk(xbc_ref.at[rows], dtr_ref.at[rows], z_ref.at[rows], yraw_ref.at[rows], st_ref.at[sub],
                  dys_ref.at[rows], acol_ref.at[rows], dte_ref.at[rows], dtb_ref, alog_ref, dske_ref, nw_ref, et_ref,
                  dxbc_ref.at[rows], dz_ref.at[rows], ddtr_ref.at[rows], ddtb_ref, dalog_ref, dnw_ref,
                  dstate, dsk_acc, dy_s, dacol_s, ddte_s)

        @pl.when(pl.program_id(0) == steps - 1)
        def _():
            ddsk_ref[...] = _dot_exact_l(jnp.broadcast_to(dsk_acc[...], (8, SSD_W)), et_ref[...])[0:1]

    def chunk(xbc_ref, dtr_ref, z_ref, yraw_ref, st_ref, dys_ref, acol_ref, dte_ref, dtb_ref, alog_ref,
              dske_ref, nw_ref, et_ref, dxbc_ref, dz_ref, ddtr_ref, ddtb_ref, dalog_ref, dnw_ref,
              dstate, dsk_acc, dy_s, dacol_s, ddte_s):
        itile, ctile, bdmask = _ssd_masks()
        last_row = (_iota((CHUNK, 1), 0) == CHUNK - 1).astype(F32)

        s1 = jnp.zeros((CHUNK, SSD_QW), F32)
        s2 = jnp.zeros((CHUNK, SSD_QW), F32)
        for tl in _TILES:
            t = _silu_gate(z_ref, yraw_ref, tl)[2]
            s1 = s1 + t * t
            s2 = s2 + dys_ref[:, tl] * nw_ref[:, tl] * t
        rstd = lax.rsqrt(jnp.sum(s1, axis=1, keepdims=True) * (1.0 / SSD_W) + EPS)
        back = rstd * rstd * rstd * jnp.sum(s2, axis=1, keepdims=True) * (1.0 / SSD_W)
        for tl in _TILES:
            zv, sz, t = _silu_gate(z_ref, yraw_ref, tl)
            dysv = dys_ref[:, tl]
            dt_ = rstd * (dysv * nw_ref[:, tl]) - t * back
            dnw_ref[:, tl] += jnp.sum(dysv * t * rstd, axis=0, keepdims=True)
            dz_ref[:, tl] = _bf(dt_ * yraw_ref[:, tl] * (sz * (1.0 + zv * (1.0 - sz))))
            dy_t = dt_ * (zv * sz)
            dy_s[:, tl] = dy_t
            dsk_acc[:, tl] += jnp.sum(dy_t * xbc_ref[:, tl], axis=0, keepdims=True)

        gl = _GROUP_LANES
        bg, cg = _ssd_bc(xbc_ref)
        for g0 in range(0, SSD_GROUPS, SSD_BLOCK_GROUPS):
            gids = range(g0, g0 + SSD_BLOCK_GROUPS)
            quads = [(i, g, ql, qs) for i, (g, ql, qs) in enumerate(_QUADS) if g in gids]
            q = {g: _ssd_group(xbc_ref, acol_ref, dte_ref, gl[g]) for g in gids}
            stg = {g: st_ref[:, gl[g]] for g in gids}
            dsn = {g: dstate[:, gl[g]] for g in gids}
            stgb, dsnb = {g: _bf(stg[g]) for g in gids}, {g: _bf(dsn[g]) for g in gids}
            btile = {g: _tile4(bg[g]) for g in gids}
            cbt = {g: _dot_nt(cg[g], btile[g]) for g in gids}
            dyg = {g: dy_s[:, gl[g]] for g in gids}
            eag, tailg = {g: q[g]["ea"] for g in gids}, {g: q[g]["tail"] for g in gids}
            xdtg, ealg = {g: q[g]["xdt"] for g in gids}, {g: q[g]["eal"] for g in gids}
            dys_g = {g: _bf(dyg[g] * eag[g]) for g in gids}
            ysg = {g: _dot(cg[g], stgb[g]) for g in gids}
            dc0 = {g: _dot_nt(dys_g[g], stgb[g]) for g in gids}
            dst = {g: _dot_tn(cg[g], dys_g[g]) for g in gids}
            dwt = {g: _dot(bg[g], dsnb[g]) for g in gids}
            db0 = {g: _dot_nt(_bf(xdtg[g] * tailg[g]), dsnb[g]) for g in gids}
            dtl = {g: dwt[g] * xdtg[g] * tailg[g] for g in gids}
            dal_row = {g: jnp.sum(dtl[g], axis=0, keepdims=True)
                       + jnp.sum(dsn[g] * stg[g], axis=0, keepdims=True) * ealg[g] for g in gids}
            for g in gids:
                dstate[:, gl[g]] = dst[g] + dsn[g] * ealg[g]
            dq = {i: _ssd_decay(acol_ref[:, ql], itile, ctile) for i, _, ql, _ in quads}
            mq = {i: cbt[g] * dq[i] for i, g, _, _ in quads}
            xbd = {i: _ssd_blockdiag(xdtg[g][:, qs], bdmask) for i, g, _, qs in quads}
            dyq = {i: _bf(dyg[g][:, qs]) for i, g, _, qs in quads}
            dm = {i: _dot_nt(dyq[i], xbd[i]) for i, _, _, _ in quads}
            dxdt_q = {i: _fold4(jnp.where(bdmask, _dot_tn(_bf(mq[i]), dyq[i]), 0.0)) for i, _, _, _ in quads}
            eq = {i: dm[i] * mq[i] for i, _, _, _ in quads}
            dacol_q = {i: eq[i] - itile * jnp.sum(eq[i], axis=0, keepdims=True) for i, _, _, _ in quads}
            dcbt = {i: _bf(dm[i] * dq[i]) for i, _, _, _ in quads}
            dc_q = {i: _dot(dcbt[i], btile[g]) for i, g, _, _ in quads}
            db_q = {i: _fold4(_dot_tn(dcbt[i], cg[g])) for i, g, _, _ in quads}
            for g in gids:
                dxdt_g = jnp.concatenate([dxdt_q[2 * g], dxdt_q[2 * g + 1]], axis=1) + dwt[g] * tailg[g]
                dxbc_ref[:, gl[g]] = dyg[g] * dske_ref[:, gl[g]] + dxdt_g * q[g]["dte"]
                ddte_s[:, gl[g]] = dxdt_g * q[g]["xs"]
                dacol_s[:, gl[g]] = (jnp.concatenate([dacol_q[2 * g], dacol_q[2 * g + 1]], axis=1)
                                     + dyg[g] * (ysg[g] * eag[g]) - dtl[g] + last_row * dal_row[g])
                dxbc_ref[:, B_OFF + g * SSD_N:B_OFF + (g + 1) * SSD_N] = db0[g] + db_q[2 * g] + db_q[2 * g + 1]
                dxbc_ref[:, C_OFF + g * SSD_N:C_OFF + (g + 1) * SSD_N] = dc0[g] + dc_q[2 * g] + dc_q[2 * g + 1]

        dacum = jnp.zeros((CHUNK, HEAD_PAD), F32)
        ddt = jnp.zeros((CHUNK, HEAD_PAD), F32)
        for i, tl in enumerate(_TILES):
            et_t = et_ref[i * SSD_QW:(i + 1) * SSD_QW, :]
            dacum = dacum + _dot_exact_l(dacol_s[:, tl], et_t, pieces=2)
            ddt = ddt + _dot_exact_l(ddte_s[:, tl], et_t, pieces=2)
        u, _, nexp, a = _ssd_heads(dtr_ref, dtb_ref, alog_ref)
        triu = _bf((_iota((CHUNK, CHUNK), 1) >= _iota((CHUNK, CHUNK), 0)).astype(F32))
        da = _dot_exact_r(triu, dacum)
        ddt = ddt + da * nexp
        dalog_ref[...] += jnp.sum(da * a, axis=0, keepdims=True)
        du = ddt * _sigmoid(u)
        ddtr_ref[...] = _bf(du)
        ddtb_ref[...] += jnp.sum(du, axis=0, keepdims=True)

    sp = _ssd_specs(steps, True)
    return pl.pallas_call(
        body,
        out_shape=(jax.ShapeDtypeStruct((s, SSD_CD), F32), jax.ShapeDtypeStruct((s, SSD_W), BF16),
                   jax.ShapeDtypeStruct((s, HEAD_PAD), BF16), jax.ShapeDtypeStruct((1, HEAD_PAD), F32),
                   jax.ShapeDtypeStruct((1, HEAD_PAD), F32), jax.ShapeDtypeStruct((1, HEAD_PAD), F32),
                   jax.ShapeDtypeStruct((1, SSD_W), F32), *rider.out_shapes),
        grid=(steps,),
        in_specs=[sp["xbc"], sp["dt"], sp["wide"], sp["wide"], sp["state"], sp["wide"], sp["wide"], sp["wide"],
                  sp["head"], sp["head"], sp["roww"], sp["roww"], sp["et"], *rider.in_specs()],
        out_specs=(sp["xbc"], sp["wide"], sp["dt"], sp["head"], sp["head"], sp["head"], sp["roww"],
                   *rider.out_specs()),
        scratch_shapes=[pltpu.VMEM((SSD_N, SSD_W), F32), pltpu.VMEM((1, SSD_W), F32),
                        *[pltpu.VMEM((CHUNK, SSD_W), F32)] * 3, *rider.scratch()],
        compiler_params=_params("arbitrary"), name=name,
    )(xbc, dt_raw, z, yraw, states, dys, acol, dte, dt_bias, a_log, dske, norm_w, et_sel, *rider.inputs)


W_IN_SHARD = IN_PROJ // N_DEV
W_IN_ROW_PARTS = ((0, 1152), (1152, 832), (1984, 840))


def _device_step(x, pos_col, target, norm1_w, conv_w_shard, conv_b, dt_bias, a_log, d_skip, ssd_norm_w, norm_f_w,
                 w_in_shard, br_ret_shard, br_ssd_shard, out_shard):
    inv_freq = jnp.asarray(ROPE_THETA ** (-np.arange(ROPE_HALF, dtype=np.float64) / ROPE_HALF), F32).reshape(1, ROPE_HALF)
    dtb, alog = _pad_heads(dt_bias), _pad_heads(a_log)
    dske = jnp.repeat(d_skip.reshape(SSD_HEADS).astype(F32), SSD_P).reshape(1, SSD_W)
    my_core = lax.axis_index("c")

    h, w_all = _rmsnorm_fwd(x, norm1_w, "rmsnorm1_fwd", _gather_rider([w_in_shard], 1.0, 1.0))
    w_all = w_all.reshape(IN_PROJ, D_MODEL)
    w_dt = jnp.pad(w_all[OFF_DT:OFF_G], ((0, HEAD_PAD - SSD_HEADS), (0, 0)))
    w_g = w_all[OFF_G:]
    rows_r, rows_z, rows_xbc = (0, OFF_Z), (OFF_Z, SSD_W), (OFF_XBC, SSD_CD)
    gather = _gather_rider([br_ret_shard, br_ssd_shard, out_shard, conv_w_shard], relay_at=0.35, pass_at=0.6)
    p_r, all_ret, all_ssd, all_out, all_conv = _matmul(h, w_all, "nt", F32, "proj_ret", rider=gather, b_rows=rows_r)
    w_br_ret = all_ret.reshape(RET_W, D_MODEL)
    w_br_ssd = all_ssd.reshape(SSD_W, D_MODEL)
    w_out = all_out.reshape(D_MODEL, D_MODEL)
    conv_w = all_conv.transpose(1, 0, 2).reshape(SSD_CONV, SSD_CD)
    p_z = _matmul(h, w_all, "nt", F32, "proj_z", b_rows=rows_z)
    p_xbc = _matmul(h, w_all, "nt", F32, "proj_xbc", b_rows=rows_xbc)
    p_dt = _matmul(h, w_dt, "nt", F32, "proj_dt")
    p_g = _matmul(h, w_g, "nt", BF16, "proj_gates")
    cos, sin = _rope_tables(pos_col, inv_freq, "rope_tables")
    y_r, ret_states = _retention_fwd(p_r, cos, sin, "retention_fwd")
    xbc_act, silu_slope = _conv_fwd(p_xbc, conv_w, conv_b, "conv_fwd")
    y_raw, y_s, ssd_states, acol, dte = _ssd_fwd(xbc_act, p_dt, p_z, dtb, alog, dske, ssd_norm_w, "ssd_fwd")
    pr = _matmul(y_r, w_br_ret, "nn", BF16, "branch_ret")
    ps = _matmul(y_s, w_br_ssd, "nn", BF16, "branch_ssd")
    merged = _merge_fwd(p_g, pr, ps, "merge_fwd")
    o = _matmul(merged, w_out, "nn", F32, "out_proj")
    dx2, g_norm_f, loss_acc = _final_fwd_bwd(x, o, norm_f_w, target, "final_norm_loss")

    g_w_out = _matmul(merged, dx2, "tn", BF16, "grad_w_out")
    dmerged = _matmul(dx2, w_out, "nt", F32, "d_merged")
    dpr, dps, dp_g = _merge_bwd(p_g, pr, ps, dmerged, "merge_bwd")
    g_w_br_ret = _matmul(y_r, dpr, "tn", BF16, "grad_w_br_ret")
    g_w_br_ssd = _matmul(y_s, dps, "tn", BF16, "grad_w_br_ssd")
    dy_r = _matmul(dpr, w_br_ret, "nt", F32, "d_y_ret")
    dy_s = _matmul(dps, w_br_ssd, "nt", F32, "d_y_ssd")
    scatter = _scatter_rider([g_w_out.reshape(N_DEV, -1, D_MODEL), g_w_br_ret.reshape(N_DEV, -1, D_MODEL),
                              g_w_br_ssd.reshape(N_DEV, -1, D_MODEL)])
    dxbc_act, dp_z, dp_dt, g_dtb, g_alog, g_dsk, g_ssd_norm, got_out, got_ret, got_ssd = _ssd_bwd(
        xbc_act, p_dt, p_z, y_raw, ssd_states, dy_s, acol, dte, dtb, alog, dske, ssd_norm_w, "ssd_bwd", scatter)
    dp_xbc, g_conv_w, g_conv_b = _conv_bwd(p_xbc, conv_w, silu_slope, dxbc_act, "conv_bwd")
    dp_r = _retention_bwd(p_r, cos, sin, ret_states, dy_r, "retention_bwd")
    g_w_in = jnp.concatenate([
        _matmul(dp_r, h, "tn", BF16, "grad_w_ret"),
        _matmul(dp_z, h, "tn", BF16, "grad_w_z"),
        _matmul(dp_xbc, h, "tn", BF16, "grad_w_xbc"),
        _matmul(dp_dt, h, "tn", BF16, "grad_w_dt")[:SSD_HEADS],
        _matmul(dp_g, h, "tn", BF16, "grad_w_gates"),
    ], axis=0)
    blocks = g_w_in.reshape(N_DEV, W_IN_SHARD, D_MODEL)
    dh, from_sibling = _matmul(dp_r, w_all, "nn", F32, "d_h_ret", rider=_sibling_rider(blocks), b_rows=rows_r)
    chip_sum = _chip_sum(blocks, from_sibling, my_core, "w_in_chip_sum")
    carriers = (("d_h_xbc", dp_xbc, w_all, rows_xbc), ("d_h_gates", dp_g, w_g, None), ("d_h_z", dp_z, w_all, rows_z))
    landed = []
    for (row0, rows), (nm, dp, w, b_rows) in zip(W_IN_ROW_PARTS, carriers):
        dh, got = _matmul(dp, w, "nn", F32, nm, add=dh, rider=_chip_rider(chip_sum, row0, rows), b_rows=b_rows)
        landed.append(got)
    dh = _matmul(dp_dt, w_dt, "nn", F32, "d_h_dt", add=dh)
    grad_x, g_norm1 = _rmsnorm_bwd(x, norm1_w, dh, dx2, "rmsnorm1_bwd")
    small = dict(norm1_w=g_norm1, conv_w=g_conv_w, conv_b=g_conv_b, dt_bias=g_dtb[:, :SSD_HEADS],
                 a_log=g_alog[:, :SSD_HEADS], d_skip=g_dsk[:, :SSD_HEADS], ssd_norm_w=g_ssd_norm,
                 norm_f_w=g_norm_f)
    big = dict(w_in=_sum_row_parts(landed, "w_in_sum"), w_br_ret=_sum_slots(got_ret, "w_br_ret_sum"),
               w_br_ssd=_sum_slots(got_ssd, "w_br_ssd_sum"), w_out=_sum_slots(got_out, "w_out_sum"))
    return loss_acc[0, 0], grad_x, small, big


def _all_reduce_small(vec, name):
    r, c = vec.shape

    def body(x_ref, out_ref, land, send_sems, recv_sems):
        x, y, cc = _mesh_pos()
        my_idx = 4 * x + 2 * y + cc
        land[my_idx] = x_ref[...]
        copies = []
        for k in range(1, N_DEV):
            px, py, pc = x ^ (k >> 2), y ^ ((k >> 1) & 1), cc ^ (k & 1)
            cp = pltpu.make_async_remote_copy(
                src_ref=x_ref, dst_ref=land.at[my_idx],
                send_sem=send_sems.at[k - 1], recv_sem=recv_sems.at[k - 1],
                device_id=(px, py, pc), device_id_type=MESH)
            cp.start()
            copies.append(cp)
        for k in range(1, N_DEV):
            px, py, pc = x ^ (k >> 2), y ^ ((k >> 1) & 1), cc ^ (k & 1)
            pltpu.make_async_remote_copy(
                src_ref=x_ref, dst_ref=land.at[4 * px + 2 * py + pc],
                send_sem=send_sems.at[k - 1], recv_sem=recv_sems.at[k - 1],
                device_id=(px, py, pc), device_id_type=MESH).wait_recv()
        for cp in copies:
            cp.wait_send()
        acc = land[0]
        for i in range(1, N_DEV):
            acc = acc + land[i]
        out_ref[...] = acc

    return pl.pallas_call(
        body,
        out_shape=jax.ShapeDtypeStruct((r, c), F32),
        in_specs=[pl.BlockSpec(memory_space=pltpu.VMEM)],
        out_specs=pl.BlockSpec(memory_space=pltpu.VMEM),
        scratch_shapes=[pltpu.VMEM((N_DEV, r, c), F32), pltpu.SemaphoreType.DMA((7,)),
                        pltpu.SemaphoreType.DMA((7,))],
        name=name)(vec)


_SMALL = ("norm1_w", "conv_w", "conv_b", "dt_bias", "a_log", "d_skip", "ssd_norm_w", "norm_f_w")
_SMALL_COLS = 128
_W_IN = "w_in"
_WEIGHTS = ("norm1_w", "w_in", "conv_w", "conv_b", "dt_bias", "a_log", "d_skip", "ssd_norm_w",
            "w_br_ret", "w_br_ssd", "w_out", "norm_f_w")


def _pack(parts):
    flat = jnp.concatenate([p.reshape(-1).astype(F32) for p in parts])
    rows = -(-flat.shape[0] // (8 * _SMALL_COLS)) * 8
    return jnp.pad(flat, (0, rows * _SMALL_COLS - flat.shape[0])).reshape(rows, _SMALL_COLS)


def _unpack(packed, shapes):
    flat = packed.reshape(-1)
    out, off = [], 0
    for shp in shapes:
        n = int(np.prod(shp))
        out.append(flat[off:off + n].reshape(shp))
        off += n
    return out


def kernel(x, positions, norm1_w, w_in, conv_w, conv_b, dt_bias, a_log, d_skip, ssd_norm_w, w_br_ret, w_br_ssd, w_out, norm_f_w, loss_target, m_norm1_w, m_w_in, m_conv_w, m_conv_b, m_dt_bias, m_a_log, m_d_skip, m_ssd_norm_w, m_w_br_ret, m_w_br_ssd, m_w_out, m_norm_f_w, v_norm1_w, v_w_in, v_conv_w, v_conv_b, v_dt_bias, v_a_log, v_d_skip, v_ssd_norm_w, v_w_br_ret, v_w_br_ssd, v_w_out, v_norm_f_w):
    w = dict(norm1_w=norm1_w, w_in=w_in, conv_w=conv_w, conv_b=conv_b, dt_bias=dt_bias, a_log=a_log,
             d_skip=d_skip, ssd_norm_w=ssd_norm_w, w_br_ret=w_br_ret, w_br_ssd=w_br_ssd, w_out=w_out,
             norm_f_w=norm_f_w)
    m = dict(norm1_w=m_norm1_w, w_in=m_w_in, conv_w=m_conv_w, conv_b=m_conv_b, dt_bias=m_dt_bias,
             a_log=m_a_log, d_skip=m_d_skip, ssd_norm_w=m_ssd_norm_w, w_br_ret=m_w_br_ret,
             w_br_ssd=m_w_br_ssd, w_out=m_w_out, norm_f_w=m_norm_f_w)
    v = dict(norm1_w=v_norm1_w, w_in=v_w_in, conv_w=v_conv_w, conv_b=v_conv_b, dt_bias=v_dt_bias,
             a_log=v_a_log, d_skip=v_d_skip, ssd_norm_w=v_ssd_norm_w, w_br_ret=v_w_br_ret,
             w_br_ssd=v_w_br_ssd, w_out=v_w_out, norm_f_w=v_norm_f_w)
    s = x.shape[1]
    my_idx = 4 * lax.axis_index("x") + 2 * lax.axis_index("y") + lax.axis_index("c")

    w[_W_IN], m[_W_IN], v[_W_IN] = w_in[0].T, m_w_in[0].T, v_w_in[0].T

    loss_part, grad_x, g_small, g_big = _device_step(
        x[0], positions.reshape(s, 1), loss_target[0], norm1_w, conv_w[0], conv_b, dt_bias, a_log, d_skip,
        ssd_norm_w, norm_f_w.reshape(1, D_MODEL), _cast_bf16(w[_W_IN], "cast_w_in"),
        _cast_bf16(w_br_ret[0], "cast_w_br_ret"), _cast_bf16(w_br_ssd[0], "cast_w_br_ssd"),
        _cast_bf16(w_out[0], "cast_w_out"))

    loss = lax.psum(loss_part, ("x", "y", "c"))

    small_shapes = [g_small[n].shape for n in _SMALL]
    summed = _unpack(_all_reduce_small(_pack([g_small[n] for n in _SMALL]), "allreduce_small"), small_shapes)
    grads = dict(zip(_SMALL, summed))
    conv_cols = SSD_CD // N_DEV
    grads["conv_w"] = lax.dynamic_slice_in_dim(grads["conv_w"], my_idx * conv_cols, conv_cols, axis=1)
    grads["norm_f_w"] = grads["norm_f_w"].reshape(D_MODEL)
    for n in ("norm1_w", "conv_w", "conv_b", "dt_bias", "a_log", "d_skip", "ssd_norm_w"):
        grads[n] = grads[n].reshape(w[n].shape)

    delta, new_m, new_v = {}, {}, {}
    for n in ("w_br_ret", "w_br_ssd", "w_out"):
        w[n], m[n], v[n] = w[n][0], m[n][0], v[n][0]
    for n in (_W_IN, "w_br_ret", "w_br_ssd", "w_out"):
        back = (lambda a: a.T[None]) if n == _W_IN else (lambda a: a[None])
        res = _adamw(w[n], g_big[n], m[n], v[n], "adamw_" + n)
        grads[n] = back(g_big[n])
        delta[n], new_m[n], new_v[n] = (back(a) for a in res)
    shapes = [w[n].shape for n in _SMALL]
    packed = _adamw(_pack([w[n] for n in _SMALL]), _pack([grads[n] for n in _SMALL]),
                    _pack([m[n] for n in _SMALL]), _pack([v[n] for n in _SMALL]), "adamw_small")
    for res, dst in zip(packed, (delta, new_m, new_v)):
        for n, a in zip(_SMALL, _unpack(res, shapes)):
            dst[n] = a

    return (loss, grad_x.reshape(x.shape), *[grads[n] for n in _WEIGHTS], *[delta[n] for n in _WEIGHTS],
            *[new_m[n] for n in _WEIGHTS], *[new_v[n] for n in _WEIGHTS])
```

```python
import functools

import numpy as np
import jax
import jax.numpy as jnp
from jax import lax
from jax.experimental import pallas as pl
from jax.experimental.pallas import tpu as pltpu

F32 = jnp.float32
BF16 = jnp.bfloat16

D_MODEL = 2048
CHUNK = 64
CHUNKS_PER_STEP = 2
STEP_ROWS = CHUNK * CHUNKS_PER_STEP
EPS = 1e-6
N_DEV = 8

RET_HEADS = 8
RET_DK = 256
RET_W = RET_HEADS * RET_DK
ROPE_THETA = 10000.0
ROPE_HALF = RET_DK // 2

SSD_W = 4096
SSD_P = 64
SSD_HEADS = 64
SSD_GROUPS = 8
SSD_N = 128
SSD_GW = SSD_W // SSD_GROUPS
SSD_QW = 256
SSD_BLOCK_GROUPS = 2
SSD_CONV = 4
SSD_CD = SSD_W + 2 * SSD_GROUPS * SSD_N
HEAD_PAD = 128
B_OFF = SSD_W
C_OFF = SSD_W + SSD_GROUPS * SSD_N

ADAM_LR = 0.001
ADAM_B1 = 0.9
ADAM_B2 = 0.999
ADAM_EPS = 1e-08
ADAM_WD = 0.01
ADAM_STEP = 10

SPLITS = (RET_W, RET_W, RET_W, RET_W, SSD_W, SSD_CD, SSD_HEADS, D_MODEL, D_MODEL)
IN_PROJ = sum(SPLITS)
OFF_Z = 4 * RET_W
OFF_XBC = OFF_Z + SSD_W
OFF_DT = OFF_XBC + SSD_CD
OFF_G = OFF_DT + SSD_HEADS

ROW_TILE = 256
CONV_TILE = 128
MM_TILE = 1024
MM_TILE_WIDE = 2048
MM_TK = 2048
VMEM_CAP = 60 << 20

MESH = pl.DeviceIdType.MESH


def _dot(a, b):
    return lax.dot_general(a, b, (((1,), (0,)), ((), ())), preferred_element_type=F32)


def _dot_nt(a, b):
    return lax.dot_general(a, b, (((1,), (1,)), ((), ())), preferred_element_type=F32)


def _dot_tn(a, b):
    return lax.dot_general(a, b, (((0,), (0,)), ((), ())), preferred_element_type=F32)


def _bf(x):
    return x.astype(BF16)


def _split3(x):
    hi = x.astype(BF16)
    r = x - hi.astype(F32)
    mid = r.astype(BF16)
    lo = (r - mid.astype(F32)).astype(BF16)
    return hi, mid, lo


def _dot_exact_l(x, sel, pieces=3):
    hi, mid, lo = _split3(x)
    r = _dot(hi, sel) + _dot(mid, sel)
    return r + _dot(lo, sel) if pieces == 3 else r


def _dot_exact_r(sel, x):
    hi, mid, lo = _split3(x)
    return _dot(sel, hi) + _dot(sel, mid) + _dot(sel, lo)


def _sigmoid(x):
    return 1.0 / (1.0 + jnp.exp(-x))


def _softplus(x):
    return jnp.maximum(x, 0.0) + jnp.log(1.0 + jnp.exp(-jnp.abs(x)))


def _iota(shape, axis):
    return lax.broadcasted_iota(jnp.int32, shape, axis)


def _params(*sem):
    return pltpu.CompilerParams(dimension_semantics=sem)


class _Rider:
    def __init__(self, inputs, out_shapes, n_sems, phases):
        self.inputs, self.out_shapes, self.n_sems, self.phases = tuple(inputs), tuple(out_shapes), n_sems, phases

    def in_specs(self):
        return [pl.BlockSpec(memory_space=pl.ANY)] * len(self.inputs)

    def out_specs(self):
        return [pl.BlockSpec(memory_space=pl.ANY)] * len(self.out_shapes)

    def scratch(self):
        return [pltpu.SemaphoreType.DMA((self.n_sems,)), pltpu.SemaphoreType.DMA((self.n_sems,))]

    def run(self, step, n_steps, ins, outs, send_sems, recv_sems, last):
        for frac, fn in self.phases:
            if (frac >= 1.0) != last:
                continue
            at = min(int(frac * n_steps), n_steps - 1)

            @pl.when(step == at)
            def _(fn=fn):
                fn(ins, outs, send_sems, recv_sems)


def _mesh_pos():
    return lax.axis_index("x"), lax.axis_index("y"), lax.axis_index("c")


def _gather_rider(shards, relay_at, pass_at):
    n = len(shards)

    def tools(a, ins, outs, send_sems, recv_sems):
        x, y, cc = _mesh_pos()
        nbrs = [(1 - x, y), (x, 1 - y)]
        diag = (1 - x, 1 - y)
        relay_from, relay_to = (x ^ cc, y ^ (1 - cc)), (x ^ (1 - cc), y ^ cc)

        def slot(px, py, pc):
            return outs[a].at[4 * px + 2 * py + pc]

        def copy(k, block, to, src=None):
            return pltpu.make_async_remote_copy(
                src_ref=slot(*block) if src is None else src, dst_ref=slot(*block),
                send_sem=send_sems.at[8 * a + k], recv_sem=recv_sems.at[8 * a + k], device_id=to, device_id_type=MESH)

        me, sibling = (x, y, cc), (x, y, 1 - cc)
        return dict(
            mine=lambda: pltpu.make_async_copy(ins[a], slot(*me), send_sems.at[8 * a + 7]),
            first=lambda: [copy(0, me, sibling, src=ins[a])] + [copy(1 + j, me, (*chip, cc), src=ins[a])
                                                                for j, chip in enumerate(nbrs)],
            landed=lambda: [copy(1 + j, (*chip, cc), me) for j, chip in enumerate(nbrs)],
            relay=lambda: copy(3, (*relay_from, cc), (*relay_to, cc)),
            relayed_in=lambda: copy(3, (*diag, cc), me),
            passed=lambda: [copy(4 + j, (*chip, cc), sibling) for j, chip in enumerate(nbrs)],
            passed_diag=lambda: copy(6, (*diag, cc), sibling),
            from_sibling=lambda: [copy(0, sibling, me)] + [copy(4 + j, (*chip, 1 - cc), me)
                                                           for j, chip in enumerate(nbrs + [diag])])

    def start(*refs):
        for a in range(n):
            t = tools(a, *refs)
            t["mine"]().start()
            for cp in t["first"]():
                cp.start()

    def relay(*refs):
        for a in range(n):
            t = tools(a, *refs)
            for got in t["landed"]():
                got.wait_recv()
            t["relay"]().start()
            for cp in t["passed"]():
                cp.start()

    def pass_diag(*refs):
        for a in range(n):
            t = tools(a, *refs)
            t["relayed_in"]().wait_recv()
            t["passed_diag"]().start()

    def finish(*refs):
        for a in range(n):
            t = tools(a, *refs)
            for cp in t["from_sibling"]():
                cp.wait_recv()
            for cp in t["first"]() + [t["relay"]()] + t["passed"]() + [t["passed_diag"]()]:
                cp.wait_send()
            t["mine"]().wait()

    outs = [jax.ShapeDtypeStruct((N_DEV,) + s.shape, s.dtype) for s in shards]
    return _Rider(shards, outs, 8 * n, ((0.0, start), (relay_at, relay), (pass_at, pass_diag), (1.0, finish)))


def _scatter_rider(blocks):
    n = len(blocks)

    def copies(a, ins, outs, send_sems, recv_sems, with_back):
        x, y, cc = _mesh_pos()
        my_idx = 4 * x + 2 * y + cc
        mine = pltpu.make_async_copy(ins[a].at[my_idx], outs[a].at[my_idx], send_sems.at[8 * a + 7])
        out, back = [], []
        for k in range(1, N_DEV):
            px, py, pc = x ^ (k >> 2), y ^ ((k >> 1) & 1), cc ^ (k & 1)
            peer_idx = 4 * px + 2 * py + pc
            sems = dict(send_sem=send_sems.at[8 * a + k - 1], recv_sem=recv_sems.at[8 * a + k - 1],
                        device_id=(px, py, pc), device_id_type=MESH)
            out.append(pltpu.make_async_remote_copy(src_ref=ins[a].at[peer_idx], dst_ref=outs[a].at[my_idx], **sems))
            if with_back:
                back.append(pltpu.make_async_remote_copy(src_ref=ins[a].at[my_idx], dst_ref=outs[a].at[peer_idx], **sems))
        return mine, out, back

    def start(*refs):
        for a in range(n):
            mine, out, _ = copies(a, *refs, False)
            mine.start()
            for cp in out:
                cp.start()

    def finish(*refs):
        for a in range(n):
            mine, out, back = copies(a, *refs, True)
            for cp in back:
                cp.wait_recv()
            for cp in out:
                cp.wait_send()
            mine.wait()

    outs = [jax.ShapeDtypeStruct(b.shape, b.dtype) for b in blocks]
    return _Rider(blocks, outs, 8 * n, ((0.0, start), (1.0, finish)))


def _sibling_rider(blocks):
    _, r, c = blocks.shape

    def copies(ins, outs, send_sems, recv_sems):
        x, y, cc = _mesh_pos()
        return [pltpu.make_async_remote_copy(
            src_ref=ins[0].at[2 * j + 1 - cc], dst_ref=outs[0].at[j], send_sem=send_sems.at[j],
            recv_sem=recv_sems.at[j], device_id=(x, y, 1 - cc), device_id_type=MESH) for j in range(4)]

    def start(*refs):
        for cp in copies(*refs):
            cp.start()

    def finish(*refs):
        for cp in copies(*refs):
            cp.wait_recv()
        for cp in copies(*refs):
            cp.wait_send()

    return _Rider([blocks], [jax.ShapeDtypeStruct((4, r, c), blocks.dtype)], 4, ((0.0, start), (1.0, finish)))


def _chip_rider(partial, row0, rows):
    _, _, c = partial.shape

    def copies(ins, outs, send_sems, recv_sems, with_back):
        x, y, cc = _mesh_pos()
        my_chip = 2 * x + y

        def rows_of(j):
            return ins[0].at[j, pl.ds(row0, rows)]

        mine = pltpu.make_async_copy(rows_of(my_chip), outs[0].at[my_chip], send_sems.at[3])
        out, back = [], []
        for k in range(1, 4):
            px, py = x ^ (k >> 1), y ^ (k & 1)
            peer = 2 * px + py
            sems = dict(send_sem=send_sems.at[k - 1], recv_sem=recv_sems.at[k - 1], device_id=(px, py, cc),
                        device_id_type=MESH)
            out.append(pltpu.make_async_remote_copy(src_ref=rows_of(peer), dst_ref=outs[0].at[my_chip], **sems))
            if with_back:
                back.append(pltpu.make_async_remote_copy(src_ref=rows_of(my_chip), dst_ref=outs[0].at[peer], **sems))
        return mine, out, back

    def start(*refs):
        mine, out, _ = copies(*refs, False)
        mine.start()
        for cp in out:
            cp.start()

    def finish(*refs):
        mine, out, back = copies(*refs, True)
        for cp in back:
            cp.wait_recv()
        for cp in out:
            cp.wait_send()
        mine.wait()

    return _Rider([partial], [jax.ShapeDtypeStruct((4, rows, c), partial.dtype)], 4, ((0.0, start), (1.0, finish)))


def _matmul(a, b, mode, out_dtype, name, add=None, rider=None, b_rows=None):
    b_shape = b.shape if b_rows is None else (b_rows[1], b.shape[1])
    if mode == "nn":
        (m, k), (k2, n) = a.shape, b_shape
    elif mode == "nt":
        (m, k), (n, k2) = a.shape, b_shape
    else:
        (k, m), (k2, n) = a.shape, b_shape
    assert k == k2, (a.shape, b_shape, mode)
    tm, tn, tk = min(m, MM_TILE), min(n, MM_TILE), min(k, MM_TK)
    if n % MM_TILE_WIDE == 0 and (k <= MM_TK or (add is None and jnp.dtype(out_dtype).itemsize == 2)):
        tn = MM_TILE_WIDE
    assert m % tm == 0 and n % tn == 0 and k % tk == 0, (m, n, k)
    nk = k // tk
    b_tile_rows = tn if mode == "nt" else tk
    b_off = 0 if b_rows is None else b_rows[0] // b_tile_rows
    assert b_rows is None or b_rows[0] % b_tile_rows == 0, (b_rows, b_tile_rows)
    dot = {"nn": _dot, "nt": _dot_nt, "tn": _dot_tn}[mode]

    gm, gn = m // tm, n // tn
    n_rin = len(rider.inputs) if rider else 0
    n_rout = len(rider.out_shapes) if rider else 0
    n_add = 1 if add is not None else 0

    def body(*refs):
        a_ref, b_ref = refs[:2]
        add_ref = refs[2] if n_add else None
        r_ins = refs[2 + n_add:2 + n_add + n_rin]
        o_ref = refs[2 + n_add + n_rin]
        r_outs = refs[3 + n_add + n_rin:3 + n_add + n_rin + n_rout]
        scratch = refs[3 + n_add + n_rin + n_rout:]
        acc_ref = scratch[0] if nk > 1 else None
        kk = pl.program_id(2)
        if rider:
            step = (pl.program_id(0) * gn + pl.program_id(1)) * nk + kk
            ride = functools.partial(rider.run, step, gm * gn * nk, r_ins, r_outs, scratch[-2], scratch[-1])
            ride(last=False)

        def finish(r):
            if add_ref is not None:
                r = r + add_ref[...]
            o_ref[...] = r.astype(o_ref.dtype)

        if nk == 1:
            finish(dot(_bf(a_ref[...]), _bf(b_ref[...])))
        else:
            @pl.when(kk == 0)
            def _():
                acc_ref[...] = dot(_bf(a_ref[...]), _bf(b_ref[...]))

            @pl.when(jnp.logical_and(kk > 0, kk < nk - 1))
            def _():
                acc_ref[...] += dot(_bf(a_ref[...]), _bf(b_ref[...]))

            @pl.when(kk == nk - 1)
            def _():
                finish(acc_ref[...] + dot(_bf(a_ref[...]), _bf(b_ref[...])))
        if rider:
            ride(last=True)

    if mode == "nn":
        a_spec = pl.BlockSpec((tm, tk), lambda i, j, kk: (i, kk))
        b_spec = pl.BlockSpec((tk, tn), lambda i, j, kk: (kk + b_off, j))
    elif mode == "nt":
        a_spec = pl.BlockSpec((tm, tk), lambda i, j, kk: (i, kk))
        b_spec = pl.BlockSpec((tn, tk), lambda i, j, kk: (j + b_off, kk))
    else:
        a_spec = pl.BlockSpec((tk, tm), lambda i, j, kk: (kk, i))
        b_spec = pl.BlockSpec((tk, tn), lambda i, j, kk: (kk + b_off, j))
    o_spec = pl.BlockSpec((tm, tn), lambda i, j, kk: (i, j))
    in_specs = [a_spec, b_spec] + ([o_spec] if add is not None else [])
    args = (a, b) + ((add,) if add is not None else ())
    block_bytes = (tm * tk * a.dtype.itemsize + tk * tn * b.dtype.itemsize
                   + tm * tn * (jnp.dtype(out_dtype).itemsize + (4 if add is not None else 0)))
    vmem = min(2 * block_bytes + 2 * tm * tn * 4 + 2 * (tm + tn) * tk + (4 << 20), VMEM_CAP)
    out_shape = jax.ShapeDtypeStruct((m, n), out_dtype)
    scratch = [pltpu.VMEM((tm, tn), F32)] if nk > 1 else []
    if rider:
        in_specs = in_specs + rider.in_specs()
        args = args + rider.inputs
        out_shape = (out_shape, *rider.out_shapes)
        o_spec = (o_spec, *rider.out_specs())
        scratch = scratch + rider.scratch()
    sem = ("arbitrary",) * 3 if rider else ("parallel", "parallel", "arbitrary")
    return pl.pallas_call(
        body,
        out_shape=out_shape,
        grid=(gm, gn, nk),
        in_specs=in_specs,
        out_specs=o_spec,
        scratch_shapes=scratch,
        compiler_params=pltpu.CompilerParams(dimension_semantics=sem, vmem_limit_bytes=int(vmem)),
        name=name,
    )(*args)


def _row_spec(width, tile=ROW_TILE):
    return pl.BlockSpec((tile, width), lambda i: (i, 0))


def _full_spec(shape):
    nd = len(shape)
    return pl.BlockSpec(shape, lambda *_: (0,) * nd)


def _tiling_2d(r, c):
    if r <= ROW_TILE or r % ROW_TILE == 0:
        tr = min(r, ROW_TILE)
        return (tr, c), r // tr, (lambda i: (i, 0))
    tc = 128 if r > 4096 else 256
    assert c % tc == 0, (r, c)
    return (r, tc), c // tc, (lambda i: (0, i))


def _spec_2d(r, c):
    blk, grid, idx = _tiling_2d(r, c)
    return pl.BlockSpec(blk, idx), grid


def _cast_bf16(x, name):
    r, c = x.shape
    sp, grid = _spec_2d(r, c)

    def body(x_ref, o_ref):
        o_ref[...] = _bf(x_ref[...])

    return pl.pallas_call(
        body, out_shape=jax.ShapeDtypeStruct((r, c), BF16), grid=(grid,), in_specs=[sp], out_specs=sp,
        compiler_params=_params("parallel"), name=name)(x)


def _rmsnorm_fwd(x, w, name, rider):
    s, d = x.shape
    steps = s // ROW_TILE
    n_rin, n_rout = len(rider.inputs), len(rider.out_shapes)

    def body(x_ref, w_ref, *refs):
        r_ins, o_ref, r_outs = refs[:n_rin], refs[n_rin], refs[n_rin + 1:n_rin + 1 + n_rout]
        ride = functools.partial(rider.run, pl.program_id(0), steps, r_ins, r_outs, refs[-2], refs[-1])
        ride(last=False)
        xv = x_ref[...]
        rstd = lax.rsqrt(jnp.mean(xv * xv, axis=1, keepdims=True) + EPS)
        o_ref[...] = _bf(xv * rstd * w_ref[...])
        ride(last=True)

    return pl.pallas_call(
        body, out_shape=(jax.ShapeDtypeStruct((s, d), BF16), *rider.out_shapes), grid=(steps,),
        in_specs=[_row_spec(d), _full_spec((1, d)), *rider.in_specs()],
        out_specs=(_row_spec(d), *rider.out_specs()), scratch_shapes=rider.scratch(),
        compiler_params=_params("arbitrary"), name=name)(x, w, *rider.inputs)


def _rmsnorm_bwd(x, w, dh, dres, name):
    s, d = x.shape

    def body(x_ref, w_ref, dh_ref, dres_ref, dx_ref, dw_ref):
        @pl.when(pl.program_id(0) == 0)
        def _():
            dw_ref[...] = jnp.zeros_like(dw_ref)

        xv = x_ref[...]
        rstd = lax.rsqrt(jnp.mean(xv * xv, axis=1, keepdims=True) + EPS)
        xhat = xv * rstd
        dhv = dh_ref[...]
        dxhat = dhv * w_ref[...]
        dx = rstd * (dxhat - xhat * jnp.mean(dxhat * xhat, axis=1, keepdims=True))
        dx_ref[...] = dx + dres_ref[...]
        dw_ref[...] += jnp.sum(dhv * xhat, axis=0, keepdims=True)

    return pl.pallas_call(
        body,
        out_shape=(jax.ShapeDtypeStruct((s, d), F32), jax.ShapeDtypeStruct((1, d), F32)),
        grid=(s // ROW_TILE,),
        in_specs=[_row_spec(d), _full_spec((1, d)), _row_spec(d), _row_spec(d)],
        out_specs=(_row_spec(d), _full_spec((1, d))),
        compiler_params=_params("arbitrary"), name=name)(x, w, dh, dres)


def _rope_tables(pos_col, inv_freq, name):
    s = pos_col.shape[0]

    def body(p_ref, f_ref, cos_ref, sin_ref):
        ang = p_ref[...].astype(F32) * f_ref[...]
        cos_ref[...] = jnp.cos(ang)
        sin_ref[...] = jnp.sin(ang)

    out = jax.ShapeDtypeStruct((s, ROPE_HALF), F32)
    return pl.pallas_call(
        body, out_shape=(out, out), grid=(s // ROW_TILE,),
        in_specs=[_row_spec(1), _full_spec((1, ROPE_HALF))],
        out_specs=(_row_spec(ROPE_HALF), _row_spec(ROPE_HALF)),
        compiler_params=_params("parallel"), name=name)(pos_col, inv_freq)


def _merge_fwd(pg, p_r, p_s, name):
    s = pg.shape[0]

    def body(g_ref, r_ref, s_ref, o_ref):
        g = g_ref[...].astype(F32)
        o_ref[...] = _bf(_sigmoid(g[:, :D_MODEL]) * r_ref[...].astype(F32)
                         + _sigmoid(g[:, D_MODEL:]) * s_ref[...].astype(F32))

    return pl.pallas_call(
        body, out_shape=jax.ShapeDtypeStruct((s, D_MODEL), BF16), grid=(s // ROW_TILE,),
        in_specs=[_row_spec(2 * D_MODEL), _row_spec(D_MODEL), _row_spec(D_MODEL)],
        out_specs=_row_spec(D_MODEL), compiler_params=_params("parallel"), name=name)(pg, p_r, p_s)


def _merge_bwd(pg, p_r, p_s, dm, name):
    s = pg.shape[0]

    def body(g_ref, r_ref, s_ref, dm_ref, dr_ref, ds_ref, dg_ref):
        g = g_ref[...].astype(F32)
        sr, ss = _sigmoid(g[:, :D_MODEL]), _sigmoid(g[:, D_MODEL:])
        d = dm_ref[...]
        dr_ref[...] = _bf(d * sr)
        ds_ref[...] = _bf(d * ss)
        dg_ref[:, :D_MODEL] = _bf(d * r_ref[...].astype(F32) * sr * (1.0 - sr))
        dg_ref[:, D_MODEL:] = _bf(d * s_ref[...].astype(F32) * ss * (1.0 - ss))

    o = jax.ShapeDtypeStruct((s, D_MODEL), BF16)
    return pl.pallas_call(
        body, out_shape=(o, o, jax.ShapeDtypeStruct((s, 2 * D_MODEL), BF16)), grid=(s // ROW_TILE,),
        in_specs=[_row_spec(2 * D_MODEL), _row_spec(D_MODEL), _row_spec(D_MODEL), _row_spec(D_MODEL)],
        out_specs=(_row_spec(D_MODEL), _row_spec(D_MODEL), _row_spec(2 * D_MODEL)),
        compiler_params=_params("parallel"), name=name)(pg, p_r, p_s, dm)


def _final_fwd_bwd(x, o, w, target, name):
    s, d = x.shape

    def body(x_ref, o_ref, w_ref, t_ref, dx_ref, dw_ref, loss_ref):
        @pl.when(pl.program_id(0) == 0)
        def _():
            dw_ref[...] = jnp.zeros_like(dw_ref)
            loss_ref[...] = jnp.zeros_like(loss_ref)

        x2 = x_ref[...] + o_ref[...]
        rstd = lax.rsqrt(jnp.mean(x2 * x2, axis=1, keepdims=True) + EPS)
        xhat = x2 * rstd
        wv = w_ref[...]
        err = xhat * wv - t_ref[...]
        loss_ref[...] += jnp.sum(jnp.sum(err * err, axis=1, keepdims=True), axis=0, keepdims=True) * (0.5 / d)
        dy = err * (1.0 / d)
        dw_ref[...] += jnp.sum(dy * xhat, axis=0, keepdims=True)
        dxhat = dy * wv
        dx_ref[...] = rstd * (dxhat - xhat * jnp.mean(dxhat * xhat, axis=1, keepdims=True))

    return pl.pallas_call(
        body,
        out_shape=(jax.ShapeDtypeStruct((s, d), F32), jax.ShapeDtypeStruct((1, d), F32),
                   jax.ShapeDtypeStruct((8, 128), F32)),
        grid=(s // ROW_TILE,),
        in_specs=[_row_spec(d), _row_spec(d), _full_spec((1, d)), _row_spec(d)],
        out_specs=(_row_spec(d), _full_spec((1, d)), _full_spec((8, 128))),
        compiler_params=_params("arbitrary"), name=name)(x, o, w, target)


def _adamw(w, g, m, v, name):
    r, c = w.shape
    sp, grid = _spec_2d(r, c)
    c1 = 1.0 / (1.0 - ADAM_B1 ** ADAM_STEP)
    c2 = 1.0 / (1.0 - ADAM_B2 ** ADAM_STEP)

    def body(w_ref, g_ref, m_ref, v_ref, d_ref, nm_ref, nv_ref):
        gv = g_ref[...]
        nm = ADAM_B1 * m_ref[...] + (1.0 - ADAM_B1) * gv
        nv = ADAM_B2 * v_ref[...] + (1.0 - ADAM_B2) * (gv * gv)
        d_ref[...] = -ADAM_LR * ((nm * c1) / (jnp.sqrt(nv * c2) + ADAM_EPS) + ADAM_WD * w_ref[...])
        nm_ref[...] = nm
        nv_ref[...] = nv

    o = jax.ShapeDtypeStruct((r, c), F32)
    return pl.pallas_call(
        body, out_shape=(o, o, o), grid=(grid,), in_specs=[sp, sp, sp, sp], out_specs=(sp, sp, sp),
        compiler_params=_params("parallel"), name=name)(w, g, m, v)


def _sum_slots(land, name):
    n, r, c = land.shape
    blk, grid, idx = _tiling_2d(r, c)

    def body(l_ref, o_ref):
        acc = l_ref[0].astype(F32)
        for i in range(1, n):
            acc = acc + l_ref[i].astype(F32)
        o_ref[...] = acc

    return pl.pallas_call(
        body, out_shape=jax.ShapeDtypeStruct((r, c), F32), grid=(grid,),
        in_specs=[pl.BlockSpec((n, *blk), lambda i: (0, *idx(i)))], out_specs=pl.BlockSpec(blk, idx),
        compiler_params=_params("parallel"), name=name)(land)


SLAB_COLS = 256


def _chip_sum(blocks, from_sibling, my_core, name):
    _, r, c = blocks.shape

    def body(core_ref, a_ref, b_ref, o_ref):
        o_ref[...] = _bf(a_ref[...].astype(F32) + b_ref[...].astype(F32))

    def slab(chip_of):
        return pl.BlockSpec((1, r, SLAB_COLS), lambda j, i, core: (chip_of(j, core), 0, i))

    grid_spec = pltpu.PrefetchScalarGridSpec(
        num_scalar_prefetch=1, grid=(4, c // SLAB_COLS),
        in_specs=[slab(lambda j, core: 2 * j + core[0]), slab(lambda j, core: j)],
        out_specs=slab(lambda j, core: j))
    return pl.pallas_call(
        body, out_shape=jax.ShapeDtypeStruct((4, r, c), BF16), grid_spec=grid_spec,
        compiler_params=_params("parallel", "parallel"), name=name,
    )(my_core.astype(jnp.int32).reshape(1), blocks, from_sibling)


def _sum_row_parts(parts, name):
    c = parts[0].shape[2]
    rows = [p.shape[1] for p in parts]

    def body(*refs):
        o_ref, off = refs[-1], 0
        for p_ref, n in zip(refs[:-1], rows):
            acc = p_ref[0].astype(F32)
            for i in range(1, p_ref.shape[0]):
                acc = acc + p_ref[i].astype(F32)
            o_ref[off:off + n, :] = acc
            off += n

    return pl.pallas_call(
        body, out_shape=jax.ShapeDtypeStruct((sum(rows), c), F32), grid=(c // SLAB_COLS,),
        in_specs=[pl.BlockSpec((p.shape[0], p.shape[1], SLAB_COLS), lambda i: (0, 0, i)) for p in parts],
        out_specs=pl.BlockSpec((sum(rows), SLAB_COLS), lambda i: (0, i)),
        compiler_params=_params("parallel"), name=name)(*parts)


def _retention_tables():
    lg = np.log1p(-(2.0 ** (-5.0 - np.arange(RET_HEADS, dtype=np.float64))))
    idx = np.arange(CHUNK, dtype=np.float64)
    intra = np.exp(np.abs(idx[:, None] - idx[None, :])[None] * lg[:, None, None])
    qd = np.exp((idx[None, :] + 1.0) * lg[:, None])
    kd = np.exp((CHUNK - 1.0 - idx[None, :]) * lg[:, None])
    cd = np.exp(CHUNK * lg)
    ones = np.ones((1, 1, RET_DK))
    return (jnp.asarray(intra, F32), jnp.asarray(qd[:, :, None] * ones, F32),
            jnp.asarray(kd[:, :, None] * ones, F32), jnp.asarray(cd[:, None, None] * ones, F32))


def _rope(t, cos, sin):
    t1, t2 = t[:, :ROPE_HALF], t[:, ROPE_HALF:]
    return jnp.concatenate([t1 * cos - t2 * sin, t2 * cos + t1 * sin], axis=1)


def _rope_t(d, cos, sin):
    d1, d2 = d[:, :ROPE_HALF], d[:, ROPE_HALF:]
    return jnp.concatenate([d1 * cos + d2 * sin, d2 * cos - d1 * sin], axis=1)


_HEADS = range(RET_HEADS)


def _ret_chunk_fwd(p_ref, cos, sin, intra_ref, qd_ref, st):
    def seg(i, h):
        return p_ref[:, i * RET_W + h * RET_DK:i * RET_W + (h + 1) * RET_DK]

    v = [seg(2, h) for h in _HEADS]
    qr = [_rope(seg(0, h), cos, sin) for h in _HEADS]
    kr = [_rope(seg(1, h), cos, sin) * (RET_DK ** -0.5) for h in _HEADS]
    qrb, vb = [_bf(a) for a in qr], [_bf(a) for a in v]
    sc = [_dot_nt(qrb[h], _bf(kr[h])) * intra_ref[h] for h in _HEADS]
    qs = [_dot(qrb[h], _bf(st[h])) for h in _HEADS]
    y = [_dot(_bf(sc[h]), vb[h]) + qs[h] * qd_ref[h] for h in _HEADS]
    g = [seg(3, h) for h in _HEADS]
    return dict(v=v, vb=vb, qr=qr, qrb=qrb, kr=kr, sc=sc, y=y, g=g)


def _group_norm(y):
    mu = jnp.mean(y, axis=1, keepdims=True)
    yc = y - mu
    rstd = lax.rsqrt(jnp.mean(yc * yc, axis=1, keepdims=True) + EPS)
    return yc * rstd, rstd


def _chunk_rows(sub):
    return pl.ds(sub * CHUNK, CHUNK)


def _ret_specs(steps, rev):
    cidx = (lambda c: steps - 1 - c) if rev else (lambda c: c)
    return dict(
        proj=pl.BlockSpec((STEP_ROWS, 4 * RET_W), lambda c: (cidx(c), 0)),
        half=pl.BlockSpec((STEP_ROWS, ROPE_HALF), lambda c: (cidx(c), 0)),
        wide=pl.BlockSpec((STEP_ROWS, RET_W), lambda c: (cidx(c), 0)),
        state=pl.BlockSpec((CHUNKS_PER_STEP, RET_HEADS, RET_DK, RET_DK), lambda c: (cidx(c), 0, 0, 0)),
        intra=_full_spec((RET_HEADS, CHUNK, CHUNK)),
        dec=_full_spec((RET_HEADS, CHUNK, RET_DK)),
        cd=_full_spec((RET_HEADS, 1, RET_DK)),
    )


def _retention_fwd(p_r, cos, sin, name):
    s = p_r.shape[0]
    nc = s // CHUNK
    intra_t, qd_t, kd_t, cd_t = _retention_tables()

    def body(p_ref, cos_ref, sin_ref, intra_ref, qd_ref, kd_ref, cd_ref, y_ref, st_ref, state):
        @pl.when(pl.program_id(0) == 0)
        def _():
            state[...] = jnp.zeros_like(state)

        for sub in range(CHUNKS_PER_STEP):
            rows = _chunk_rows(sub)
            y_v, st_v = y_ref.at[rows], st_ref.at[sub]
            st = [state[h] for h in _HEADS]
            f = _ret_chunk_fwd(p_ref.at[rows], cos_ref[rows, :], sin_ref[rows, :], intra_ref, qd_ref, st)
            new_st = [st[h] * cd_ref[h] + _dot_tn(_bf(f["kr"][h] * kd_ref[h]), f["vb"][h]) for h in _HEADS]
            out = [_bf(_group_norm(f["y"][h])[0] * (f["g"][h] * _sigmoid(f["g"][h]))) for h in _HEADS]
            for h in _HEADS:
                st_v[h] = _bf(st[h])
                state[h] = new_st[h]
                y_v[:, h * RET_DK:(h + 1) * RET_DK] = out[h]

    steps = nc // CHUNKS_PER_STEP
    sp = _ret_specs(steps, False)
    return pl.pallas_call(
        body,
        out_shape=(jax.ShapeDtypeStruct((s, RET_W), BF16),
                   jax.ShapeDtypeStruct((nc, RET_HEADS, RET_DK, RET_DK), BF16)),
        grid=(steps,),
        in_specs=[sp["proj"], sp["half"], sp["half"], sp["intra"], sp["dec"], sp["dec"], sp["cd"]],
        out_specs=(sp["wide"], sp["state"]),
        scratch_shapes=[pltpu.VMEM((RET_HEADS, RET_DK, RET_DK), F32)],
        compiler_params=_params("arbitrary"), name=name)(p_r, cos, sin, intra_t, qd_t, kd_t, cd_t)


def _retention_bwd(p_r, cos, sin, states, dy_r, name):
    s = p_r.shape[0]
    nc = s // CHUNK
    intra_t, qd_t, kd_t, cd_t = _retention_tables()

    def body(p_ref, cos_ref, sin_ref, intra_ref, qd_ref, kd_ref, cd_ref, st_ref, dy_ref, dp_ref, dstate):
        @pl.when(pl.program_id(0) == 0)
        def _():
            dstate[...] = jnp.zeros_like(dstate)

        for sub in reversed(range(CHUNKS_PER_STEP)):
            rows = _chunk_rows(sub)
            chunk(p_ref.at[rows], cos_ref[rows, :], sin_ref[rows, :], intra_ref, qd_ref, kd_ref, cd_ref,
                  st_ref.at[sub], dy_ref.at[rows], dp_ref.at[rows], dstate)

    def chunk(p_ref, cos, sin, intra_ref, qd_ref, kd_ref, cd_ref, st_ref, dy_ref, dp_ref, dstate):
        st = [st_ref[h] for h in _HEADS]
        dsn = [dstate[h] for h in _HEADS]
        f = _ret_chunk_fwd(p_ref, cos, sin, intra_ref, qd_ref, st)
        vb, qrb, kr, g = f["vb"], f["qrb"], f["kr"], f["g"]
        norm = [_group_norm(f["y"][h]) for h in _HEADS]
        sg = [_sigmoid(g[h]) for h in _HEADS]
        dyr = [dy_ref[:, h * RET_DK:(h + 1) * RET_DK] for h in _HEADS]
        dyn = [dyr[h] * (g[h] * sg[h]) for h in _HEADS]
        dg = [dyr[h] * norm[h][0] * (sg[h] * (1.0 + g[h] * (1.0 - sg[h]))) for h in _HEADS]
        dy = [norm[h][1] * (dyn[h] - jnp.mean(dyn[h], axis=1, keepdims=True)
                            - norm[h][0] * jnp.mean(dyn[h] * norm[h][0], axis=1, keepdims=True)) for h in _HEADS]
        dyb, dsnb = [_bf(a) for a in dy], [_bf(a) for a in dsn]
        ds = [_bf(_dot_nt(dyb[h], vb[h]) * intra_ref[h]) for h in _HEADS]
        t = [_bf(dy[h] * qd_ref[h]) for h in _HEADS]
        dv = [_dot_tn(_bf(f["sc"][h]), dyb[h]) + _dot(_bf(kr[h] * kd_ref[h]), dsnb[h]) for h in _HEADS]
        dqr = [_dot(ds[h], _bf(kr[h])) + _dot_nt(t[h], _bf(st[h])) for h in _HEADS]
        dkr = [_dot_tn(ds[h], qrb[h]) + _dot_nt(vb[h], dsnb[h]) * kd_ref[h] for h in _HEADS]
        new_ds = [dsn[h] * cd_ref[h] + _dot_tn(qrb[h], t[h]) for h in _HEADS]
        for h in _HEADS:
            lo = h * RET_DK
            dstate[h] = new_ds[h]
            dp_ref[:, lo:lo + RET_DK] = _bf(_rope_t(dqr[h], cos, sin))
            dp_ref[:, RET_W + lo:RET_W + lo + RET_DK] = _bf(_rope_t(dkr[h], cos, sin) * (RET_DK ** -0.5))
            dp_ref[:, 2 * RET_W + lo:2 * RET_W + lo + RET_DK] = _bf(dv[h])
            dp_ref[:, 3 * RET_W + lo:3 * RET_W + lo + RET_DK] = _bf(dg[h])

    steps = nc // CHUNKS_PER_STEP
    sp = _ret_specs(steps, True)
    return pl.pallas_call(
        body,
        out_shape=jax.ShapeDtypeStruct((s, 4 * RET_W), BF16),
        grid=(steps,),
        in_specs=[sp["proj"], sp["half"], sp["half"], sp["intra"], sp["dec"], sp["dec"], sp["cd"],
                  sp["state"], sp["wide"]],
        out_specs=sp["proj"],
        scratch_shapes=[pltpu.VMEM((RET_HEADS, RET_DK, RET_DK), F32)],
        compiler_params=_params("arbitrary"), name=name)(p_r, cos, sin, intra_t, qd_t, kd_t, cd_t, states, dy_r)


CONV_SLAB = 256
_CONV_SLABS = [slice(i * CONV_SLAB, (i + 1) * CONV_SLAB) for i in range(SSD_CD // CONV_SLAB)]


def _conv_taps(ext, w):
    acc = w[SSD_CONV - 1:SSD_CONV] * ext
    for j in range(SSD_CONV - 1):
        acc = acc + w[j:j + 1] * pltpu.roll(ext, SSD_CONV - 1 - j, axis=0)
    return acc


def _conv_fwd(xbc_raw, conv_w, conv_b, name):
    s = xbc_raw.shape[0]
    t8 = CONV_TILE // 8

    def body(cur_ref, prev_ref, w_ref, b_ref, o_ref, slope_ref):
        keep = (pl.program_id(0) > 0).astype(F32)
        for sl in _CONV_SLABS:
            ext = jnp.concatenate([prev_ref[:, sl] * keep, cur_ref[:, sl]], axis=0)
            u = _conv_taps(ext, w_ref[:, sl])[8:] + b_ref[:, sl]
            sg = _sigmoid(u)
            o_ref[:, sl] = u * sg
            slope_ref[:, sl] = sg * (1.0 + u * (1.0 - sg))

    out = jax.ShapeDtypeStruct((s, SSD_CD), F32)
    return pl.pallas_call(
        body, out_shape=(out, out), grid=(s // CONV_TILE,),
        in_specs=[_row_spec(SSD_CD, CONV_TILE),
                  pl.BlockSpec((8, SSD_CD), lambda i: (jnp.maximum(i * t8 - 1, 0), 0)),
                  _full_spec((SSD_CONV, SSD_CD)), _full_spec((1, SSD_CD))],
        out_specs=(_row_spec(SSD_CD, CONV_TILE), _row_spec(SSD_CD, CONV_TILE)),
        compiler_params=_params("parallel"), name=name)(xbc_raw, xbc_raw, conv_w, conv_b)


def _conv_bwd(xbc_raw, conv_w, slope, dact, name):
    s = xbc_raw.shape[0]
    nt = s // CONV_TILE
    t8 = CONV_TILE // 8
    rows = CONV_TILE + 8

    def body(cur_ref, prev_ref, w_ref, s_ref, snext_ref, d_ref, dnext_ref, dx_ref, dw_ref, db_ref):
        i = pl.program_id(0)

        @pl.when(i == 0)
        def _():
            dw_ref[...] = jnp.zeros_like(dw_ref)
            db_ref[...] = jnp.zeros_like(db_ref)

        keep_prev = (i > 0).astype(F32)
        keep_next = (i < nt - 1).astype(F32)
        for sl in _CONV_SLABS:
            w = w_ref[:, sl]
            cur = cur_ref[:, sl]
            ext = jnp.concatenate([prev_ref[:, sl] * keep_prev, cur], axis=0)
            duc = d_ref[:, sl] * s_ref[:, sl]
            du = jnp.concatenate([duc, dnext_ref[:, sl] * snext_ref[:, sl] * keep_next], axis=0)
            dx = w[SSD_CONV - 1:SSD_CONV] * du
            for j in range(SSD_CONV - 1):
                dx = dx + w[j:j + 1] * pltpu.roll(du, rows - (SSD_CONV - 1 - j), axis=0)
            dx_ref[:, sl] = _bf(dx[:CONV_TILE])
            for j in range(SSD_CONV - 1):
                shifted = pltpu.roll(ext, SSD_CONV - 1 - j, axis=0)[8:]
                dw_ref[j:j + 1, sl] += jnp.sum(duc * shifted, axis=0, keepdims=True)
            dw_ref[SSD_CONV - 1:SSD_CONV, sl] += jnp.sum(duc * cur, axis=0, keepdims=True)
            db_ref[:, sl] += jnp.sum(duc, axis=0, keepdims=True)

    row = _row_spec(SSD_CD, CONV_TILE)
    prev = pl.BlockSpec((8, SSD_CD), lambda i: (jnp.maximum(i * t8 - 1, 0), 0))
    nxt = pl.BlockSpec((8, SSD_CD), lambda i: (jnp.minimum((i + 1) * t8, s // 8 - 1), 0))
    return pl.pallas_call(
        body,
        out_shape=(jax.ShapeDtypeStruct((s, SSD_CD), BF16), jax.ShapeDtypeStruct((SSD_CONV, SSD_CD), F32),
                   jax.ShapeDtypeStruct((1, SSD_CD), F32)),
        grid=(nt,),
        in_specs=[row, prev, _full_spec((SSD_CONV, SSD_CD)), row, nxt, row, nxt],
        out_specs=(row, _full_spec((SSD_CONV, SSD_CD)), _full_spec((1, SSD_CD))),
        compiler_params=_params("arbitrary"), name=name)(xbc_raw, xbc_raw, conv_w, slope, slope, dact, dact)


def _head_select():
    e = np.zeros((HEAD_PAD, SSD_W), np.float32)
    for h in range(SSD_HEADS):
        e[h, h * SSD_P:(h + 1) * SSD_P] = 1.0
    return jnp.asarray(e, BF16), jnp.asarray(e.T, BF16)


def _pad_heads(v):
    return jnp.pad(v.reshape(1, SSD_HEADS).astype(F32), ((0, 0), (0, HEAD_PAD - SSD_HEADS)))


def _ssd_masks():
    r = _iota((CHUNK, SSD_QW), 0)
    c = _iota((CHUNK, SSD_QW), 1) % SSD_P
    itile = (r == c).astype(F32)
    ctile = r >= c
    rb = _iota((SSD_QW, SSD_QW), 0) // SSD_P
    cb = _iota((SSD_QW, SSD_QW), 1) // SSD_P
    return itile, ctile, rb == cb


_TILES = [slice(i * SSD_QW, (i + 1) * SSD_QW) for i in range(SSD_W // SSD_QW)]


def _ssd_heads(dtr_ref, dtb_ref, alog_ref):
    u = dtr_ref[...] + dtb_ref[...]
    dt = _softplus(u)
    nexp = -jnp.exp(alog_ref[...])
    return u, dt, nexp, dt * nexp


def _ssd_group(xbc_ref, acol_ref, dte_ref, gs):
    acol = acol_ref[:, gs]
    alast = acol[CHUNK - 1:CHUNK, :]
    xs, dte = xbc_ref[:, gs], dte_ref[:, gs]
    return dict(xs=xs, dte=dte, xdt=xs * dte, ea=jnp.exp(acol), tail=jnp.exp(alast - acol), eal=jnp.exp(alast))


def _silu_gate(z_ref, y_ref, tl):
    zv = z_ref[:, tl]
    sz = _sigmoid(zv)
    return zv, sz, y_ref[:, tl] * (zv * sz)


def _tile4(x):
    return jnp.concatenate([x, x, x, x], axis=0)


def _fold4(x):
    return x[0:CHUNK] + x[CHUNK:2 * CHUNK] + x[2 * CHUNK:3 * CHUNK] + x[3 * CHUNK:4 * CHUNK]


def _ssd_specs(steps, rev):
    cidx = (lambda c: steps - 1 - c) if rev else (lambda c: c)
    return dict(
        xbc=pl.BlockSpec((STEP_ROWS, SSD_CD), lambda c: (cidx(c), 0)),
        dt=pl.BlockSpec((STEP_ROWS, HEAD_PAD), lambda c: (cidx(c), 0)),
        wide=pl.BlockSpec((STEP_ROWS, SSD_W), lambda c: (cidx(c), 0)),
        state=pl.BlockSpec((CHUNKS_PER_STEP, SSD_N, SSD_W), lambda c: (cidx(c), 0, 0)),
        head=_full_spec((1, HEAD_PAD)),
        roww=_full_spec((1, SSD_W)),
        e=_full_spec((HEAD_PAD, SSD_W)),
        et=_full_spec((SSD_W, HEAD_PAD)),
    )


_GROUPS = range(SSD_GROUPS)
_GROUP_LANES = [slice(g * SSD_GW, (g + 1) * SSD_GW) for g in _GROUPS]
_QUADS = [(g, slice(g * SSD_GW + q * SSD_QW, g * SSD_GW + (q + 1) * SSD_QW), slice(q * SSD_QW, (q + 1) * SSD_QW))
          for g in _GROUPS for q in range(SSD_GW // SSD_QW)]


def _ssd_bc(xbc_ref):
    bg = [_bf(xbc_ref[:, B_OFF + g * SSD_N:B_OFF + (g + 1) * SSD_N]) for g in _GROUPS]
    cg = [_bf(xbc_ref[:, C_OFF + g * SSD_N:C_OFF + (g + 1) * SSD_N]) for g in _GROUPS]
    return bg, cg


def _ssd_decay(aq, itile, ctile):
    arow = jnp.sum(aq * itile, axis=0, keepdims=True)
    return jnp.exp(jnp.where(ctile, aq - arow, -jnp.inf))


def _ssd_blockdiag(xq, bdmask):
    return jnp.where(bdmask, _tile4(_bf(xq)), jnp.zeros((), BF16))


def _ssd_fwd(xbc, dt_raw, z, dt_bias, a_log, dske, norm_w, name):
    s = xbc.shape[0]
    nc = s // CHUNK
    e_sel, _ = _head_select()

    def body(xbc_ref, dtr_ref, z_ref, dtb_ref, alog_ref, dske_ref, nw_ref, e_ref,
             yraw_ref, ys_ref, st_ref, acol_ref, dte_ref, state):
        @pl.when(pl.program_id(0) == 0)
        def _():
            state[...] = jnp.zeros_like(state)

        for sub in range(CHUNKS_PER_STEP):
            rows = _chunk_rows(sub)
            chunk(xbc_ref.at[rows], dtr_ref.at[rows], z_ref.at[rows], dtb_ref, alog_ref, dske_ref, nw_ref, e_ref,
                  yraw_ref.at[rows], ys_ref.at[rows], st_ref.at[sub], acol_ref.at[rows], dte_ref.at[rows], state)

    def chunk(xbc_ref, dtr_ref, z_ref, dtb_ref, alog_ref, dske_ref, nw_ref, e_ref,
              yraw_ref, ys_ref, st_ref, acol_ref, dte_ref, state):
        _, dt, _, a = _ssd_heads(dtr_ref, dtb_ref, alog_ref)
        tril = _bf((_iota((CHUNK, CHUNK), 0) >= _iota((CHUNK, CHUNK), 1)).astype(F32))
        ac3, dt3 = _split3(_dot_exact_r(tril, a)), _split3(dt)
        for tl in _TILES:
            e_t = e_ref[:, tl]
            acol_ref[:, tl] = _dot(ac3[0], e_t) + _dot(ac3[1], e_t) + _dot(ac3[2], e_t)
            dte_ref[:, tl] = _dot(dt3[0], e_t) + _dot(dt3[1], e_t) + _dot(dt3[2], e_t)

        itile, ctile, bdmask = _ssd_masks()
        st_ref[...] = state[...]
        bg, cg = _ssd_bc(xbc_ref)
        for g0 in range(0, SSD_GROUPS, SSD_BLOCK_GROUPS):
            gids = range(g0, g0 + SSD_BLOCK_GROUPS)
            quads = [(g, ql, qs) for g, ql, qs in _QUADS if g in gids]
            q = {g: _ssd_group(xbc_ref, acol_ref, dte_ref, _GROUP_LANES[g]) for g in gids}
            stg = {g: state[:, _GROUP_LANES[g]] for g in gids}
            cbt = {g: _dot_nt(cg[g], _tile4(bg[g])) for g in gids}
            ys = {g: _dot(cg[g], _bf(stg[g])) for g in gids}
            dq = [_ssd_decay(acol_ref[:, ql], itile, ctile) for _, ql, _ in quads]
            xbd = [_ssd_blockdiag(q[g]["xdt"][:, qs], bdmask) for g, _, qs in quads]
            yq = [_dot(_bf(cbt[g] * dq[i]), xbd[i]) + ys[g][:, qs] * q[g]["ea"][:, qs]
                  + dske_ref[:, ql] * q[g]["xs"][:, qs] for i, (g, ql, qs) in enumerate(quads)]
            new_st = {g: stg[g] * q[g]["eal"] + _dot_tn(bg[g], _bf(q[g]["xdt"] * q[g]["tail"])) for g in gids}
            for i, (_, ql, _) in enumerate(quads):
                yraw_ref[:, ql] = yq[i]
            for g in gids:
                state[:, _GROUP_LANES[g]] = new_st[g]

        sq = jnp.zeros((CHUNK, SSD_QW), F32)
        for tl in _TILES:
            t = _silu_gate(z_ref, yraw_ref, tl)[2]
            sq = sq + t * t
        rstd = lax.rsqrt(jnp.sum(sq, axis=1, keepdims=True) * (1.0 / SSD_W) + EPS)
        for tl in _TILES:
            ys_ref[:, tl] = _bf(_silu_gate(z_ref, yraw_ref, tl)[2] * rstd * nw_ref[:, tl])

    steps = nc // CHUNKS_PER_STEP
    sp = _ssd_specs(steps, False)
    wide = jax.ShapeDtypeStruct((s, SSD_W), F32)
    return pl.pallas_call(
        body,
        out_shape=(wide, jax.ShapeDtypeStruct((s, SSD_W), BF16), jax.ShapeDtypeStruct((nc, SSD_N, SSD_W), F32),
                   wide, wide),
        grid=(steps,),
        in_specs=[sp["xbc"], sp["dt"], sp["wide"], sp["head"], sp["head"], sp["roww"], sp["roww"], sp["e"]],
        out_specs=(sp["wide"], sp["wide"], sp["state"], sp["wide"], sp["wide"]),
        scratch_shapes=[pltpu.VMEM((SSD_N, SSD_W), F32)],
        compiler_params=_params("arbitrary"), name=name,
    )(xbc, dt_raw, z, dt_bias, a_log, dske, norm_w, e_sel)


def _ssd_bwd(xbc, dt_raw, z, yraw, states, dys, acol, dte, dt_bias, a_log, dske, norm_w, name, rider):
    s = xbc.shape[0]
    nc = s // CHUNK
    steps = nc // CHUNKS_PER_STEP
    _, et_sel = _head_select()
    n_in, n_out, n_scratch = 13, 7, 5
    n_rin, n_rout = len(rider.inputs), len(rider.out_shapes)

    def body(*refs):
        ins, refs = refs[:n_in], refs[n_in:]
        r_ins, refs = refs[:n_rin], refs[n_rin:]
        outs, refs = refs[:n_out], refs[n_out:]
        r_outs, refs = refs[:n_rout], refs[n_rout:]
        ride = functools.partial(rider.run, pl.program_id(0), steps, r_ins, r_outs, refs[n_scratch],
                                 refs[n_scratch + 1])
        ride(last=False)
        compute(*ins, *outs, *refs[:n_scratch])
        ride(last=True)

    def compute(xbc_ref, dtr_ref, z_ref, yraw_ref, st_ref, dys_ref, acol_ref, dte_ref, dtb_ref, alog_ref,
                dske_ref, nw_ref, et_ref, dxbc_ref, dz_ref, ddtr_ref, ddtb_ref, dalog_ref, ddsk_ref, dnw_ref,
                dstate, dsk_acc, dy_s, dacol_s, ddte_s):
        @pl.when(pl.program_id(0) == 0)
        def _():
            dstate[...] = jnp.zeros_like(dstate)
            dsk_acc[...] = jnp.zeros_like(dsk_acc)
            ddtb_ref[...] = jnp.zeros_like(ddtb_ref)
            dalog_ref[...] = jnp.zeros_like(dalog_ref)
            dnw_ref[...] = jnp.zeros_like(dnw_ref)

        for sub in reversed(range(CHUNKS_PER_STEP)):
            rows = _chunk_rows(sub)
            chunk(xbc_ref.at[rows], dtr_ref.at[rows], z_ref.at[rows], yraw_ref.at[rows], st_ref.at[sub],
                  dys_ref.at[rows], acol_ref.at[rows], dte_ref.at[rows], dtb_ref, alog_ref, dske_ref, nw_ref, et_ref,
                  dxbc_ref.at[rows], dz_ref.at[rows], ddtr_ref.at[rows], ddtb_ref, dalog_ref, dnw_ref,
                  dstate, dsk_acc, dy_s, dacol_s, ddte_s)

        @pl.when(pl.program_id(0) == steps - 1)
        def _():
            ddsk_ref[...] = _dot_exact_l(jnp.broadcast_to(dsk_acc[...], (8, SSD_W)), et_ref[...])[0:1]

    def chunk(xbc_ref, dtr_ref, z_ref, yraw_ref, st_ref, dys_ref, acol_ref, dte_ref, dtb_ref, alog_ref,
              dske_ref, nw_ref, et_ref, dxbc_ref, dz_ref, ddtr_ref, ddtb_ref, dalog_ref, dnw_ref,
              dstate, dsk_acc, dy_s, dacol_s, ddte_s):
        itile, ctile, bdmask = _ssd_masks()
        last_row = (_iota((CHUNK, 1), 0) == CHUNK - 1).astype(F32)

        s1 = jnp.zeros((CHUNK, SSD_QW), F32)
        s2 = jnp.zeros((CHUNK, SSD_QW), F32)
        for tl in _TILES:
            t = _silu_gate(z_ref, yraw_ref, tl)[2]
            s1 = s1 + t * t
            s2 = s2 + dys_ref[:, tl] * nw_ref[:, tl] * t
        rstd = lax.rsqrt(jnp.sum(s1, axis=1, keepdims=True) * (1.0 / SSD_W) + EPS)
        back = rstd * rstd * rstd * jnp.sum(s2, axis=1, keepdims=True) * (1.0 / SSD_W)
        for tl in _TILES:
            zv, sz, t = _silu_gate(z_ref, yraw_ref, tl)
            dysv = dys_ref[:, tl]
            dt_ = rstd * (dysv * nw_ref[:, tl]) - t * back
            dnw_ref[:, tl] += jnp.sum(dysv * t * rstd, axis=0, keepdims=True)
            dz_ref[:, tl] = _bf(dt_ * yraw_ref[:, tl] * (sz * (1.0 + zv * (1.0 - sz))))
            dy_t = dt_ * (zv * sz)
            dy_s[:, tl] = dy_t
            dsk_acc[:, tl] += jnp.sum(dy_t * xbc_ref[:, tl], axis=0, keepdims=True)

        gl = _GROUP_LANES
        bg, cg = _ssd_bc(xbc_ref)
        for g0 in range(0, SSD_GROUPS, SSD_BLOCK_GROUPS):
            gids = range(g0, g0 + SSD_BLOCK_GROUPS)
            quads = [(i, g, ql, qs) for i, (g, ql, qs) in enumerate(_QUADS) if g in gids]
            q = {g: _ssd_group(xbc_ref, acol_ref, dte_ref, gl[g]) for g in gids}
            stg = {g: st_ref[:, gl[g]] for g in gids}
            dsn = {g: dstate[:, gl[g]] for g in gids}
            stgb, dsnb = {g: _bf(stg[g]) for g in gids}, {g: _bf(dsn[g]) for g in gids}
            btile = {g: _tile4(bg[g]) for g in gids}
            cbt = {g: _dot_nt(cg[g], btile[g]) for g in gids}
            dyg = {g: dy_s[:, gl[g]] for g in gids}
            eag, tailg = {g: q[g]["ea"] for g in gids}, {g: q[g]["tail"] for g in gids}
            xdtg, ealg = {g: q[g]["xdt"] for g in gids}, {g: q[g]["eal"] for g in gids}
            dys_g = {g: _bf(dyg[g] * eag[g]) for g in gids}
            ysg = {g: _dot(cg[g], stgb[g]) for g in gids}
            dc0 = {g: _dot_nt(dys_g[g], stgb[g]) for g in gids}
            dst = {g: _dot_tn(cg[g], dys_g[g]) for g in gids}
            dwt = {g: _dot(bg[g], dsnb[g]) for g in gids}
            db0 = {g: _dot_nt(_bf(xdtg[g] * tailg[g]), dsnb[g]) for g in gids}
            dtl = {g: dwt[g] * xdtg[g] * tailg[g] for g in gids}
            dal_row = {g: jnp.sum(dtl[g], axis=0, keepdims=True)
                       + jnp.sum(dsn[g] * stg[g], axis=0, keepdims=True) * ealg[g] for g in gids}
            for g in gids:
                dstate[:, gl[g]] = dst[g] + dsn[g] * ealg[g]
            dq = {i: _ssd_decay(acol_ref[:, ql], itile, ctile) for i, _, ql, _ in quads}
            mq = {i: cbt[g] * dq[i] for i, g, _, _ in quads}
            xbd = {i: _ssd_blockdiag(xdtg[g][:, qs], bdmask) for i, g, _, qs in quads}
            dyq = {i: _bf(dyg[g][:, qs]) for i, g, _, qs in quads}
            dm = {i: _dot_nt(dyq[i], xbd[i]) for i, _, _, _ in quads}
            dxdt_q = {i: _fold4(jnp.where(bdmask, _dot_tn(_bf(mq[i]), dyq[i]), 0.0)) for i, _, _, _ in quads}
            eq = {i: dm[i] * mq[i] for i, _, _, _ in quads}
            dacol_q = {i: eq[i] - itile * jnp.sum(eq[i], axis=0, keepdims=True) for i, _, _, _ in quads}
            dcbt = {i: _bf(dm[i] * dq[i]) for i, _, _, _ in quads}
            dc_q = {i: _dot(dcbt[i], btile[g]) for i, g, _, _ in quads}
            db_q = {i: _fold4(_dot_tn(dcbt[i], cg[g])) for i, g, _, _ in quads}
            for g in gids:
                dxdt_g = jnp.concatenate([dxdt_q[2 * g], dxdt_q[2 * g + 1]], axis=1) + dwt[g] * tailg[g]
                dxbc_ref[:, gl[g]] = dyg[g] * dske_ref[:, gl[g]] + dxdt_g * q[g]["dte"]
                ddte_s[:, gl[g]] = dxdt_g * q[g]["xs"]
                dacol_s[:, gl[g]] = (jnp.concatenate([dacol_q[2 * g], dacol_q[2 * g + 1]], axis=1)
                                     + dyg[g] * (ysg[g] * eag[g]) - dtl[g] + last_row * dal_row[g])
                dxbc_ref[:, B_OFF + g * SSD_N:B_OFF + (g + 1) * SSD_N] = db0[g] + db_q[2 * g] + db_q[2 * g + 1]
                dxbc_ref[:, C_OFF + g * SSD_N:C_OFF + (g + 1) * SSD_N] = dc0[g] + dc_q[2 * g] + dc_q[2 * g + 1]

        dacum = jnp.zeros((CHUNK, HEAD_PAD), F32)
        ddt = jnp.zeros((CHUNK, HEAD_PAD), F32)
        for i, tl in enumerate(_TILES):
            et_t = et_ref[i * SSD_QW:(i + 1) * SSD_QW, :]
            dacum = dacum + _dot_exact_l(dacol_s[:, tl], et_t, pieces=2)
            ddt = ddt + _dot_exact_l(ddte_s[:, tl], et_t, pieces=2)
        u, _, nexp, a = _ssd_heads(dtr_ref, dtb_ref, alog_ref)
        triu = _bf((_iota((CHUNK, CHUNK), 1) >= _iota((CHUNK, CHUNK), 0)).astype(F32))
        da = _dot_exact_r(triu, dacum)
        ddt = ddt + da * nexp
        dalog_ref[...] += jnp.sum(da * a, axis=0, keepdims=True)
        du = ddt * _sigmoid(u)
        ddtr_ref[...] = _bf(du)
        ddtb_ref[...] += jnp.sum(du, axis=0, keepdims=True)

    sp = _ssd_specs(steps, True)
    return pl.pallas_call(
        body,
        out_shape=(jax.ShapeDtypeStruct((s, SSD_CD), F32), jax.ShapeDtypeStruct((s, SSD_W), BF16),
                   jax.ShapeDtypeStruct((s, HEAD_PAD), BF16), jax.ShapeDtypeStruct((1, HEAD_PAD), F32),
                   jax.ShapeDtypeStruct((1, HEAD_PAD), F32), jax.ShapeDtypeStruct((1, HEAD_PAD), F32),
                   jax.ShapeDtypeStruct((1, SSD_W), F32), *rider.out_shapes),
        grid=(steps,),
        in_specs=[sp["xbc"], sp["dt"], sp["wide"], sp["wide"], sp["state"], sp["wide"], sp["wide"], sp["wide"],
                  sp["head"], sp["head"], sp["roww"], sp["roww"], sp["et"], *rider.in_specs()],
        out_specs=(sp["xbc"], sp["wide"], sp["dt"], sp["head"], sp["head"], sp["head"], sp["roww"],
                   *rider.out_specs()),
        scratch_shapes=[pltpu.VMEM((SSD_N, SSD_W), F32), pltpu.VMEM((1, SSD_W), F32),
                        *[pltpu.VMEM((CHUNK, SSD_W), F32)] * 3, *rider.scratch()],
        compiler_params=_params("arbitrary"), name=name,
    )(xbc, dt_raw, z, yraw, states, dys, acol, dte, dt_bias, a_log, dske, norm_w, et_sel, *rider.inputs)


W_IN_SHARD = IN_PROJ // N_DEV
W_IN_ROW_PARTS = ((0, 1152), (1152, 832), (1984, 840))


def _device_step(x, pos_col, target, norm1_w, conv_w_shard, conv_b, dt_bias, a_log, d_skip, ssd_norm_w, norm_f_w,
                 w_in_shard, br_ret_shard, br_ssd_shard, out_shard):
    inv_freq = jnp.asarray(ROPE_THETA ** (-np.arange(ROPE_HALF, dtype=np.float64) / ROPE_HALF), F32).reshape(1, ROPE_HALF)
    dtb, alog = _pad_heads(dt_bias), _pad_heads(a_log)
    dske = jnp.repeat(d_skip.reshape(SSD_HEADS).astype(F32), SSD_P).reshape(1, SSD_W)
    my_core = lax.axis_index("c")

    h, w_all = _rmsnorm_fwd(x, norm1_w, "rmsnorm1_fwd", _gather_rider([w_in_shard], 1.0, 1.0))
    w_all = w_all.reshape(IN_PROJ, D_MODEL)
    w_dt = jnp.pad(w_all[OFF_DT:OFF_G], ((0, HEAD_PAD - SSD_HEADS), (0, 0)))
    w_g = w_all[OFF_G:]
    rows_r, rows_z, rows_xbc = (0, OFF_Z), (OFF_Z, SSD_W), (OFF_XBC, SSD_CD)
    gather = _gather_rider([br_ret_shard, br_ssd_shard, out_shard, conv_w_shard], relay_at=0.35, pass_at=0.6)
    p_r, all_ret, all_ssd, all_out, all_conv = _matmul(h, w_all, "nt", F32, "proj_ret", rider=gather, b_rows=rows_r)
    w_br_ret = all_ret.reshape(RET_W, D_MODEL)
    w_br_ssd = all_ssd.reshape(SSD_W, D_MODEL)
    w_out = all_out.reshape(D_MODEL, D_MODEL)
    conv_w = all_conv.transpose(1, 0, 2).reshape(SSD_CONV, SSD_CD)
    p_z = _matmul(h, w_all, "nt", F32, "proj_z", b_rows=rows_z)
    p_xbc = _matmul(h, w_all, "nt", F32, "proj_xbc", b_rows=rows_xbc)
    p_dt = _matmul(h, w_dt, "nt", F32, "proj_dt")
    p_g = _matmul(h, w_g, "nt", BF16, "proj_gates")
    cos, sin = _rope_tables(pos_col, inv_freq, "rope_tables")
    y_r, ret_states = _retention_fwd(p_r, cos, sin, "retention_fwd")
    xbc_act, silu_slope = _conv_fwd(p_xbc, conv_w, conv_b, "conv_fwd")
    y_raw, y_s, ssd_states, acol, dte = _ssd_fwd(xbc_act, p_dt, p_z, dtb, alog, dske, ssd_norm_w, "ssd_fwd")
    pr = _matmul(y_r, w_br_ret, "nn", BF16, "branch_ret")
    ps = _matmul(y_s, w_br_ssd, "nn", BF16, "branch_ssd")
    merged = _merge_fwd(p_g, pr, ps, "merge_fwd")
    o = _matmul(merged, w_out, "nn", F32, "out_proj")
    dx2, g_norm_f, loss_acc = _final_fwd_bwd(x, o, norm_f_w, target, "final_norm_loss")

    g_w_out = _matmul(merged, dx2, "tn", BF16, "grad_w_out")
    dmerged = _matmul(dx2, w_out, "nt", F32, "d_merged")
    dpr, dps, dp_g = _merge_bwd(p_g, pr, ps, dmerged, "merge_bwd")
    g_w_br_ret = _matmul(y_r, dpr, "tn", BF16, "grad_w_br_ret")
    g_w_br_ssd = _matmul(y_s, dps, "tn", BF16, "grad_w_br_ssd")
    dy_r = _matmul(dpr, w_br_ret, "nt", F32, "d_y_ret")
    dy_s = _matmul(dps, w_br_ssd, "nt", F32, "d_y_ssd")
    scatter = _scatter_rider([g_w_out.reshape(N_DEV, -1, D_MODEL), g_w_br_ret.reshape(N_DEV, -1, D_MODEL),
                              g_w_br_ssd.reshape(N_DEV, -1, D_MODEL)])
    dxbc_act, dp_z, dp_dt, g_dtb, g_alog, g_dsk, g_ssd_norm, got_out, got_ret, got_ssd = _ssd_bwd(
        xbc_act, p_dt, p_z, y_raw, ssd_states, dy_s, acol, dte, dtb, alog, dske, ssd_norm_w, "ssd_bwd", scatter)
    dp_xbc, g_conv_w, g_conv_b = _conv_bwd(p_xbc, conv_w, silu_slope, dxbc_act, "conv_bwd")
    dp_r = _retention_bwd(p_r, cos, sin, ret_states, dy_r, "retention_bwd")
    g_w_in = jnp.concatenate([
        _matmul(dp_r, h, "tn", BF16, "grad_w_ret"),
        _matmul(dp_z, h, "tn", BF16, "grad_w_z"),
        _matmul(dp_xbc, h, "tn", BF16, "grad_w_xbc"),
        _matmul(dp_dt, h, "tn", BF16, "grad_w_dt")[:SSD_HEADS],
        _matmul(dp_g, h, "tn", BF16, "grad_w_gates"),
    ], axis=0)
    blocks = g_w_in.reshape(N_DEV, W_IN_SHARD, D_MODEL)
    dh, from_sibling = _matmul(dp_r, w_all, "nn", F32, "d_h_ret", rider=_sibling_rider(blocks), b_rows=rows_r)
    chip_sum = _chip_sum(blocks, from_sibling, my_core, "w_in_chip_sum")
    carriers = (("d_h_xbc", dp_xbc, w_all, rows_xbc), ("d_h_gates", dp_g, w_g, None), ("d_h_z", dp_z, w_all, rows_z))
    landed = []
    for (row0, rows), (nm, dp, w, b_rows) in zip(W_IN_ROW_PARTS, carriers):
        dh, got = _matmul(dp, w, "nn", F32, nm, add=dh, rider=_chip_rider(chip_sum, row0, rows), b_rows=b_rows)
        landed.append(got)
    dh = _matmul(dp_dt, w_dt, "nn", F32, "d_h_dt", add=dh)
    grad_x, g_norm1 = _rmsnorm_bwd(x, norm1_w, dh, dx2, "rmsnorm1_bwd")
    small = dict(norm1_w=g_norm1, conv_w=g_conv_w, conv_b=g_conv_b, dt_bias=g_dtb[:, :SSD_HEADS],
                 a_log=g_alog[:, :SSD_HEADS], d_skip=g_dsk[:, :SSD_HEADS], ssd_norm_w=g_ssd_norm,
                 norm_f_w=g_norm_f)
    big = dict(w_in=_sum_row_parts(landed, "w_in_sum"), w_br_ret=_sum_slots(got_ret, "w_br_ret_sum"),
               w_br_ssd=_sum_slots(got_ssd, "w_br_ssd_sum"), w_out=_sum_slots(got_out, "w_out_sum"))
    return loss_acc[0, 0], grad_x, small, big


def _all_reduce_small(vec, name):
    r, c = vec.shape

    def body(x_ref, out_ref, land, send_sems, recv_sems):
        x, y, cc = _mesh_pos()
        my_idx = 4 * x + 2 * y + cc
        land[my_idx] = x_ref[...]
        copies = []
        for k in range(1, N_DEV):
            px, py, pc = x ^ (k >> 2), y ^ ((k >> 1) & 1), cc ^ (k & 1)
            cp = pltpu.make_async_remote_copy(
                src_ref=x_ref, dst_ref=land.at[my_idx],
                send_sem=send_sems.at[k - 1], recv_sem=recv_sems.at[k - 1],
                device_id=(px, py, pc), device_id_type=MESH)
            cp.start()
            copies.append(cp)
        for k in range(1, N_DEV):
            px, py, pc = x ^ (k >> 2), y ^ ((k >> 1) & 1), cc ^ (k & 1)
            pltpu.make_async_remote_copy(
                src_ref=x_ref, dst_ref=land.at[4 * px + 2 * py + pc],
                send_sem=send_sems.at[k - 1], recv_sem=recv_sems.at[k - 1],
                device_id=(px, py, pc), device_id_type=MESH).wait_recv()
        for cp in copies:
            cp.wait_send()
        acc = land[0]
        for i in range(1, N_DEV):
            acc = acc + land[i]
        out_ref[...] = acc

    return pl.pallas_call(
        body,
        out_shape=jax.ShapeDtypeStruct((r, c), F32),
        in_specs=[pl.BlockSpec(memory_space=pltpu.VMEM)],
        out_specs=pl.BlockSpec(memory_space=pltpu.VMEM),
        scratch_shapes=[pltpu.VMEM((N_DEV, r, c), F32), pltpu.SemaphoreType.DMA((7,)),
                        pltpu.SemaphoreType.DMA((7,))],
        name=name)(vec)


_SMALL = ("norm1_w", "conv_w", "conv_b", "dt_bias", "a_log", "d_skip", "ssd_norm_w", "norm_f_w")
_SMALL_COLS = 128
_W_IN = "w_in"
_WEIGHTS = ("norm1_w", "w_in", "conv_w", "conv_b", "dt_bias", "a_log", "d_skip", "ssd_norm_w",
            "w_br_ret", "w_br_ssd", "w_out", "norm_f_w")


def _pack(parts):
    flat = jnp.concatenate([p.reshape(-1).astype(F32) for p in parts])
    rows = -(-flat.shape[0] // (8 * _SMALL_COLS)) * 8
    return jnp.pad(flat, (0, rows * _SMALL_COLS - flat.shape[0])).reshape(rows, _SMALL_COLS)


def _unpack(packed, shapes):
    flat = packed.reshape(-1)
    out, off = [], 0
    for shp in shapes:
        n = int(np.prod(shp))
        out.append(flat[off:off + n].reshape(shp))
        off += n
    return out


def kernel(x, positions, norm1_w, w_in, conv_w, conv_b, dt_bias, a_log, d_skip, ssd_norm_w, w_br_ret, w_br_ssd, w_out, norm_f_w, loss_target, m_norm1_w, m_w_in, m_conv_w, m_conv_b, m_dt_bias, m_a_log, m_d_skip, m_ssd_norm_w, m_w_br_ret, m_w_br_ssd, m_w_out, m_norm_f_w, v_norm1_w, v_w_in, v_conv_w, v_conv_b, v_dt_bias, v_a_log, v_d_skip, v_ssd_norm_w, v_w_br_ret, v_w_br_ssd, v_w_out, v_norm_f_w):
    w = dict(norm1_w=norm1_w, w_in=w_in, conv_w=conv_w, conv_b=conv_b, dt_bias=dt_bias, a_log=a_log,
             d_skip=d_skip, ssd_norm_w=ssd_norm_w, w_br_ret=w_br_ret, w_br_ssd=w_br_ssd, w_out=w_out,
             norm_f_w=norm_f_w)
    m = dict(norm1_w=m_norm1_w, w_in=m_w_in, conv_w=m_conv_w, conv_b=m_conv_b, dt_bias=m_dt_bias,
             a_log=m_a_log, d_skip=m_d_skip, ssd_norm_w=m_ssd_norm_w, w_br_ret=m_w_br_ret,
             w_br_ssd=m_w_br_ssd, w_out=m_w_out, norm_f_w=m_norm_f_w)
    v = dict(norm1_w=v_norm1_w, w_in=v_w_in, conv_w=v_conv_w, conv_b=v_conv_b, dt_bias=v_dt_bias,
             a_log=v_a_log, d_skip=v_d_skip, ssd_norm_w=v_ssd_norm_w, w_br_ret=v_w_br_ret,
             w_br_ssd=v_w_br_ssd, w_out=v_w_out, norm_f_w=v_norm_f_w)
    s = x.shape[1]
    my_idx = 4 * lax.axis_index("x") + 2 * lax.axis_index("y") + lax.axis_index("c")

    w[_W_IN], m[_W_IN], v[_W_IN] = w_in[0].T, m_w_in[0].T, v_w_in[0].T

    loss_part, grad_x, g_small, g_big = _device_step(
        x[0], positions.reshape(s, 1), loss_target[0], norm1_w, conv_w[0], conv_b, dt_bias, a_log, d_skip,
        ssd_norm_w, norm_f_w.reshape(1, D_MODEL), _cast_bf16(w[_W_IN], "cast_w_in"),
        _cast_bf16(w_br_ret[0], "cast_w_br_ret"), _cast_bf16(w_br_ssd[0], "cast_w_br_ssd"),
        _cast_bf16(w_out[0], "cast_w_out"))

    loss = lax.psum(loss_part, ("x", "y", "c"))

    small_shapes = [g_small[n].shape for n in _SMALL]
    summed = _unpack(_all_reduce_small(_pack([g_small[n] for n in _SMALL]), "allreduce_small"), small_shapes)
    grads = dict(zip(_SMALL, summed))
    conv_cols = SSD_CD // N_DEV
    grads["conv_w"] = lax.dynamic_slice_in_dim(grads["conv_w"], my_idx * conv_cols, conv_cols, axis=1)
    grads["norm_f_w"] = grads["norm_f_w"].reshape(D_MODEL)
    for n in ("norm1_w", "conv_w", "conv_b", "dt_bias", "a_log", "d_skip", "ssd_norm_w"):
        grads[n] = grads[n].reshape(w[n].shape)

    delta, new_m, new_v = {}, {}, {}
    for n in ("w_br_ret", "w_br_ssd", "w_out"):
        w[n], m[n], v[n] = w[n][0], m[n][0], v[n][0]
    for n in (_W_IN, "w_br_ret", "w_br_ssd", "w_out"):
        back = (lambda a: a.T[None]) if n == _W_IN else (lambda a: a[None])
        res = _adamw(w[n], g_big[n], m[n], v[n], "adamw_" + n)
        grads[n] = back(g_big[n])
        delta[n], new_m[n], new_v[n] = (back(a) for a in res)
    shapes = [w[n].shape for n in _SMALL]
    packed = _adamw(_pack([w[n] for n in _SMALL]), _pack([grads[n] for n in _SMALL]),
                    _pack([m[n] for n in _SMALL]), _pack([v[n] for n in _SMALL]), "adamw_small")
    for res, dst in zip(packed, (delta, new_m, new_v)):
        for n, a in zip(_SMALL, _unpack(res, shapes)):
            dst[n] = a

    return (loss, grad_x.reshape(x.shape), *[grads[n] for n in _WEIGHTS], *[delta[n] for n in _WEIGHTS],
            *[new_m[n] for n in _WEIGHTS], *[new_v[n] for n in _WEIGHTS])
```

```python
import functools

import numpy as np
import jax
import jax.numpy as jnp
from jax import lax
from jax.experimental import pallas as pl
from jax.experimental.pallas import tpu as pltpu

F32 = jnp.float32
BF16 = jnp.bfloat16

D_MODEL = 2048
CHUNK = 64
CHUNKS_PER_STEP = 2
STEP_ROWS = CHUNK * CHUNKS_PER_STEP
EPS = 1e-6
N_DEV = 8

RET_HEADS = 8
RET_DK = 256
RET_W = RET_HEADS * RET_DK
ROPE_THETA = 10000.0
ROPE_HALF = RET_DK // 2

SSD_W = 4096
SSD_P = 64
SSD_HEADS = 64
SSD_GROUPS = 8
SSD_N = 128
SSD_GW = SSD_W // SSD_GROUPS
SSD_QW = 256
SSD_BLOCK_GROUPS = 2
SSD_CONV = 4
SSD_CD = SSD_W + 2 * SSD_GROUPS * SSD_N
HEAD_PAD = 128
B_OFF = SSD_W
C_OFF = SSD_W + SSD_GROUPS * SSD_N

ADAM_LR = 0.001
ADAM_B1 = 0.9
ADAM_B2 = 0.999
ADAM_EPS = 1e-08
ADAM_WD = 0.01
ADAM_STEP = 10

SPLITS = (RET_W, RET_W, RET_W, RET_W, SSD_W, SSD_CD, SSD_HEADS, D_MODEL, D_MODEL)
IN_PROJ = sum(SPLITS)
OFF_Z = 4 * RET_W
OFF_XBC = OFF_Z + SSD_W
OFF_DT = OFF_XBC + SSD_CD
OFF_G = OFF_DT + SSD_HEADS

ROW_TILE = 256
CONV_TILE = 128
MM_TILE = 1024
MM_TILE_WIDE = 2048
MM_TK = 2048
VMEM_CAP = 60 << 20

MESH = pl.DeviceIdType.MESH


def _dot(a, b):
    return lax.dot_general(a, b, (((1,), (0,)), ((), ())), preferred_element_type=F32)


def _dot_nt(a, b):
    return lax.dot_general(a, b, (((1,), (1,)), ((), ())), preferred_element_type=F32)


def _dot_tn(a, b):
    return lax.dot_general(a, b, (((0,), (0,)), ((), ())), preferred_element_type=F32)


def _bf(x):
    return x.astype(BF16)


def _split3(x):
    hi = x.astype(BF16)
    r = x - hi.astype(F32)
    mid = r.astype(BF16)
    lo = (r - mid.astype(F32)).astype(BF16)
    return hi, mid, lo


def _dot_exact_l(x, sel, pieces=3):
    hi, mid, lo = _split3(x)
    r = _dot(hi, sel) + _dot(mid, sel)
    return r + _dot(lo, sel) if pieces == 3 else r


def _dot_exact_r(sel, x):
    hi, mid, lo = _split3(x)
    return _dot(sel, hi) + _dot(sel, mid) + _dot(sel, lo)


def _sigmoid(x):
    return 1.0 / (1.0 + jnp.exp(-x))


def _softplus(x):
    return jnp.maximum(x, 0.0) + jnp.log(1.0 + jnp.exp(-jnp.abs(x)))


def _iota(shape, axis):
    return lax.broadcasted_iota(jnp.int32, shape, axis)


def _params(*sem):
    return pltpu.CompilerParams(dimension_semantics=sem)


class _Rider:
    def __init__(self, inputs, out_shapes, n_sems, phases):
        self.inputs, self.out_shapes, self.n_sems, self.phases = tuple(inputs), tuple(out_shapes), n_sems, phases

    def in_specs(self):
        return [pl.BlockSpec(memory_space=pl.ANY)] * len(self.inputs)

    def out_specs(self):
        return [pl.BlockSpec(memory_space=pl.ANY)] * len(self.out_shapes)

    def scratch(self):
        return [pltpu.SemaphoreType.DMA((self.n_sems,)), pltpu.SemaphoreType.DMA((self.n_sems,))]

    def run(self, step, n_steps, ins, outs, send_sems, recv_sems, last):
        for frac, fn in self.phases:
            if (frac >= 1.0) != last:
                continue
            at = min(int(frac * n_steps), n_steps - 1)

            @pl.when(step == at)
            def _(fn=fn):
                fn(ins, outs, send_sems, recv_sems)


def _mesh_pos():
    return lax.axis_index("x"), lax.axis_index("y"), lax.axis_index("c")


def _gather_rider(shards, relay_at, pass_at):
    n = len(shards)

    def tools(a, ins, outs, send_sems, recv_sems):
        x, y, cc = _mesh_pos()
        nbrs = [(1 - x, y), (x, 1 - y)]
        diag = (1 - x, 1 - y)
        relay_from, relay_to = (x ^ cc, y ^ (1 - cc)), (x ^ (1 - cc), y ^ cc)

        def slot(px, py, pc):
            return outs[a].at[4 * px + 2 * py + pc]

        def copy(k, block, to, src=None):
            return pltpu.make_async_remote_copy(
                src_ref=slot(*block) if src is None else src, dst_ref=slot(*block),
                send_sem=send_sems.at[8 * a + k], recv_sem=recv_sems.at[8 * a + k], device_id=to, device_id_type=MESH)

        me, sibling = (x, y, cc), (x, y, 1 - cc)
        return dict(
            mine=lambda: pltpu.make_async_copy(ins[a], slot(*me), send_sems.at[8 * a + 7]),
            first=lambda: [copy(0, me, sibling, src=ins[a])] + [copy(1 + j, me, (*chip, cc), src=ins[a])
                                                                for j, chip in enumerate(nbrs)],
            landed=lambda: [copy(1 + j, (*chip, cc), me) for j, chip in enumerate(nbrs)],
            relay=lambda: copy(3, (*relay_from, cc), (*relay_to, cc)),
            relayed_in=lambda: copy(3, (*diag, cc), me),
            passed=lambda: [copy(4 + j, (*chip, cc), sibling) for j, chip in enumerate(nbrs)],
            passed_diag=lambda: copy(6, (*diag, cc), sibling),
            from_sibling=lambda: [copy(0, sibling, me)] + [copy(4 + j, (*chip, 1 - cc), me)
                                                           for j, chip in enumerate(nbrs + [diag])])

    def start(*refs):
        for a in range(n):
            t = tools(a, *refs)
            t["mine"]().start()
            for cp in t["first"]():
                cp.start()

    def relay(*refs):
        for a in range(n):
            t = tools(a, *refs)
            for got in t["landed"]():
                got.wait_recv()
            t["relay"]().start()
            for cp in t["passed"]():
                cp.start()

    def pass_diag(*refs):
        for a in range(n):
            t = tools(a, *refs)
            t["relayed_in"]().wait_recv()
            t["passed_diag"]().start()

    def finish(*refs):
        for a in range(n):
            t = tools(a, *refs)
            for cp in t["from_sibling"]():
                cp.wait_recv()
            for cp in t["first"]() + [t["relay"]()] + t["passed"]() + [t["passed_diag"]()]:
                cp.wait_send()
            t["mine"]().wait()

    outs = [jax.ShapeDtypeStruct((N_DEV,) + s.shape, s.dtype) for s in shards]
    return _Rider(shards, outs, 8 * n, ((0.0, start), (relay_at, relay), (pass_at, pass_diag), (1.0, finish)))


def _scatter_rider(blocks):
    n = len(blocks)

    def copies(a, ins, outs, send_sems, recv_sems, with_back):
        x, y, cc = _mesh_pos()
        my_idx = 4 * x + 2 * y + cc
        mine = pltpu.make_async_copy(ins[a].at[my_idx], outs[a].at[my_idx], send_sems.at[8 * a + 7])
        out, back = [], []
        for k in range(1, N_DEV):
            px, py, pc = x ^ (k >> 2), y ^ ((k >> 1) & 1), cc ^ (k & 1)
            peer_idx = 4 * px + 2 * py + pc
            sems = dict(send_sem=send_sems.at[8 * a + k - 1], recv_sem=recv_sems.at[8 * a + k - 1],
                        device_id=(px, py, pc), device_id_type=MESH)
            out.append(pltpu.make_async_remote_copy(src_ref=ins[a].at[peer_idx], dst_ref=outs[a].at[my_idx], **sems))
            if with_back:
                back.append(pltpu.make_async_remote_copy(src_ref=ins[a].at[my_idx], dst_ref=outs[a].at[peer_idx], **sems))
        return mine, out, back

    def start(*refs):
        for a in range(n):
            mine, out, _ = copies(a, *refs, False)
            mine.start()
            for cp in out:
                cp.start()

    def finish(*refs):
        for a in range(n):
            mine, out, back = copies(a, *refs, True)
            for cp in back:
                cp.wait_recv()
            for cp in out:
                cp.wait_send()
            mine.wait()

    outs = [jax.ShapeDtypeStruct(b.shape, b.dtype) for b in blocks]
    return _Rider(blocks, outs, 8 * n, ((0.0, start), (1.0, finish)))


def _sibling_rider(blocks):
    _, r, c = blocks.shape

    def copies(ins, outs, send_sems, recv_sems):
        x, y, cc = _mesh_pos()
        return [pltpu.make_async_remote_copy(
            src_ref=ins[0].at[2 * j + 1 - cc], dst_ref=outs[0].at[j], send_sem=send_sems.at[j],
            recv_sem=recv_sems.at[j], device_id=(x, y, 1 - cc), device_id_type=MESH) for j in range(4)]

    def start(*refs):
        for cp in copies(*refs):
            cp.start()

    def finish(*refs):
        for cp in copies(*refs):
            cp.wait_recv()
        for cp in copies(*refs):
            cp.wait_send()

    return _Rider([blocks], [jax.ShapeDtypeStruct((4, r, c), blocks.dtype)], 4, ((0.0, start), (1.0, finish)))


def _chip_rider(partial, row0, rows):
    _, _, c = partial.shape

    def copies(ins, outs, send_sems, recv_sems, with_back):
        x, y, cc = _mesh_pos()
        my_chip = 2 * x + y

        def rows_of(j):
            return ins[0].at[j, pl.ds(row0, rows)]

        mine = pltpu.make_async_copy(rows_of(my_chip), outs[0].at[my_chip], send_sems.at[3])
        out, back = [], []
        for k in range(1, 4):
            px, py = x ^ (k >> 1), y ^ (k & 1)
            peer = 2 * px + py
            sems = dict(send_sem=send_sems.at[k - 1], recv_sem=recv_sems.at[k - 1], device_id=(px, py, cc),
                        device_id_type=MESH)
            out.append(pltpu.make_async_remote_copy(src_ref=rows_of(peer), dst_ref=outs[0].at[my_chip], **sems))
            if with_back:
                back.append(pltpu.make_async_remote_copy(src_ref=rows_of(my_chip), dst_ref=outs[0].at[peer], **sems))
        return mine, out, back

    def start(*refs):
        mine, out, _ = copies(*refs, False)
        mine.start()
        for cp in out:
            cp.start()

    def finish(*refs):
        mine, out, back = copies(*refs, True)
        for cp in back:
            cp.wait_recv()
        for cp in out:
            cp.wait_send()
        mine.wait()

    return _Rider([partial], [jax.ShapeDtypeStruct((4, rows, c), partial.dtype)], 4, ((0.0, start), (1.0, finish)))


def _matmul(a, b, mode, out_dtype, name, add=None, rider=None, b_rows=None, extra=None):
    b_shape = b.shape if b_rows is None else (b_rows[1], b.shape[1])
    if mode == "nn":
        (m, k), (k2, n) = a.shape, b_shape
    elif mode == "nt":
        (m, k), (n, k2) = a.shape, b_shape
    else:
        (k, m), (k2, n) = a.shape, b_shape
    assert k == k2, (a.shape, b_shape, mode)
    tm, tn, tk = min(m, MM_TILE), min(n, MM_TILE), min(k, MM_TK)
    if n % MM_TILE_WIDE == 0 and (k <= MM_TK or (add is None and jnp.dtype(out_dtype).itemsize == 2)):
        tn = MM_TILE_WIDE
    assert m % tm == 0 and n % tn == 0 and k % tk == 0, (m, n, k)
    nk = k // tk
    b_tile_rows = tn if mode == "nt" else tk
    b_off = 0 if b_rows is None else b_rows[0] // b_tile_rows
    assert b_rows is None or b_rows[0] % b_tile_rows == 0, (b_rows, b_tile_rows)
    dot = {"nn": _dot, "nt": _dot_nt, "tn": _dot_tn}[mode]

    gm, gn = m // tm, n // tn
    n_rin = len(rider.inputs) if rider else 0
    n_rout = len(rider.out_shapes) if rider else 0
    n_add = 1 if add is not None else 0

    n_extra = 2 if extra is not None else 0

    def body(*refs):
        a_ref, b_ref = refs[:2]
        add_ref = refs[2] if n_add else None
        extra_refs = refs[2 + n_add:2 + n_add + n_extra]
        first = 2 + n_add + n_extra
        r_ins = refs[first:first + n_rin]
        o_ref = refs[first + n_rin]
        r_outs = refs[first + n_rin + 1:first + n_rin + 1 + n_rout]
        scratch = refs[first + n_rin + 1 + n_rout:]
        acc_ref = scratch[0] if nk > 1 else None
        kk = pl.program_id(2)
        if rider:
            step = (pl.program_id(0) * gn + pl.program_id(1)) * nk + kk
            ride = functools.partial(rider.run, step, gm * gn * nk, r_ins, r_outs, scratch[-2], scratch[-1])
            ride(last=False)

        def finish(r):
            if extra_refs:
                r = r + _dot(_bf(extra_refs[0][...]), _bf(extra_refs[1][...]))
            if add_ref is not None:
                r = r + add_ref[...]
            o_ref[...] = r.astype(o_ref.dtype)

        if nk == 1:
            finish(dot(_bf(a_ref[...]), _bf(b_ref[...])))
        else:
            @pl.when(kk == 0)
            def _():
                acc_ref[...] = dot(_bf(a_ref[...]), _bf(b_ref[...]))

            @pl.when(jnp.logical_and(kk > 0, kk < nk - 1))
            def _():
                acc_ref[...] += dot(_bf(a_ref[...]), _bf(b_ref[...]))

            @pl.when(kk == nk - 1)
            def _():
                finish(acc_ref[...] + dot(_bf(a_ref[...]), _bf(b_ref[...])))
        if rider:
            ride(last=True)

    if mode == "nn":
        a_spec = pl.BlockSpec((tm, tk), lambda i, j, kk: (i, kk))
        b_spec = pl.BlockSpec((tk, tn), lambda i, j, kk: (kk + b_off, j))
    elif mode == "nt":
        a_spec = pl.BlockSpec((tm, tk), lambda i, j, kk: (i, kk))
        b_spec = pl.BlockSpec((tn, tk), lambda i, j, kk: (j + b_off, kk))
    else:
        a_spec = pl.BlockSpec((tk, tm), lambda i, j, kk: (kk, i))
        b_spec = pl.BlockSpec((tk, tn), lambda i, j, kk: (kk + b_off, j))
    o_spec = pl.BlockSpec((tm, tn), lambda i, j, kk: (i, j))
    in_specs = [a_spec, b_spec] + ([o_spec] if add is not None else [])
    args = (a, b) + ((add,) if add is not None else ())
    if extra is not None:
        (m2, k_extra), (k_extra2, n2) = extra[0].shape, extra[1].shape
        assert (m2, n2, k_extra) == (m, n, k_extra2), (extra[0].shape, extra[1].shape)
        in_specs += [pl.BlockSpec((tm, k_extra), lambda i, j, kk: (i, 0)),
                     pl.BlockSpec((k_extra, tn), lambda i, j, kk: (0, j))]
        args += tuple(extra)
    block_bytes = (tm * tk * a.dtype.itemsize + tk * tn * b.dtype.itemsize
                   + tm * tn * (jnp.dtype(out_dtype).itemsize + (4 if add is not None else 0)))
    vmem = min(2 * block_bytes + 2 * tm * tn * 4 + 2 * (tm + tn) * tk + (4 << 20), VMEM_CAP)
    out_shape = jax.ShapeDtypeStruct((m, n), out_dtype)
    scratch = [pltpu.VMEM((tm, tn), F32)] if nk > 1 else []
    if rider:
        in_specs = in_specs + rider.in_specs()
        args = args + rider.inputs
        out_shape = (out_shape, *rider.out_shapes)
        o_spec = (o_spec, *rider.out_specs())
        scratch = scratch + rider.scratch()
    sem = ("arbitrary",) * 3 if rider else ("parallel", "parallel", "arbitrary")
    return pl.pallas_call(
        body,
        out_shape=out_shape,
        grid=(gm, gn, nk),
        in_specs=in_specs,
        out_specs=o_spec,
        scratch_shapes=scratch,
        compiler_params=pltpu.CompilerParams(dimension_semantics=sem, vmem_limit_bytes=int(vmem)),
        name=name,
    )(*args)


def _row_spec(width, tile=ROW_TILE):
    return pl.BlockSpec((tile, width), lambda i: (i, 0))


def _full_spec(shape):
    nd = len(shape)
    return pl.BlockSpec(shape, lambda *_: (0,) * nd)


def _tiling_2d(r, c):
    if r <= ROW_TILE or r % ROW_TILE == 0:
        tr = min(r, ROW_TILE)
        return (tr, c), r // tr, (lambda i: (i, 0))
    tc = 128 if r > 4096 else 256
    assert c % tc == 0, (r, c)
    return (r, tc), c // tc, (lambda i: (0, i))


def _spec_2d(r, c):
    blk, grid, idx = _tiling_2d(r, c)
    return pl.BlockSpec(blk, idx), grid


def _cast_bf16(x, name):
    r, c = x.shape
    sp, grid = _spec_2d(r, c)

    def body(x_ref, o_ref):
        o_ref[...] = _bf(x_ref[...])

    return pl.pallas_call(
        body, out_shape=jax.ShapeDtypeStruct((r, c), BF16), grid=(grid,), in_specs=[sp], out_specs=sp,
        compiler_params=_params("parallel"), name=name)(x)


def _rmsnorm_fwd(x, w, name, rider):
    s, d = x.shape
    steps = s // ROW_TILE
    n_rin, n_rout = len(rider.inputs), len(rider.out_shapes)

    def body(x_ref, w_ref, *refs):
        r_ins, o_ref, r_outs = refs[:n_rin], refs[n_rin], refs[n_rin + 1:n_rin + 1 + n_rout]
        ride = functools.partial(rider.run, pl.program_id(0), steps, r_ins, r_outs, refs[-2], refs[-1])
        ride(last=False)
        xv = x_ref[...]
        rstd = lax.rsqrt(jnp.mean(xv * xv, axis=1, keepdims=True) + EPS)
        o_ref[...] = _bf(xv * rstd * w_ref[...])
        ride(last=True)

    return pl.pallas_call(
        body, out_shape=(jax.ShapeDtypeStruct((s, d), BF16), *rider.out_shapes), grid=(steps,),
        in_specs=[_row_spec(d), _full_spec((1, d)), *rider.in_specs()],
        out_specs=(_row_spec(d), *rider.out_specs()), scratch_shapes=rider.scratch(),
        compiler_params=_params("arbitrary"), name=name)(x, w, *rider.inputs)


def _rmsnorm_bwd(x, w, dh, dres, name):
    s, d = x.shape

    def body(x_ref, w_ref, dh_ref, dres_ref, dx_ref, dw_ref):
        @pl.when(pl.program_id(0) == 0)
        def _():
            dw_ref[...] = jnp.zeros_like(dw_ref)

        xv = x_ref[...]
        rstd = lax.rsqrt(jnp.mean(xv * xv, axis=1, keepdims=True) + EPS)
        xhat = xv * rstd
        dhv = dh_ref[...]
        dxhat = dhv * w_ref[...]
        dx = rstd * (dxhat - xhat * jnp.mean(dxhat * xhat, axis=1, keepdims=True))
        dx_ref[...] = dx + dres_ref[...]
        dw_ref[...] += jnp.sum(dhv * xhat, axis=0, keepdims=True)

    return pl.pallas_call(
        body,
        out_shape=(jax.ShapeDtypeStruct((s, d), F32), jax.ShapeDtypeStruct((1, d), F32)),
        grid=(s // ROW_TILE,),
        in_specs=[_row_spec(d), _full_spec((1, d)), _row_spec(d), _row_spec(d)],
        out_specs=(_row_spec(d), _full_spec((1, d))),
        compiler_params=_params("arbitrary"), name=name)(x, w, dh, dres)


def _rope_tables(pos_col, inv_freq, name):
    s = pos_col.shape[0]

    def body(p_ref, f_ref, cos_ref, sin_ref):
        ang = p_ref[...].astype(F32) * f_ref[...]
        cos_ref[...] = jnp.cos(ang)
        sin_ref[...] = jnp.sin(ang)

    out = jax.ShapeDtypeStruct((s, ROPE_HALF), F32)
    return pl.pallas_call(
        body, out_shape=(out, out), grid=(s // ROW_TILE,),
        in_specs=[_row_spec(1), _full_spec((1, ROPE_HALF))],
        out_specs=(_row_spec(ROPE_HALF), _row_spec(ROPE_HALF)),
        compiler_params=_params("parallel"), name=name)(pos_col, inv_freq)


def _merge_fwd(pg, p_r, p_s, name):
    s = pg.shape[0]

    def body(g_ref, r_ref, s_ref, o_ref):
        g = g_ref[...].astype(F32)
        o_ref[...] = _bf(_sigmoid(g[:, :D_MODEL]) * r_ref[...].astype(F32)
                         + _sigmoid(g[:, D_MODEL:]) * s_ref[...].astype(F32))

    return pl.pallas_call(
        body, out_shape=jax.ShapeDtypeStruct((s, D_MODEL), BF16), grid=(s // ROW_TILE,),
        in_specs=[_row_spec(2 * D_MODEL), _row_spec(D_MODEL), _row_spec(D_MODEL)],
        out_specs=_row_spec(D_MODEL), compiler_params=_params("parallel"), name=name)(pg, p_r, p_s)


def _merge_bwd(pg, p_r, p_s, dm, name):
    s = pg.shape[0]

    def body(g_ref, r_ref, s_ref, dm_ref, dr_ref, ds_ref, dg_ref):
        g = g_ref[...].astype(F32)
        sr, ss = _sigmoid(g[:, :D_MODEL]), _sigmoid(g[:, D_MODEL:])
        d = dm_ref[...]
        dr_ref[...] = _bf(d * sr)
        ds_ref[...] = _bf(d * ss)
        dg_ref[:, :D_MODEL] = _bf(d * r_ref[...].astype(F32) * sr * (1.0 - sr))
        dg_ref[:, D_MODEL:] = _bf(d * s_ref[...].astype(F32) * ss * (1.0 - ss))

    o = jax.ShapeDtypeStruct((s, D_MODEL), BF16)
    return pl.pallas_call(
        body, out_shape=(o, o, jax.ShapeDtypeStruct((s, 2 * D_MODEL), BF16)), grid=(s // ROW_TILE,),
        in_specs=[_row_spec(2 * D_MODEL), _row_spec(D_MODEL), _row_spec(D_MODEL), _row_spec(D_MODEL)],
        out_specs=(_row_spec(D_MODEL), _row_spec(D_MODEL), _row_spec(2 * D_MODEL)),
        compiler_params=_params("parallel"), name=name)(pg, p_r, p_s, dm)


def _final_fwd_bwd(x, o, w, target, name):
    s, d = x.shape

    def body(x_ref, o_ref, w_ref, t_ref, dx_ref, dw_ref, loss_ref):
        @pl.when(pl.program_id(0) == 0)
        def _():
            dw_ref[...] = jnp.zeros_like(dw_ref)
            loss_ref[...] = jnp.zeros_like(loss_ref)

        x2 = x_ref[...] + o_ref[...]
        rstd = lax.rsqrt(jnp.mean(x2 * x2, axis=1, keepdims=True) + EPS)
        xhat = x2 * rstd
        wv = w_ref[...]
        err = xhat * wv - t_ref[...]
        loss_ref[...] += jnp.sum(jnp.sum(err * err, axis=1, keepdims=True), axis=0, keepdims=True) * (0.5 / d)
        dy = err * (1.0 / d)
        dw_ref[...] += jnp.sum(dy * xhat, axis=0, keepdims=True)
        dxhat = dy * wv
        dx_ref[...] = rstd * (dxhat - xhat * jnp.mean(dxhat * xhat, axis=1, keepdims=True))

    return pl.pallas_call(
        body,
        out_shape=(jax.ShapeDtypeStruct((s, d), F32), jax.ShapeDtypeStruct((1, d), F32),
                   jax.ShapeDtypeStruct((8, 128), F32)),
        grid=(s // ROW_TILE,),
        in_specs=[_row_spec(d), _row_spec(d), _full_spec((1, d)), _row_spec(d)],
        out_specs=(_row_spec(d), _full_spec((1, d)), _full_spec((8, 128))),
        compiler_params=_params("arbitrary"), name=name)(x, o, w, target)


def _adamw(w, g, m, v, name):
    r, c = w.shape
    sp, grid = _spec_2d(r, c)
    c1 = 1.0 / (1.0 - ADAM_B1 ** ADAM_STEP)
    c2 = 1.0 / (1.0 - ADAM_B2 ** ADAM_STEP)

    def body(w_ref, g_ref, m_ref, v_ref, d_ref, nm_ref, nv_ref):
        gv = g_ref[...]
        nm = ADAM_B1 * m_ref[...] + (1.0 - ADAM_B1) * gv
        nv = ADAM_B2 * v_ref[...] + (1.0 - ADAM_B2) * (gv * gv)
        d_ref[...] = -ADAM_LR * ((nm * c1) / (jnp.sqrt(nv * c2) + ADAM_EPS) + ADAM_WD * w_ref[...])
        nm_ref[...] = nm
        nv_ref[...] = nv

    o = jax.ShapeDtypeStruct((r, c), F32)
    return pl.pallas_call(
        body, out_shape=(o, o, o), grid=(grid,), in_specs=[sp, sp, sp, sp], out_specs=(sp, sp, sp),
        compiler_params=_params("parallel"), name=name)(w, g, m, v)


def _sum_slots(land, name):
    n, r, c = land.shape
    blk, grid, idx = _tiling_2d(r, c)

    def body(l_ref, o_ref):
        acc = l_ref[0].astype(F32)
        for i in range(1, n):
            acc = acc + l_ref[i].astype(F32)
        o_ref[...] = acc

    return pl.pallas_call(
        body, out_shape=jax.ShapeDtypeStruct((r, c), F32), grid=(grid,),
        in_specs=[pl.BlockSpec((n, *blk), lambda i: (0, *idx(i)))], out_specs=pl.BlockSpec(blk, idx),
        compiler_params=_params("parallel"), name=name)(land)


SLAB_COLS = 256


def _chip_sum(blocks, from_sibling, my_core, name):
    _, r, c = blocks.shape

    def body(core_ref, a_ref, b_ref, o_ref):
        o_ref[...] = _bf(a_ref[...].astype(F32) + b_ref[...].astype(F32))

    def slab(chip_of):
        return pl.BlockSpec((1, r, SLAB_COLS), lambda j, i, core: (chip_of(j, core), 0, i))

    grid_spec = pltpu.PrefetchScalarGridSpec(
        num_scalar_prefetch=1, grid=(4, c // SLAB_COLS),
        in_specs=[slab(lambda j, core: 2 * j + core[0]), slab(lambda j, core: j)],
        out_specs=slab(lambda j, core: j))
    return pl.pallas_call(
        body, out_shape=jax.ShapeDtypeStruct((4, r, c), BF16), grid_spec=grid_spec,
        compiler_params=_params("parallel", "parallel"), name=name,
    )(my_core.astype(jnp.int32).reshape(1), blocks, from_sibling)


def _sum_row_parts(parts, name):
    c = parts[0].shape[2]
    rows = [p.shape[1] for p in parts]

    def body(*refs):
        o_ref, off = refs[-1], 0
        for p_ref, n in zip(refs[:-1], rows):
            acc = p_ref[0].astype(F32)
            for i in range(1, p_ref.shape[0]):
                acc = acc + p_ref[i].astype(F32)
            o_ref[off:off + n, :] = acc
            off += n

    return pl.pallas_call(
        body, out_shape=jax.ShapeDtypeStruct((sum(rows), c), F32), grid=(c // SLAB_COLS,),
        in_specs=[pl.BlockSpec((p.shape[0], p.shape[1], SLAB_COLS), lambda i: (0, 0, i)) for p in parts],
        out_specs=pl.BlockSpec((sum(rows), SLAB_COLS), lambda i: (0, i)),
        compiler_params=_params("parallel"), name=name)(*parts)


def _retention_tables():
    lg = np.log1p(-(2.0 ** (-5.0 - np.arange(RET_HEADS, dtype=np.float64))))
    idx = np.arange(CHUNK, dtype=np.float64)
    intra = np.exp(np.abs(idx[:, None] - idx[None, :])[None] * lg[:, None, None])
    qd = np.exp((idx[None, :] + 1.0) * lg[:, None])
    kd = np.exp((CHUNK - 1.0 - idx[None, :]) * lg[:, None])
    cd = np.exp(CHUNK * lg)
    ones = np.ones((1, 1, RET_DK))
    return (jnp.asarray(intra, F32), jnp.asarray(qd[:, :, None] * ones, F32),
            jnp.asarray(kd[:, :, None] * ones, F32), jnp.asarray(cd[:, None, None] * ones, F32))


def _rope(t, cos, sin):
    t1, t2 = t[:, :ROPE_HALF], t[:, ROPE_HALF:]
    return jnp.concatenate([t1 * cos - t2 * sin, t2 * cos + t1 * sin], axis=1)


def _rope_t(d, cos, sin):
    d1, d2 = d[:, :ROPE_HALF], d[:, ROPE_HALF:]
    return jnp.concatenate([d1 * cos + d2 * sin, d2 * cos - d1 * sin], axis=1)


_HEADS = range(RET_HEADS)


def _ret_chunk_fwd(p_ref, cos, sin, intra_ref, qd_ref, st):
    def seg(i, h):
        return p_ref[:, i * RET_W + h * RET_DK:i * RET_W + (h + 1) * RET_DK]

    v = [seg(2, h) for h in _HEADS]
    qr = [_rope(seg(0, h), cos, sin) for h in _HEADS]
    kr = [_rope(seg(1, h), cos, sin) * (RET_DK ** -0.5) for h in _HEADS]
    qrb, vb = [_bf(a) for a in qr], [_bf(a) for a in v]
    sc = [_dot_nt(qrb[h], _bf(kr[h])) * intra_ref[h] for h in _HEADS]
    qs = [_dot(qrb[h], _bf(st[h])) for h in _HEADS]
    y = [_dot(_bf(sc[h]), vb[h]) + qs[h] * qd_ref[h] for h in _HEADS]
    g = [seg(3, h) for h in _HEADS]
    return dict(v=v, vb=vb, qr=qr, qrb=qrb, kr=kr, sc=sc, y=y, g=g)


def _group_norm(y):
    mu = jnp.mean(y, axis=1, keepdims=True)
    yc = y - mu
    rstd = lax.rsqrt(jnp.mean(yc * yc, axis=1, keepdims=True) + EPS)
    return yc * rstd, rstd


def _chunk_rows(sub):
    return pl.ds(sub * CHUNK, CHUNK)


def _ret_specs(steps, rev):
    cidx = (lambda c: steps - 1 - c) if rev else (lambda c: c)
    return dict(
        proj=pl.BlockSpec((STEP_ROWS, 4 * RET_W), lambda c: (cidx(c), 0)),
        half=pl.BlockSpec((STEP_ROWS, ROPE_HALF), lambda c: (cidx(c), 0)),
        wide=pl.BlockSpec((STEP_ROWS, RET_W), lambda c: (cidx(c), 0)),
        state=pl.BlockSpec((CHUNKS_PER_STEP, RET_HEADS, RET_DK, RET_DK), lambda c: (cidx(c), 0, 0, 0)),
        intra=_full_spec((RET_HEADS, CHUNK, CHUNK)),
        dec=_full_spec((RET_HEADS, CHUNK, RET_DK)),
        cd=_full_spec((RET_HEADS, 1, RET_DK)),
    )


def _retention_fwd(p_r, cos, sin, name):
    s = p_r.shape[0]
    nc = s // CHUNK
    intra_t, qd_t, kd_t, cd_t = _retention_tables()

    def body(p_ref, cos_ref, sin_ref, intra_ref, qd_ref, kd_ref, cd_ref, y_ref, st_ref, state):
        @pl.when(pl.program_id(0) == 0)
        def _():
            state[...] = jnp.zeros_like(state)

        for sub in range(CHUNKS_PER_STEP):
            rows = _chunk_rows(sub)
            y_v, st_v = y_ref.at[rows], st_ref.at[sub]
            st = [state[h] for h in _HEADS]
            f = _ret_chunk_fwd(p_ref.at[rows], cos_ref[rows, :], sin_ref[rows, :], intra_ref, qd_ref, st)
            new_st = [st[h] * cd_ref[h] + _dot_tn(_bf(f["kr"][h] * kd_ref[h]), f["vb"][h]) for h in _HEADS]
            out = [_bf(_group_norm(f["y"][h])[0] * (f["g"][h] * _sigmoid(f["g"][h]))) for h in _HEADS]
            for h in _HEADS:
                st_v[h] = _bf(st[h])
                state[h] = new_st[h]
                y_v[:, h * RET_DK:(h + 1) * RET_DK] = out[h]

    steps = nc // CHUNKS_PER_STEP
    sp = _ret_specs(steps, False)
    return pl.pallas_call(
        body,
        out_shape=(jax.ShapeDtypeStruct((s, RET_W), BF16),
                   jax.ShapeDtypeStruct((nc, RET_HEADS, RET_DK, RET_DK), BF16)),
        grid=(steps,),
        in_specs=[sp["proj"], sp["half"], sp["half"], sp["intra"], sp["dec"], sp["dec"], sp["cd"]],
        out_specs=(sp["wide"], sp["state"]),
        scratch_shapes=[pltpu.VMEM((RET_HEADS, RET_DK, RET_DK), F32)],
        compiler_params=_params("arbitrary"), name=name)(p_r, cos, sin, intra_t, qd_t, kd_t, cd_t)


def _retention_bwd(p_r, cos, sin, states, dy_r, name):
    s = p_r.shape[0]
    nc = s // CHUNK
    intra_t, qd_t, kd_t, cd_t = _retention_tables()

    def body(p_ref, cos_ref, sin_ref, intra_ref, qd_ref, kd_ref, cd_ref, st_ref, dy_ref, dp_ref, dstate):
        @pl.when(pl.program_id(0) == 0)
        def _():
            dstate[...] = jnp.zeros_like(dstate)

        for sub in reversed(range(CHUNKS_PER_STEP)):
            rows = _chunk_rows(sub)
            chunk(p_ref.at[rows], cos_ref[rows, :], sin_ref[rows, :], intra_ref, qd_ref, kd_ref, cd_ref,
                  st_ref.at[sub], dy_ref.at[rows], dp_ref.at[rows], dstate)

    def chunk(p_ref, cos, sin, intra_ref, qd_ref, kd_ref, cd_ref, st_ref, dy_ref, dp_ref, dstate):
        st = [st_ref[h] for h in _HEADS]
        dsn = [dstate[h] for h in _HEADS]
        f = _ret_chunk_fwd(p_ref, cos, sin, intra_ref, qd_ref, st)
        vb, qrb, kr, g = f["vb"], f["qrb"], f["kr"], f["g"]
        norm = [_group_norm(f["y"][h]) for h in _HEADS]
        sg = [_sigmoid(g[h]) for h in _HEADS]
        dyr = [dy_ref[:, h * RET_DK:(h + 1) * RET_DK] for h in _HEADS]
        dyn = [dyr[h] * (g[h] * sg[h]) for h in _HEADS]
        dg = [dyr[h] * norm[h][0] * (sg[h] * (1.0 + g[h] * (1.0 - sg[h]))) for h in _HEADS]
        dy = [norm[h][1] * (dyn[h] - jnp.mean(dyn[h], axis=1, keepdims=True)
                            - norm[h][0] * jnp.mean(dyn[h] * norm[h][0], axis=1, keepdims=True)) for h in _HEADS]
        dyb, dsnb = [_bf(a) for a in dy], [_bf(a) for a in dsn]
        ds = [_bf(_dot_nt(dyb[h], vb[h]) * intra_ref[h]) for h in _HEADS]
        t = [_bf(dy[h] * qd_ref[h]) for h in _HEADS]
        dv = [_dot_tn(_bf(f["sc"][h]), dyb[h]) + _dot(_bf(kr[h] * kd_ref[h]), dsnb[h]) for h in _HEADS]
        dqr = [_dot(ds[h], _bf(kr[h])) + _dot_nt(t[h], _bf(st[h])) for h in _HEADS]
        dkr = [_dot_tn(ds[h], qrb[h]) + _dot_nt(vb[h], dsnb[h]) * kd_ref[h] for h in _HEADS]
        new_ds = [dsn[h] * cd_ref[h] + _dot_tn(qrb[h], t[h]) for h in _HEADS]
        for h in _HEADS:
            lo = h * RET_DK
            dstate[h] = new_ds[h]
            dp_ref[:, lo:lo + RET_DK] = _bf(_rope_t(dqr[h], cos, sin))
            dp_ref[:, RET_W + lo:RET_W + lo + RET_DK] = _bf(_rope_t(dkr[h], cos, sin) * (RET_DK ** -0.5))
            dp_ref[:, 2 * RET_W + lo:2 * RET_W + lo + RET_DK] = _bf(dv[h])
            dp_ref[:, 3 * RET_W + lo:3 * RET_W + lo + RET_DK] = _bf(dg[h])

    steps = nc // CHUNKS_PER_STEP
    sp = _ret_specs(steps, True)
    return pl.pallas_call(
        body,
        out_shape=jax.ShapeDtypeStruct((s, 4 * RET_W), BF16),
        grid=(steps,),
        in_specs=[sp["proj"], sp["half"], sp["half"], sp["intra"], sp["dec"], sp["dec"], sp["cd"],
                  sp["state"], sp["wide"]],
        out_specs=sp["proj"],
        scratch_shapes=[pltpu.VMEM((RET_HEADS, RET_DK, RET_DK), F32)],
        compiler_params=_params("arbitrary"), name=name)(p_r, cos, sin, intra_t, qd_t, kd_t, cd_t, states, dy_r)


CONV_SLAB = 256
_CONV_SLABS = [slice(i * CONV_SLAB, (i + 1) * CONV_SLAB) for i in range(SSD_CD // CONV_SLAB)]


def _conv_taps(ext, w):
    acc = w[SSD_CONV - 1:SSD_CONV] * ext
    for j in range(SSD_CONV - 1):
        acc = acc + w[j:j + 1] * pltpu.roll(ext, SSD_CONV - 1 - j, axis=0)
    return acc


def _conv_fwd(xbc_raw, conv_w, conv_b, name):
    s = xbc_raw.shape[0]
    t8 = CONV_TILE // 8

    def body(cur_ref, prev_ref, w_ref, b_ref, o_ref, slope_ref):
        keep = (pl.program_id(0) > 0).astype(F32)
        for sl in _CONV_SLABS:
            ext = jnp.concatenate([prev_ref[:, sl] * keep, cur_ref[:, sl]], axis=0)
            u = _conv_taps(ext, w_ref[:, sl])[8:] + b_ref[:, sl]
            sg = _sigmoid(u)
            o_ref[:, sl] = u * sg
            slope_ref[:, sl] = _bf(sg * (1.0 + u * (1.0 - sg)))

    out = jax.ShapeDtypeStruct((s, SSD_CD), F32)
    return pl.pallas_call(
        body, out_shape=(out, jax.ShapeDtypeStruct((s, SSD_CD), BF16)), grid=(s // CONV_TILE,),
        in_specs=[_row_spec(SSD_CD, CONV_TILE),
                  pl.BlockSpec((8, SSD_CD), lambda i: (jnp.maximum(i * t8 - 1, 0), 0)),
                  _full_spec((SSD_CONV, SSD_CD)), _full_spec((1, SSD_CD))],
        out_specs=(_row_spec(SSD_CD, CONV_TILE), _row_spec(SSD_CD, CONV_TILE)),
        compiler_params=_params("parallel"), name=name)(xbc_raw, xbc_raw, conv_w, conv_b)


def _conv_bwd(xbc_raw, conv_w, slope, dact, name):
    s = xbc_raw.shape[0]
    nt = s // CONV_TILE
    t8 = CONV_TILE // 8
    rows = CONV_TILE + 8

    def body(cur_ref, prev_ref, w_ref, s_ref, snext_ref, d_ref, dnext_ref, dx_ref, dw_ref, db_ref):
        i = pl.program_id(0)

        @pl.when(i == 0)
        def _():
            dw_ref[...] = jnp.zeros_like(dw_ref)
            db_ref[...] = jnp.zeros_like(db_ref)

        keep_prev = (i > 0).astype(F32)
        keep_next = (i < nt - 1).astype(F32)
        for sl in _CONV_SLABS:
            w = w_ref[:, sl]
            cur = cur_ref[:, sl]
            ext = jnp.concatenate([prev_ref[:, sl] * keep_prev, cur], axis=0)
            duc = d_ref[:, sl] * s_ref[:, sl].astype(F32)
            after = dnext_ref[:, sl] * snext_ref[:, sl].astype(F32)[:8] * keep_next
            du = jnp.concatenate([duc, after], axis=0)
            dx = w[SSD_CONV - 1:SSD_CONV] * du
            for j in range(SSD_CONV - 1):
                dx = dx + w[j:j + 1] * pltpu.roll(du, rows - (SSD_CONV - 1 - j), axis=0)
            dx_ref[:, sl] = _bf(dx[:CONV_TILE])
            for j in range(SSD_CONV - 1):
                shifted = pltpu.roll(ext, SSD_CONV - 1 - j, axis=0)[8:]
                dw_ref[j:j + 1, sl] += jnp.sum(duc * shifted, axis=0, keepdims=True)
            dw_ref[SSD_CONV - 1:SSD_CONV, sl] += jnp.sum(duc * cur, axis=0, keepdims=True)
            db_ref[:, sl] += jnp.sum(duc, axis=0, keepdims=True)

    row = _row_spec(SSD_CD, CONV_TILE)
    prev = pl.BlockSpec((8, SSD_CD), lambda i: (jnp.maximum(i * t8 - 1, 0), 0))
    nxt = pl.BlockSpec((8, SSD_CD), lambda i: (jnp.minimum((i + 1) * t8, s // 8 - 1), 0))
    nxt16 = pl.BlockSpec((16, SSD_CD), lambda i: (jnp.minimum((i + 1) * (t8 // 2), s // 16 - 1), 0))
    return pl.pallas_call(
        body,
        out_shape=(jax.ShapeDtypeStruct((s, SSD_CD), BF16), jax.ShapeDtypeStruct((SSD_CONV, SSD_CD), F32),
                   jax.ShapeDtypeStruct((1, SSD_CD), F32)),
        grid=(nt,),
        in_specs=[row, prev, _full_spec((SSD_CONV, SSD_CD)), row, nxt16, row, nxt],
        out_specs=(row, _full_spec((SSD_CONV, SSD_CD)), _full_spec((1, SSD_CD))),
        compiler_params=_params("arbitrary"), name=name)(xbc_raw, xbc_raw, conv_w, slope, slope, dact, dact)


def _head_select():
    e = np.zeros((HEAD_PAD, SSD_W), np.float32)
    for h in range(SSD_HEADS):
        e[h, h * SSD_P:(h + 1) * SSD_P] = 1.0
    return jnp.asarray(e, BF16), jnp.asarray(e.T, BF16)


def _pad_heads(v):
    return jnp.pad(v.reshape(1, SSD_HEADS).astype(F32), ((0, 0), (0, HEAD_PAD - SSD_HEADS)))


def _ssd_masks():
    r = _iota((CHUNK, SSD_QW), 0)
    c = _iota((CHUNK, SSD_QW), 1) % SSD_P
    itile = (r == c).astype(F32)
    ctile = r >= c
    rb = _iota((SSD_QW, SSD_QW), 0) // SSD_P
    cb = _iota((SSD_QW, SSD_QW), 1) // SSD_P
    return itile, ctile, rb == cb


_TILES = [slice(i * SSD_QW, (i + 1) * SSD_QW) for i in range(SSD_W // SSD_QW)]


def _ssd_heads(dtr_ref, dtb_ref, alog_ref):
    u = dtr_ref[...] + dtb_ref[...]
    dt = _softplus(u)
    nexp = -jnp.exp(alog_ref[...])
    return u, dt, nexp, dt * nexp


def _ssd_group(xbc_ref, acol_ref, dte_ref, gs):
    acol = acol_ref[:, gs]
    alast = acol[CHUNK - 1:CHUNK, :]
    xs, dte = xbc_ref[:, gs], dte_ref[:, gs]
    return dict(xs=xs, dte=dte, xdt=xs * dte, ea=jnp.exp(acol), tail=jnp.exp(alast - acol), eal=jnp.exp(alast))


def _silu_gate(z_ref, y_ref, tl):
    zv = z_ref[:, tl]
    sz = _sigmoid(zv)
    return zv, sz, y_ref[:, tl] * (zv * sz)


def _tile4(x):
    return jnp.concatenate([x, x, x, x], axis=0)


def _fold4(x):
    return x[0:CHUNK] + x[CHUNK:2 * CHUNK] + x[2 * CHUNK:3 * CHUNK] + x[3 * CHUNK:4 * CHUNK]


def _ssd_specs(steps, rev):
    cidx = (lambda c: steps - 1 - c) if rev else (lambda c: c)
    return dict(
        xbc=pl.BlockSpec((STEP_ROWS, SSD_CD), lambda c: (cidx(c), 0)),
        dt=pl.BlockSpec((STEP_ROWS, HEAD_PAD), lambda c: (cidx(c), 0)),
        wide=pl.BlockSpec((STEP_ROWS, SSD_W), lambda c: (cidx(c), 0)),
        state=pl.BlockSpec((CHUNKS_PER_STEP, SSD_N, SSD_W), lambda c: (cidx(c), 0, 0)),
        head=_full_spec((1, HEAD_PAD)),
        roww=_full_spec((1, SSD_W)),
        e=_full_spec((HEAD_PAD, SSD_W)),
        et=_full_spec((SSD_W, HEAD_PAD)),
    )


_GROUPS = range(SSD_GROUPS)
_GROUP_LANES = [slice(g * SSD_GW, (g + 1) * SSD_GW) for g in _GROUPS]
_QUADS = [(g, slice(g * SSD_GW + q * SSD_QW, g * SSD_GW + (q + 1) * SSD_QW), slice(q * SSD_QW, (q + 1) * SSD_QW))
          for g in _GROUPS for q in range(SSD_GW // SSD_QW)]


def _ssd_bc(xbc_ref):
    bg = [_bf(xbc_ref[:, B_OFF + g * SSD_N:B_OFF + (g + 1) * SSD_N]) for g in _GROUPS]
    cg = [_bf(xbc_ref[:, C_OFF + g * SSD_N:C_OFF + (g + 1) * SSD_N]) for g in _GROUPS]
    return bg, cg


def _ssd_decay(aq, itile, ctile):
    arow = jnp.sum(aq * itile, axis=0, keepdims=True)
    return jnp.exp(jnp.where(ctile, aq - arow, -jnp.inf))


def _ssd_blockdiag(xq, bdmask):
    return jnp.where(bdmask, _tile4(_bf(xq)), jnp.zeros((), BF16))


def _ssd_fwd(xbc, dt_raw, z, dt_bias, a_log, dske, norm_w, name):
    s = xbc.shape[0]
    nc = s // CHUNK
    e_sel, _ = _head_select()

    def body(xbc_ref, dtr_ref, z_ref, dtb_ref, alog_ref, dske_ref, nw_ref, e_ref,
             yraw_ref, ys_ref, st_ref, acol_ref, dte_ref, state):
        @pl.when(pl.program_id(0) == 0)
        def _():
            state[...] = jnp.zeros_like(state)

        for sub in range(CHUNKS_PER_STEP):
            rows = _chunk_rows(sub)
            chunk(xbc_ref.at[rows], dtr_ref.at[rows], z_ref.at[rows], dtb_ref, alog_ref, dske_ref, nw_ref, e_ref,
                  yraw_ref.at[rows], ys_ref.at[rows], st_ref.at[sub], acol_ref.at[rows], dte_ref.at[rows], state)

    def chunk(xbc_ref, dtr_ref, z_ref, dtb_ref, alog_ref, dske_ref, nw_ref, e_ref,
              yraw_ref, ys_ref, st_ref, acol_ref, dte_ref, state):
        _, dt, _, a = _ssd_heads(dtr_ref, dtb_ref, alog_ref)
        tril = _bf((_iota((CHUNK, CHUNK), 0) >= _iota((CHUNK, CHUNK), 1)).astype(F32))
        ac3, dt3 = _split3(_dot_exact_r(tril, a)), _split3(dt)
        for tl in _TILES:
            e_t = e_ref[:, tl]
            acol_ref[:, tl] = _dot(ac3[0], e_t) + _dot(ac3[1], e_t) + _dot(ac3[2], e_t)
            dte_ref[:, tl] = _dot(dt3[0], e_t) + _dot(dt3[1], e_t) + _dot(dt3[2], e_t)

        itile, ctile, bdmask = _ssd_masks()
        st_ref[...] = state[...]
        bg, cg = _ssd_bc(xbc_ref)
        for g0 in range(0, SSD_GROUPS, SSD_BLOCK_GROUPS):
            gids = range(g0, g0 + SSD_BLOCK_GROUPS)
            quads = [(g, ql, qs) for g, ql, qs in _QUADS if g in gids]
            q = {g: _ssd_group(xbc_ref, acol_ref, dte_ref, _GROUP_LANES[g]) for g in gids}
            stg = {g: state[:, _GROUP_LANES[g]] for g in gids}
            cbt = {g: _dot_nt(cg[g], _tile4(bg[g])) for g in gids}
            ys = {g: _dot(cg[g], _bf(stg[g])) for g in gids}
            dq = [_ssd_decay(acol_ref[:, ql], itile, ctile) for _, ql, _ in quads]
            xbd = [_ssd_blockdiag(q[g]["xdt"][:, qs], bdmask) for g, _, qs in quads]
            yq = [_dot(_bf(cbt[g] * dq[i]), xbd[i]) + ys[g][:, qs] * q[g]["ea"][:, qs]
                  + dske_ref[:, ql] * q[g]["xs"][:, qs] for i, (g, ql, qs) in enumerate(quads)]
            new_st = {g: stg[g] * q[g]["eal"] + _dot_tn(bg[g], _bf(q[g]["xdt"] * q[g]["tail"])) for g in gids}
            for i, (_, ql, _) in enumerate(quads):
                yraw_ref[:, ql] = yq[i]
            for g in gids:
                state[:, _GROUP_LANES[g]] = new_st[g]

        sq = jnp.zeros((CHUNK, SSD_QW), F32)
        for tl in _TILES:
            t = _silu_gate(z_ref, yraw_ref, tl)[2]
            sq = sq + t * t
        rstd = lax.rsqrt(jnp.sum(sq, axis=1, keepdims=True) * (1.0 / SSD_W) + EPS)
        for tl in _TILES:
            ys_ref[:, tl] = _bf(_silu_gate(z_ref, yraw_ref, tl)[2] * rstd * nw_ref[:, tl])

    steps = nc // CHUNKS_PER_STEP
    sp = _ssd_specs(steps, False)
    wide = jax.ShapeDtypeStruct((s, SSD_W), F32)
    return pl.pallas_call(
        body,
        out_shape=(wide, jax.ShapeDtypeStruct((s, SSD_W), BF16), jax.ShapeDtypeStruct((nc, SSD_N, SSD_W), F32),
                   wide, wide),
        grid=(steps,),
        in_specs=[sp["xbc"], sp["dt"], sp["wide"], sp["head"], sp["head"], sp["roww"], sp["roww"], sp["e"]],
        out_specs=(sp["wide"], sp["wide"], sp["state"], sp["wide"], sp["wide"]),
        scratch_shapes=[pltpu.VMEM((SSD_N, SSD_W), F32)],
        compiler_params=_params("arbitrary"), name=name,
    )(xbc, dt_raw, z, dt_bias, a_log, dske, norm_w, e_sel)


def _ssd_bwd(xbc, dt_raw, z, yraw, states, dys, acol, dte, dt_bias, a_log, dske, norm_w, name, rider):
    s = xbc.shape[0]
    nc = s // CHUNK
    steps = nc // CHUNKS_PER_STEP
    _, et_sel = _head_select()
    n_in, n_out, n_scratch = 13, 7, 5
    n_rin, n_rout = len(rider.inputs), len(rider.out_shapes)

    def body(*refs):
        ins, refs = refs[:n_in], refs[n_in:]
        r_ins, refs = refs[:n_rin], refs[n_rin:]
        outs, refs = refs[:n_out], refs[n_out:]
        r_outs, refs = refs[:n_rout], refs[n_rout:]
        ride = functools.partial(rider.run, pl.program_id(0), steps, r_ins, r_outs, refs[n_scratch],
                                 refs[n_scratch + 1])
        ride(last=False)
        compute(*ins, *outs, *refs[:n_scratch])
        ride(last=True)

    def compute(xbc_ref, dtr_ref, z_ref, yraw_ref, st_ref, dys_ref, acol_ref, dte_ref, dtb_ref, alog_ref,
                dske_ref, nw_ref, et_ref, dxbc_ref, dz_ref, ddtr_ref, ddtb_ref, dalog_ref, ddsk_ref, dnw_ref,
                dstate, dsk_acc, dy_s, dacol_s, ddte_s):
        @pl.when(pl.program_id(0) == 0)
        def _():
            dstate[...] = jnp.zeros_like(dstate)
            dsk_acc[...] = jnp.zeros_like(dsk_acc)
            ddtb_ref[...] = jnp.zeros_like(ddtb_ref)
            dalog_ref[...] = jnp.zeros_like(dalog_ref)
            dnw_ref[...] = jnp.zeros_like(dnw_ref)

        for sub in reversed(range(CHUNKS_PER_STEP)):
            rows = _chunk_rows(sub)
            chunk(xbc_ref.at[rows], dtr_ref.at[rows], z_ref.at[rows], yraw_ref.at[rows], st_ref.at[sub],
                  dys_ref.at[rows], acol_ref.at[rows], dte_ref.at[rows], dtb_ref, alog_ref, dske_ref, nw_ref, et_ref,
                  dxbc_ref.at[rows], dz_ref.at[rows], ddtr_ref.at[rows], ddtb_ref, dalog_ref, dnw_ref,
                  dstate, dsk_acc, dy_s, dacol_s, ddte_s)

        @pl.when(pl.program_id(0) == steps - 1)
        def _():
            ddsk_ref[...] = _dot_exact_l(jnp.broadcast_to(dsk_acc[...], (8, SSD_W)), et_ref[...])[0:1]

    def chunk(xbc_ref, dtr_ref, z_ref, yraw_ref, st_ref, dys_ref, acol_ref, dte_ref, dtb_ref, alog_ref,
              dske_ref, nw_ref, et_ref, dxbc_ref, dz_ref, ddtr_ref, ddtb_ref, dalog_ref, dnw_ref,
              dstate, dsk_acc, dy_s, dacol_s, ddte_s):
        itile, ctile, bdmask = _ssd_masks()
        last_row = (_iota((CHUNK, 1), 0) == CHUNK - 1).astype(F32)

        s1 = jnp.zeros((CHUNK, SSD_QW), F32)
        s2 = jnp.zeros((CHUNK, SSD_QW), F32)
        for tl in _TILES:
            t = _silu_gate(z_ref, yraw_ref, tl)[2]
            s1 = s1 + t * t
            s2 = s2 + dys_ref[:, tl] * nw_ref[:, tl] * t
        rstd = lax.rsqrt(jnp.sum(s1, axis=1, keepdims=True) * (1.0 / SSD_W) + EPS)
        back = rstd * rstd * rstd * jnp.sum(s2, axis=1, keepdims=True) * (1.0 / SSD_W)
        for tl in _TILES:
            zv, sz, t = _silu_gate(z_ref, yraw_ref, tl)
            dysv = dys_ref[:, tl]
            dt_ = rstd * (dysv * nw_ref[:, tl]) - t * back
            dnw_ref[:, tl] += jnp.sum(dysv * t * rstd, axis=0, keepdims=True)
            dz_ref[:, tl] = _bf(dt_ * yraw_ref[:, tl] * (sz * (1.0 + zv * (1.0 - sz))))
            dy_t = dt_ * (zv * sz)
            dy_s[:, tl] = dy_t
            dsk_acc[:, tl] += jnp.sum(dy_t * xbc_ref[:, tl], axis=0, keepdims=True)

        gl = _GROUP_LANES
        bg, cg = _ssd_bc(xbc_ref)
        for g0 in range(0, SSD_GROUPS, SSD_BLOCK_GROUPS):
            gids = range(g0, g0 + SSD_BLOCK_GROUPS)
            quads = [(i, g, ql, qs) for i, (g, ql, qs) in enumerate(_QUADS) if g in gids]
            q = {g: _ssd_group(xbc_ref, acol_ref, dte_ref, gl[g]) for g in gids}
            stg = {g: st_ref[:, gl[g]] for g in gids}
            dsn = {g: dstate[:, gl[g]] for g in gids}
            stgb, dsnb = {g: _bf(stg[g]) for g in gids}, {g: _bf(dsn[g]) for g in gids}
            btile = {g: _tile4(bg[g]) for g in gids}
            cbt = {g: _dot_nt(cg[g], btile[g]) for g in gids}
            dyg = {g: dy_s[:, gl[g]] for g in gids}
            eag, tailg = {g: q[g]["ea"] for g in gids}, {g: q[g]["tail"] for g in gids}
            xdtg, ealg = {g: q[g]["xdt"] for g in gids}, {g: q[g]["eal"] for g in gids}
            dys_g = {g: _bf(dyg[g] * eag[g]) for g in gids}
            ysg = {g: _dot(cg[g], stgb[g]) for g in gids}
            dc0 = {g: _dot_nt(dys_g[g], stgb[g]) for g in gids}
            dst = {g: _dot_tn(cg[g], dys_g[g]) for g in gids}
            dwt = {g: _dot(bg[g], dsnb[g]) for g in gids}
            db0 = {g: _dot_nt(_bf(xdtg[g] * tailg[g]), dsnb[g]) for g in gids}
            dtl = {g: dwt[g] * xdtg[g] * tailg[g] for g in gids}
            dal_row = {g: jnp.sum(dtl[g], axis=0, keepdims=True)
                       + jnp.sum(dsn[g] * stg[g], axis=0, keepdims=True) * ealg[g] for g in gids}
            for g in gids:
                dstate[:, gl[g]] = dst[g] + dsn[g] * ealg[g]
            dq = {i: _ssd_decay(acol_ref[:, ql], itile, ctile) for i, _, ql, _ in quads}
            mq = {i: cbt[g] * dq[i] for i, g, _, _ in quads}
            xbd = {i: _ssd_blockdiag(xdtg[g][:, qs], bdmask) for i, g, _, qs in quads}
            dyq = {i: _bf(dyg[g][:, qs]) for i, g, _, qs in quads}
            dm = {i: _dot_nt(dyq[i], xbd[i]) for i, _, _, _ in quads}
            dxdt_q = {i: _fold4(jnp.where(bdmask, _dot_tn(_bf(mq[i]), dyq[i]), 0.0)) for i, _, _, _ in quads}
            eq = {i: dm[i] * mq[i] for i, _, _, _ in quads}
            dacol_q = {i: eq[i] - itile * jnp.sum(eq[i], axis=0, keepdims=True) for i, _, _, _ in quads}
            dcbt = {i: _bf(dm[i] * dq[i]) for i, _, _, _ in quads}
            dc_q = {i: _dot(dcbt[i], btile[g]) for i, g, _, _ in quads}
            db_q = {i: _fold4(_dot_tn(dcbt[i], cg[g])) for i, g, _, _ in quads}
            for g in gids:
                dxdt_g = jnp.concatenate([dxdt_q[2 * g], dxdt_q[2 * g + 1]], axis=1) + dwt[g] * tailg[g]
                dxbc_ref[:, gl[g]] = dyg[g] * dske_ref[:, gl[g]] + dxdt_g * q[g]["dte"]
                ddte_s[:, gl[g]] = dxdt_g * q[g]["xs"]
                dacol_s[:, gl[g]] = (jnp.concatenate([dacol_q[2 * g], dacol_q[2 * g + 1]], axis=1)
                                     + dyg[g] * (ysg[g] * eag[g]) - dtl[g] + last_row * dal_row[g])
                dxbc_ref[:, B_OFF + g * SSD_N:B_OFF + (g + 1) * SSD_N] = db0[g] + db_q[2 * g] + db_q[2 * g + 1]
                dxbc_ref[:, C_OFF + g * SSD_N:C_OFF + (g + 1) * SSD_N] = dc0[g] + dc_q[2 * g] + dc_q[2 * g + 1]

        dacum = jnp.zeros((CHUNK, HEAD_PAD), F32)
        ddt = jnp.zeros((CHUNK, HEAD_PAD), F32)
        for i, tl in enumerate(_TILES):
            et_t = et_ref[i * SSD_QW:(i + 1) * SSD_QW, :]
            dacum = dacum + _dot_exact_l(dacol_s[:, tl], et_t, pieces=2)
            ddt = ddt + _dot_exact_l(ddte_s[:, tl], et_t, pieces=2)
        u, _, nexp, a = _ssd_heads(dtr_ref, dtb_ref, alog_ref)
        triu = _bf((_iota((CHUNK, CHUNK), 1) >= _iota((CHUNK, CHUNK), 0)).astype(F32))
        da = _dot_exact_r(triu, dacum)
        ddt = ddt + da * nexp
        dalog_ref[...] += jnp.sum(da * a, axis=0, keepdims=True)
        du = ddt * _sigmoid(u)
        ddtr_ref[...] = _bf(du)
        ddtb_ref[...] += jnp.sum(du, axis=0, keepdims=True)

    sp = _ssd_specs(steps, True)
    return pl.pallas_call(
        body,
        out_shape=(jax.ShapeDtypeStruct((s, SSD_CD), F32), jax.ShapeDtypeStruct((s, SSD_W), BF16),
                   jax.ShapeDtypeStruct((s, HEAD_PAD), BF16), jax.ShapeDtypeStruct((1, HEAD_PAD), F32),
                   jax.ShapeDtypeStruct((1, HEAD_PAD), F32), jax.ShapeDtypeStruct((1, HEAD_PAD), F32),
                   jax.ShapeDtypeStruct((1, SSD_W), F32), *rider.out_shapes),
        grid=(steps,),
        in_specs=[sp["xbc"], sp["dt"], sp["wide"], sp["wide"], sp["state"], sp["wide"], sp["wide"], sp["wide"],
                  sp["head"], sp["head"], sp["roww"], sp["roww"], sp["et"], *rider.in_specs()],
        out_specs=(sp["xbc"], sp["wide"], sp["dt"], sp["head"], sp["head"], sp["head"], sp["roww"],
                   *rider.out_specs()),
        scratch_shapes=[pltpu.VMEM((SSD_N, SSD_W), F32), pltpu.VMEM((1, SSD_W), F32),
                        *[pltpu.VMEM((CHUNK, SSD_W), F32)] * 3, *rider.scratch()],
        compiler_params=_params("arbitrary"), name=name,
    )(xbc, dt_raw, z, yraw, states, dys, acol, dte, dt_bias, a_log, dske, norm_w, et_sel, *rider.inputs)


W_IN_SHARD = IN_PROJ // N_DEV
W_IN_ROW_PARTS = ((0, 1152), (1152, 832), (1984, 840))


def _device_step(x, pos_col, target, norm1_w, conv_w_shard, conv_b, dt_bias, a_log, d_skip, ssd_norm_w, norm_f_w,
                 w_in_shard, br_ret_shard, br_ssd_shard, out_shard):
    inv_freq = jnp.asarray(ROPE_THETA ** (-np.arange(ROPE_HALF, dtype=np.float64) / ROPE_HALF), F32).reshape(1, ROPE_HALF)
    dtb, alog = _pad_heads(dt_bias), _pad_heads(a_log)
    dske = jnp.repeat(d_skip.reshape(SSD_HEADS).astype(F32), SSD_P).reshape(1, SSD_W)
    my_core = lax.axis_index("c")

    h, w_all = _rmsnorm_fwd(x, norm1_w, "rmsnorm1_fwd", _gather_rider([w_in_shard], 1.0, 1.0))
    w_all = w_all.reshape(IN_PROJ, D_MODEL)
    w_dt = jnp.pad(w_all[OFF_DT:OFF_G], ((0, HEAD_PAD - SSD_HEADS), (0, 0)))
    w_g = w_all[OFF_G:]
    rows_r, rows_z, rows_xbc = (0, OFF_Z), (OFF_Z, SSD_W), (OFF_XBC, SSD_CD)
    gather = _gather_rider([br_ret_shard, br_ssd_shard, out_shard, conv_w_shard], relay_at=0.35, pass_at=0.6)
    p_r, all_ret, all_ssd, all_out, all_conv = _matmul(h, w_all, "nt", F32, "proj_ret", rider=gather, b_rows=rows_r)
    w_br_ret = all_ret.reshape(RET_W, D_MODEL)
    w_br_ssd = all_ssd.reshape(SSD_W, D_MODEL)
    w_out = all_out.reshape(D_MODEL, D_MODEL)
    conv_w = all_conv.transpose(1, 0, 2).reshape(SSD_CONV, SSD_CD)
    p_z = _matmul(h, w_all, "nt", F32, "proj_z", b_rows=rows_z)
    p_xbc = _matmul(h, w_all, "nt", F32, "proj_xbc", b_rows=rows_xbc)
    p_dt = _matmul(h, w_dt, "nt", F32, "proj_dt")
    p_g = _matmul(h, w_g, "nt", BF16, "proj_gates")
    cos, sin = _rope_tables(pos_col, inv_freq, "rope_tables")
    y_r, ret_states = _retention_fwd(p_r, cos, sin, "retention_fwd")
    xbc_act, silu_slope = _conv_fwd(p_xbc, conv_w, conv_b, "conv_fwd")
    y_raw, y_s, ssd_states, acol, dte = _ssd_fwd(xbc_act, p_dt, p_z, dtb, alog, dske, ssd_norm_w, "ssd_fwd")
    pr = _matmul(y_r, w_br_ret, "nn", BF16, "branch_ret")
    ps = _matmul(y_s, w_br_ssd, "nn", BF16, "branch_ssd")
    merged = _merge_fwd(p_g, pr, ps, "merge_fwd")
    o = _matmul(merged, w_out, "nn", F32, "out_proj")
    dx2, g_norm_f, loss_acc = _final_fwd_bwd(x, o, norm_f_w, target, "final_norm_loss")

    g_w_out = _matmul(merged, dx2, "tn", BF16, "grad_w_out")
    dmerged = _matmul(dx2, w_out, "nt", F32, "d_merged")
    dpr, dps, dp_g = _merge_bwd(p_g, pr, ps, dmerged, "merge_bwd")
    g_w_br_ret = _matmul(y_r, dpr, "tn", BF16, "grad_w_br_ret")
    g_w_br_ssd = _matmul(y_s, dps, "tn", BF16, "grad_w_br_ssd")
    dy_r = _matmul(dpr, w_br_ret, "nt", F32, "d_y_ret")
    dy_s = _matmul(dps, w_br_ssd, "nt", F32, "d_y_ssd")
    scatter = _scatter_rider([g_w_out.reshape(N_DEV, -1, D_MODEL), g_w_br_ret.reshape(N_DEV, -1, D_MODEL),
                              g_w_br_ssd.reshape(N_DEV, -1, D_MODEL)])
    dxbc_act, dp_z, dp_dt, g_dtb, g_alog, g_dsk, g_ssd_norm, got_out, got_ret, got_ssd = _ssd_bwd(
        xbc_act, p_dt, p_z, y_raw, ssd_states, dy_s, acol, dte, dtb, alog, dske, ssd_norm_w, "ssd_bwd", scatter)
    dp_xbc, g_conv_w, g_conv_b = _conv_bwd(p_xbc, conv_w, silu_slope, dxbc_act, "conv_bwd")
    dp_r = _retention_bwd(p_r, cos, sin, ret_states, dy_r, "retention_bwd")
    g_w_in = jnp.concatenate([
        _matmul(dp_r, h, "tn", BF16, "grad_w_ret"),
        _matmul(dp_z, h, "tn", BF16, "grad_w_z"),
        _matmul(dp_xbc, h, "tn", BF16, "grad_w_xbc"),
        _matmul(dp_dt, h, "tn", BF16, "grad_w_dt")[:SSD_HEADS],
        _matmul(dp_g, h, "tn", BF16, "grad_w_gates"),
    ], axis=0)
    blocks = g_w_in.reshape(N_DEV, W_IN_SHARD, D_MODEL)
    dh, from_sibling = _matmul(dp_r, w_all, "nn", F32, "d_h_ret", rider=_sibling_rider(blocks), b_rows=rows_r)
    chip_sum = _chip_sum(blocks, from_sibling, my_core, "w_in_chip_sum")
    carriers = (("d_h_xbc", dp_xbc, w_all, rows_xbc), ("d_h_gates", dp_g, w_g, None), ("d_h_z", dp_z, w_all, rows_z))
    landed = []
    for (row0, rows), (nm, dp, w, b_rows) in zip(W_IN_ROW_PARTS, carriers):
        dt_pair = (dp_dt, w_dt) if nm == "d_h_z" else None
        dh, got = _matmul(dp, w, "nn", F32, nm, add=dh, rider=_chip_rider(chip_sum, row0, rows), b_rows=b_rows,
                          extra=dt_pair)
        landed.append(got)
    grad_x, g_norm1 = _rmsnorm_bwd(x, norm1_w, dh, dx2, "rmsnorm1_bwd")
    small = dict(norm1_w=g_norm1, conv_w=g_conv_w, conv_b=g_conv_b, dt_bias=g_dtb[:, :SSD_HEADS],
                 a_log=g_alog[:, :SSD_HEADS], d_skip=g_dsk[:, :SSD_HEADS], ssd_norm_w=g_ssd_norm,
                 norm_f_w=g_norm_f)
    big = dict(w_in=_sum_row_parts(landed, "w_in_sum"), w_br_ret=_sum_slots(got_ret, "w_br_ret_sum"),
               w_br_ssd=_sum_slots(got_ssd, "w_br_ssd_sum"), w_out=_sum_slots(got_out, "w_out_sum"))
    return loss_acc[0, 0], grad_x, small, big


def _all_reduce_small(vec, name):
    r, c = vec.shape

    def body(x_ref, out_ref, land, send_sems, recv_sems):
        x, y, cc = _mesh_pos()
        my_idx = 4 * x + 2 * y + cc
        land[my_idx] = x_ref[...]
        copies = []
        for k in range(1, N_DEV):
            px, py, pc = x ^ (k >> 2), y ^ ((k >> 1) & 1), cc ^ (k & 1)
            cp = pltpu.make_async_remote_copy(
                src_ref=x_ref, dst_ref=land.at[my_idx],
                send_sem=send_sems.at[k - 1], recv_sem=recv_sems.at[k - 1],
                device_id=(px, py, pc), device_id_type=MESH)
            cp.start()
            copies.append(cp)
        for k in range(1, N_DEV):
            px, py, pc = x ^ (k >> 2), y ^ ((k >> 1) & 1), cc ^ (k & 1)
            pltpu.make_async_remote_copy(
                src_ref=x_ref, dst_ref=land.at[4 * px + 2 * py + pc],
                send_sem=send_sems.at[k - 1], recv_sem=recv_sems.at[k - 1],
                device_id=(px, py, pc), device_id_type=MESH).wait_recv()
        for cp in copies:
            cp.wait_send()
        acc = land[0]
        for i in range(1, N_DEV):
            acc = acc + land[i]
        out_ref[...] = acc

    return pl.pallas_call(
        body,
        out_shape=jax.ShapeDtypeStruct((r, c), F32),
        in_specs=[pl.BlockSpec(memory_space=pltpu.VMEM)],
        out_specs=pl.BlockSpec(memory_space=pltpu.VMEM),
        scratch_shapes=[pltpu.VMEM((N_DEV, r, c), F32), pltpu.SemaphoreType.DMA((7,)),
                        pltpu.SemaphoreType.DMA((7,))],
        name=name)(vec)


_SMALL = ("norm1_w", "conv_w", "conv_b", "dt_bias", "a_log", "d_skip", "ssd_norm_w", "norm_f_w")
_SMALL_COLS = 128
_W_IN = "w_in"
_WEIGHTS = ("norm1_w", "w_in", "conv_w", "conv_b", "dt_bias", "a_log", "d_skip", "ssd_norm_w",
            "w_br_ret", "w_br_ssd", "w_out", "norm_f_w")


def _pack(parts):
    flat = jnp.concatenate([p.reshape(-1).astype(F32) for p in parts])
    rows = -(-flat.shape[0] // (8 * _SMALL_COLS)) * 8
    return jnp.pad(flat, (0, rows * _SMALL_COLS - flat.shape[0])).reshape(rows, _SMALL_COLS)


def _unpack(packed, shapes):
    flat = packed.reshape(-1)
    out, off = [], 0
    for shp in shapes:
        n = int(np.prod(shp))
        out.append(flat[off:off + n].reshape(shp))
        off += n
    return out


def kernel(x, positions, norm1_w, w_in, conv_w, conv_b, dt_bias, a_log, d_skip, ssd_norm_w, w_br_ret, w_br_ssd, w_out, norm_f_w, loss_target, m_norm1_w, m_w_in, m_conv_w, m_conv_b, m_dt_bias, m_a_log, m_d_skip, m_ssd_norm_w, m_w_br_ret, m_w_br_ssd, m_w_out, m_norm_f_w, v_norm1_w, v_w_in, v_conv_w, v_conv_b, v_dt_bias, v_a_log, v_d_skip, v_ssd_norm_w, v_w_br_ret, v_w_br_ssd, v_w_out, v_norm_f_w):
    w = dict(norm1_w=norm1_w, w_in=w_in, conv_w=conv_w, conv_b=conv_b, dt_bias=dt_bias, a_log=a_log,
             d_skip=d_skip, ssd_norm_w=ssd_norm_w, w_br_ret=w_br_ret, w_br_ssd=w_br_ssd, w_out=w_out,
             norm_f_w=norm_f_w)
    m = dict(norm1_w=m_norm1_w, w_in=m_w_in, conv_w=m_conv_w, conv_b=m_conv_b, dt_bias=m_dt_bias,
             a_log=m_a_log, d_skip=m_d_skip, ssd_norm_w=m_ssd_norm_w, w_br_ret=m_w_br_ret,
             w_br_ssd=m_w_br_ssd, w_out=m_w_out, norm_f_w=m_norm_f_w)
    v = dict(norm1_w=v_norm1_w, w_in=v_w_in, conv_w=v_conv_w, conv_b=v_conv_b, dt_bias=v_dt_bias,
             a_log=v_a_log, d_skip=v_d_skip, ssd_norm_w=v_ssd_norm_w, w_br_ret=v_w_br_ret,
             w_br_ssd=v_w_br_ssd, w_out=v_w_out, norm_f_w=v_norm_f_w)
    s = x.shape[1]
    my_idx = 4 * lax.axis_index("x") + 2 * lax.axis_index("y") + lax.axis_index("c")

    w[_W_IN], m[_W_IN], v[_W_IN] = w_in[0].T, m_w_in[0].T, v_w_in[0].T

    loss_part, grad_x, g_small, g_big = _device_step(
        x[0], positions.reshape(s, 1), loss_target[0], norm1_w, conv_w[0], conv_b, dt_bias, a_log, d_skip,
        ssd_norm_w, norm_f_w.reshape(1, D_MODEL), _cast_bf16(w[_W_IN], "cast_w_in"),
        _cast_bf16(w_br_ret[0], "cast_w_br_ret"), _cast_bf16(w_br_ssd[0], "cast_w_br_ssd"),
        _cast_bf16(w_out[0], "cast_w_out"))

    loss = lax.psum(loss_part, ("x", "y", "c"))

    small_shapes = [g_small[n].shape for n in _SMALL]
    summed = _unpack(_all_reduce_small(_pack([g_small[n] for n in _SMALL]), "allreduce_small"), small_shapes)
    grads = dict(zip(_SMALL, summed))
    conv_cols = SSD_CD // N_DEV
    grads["conv_w"] = lax.dynamic_slice_in_dim(grads["conv_w"], my_idx * conv_cols, conv_cols, axis=1)
    grads["norm_f_w"] = grads["norm_f_w"].reshape(D_MODEL)
    for n in ("norm1_w", "conv_w", "conv_b", "dt_bias", "a_log", "d_skip", "ssd_norm_w"):
        grads[n] = grads[n].reshape(w[n].shape)

    delta, new_m, new_v = {}, {}, {}
    for n in ("w_br_ret", "w_br_ssd", "w_out"):
        w[n], m[n], v[n] = w[n][0], m[n][0], v[n][0]
    for n in (_W_IN, "w_br_ret", "w_br_ssd", "w_out"):
        back = (lambda a: a.T[None]) if n == _W_IN else (lambda a: a[None])
        res = _adamw(w[n], g_big[n], m[n], v[n], "adamw_" + n)
        grads[n] = back(g_big[n])
        delta[n], new_m[n], new_v[n] = (back(a) for a in res)
    shapes = [w[n].shape for n in _SMALL]
    packed = _adamw(_pack([w[n] for n in _SMALL]), _pack([grads[n] for n in _SMALL]),
                    _pack([m[n] for n in _SMALL]), _pack([v[n] for n in _SMALL]), "adamw_small")
    for res, dst in zip(packed, (delta, new_m, new_v)):
        for n, a in zip(_SMALL, _unpack(res, shapes)):
            dst[n] = a

    return (loss, grad_x.reshape(x.shape), *[grads[n] for n in _WEIGHTS], *[delta[n] for n in _WEIGHTS],
            *[new_m[n] for n in _WEIGHTS], *[new_v[n] for n in _WEIGHTS])
```

```python
import functools

import numpy as np
import jax
import jax.numpy as jnp
from jax import lax
from jax.experimental import pallas as pl
from jax.experimental.pallas import tpu as pltpu

F32 = jnp.float32
BF16 = jnp.bfloat16

D_MODEL = 2048
CHUNK = 64
CHUNKS_PER_STEP = 2
STEP_ROWS = CHUNK * CHUNKS_PER_STEP
RET_CHUNKS_PER_STEP = 4
RET_STEP_ROWS = CHUNK * RET_CHUNKS_PER_STEP
EPS = 1e-6
N_DEV = 8

RET_HEADS = 8
RET_DK = 256
RET_W = RET_HEADS * RET_DK
ROPE_THETA = 10000.0
ROPE_HALF = RET_DK // 2

SSD_W = 4096
SSD_P = 64
SSD_HEADS = 64
SSD_GROUPS = 8
SSD_N = 128
SSD_GW = SSD_W // SSD_GROUPS
SSD_QW = 256
SSD_BLOCK_GROUPS = 2
SSD_CONV = 4
SSD_CD = SSD_W + 2 * SSD_GROUPS * SSD_N
HEAD_PAD = 128
B_OFF = SSD_W
C_OFF = SSD_W + SSD_GROUPS * SSD_N

ADAM_LR = 0.001
ADAM_B1 = 0.9
ADAM_B2 = 0.999
ADAM_EPS = 1e-08
ADAM_WD = 0.01
ADAM_STEP = 10

SPLITS = (RET_W, RET_W, RET_W, RET_W, SSD_W, SSD_CD, SSD_HEADS, D_MODEL, D_MODEL)
IN_PROJ = sum(SPLITS)
OFF_Z = 4 * RET_W
OFF_XBC = OFF_Z + SSD_W
OFF_DT = OFF_XBC + SSD_CD
OFF_G = OFF_DT + SSD_HEADS

ROW_TILE = 256
CONV_TILE = 128
MM_TILE = 1024
MM_TILE_WIDE = 2048
MM_TK = 2048
VMEM_CAP = 60 << 20

MESH = pl.DeviceIdType.MESH


def _dot(a, b):
    return lax.dot_general(a, b, (((1,), (0,)), ((), ())), preferred_element_type=F32)


def _dot_nt(a, b):
    return lax.dot_general(a, b, (((1,), (1,)), ((), ())), preferred_element_type=F32)


def _dot_tn(a, b):
    return lax.dot_general(a, b, (((0,), (0,)), ((), ())), preferred_element_type=F32)


def _bf(x):
    return x.astype(BF16)


def _split3(x):
    hi = x.astype(BF16)
    r = x - hi.astype(F32)
    mid = r.astype(BF16)
    lo = (r - mid.astype(F32)).astype(BF16)
    return hi, mid, lo


def _dot_exact_l(x, sel, pieces=3):
    hi, mid, lo = _split3(x)
    r = _dot(hi, sel) + _dot(mid, sel)
    return r + _dot(lo, sel) if pieces == 3 else r


def _dot_exact_r(sel, x):
    hi, mid, lo = _split3(x)
    return _dot(sel, hi) + _dot(sel, mid) + _dot(sel, lo)


def _sigmoid(x):
    return 1.0 / (1.0 + jnp.exp(-x))


def _softplus(x):
    return jnp.maximum(x, 0.0) + jnp.log(1.0 + jnp.exp(-jnp.abs(x)))


def _iota(shape, axis):
    return lax.broadcasted_iota(jnp.int32, shape, axis)


def _params(*sem):
    return pltpu.CompilerParams(dimension_semantics=sem)


class _Rider:
    def __init__(self, inputs, out_shapes, n_sems, phases):
        self.inputs, self.out_shapes, self.n_sems, self.phases = tuple(inputs), tuple(out_shapes), n_sems, phases

    def in_specs(self):
        return [pl.BlockSpec(memory_space=pl.ANY)] * len(self.inputs)

    def out_specs(self):
        return [pl.BlockSpec(memory_space=pl.ANY)] * len(self.out_shapes)

    def scratch(self):
        return [pltpu.SemaphoreType.DMA((self.n_sems,)), pltpu.SemaphoreType.DMA((self.n_sems,))]

    def run(self, step, n_steps, ins, outs, send_sems, recv_sems, last):
        for frac, fn in self.phases:
            if (frac >= 1.0) != last:
                continue
            at = min(int(frac * n_steps), n_steps - 1)

            @pl.when(step == at)
            def _(fn=fn):
                fn(ins, outs, send_sems, recv_sems)


def _mesh_pos():
    return lax.axis_index("x"), lax.axis_index("y"), lax.axis_index("c")


def _gather_rider(shards, relay_at, pass_at):
    n = len(shards)

    def tools(a, ins, outs, send_sems, recv_sems):
        x, y, cc = _mesh_pos()
        nbrs = [(1 - x, y), (x, 1 - y)]
        diag = (1 - x, 1 - y)
        relay_from, relay_to = (x ^ cc, y ^ (1 - cc)), (x ^ (1 - cc), y ^ cc)

        def slot(px, py, pc):
            return outs[a].at[4 * px + 2 * py + pc]

        def copy(k, block, to, src=None):
            return pltpu.make_async_remote_copy(
                src_ref=slot(*block) if src is None else src, dst_ref=slot(*block),
                send_sem=send_sems.at[8 * a + k], recv_sem=recv_sems.at[8 * a + k], device_id=to, device_id_type=MESH)

        me, sibling = (x, y, cc), (x, y, 1 - cc)
        return dict(
            mine=lambda: pltpu.make_async_copy(ins[a], slot(*me), send_sems.at[8 * a + 7]),
            first=lambda: [copy(0, me, sibling, src=ins[a])] + [copy(1 + j, me, (*chip, cc), src=ins[a])
                                                                for j, chip in enumerate(nbrs)],
            landed=lambda: [copy(1 + j, (*chip, cc), me) for j, chip in enumerate(nbrs)],
            relay=lambda: copy(3, (*relay_from, cc), (*relay_to, cc)),
            relayed_in=lambda: copy(3, (*diag, cc), me),
            passed=lambda: [copy(4 + j, (*chip, cc), sibling) for j, chip in enumerate(nbrs)],
            passed_diag=lambda: copy(6, (*diag, cc), sibling),
            from_sibling=lambda: [copy(0, sibling, me)] + [copy(4 + j, (*chip, 1 - cc), me)
                                                           for j, chip in enumerate(nbrs + [diag])])

    def start(*refs):
        for a in range(n):
            t = tools(a, *refs)
            t["mine"]().start()
            for cp in t["first"]():
                cp.start()

    def relay(*refs):
        for a in range(n):
            t = tools(a, *refs)
            for got in t["landed"]():
                got.wait_recv()
            t["relay"]().start()
            for cp in t["passed"]():
                cp.start()

    def pass_diag(*refs):
        for a in range(n):
            t = tools(a, *refs)
            t["relayed_in"]().wait_recv()
            t["passed_diag"]().start()

    def finish(*refs):
        for a in range(n):
            t = tools(a, *refs)
            for cp in t["from_sibling"]():
                cp.wait_recv()
            for cp in t["first"]() + [t["relay"]()] + t["passed"]() + [t["passed_diag"]()]:
                cp.wait_send()
            t["mine"]().wait()

    outs = [jax.ShapeDtypeStruct((N_DEV,) + s.shape, s.dtype) for s in shards]
    return _Rider(shards, outs, 8 * n, ((0.0, start), (relay_at, relay), (pass_at, pass_diag), (1.0, finish)))


def _scatter_rider(blocks):
    n = len(blocks)

    def copies(a, ins, outs, send_sems, recv_sems, with_back):
        x, y, cc = _mesh_pos()
        my_idx = 4 * x + 2 * y + cc
        mine = pltpu.make_async_copy(ins[a].at[my_idx], outs[a].at[my_idx], send_sems.at[8 * a + 7])
        out, back = [], []
        for k in range(1, N_DEV):
            px, py, pc = x ^ (k >> 2), y ^ ((k >> 1) & 1), cc ^ (k & 1)
            peer_idx = 4 * px + 2 * py + pc
            sems = dict(send_sem=send_sems.at[8 * a + k - 1], recv_sem=recv_sems.at[8 * a + k - 1],
                        device_id=(px, py, pc), device_id_type=MESH)
            out.append(pltpu.make_async_remote_copy(src_ref=ins[a].at[peer_idx], dst_ref=outs[a].at[my_idx], **sems))
            if with_back:
                back.append(pltpu.make_async_remote_copy(src_ref=ins[a].at[my_idx], dst_ref=outs[a].at[peer_idx], **sems))
        return mine, out, back

    def start(*refs):
        for a in range(n):
            mine, out, _ = copies(a, *refs, False)
            mine.start()
            for cp in out:
                cp.start()

    def finish(*refs):
        for a in range(n):
            mine, out, back = copies(a, *refs, True)
            for cp in back:
                cp.wait_recv()
            for cp in out:
                cp.wait_send()
            mine.wait()

    outs = [jax.ShapeDtypeStruct(b.shape, b.dtype) for b in blocks]
    return _Rider(blocks, outs, 8 * n, ((0.0, start), (1.0, finish)))


def _sibling_rider(blocks):
    _, r, c = blocks.shape

    def copies(ins, outs, send_sems, recv_sems):
        x, y, cc = _mesh_pos()
        return [pltpu.make_async_remote_copy(
            src_ref=ins[0].at[2 * j + 1 - cc], dst_ref=outs[0].at[j], send_sem=send_sems.at[j],
            recv_sem=recv_sems.at[j], device_id=(x, y, 1 - cc), device_id_type=MESH) for j in range(4)]

    def start(*refs):
        for cp in copies(*refs):
            cp.start()

    def finish(*refs):
        for cp in copies(*refs):
            cp.wait_recv()
        for cp in copies(*refs):
            cp.wait_send()

    return _Rider([blocks], [jax.ShapeDtypeStruct((4, r, c), blocks.dtype)], 4, ((0.0, start), (1.0, finish)))


def _chip_rider(partial, row0, rows):
    _, _, c = partial.shape

    def copies(ins, outs, send_sems, recv_sems, with_back):
        x, y, cc = _mesh_pos()
        my_chip = 2 * x + y

        def rows_of(j):
            return ins[0].at[j, pl.ds(row0, rows)]

        mine = pltpu.make_async_copy(rows_of(my_chip), outs[0].at[my_chip], send_sems.at[3])
        out, back = [], []
        for k in range(1, 4):
            px, py = x ^ (k >> 1), y ^ (k & 1)
            peer = 2 * px + py
            sems = dict(send_sem=send_sems.at[k - 1], recv_sem=recv_sems.at[k - 1], device_id=(px, py, cc),
                        device_id_type=MESH)
            out.append(pltpu.make_async_remote_copy(src_ref=rows_of(peer), dst_ref=outs[0].at[my_chip], **sems))
            if with_back:
                back.append(pltpu.make_async_remote_copy(src_ref=rows_of(my_chip), dst_ref=outs[0].at[peer], **sems))
        return mine, out, back

    def start(*refs):
        mine, out, _ = copies(*refs, False)
        mine.start()
        for cp in out:
            cp.start()

    def finish(*refs):
        mine, out, back = copies(*refs, True)
        for cp in back:
            cp.wait_recv()
        for cp in out:
            cp.wait_send()
        mine.wait()

    return _Rider([partial], [jax.ShapeDtypeStruct((4, rows, c), partial.dtype)], 4, ((0.0, start), (1.0, finish)))


def _matmul(a, b, mode, out_dtype, name, add=None, rider=None, b_rows=None, extra=None):
    b_shape = b.shape if b_rows is None else (b_rows[1], b.shape[1])
    if mode == "nn":
        (m, k), (k2, n) = a.shape, b_shape
    elif mode == "nt":
        (m, k), (n, k2) = a.shape, b_shape
    else:
        (k, m), (k2, n) = a.shape, b_shape
    assert k == k2, (a.shape, b_shape, mode)
    tm, tn, tk = min(m, MM_TILE), min(n, MM_TILE), min(k, MM_TK)
    if n % MM_TILE_WIDE == 0 and (k <= MM_TK or (add is None and jnp.dtype(out_dtype).itemsize == 2)):
        tn = MM_TILE_WIDE
    assert m % tm == 0 and n % tn == 0 and k % tk == 0, (m, n, k)
    nk = k // tk
    b_tile_rows = tn if mode == "nt" else tk
    b_off = 0 if b_rows is None else b_rows[0] // b_tile_rows
    assert b_rows is None or b_rows[0] % b_tile_rows == 0, (b_rows, b_tile_rows)
    dot = {"nn": _dot, "nt": _dot_nt, "tn": _dot_tn}[mode]

    gm, gn = m // tm, n // tn
    n_rin = len(rider.inputs) if rider else 0
    n_rout = len(rider.out_shapes) if rider else 0
    n_add = 1 if add is not None else 0

    n_extra = 2 if extra is not None else 0

    def body(*refs):
        a_ref, b_ref = refs[:2]
        add_ref = refs[2] if n_add else None
        extra_refs = refs[2 + n_add:2 + n_add + n_extra]
        first = 2 + n_add + n_extra
        r_ins = refs[first:first + n_rin]
        o_ref = refs[first + n_rin]
        r_outs = refs[first + n_rin + 1:first + n_rin + 1 + n_rout]
        scratch = refs[first + n_rin + 1 + n_rout:]
        acc_ref = scratch[0] if nk > 1 else None
        kk = pl.program_id(2)
        if rider:
            step = (pl.program_id(0) * gn + pl.program_id(1)) * nk + kk
            ride = functools.partial(rider.run, step, gm * gn * nk, r_ins, r_outs, scratch[-2], scratch[-1])
            ride(last=False)

        def finish(r):
            if extra_refs:
                r = r + _dot(_bf(extra_refs[0][...]), _bf(extra_refs[1][...]))
            if add_ref is not None:
                r = r + add_ref[...]
            o_ref[...] = r.astype(o_ref.dtype)

        if nk == 1:
            finish(dot(_bf(a_ref[...]), _bf(b_ref[...])))
        else:
            @pl.when(kk == 0)
            def _():
                acc_ref[...] = dot(_bf(a_ref[...]), _bf(b_ref[...]))

            @pl.when(jnp.logical_and(kk > 0, kk < nk - 1))
            def _():
                acc_ref[...] += dot(_bf(a_ref[...]), _bf(b_ref[...]))

            @pl.when(kk == nk - 1)
            def _():
                finish(acc_ref[...] + dot(_bf(a_ref[...]), _bf(b_ref[...])))
        if rider:
            ride(last=True)

    if mode == "nn":
        a_spec = pl.BlockSpec((tm, tk), lambda i, j, kk: (i, kk))
        b_spec = pl.BlockSpec((tk, tn), lambda i, j, kk: (kk + b_off, j))
    elif mode == "nt":
        a_spec = pl.BlockSpec((tm, tk), lambda i, j, kk: (i, kk))
        b_spec = pl.BlockSpec((tn, tk), lambda i, j, kk: (j + b_off, kk))
    else:
        a_spec = pl.BlockSpec((tk, tm), lambda i, j, kk: (kk, i))
        b_spec = pl.BlockSpec((tk, tn), lambda i, j, kk: (kk + b_off, j))
    o_spec = pl.BlockSpec((tm, tn), lambda i, j, kk: (i, j))
    in_specs = [a_spec, b_spec] + ([o_spec] if add is not None else [])
    args = (a, b) + ((add,) if add is not None else ())
    if extra is not None:
        (m2, k_extra), (k_extra2, n2) = extra[0].shape, extra[1].shape
        assert (m2, n2, k_extra) == (m, n, k_extra2), (extra[0].shape, extra[1].shape)
        in_specs += [pl.BlockSpec((tm, k_extra), lambda i, j, kk: (i, 0)),
                     pl.BlockSpec((k_extra, tn), lambda i, j, kk: (0, j))]
        args += tuple(extra)
    block_bytes = (tm * tk * a.dtype.itemsize + tk * tn * b.dtype.itemsize
                   + tm * tn * (jnp.dtype(out_dtype).itemsize + (4 if add is not None else 0)))
    vmem = min(2 * block_bytes + 2 * tm * tn * 4 + 2 * (tm + tn) * tk + (4 << 20), VMEM_CAP)
    out_shape = jax.ShapeDtypeStruct((m, n), out_dtype)
    scratch = [pltpu.VMEM((tm, tn), F32)] if nk > 1 else []
    if rider:
        in_specs = in_specs + rider.in_specs()
        args = args + rider.inputs
        out_shape = (out_shape, *rider.out_shapes)
        o_spec = (o_spec, *rider.out_specs())
        scratch = scratch + rider.scratch()
    sem = ("arbitrary",) * 3 if rider else ("parallel", "parallel", "arbitrary")
    return pl.pallas_call(
        body,
        out_shape=out_shape,
        grid=(gm, gn, nk),
        in_specs=in_specs,
        out_specs=o_spec,
        scratch_shapes=scratch,
        compiler_params=pltpu.CompilerParams(dimension_semantics=sem, vmem_limit_bytes=int(vmem)),
        name=name,
    )(*args)


def _row_spec(width, tile=ROW_TILE):
    return pl.BlockSpec((tile, width), lambda i: (i, 0))


def _full_spec(shape):
    nd = len(shape)
    return pl.BlockSpec(shape, lambda *_: (0,) * nd)


def _tiling_2d(r, c):
    if r <= ROW_TILE or r % ROW_TILE == 0:
        tr = min(r, ROW_TILE)
        return (tr, c), r // tr, (lambda i: (i, 0))
    tc = 128 if r > 4096 else 256
    assert c % tc == 0, (r, c)
    return (r, tc), c // tc, (lambda i: (0, i))


def _spec_2d(r, c):
    blk, grid, idx = _tiling_2d(r, c)
    return pl.BlockSpec(blk, idx), grid


def _cast_bf16(x, name):
    r, c = x.shape
    sp, grid = _spec_2d(r, c)

    def body(x_ref, o_ref):
        o_ref[...] = _bf(x_ref[...])

    return pl.pallas_call(
        body, out_shape=jax.ShapeDtypeStruct((r, c), BF16), grid=(grid,), in_specs=[sp], out_specs=sp,
        compiler_params=_params("parallel"), name=name)(x)


def _rmsnorm_fwd(x, w, name, rider):
    s, d = x.shape
    steps = s // ROW_TILE
    n_rin, n_rout = len(rider.inputs), len(rider.out_shapes)

    def body(x_ref, w_ref, *refs):
        r_ins, o_ref, r_outs = refs[:n_rin], refs[n_rin], refs[n_rin + 1:n_rin + 1 + n_rout]
        ride = functools.partial(rider.run, pl.program_id(0), steps, r_ins, r_outs, refs[-2], refs[-1])
        ride(last=False)
        xv = x_ref[...]
        rstd = lax.rsqrt(jnp.mean(xv * xv, axis=1, keepdims=True) + EPS)
        o_ref[...] = _bf(xv * rstd * w_ref[...])
        ride(last=True)

    return pl.pallas_call(
        body, out_shape=(jax.ShapeDtypeStruct((s, d), BF16), *rider.out_shapes), grid=(steps,),
        in_specs=[_row_spec(d), _full_spec((1, d)), *rider.in_specs()],
        out_specs=(_row_spec(d), *rider.out_specs()), scratch_shapes=rider.scratch(),
        compiler_params=_params("arbitrary"), name=name)(x, w, *rider.inputs)


def _rmsnorm_bwd(x, w, dh, dres, name):
    s, d = x.shape

    def body(x_ref, w_ref, dh_ref, dres_ref, dx_ref, dw_ref):
        @pl.when(pl.program_id(0) == 0)
        def _():
            dw_ref[...] = jnp.zeros_like(dw_ref)

        xv = x_ref[...]
        rstd = lax.rsqrt(jnp.mean(xv * xv, axis=1, keepdims=True) + EPS)
        xhat = xv * rstd
        dhv = dh_ref[...]
        dxhat = dhv * w_ref[...]
        dx = rstd * (dxhat - xhat * jnp.mean(dxhat * xhat, axis=1, keepdims=True))
        dx_ref[...] = dx + dres_ref[...]
        dw_ref[...] += jnp.sum(dhv * xhat, axis=0, keepdims=True)

    return pl.pallas_call(
        body,
        out_shape=(jax.ShapeDtypeStruct((s, d), F32), jax.ShapeDtypeStruct((1, d), F32)),
        grid=(s // ROW_TILE,),
        in_specs=[_row_spec(d), _full_spec((1, d)), _row_spec(d), _row_spec(d)],
        out_specs=(_row_spec(d), _full_spec((1, d))),
        compiler_params=_params("arbitrary"), name=name)(x, w, dh, dres)


def _rope_tables(pos_col, inv_freq, name):
    s = pos_col.shape[0]

    def body(p_ref, f_ref, cos_ref, sin_ref):
        ang = p_ref[...].astype(F32) * f_ref[...]
        cos_ref[...] = jnp.cos(ang)
        sin_ref[...] = jnp.sin(ang)

    out = jax.ShapeDtypeStruct((s, ROPE_HALF), F32)
    return pl.pallas_call(
        body, out_shape=(out, out), grid=(s // ROW_TILE,),
        in_specs=[_row_spec(1), _full_spec((1, ROPE_HALF))],
        out_specs=(_row_spec(ROPE_HALF), _row_spec(ROPE_HALF)),
        compiler_params=_params("parallel"), name=name)(pos_col, inv_freq)


def _merge_fwd(pg, p_r, p_s, name):
    s = pg.shape[0]

    def body(g_ref, r_ref, s_ref, o_ref):
        g = g_ref[...].astype(F32)
        o_ref[...] = _bf(_sigmoid(g[:, :D_MODEL]) * r_ref[...].astype(F32)
                         + _sigmoid(g[:, D_MODEL:]) * s_ref[...].astype(F32))

    return pl.pallas_call(
        body, out_shape=jax.ShapeDtypeStruct((s, D_MODEL), BF16), grid=(s // ROW_TILE,),
        in_specs=[_row_spec(2 * D_MODEL), _row_spec(D_MODEL), _row_spec(D_MODEL)],
        out_specs=_row_spec(D_MODEL), compiler_params=_params("parallel"), name=name)(pg, p_r, p_s)


def _merge_bwd(pg, p_r, p_s, dm, name):
    s = pg.shape[0]

    def body(g_ref, r_ref, s_ref, dm_ref, dr_ref, ds_ref, dg_ref):
        g = g_ref[...].astype(F32)
        sr, ss = _sigmoid(g[:, :D_MODEL]), _sigmoid(g[:, D_MODEL:])
        d = dm_ref[...]
        dr_ref[...] = _bf(d * sr)
        ds_ref[...] = _bf(d * ss)
        dg_ref[:, :D_MODEL] = _bf(d * r_ref[...].astype(F32) * sr * (1.0 - sr))
        dg_ref[:, D_MODEL:] = _bf(d * s_ref[...].astype(F32) * ss * (1.0 - ss))

    o = jax.ShapeDtypeStruct((s, D_MODEL), BF16)
    return pl.pallas_call(
        body, out_shape=(o, o, jax.ShapeDtypeStruct((s, 2 * D_MODEL), BF16)), grid=(s // ROW_TILE,),
        in_specs=[_row_spec(2 * D_MODEL), _row_spec(D_MODEL), _row_spec(D_MODEL), _row_spec(D_MODEL)],
        out_specs=(_row_spec(D_MODEL), _row_spec(D_MODEL), _row_spec(2 * D_MODEL)),
        compiler_params=_params("parallel"), name=name)(pg, p_r, p_s, dm)


def _final_fwd_bwd(x, o, w, target, name):
    s, d = x.shape

    def body(x_ref, o_ref, w_ref, t_ref, dx_ref, dw_ref, loss_ref):
        @pl.when(pl.program_id(0) == 0)
        def _():
            dw_ref[...] = jnp.zeros_like(dw_ref)
            loss_ref[...] = jnp.zeros_like(loss_ref)

        x2 = x_ref[...] + o_ref[...]
        rstd = lax.rsqrt(jnp.mean(x2 * x2, axis=1, keepdims=True) + EPS)
        xhat = x2 * rstd
        wv = w_ref[...]
        err = xhat * wv - t_ref[...]
        loss_ref[...] += jnp.sum(jnp.sum(err * err, axis=1, keepdims=True), axis=0, keepdims=True) * (0.5 / d)
        dy = err * (1.0 / d)
        dw_ref[...] += jnp.sum(dy * xhat, axis=0, keepdims=True)
        dxhat = dy * wv
        dx_ref[...] = rstd * (dxhat - xhat * jnp.mean(dxhat * xhat, axis=1, keepdims=True))

    return pl.pallas_call(
        body,
        out_shape=(jax.ShapeDtypeStruct((s, d), F32), jax.ShapeDtypeStruct((1, d), F32),
                   jax.ShapeDtypeStruct((8, 128), F32)),
        grid=(s // ROW_TILE,),
        in_specs=[_row_spec(d), _row_spec(d), _full_spec((1, d)), _row_spec(d)],
        out_specs=(_row_spec(d), _full_spec((1, d)), _full_spec((8, 128))),
        compiler_params=_params("arbitrary"), name=name)(x, o, w, target)


def _adamw(w, g, m, v, name):
    r, c = w.shape
    sp, grid = _spec_2d(r, c)
    c1 = 1.0 / (1.0 - ADAM_B1 ** ADAM_STEP)
    c2 = 1.0 / (1.0 - ADAM_B2 ** ADAM_STEP)

    def body(w_ref, g_ref, m_ref, v_ref, d_ref, nm_ref, nv_ref):
        gv = g_ref[...]
        nm = ADAM_B1 * m_ref[...] + (1.0 - ADAM_B1) * gv
        nv = ADAM_B2 * v_ref[...] + (1.0 - ADAM_B2) * (gv * gv)
        d_ref[...] = -ADAM_LR * ((nm * c1) / (jnp.sqrt(nv * c2) + ADAM_EPS) + ADAM_WD * w_ref[...])
        nm_ref[...] = nm
        nv_ref[...] = nv

    o = jax.ShapeDtypeStruct((r, c), F32)
    return pl.pallas_call(
        body, out_shape=(o, o, o), grid=(grid,), in_specs=[sp, sp, sp, sp], out_specs=(sp, sp, sp),
        compiler_params=_params("parallel"), name=name)(w, g, m, v)


def _sum_slots(land, name):
    n, r, c = land.shape
    blk, grid, idx = _tiling_2d(r, c)

    def body(l_ref, o_ref):
        acc = l_ref[0].astype(F32)
        for i in range(1, n):
            acc = acc + l_ref[i].astype(F32)
        o_ref[...] = acc

    return pl.pallas_call(
        body, out_shape=jax.ShapeDtypeStruct((r, c), F32), grid=(grid,),
        in_specs=[pl.BlockSpec((n, *blk), lambda i: (0, *idx(i)))], out_specs=pl.BlockSpec(blk, idx),
        compiler_params=_params("parallel"), name=name)(land)


SLAB_COLS = 256


def _chip_sum(blocks, from_sibling, my_core, name):
    _, r, c = blocks.shape

    def body(core_ref, a_ref, b_ref, o_ref):
        o_ref[...] = _bf(a_ref[...].astype(F32) + b_ref[...].astype(F32))

    def slab(chip_of):
        return pl.BlockSpec((1, r, SLAB_COLS), lambda j, i, core: (chip_of(j, core), 0, i))

    grid_spec = pltpu.PrefetchScalarGridSpec(
        num_scalar_prefetch=1, grid=(4, c // SLAB_COLS),
        in_specs=[slab(lambda j, core: 2 * j + core[0]), slab(lambda j, core: j)],
        out_specs=slab(lambda j, core: j))
    return pl.pallas_call(
        body, out_shape=jax.ShapeDtypeStruct((4, r, c), BF16), grid_spec=grid_spec,
        compiler_params=_params("parallel", "parallel"), name=name,
    )(my_core.astype(jnp.int32).reshape(1), blocks, from_sibling)


def _sum_row_parts(parts, name):
    c = parts[0].shape[2]
    rows = [p.shape[1] for p in parts]

    def body(*refs):
        o_ref, off = refs[-1], 0
        for p_ref, n in zip(refs[:-1], rows):
            acc = p_ref[0].astype(F32)
            for i in range(1, p_ref.shape[0]):
                acc = acc + p_ref[i].astype(F32)
            o_ref[off:off + n, :] = acc
            off += n

    return pl.pallas_call(
        body, out_shape=jax.ShapeDtypeStruct((sum(rows), c), F32), grid=(c // SLAB_COLS,),
        in_specs=[pl.BlockSpec((p.shape[0], p.shape[1], SLAB_COLS), lambda i: (0, 0, i)) for p in parts],
        out_specs=pl.BlockSpec((sum(rows), SLAB_COLS), lambda i: (0, i)),
        compiler_params=_params("parallel"), name=name)(*parts)


def _retention_tables():
    lg = np.log1p(-(2.0 ** (-5.0 - np.arange(RET_HEADS, dtype=np.float64))))
    idx = np.arange(CHUNK, dtype=np.float64)
    intra = np.exp(np.abs(idx[:, None] - idx[None, :])[None] * lg[:, None, None])
    qd = np.exp((idx[None, :] + 1.0) * lg[:, None])
    kd = np.exp((CHUNK - 1.0 - idx[None, :]) * lg[:, None])
    cd = np.exp(CHUNK * lg)
    ones = np.ones((1, 1, RET_DK))
    return (jnp.asarray(intra, F32), jnp.asarray(qd[:, :, None] * ones, F32),
            jnp.asarray(kd[:, :, None] * ones, F32), jnp.asarray(cd[:, None, None] * ones, F32))


def _rope(t, cos, sin):
    t1, t2 = t[:, :ROPE_HALF], t[:, ROPE_HALF:]
    return jnp.concatenate([t1 * cos - t2 * sin, t2 * cos + t1 * sin], axis=1)


def _rope_t(d, cos, sin):
    d1, d2 = d[:, :ROPE_HALF], d[:, ROPE_HALF:]
    return jnp.concatenate([d1 * cos + d2 * sin, d2 * cos - d1 * sin], axis=1)


_HEADS = range(RET_HEADS)


def _ret_chunk_fwd(p_ref, cos, sin, intra_ref, qd_ref, st):
    def seg(i, h):
        return p_ref[:, i * RET_W + h * RET_DK:i * RET_W + (h + 1) * RET_DK]

    v = [seg(2, h) for h in _HEADS]
    qr = [_rope(seg(0, h), cos, sin) for h in _HEADS]
    kr = [_rope(seg(1, h), cos, sin) * (RET_DK ** -0.5) for h in _HEADS]
    qrb, vb = [_bf(a) for a in qr], [_bf(a) for a in v]
    sc = [_dot_nt(qrb[h], _bf(kr[h])) * intra_ref[h] for h in _HEADS]
    qs = [_dot(qrb[h], _bf(st[h])) for h in _HEADS]
    y = [_dot(_bf(sc[h]), vb[h]) + qs[h] * qd_ref[h] for h in _HEADS]
    g = [seg(3, h) for h in _HEADS]
    return dict(v=v, vb=vb, qr=qr, qrb=qrb, kr=kr, sc=sc, y=y, g=g)


def _group_norm(y):
    mu = jnp.mean(y, axis=1, keepdims=True)
    yc = y - mu
    rstd = lax.rsqrt(jnp.mean(yc * yc, axis=1, keepdims=True) + EPS)
    return yc * rstd, rstd


def _chunk_rows(sub):
    return pl.ds(sub * CHUNK, CHUNK)


def _ret_specs(steps, rev):
    cidx = (lambda c: steps - 1 - c) if rev else (lambda c: c)
    return dict(
        proj=pl.BlockSpec((RET_STEP_ROWS, 4 * RET_W), lambda c: (cidx(c), 0)),
        half=pl.BlockSpec((RET_STEP_ROWS, ROPE_HALF), lambda c: (cidx(c), 0)),
        wide=pl.BlockSpec((RET_STEP_ROWS, RET_W), lambda c: (cidx(c), 0)),
        state=pl.BlockSpec((RET_CHUNKS_PER_STEP, RET_HEADS, RET_DK, RET_DK), lambda c: (cidx(c), 0, 0, 0)),
        intra=_full_spec((RET_HEADS, CHUNK, CHUNK)),
        dec=_full_spec((RET_HEADS, CHUNK, RET_DK)),
        cd=_full_spec((RET_HEADS, 1, RET_DK)),
    )


def _retention_fwd(p_r, cos, sin, name):
    s = p_r.shape[0]
    nc = s // CHUNK
    intra_t, qd_t, kd_t, cd_t = _retention_tables()

    def body(p_ref, cos_ref, sin_ref, intra_ref, qd_ref, kd_ref, cd_ref, y_ref, st_ref, state):
        @pl.when(pl.program_id(0) == 0)
        def _():
            state[...] = jnp.zeros_like(state)

        for sub in range(RET_CHUNKS_PER_STEP):
            rows = _chunk_rows(sub)
            y_v, st_v = y_ref.at[rows], st_ref.at[sub]
            st = [state[h] for h in _HEADS]
            f = _ret_chunk_fwd(p_ref.at[rows], cos_ref[rows, :], sin_ref[rows, :], intra_ref, qd_ref, st)
            new_st = [st[h] * cd_ref[h] + _dot_tn(_bf(f["kr"][h] * kd_ref[h]), f["vb"][h]) for h in _HEADS]
            out = [_bf(_group_norm(f["y"][h])[0] * (f["g"][h] * _sigmoid(f["g"][h]))) for h in _HEADS]
            for h in _HEADS:
                st_v[h] = _bf(st[h])
                state[h] = new_st[h]
                y_v[:, h * RET_DK:(h + 1) * RET_DK] = out[h]

    steps = nc // RET_CHUNKS_PER_STEP
    sp = _ret_specs(steps, False)
    return pl.pallas_call(
        body,
        out_shape=(jax.ShapeDtypeStruct((s, RET_W), BF16),
                   jax.ShapeDtypeStruct((nc, RET_HEADS, RET_DK, RET_DK), BF16)),
        grid=(steps,),
        in_specs=[sp["proj"], sp["half"], sp["half"], sp["intra"], sp["dec"], sp["dec"], sp["cd"]],
        out_specs=(sp["wide"], sp["state"]),
        scratch_shapes=[pltpu.VMEM((RET_HEADS, RET_DK, RET_DK), F32)],
        compiler_params=_params("arbitrary"), name=name)(p_r, cos, sin, intra_t, qd_t, kd_t, cd_t)


def _retention_bwd(p_r, cos, sin, states, dy_r, name):
    s = p_r.shape[0]
    nc = s // CHUNK
    intra_t, qd_t, kd_t, cd_t = _retention_tables()

    def body(p_ref, cos_ref, sin_ref, intra_ref, qd_ref, kd_ref, cd_ref, st_ref, dy_ref, dp_ref, dstate):
        @pl.when(pl.program_id(0) == 0)
        def _():
            dstate[...] = jnp.zeros_like(dstate)

        for sub in reversed(range(RET_CHUNKS_PER_STEP)):
            rows = _chunk_rows(sub)
            chunk(p_ref.at[rows], cos_ref[rows, :], sin_ref[rows, :], intra_ref, qd_ref, kd_ref, cd_ref,
                  st_ref.at[sub], dy_ref.at[rows], dp_ref.at[rows], dstate)

    def chunk(p_ref, cos, sin, intra_ref, qd_ref, kd_ref, cd_ref, st_ref, dy_ref, dp_ref, dstate):
        st = [st_ref[h] for h in _HEADS]
        dsn = [dstate[h] for h in _HEADS]
        f = _ret_chunk_fwd(p_ref, cos, sin, intra_ref, qd_ref, st)
        vb, qrb, kr, g = f["vb"], f["qrb"], f["kr"], f["g"]
        norm = [_group_norm(f["y"][h]) for h in _HEADS]
        sg = [_sigmoid(g[h]) for h in _HEADS]
        dyr = [dy_ref[:, h * RET_DK:(h + 1) * RET_DK] for h in _HEADS]
        dyn = [dyr[h] * (g[h] * sg[h]) for h in _HEADS]
        dg = [dyr[h] * norm[h][0] * (sg[h] * (1.0 + g[h] * (1.0 - sg[h]))) for h in _HEADS]
        dy = [norm[h][1] * (dyn[h] - jnp.mean(dyn[h], axis=1, keepdims=True)
                            - norm[h][0] * jnp.mean(dyn[h] * norm[h][0], axis=1, keepdims=True)) for h in _HEADS]
        dyb, dsnb = [_bf(a) for a in dy], [_bf(a) for a in dsn]
        ds = [_bf(_dot_nt(dyb[h], vb[h]) * intra_ref[h]) for h in _HEADS]
        t = [_bf(dy[h] * qd_ref[h]) for h in _HEADS]
        dv = [_dot_tn(_bf(f["sc"][h]), dyb[h]) + _dot(_bf(kr[h] * kd_ref[h]), dsnb[h]) for h in _HEADS]
        dqr = [_dot(ds[h], _bf(kr[h])) + _dot_nt(t[h], _bf(st[h])) for h in _HEADS]
        dkr = [_dot_tn(ds[h], qrb[h]) + _dot_nt(vb[h], dsnb[h]) * kd_ref[h] for h in _HEADS]
        new_ds = [dsn[h] * cd_ref[h] + _dot_tn(qrb[h], t[h]) for h in _HEADS]
        for h in _HEADS:
            lo = h * RET_DK
            dstate[h] = new_ds[h]
            dp_ref[:, lo:lo + RET_DK] = _bf(_rope_t(dqr[h], cos, sin))
            dp_ref[:, RET_W + lo:RET_W + lo + RET_DK] = _bf(_rope_t(dkr[h], cos, sin) * (RET_DK ** -0.5))
            dp_ref[:, 2 * RET_W + lo:2 * RET_W + lo + RET_DK] = _bf(dv[h])
            dp_ref[:, 3 * RET_W + lo:3 * RET_W + lo + RET_DK] = _bf(dg[h])

    steps = nc // RET_CHUNKS_PER_STEP
    sp = _ret_specs(steps, True)
    return pl.pallas_call(
        body,
        out_shape=jax.ShapeDtypeStruct((s, 4 * RET_W), BF16),
        grid=(steps,),
        in_specs=[sp["proj"], sp["half"], sp["half"], sp["intra"], sp["dec"], sp["dec"], sp["cd"],
                  sp["state"], sp["wide"]],
        out_specs=sp["proj"],
        scratch_shapes=[pltpu.VMEM((RET_HEADS, RET_DK, RET_DK), F32)],
        compiler_params=_params("arbitrary"), name=name)(p_r, cos, sin, intra_t, qd_t, kd_t, cd_t, states, dy_r)


CONV_SLAB = 256
_CONV_SLABS = [slice(i * CONV_SLAB, (i + 1) * CONV_SLAB) for i in range(SSD_CD // CONV_SLAB)]


def _conv_taps(ext, w):
    acc = w[SSD_CONV - 1:SSD_CONV] * ext
    for j in range(SSD_CONV - 1):
        acc = acc + w[j:j + 1] * pltpu.roll(ext, SSD_CONV - 1 - j, axis=0)
    return acc


def _conv_fwd(xbc_raw, conv_w, conv_b, name):
    s = xbc_raw.shape[0]
    t8 = CONV_TILE // 8

    def body(cur_ref, prev_ref, w_ref, b_ref, o_ref, slope_ref):
        keep = (pl.program_id(0) > 0).astype(F32)
        for sl in _CONV_SLABS:
            ext = jnp.concatenate([prev_ref[:, sl] * keep, cur_ref[:, sl]], axis=0)
            u = _conv_taps(ext, w_ref[:, sl])[8:] + b_ref[:, sl]
            sg = _sigmoid(u)
            o_ref[:, sl] = u * sg
            slope_ref[:, sl] = _bf(sg * (1.0 + u * (1.0 - sg)))

    out = jax.ShapeDtypeStruct((s, SSD_CD), F32)
    return pl.pallas_call(
        body, out_shape=(out, jax.ShapeDtypeStruct((s, SSD_CD), BF16)), grid=(s // CONV_TILE,),
        in_specs=[_row_spec(SSD_CD, CONV_TILE),
                  pl.BlockSpec((8, SSD_CD), lambda i: (jnp.maximum(i * t8 - 1, 0), 0)),
                  _full_spec((SSD_CONV, SSD_CD)), _full_spec((1, SSD_CD))],
        out_specs=(_row_spec(SSD_CD, CONV_TILE), _row_spec(SSD_CD, CONV_TILE)),
        compiler_params=_params("parallel"), name=name)(xbc_raw, xbc_raw, conv_w, conv_b)


def _conv_bwd(xbc_raw, conv_w, slope, dact, name):
    s = xbc_raw.shape[0]
    nt = s // CONV_TILE
    t8 = CONV_TILE // 8
    rows = CONV_TILE + 8

    def body(cur_ref, prev_ref, w_ref, s_ref, snext_ref, d_ref, dnext_ref, dx_ref, dw_ref, db_ref):
        i = pl.program_id(0)

        @pl.when(i == 0)
        def _():
            dw_ref[...] = jnp.zeros_like(dw_ref)
            db_ref[...] = jnp.zeros_like(db_ref)

        keep_prev = (i > 0).astype(F32)
        keep_next = (i < nt - 1).astype(F32)
        for sl in _CONV_SLABS:
            w = w_ref[:, sl]
            cur = cur_ref[:, sl]
            ext = jnp.concatenate([prev_ref[:, sl] * keep_prev, cur], axis=0)
            duc = d_ref[:, sl].astype(F32) * s_ref[:, sl].astype(F32)
            after = (dnext_ref[:, sl].astype(F32) * snext_ref[:, sl].astype(F32))[:8] * keep_next
            du = jnp.concatenate([duc, after], axis=0)
            dx = w[SSD_CONV - 1:SSD_CONV] * du
            for j in range(SSD_CONV - 1):
                dx = dx + w[j:j + 1] * pltpu.roll(du, rows - (SSD_CONV - 1 - j), axis=0)
            dx_ref[:, sl] = _bf(dx[:CONV_TILE])
            for j in range(SSD_CONV - 1):
                shifted = pltpu.roll(ext, SSD_CONV - 1 - j, axis=0)[8:]
                dw_ref[j:j + 1, sl] += jnp.sum(duc * shifted, axis=0, keepdims=True)
            dw_ref[SSD_CONV - 1:SSD_CONV, sl] += jnp.sum(duc * cur, axis=0, keepdims=True)
            db_ref[:, sl] += jnp.sum(duc, axis=0, keepdims=True)

    row = _row_spec(SSD_CD, CONV_TILE)
    prev = pl.BlockSpec((8, SSD_CD), lambda i: (jnp.maximum(i * t8 - 1, 0), 0))
    nxt16 = pl.BlockSpec((16, SSD_CD), lambda i: (jnp.minimum((i + 1) * (t8 // 2), s // 16 - 1), 0))
    return pl.pallas_call(
        body,
        out_shape=(jax.ShapeDtypeStruct((s, SSD_CD), BF16), jax.ShapeDtypeStruct((SSD_CONV, SSD_CD), F32),
                   jax.ShapeDtypeStruct((1, SSD_CD), F32)),
        grid=(nt,),
        in_specs=[row, prev, _full_spec((SSD_CONV, SSD_CD)), row, nxt16, row, nxt16],
        out_specs=(row, _full_spec((SSD_CONV, SSD_CD)), _full_spec((1, SSD_CD))),
        compiler_params=_params("arbitrary"), name=name)(xbc_raw, xbc_raw, conv_w, slope, slope, dact, dact)


def _head_select():
    e = np.zeros((HEAD_PAD, SSD_W), np.float32)
    for h in range(SSD_HEADS):
        e[h, h * SSD_P:(h + 1) * SSD_P] = 1.0
    return jnp.asarray(e, BF16), jnp.asarray(e.T, BF16)


def _pad_heads(v):
    return jnp.pad(v.reshape(1, SSD_HEADS).astype(F32), ((0, 0), (0, HEAD_PAD - SSD_HEADS)))


def _ssd_masks():
    r = _iota((CHUNK, SSD_QW), 0)
    c = _iota((CHUNK, SSD_QW), 1) % SSD_P
    itile = (r == c).astype(F32)
    ctile = r >= c
    rb = _iota((SSD_QW, SSD_QW), 0) // SSD_P
    cb = _iota((SSD_QW, SSD_QW), 1) // SSD_P
    return itile, ctile, rb == cb


_TILES = [slice(i * SSD_QW, (i + 1) * SSD_QW) for i in range(SSD_W // SSD_QW)]


def _ssd_heads(dtr_ref, dtb_ref, alog_ref):
    u = dtr_ref[...] + dtb_ref[...]
    dt = _softplus(u)
    nexp = -jnp.exp(alog_ref[...])
    return u, dt, nexp, dt * nexp


def _ssd_group(xbc_ref, acol_ref, dte_ref, gs):
    acol = acol_ref[:, gs]
    alast = acol[CHUNK - 1:CHUNK, :]
    xs, dte = xbc_ref[:, gs], dte_ref[:, gs]
    return dict(xs=xs, dte=dte, xdt=xs * dte, ea=jnp.exp(acol), tail=jnp.exp(alast - acol), eal=jnp.exp(alast))


def _silu_gate(z_ref, y_ref, tl):
    zv = z_ref[:, tl]
    sz = _sigmoid(zv)
    return zv, sz, y_ref[:, tl] * (zv * sz)


def _tile4(x):
    return jnp.concatenate([x, x, x, x], axis=0)


def _fold4(x):
    return x[0:CHUNK] + x[CHUNK:2 * CHUNK] + x[2 * CHUNK:3 * CHUNK] + x[3 * CHUNK:4 * CHUNK]


def _ssd_specs(steps, rev):
    cidx = (lambda c: steps - 1 - c) if rev else (lambda c: c)
    return dict(
        xbc=pl.BlockSpec((STEP_ROWS, SSD_CD), lambda c: (cidx(c), 0)),
        dt=pl.BlockSpec((STEP_ROWS, HEAD_PAD), lambda c: (cidx(c), 0)),
        wide=pl.BlockSpec((STEP_ROWS, SSD_W), lambda c: (cidx(c), 0)),
        state=pl.BlockSpec((CHUNKS_PER_STEP, SSD_N, SSD_W), lambda c: (cidx(c), 0, 0)),
        head=_full_spec((1, HEAD_PAD)),
        roww=_full_spec((1, SSD_W)),
        e=_full_spec((HEAD_PAD, SSD_W)),
        et=_full_spec((SSD_W, HEAD_PAD)),
    )


_GROUPS = range(SSD_GROUPS)
_GROUP_LANES = [slice(g * SSD_GW, (g + 1) * SSD_GW) for g in _GROUPS]
_QUADS = [(g, slice(g * SSD_GW + q * SSD_QW, g * SSD_GW + (q + 1) * SSD_QW), slice(q * SSD_QW, (q + 1) * SSD_QW))
          for g in _GROUPS for q in range(SSD_GW // SSD_QW)]


def _ssd_bc(xbc_ref):
    bg = [_bf(xbc_ref[:, B_OFF + g * SSD_N:B_OFF + (g + 1) * SSD_N]) for g in _GROUPS]
    cg = [_bf(xbc_ref[:, C_OFF + g * SSD_N:C_OFF + (g + 1) * SSD_N]) for g in _GROUPS]
    return bg, cg


def _ssd_decay(aq, itile, ctile):
    arow = jnp.sum(aq * itile, axis=0, keepdims=True)
    return jnp.exp(jnp.where(ctile, aq - arow, -jnp.inf))


def _ssd_blockdiag(xq, bdmask):
    return jnp.where(bdmask, _tile4(_bf(xq)), jnp.zeros((), BF16))


def _ssd_fwd(xbc, dt_raw, z, dt_bias, a_log, dske, norm_w, name):
    s = xbc.shape[0]
    nc = s // CHUNK
    e_sel, _ = _head_select()

    def body(xbc_ref, dtr_ref, z_ref, dtb_ref, alog_ref, dske_ref, nw_ref, e_ref,
             yraw_ref, ys_ref, st_ref, acol_ref, dte_ref, state):
        @pl.when(pl.program_id(0) == 0)
        def _():
            state[...] = jnp.zeros_like(state)

        for sub in range(CHUNKS_PER_STEP):
            rows = _chunk_rows(sub)
            chunk(xbc_ref.at[rows], dtr_ref.at[rows], z_ref.at[rows], dtb_ref, alog_ref, dske_ref, nw_ref, e_ref,
                  yraw_ref.at[rows], ys_ref.at[rows], st_ref.at[sub], acol_ref.at[rows], dte_ref.at[rows], state)

    def chunk(xbc_ref, dtr_ref, z_ref, dtb_ref, alog_ref, dske_ref, nw_ref, e_ref,
              yraw_ref, ys_ref, st_ref, acol_ref, dte_ref, state):
        _, dt, _, a = _ssd_heads(dtr_ref, dtb_ref, alog_ref)
        tril = _bf((_iota((CHUNK, CHUNK), 0) >= _iota((CHUNK, CHUNK), 1)).astype(F32))
        ac3, dt3 = _split3(_dot_exact_r(tril, a)), _split3(dt)
        for tl in _TILES:
            e_t = e_ref[:, tl]
            acol_ref[:, tl] = _dot(ac3[0], e_t) + _dot(ac3[1], e_t) + _dot(ac3[2], e_t)
            dte_ref[:, tl] = _dot(dt3[0], e_t) + _dot(dt3[1], e_t) + _dot(dt3[2], e_t)

        itile, ctile, bdmask = _ssd_masks()
        st_ref[...] = state[...]
        bg, cg = _ssd_bc(xbc_ref)
        for g0 in range(0, SSD_GROUPS, SSD_BLOCK_GROUPS):
            gids = range(g0, g0 + SSD_BLOCK_GROUPS)
            quads = [(g, ql, qs) for g, ql, qs in _QUADS if g in gids]
            q = {g: _ssd_group(xbc_ref, acol_ref, dte_ref, _GROUP_LANES[g]) for g in gids}
            stg = {g: state[:, _GROUP_LANES[g]] for g in gids}
            cbt = {g: _dot_nt(cg[g], _tile4(bg[g])) for g in gids}
            ys = {g: _dot(cg[g], _bf(stg[g])) for g in gids}
            dq = [_ssd_decay(acol_ref[:, ql], itile, ctile) for _, ql, _ in quads]
            xbd = [_ssd_blockdiag(q[g]["xdt"][:, qs], bdmask) for g, _, qs in quads]
            yq = [_dot(_bf(cbt[g] * dq[i]), xbd[i]) + ys[g][:, qs] * q[g]["ea"][:, qs]
                  + dske_ref[:, ql] * q[g]["xs"][:, qs] for i, (g, ql, qs) in enumerate(quads)]
            new_st = {g: stg[g] * q[g]["eal"] + _dot_tn(bg[g], _bf(q[g]["xdt"] * q[g]["tail"])) for g in gids}
            for i, (_, ql, _) in enumerate(quads):
                yraw_ref[:, ql] = yq[i]
            for g in gids:
                state[:, _GROUP_LANES[g]] = new_st[g]

        sq = jnp.zeros((CHUNK, SSD_QW), F32)
        for tl in _TILES:
            t = _silu_gate(z_ref, yraw_ref, tl)[2]
            sq = sq + t * t
        rstd = lax.rsqrt(jnp.sum(sq, axis=1, keepdims=True) * (1.0 / SSD_W) + EPS)
        for tl in _TILES:
            ys_ref[:, tl] = _bf(_silu_gate(z_ref, yraw_ref, tl)[2] * rstd * nw_ref[:, tl])

    steps = nc // CHUNKS_PER_STEP
    sp = _ssd_specs(steps, False)
    wide = jax.ShapeDtypeStruct((s, SSD_W), F32)
    return pl.pallas_call(
        body,
        out_shape=(wide, jax.ShapeDtypeStruct((s, SSD_W), BF16), jax.ShapeDtypeStruct((nc, SSD_N, SSD_W), F32),
                   wide, wide),
        grid=(steps,),
        in_specs=[sp["xbc"], sp["dt"], sp["wide"], sp["head"], sp["head"], sp["roww"], sp["roww"], sp["e"]],
        out_specs=(sp["wide"], sp["wide"], sp["state"], sp["wide"], sp["wide"]),
        scratch_shapes=[pltpu.VMEM((SSD_N, SSD_W), F32)],
        compiler_params=_params("arbitrary"), name=name,
    )(xbc, dt_raw, z, dt_bias, a_log, dske, norm_w, e_sel)


def _ssd_bwd(xbc, dt_raw, z, yraw, states, dys, acol, dte, dt_bias, a_log, dske, norm_w, name, rider):
    s = xbc.shape[0]
    nc = s // CHUNK
    steps = nc // CHUNKS_PER_STEP
    _, et_sel = _head_select()
    n_in, n_out, n_scratch = 13, 7, 5
    n_rin, n_rout = len(rider.inputs), len(rider.out_shapes)

    def body(*refs):
        ins, refs = refs[:n_in], refs[n_in:]
        r_ins, refs = refs[:n_rin], refs[n_rin:]
        outs, refs = refs[:n_out], refs[n_out:]
        r_outs, refs = refs[:n_rout], refs[n_rout:]
        ride = functools.partial(rider.run, pl.program_id(0), steps, r_ins, r_outs, refs[n_scratch],
                                 refs[n_scratch + 1])
        ride(last=False)
        compute(*ins, *outs, *refs[:n_scratch])
        ride(last=True)

    def compute(xbc_ref, dtr_ref, z_ref, yraw_ref, st_ref, dys_ref, acol_ref, dte_ref, dtb_ref, alog_ref,
                dske_ref, nw_ref, et_ref, dxbc_ref, dz_ref, ddtr_ref, ddtb_ref, dalog_ref, ddsk_ref, dnw_ref,
                dstate, dsk_acc, dy_s, dacol_s, ddte_s):
        @pl.when(pl.program_id(0) == 0)
        def _():
            dstate[...] = jnp.zeros_like(dstate)
            dsk_acc[...] = jnp.zeros_like(dsk_acc)
            ddtb_ref[...] = jnp.zeros_like(ddtb_ref)
            dalog_ref[...] = jnp.zeros_like(dalog_ref)
            dnw_ref[...] = jnp.zeros_like(dnw_ref)

        for sub in reversed(range(CHUNKS_PER_STEP)):
            rows = _chunk_rows(sub)
            chunk(xbc_ref.at[rows], dtr_ref.at[rows], z_ref.at[rows], yraw_ref.at[rows], st_ref.at[sub],
                  dys_ref.at[rows], acol_ref.at[rows], dte_ref.at[rows], dtb_ref, alog_ref, dske_ref, nw_ref, et_ref,
                  dxbc_ref.at[rows], dz_ref.at[rows], ddtr_ref.at[rows], ddtb_ref, dalog_ref, dnw_ref,
                  dstate, dsk_acc, dy_s, dacol_s, ddte_s)

        @pl.when(pl.program_id(0) == steps - 1)
        def _():
            ddsk_ref[...] = _dot_exact_l(jnp.broadcast_to(dsk_acc[...], (8, SSD_W)), et_ref[...])[0:1]

    def chunk(xbc_ref, dtr_ref, z_ref, yraw_ref, st_ref, dys_ref, acol_ref, dte_ref, dtb_ref, alog_ref,
              dske_ref, nw_ref, et_ref, dxbc_ref, dz_ref, ddtr_ref, ddtb_ref, dalog_ref, dnw_ref,
              dstate, dsk_acc, dy_s, dacol_s, ddte_s):
        itile, ctile, bdmask = _ssd_masks()
        last_row = (_iota((CHUNK, 1), 0) == CHUNK - 1).astype(F32)

        s1 = jnp.zeros((CHUNK, SSD_QW), F32)
        s2 = jnp.zeros((CHUNK, SSD_QW), F32)
        for tl in _TILES:
            t = _silu_gate(z_ref, yraw_ref, tl)[2]
            s1 = s1 + t * t
            s2 = s2 + dys_ref[:, tl] * nw_ref[:, tl] * t
        rstd = lax.rsqrt(jnp.sum(s1, axis=1, keepdims=True) * (1.0 / SSD_W) + EPS)
        back = rstd * rstd * rstd * jnp.sum(s2, axis=1, keepdims=True) * (1.0 / SSD_W)
        for tl in _TILES:
            zv, sz, t = _silu_gate(z_ref, yraw_ref, tl)
            dysv = dys_ref[:, tl]
            dt_ = rstd * (dysv * nw_ref[:, tl]) - t * back
            dnw_ref[:, tl] += jnp.sum(dysv * t * rstd, axis=0, keepdims=True)
            dz_ref[:, tl] = _bf(dt_ * yraw_ref[:, tl] * (sz * (1.0 + zv * (1.0 - sz))))
            dy_t = dt_ * (zv * sz)
            dy_s[:, tl] = dy_t
            dsk_acc[:, tl] += jnp.sum(dy_t * xbc_ref[:, tl], axis=0, keepdims=True)

        gl = _GROUP_LANES
        bg, cg = _ssd_bc(xbc_ref)
        for g0 in range(0, SSD_GROUPS, SSD_BLOCK_GROUPS):
            gids = range(g0, g0 + SSD_BLOCK_GROUPS)
            quads = [(i, g, ql, qs) for i, (g, ql, qs) in enumerate(_QUADS) if g in gids]
            q = {g: _ssd_group(xbc_ref, acol_ref, dte_ref, gl[g]) for g in gids}
            stg = {g: st_ref[:, gl[g]] for g in gids}
            dsn = {g: dstate[:, gl[g]] for g in gids}
            stgb, dsnb = {g: _bf(stg[g]) for g in gids}, {g: _bf(dsn[g]) for g in gids}
            btile = {g: _tile4(bg[g]) for g in gids}
            cbt = {g: _dot_nt(cg[g], btile[g]) for g in gids}
            dyg = {g: dy_s[:, gl[g]] for g in gids}
            eag, tailg = {g: q[g]["ea"] for g in gids}, {g: q[g]["tail"] for g in gids}
            xdtg, ealg = {g: q[g]["xdt"] for g in gids}, {g: q[g]["eal"] for g in gids}
            dys_g = {g: _bf(dyg[g] * eag[g]) for g in gids}
            ysg = {g: _dot(cg[g], stgb[g]) for g in gids}
            dc0 = {g: _dot_nt(dys_g[g], stgb[g]) for g in gids}
            dst = {g: _dot_tn(cg[g], dys_g[g]) for g in gids}
            dwt = {g: _dot(bg[g], dsnb[g]) for g in gids}
            db0 = {g: _dot_nt(_bf(xdtg[g] * tailg[g]), dsnb[g]) for g in gids}
            dtl = {g: dwt[g] * xdtg[g] * tailg[g] for g in gids}
            dal_row = {g: jnp.sum(dtl[g], axis=0, keepdims=True)
                       + jnp.sum(dsn[g] * stg[g], axis=0, keepdims=True) * ealg[g] for g in gids}
            for g in gids:
                dstate[:, gl[g]] = dst[g] + dsn[g] * ealg[g]
            dq = {i: _ssd_decay(acol_ref[:, ql], itile, ctile) for i, _, ql, _ in quads}
            mq = {i: cbt[g] * dq[i] for i, g, _, _ in quads}
            xbd = {i: _ssd_blockdiag(xdtg[g][:, qs], bdmask) for i, g, _, qs in quads}
            dyq = {i: _bf(dyg[g][:, qs]) for i, g, _, qs in quads}
            dm = {i: _dot_nt(dyq[i], xbd[i]) for i, _, _, _ in quads}
            dxdt_q = {i: _fold4(jnp.where(bdmask, _dot_tn(_bf(mq[i]), dyq[i]), 0.0)) for i, _, _, _ in quads}
            eq = {i: dm[i] * mq[i] for i, _, _, _ in quads}
            dacol_q = {i: eq[i] - itile * jnp.sum(eq[i], axis=0, keepdims=True) for i, _, _, _ in quads}
            dcbt = {i: _bf(dm[i] * dq[i]) for i, _, _, _ in quads}
            dc_q = {i: _dot(dcbt[i], btile[g]) for i, g, _, _ in quads}
            db_q = {i: _fold4(_dot_tn(dcbt[i], cg[g])) for i, g, _, _ in quads}
            for g in gids:
                dxdt_g = jnp.concatenate([dxdt_q[2 * g], dxdt_q[2 * g + 1]], axis=1) + dwt[g] * tailg[g]
                dxbc_ref[:, gl[g]] = _bf(dyg[g] * dske_ref[:, gl[g]] + dxdt_g * q[g]["dte"])
                ddte_s[:, gl[g]] = dxdt_g * q[g]["xs"]
                dacol_s[:, gl[g]] = (jnp.concatenate([dacol_q[2 * g], dacol_q[2 * g + 1]], axis=1)
                                     + dyg[g] * (ysg[g] * eag[g]) - dtl[g] + last_row * dal_row[g])
                dxbc_ref[:, B_OFF + g * SSD_N:B_OFF + (g + 1) * SSD_N] = _bf(db0[g] + db_q[2 * g] + db_q[2 * g + 1])
                dxbc_ref[:, C_OFF + g * SSD_N:C_OFF + (g + 1) * SSD_N] = _bf(dc0[g] + dc_q[2 * g] + dc_q[2 * g + 1])

        dacum = jnp.zeros((CHUNK, HEAD_PAD), F32)
        ddt = jnp.zeros((CHUNK, HEAD_PAD), F32)
        for i, tl in enumerate(_TILES):
            et_t = et_ref[i * SSD_QW:(i + 1) * SSD_QW, :]
            dacum = dacum + _dot_exact_l(dacol_s[:, tl], et_t, pieces=2)
            ddt = ddt + _dot_exact_l(ddte_s[:, tl], et_t, pieces=2)
        u, _, nexp, a = _ssd_heads(dtr_ref, dtb_ref, alog_ref)
        triu = _bf((_iota((CHUNK, CHUNK), 1) >= _iota((CHUNK, CHUNK), 0)).astype(F32))
        da = _dot_exact_r(triu, dacum)
        ddt = ddt + da * nexp
        dalog_ref[...] += jnp.sum(da * a, axis=0, keepdims=True)
        du = ddt * _sigmoid(u)
        ddtr_ref[...] = _bf(du)
        ddtb_ref[...] += jnp.sum(du, axis=0, keepdims=True)

    sp = _ssd_specs(steps, True)
    return pl.pallas_call(
        body,
        out_shape=(jax.ShapeDtypeStruct((s, SSD_CD), BF16), jax.ShapeDtypeStruct((s, SSD_W), BF16),
                   jax.ShapeDtypeStruct((s, HEAD_PAD), BF16), jax.ShapeDtypeStruct((1, HEAD_PAD), F32),
                   jax.ShapeDtypeStruct((1, HEAD_PAD), F32), jax.ShapeDtypeStruct((1, HEAD_PAD), F32),
                   jax.ShapeDtypeStruct((1, SSD_W), F32), *rider.out_shapes),
        grid=(steps,),
        in_specs=[sp["xbc"], sp["dt"], sp["wide"], sp["wide"], sp["state"], sp["wide"], sp["wide"], sp["wide"],
                  sp["head"], sp["head"], sp["roww"], sp["roww"], sp["et"], *rider.in_specs()],
        out_specs=(sp["xbc"], sp["wide"], sp["dt"], sp["head"], sp["head"], sp["head"], sp["roww"],
                   *rider.out_specs()),
        scratch_shapes=[pltpu.VMEM((SSD_N, SSD_W), F32), pltpu.VMEM((1, SSD_W), F32),
                        *[pltpu.VMEM((CHUNK, SSD_W), F32)] * 3, *rider.scratch()],
        compiler_params=_params("arbitrary"), name=name,
    )(xbc, dt_raw, z, yraw, states, dys, acol, dte, dt_bias, a_log, dske, norm_w, et_sel, *rider.inputs)


W_IN_SHARD = IN_PROJ // N_DEV
W_IN_ROW_PARTS = ((0, 1152), (1152, 832), (1984, 840))


def _device_step(x, pos_col, target, norm1_w, conv_w_shard, conv_b, dt_bias, a_log, d_skip, ssd_norm_w, norm_f_w,
                 w_in_shard, br_ret_shard, br_ssd_shard, out_shard):
    inv_freq = jnp.asarray(ROPE_THETA ** (-np.arange(ROPE_HALF, dtype=np.float64) / ROPE_HALF), F32).reshape(1, ROPE_HALF)
    dtb, alog = _pad_heads(dt_bias), _pad_heads(a_log)
    dske = jnp.repeat(d_skip.reshape(SSD_HEADS).astype(F32), SSD_P).reshape(1, SSD_W)
    my_core = lax.axis_index("c")

    h, w_all = _rmsnorm_fwd(x, norm1_w, "rmsnorm1_fwd", _gather_rider([w_in_shard], 1.0, 1.0))
    w_all = w_all.reshape(IN_PROJ, D_MODEL)
    w_dt = jnp.pad(w_all[OFF_DT:OFF_G], ((0, HEAD_PAD - SSD_HEADS), (0, 0)))
    w_g = w_all[OFF_G:]
    rows_r, rows_z, rows_xbc = (0, OFF_Z), (OFF_Z, SSD_W), (OFF_XBC, SSD_CD)
    gather = _gather_rider([br_ret_shard, br_ssd_shard, out_shard, conv_w_shard], relay_at=0.35, pass_at=0.6)
    p_r, all_ret, all_ssd, all_out, all_conv = _matmul(h, w_all, "nt", F32, "proj_ret", rider=gather, b_rows=rows_r)
    w_br_ret = all_ret.reshape(RET_W, D_MODEL)
    w_br_ssd = all_ssd.reshape(SSD_W, D_MODEL)
    w_out = all_out.reshape(D_MODEL, D_MODEL)
    conv_w = all_conv.transpose(1, 0, 2).reshape(SSD_CONV, SSD_CD)
    p_z = _matmul(h, w_all, "nt", F32, "proj_z", b_rows=rows_z)
    p_xbc = _matmul(h, w_all, "nt", F32, "proj_xbc", b_rows=rows_xbc)
    p_dt = _matmul(h, w_dt, "nt", F32, "proj_dt")
    p_g = _matmul(h, w_g, "nt", BF16, "proj_gates")
    cos, sin = _rope_tables(pos_col, inv_freq, "rope_tables")
    y_r, ret_states = _retention_fwd(p_r, cos, sin, "retention_fwd")
    xbc_act, silu_slope = _conv_fwd(p_xbc, conv_w, conv_b, "conv_fwd")
    y_raw, y_s, ssd_states, acol, dte = _ssd_fwd(xbc_act, p_dt, p_z, dtb, alog, dske, ssd_norm_w, "ssd_fwd")
    pr = _matmul(y_r, w_br_ret, "nn", BF16, "branch_ret")
    ps = _matmul(y_s, w_br_ssd, "nn", BF16, "branch_ssd")
    merged = _merge_fwd(p_g, pr, ps, "merge_fwd")
    o = _matmul(merged, w_out, "nn", F32, "out_proj")
    dx2, g_norm_f, loss_acc = _final_fwd_bwd(x, o, norm_f_w, target, "final_norm_loss")

    g_w_out = _matmul(merged, dx2, "tn", BF16, "grad_w_out")
    dmerged = _matmul(dx2, w_out, "nt", F32, "d_merged")
    dpr, dps, dp_g = _merge_bwd(p_g, pr, ps, dmerged, "merge_bwd")
    g_w_br_ret = _matmul(y_r, dpr, "tn", BF16, "grad_w_br_ret")
    g_w_br_ssd = _matmul(y_s, dps, "tn", BF16, "grad_w_br_ssd")
    dy_r = _matmul(dpr, w_br_ret, "nt", F32, "d_y_ret")
    dy_s = _matmul(dps, w_br_ssd, "nt", F32, "d_y_ssd")
    scatter = _scatter_rider([g_w_out.reshape(N_DEV, -1, D_MODEL), g_w_br_ret.reshape(N_DEV, -1, D_MODEL),
                              g_w_br_ssd.reshape(N_DEV, -1, D_MODEL)])
    dxbc_act, dp_z, dp_dt, g_dtb, g_alog, g_dsk, g_ssd_norm, got_out, got_ret, got_ssd = _ssd_bwd(
        xbc_act, p_dt, p_z, y_raw, ssd_states, dy_s, acol, dte, dtb, alog, dske, ssd_norm_w, "ssd_bwd", scatter)
    dp_xbc, g_conv_w, g_conv_b = _conv_bwd(p_xbc, conv_w, silu_slope, dxbc_act, "conv_bwd")
    dp_r = _retention_bwd(p_r, cos, sin, ret_states, dy_r, "retention_bwd")
    g_w_in = jnp.concatenate([
        _matmul(dp_r, h, "tn", BF16, "grad_w_ret"),
        _matmul(dp_z, h, "tn", BF16, "grad_w_z"),
        _matmul(dp_xbc, h, "tn", BF16, "grad_w_xbc"),
        _matmul(dp_dt, h, "tn", BF16, "grad_w_dt")[:SSD_HEADS],
        _matmul(dp_g, h, "tn", BF16, "grad_w_gates"),
    ], axis=0)
    blocks = g_w_in.reshape(N_DEV, W_IN_SHARD, D_MODEL)
    dh, from_sibling = _matmul(dp_r, w_all, "nn", F32, "d_h_ret", rider=_sibling_rider(blocks), b_rows=rows_r)
    chip_sum = _chip_sum(blocks, from_sibling, my_core, "w_in_chip_sum")
    carriers = (("d_h_xbc", dp_xbc, w_all, rows_xbc), ("d_h_gates", dp_g, w_g, None), ("d_h_z", dp_z, w_all, rows_z))
    landed = []
    for (row0, rows), (nm, dp, w, b_rows) in zip(W_IN_ROW_PARTS, carriers):
        dt_pair = (dp_dt, w_dt) if nm == "d_h_z" else None
        dh, got = _matmul(dp, w, "nn", F32, nm, add=dh, rider=_chip_rider(chip_sum, row0, rows), b_rows=b_rows,
                          extra=dt_pair)
        landed.append(got)
    grad_x, g_norm1 = _rmsnorm_bwd(x, norm1_w, dh, dx2, "rmsnorm1_bwd")
    small = dict(norm1_w=g_norm1, conv_w=g_conv_w, conv_b=g_conv_b, dt_bias=g_dtb[:, :SSD_HEADS],
                 a_log=g_alog[:, :SSD_HEADS], d_skip=g_dsk[:, :SSD_HEADS], ssd_norm_w=g_ssd_norm,
                 norm_f_w=g_norm_f)
    big = dict(w_in=_sum_row_parts(landed, "w_in_sum"), w_br_ret=_sum_slots(got_ret, "w_br_ret_sum"),
               w_br_ssd=_sum_slots(got_ssd, "w_br_ssd_sum"), w_out=_sum_slots(got_out, "w_out_sum"))
    return loss_acc[0, 0], grad_x, small, big


def _all_reduce_small(vec, name):
    r, c = vec.shape

    def body(x_ref, out_ref, land, send_sems, recv_sems):
        x, y, cc = _mesh_pos()
        my_idx = 4 * x + 2 * y + cc
        land[my_idx] = x_ref[...]
        copies = []
        for k in range(1, N_DEV):
            px, py, pc = x ^ (k >> 2), y ^ ((k >> 1) & 1), cc ^ (k & 1)
            cp = pltpu.make_async_remote_copy(
                src_ref=x_ref, dst_ref=land.at[my_idx],
                send_sem=send_sems.at[k - 1], recv_sem=recv_sems.at[k - 1],
                device_id=(px, py, pc), device_id_type=MESH)
            cp.start()
            copies.append(cp)
        for k in range(1, N_DEV):
            px, py, pc = x ^ (k >> 2), y ^ ((k >> 1) & 1), cc ^ (k & 1)
            pltpu.make_async_remote_copy(
                src_ref=x_ref, dst_ref=land.at[4 * px + 2 * py + pc],
                send_sem=send_sems.at[k - 1], recv_sem=recv_sems.at[k - 1],
                device_id=(px, py, pc), device_id_type=MESH).wait_recv()
        for cp in copies:
            cp.wait_send()
        acc = land[0]
        for i in range(1, N_DEV):
            acc = acc + land[i]
        out_ref[...] = acc

    return pl.pallas_call(
        body,
        out_shape=jax.ShapeDtypeStruct((r, c), F32),
        in_specs=[pl.BlockSpec(memory_space=pltpu.VMEM)],
        out_specs=pl.BlockSpec(memory_space=pltpu.VMEM),
        scratch_shapes=[pltpu.VMEM((N_DEV, r, c), F32), pltpu.SemaphoreType.DMA((7,)),
                        pltpu.SemaphoreType.DMA((7,))],
        name=name)(vec)


_SMALL = ("norm1_w", "conv_w", "conv_b", "dt_bias", "a_log", "d_skip", "ssd_norm_w", "norm_f_w")
_SMALL_COLS = 128
_W_IN = "w_in"
_WEIGHTS = ("norm1_w", "w_in", "conv_w", "conv_b", "dt_bias", "a_log", "d_skip", "ssd_norm_w",
            "w_br_ret", "w_br_ssd", "w_out", "norm_f_w")


def _pack(parts):
    flat = jnp.concatenate([p.reshape(-1).astype(F32) for p in parts])
    rows = -(-flat.shape[0] // (8 * _SMALL_COLS)) * 8
    return jnp.pad(flat, (0, rows * _SMALL_COLS - flat.shape[0])).reshape(rows, _SMALL_COLS)


def _unpack(packed, shapes):
    flat = packed.reshape(-1)
    out, off = [], 0
    for shp in shapes:
        n = int(np.prod(shp))
        out.append(flat[off:off + n].reshape(shp))
        off += n
    return out


def kernel(x, positions, norm1_w, w_in, conv_w, conv_b, dt_bias, a_log, d_skip, ssd_norm_w, w_br_ret, w_br_ssd, w_out, norm_f_w, loss_target, m_norm1_w, m_w_in, m_conv_w, m_conv_b, m_dt_bias, m_a_log, m_d_skip, m_ssd_norm_w, m_w_br_ret, m_w_br_ssd, m_w_out, m_norm_f_w, v_norm1_w, v_w_in, v_conv_w, v_conv_b, v_dt_bias, v_a_log, v_d_skip, v_ssd_norm_w, v_w_br_ret, v_w_br_ssd, v_w_out, v_norm_f_w):
    w = dict(norm1_w=norm1_w, w_in=w_in, conv_w=conv_w, conv_b=conv_b, dt_bias=dt_bias, a_log=a_log,
             d_skip=d_skip, ssd_norm_w=ssd_norm_w, w_br_ret=w_br_ret, w_br_ssd=w_br_ssd, w_out=w_out,
             norm_f_w=norm_f_w)
    m = dict(norm1_w=m_norm1_w, w_in=m_w_in, conv_w=m_conv_w, conv_b=m_conv_b, dt_bias=m_dt_bias,
             a_log=m_a_log, d_skip=m_d_skip, ssd_norm_w=m_ssd_norm_w, w_br_ret=m_w_br_ret,
             w_br_ssd=m_w_br_ssd, w_out=m_w_out, norm_f_w=m_norm_f_w)
    v = dict(norm1_w=v_norm1_w, w_in=v_w_in, conv_w=v_conv_w, conv_b=v_conv_b, dt_bias=v_dt_bias,
             a_log=v_a_log, d_skip=v_d_skip, ssd_norm_w=v_ssd_norm_w, w_br_ret=v_w_br_ret,
             w_br_ssd=v_w_br_ssd, w_out=v_w_out, norm_f_w=v_norm_f_w)
    s = x.shape[1]
    my_idx = 4 * lax.axis_index("x") + 2 * lax.axis_index("y") + lax.axis_index("c")

    w[_W_IN], m[_W_IN], v[_W_IN] = w_in[0].T, m_w_in[0].T, v_w_in[0].T

    loss_part, grad_x, g_small, g_big = _device_step(
        x[0], positions.reshape(s, 1), loss_target[0], norm1_w, conv_w[0], conv_b, dt_bias, a_log, d_skip,
        ssd_norm_w, norm_f_w.reshape(1, D_MODEL), _cast_bf16(w[_W_IN], "cast_w_in"),
        _cast_bf16(w_br_ret[0], "cast_w_br_ret"), _cast_bf16(w_br_ssd[0], "cast_w_br_ssd"),
        _cast_bf16(w_out[0], "cast_w_out"))

    loss = lax.psum(loss_part, ("x", "y", "c"))

    small_shapes = [g_small[n].shape for n in _SMALL]
    summed = _unpack(_all_reduce_small(_pack([g_small[n] for n in _SMALL]), "allreduce_small"), small_shapes)
    grads = dict(zip(_SMALL, summed))
    conv_cols = SSD_CD // N_DEV
    grads["conv_w"] = lax.dynamic_slice_in_dim(grads["conv_w"], my_idx * conv_cols, conv_cols, axis=1)
    grads["norm_f_w"] = grads["norm_f_w"].reshape(D_MODEL)
    for n in ("norm1_w", "conv_w", "conv_b", "dt_bias", "a_log", "d_skip", "ssd_norm_w"):
        grads[n] = grads[n].reshape(w[n].shape)

    delta, new_m, new_v = {}, {}, {}
    for n in ("w_br_ret", "w_br_ssd", "w_out"):
        w[n], m[n], v[n] = w[n][0], m[n][0], v[n][0]
    for n in (_W_IN, "w_br_ret", "w_br_ssd", "w_out"):
        back = (lambda a: a.T[None]) if n == _W_IN else (lambda a: a[None])
        res = _adamw(w[n], g_big[n], m[n], v[n], "adamw_" + n)
        grads[n] = back(g_big[n])
        delta[n], new_m[n], new_v[n] = (back(a) for a in res)
    shapes = [w[n].shape for n in _SMALL]
    packed = _adamw(_pack([w[n] for n in _SMALL]), _pack([grads[n] for n in _SMALL]),
                    _pack([m[n] for n in _SMALL]), _pack([v[n] for n in _SMALL]), "adamw_small")
    for res, dst in zip(packed, (delta, new_m, new_v)):
        for n, a in zip(_SMALL, _unpack(res, shapes)):
            dst[n] = a

    return (loss, grad_x.reshape(x.shape), *[grads[n] for n in _WEIGHTS], *[delta[n] for n in _WEIGHTS],
            *[new_m[n] for n in _WEIGHTS], *[new_v[n] for n in _WEIGHTS])
```

```python
import functools

import numpy as np
import jax
import jax.numpy as jnp
from jax import lax
from jax.experimental import pallas as pl
from jax.experimental.pallas import tpu as pltpu

F32 = jnp.float32
BF16 = jnp.bfloat16

D_MODEL = 2048
CHUNK = 64
CHUNKS_PER_STEP = 2
STEP_ROWS = CHUNK * CHUNKS_PER_STEP
RET_CHUNKS_PER_STEP = 4
RET_STEP_ROWS = CHUNK * RET_CHUNKS_PER_STEP
EPS = 1e-6
N_DEV = 8

RET_HEADS = 8
RET_DK = 256
RET_W = RET_HEADS * RET_DK
ROPE_THETA = 10000.0
ROPE_HALF = RET_DK // 2

SSD_W = 4096
SSD_P = 64
SSD_HEADS = 64
SSD_GROUPS = 8
SSD_N = 128
SSD_GW = SSD_W // SSD_GROUPS
SSD_QW = 256
SSD_BLOCK_GROUPS = 2
SSD_CONV = 4
SSD_CD = SSD_W + 2 * SSD_GROUPS * SSD_N
HEAD_PAD = 128
B_OFF = SSD_W
C_OFF = SSD_W + SSD_GROUPS * SSD_N

ADAM_LR = 0.001
ADAM_B1 = 0.9
ADAM_B2 = 0.999
ADAM_EPS = 1e-08
ADAM_WD = 0.01
ADAM_STEP = 10

SPLITS = (RET_W, RET_W, RET_W, RET_W, SSD_W, SSD_CD, SSD_HEADS, D_MODEL, D_MODEL)
IN_PROJ = sum(SPLITS)
OFF_Z = 4 * RET_W
OFF_XBC = OFF_Z + SSD_W
OFF_DT = OFF_XBC + SSD_CD
OFF_G = OFF_DT + SSD_HEADS

ROW_TILE = 256
CONV_TILE = 128
MM_TILE = 1024
MM_TILE_WIDE = 2048
MM_TK = 2048
VMEM_CAP = 60 << 20

MESH = pl.DeviceIdType.MESH


def _dot(a, b):
    return lax.dot_general(a, b, (((1,), (0,)), ((), ())), preferred_element_type=F32)


def _dot_nt(a, b):
    return lax.dot_general(a, b, (((1,), (1,)), ((), ())), preferred_element_type=F32)


def _dot_tn(a, b):
    return lax.dot_general(a, b, (((0,), (0,)), ((), ())), preferred_element_type=F32)


def _bf(x):
    return x.astype(BF16)


def _split3(x):
    hi = x.astype(BF16)
    r = x - hi.astype(F32)
    mid = r.astype(BF16)
    lo = (r - mid.astype(F32)).astype(BF16)
    return hi, mid, lo


def _dot_exact_l(x, sel, pieces=3):
    hi, mid, lo = _split3(x)
    r = _dot(hi, sel) + _dot(mid, sel)
    return r + _dot(lo, sel) if pieces == 3 else r


def _dot_exact_r(sel, x):
    hi, mid, lo = _split3(x)
    return _dot(sel, hi) + _dot(sel, mid) + _dot(sel, lo)


def _sigmoid(x):
    return 1.0 / (1.0 + jnp.exp(-x))


def _softplus(x):
    return jnp.maximum(x, 0.0) + jnp.log(1.0 + jnp.exp(-jnp.abs(x)))


def _iota(shape, axis):
    return lax.broadcasted_iota(jnp.int32, shape, axis)


def _params(*sem):
    return pltpu.CompilerParams(dimension_semantics=sem)


class _Rider:
    def __init__(self, inputs, out_shapes, n_sems, phases):
        self.inputs, self.out_shapes, self.n_sems, self.phases = tuple(inputs), tuple(out_shapes), n_sems, phases

    def in_specs(self):
        return [pl.BlockSpec(memory_space=pl.ANY)] * len(self.inputs)

    def out_specs(self):
        return [pl.BlockSpec(memory_space=pl.ANY)] * len(self.out_shapes)

    def scratch(self):
        return [pltpu.SemaphoreType.DMA((self.n_sems,)), pltpu.SemaphoreType.DMA((self.n_sems,))]

    def run(self, step, n_steps, ins, outs, send_sems, recv_sems, last):
        for frac, fn in self.phases:
            if (frac >= 1.0) != last:
                continue
            at = min(int(frac * n_steps), n_steps - 1)

            @pl.when(step == at)
            def _(fn=fn):
                fn(ins, outs, send_sems, recv_sems)


def _mesh_pos():
    return lax.axis_index("x"), lax.axis_index("y"), lax.axis_index("c")


def _gather_rider(shards, relay_at, pass_at):
    n = len(shards)

    def tools(a, ins, outs, send_sems, recv_sems):
        x, y, cc = _mesh_pos()
        nbrs = [(1 - x, y), (x, 1 - y)]
        diag = (1 - x, 1 - y)
        relay_from, relay_to = (x ^ cc, y ^ (1 - cc)), (x ^ (1 - cc), y ^ cc)

        def slot(px, py, pc):
            return outs[a].at[4 * px + 2 * py + pc]

        def copy(k, block, to, src=None):
            return pltpu.make_async_remote_copy(
                src_ref=slot(*block) if src is None else src, dst_ref=slot(*block),
                send_sem=send_sems.at[8 * a + k], recv_sem=recv_sems.at[8 * a + k], device_id=to, device_id_type=MESH)

        me, sibling = (x, y, cc), (x, y, 1 - cc)
        return dict(
            mine=lambda: pltpu.make_async_copy(ins[a], slot(*me), send_sems.at[8 * a + 7]),
            first=lambda: [copy(0, me, sibling, src=ins[a])] + [copy(1 + j, me, (*chip, cc), src=ins[a])
                                                                for j, chip in enumerate(nbrs)],
            landed=lambda: [copy(1 + j, (*chip, cc), me) for j, chip in enumerate(nbrs)],
            relay=lambda: copy(3, (*relay_from, cc), (*relay_to, cc)),
            relayed_in=lambda: copy(3, (*diag, cc), me),
            passed=lambda: [copy(4 + j, (*chip, cc), sibling) for j, chip in enumerate(nbrs)],
            passed_diag=lambda: copy(6, (*diag, cc), sibling),
            from_sibling=lambda: [copy(0, sibling, me)] + [copy(4 + j, (*chip, 1 - cc), me)
                                                           for j, chip in enumerate(nbrs + [diag])])

    def start(*refs):
        for a in range(n):
            t = tools(a, *refs)
            t["mine"]().start()
            for cp in t["first"]():
                cp.start()

    def relay(*refs):
        for a in range(n):
            t = tools(a, *refs)
            for got in t["landed"]():
                got.wait_recv()
            t["relay"]().start()
            for cp in t["passed"]():
                cp.start()

    def pass_diag(*refs):
        for a in range(n):
            t = tools(a, *refs)
            t["relayed_in"]().wait_recv()
            t["passed_diag"]().start()

    def finish(*refs):
        for a in range(n):
            t = tools(a, *refs)
            for cp in t["from_sibling"]():
                cp.wait_recv()
            for cp in t["first"]() + [t["relay"]()] + t["passed"]() + [t["passed_diag"]()]:
                cp.wait_send()
            t["mine"]().wait()

    outs = [jax.ShapeDtypeStruct((N_DEV,) + s.shape, s.dtype) for s in shards]
    return _Rider(shards, outs, 8 * n, ((0.0, start), (relay_at, relay), (pass_at, pass_diag), (1.0, finish)))


def _scatter_rider(blocks):
    n = len(blocks)

    def copies(a, ins, outs, send_sems, recv_sems, with_back):
        x, y, cc = _mesh_pos()
        my_idx = 4 * x + 2 * y + cc
        mine = pltpu.make_async_copy(ins[a].at[my_idx], outs[a].at[my_idx], send_sems.at[8 * a + 7])
        out, back = [], []
        for k in range(1, N_DEV):
            px, py, pc = x ^ (k >> 2), y ^ ((k >> 1) & 1), cc ^ (k & 1)
            peer_idx = 4 * px + 2 * py + pc
            sems = dict(send_sem=send_sems.at[8 * a + k - 1], recv_sem=recv_sems.at[8 * a + k - 1],
                        device_id=(px, py, pc), device_id_type=MESH)
            out.append(pltpu.make_async_remote_copy(src_ref=ins[a].at[peer_idx], dst_ref=outs[a].at[my_idx], **sems))
            if with_back:
                back.append(pltpu.make_async_remote_copy(src_ref=ins[a].at[my_idx], dst_ref=outs[a].at[peer_idx], **sems))
        return mine, out, back

    def start(*refs):
        for a in range(n):
            mine, out, _ = copies(a, *refs, False)
            mine.start()
            for cp in out:
                cp.start()

    def finish(*refs):
        for a in range(n):
            mine, out, back = copies(a, *refs, True)
            for cp in back:
                cp.wait_recv()
            for cp in out:
                cp.wait_send()
            mine.wait()

    outs = [jax.ShapeDtypeStruct(b.shape, b.dtype) for b in blocks]
    return _Rider(blocks, outs, 8 * n, ((0.0, start), (1.0, finish)))


def _sibling_rider(blocks):
    _, r, c = blocks.shape

    def copies(ins, outs, send_sems, recv_sems):
        x, y, cc = _mesh_pos()
        return [pltpu.make_async_remote_copy(
            src_ref=ins[0].at[2 * j + 1 - cc], dst_ref=outs[0].at[j], send_sem=send_sems.at[j],
            recv_sem=recv_sems.at[j], device_id=(x, y, 1 - cc), device_id_type=MESH) for j in range(4)]

    def start(*refs):
        for cp in copies(*refs):
            cp.start()

    def finish(*refs):
        for cp in copies(*refs):
            cp.wait_recv()
        for cp in copies(*refs):
            cp.wait_send()

    return _Rider([blocks], [jax.ShapeDtypeStruct((4, r, c), blocks.dtype)], 4, ((0.0, start), (1.0, finish)))


def _chip_rider(partial, row0, rows):
    _, _, c = partial.shape

    def copies(ins, outs, send_sems, recv_sems, with_back):
        x, y, cc = _mesh_pos()
        my_chip = 2 * x + y

        def rows_of(j):
            return ins[0].at[j, pl.ds(row0, rows)]

        mine = pltpu.make_async_copy(rows_of(my_chip), outs[0].at[my_chip], send_sems.at[3])
        out, back = [], []
        for k in range(1, 4):
            px, py = x ^ (k >> 1), y ^ (k & 1)
            peer = 2 * px + py
            sems = dict(send_sem=send_sems.at[k - 1], recv_sem=recv_sems.at[k - 1], device_id=(px, py, cc),
                        device_id_type=MESH)
            out.append(pltpu.make_async_remote_copy(src_ref=rows_of(peer), dst_ref=outs[0].at[my_chip], **sems))
            if with_back:
                back.append(pltpu.make_async_remote_copy(src_ref=rows_of(my_chip), dst_ref=outs[0].at[peer], **sems))
        return mine, out, back

    def start(*refs):
        mine, out, _ = copies(*refs, False)
        mine.start()
        for cp in out:
            cp.start()

    def finish(*refs):
        mine, out, back = copies(*refs, True)
        for cp in back:
            cp.wait_recv()
        for cp in out:
            cp.wait_send()
        mine.wait()

    return _Rider([partial], [jax.ShapeDtypeStruct((4, rows, c), partial.dtype)], 4, ((0.0, start), (1.0, finish)))


def _matmul(a, b, mode, out_dtype, name, add=None, rider=None, b_rows=None, extra=None):
    b_shape = b.shape if b_rows is None else (b_rows[1], b.shape[1])
    if mode == "nn":
        (m, k), (k2, n) = a.shape, b_shape
    elif mode == "nt":
        (m, k), (n, k2) = a.shape, b_shape
    else:
        (k, m), (k2, n) = a.shape, b_shape
    assert k == k2, (a.shape, b_shape, mode)
    tm, tn, tk = min(m, MM_TILE), min(n, MM_TILE), min(k, MM_TK)
    if n % MM_TILE_WIDE == 0 and (k <= MM_TK or (add is None and jnp.dtype(out_dtype).itemsize == 2)):
        tn = MM_TILE_WIDE
    assert m % tm == 0 and n % tn == 0 and k % tk == 0, (m, n, k)
    nk = k // tk
    b_tile_rows = tn if mode == "nt" else tk
    b_off = 0 if b_rows is None else b_rows[0] // b_tile_rows
    assert b_rows is None or b_rows[0] % b_tile_rows == 0, (b_rows, b_tile_rows)
    dot = {"nn": _dot, "nt": _dot_nt, "tn": _dot_tn}[mode]

    gm, gn = m // tm, n // tn
    n_rin = len(rider.inputs) if rider else 0
    n_rout = len(rider.out_shapes) if rider else 0
    n_add = 1 if add is not None else 0

    n_extra = 2 if extra is not None else 0

    def body(*refs):
        a_ref, b_ref = refs[:2]
        add_ref = refs[2] if n_add else None
        extra_refs = refs[2 + n_add:2 + n_add + n_extra]
        first = 2 + n_add + n_extra
        r_ins = refs[first:first + n_rin]
        o_ref = refs[first + n_rin]
        r_outs = refs[first + n_rin + 1:first + n_rin + 1 + n_rout]
        scratch = refs[first + n_rin + 1 + n_rout:]
        acc_ref = scratch[0] if nk > 1 else None
        kk = pl.program_id(2)
        if rider:
            step = (pl.program_id(0) * gn + pl.program_id(1)) * nk + kk
            ride = functools.partial(rider.run, step, gm * gn * nk, r_ins, r_outs, scratch[-2], scratch[-1])
            ride(last=False)

        def finish(r):
            if extra_refs:
                r = r + _dot(_bf(extra_refs[0][...]), _bf(extra_refs[1][...]))
            if add_ref is not None:
                r = r + add_ref[...]
            o_ref[...] = r.astype(o_ref.dtype)

        if nk == 1:
            finish(dot(_bf(a_ref[...]), _bf(b_ref[...])))
        else:
            @pl.when(kk == 0)
            def _():
                acc_ref[...] = dot(_bf(a_ref[...]), _bf(b_ref[...]))

            @pl.when(jnp.logical_and(kk > 0, kk < nk - 1))
            def _():
                acc_ref[...] += dot(_bf(a_ref[...]), _bf(b_ref[...]))

            @pl.when(kk == nk - 1)
            def _():
                finish(acc_ref[...] + dot(_bf(a_ref[...]), _bf(b_ref[...])))
        if rider:
            ride(last=True)

    if mode == "nn":
        a_spec = pl.BlockSpec((tm, tk), lambda i, j, kk: (i, kk))
        b_spec = pl.BlockSpec((tk, tn), lambda i, j, kk: (kk + b_off, j))
    elif mode == "nt":
        a_spec = pl.BlockSpec((tm, tk), lambda i, j, kk: (i, kk))
        b_spec = pl.BlockSpec((tn, tk), lambda i, j, kk: (j + b_off, kk))
    else:
        a_spec = pl.BlockSpec((tk, tm), lambda i, j, kk: (kk, i))
        b_spec = pl.BlockSpec((tk, tn), lambda i, j, kk: (kk + b_off, j))
    o_spec = pl.BlockSpec((tm, tn), lambda i, j, kk: (i, j))
    in_specs = [a_spec, b_spec] + ([o_spec] if add is not None else [])
    args = (a, b) + ((add,) if add is not None else ())
    if extra is not None:
        (m2, k_extra), (k_extra2, n2) = extra[0].shape, extra[1].shape
        assert (m2, n2, k_extra) == (m, n, k_extra2), (extra[0].shape, extra[1].shape)
        in_specs += [pl.BlockSpec((tm, k_extra), lambda i, j, kk: (i, 0)),
                     pl.BlockSpec((k_extra, tn), lambda i, j, kk: (0, j))]
        args += tuple(extra)
    block_bytes = (tm * tk * a.dtype.itemsize + tk * tn * b.dtype.itemsize
                   + tm * tn * (jnp.dtype(out_dtype).itemsize + (4 if add is not None else 0)))
    vmem = min(2 * block_bytes + 2 * tm * tn * 4 + 2 * (tm + tn) * tk + (4 << 20), VMEM_CAP)
    out_shape = jax.ShapeDtypeStruct((m, n), out_dtype)
    scratch = [pltpu.VMEM((tm, tn), F32)] if nk > 1 else []
    if rider:
        in_specs = in_specs + rider.in_specs()
        args = args + rider.inputs
        out_shape = (out_shape, *rider.out_shapes)
        o_spec = (o_spec, *rider.out_specs())
        scratch = scratch + rider.scratch()
    sem = ("arbitrary",) * 3 if rider else ("parallel", "parallel", "arbitrary")
    return pl.pallas_call(
        body,
        out_shape=out_shape,
        grid=(gm, gn, nk),
        in_specs=in_specs,
        out_specs=o_spec,
        scratch_shapes=scratch,
        compiler_params=pltpu.CompilerParams(dimension_semantics=sem, vmem_limit_bytes=int(vmem)),
        name=name,
    )(*args)


def _row_spec(width, tile=ROW_TILE):
    return pl.BlockSpec((tile, width), lambda i: (i, 0))


def _full_spec(shape):
    nd = len(shape)
    return pl.BlockSpec(shape, lambda *_: (0,) * nd)


def _tiling_2d(r, c):
    if r <= ROW_TILE or r % ROW_TILE == 0:
        tr = min(r, ROW_TILE)
        return (tr, c), r // tr, (lambda i: (i, 0))
    tc = 128 if r > 4096 else 256
    assert c % tc == 0, (r, c)
    return (r, tc), c // tc, (lambda i: (0, i))


def _spec_2d(r, c):
    blk, grid, idx = _tiling_2d(r, c)
    return pl.BlockSpec(blk, idx), grid


def _cast_bf16(x, name):
    r, c = x.shape
    sp, grid = _spec_2d(r, c)

    def body(x_ref, o_ref):
        o_ref[...] = _bf(x_ref[...])

    return pl.pallas_call(
        body, out_shape=jax.ShapeDtypeStruct((r, c), BF16), grid=(grid,), in_specs=[sp], out_specs=sp,
        compiler_params=_params("parallel"), name=name)(x)


def _rmsnorm_fwd(x, w, name, rider):
    s, d = x.shape
    steps = s // ROW_TILE
    n_rin, n_rout = len(rider.inputs), len(rider.out_shapes)

    def body(x_ref, w_ref, *refs):
        r_ins, o_ref, r_outs = refs[:n_rin], refs[n_rin], refs[n_rin + 1:n_rin + 1 + n_rout]
        ride = functools.partial(rider.run, pl.program_id(0), steps, r_ins, r_outs, refs[-2], refs[-1])
        ride(last=False)
        xv = x_ref[...]
        rstd = lax.rsqrt(jnp.mean(xv * xv, axis=1, keepdims=True) + EPS)
        o_ref[...] = _bf(xv * rstd * w_ref[...])
        ride(last=True)

    return pl.pallas_call(
        body, out_shape=(jax.ShapeDtypeStruct((s, d), BF16), *rider.out_shapes), grid=(steps,),
        in_specs=[_row_spec(d), _full_spec((1, d)), *rider.in_specs()],
        out_specs=(_row_spec(d), *rider.out_specs()), scratch_shapes=rider.scratch(),
        compiler_params=_params("arbitrary"), name=name)(x, w, *rider.inputs)


def _rmsnorm_bwd(x, w, dh, dres, name):
    s, d = x.shape

    def body(x_ref, w_ref, dh_ref, dres_ref, dx_ref, dw_ref):
        @pl.when(pl.program_id(0) == 0)
        def _():
            dw_ref[...] = jnp.zeros_like(dw_ref)

        xv = x_ref[...]
        rstd = lax.rsqrt(jnp.mean(xv * xv, axis=1, keepdims=True) + EPS)
        xhat = xv * rstd
        dhv = dh_ref[...]
        dxhat = dhv * w_ref[...]
        dx = rstd * (dxhat - xhat * jnp.mean(dxhat * xhat, axis=1, keepdims=True))
        dx_ref[...] = dx + dres_ref[...]
        dw_ref[...] += jnp.sum(dhv * xhat, axis=0, keepdims=True)

    return pl.pallas_call(
        body,
        out_shape=(jax.ShapeDtypeStruct((s, d), F32), jax.ShapeDtypeStruct((1, d), F32)),
        grid=(s // ROW_TILE,),
        in_specs=[_row_spec(d), _full_spec((1, d)), _row_spec(d), _row_spec(d)],
        out_specs=(_row_spec(d), _full_spec((1, d))),
        compiler_params=_params("arbitrary"), name=name)(x, w, dh, dres)


def _rope_tables(pos_col, inv_freq, name):
    s = pos_col.shape[0]

    def body(p_ref, f_ref, cos_ref, sin_ref):
        ang = p_ref[...].astype(F32) * f_ref[...]
        cos_ref[...] = jnp.cos(ang)
        sin_ref[...] = jnp.sin(ang)

    out = jax.ShapeDtypeStruct((s, ROPE_HALF), F32)
    return pl.pallas_call(
        body, out_shape=(out, out), grid=(s // ROW_TILE,),
        in_specs=[_row_spec(1), _full_spec((1, ROPE_HALF))],
        out_specs=(_row_spec(ROPE_HALF), _row_spec(ROPE_HALF)),
        compiler_params=_params("parallel"), name=name)(pos_col, inv_freq)


def _merge_fwd(pg, p_r, p_s, name):
    s = pg.shape[0]

    def body(g_ref, r_ref, s_ref, o_ref):
        g = g_ref[...].astype(F32)
        o_ref[...] = _bf(_sigmoid(g[:, :D_MODEL]) * r_ref[...].astype(F32)
                         + _sigmoid(g[:, D_MODEL:]) * s_ref[...].astype(F32))

    return pl.pallas_call(
        body, out_shape=jax.ShapeDtypeStruct((s, D_MODEL), BF16), grid=(s // ROW_TILE,),
        in_specs=[_row_spec(2 * D_MODEL), _row_spec(D_MODEL), _row_spec(D_MODEL)],
        out_specs=_row_spec(D_MODEL), compiler_params=_params("parallel"), name=name)(pg, p_r, p_s)


def _merge_bwd(pg, p_r, p_s, dm, name):
    s = pg.shape[0]

    def body(g_ref, r_ref, s_ref, dm_ref, dr_ref, ds_ref, dg_ref):
        g = g_ref[...].astype(F32)
        sr, ss = _sigmoid(g[:, :D_MODEL]), _sigmoid(g[:, D_MODEL:])
        d = dm_ref[...]
        dr_ref[...] = _bf(d * sr)
        ds_ref[...] = _bf(d * ss)
        dg_ref[:, :D_MODEL] = _bf(d * r_ref[...].astype(F32) * sr * (1.0 - sr))
        dg_ref[:, D_MODEL:] = _bf(d * s_ref[...].astype(F32) * ss * (1.0 - ss))

    o = jax.ShapeDtypeStruct((s, D_MODEL), BF16)
    return pl.pallas_call(
        body, out_shape=(o, o, jax.ShapeDtypeStruct((s, 2 * D_MODEL), BF16)), grid=(s // ROW_TILE,),
        in_specs=[_row_spec(2 * D_MODEL), _row_spec(D_MODEL), _row_spec(D_MODEL), _row_spec(D_MODEL)],
        out_specs=(_row_spec(D_MODEL), _row_spec(D_MODEL), _row_spec(2 * D_MODEL)),
        compiler_params=_params("parallel"), name=name)(pg, p_r, p_s, dm)


def _final_fwd_bwd(x, o, w, target, name):
    s, d = x.shape

    def body(x_ref, o_ref, w_ref, t_ref, dx_ref, dw_ref, loss_ref):
        @pl.when(pl.program_id(0) == 0)
        def _():
            dw_ref[...] = jnp.zeros_like(dw_ref)
            loss_ref[...] = jnp.zeros_like(loss_ref)

        x2 = x_ref[...] + o_ref[...]
        rstd = lax.rsqrt(jnp.mean(x2 * x2, axis=1, keepdims=True) + EPS)
        xhat = x2 * rstd
        wv = w_ref[...]
        err = xhat * wv - t_ref[...]
        loss_ref[...] += jnp.sum(jnp.sum(err * err, axis=1, keepdims=True), axis=0, keepdims=True) * (0.5 / d)
        dy = err * (1.0 / d)
        dw_ref[...] += jnp.sum(dy * xhat, axis=0, keepdims=True)
        dxhat = dy * wv
        dx_ref[...] = rstd * (dxhat - xhat * jnp.mean(dxhat * xhat, axis=1, keepdims=True))

    return pl.pallas_call(
        body,
        out_shape=(jax.ShapeDtypeStruct((s, d), F32), jax.ShapeDtypeStruct((1, d), F32),
                   jax.ShapeDtypeStruct((8, 128), F32)),
        grid=(s // ROW_TILE,),
        in_specs=[_row_spec(d), _row_spec(d), _full_spec((1, d)), _row_spec(d)],
        out_specs=(_row_spec(d), _full_spec((1, d)), _full_spec((8, 128))),
        compiler_params=_params("arbitrary"), name=name)(x, o, w, target)


def _adamw(w, g, m, v, name):
    r, c = w.shape
    sp, grid = _spec_2d(r, c)
    c1 = 1.0 / (1.0 - ADAM_B1 ** ADAM_STEP)
    c2 = 1.0 / (1.0 - ADAM_B2 ** ADAM_STEP)

    def body(w_ref, g_ref, m_ref, v_ref, d_ref, nm_ref, nv_ref):
        gv = g_ref[...]
        nm = ADAM_B1 * m_ref[...] + (1.0 - ADAM_B1) * gv
        nv = ADAM_B2 * v_ref[...] + (1.0 - ADAM_B2) * (gv * gv)
        d_ref[...] = -ADAM_LR * ((nm * c1) / (jnp.sqrt(nv * c2) + ADAM_EPS) + ADAM_WD * w_ref[...])
        nm_ref[...] = nm
        nv_ref[...] = nv

    o = jax.ShapeDtypeStruct((r, c), F32)
    return pl.pallas_call(
        body, out_shape=(o, o, o), grid=(grid,), in_specs=[sp, sp, sp, sp], out_specs=(sp, sp, sp),
        compiler_params=_params("parallel"), name=name)(w, g, m, v)


def _sum_slots(land, name):
    n, r, c = land.shape
    blk, grid, idx = _tiling_2d(r, c)

    def body(l_ref, o_ref):
        acc = l_ref[0].astype(F32)
        for i in range(1, n):
            acc = acc + l_ref[i].astype(F32)
        o_ref[...] = acc

    return pl.pallas_call(
        body, out_shape=jax.ShapeDtypeStruct((r, c), F32), grid=(grid,),
        in_specs=[pl.BlockSpec((n, *blk), lambda i: (0, *idx(i)))], out_specs=pl.BlockSpec(blk, idx),
        compiler_params=_params("parallel"), name=name)(land)


SLAB_COLS = 256


def _chip_sum(blocks, from_sibling, my_core, name):
    _, r, c = blocks.shape

    def body(core_ref, a_ref, b_ref, o_ref):
        o_ref[...] = _bf(a_ref[...].astype(F32) + b_ref[...].astype(F32))

    def slab(chip_of):
        return pl.BlockSpec((1, r, SLAB_COLS), lambda j, i, core: (chip_of(j, core), 0, i))

    grid_spec = pltpu.PrefetchScalarGridSpec(
        num_scalar_prefetch=1, grid=(4, c // SLAB_COLS),
        in_specs=[slab(lambda j, core: 2 * j + core[0]), slab(lambda j, core: j)],
        out_specs=slab(lambda j, core: j))
    return pl.pallas_call(
        body, out_shape=jax.ShapeDtypeStruct((4, r, c), BF16), grid_spec=grid_spec,
        compiler_params=_params("parallel", "parallel"), name=name,
    )(my_core.astype(jnp.int32).reshape(1), blocks, from_sibling)


def _sum_row_parts(parts, name):
    c = parts[0].shape[2]
    rows = [p.shape[1] for p in parts]

    def body(*refs):
        o_ref, off = refs[-1], 0
        for p_ref, n in zip(refs[:-1], rows):
            acc = p_ref[0].astype(F32)
            for i in range(1, p_ref.shape[0]):
                acc = acc + p_ref[i].astype(F32)
            o_ref[off:off + n, :] = acc
            off += n

    return pl.pallas_call(
        body, out_shape=jax.ShapeDtypeStruct((sum(rows), c), F32), grid=(c // SLAB_COLS,),
        in_specs=[pl.BlockSpec((p.shape[0], p.shape[1], SLAB_COLS), lambda i: (0, 0, i)) for p in parts],
        out_specs=pl.BlockSpec((sum(rows), SLAB_COLS), lambda i: (0, i)),
        compiler_params=_params("parallel"), name=name)(*parts)


def _retention_tables():
    lg = np.log1p(-(2.0 ** (-5.0 - np.arange(RET_HEADS, dtype=np.float64))))
    idx = np.arange(CHUNK, dtype=np.float64)
    intra = np.exp(np.abs(idx[:, None] - idx[None, :])[None] * lg[:, None, None])
    qd = np.exp((idx[None, :] + 1.0) * lg[:, None])
    kd = np.exp((CHUNK - 1.0 - idx[None, :]) * lg[:, None])
    cd = np.exp(CHUNK * lg)
    ones = np.ones((1, 1, RET_DK))
    return (jnp.asarray(intra, F32), jnp.asarray(qd[:, :, None] * ones, F32),
            jnp.asarray(kd[:, :, None] * ones, F32), jnp.asarray(cd[:, None, None] * ones, F32))


def _rope(t, cos, sin):
    t1, t2 = t[:, :ROPE_HALF], t[:, ROPE_HALF:]
    return jnp.concatenate([t1 * cos - t2 * sin, t2 * cos + t1 * sin], axis=1)


def _rope_t(d, cos, sin):
    d1, d2 = d[:, :ROPE_HALF], d[:, ROPE_HALF:]
    return jnp.concatenate([d1 * cos + d2 * sin, d2 * cos - d1 * sin], axis=1)


_HEADS = range(RET_HEADS)


def _ret_chunk_fwd(p_ref, cos, sin, intra_ref, qd_ref, st):
    def seg(i, h):
        return p_ref[:, i * RET_W + h * RET_DK:i * RET_W + (h + 1) * RET_DK]

    v = [seg(2, h) for h in _HEADS]
    qr = [_rope(seg(0, h), cos, sin) for h in _HEADS]
    kr = [_rope(seg(1, h), cos, sin) * (RET_DK ** -0.5) for h in _HEADS]
    qrb, vb = [_bf(a) for a in qr], [_bf(a) for a in v]
    sc = [_dot_nt(qrb[h], _bf(kr[h])) * intra_ref[h] for h in _HEADS]
    qs = [_dot(qrb[h], _bf(st[h])) for h in _HEADS]
    y = [_dot(_bf(sc[h]), vb[h]) + qs[h] * qd_ref[h] for h in _HEADS]
    g = [seg(3, h) for h in _HEADS]
    return dict(v=v, vb=vb, qr=qr, qrb=qrb, kr=kr, sc=sc, y=y, g=g)


def _group_norm(y):
    mu = jnp.mean(y, axis=1, keepdims=True)
    yc = y - mu
    rstd = lax.rsqrt(jnp.mean(yc * yc, axis=1, keepdims=True) + EPS)
    return yc * rstd, rstd


def _chunk_rows(sub):
    return pl.ds(sub * CHUNK, CHUNK)


def _ret_specs(steps, rev):
    cidx = (lambda c: steps - 1 - c) if rev else (lambda c: c)
    return dict(
        proj=pl.BlockSpec((RET_STEP_ROWS, 4 * RET_W), lambda c: (cidx(c), 0)),
        half=pl.BlockSpec((RET_STEP_ROWS, ROPE_HALF), lambda c: (cidx(c), 0)),
        wide=pl.BlockSpec((RET_STEP_ROWS, RET_W), lambda c: (cidx(c), 0)),
        state=pl.BlockSpec((RET_CHUNKS_PER_STEP, RET_HEADS, RET_DK, RET_DK), lambda c: (cidx(c), 0, 0, 0)),
        intra=_full_spec((RET_HEADS, CHUNK, CHUNK)),
        dec=_full_spec((RET_HEADS, CHUNK, RET_DK)),
        cd=_full_spec((RET_HEADS, 1, RET_DK)),
    )


def _retention_fwd(p_r, cos, sin, name):
    s = p_r.shape[0]
    nc = s // CHUNK
    intra_t, qd_t, kd_t, cd_t = _retention_tables()

    def body(p_ref, cos_ref, sin_ref, intra_ref, qd_ref, kd_ref, cd_ref, y_ref, st_ref, state):
        @pl.when(pl.program_id(0) == 0)
        def _():
            state[...] = jnp.zeros_like(state)

        for sub in range(RET_CHUNKS_PER_STEP):
            rows = _chunk_rows(sub)
            y_v, st_v = y_ref.at[rows], st_ref.at[sub]
            st = [state[h] for h in _HEADS]
            f = _ret_chunk_fwd(p_ref.at[rows], cos_ref[rows, :], sin_ref[rows, :], intra_ref, qd_ref, st)
            new_st = [st[h] * cd_ref[h] + _dot_tn(_bf(f["kr"][h] * kd_ref[h]), f["vb"][h]) for h in _HEADS]
            out = [_bf(_group_norm(f["y"][h])[0] * (f["g"][h] * _sigmoid(f["g"][h]))) for h in _HEADS]
            for h in _HEADS:
                st_v[h] = _bf(st[h])
                state[h] = new_st[h]
                y_v[:, h * RET_DK:(h + 1) * RET_DK] = out[h]

    steps = nc // RET_CHUNKS_PER_STEP
    sp = _ret_specs(steps, False)
    return pl.pallas_call(
        body,
        out_shape=(jax.ShapeDtypeStruct((s, RET_W), BF16),
                   jax.ShapeDtypeStruct((nc, RET_HEADS, RET_DK, RET_DK), BF16)),
        grid=(steps,),
        in_specs=[sp["proj"], sp["half"], sp["half"], sp["intra"], sp["dec"], sp["dec"], sp["cd"]],
        out_specs=(sp["wide"], sp["state"]),
        scratch_shapes=[pltpu.VMEM((RET_HEADS, RET_DK, RET_DK), F32)],
        compiler_params=_params("arbitrary"), name=name)(p_r, cos, sin, intra_t, qd_t, kd_t, cd_t)


def _retention_bwd(p_r, cos, sin, states, dy_r, name):
    s = p_r.shape[0]
    nc = s // CHUNK
    intra_t, qd_t, kd_t, cd_t = _retention_tables()

    def body(p_ref, cos_ref, sin_ref, intra_ref, qd_ref, kd_ref, cd_ref, st_ref, dy_ref, dp_ref, dstate):
        @pl.when(pl.program_id(0) == 0)
        def _():
            dstate[...] = jnp.zeros_like(dstate)

        for sub in reversed(range(RET_CHUNKS_PER_STEP)):
            rows = _chunk_rows(sub)
            chunk(p_ref.at[rows], cos_ref[rows, :], sin_ref[rows, :], intra_ref, qd_ref, kd_ref, cd_ref,
                  st_ref.at[sub], dy_ref.at[rows], dp_ref.at[rows], dstate)

    def chunk(p_ref, cos, sin, intra_ref, qd_ref, kd_ref, cd_ref, st_ref, dy_ref, dp_ref, dstate):
        st = [st_ref[h] for h in _HEADS]
        dsn = [dstate[h] for h in _HEADS]
        f = _ret_chunk_fwd(p_ref, cos, sin, intra_ref, qd_ref, st)
        vb, qrb, kr, g = f["vb"], f["qrb"], f["kr"], f["g"]
        norm = [_group_norm(f["y"][h]) for h in _HEADS]
        sg = [_sigmoid(g[h]) for h in _HEADS]
        dyr = [dy_ref[:, h * RET_DK:(h + 1) * RET_DK] for h in _HEADS]
        dyn = [dyr[h] * (g[h] * sg[h]) for h in _HEADS]
        dg = [dyr[h] * norm[h][0] * (sg[h] * (1.0 + g[h] * (1.0 - sg[h]))) for h in _HEADS]
        dy = [norm[h][1] * (dyn[h] - jnp.mean(dyn[h], axis=1, keepdims=True)
                            - norm[h][0] * jnp.mean(dyn[h] * norm[h][0], axis=1, keepdims=True)) for h in _HEADS]
        dyb, dsnb = [_bf(a) for a in dy], [_bf(a) for a in dsn]
        ds = [_bf(_dot_nt(dyb[h], vb[h]) * intra_ref[h]) for h in _HEADS]
        t = [_bf(dy[h] * qd_ref[h]) for h in _HEADS]
        dv = [_dot_tn(_bf(f["sc"][h]), dyb[h]) + _dot(_bf(kr[h] * kd_ref[h]), dsnb[h]) for h in _HEADS]
        dqr = [_dot(ds[h], _bf(kr[h])) + _dot_nt(t[h], _bf(st[h])) for h in _HEADS]
        dkr = [_dot_tn(ds[h], qrb[h]) + _dot_nt(vb[h], dsnb[h]) * kd_ref[h] for h in _HEADS]
        new_ds = [dsn[h] * cd_ref[h] + _dot_tn(qrb[h], t[h]) for h in _HEADS]
        for h in _HEADS:
            lo = h * RET_DK
            dstate[h] = new_ds[h]
            dp_ref[:, lo:lo + RET_DK] = _bf(_rope_t(dqr[h], cos, sin))
            dp_ref[:, RET_W + lo:RET_W + lo + RET_DK] = _bf(_rope_t(dkr[h], cos, sin) * (RET_DK ** -0.5))
            dp_ref[:, 2 * RET_W + lo:2 * RET_W + lo + RET_DK] = _bf(dv[h])
            dp_ref[:, 3 * RET_W + lo:3 * RET_W + lo + RET_DK] = _bf(dg[h])

    steps = nc // RET_CHUNKS_PER_STEP
    sp = _ret_specs(steps, True)
    return pl.pallas_call(
        body,
        out_shape=jax.ShapeDtypeStruct((s, 4 * RET_W), BF16),
        grid=(steps,),
        in_specs=[sp["proj"], sp["half"], sp["half"], sp["intra"], sp["dec"], sp["dec"], sp["cd"],
                  sp["state"], sp["wide"]],
        out_specs=sp["proj"],
        scratch_shapes=[pltpu.VMEM((RET_HEADS, RET_DK, RET_DK), F32)],
        compiler_params=_params("arbitrary"), name=name)(p_r, cos, sin, intra_t, qd_t, kd_t, cd_t, states, dy_r)


CONV_SLAB = 256
_CONV_SLABS = [slice(i * CONV_SLAB, (i + 1) * CONV_SLAB) for i in range(SSD_CD // CONV_SLAB)]


def _conv_taps(ext, w):
    acc = w[SSD_CONV - 1:SSD_CONV] * ext
    for j in range(SSD_CONV - 1):
        acc = acc + w[j:j + 1] * pltpu.roll(ext, SSD_CONV - 1 - j, axis=0)
    return acc


def _conv_fwd(xbc_raw, conv_w, conv_b, name):
    s = xbc_raw.shape[0]
    t8 = CONV_TILE // 8

    def body(cur_ref, prev_ref, w_ref, b_ref, o_ref, slope_ref):
        keep = (pl.program_id(0) > 0).astype(F32)
        for sl in _CONV_SLABS:
            ext = jnp.concatenate([prev_ref[:, sl] * keep, cur_ref[:, sl]], axis=0)
            u = _conv_taps(ext, w_ref[:, sl])[8:] + b_ref[:, sl]
            sg = _sigmoid(u)
            o_ref[:, sl] = u * sg
            slope_ref[:, sl] = _bf(sg * (1.0 + u * (1.0 - sg)))

    out = jax.ShapeDtypeStruct((s, SSD_CD), F32)
    return pl.pallas_call(
        body, out_shape=(out, jax.ShapeDtypeStruct((s, SSD_CD), BF16)), grid=(s // CONV_TILE,),
        in_specs=[_row_spec(SSD_CD, CONV_TILE),
                  pl.BlockSpec((8, SSD_CD), lambda i: (jnp.maximum(i * t8 - 1, 0), 0)),
                  _full_spec((SSD_CONV, SSD_CD)), _full_spec((1, SSD_CD))],
        out_specs=(_row_spec(SSD_CD, CONV_TILE), _row_spec(SSD_CD, CONV_TILE)),
        compiler_params=_params("parallel"), name=name)(xbc_raw, xbc_raw, conv_w, conv_b)


def _conv_bwd(xbc_raw, conv_w, slope, dact, name):
    s = xbc_raw.shape[0]
    nt = s // CONV_TILE
    t8 = CONV_TILE // 8
    rows = CONV_TILE + 8

    def body(cur_ref, w_ref, s_ref, snext_ref, d_ref, dnext_ref, dx_ref, dw_ref, db_ref):
        i = pl.program_id(0)

        @pl.when(i == 0)
        def _():
            dw_ref[...] = jnp.zeros_like(dw_ref)
            db_ref[...] = jnp.zeros_like(db_ref)

        keep_next = (i < nt - 1).astype(F32)
        for sl in _CONV_SLABS:
            w = w_ref[:, sl]
            cur = cur_ref[:, sl]
            duc = d_ref[:, sl].astype(F32) * s_ref[:, sl].astype(F32)
            after = (dnext_ref[:, sl].astype(F32) * snext_ref[:, sl].astype(F32))[:8] * keep_next
            du = jnp.concatenate([duc, after], axis=0)
            dx = w[SSD_CONV - 1:SSD_CONV] * duc
            for j in range(SSD_CONV - 1):
                ahead = pltpu.roll(du, rows - (SSD_CONV - 1 - j), axis=0)[:CONV_TILE]
                dx = dx + w[j:j + 1] * ahead
                dw_ref[j:j + 1, sl] += jnp.sum(cur * ahead, axis=0, keepdims=True)
            dx_ref[:, sl] = _bf(dx)
            dw_ref[SSD_CONV - 1:SSD_CONV, sl] += jnp.sum(cur * duc, axis=0, keepdims=True)
            db_ref[:, sl] += jnp.sum(duc, axis=0, keepdims=True)

    row = _row_spec(SSD_CD, CONV_TILE)
    nxt16 = pl.BlockSpec((16, SSD_CD), lambda i: (jnp.minimum((i + 1) * (t8 // 2), s // 16 - 1), 0))
    return pl.pallas_call(
        body,
        out_shape=(jax.ShapeDtypeStruct((s, SSD_CD), BF16), jax.ShapeDtypeStruct((SSD_CONV, SSD_CD), F32),
                   jax.ShapeDtypeStruct((1, SSD_CD), F32)),
        grid=(nt,),
        in_specs=[row, _full_spec((SSD_CONV, SSD_CD)), row, nxt16, row, nxt16],
        out_specs=(row, _full_spec((SSD_CONV, SSD_CD)), _full_spec((1, SSD_CD))),
        compiler_params=_params("arbitrary"), name=name)(xbc_raw, conv_w, slope, slope, dact, dact)


def _head_select():
    e = np.zeros((HEAD_PAD, SSD_W), np.float32)
    for h in range(SSD_HEADS):
        e[h, h * SSD_P:(h + 1) * SSD_P] = 1.0
    return jnp.asarray(e, BF16), jnp.asarray(e.T, BF16)


def _pad_heads(v):
    return jnp.pad(v.reshape(1, SSD_HEADS).astype(F32), ((0, 0), (0, HEAD_PAD - SSD_HEADS)))


def _ssd_masks():
    r = _iota((CHUNK, SSD_QW), 0)
    c = _iota((CHUNK, SSD_QW), 1) % SSD_P
    itile = (r == c).astype(F32)
    ctile = r >= c
    rb = _iota((SSD_QW, SSD_QW), 0) // SSD_P
    cb = _iota((SSD_QW, SSD_QW), 1) // SSD_P
    return itile, ctile, rb == cb


_TILES = [slice(i * SSD_QW, (i + 1) * SSD_QW) for i in range(SSD_W // SSD_QW)]


def _ssd_heads(dtr_ref, dtb_ref, alog_ref):
    u = dtr_ref[...] + dtb_ref[...]
    dt = _softplus(u)
    nexp = -jnp.exp(alog_ref[...])
    return u, dt, nexp, dt * nexp


def _ssd_group(xbc_ref, acol_ref, dte_ref, gs):
    acol = acol_ref[:, gs]
    alast = acol[CHUNK - 1:CHUNK, :]
    xs, dte = xbc_ref[:, gs], dte_ref[:, gs]
    return dict(xs=xs, dte=dte, xdt=xs * dte, ea=jnp.exp(acol), tail=jnp.exp(alast - acol), eal=jnp.exp(alast))


def _silu_gate(z_ref, y_ref, tl):
    zv = z_ref[:, tl]
    sz = _sigmoid(zv)
    return zv, sz, y_ref[:, tl] * (zv * sz)


def _tile4(x):
    return jnp.concatenate([x, x, x, x], axis=0)


def _fold4(x):
    return x[0:CHUNK] + x[CHUNK:2 * CHUNK] + x[2 * CHUNK:3 * CHUNK] + x[3 * CHUNK:4 * CHUNK]


def _ssd_specs(steps, rev):
    cidx = (lambda c: steps - 1 - c) if rev else (lambda c: c)
    return dict(
        xbc=pl.BlockSpec((STEP_ROWS, SSD_CD), lambda c: (cidx(c), 0)),
        dt=pl.BlockSpec((STEP_ROWS, HEAD_PAD), lambda c: (cidx(c), 0)),
        wide=pl.BlockSpec((STEP_ROWS, SSD_W), lambda c: (cidx(c), 0)),
        state=pl.BlockSpec((CHUNKS_PER_STEP, SSD_N, SSD_W), lambda c: (cidx(c), 0, 0)),
        head=_full_spec((1, HEAD_PAD)),
        roww=_full_spec((1, SSD_W)),
        e=_full_spec((HEAD_PAD, SSD_W)),
        et=_full_spec((SSD_W, HEAD_PAD)),
    )


_GROUPS = range(SSD_GROUPS)
_GROUP_LANES = [slice(g * SSD_GW, (g + 1) * SSD_GW) for g in _GROUPS]
_QUADS = [(g, slice(g * SSD_GW + q * SSD_QW, g * SSD_GW + (q + 1) * SSD_QW), slice(q * SSD_QW, (q + 1) * SSD_QW))
          for g in _GROUPS for q in range(SSD_GW // SSD_QW)]


def _ssd_bc(xbc_ref):
    bg = [_bf(xbc_ref[:, B_OFF + g * SSD_N:B_OFF + (g + 1) * SSD_N]) for g in _GROUPS]
    cg = [_bf(xbc_ref[:, C_OFF + g * SSD_N:C_OFF + (g + 1) * SSD_N]) for g in _GROUPS]
    return bg, cg


def _ssd_decay(aq, itile, ctile):
    arow = jnp.sum(aq * itile, axis=0, keepdims=True)
    return jnp.exp(jnp.where(ctile, aq - arow, -jnp.inf))


def _ssd_blockdiag(xq, bdmask):
    return jnp.where(bdmask, _tile4(_bf(xq)), jnp.zeros((), BF16))


def _ssd_fwd(xbc, dt_raw, z, dt_bias, a_log, dske, norm_w, name):
    s = xbc.shape[0]
    nc = s // CHUNK
    e_sel, _ = _head_select()

    def body(xbc_ref, dtr_ref, z_ref, dtb_ref, alog_ref, dske_ref, nw_ref, e_ref,
             yraw_ref, ys_ref, st_ref, acol_ref, dte_ref, state):
        @pl.when(pl.program_id(0) == 0)
        def _():
            state[...] = jnp.zeros_like(state)

        for sub in range(CHUNKS_PER_STEP):
            rows = _chunk_rows(sub)
            chunk(xbc_ref.at[rows], dtr_ref.at[rows], z_ref.at[rows], dtb_ref, alog_ref, dske_ref, nw_ref, e_ref,
                  yraw_ref.at[rows], ys_ref.at[rows], st_ref.at[sub], acol_ref.at[rows], dte_ref.at[rows], state)

    def chunk(xbc_ref, dtr_ref, z_ref, dtb_ref, alog_ref, dske_ref, nw_ref, e_ref,
              yraw_ref, ys_ref, st_ref, acol_ref, dte_ref, state):
        _, dt, _, a = _ssd_heads(dtr_ref, dtb_ref, alog_ref)
        tril = _bf((_iota((CHUNK, CHUNK), 0) >= _iota((CHUNK, CHUNK), 1)).astype(F32))
        ac3, dt3 = _split3(_dot_exact_r(tril, a)), _split3(dt)
        for tl in _TILES:
            e_t = e_ref[:, tl]
            acol_ref[:, tl] = _dot(ac3[0], e_t) + _dot(ac3[1], e_t) + _dot(ac3[2], e_t)
            dte_ref[:, tl] = _dot(dt3[0], e_t) + _dot(dt3[1], e_t)

        itile, ctile, bdmask = _ssd_masks()
        st_ref[...] = state[...]
        bg, cg = _ssd_bc(xbc_ref)
        for g0 in range(0, SSD_GROUPS, SSD_BLOCK_GROUPS):
            gids = range(g0, g0 + SSD_BLOCK_GROUPS)
            quads = [(g, ql, qs) for g, ql, qs in _QUADS if g in gids]
            q = {g: _ssd_group(xbc_ref, acol_ref, dte_ref, _GROUP_LANES[g]) for g in gids}
            stg = {g: state[:, _GROUP_LANES[g]] for g in gids}
            cbt = {g: _dot_nt(cg[g], _tile4(bg[g])) for g in gids}
            ys = {g: _dot(cg[g], _bf(stg[g])) for g in gids}
            dq = [_ssd_decay(acol_ref[:, ql], itile, ctile) for _, ql, _ in quads]
            xbd = [_ssd_blockdiag(q[g]["xdt"][:, qs], bdmask) for g, _, qs in quads]
            yq = [_dot(_bf(cbt[g] * dq[i]), xbd[i]) + ys[g][:, qs] * q[g]["ea"][:, qs]
                  + dske_ref[:, ql] * q[g]["xs"][:, qs] for i, (g, ql, qs) in enumerate(quads)]
            new_st = {g: stg[g] * q[g]["eal"] + _dot_tn(bg[g], _bf(q[g]["xdt"] * q[g]["tail"])) for g in gids}
            for i, (_, ql, _) in enumerate(quads):
                yraw_ref[:, ql] = yq[i]
            for g in gids:
                state[:, _GROUP_LANES[g]] = new_st[g]

        sq = jnp.zeros((CHUNK, SSD_QW), F32)
        for tl in _TILES:
            t = _silu_gate(z_ref, yraw_ref, tl)[2]
            sq = sq + t * t
        rstd = lax.rsqrt(jnp.sum(sq, axis=1, keepdims=True) * (1.0 / SSD_W) + EPS)
        for tl in _TILES:
            ys_ref[:, tl] = _bf(_silu_gate(z_ref, yraw_ref, tl)[2] * rstd * nw_ref[:, tl])

    steps = nc // CHUNKS_PER_STEP
    sp = _ssd_specs(steps, False)
    wide = jax.ShapeDtypeStruct((s, SSD_W), F32)
    return pl.pallas_call(
        body,
        out_shape=(wide, jax.ShapeDtypeStruct((s, SSD_W), BF16), jax.ShapeDtypeStruct((nc, SSD_N, SSD_W), F32),
                   wide, wide),
        grid=(steps,),
        in_specs=[sp["xbc"], sp["dt"], sp["wide"], sp["head"], sp["head"], sp["roww"], sp["roww"], sp["e"]],
        out_specs=(sp["wide"], sp["wide"], sp["state"], sp["wide"], sp["wide"]),
        scratch_shapes=[pltpu.VMEM((SSD_N, SSD_W), F32)],
        compiler_params=_params("arbitrary"), name=name,
    )(xbc, dt_raw, z, dt_bias, a_log, dske, norm_w, e_sel)


def _ssd_bwd(xbc, dt_raw, z, yraw, states, dys, acol, dte, dt_bias, a_log, dske, norm_w, name, rider):
    s = xbc.shape[0]
    nc = s // CHUNK
    steps = nc // CHUNKS_PER_STEP
    _, et_sel = _head_select()
    n_in, n_out, n_scratch = 13, 7, 5
    n_rin, n_rout = len(rider.inputs), len(rider.out_shapes)

    def body(*refs):
        ins, refs = refs[:n_in], refs[n_in:]
        r_ins, refs = refs[:n_rin], refs[n_rin:]
        outs, refs = refs[:n_out], refs[n_out:]
        r_outs, refs = refs[:n_rout], refs[n_rout:]
        ride = functools.partial(rider.run, pl.program_id(0), steps, r_ins, r_outs, refs[n_scratch],
                                 refs[n_scratch + 1])
        ride(last=False)
        compute(*ins, *outs, *refs[:n_scratch])
        ride(last=True)

    def compute(xbc_ref, dtr_ref, z_ref, yraw_ref, st_ref, dys_ref, acol_ref, dte_ref, dtb_ref, alog_ref,
                dske_ref, nw_ref, et_ref, dxbc_ref, dz_ref, ddtr_ref, ddtb_ref, dalog_ref, ddsk_ref, dnw_ref,
                dstate, dsk_acc, dy_s, dacol_s, ddte_s):
        @pl.when(pl.program_id(0) == 0)
        def _():
            dstate[...] = jnp.zeros_like(dstate)
            dsk_acc[...] = jnp.zeros_like(dsk_acc)
            ddtb_ref[...] = jnp.zeros_like(ddtb_ref)
            dalog_ref[...] = jnp.zeros_like(dalog_ref)
            dnw_ref[...] = jnp.zeros_like(dnw_ref)

        for sub in reversed(range(CHUNKS_PER_STEP)):
            rows = _chunk_rows(sub)
            chunk(xbc_ref.at[rows], dtr_ref.at[rows], z_ref.at[rows], yraw_ref.at[rows], st_ref.at[sub],
                  dys_ref.at[rows], acol_ref.at[rows], dte_ref.at[rows], dtb_ref, alog_ref, dske_ref, nw_ref, et_ref,
                  dxbc_ref.at[rows], dz_ref.at[rows], ddtr_ref.at[rows], ddtb_ref, dalog_ref, dnw_ref,
                  dstate, dsk_acc, dy_s, dacol_s, ddte_s)

        @pl.when(pl.program_id(0) == steps - 1)
        def _():
            ddsk_ref[...] = _dot_exact_l(jnp.broadcast_to(dsk_acc[...], (8, SSD_W)), et_ref[...])[0:1]

    def chunk(xbc_ref, dtr_ref, z_ref, yraw_ref, st_ref, dys_ref, acol_ref, dte_ref, dtb_ref, alog_ref,
              dske_ref, nw_ref, et_ref, dxbc_ref, dz_ref, ddtr_ref, ddtb_ref, dalog_ref, dnw_ref,
              dstate, dsk_acc, dy_s, dacol_s, ddte_s):
        itile, ctile, bdmask = _ssd_masks()
        last_row = (_iota((CHUNK, 1), 0) == CHUNK - 1).astype(F32)

        s1 = jnp.zeros((CHUNK, SSD_QW), F32)
        s2 = jnp.zeros((CHUNK, SSD_QW), F32)
        for tl in _TILES:
            t = _silu_gate(z_ref, yraw_ref, tl)[2]
            s1 = s1 + t * t
            s2 = s2 + dys_ref[:, tl] * nw_ref[:, tl] * t
        rstd = lax.rsqrt(jnp.sum(s1, axis=1, keepdims=True) * (1.0 / SSD_W) + EPS)
        back = rstd * rstd * rstd * jnp.sum(s2, axis=1, keepdims=True) * (1.0 / SSD_W)
        for tl in _TILES:
            zv, sz, t = _silu_gate(z_ref, yraw_ref, tl)
            dysv = dys_ref[:, tl]
            dt_ = rstd * (dysv * nw_ref[:, tl]) - t * back
            dnw_ref[:, tl] += jnp.sum(dysv * t * rstd, axis=0, keepdims=True)
            dz_ref[:, tl] = _bf(dt_ * yraw_ref[:, tl] * (sz * (1.0 + zv * (1.0 - sz))))
            dy_t = dt_ * (zv * sz)
            dy_s[:, tl] = dy_t
            dsk_acc[:, tl] += jnp.sum(dy_t * xbc_ref[:, tl], axis=0, keepdims=True)

        gl = _GROUP_LANES
        bg, cg = _ssd_bc(xbc_ref)
        for g0 in range(0, SSD_GROUPS, SSD_BLOCK_GROUPS):
            gids = range(g0, g0 + SSD_BLOCK_GROUPS)
            quads = [(i, g, ql, qs) for i, (g, ql, qs) in enumerate(_QUADS) if g in gids]
            q = {g: _ssd_group(xbc_ref, acol_ref, dte_ref, gl[g]) for g in gids}
            stg = {g: st_ref[:, gl[g]] for g in gids}
            dsn = {g: dstate[:, gl[g]] for g in gids}
            stgb, dsnb = {g: _bf(stg[g]) for g in gids}, {g: _bf(dsn[g]) for g in gids}
            btile = {g: _tile4(bg[g]) for g in gids}
            cbt = {g: _dot_nt(cg[g], btile[g]) for g in gids}
            dyg = {g: dy_s[:, gl[g]] for g in gids}
            eag, tailg = {g: q[g]["ea"] for g in gids}, {g: q[g]["tail"] for g in gids}
            xdtg, ealg = {g: q[g]["xdt"] for g in gids}, {g: q[g]["eal"] for g in gids}
            dys_g = {g: _bf(dyg[g] * eag[g]) for g in gids}
            ysg = {g: _dot(cg[g], stgb[g]) for g in gids}
            dc0 = {g: _dot_nt(dys_g[g], stgb[g]) for g in gids}
            dst = {g: _dot_tn(cg[g], dys_g[g]) for g in gids}
            dwt = {g: _dot(bg[g], dsnb[g]) for g in gids}
            db0 = {g: _dot_nt(_bf(xdtg[g] * tailg[g]), dsnb[g]) for g in gids}
            dtl = {g: dwt[g] * xdtg[g] * tailg[g] for g in gids}
            dal_row = {g: jnp.sum(dtl[g], axis=0, keepdims=True)
                       + jnp.sum(dsn[g] * stg[g], axis=0, keepdims=True) * ealg[g] for g in gids}
            for g in gids:
                dstate[:, gl[g]] = dst[g] + dsn[g] * ealg[g]
            dq = {i: _ssd_decay(acol_ref[:, ql], itile, ctile) for i, _, ql, _ in quads}
            mq = {i: cbt[g] * dq[i] for i, g, _, _ in quads}
            xbd = {i: _ssd_blockdiag(xdtg[g][:, qs], bdmask) for i, g, _, qs in quads}
            dyq = {i: _bf(dyg[g][:, qs]) for i, g, _, qs in quads}
            dm = {i: _dot_nt(dyq[i], xbd[i]) for i, _, _, _ in quads}
            dxdt_q = {i: _fold4(jnp.where(bdmask, _dot_tn(_bf(mq[i]), dyq[i]), 0.0)) for i, _, _, _ in quads}
            eq = {i: dm[i] * mq[i] for i, _, _, _ in quads}
            dacol_q = {i: eq[i] - itile * jnp.sum(eq[i], axis=0, keepdims=True) for i, _, _, _ in quads}
            dcbt = {i: _bf(dm[i] * dq[i]) for i, _, _, _ in quads}
            dc_q = {i: _dot(dcbt[i], btile[g]) for i, g, _, _ in quads}
            db_q = {i: _fold4(_dot_tn(dcbt[i], cg[g])) for i, g, _, _ in quads}
            for g in gids:
                dxdt_g = jnp.concatenate([dxdt_q[2 * g], dxdt_q[2 * g + 1]], axis=1) + dwt[g] * tailg[g]
                dxbc_ref[:, gl[g]] = _bf(dyg[g] * dske_ref[:, gl[g]] + dxdt_g * q[g]["dte"])
                ddte_s[:, gl[g]] = dxdt_g * q[g]["xs"]
                dacol_s[:, gl[g]] = (jnp.concatenate([dacol_q[2 * g], dacol_q[2 * g + 1]], axis=1)
                                     + dyg[g] * (ysg[g] * eag[g]) - dtl[g] + last_row * dal_row[g])
                dxbc_ref[:, B_OFF + g * SSD_N:B_OFF + (g + 1) * SSD_N] = _bf(db0[g] + db_q[2 * g] + db_q[2 * g + 1])
                dxbc_ref[:, C_OFF + g * SSD_N:C_OFF + (g + 1) * SSD_N] = _bf(dc0[g] + dc_q[2 * g] + dc_q[2 * g + 1])

        dacum = jnp.zeros((CHUNK, HEAD_PAD), F32)
        ddt = jnp.zeros((CHUNK, HEAD_PAD), F32)
        for i, tl in enumerate(_TILES):
            et_t = et_ref[i * SSD_QW:(i + 1) * SSD_QW, :]
            dacum = dacum + _dot_exact_l(dacol_s[:, tl], et_t, pieces=2)
            ddt = ddt + _dot_exact_l(ddte_s[:, tl], et_t, pieces=2)
        u, _, nexp, a = _ssd_heads(dtr_ref, dtb_ref, alog_ref)
        triu = _bf((_iota((CHUNK, CHUNK), 1) >= _iota((CHUNK, CHUNK), 0)).astype(F32))
        da = _dot_exact_r(triu, dacum)
        ddt = ddt + da * nexp
        dalog_ref[...] += jnp.sum(da * a, axis=0, keepdims=True)
        du = ddt * _sigmoid(u)
        ddtr_ref[...] = _bf(du)
        ddtb_ref[...] += jnp.sum(du, axis=0, keepdims=True)

    sp = _ssd_specs(steps, True)
    return pl.pallas_call(
        body,
        out_shape=(jax.ShapeDtypeStruct((s, SSD_CD), BF16), jax.ShapeDtypeStruct((s, SSD_W), BF16),
                   jax.ShapeDtypeStruct((s, HEAD_PAD), BF16), jax.ShapeDtypeStruct((1, HEAD_PAD), F32),
                   jax.ShapeDtypeStruct((1, HEAD_PAD), F32), jax.ShapeDtypeStruct((1, HEAD_PAD), F32),
                   jax.ShapeDtypeStruct((1, SSD_W), F32), *rider.out_shapes),
        grid=(steps,),
        in_specs=[sp["xbc"], sp["dt"], sp["wide"], sp["wide"], sp["state"], sp["wide"], sp["wide"], sp["wide"],
                  sp["head"], sp["head"], sp["roww"], sp["roww"], sp["et"], *rider.in_specs()],
        out_specs=(sp["xbc"], sp["wide"], sp["dt"], sp["head"], sp["head"], sp["head"], sp["roww"],
                   *rider.out_specs()),
        scratch_shapes=[pltpu.VMEM((SSD_N, SSD_W), F32), pltpu.VMEM((1, SSD_W), F32),
                        *[pltpu.VMEM((CHUNK, SSD_W), F32)] * 3, *rider.scratch()],
        compiler_params=_params("arbitrary"), name=name,
    )(xbc, dt_raw, z, yraw, states, dys, acol, dte, dt_bias, a_log, dske, norm_w, et_sel, *rider.inputs)


W_IN_SHARD = IN_PROJ // N_DEV
W_IN_ROW_PARTS = ((0, 1152), (1152, 832), (1984, 840))


def _device_step(x, pos_col, target, norm1_w, conv_w_shard, conv_b, dt_bias, a_log, d_skip, ssd_norm_w, norm_f_w,
                 w_in_shard, br_ret_shard, br_ssd_shard, out_shard):
    inv_freq = jnp.asarray(ROPE_THETA ** (-np.arange(ROPE_HALF, dtype=np.float64) / ROPE_HALF), F32).reshape(1, ROPE_HALF)
    dtb, alog = _pad_heads(dt_bias), _pad_heads(a_log)
    dske = jnp.repeat(d_skip.reshape(SSD_HEADS).astype(F32), SSD_P).reshape(1, SSD_W)
    my_core = lax.axis_index("c")

    h, w_all = _rmsnorm_fwd(x, norm1_w, "rmsnorm1_fwd", _gather_rider([w_in_shard], 1.0, 1.0))
    w_all = w_all.reshape(IN_PROJ, D_MODEL)
    w_dt = jnp.pad(w_all[OFF_DT:OFF_G], ((0, HEAD_PAD - SSD_HEADS), (0, 0)))
    w_g = w_all[OFF_G:]
    rows_r, rows_z, rows_xbc = (0, OFF_Z), (OFF_Z, SSD_W), (OFF_XBC, SSD_CD)
    gather = _gather_rider([br_ret_shard, br_ssd_shard, out_shard, conv_w_shard], relay_at=0.35, pass_at=0.6)
    p_r, all_ret, all_ssd, all_out, all_conv = _matmul(h, w_all, "nt", F32, "proj_ret", rider=gather, b_rows=rows_r)
    w_br_ret = all_ret.reshape(RET_W, D_MODEL)
    w_br_ssd = all_ssd.reshape(SSD_W, D_MODEL)
    w_out = all_out.reshape(D_MODEL, D_MODEL)
    conv_w = all_conv.transpose(1, 0, 2).reshape(SSD_CONV, SSD_CD)
    p_z = _matmul(h, w_all, "nt", F32, "proj_z", b_rows=rows_z)
    p_xbc = _matmul(h, w_all, "nt", F32, "proj_xbc", b_rows=rows_xbc)
    p_dt = _matmul(h, w_dt, "nt", F32, "proj_dt")
    p_g = _matmul(h, w_g, "nt", BF16, "proj_gates")
    cos, sin = _rope_tables(pos_col, inv_freq, "rope_tables")
    y_r, ret_states = _retention_fwd(p_r, cos, sin, "retention_fwd")
    xbc_act, silu_slope = _conv_fwd(p_xbc, conv_w, conv_b, "conv_fwd")
    y_raw, y_s, ssd_states, acol, dte = _ssd_fwd(xbc_act, p_dt, p_z, dtb, alog, dske, ssd_norm_w, "ssd_fwd")
    pr = _matmul(y_r, w_br_ret, "nn", BF16, "branch_ret")
    ps = _matmul(y_s, w_br_ssd, "nn", BF16, "branch_ssd")
    merged = _merge_fwd(p_g, pr, ps, "merge_fwd")
    o = _matmul(merged, w_out, "nn", F32, "out_proj")
    dx2, g_norm_f, loss_acc = _final_fwd_bwd(x, o, norm_f_w, target, "final_norm_loss")

    g_w_out = _matmul(merged, dx2, "tn", BF16, "grad_w_out")
    dmerged = _matmul(dx2, w_out, "nt", F32, "d_merged")
    dpr, dps, dp_g = _merge_bwd(p_g, pr, ps, dmerged, "merge_bwd")
    g_w_br_ret = _matmul(y_r, dpr, "tn", BF16, "grad_w_br_ret")
    g_w_br_ssd = _matmul(y_s, dps, "tn", BF16, "grad_w_br_ssd")
    dy_r = _matmul(dpr, w_br_ret, "nt", F32, "d_y_ret")
    dy_s = _matmul(dps, w_br_ssd, "nt", F32, "d_y_ssd")
    scatter = _scatter_rider([g_w_out.reshape(N_DEV, -1, D_MODEL), g_w_br_ret.reshape(N_DEV, -1, D_MODEL),
                              g_w_br_ssd.reshape(N_DEV, -1, D_MODEL)])
    dxbc_act, dp_z, dp_dt, g_dtb, g_alog, g_dsk, g_ssd_norm, got_out, got_ret, got_ssd = _ssd_bwd(
        xbc_act, p_dt, p_z, y_raw, ssd_states, dy_s, acol, dte, dtb, alog, dske, ssd_norm_w, "ssd_bwd", scatter)
    dp_xbc, g_conv_w, g_conv_b = _conv_bwd(p_xbc, conv_w, silu_slope, dxbc_act, "conv_bwd")
    dp_r = _retention_bwd(p_r, cos, sin, ret_states, dy_r, "retention_bwd")
    g_w_in = jnp.concatenate([
        _matmul(dp_r, h, "tn", BF16, "grad_w_ret"),
        _matmul(dp_z, h, "tn", BF16, "grad_w_z"),
        _matmul(dp_xbc, h, "tn", BF16, "grad_w_xbc"),
        _matmul(dp_dt, h, "tn", BF16, "grad_w_dt")[:SSD_HEADS],
        _matmul(dp_g, h, "tn", BF16, "grad_w_gates"),
    ], axis=0)
    blocks = g_w_in.reshape(N_DEV, W_IN_SHARD, D_MODEL)
    dh, from_sibling = _matmul(dp_r, w_all, "nn", F32, "d_h_ret", rider=_sibling_rider(blocks), b_rows=rows_r)
    chip_sum = _chip_sum(blocks, from_sibling, my_core, "w_in_chip_sum")
    carriers = (("d_h_xbc", dp_xbc, w_all, rows_xbc), ("d_h_gates", dp_g, w_g, None), ("d_h_z", dp_z, w_all, rows_z))
    landed = []
    for (row0, rows), (nm, dp, w, b_rows) in zip(W_IN_ROW_PARTS, carriers):
        dt_pair = (dp_dt, w_dt) if nm == "d_h_z" else None
        dh, got = _matmul(dp, w, "nn", F32, nm, add=dh, rider=_chip_rider(chip_sum, row0, rows), b_rows=b_rows,
                          extra=dt_pair)
        landed.append(got)
    grad_x, g_norm1 = _rmsnorm_bwd(x, norm1_w, dh, dx2, "rmsnorm1_bwd")
    small = dict(norm1_w=g_norm1, conv_w=g_conv_w, conv_b=g_conv_b, dt_bias=g_dtb[:, :SSD_HEADS],
                 a_log=g_alog[:, :SSD_HEADS], d_skip=g_dsk[:, :SSD_HEADS], ssd_norm_w=g_ssd_norm,
                 norm_f_w=g_norm_f)
    big = dict(w_in=_sum_row_parts(landed, "w_in_sum"), w_br_ret=_sum_slots(got_ret, "w_br_ret_sum"),
               w_br_ssd=_sum_slots(got_ssd, "w_br_ssd_sum"), w_out=_sum_slots(got_out, "w_out_sum"))
    return loss_acc[0, 0], grad_x, small, big


def _all_reduce_small(vec, name):
    r, c = vec.shape

    def body(x_ref, out_ref, land, send_sems, recv_sems):
        x, y, cc = _mesh_pos()
        my_idx = 4 * x + 2 * y + cc
        land[my_idx] = x_ref[...]
        copies = []
        for k in range(1, N_DEV):
            px, py, pc = x ^ (k >> 2), y ^ ((k >> 1) & 1), cc ^ (k & 1)
            cp = pltpu.make_async_remote_copy(
                src_ref=x_ref, dst_ref=land.at[my_idx],
                send_sem=send_sems.at[k - 1], recv_sem=recv_sems.at[k - 1],
                device_id=(px, py, pc), device_id_type=MESH)
            cp.start()
            copies.append(cp)
        for k in range(1, N_DEV):
            px, py, pc = x ^ (k >> 2), y ^ ((k >> 1) & 1), cc ^ (k & 1)
            pltpu.make_async_remote_copy(
                src_ref=x_ref, dst_ref=land.at[4 * px + 2 * py + pc],
                send_sem=send_sems.at[k - 1], recv_sem=recv_sems.at[k - 1],
                device_id=(px, py, pc), device_id_type=MESH).wait_recv()
        for cp in copies:
            cp.wait_send()
        acc = land[0]
        for i in range(1, N_DEV):
            acc = acc + land[i]
        out_ref[...] = acc

    return pl.pallas_call(
        body,
        out_shape=jax.ShapeDtypeStruct((r, c), F32),
        in_specs=[pl.BlockSpec(memory_space=pltpu.VMEM)],
        out_specs=pl.BlockSpec(memory_space=pltpu.VMEM),
        scratch_shapes=[pltpu.VMEM((N_DEV, r, c), F32), pltpu.SemaphoreType.DMA((7,)),
                        pltpu.SemaphoreType.DMA((7,))],
        name=name)(vec)


_SMALL = ("norm1_w", "conv_w", "conv_b", "dt_bias", "a_log", "d_skip", "ssd_norm_w", "norm_f_w")
_SMALL_COLS = 128
_W_IN = "w_in"
_WEIGHTS = ("norm1_w", "w_in", "conv_w", "conv_b", "dt_bias", "a_log", "d_skip", "ssd_norm_w",
            "w_br_ret", "w_br_ssd", "w_out", "norm_f_w")


def _pack(parts):
    flat = jnp.concatenate([p.reshape(-1).astype(F32) for p in parts])
    rows = -(-flat.shape[0] // (8 * _SMALL_COLS)) * 8
    return jnp.pad(flat, (0, rows * _SMALL_COLS - flat.shape[0])).reshape(rows, _SMALL_COLS)


def _unpack(packed, shapes):
    flat = packed.reshape(-1)
    out, off = [], 0
    for shp in shapes:
        n = int(np.prod(shp))
        out.append(flat[off:off + n].reshape(shp))
        off += n
    return out


def kernel(x, positions, norm1_w, w_in, conv_w, conv_b, dt_bias, a_log, d_skip, ssd_norm_w, w_br_ret, w_br_ssd, w_out, norm_f_w, loss_target, m_norm1_w, m_w_in, m_conv_w, m_conv_b, m_dt_bias, m_a_log, m_d_skip, m_ssd_norm_w, m_w_br_ret, m_w_br_ssd, m_w_out, m_norm_f_w, v_norm1_w, v_w_in, v_conv_w, v_conv_b, v_dt_bias, v_a_log, v_d_skip, v_ssd_norm_w, v_w_br_ret, v_w_br_ssd, v_w_out, v_norm_f_w):
    w = dict(norm1_w=norm1_w, w_in=w_in, conv_w=conv_w, conv_b=conv_b, dt_bias=dt_bias, a_log=a_log,
             d_skip=d_skip, ssd_norm_w=ssd_norm_w, w_br_ret=w_br_ret, w_br_ssd=w_br_ssd, w_out=w_out,
             norm_f_w=norm_f_w)
    m = dict(norm1_w=m_norm1_w, w_in=m_w_in, conv_w=m_conv_w, conv_b=m_conv_b, dt_bias=m_dt_bias,
             a_log=m_a_log, d_skip=m_d_skip, ssd_norm_w=m_ssd_norm_w, w_br_ret=m_w_br_ret,
             w_br_ssd=m_w_br_ssd, w_out=m_w_out, norm_f_w=m_norm_f_w)
    v = dict(norm1_w=v_norm1_w, w_in=v_w_in, conv_w=v_conv_w, conv_b=v_conv_b, dt_bias=v_dt_bias,
             a_log=v_a_log, d_skip=v_d_skip, ssd_norm_w=v_ssd_norm_w, w_br_ret=v_w_br_ret,
             w_br_ssd=v_w_br_ssd, w_out=v_w_out, norm_f_w=v_norm_f_w)
    s = x.shape[1]
    my_idx = 4 * lax.axis_index("x") + 2 * lax.axis_index("y") + lax.axis_index("c")

    w[_W_IN], m[_W_IN], v[_W_IN] = w_in[0].T, m_w_in[0].T, v_w_in[0].T

    loss_part, grad_x, g_small, g_big = _device_step(
        x[0], positions.reshape(s, 1), loss_target[0], norm1_w, conv_w[0], conv_b, dt_bias, a_log, d_skip,
        ssd_norm_w, norm_f_w.reshape(1, D_MODEL), _cast_bf16(w[_W_IN], "cast_w_in"),
        _cast_bf16(w_br_ret[0], "cast_w_br_ret"), _cast_bf16(w_br_ssd[0], "cast_w_br_ssd"),
        _cast_bf16(w_out[0], "cast_w_out"))

    loss = lax.psum(loss_part, ("x", "y", "c"))

    small_shapes = [g_small[n].shape for n in _SMALL]
    summed = _unpack(_all_reduce_small(_pack([g_small[n] for n in _SMALL]), "allreduce_small"), small_shapes)
    grads = dict(zip(_SMALL, summed))
    conv_cols = SSD_CD // N_DEV
    grads["conv_w"] = lax.dynamic_slice_in_dim(grads["conv_w"], my_idx * conv_cols, conv_cols, axis=1)
    grads["norm_f_w"] = grads["norm_f_w"].reshape(D_MODEL)
    for n in ("norm1_w", "conv_w", "conv_b", "dt_bias", "a_log", "d_skip", "ssd_norm_w"):
        grads[n] = grads[n].reshape(w[n].shape)

    delta, new_m, new_v = {}, {}, {}
    for n in ("w_br_ret", "w_br_ssd", "w_out"):
        w[n], m[n], v[n] = w[n][0], m[n][0], v[n][0]
    for n in (_W_IN, "w_br_ret", "w_br_ssd", "w_out"):
        back = (lambda a: a.T[None]) if n == _W_IN else (lambda a: a[None])
        res = _adamw(w[n], g_big[n], m[n], v[n], "adamw_" + n)
        grads[n] = back(g_big[n])
        delta[n], new_m[n], new_v[n] = (back(a) for a in res)
    shapes = [w[n].shape for n in _SMALL]
    packed = _adamw(_pack([w[n] for n in _SMALL]), _pack([grads[n] for n in _SMALL]),
                    _pack([m[n] for n in _SMALL]), _pack([v[n] for n in _SMALL]), "adamw_small")
    for res, dst in zip(packed, (delta, new_m, new_v)):
        for n, a in zip(_SMALL, _unpack(res, shapes)):
            dst[n] = a

    return (loss, grad_x.reshape(x.shape), *[grads[n] for n in _WEIGHTS], *[delta[n] for n in _WEIGHTS],
            *[new_m[n] for n in _WEIGHTS], *[new_v[n] for n in _WEIGHTS])
```

```python
import functools

import numpy as np
import jax
import jax.numpy as jnp
from jax import lax
from jax.experimental import pallas as pl
from jax.experimental.pallas import tpu as pltpu

F32 = jnp.float32
BF16 = jnp.bfloat16

D_MODEL = 2048
CHUNK = 64
CHUNKS_PER_STEP = 2
STEP_ROWS = CHUNK * CHUNKS_PER_STEP
RET_CHUNKS_PER_STEP = 4
RET_STEP_ROWS = CHUNK * RET_CHUNKS_PER_STEP
EPS = 1e-6
N_DEV = 8

RET_HEADS = 8
RET_DK = 256
RET_W = RET_HEADS * RET_DK
ROPE_THETA = 10000.0
ROPE_HALF = RET_DK // 2

SSD_W = 4096
SSD_P = 64
SSD_HEADS = 64
SSD_GROUPS = 8
SSD_N = 128
SSD_GW = SSD_W // SSD_GROUPS
SSD_QW = 256
SSD_BLOCK_GROUPS = 2
SSD_CONV = 4
SSD_CD = SSD_W + 2 * SSD_GROUPS * SSD_N
HEAD_PAD = 128
B_OFF = SSD_W
C_OFF = SSD_W + SSD_GROUPS * SSD_N

ADAM_LR = 0.001
ADAM_B1 = 0.9
ADAM_B2 = 0.999
ADAM_EPS = 1e-08
ADAM_WD = 0.01
ADAM_STEP = 10

SPLITS = (RET_W, RET_W, RET_W, RET_W, SSD_W, SSD_CD, SSD_HEADS, D_MODEL, D_MODEL)
IN_PROJ = sum(SPLITS)
OFF_Z = 4 * RET_W
OFF_XBC = OFF_Z + SSD_W
OFF_DT = OFF_XBC + SSD_CD
OFF_G = OFF_DT + SSD_HEADS

ROW_TILE = 256
CONV_TILE = 128
MM_TILE = 1024
MM_TILE_WIDE = 2048
MM_TK = 2048
VMEM_CAP = 60 << 20

MESH = pl.DeviceIdType.MESH


def _dot(a, b):
    return lax.dot_general(a, b, (((1,), (0,)), ((), ())), preferred_element_type=F32)


def _dot_nt(a, b):
    return lax.dot_general(a, b, (((1,), (1,)), ((), ())), preferred_element_type=F32)


def _dot_tn(a, b):
    return lax.dot_general(a, b, (((0,), (0,)), ((), ())), preferred_element_type=F32)


def _bf(x):
    return x.astype(BF16)


def _split3(x):
    hi = x.astype(BF16)
    r = x - hi.astype(F32)
    mid = r.astype(BF16)
    lo = (r - mid.astype(F32)).astype(BF16)
    return hi, mid, lo


def _dot_exact_l(x, sel, pieces=3):
    hi, mid, lo = _split3(x)
    r = _dot(hi, sel) + _dot(mid, sel)
    return r + _dot(lo, sel) if pieces == 3 else r


def _dot_exact_r(sel, x):
    hi, mid, lo = _split3(x)
    return _dot(sel, hi) + _dot(sel, mid) + _dot(sel, lo)


def _sigmoid(x):
    return 1.0 / (1.0 + jnp.exp(-x))


def _softplus(x):
    return jnp.maximum(x, 0.0) + jnp.log(1.0 + jnp.exp(-jnp.abs(x)))


def _iota(shape, axis):
    return lax.broadcasted_iota(jnp.int32, shape, axis)


def _params(*sem):
    return pltpu.CompilerParams(dimension_semantics=sem)


class _Rider:
    def __init__(self, inputs, out_shapes, n_sems, phases):
        self.inputs, self.out_shapes, self.n_sems, self.phases = tuple(inputs), tuple(out_shapes), n_sems, phases

    def in_specs(self):
        return [pl.BlockSpec(memory_space=pl.ANY)] * len(self.inputs)

    def out_specs(self):
        return [pl.BlockSpec(memory_space=pl.ANY)] * len(self.out_shapes)

    def scratch(self):
        return [pltpu.SemaphoreType.DMA((self.n_sems,)), pltpu.SemaphoreType.DMA((self.n_sems,))]

    def run(self, step, n_steps, ins, outs, send_sems, recv_sems, last):
        for frac, fn in self.phases:
            if (frac >= 1.0) != last:
                continue
            at = min(int(frac * n_steps), n_steps - 1)

            @pl.when(step == at)
            def _(fn=fn):
                fn(ins, outs, send_sems, recv_sems)


def _mesh_pos():
    return lax.axis_index("x"), lax.axis_index("y"), lax.axis_index("c")


def _gather_rider(shards, relay_at, pass_at):
    n = len(shards)

    def tools(a, ins, outs, send_sems, recv_sems):
        x, y, cc = _mesh_pos()
        nbrs = [(1 - x, y), (x, 1 - y)]
        diag = (1 - x, 1 - y)
        relay_from, relay_to = (x ^ cc, y ^ (1 - cc)), (x ^ (1 - cc), y ^ cc)

        def slot(px, py, pc):
            return outs[a].at[4 * px + 2 * py + pc]

        def copy(k, block, to, src=None):
            return pltpu.make_async_remote_copy(
                src_ref=slot(*block) if src is None else src, dst_ref=slot(*block),
                send_sem=send_sems.at[8 * a + k], recv_sem=recv_sems.at[8 * a + k], device_id=to, device_id_type=MESH)

        me, sibling = (x, y, cc), (x, y, 1 - cc)
        return dict(
            mine=lambda: pltpu.make_async_copy(ins[a], slot(*me), send_sems.at[8 * a + 7]),
            first=lambda: [copy(0, me, sibling, src=ins[a])] + [copy(1 + j, me, (*chip, cc), src=ins[a])
                                                                for j, chip in enumerate(nbrs)],
            landed=lambda: [copy(1 + j, (*chip, cc), me) for j, chip in enumerate(nbrs)],
            relay=lambda: copy(3, (*relay_from, cc), (*relay_to, cc)),
            relayed_in=lambda: copy(3, (*diag, cc), me),
            passed=lambda: [copy(4 + j, (*chip, cc), sibling) for j, chip in enumerate(nbrs)],
            passed_diag=lambda: copy(6, (*diag, cc), sibling),
            from_sibling=lambda: [copy(0, sibling, me)] + [copy(4 + j, (*chip, 1 - cc), me)
                                                           for j, chip in enumerate(nbrs + [diag])])

    def start(*refs):
        for a in range(n):
            t = tools(a, *refs)
            t["mine"]().start()
            for cp in t["first"]():
                cp.start()

    def relay(*refs):
        for a in range(n):
            t = tools(a, *refs)
            for got in t["landed"]():
                got.wait_recv()
            t["relay"]().start()
            for cp in t["passed"]():
                cp.start()

    def pass_diag(*refs):
        for a in range(n):
            t = tools(a, *refs)
            t["relayed_in"]().wait_recv()
            t["passed_diag"]().start()

    def finish(*refs):
        for a in range(n):
            t = tools(a, *refs)
            for cp in t["from_sibling"]():
                cp.wait_recv()
            for cp in t["first"]() + [t["relay"]()] + t["passed"]() + [t["passed_diag"]()]:
                cp.wait_send()
            t["mine"]().wait()

    outs = [jax.ShapeDtypeStruct((N_DEV,) + s.shape, s.dtype) for s in shards]
    return _Rider(shards, outs, 8 * n, ((0.0, start), (relay_at, relay), (pass_at, pass_diag), (1.0, finish)))


def _scatter_rider(blocks):
    n = len(blocks)

    def copies(a, ins, outs, send_sems, recv_sems, with_back):
        x, y, cc = _mesh_pos()
        my_idx = 4 * x + 2 * y + cc
        mine = pltpu.make_async_copy(ins[a].at[my_idx], outs[a].at[my_idx], send_sems.at[8 * a + 7])
        out, back = [], []
        for k in range(1, N_DEV):
            px, py, pc = x ^ (k >> 2), y ^ ((k >> 1) & 1), cc ^ (k & 1)
            peer_idx = 4 * px + 2 * py + pc
            sems = dict(send_sem=send_sems.at[8 * a + k - 1], recv_sem=recv_sems.at[8 * a + k - 1],
                        device_id=(px, py, pc), device_id_type=MESH)
            out.append(pltpu.make_async_remote_copy(src_ref=ins[a].at[peer_idx], dst_ref=outs[a].at[my_idx], **sems))
            if with_back:
                back.append(pltpu.make_async_remote_copy(src_ref=ins[a].at[my_idx], dst_ref=outs[a].at[peer_idx], **sems))
        return mine, out, back

    def start(*refs):
        for a in range(n):
            mine, out, _ = copies(a, *refs, False)
            mine.start()
            for cp in out:
                cp.start()

    def finish(*refs):
        for a in range(n):
            mine, out, back = copies(a, *refs, True)
            for cp in back:
                cp.wait_recv()
            for cp in out:
                cp.wait_send()
            mine.wait()

    outs = [jax.ShapeDtypeStruct(b.shape, b.dtype) for b in blocks]
    return _Rider(blocks, outs, 8 * n, ((0.0, start), (1.0, finish)))


def _sibling_rider(blocks):
    _, r, c = blocks.shape

    def copies(ins, outs, send_sems, recv_sems):
        x, y, cc = _mesh_pos()
        return [pltpu.make_async_remote_copy(
            src_ref=ins[0].at[2 * j + 1 - cc], dst_ref=outs[0].at[j], send_sem=send_sems.at[j],
            recv_sem=recv_sems.at[j], device_id=(x, y, 1 - cc), device_id_type=MESH) for j in range(4)]

    def start(*refs):
        for cp in copies(*refs):
            cp.start()

    def finish(*refs):
        for cp in copies(*refs):
            cp.wait_recv()
        for cp in copies(*refs):
            cp.wait_send()

    return _Rider([blocks], [jax.ShapeDtypeStruct((4, r, c), blocks.dtype)], 4, ((0.0, start), (1.0, finish)))


def _chip_rider(partial, row0, rows):
    _, _, c = partial.shape

    def copies(ins, outs, send_sems, recv_sems, with_back):
        x, y, cc = _mesh_pos()
        my_chip = 2 * x + y

        def rows_of(j):
            return ins[0].at[j, pl.ds(row0, rows)]

        mine = pltpu.make_async_copy(rows_of(my_chip), outs[0].at[my_chip], send_sems.at[3])
        out, back = [], []
        for k in range(1, 4):
            px, py = x ^ (k >> 1), y ^ (k & 1)
            peer = 2 * px + py
            sems = dict(send_sem=send_sems.at[k - 1], recv_sem=recv_sems.at[k - 1], device_id=(px, py, cc),
                        device_id_type=MESH)
            out.append(pltpu.make_async_remote_copy(src_ref=rows_of(peer), dst_ref=outs[0].at[my_chip], **sems))
            if with_back:
                back.append(pltpu.make_async_remote_copy(src_ref=rows_of(my_chip), dst_ref=outs[0].at[peer], **sems))
        return mine, out, back

    def start(*refs):
        mine, out, _ = copies(*refs, False)
        mine.start()
        for cp in out:
            cp.start()

    def finish(*refs):
        mine, out, back = copies(*refs, True)
        for cp in back:
            cp.wait_recv()
        for cp in out:
            cp.wait_send()
        mine.wait()

    return _Rider([partial], [jax.ShapeDtypeStruct((4, rows, c), partial.dtype)], 4, ((0.0, start), (1.0, finish)))


def _matmul(a, b, mode, out_dtype, name, add=None, rider=None, b_rows=None, extra=None):
    b_shape = b.shape if b_rows is None else (b_rows[1], b.shape[1])
    if mode == "nn":
        (m, k), (k2, n) = a.shape, b_shape
    elif mode == "nt":
        (m, k), (n, k2) = a.shape, b_shape
    else:
        (k, m), (k2, n) = a.shape, b_shape
    assert k == k2, (a.shape, b_shape, mode)
    tm, tn, tk = min(m, MM_TILE), min(n, MM_TILE), min(k, MM_TK)
    if n % MM_TILE_WIDE == 0 and (k <= MM_TK or (add is None and jnp.dtype(out_dtype).itemsize == 2)):
        tn = MM_TILE_WIDE
    assert m % tm == 0 and n % tn == 0 and k % tk == 0, (m, n, k)
    nk = k // tk
    b_tile_rows = tn if mode == "nt" else tk
    b_off = 0 if b_rows is None else b_rows[0] // b_tile_rows
    assert b_rows is None or b_rows[0] % b_tile_rows == 0, (b_rows, b_tile_rows)
    dot = {"nn": _dot, "nt": _dot_nt, "tn": _dot_tn}[mode]

    gm, gn = m // tm, n // tn
    n_rin = len(rider.inputs) if rider else 0
    n_rout = len(rider.out_shapes) if rider else 0
    n_add = 1 if add is not None else 0

    n_extra = 2 if extra is not None else 0

    def body(*refs):
        a_ref, b_ref = refs[:2]
        add_ref = refs[2] if n_add else None
        extra_refs = refs[2 + n_add:2 + n_add + n_extra]
        first = 2 + n_add + n_extra
        r_ins = refs[first:first + n_rin]
        o_ref = refs[first + n_rin]
        r_outs = refs[first + n_rin + 1:first + n_rin + 1 + n_rout]
        scratch = refs[first + n_rin + 1 + n_rout:]
        acc_ref = scratch[0] if nk > 1 else None
        kk = pl.program_id(2)
        if rider:
            step = (pl.program_id(0) * gn + pl.program_id(1)) * nk + kk
            ride = functools.partial(rider.run, step, gm * gn * nk, r_ins, r_outs, scratch[-2], scratch[-1])
            ride(last=False)

        def finish(r):
            if extra_refs:
                r = r + _dot(_bf(extra_refs[0][...]), _bf(extra_refs[1][...]))
            if add_ref is not None:
                r = r + add_ref[...]
            o_ref[...] = r.astype(o_ref.dtype)

        if nk == 1:
            finish(dot(_bf(a_ref[...]), _bf(b_ref[...])))
        else:
            @pl.when(kk == 0)
            def _():
                acc_ref[...] = dot(_bf(a_ref[...]), _bf(b_ref[...]))

            @pl.when(jnp.logical_and(kk > 0, kk < nk - 1))
            def _():
                acc_ref[...] += dot(_bf(a_ref[...]), _bf(b_ref[...]))

            @pl.when(kk == nk - 1)
            def _():
                finish(acc_ref[...] + dot(_bf(a_ref[...]), _bf(b_ref[...])))
        if rider:
            ride(last=True)

    if mode == "nn":
        a_spec = pl.BlockSpec((tm, tk), lambda i, j, kk: (i, kk))
        b_spec = pl.BlockSpec((tk, tn), lambda i, j, kk: (kk + b_off, j))
    elif mode == "nt":
        a_spec = pl.BlockSpec((tm, tk), lambda i, j, kk: (i, kk))
        b_spec = pl.BlockSpec((tn, tk), lambda i, j, kk: (j + b_off, kk))
    else:
        a_spec = pl.BlockSpec((tk, tm), lambda i, j, kk: (kk, i))
        b_spec = pl.BlockSpec((tk, tn), lambda i, j, kk: (kk + b_off, j))
    o_spec = pl.BlockSpec((tm, tn), lambda i, j, kk: (i, j))
    in_specs = [a_spec, b_spec] + ([o_spec] if add is not None else [])
    args = (a, b) + ((add,) if add is not None else ())
    if extra is not None:
        (m2, k_extra), (k_extra2, n2) = extra[0].shape, extra[1].shape
        assert (m2, n2, k_extra) == (m, n, k_extra2), (extra[0].shape, extra[1].shape)
        in_specs += [pl.BlockSpec((tm, k_extra), lambda i, j, kk: (i, 0)),
                     pl.BlockSpec((k_extra, tn), lambda i, j, kk: (0, j))]
        args += tuple(extra)
    block_bytes = (tm * tk * a.dtype.itemsize + tk * tn * b.dtype.itemsize
                   + tm * tn * (jnp.dtype(out_dtype).itemsize + (4 if add is not None else 0)))
    vmem = min(2 * block_bytes + 2 * tm * tn * 4 + 2 * (tm + tn) * tk + (4 << 20), VMEM_CAP)
    out_shape = jax.ShapeDtypeStruct((m, n), out_dtype)
    scratch = [pltpu.VMEM((tm, tn), F32)] if nk > 1 else []
    if rider:
        in_specs = in_specs + rider.in_specs()
        args = args + rider.inputs
        out_shape = (out_shape, *rider.out_shapes)
        o_spec = (o_spec, *rider.out_specs())
        scratch = scratch + rider.scratch()
    sem = ("arbitrary",) * 3 if rider else ("parallel", "parallel", "arbitrary")
    return pl.pallas_call(
        body,
        out_shape=out_shape,
        grid=(gm, gn, nk),
        in_specs=in_specs,
        out_specs=o_spec,
        scratch_shapes=scratch,
        compiler_params=pltpu.CompilerParams(dimension_semantics=sem, vmem_limit_bytes=int(vmem)),
        name=name,
    )(*args)


def _row_spec(width, tile=ROW_TILE):
    return pl.BlockSpec((tile, width), lambda i: (i, 0))


def _full_spec(shape):
    nd = len(shape)
    return pl.BlockSpec(shape, lambda *_: (0,) * nd)


def _tiling_2d(r, c):
    if r <= ROW_TILE or r % ROW_TILE == 0:
        tr = min(r, ROW_TILE)
        return (tr, c), r // tr, (lambda i: (i, 0))
    tc = 128 if r > 4096 else 256
    assert c % tc == 0, (r, c)
    return (r, tc), c // tc, (lambda i: (0, i))


def _spec_2d(r, c):
    blk, grid, idx = _tiling_2d(r, c)
    return pl.BlockSpec(blk, idx), grid


def _cast_bf16(x, name):
    r, c = x.shape
    sp, grid = _spec_2d(r, c)

    def body(x_ref, o_ref):
        o_ref[...] = _bf(x_ref[...])

    return pl.pallas_call(
        body, out_shape=jax.ShapeDtypeStruct((r, c), BF16), grid=(grid,), in_specs=[sp], out_specs=sp,
        compiler_params=_params("parallel"), name=name)(x)


def _rmsnorm_rope_fwd(x, w, pos_col, inv_freq, name, rider):
    s, d = x.shape
    steps = s // ROW_TILE
    n_rin, n_rout = len(rider.inputs), len(rider.out_shapes)

    def body(x_ref, w_ref, p_ref, f_ref, *refs):
        r_ins, (o_ref, cos_ref, sin_ref) = refs[:n_rin], refs[n_rin:n_rin + 3]
        r_outs = refs[n_rin + 3:n_rin + 3 + n_rout]
        ride = functools.partial(rider.run, pl.program_id(0), steps, r_ins, r_outs, refs[-2], refs[-1])
        ride(last=False)
        xv = x_ref[...]
        rstd = lax.rsqrt(jnp.mean(xv * xv, axis=1, keepdims=True) + EPS)
        o_ref[...] = _bf(xv * rstd * w_ref[...])
        ang = p_ref[...].astype(F32) * f_ref[...]
        cos_ref[...] = jnp.cos(ang)
        sin_ref[...] = jnp.sin(ang)
        ride(last=True)

    table = jax.ShapeDtypeStruct((s, ROPE_HALF), F32)
    return pl.pallas_call(
        body, out_shape=(jax.ShapeDtypeStruct((s, d), BF16), table, table, *rider.out_shapes), grid=(steps,),
        in_specs=[_row_spec(d), _full_spec((1, d)), _row_spec(1), _full_spec((1, ROPE_HALF)), *rider.in_specs()],
        out_specs=(_row_spec(d), _row_spec(ROPE_HALF), _row_spec(ROPE_HALF), *rider.out_specs()),
        scratch_shapes=rider.scratch(),
        compiler_params=_params("arbitrary"), name=name)(x, w, pos_col, inv_freq, *rider.inputs)


def _rmsnorm_bwd(x, w, dh, dres, name):
    s, d = x.shape

    def body(x_ref, w_ref, dh_ref, dres_ref, dx_ref, dw_ref):
        @pl.when(pl.program_id(0) == 0)
        def _():
            dw_ref[...] = jnp.zeros_like(dw_ref)

        xv = x_ref[...]
        rstd = lax.rsqrt(jnp.mean(xv * xv, axis=1, keepdims=True) + EPS)
        xhat = xv * rstd
        dhv = dh_ref[...]
        dxhat = dhv * w_ref[...]
        dx = rstd * (dxhat - xhat * jnp.mean(dxhat * xhat, axis=1, keepdims=True))
        dx_ref[...] = dx + dres_ref[...]
        dw_ref[...] += jnp.sum(dhv * xhat, axis=0, keepdims=True)

    return pl.pallas_call(
        body,
        out_shape=(jax.ShapeDtypeStruct((s, d), F32), jax.ShapeDtypeStruct((1, d), F32)),
        grid=(s // ROW_TILE,),
        in_specs=[_row_spec(d), _full_spec((1, d)), _row_spec(d), _row_spec(d)],
        out_specs=(_row_spec(d), _full_spec((1, d))),
        compiler_params=_params("arbitrary"), name=name)(x, w, dh, dres)


def _merge_fwd(pg, p_r, p_s, name):
    s = pg.shape[0]

    def body(g_ref, r_ref, s_ref, o_ref):
        g = g_ref[...].astype(F32)
        o_ref[...] = _bf(_sigmoid(g[:, :D_MODEL]) * r_ref[...].astype(F32)
                         + _sigmoid(g[:, D_MODEL:]) * s_ref[...].astype(F32))

    return pl.pallas_call(
        body, out_shape=jax.ShapeDtypeStruct((s, D_MODEL), BF16), grid=(s // ROW_TILE,),
        in_specs=[_row_spec(2 * D_MODEL), _row_spec(D_MODEL), _row_spec(D_MODEL)],
        out_specs=_row_spec(D_MODEL), compiler_params=_params("parallel"), name=name)(pg, p_r, p_s)


def _merge_bwd(pg, p_r, p_s, dm, name):
    s = pg.shape[0]

    def body(g_ref, r_ref, s_ref, dm_ref, dr_ref, ds_ref, dg_ref):
        g = g_ref[...].astype(F32)
        sr, ss = _sigmoid(g[:, :D_MODEL]), _sigmoid(g[:, D_MODEL:])
        d = dm_ref[...]
        dr_ref[...] = _bf(d * sr)
        ds_ref[...] = _bf(d * ss)
        dg_ref[:, :D_MODEL] = _bf(d * r_ref[...].astype(F32) * sr * (1.0 - sr))
        dg_ref[:, D_MODEL:] = _bf(d * s_ref[...].astype(F32) * ss * (1.0 - ss))

    o = jax.ShapeDtypeStruct((s, D_MODEL), BF16)
    return pl.pallas_call(
        body, out_shape=(o, o, jax.ShapeDtypeStruct((s, 2 * D_MODEL), BF16)), grid=(s // ROW_TILE,),
        in_specs=[_row_spec(2 * D_MODEL), _row_spec(D_MODEL), _row_spec(D_MODEL), _row_spec(D_MODEL)],
        out_specs=(_row_spec(D_MODEL), _row_spec(D_MODEL), _row_spec(2 * D_MODEL)),
        compiler_params=_params("parallel"), name=name)(pg, p_r, p_s, dm)


def _final_fwd_bwd(x, o, w, target, name):
    s, d = x.shape

    def body(x_ref, o_ref, w_ref, t_ref, dx_ref, dw_ref, loss_ref):
        @pl.when(pl.program_id(0) == 0)
        def _():
            dw_ref[...] = jnp.zeros_like(dw_ref)
            loss_ref[...] = jnp.zeros_like(loss_ref)

        x2 = x_ref[...] + o_ref[...]
        rstd = lax.rsqrt(jnp.mean(x2 * x2, axis=1, keepdims=True) + EPS)
        xhat = x2 * rstd
        wv = w_ref[...]
        err = xhat * wv - t_ref[...]
        loss_ref[...] += jnp.sum(jnp.sum(err * err, axis=1, keepdims=True), axis=0, keepdims=True) * (0.5 / d)
        dy = err * (1.0 / d)
        dw_ref[...] += jnp.sum(dy * xhat, axis=0, keepdims=True)
        dxhat = dy * wv
        dx_ref[...] = rstd * (dxhat - xhat * jnp.mean(dxhat * xhat, axis=1, keepdims=True))

    return pl.pallas_call(
        body,
        out_shape=(jax.ShapeDtypeStruct((s, d), F32), jax.ShapeDtypeStruct((1, d), F32),
                   jax.ShapeDtypeStruct((8, 128), F32)),
        grid=(s // ROW_TILE,),
        in_specs=[_row_spec(d), _row_spec(d), _full_spec((1, d)), _row_spec(d)],
        out_specs=(_row_spec(d), _full_spec((1, d)), _full_spec((8, 128))),
        compiler_params=_params("arbitrary"), name=name)(x, o, w, target)


def _adamw(w, g, m, v, name):
    r, c = w.shape
    sp, grid = _spec_2d(r, c)
    c1 = 1.0 / (1.0 - ADAM_B1 ** ADAM_STEP)
    c2 = 1.0 / (1.0 - ADAM_B2 ** ADAM_STEP)

    def body(w_ref, g_ref, m_ref, v_ref, d_ref, nm_ref, nv_ref):
        gv = g_ref[...]
        nm = ADAM_B1 * m_ref[...] + (1.0 - ADAM_B1) * gv
        nv = ADAM_B2 * v_ref[...] + (1.0 - ADAM_B2) * (gv * gv)
        d_ref[...] = -ADAM_LR * ((nm * c1) / (jnp.sqrt(nv * c2) + ADAM_EPS) + ADAM_WD * w_ref[...])
        nm_ref[...] = nm
        nv_ref[...] = nv

    o = jax.ShapeDtypeStruct((r, c), F32)
    return pl.pallas_call(
        body, out_shape=(o, o, o), grid=(grid,), in_specs=[sp, sp, sp, sp], out_specs=(sp, sp, sp),
        compiler_params=_params("parallel"), name=name)(w, g, m, v)


def _sum_slots(land, name):
    n, r, c = land.shape
    blk, grid, idx = _tiling_2d(r, c)

    def body(l_ref, o_ref):
        acc = l_ref[0].astype(F32)
        for i in range(1, n):
            acc = acc + l_ref[i].astype(F32)
        o_ref[...] = acc

    return pl.pallas_call(
        body, out_shape=jax.ShapeDtypeStruct((r, c), F32), grid=(grid,),
        in_specs=[pl.BlockSpec((n, *blk), lambda i: (0, *idx(i)))], out_specs=pl.BlockSpec(blk, idx),
        compiler_params=_params("parallel"), name=name)(land)


SLAB_COLS = 256


def _chip_sum(blocks, from_sibling, my_core, name):
    _, r, c = blocks.shape

    def body(core_ref, a_ref, b_ref, o_ref):
        o_ref[...] = _bf(a_ref[...].astype(F32) + b_ref[...].astype(F32))

    def slab(chip_of):
        return pl.BlockSpec((1, r, SLAB_COLS), lambda j, i, core: (chip_of(j, core), 0, i))

    grid_spec = pltpu.PrefetchScalarGridSpec(
        num_scalar_prefetch=1, grid=(4, c // SLAB_COLS),
        in_specs=[slab(lambda j, core: 2 * j + core[0]), slab(lambda j, core: j)],
        out_specs=slab(lambda j, core: j))
    return pl.pallas_call(
        body, out_shape=jax.ShapeDtypeStruct((4, r, c), BF16), grid_spec=grid_spec,
        compiler_params=_params("parallel", "parallel"), name=name,
    )(my_core.astype(jnp.int32).reshape(1), blocks, from_sibling)


def _sum_row_parts(parts, name):
    c = parts[0].shape[2]
    rows = [p.shape[1] for p in parts]

    def body(*refs):
        o_ref, off = refs[-1], 0
        for p_ref, n in zip(refs[:-1], rows):
            acc = p_ref[0].astype(F32)
            for i in range(1, p_ref.shape[0]):
                acc = acc + p_ref[i].astype(F32)
            o_ref[off:off + n, :] = acc
            off += n

    return pl.pallas_call(
        body, out_shape=jax.ShapeDtypeStruct((sum(rows), c), F32), grid=(c // SLAB_COLS,),
        in_specs=[pl.BlockSpec((p.shape[0], p.shape[1], SLAB_COLS), lambda i: (0, 0, i)) for p in parts],
        out_specs=pl.BlockSpec((sum(rows), SLAB_COLS), lambda i: (0, i)),
        compiler_params=_params("parallel"), name=name)(*parts)


def _retention_tables():
    lg = np.log1p(-(2.0 ** (-5.0 - np.arange(RET_HEADS, dtype=np.float64))))
    idx = np.arange(CHUNK, dtype=np.float64)
    intra = np.exp(np.abs(idx[:, None] - idx[None, :])[None] * lg[:, None, None])
    qd = np.exp((idx[None, :] + 1.0) * lg[:, None])
    kd = np.exp((CHUNK - 1.0 - idx[None, :]) * lg[:, None])
    cd = np.exp(CHUNK * lg)
    ones = np.ones((1, 1, RET_DK))
    return (jnp.asarray(intra, F32), jnp.asarray(qd[:, :, None] * ones, F32),
            jnp.asarray(kd[:, :, None] * ones, F32), jnp.asarray(cd[:, None, None] * ones, F32))


def _rope(t, cos, sin):
    t1, t2 = t[:, :ROPE_HALF], t[:, ROPE_HALF:]
    return jnp.concatenate([t1 * cos - t2 * sin, t2 * cos + t1 * sin], axis=1)


def _rope_t(d, cos, sin):
    d1, d2 = d[:, :ROPE_HALF], d[:, ROPE_HALF:]
    return jnp.concatenate([d1 * cos + d2 * sin, d2 * cos - d1 * sin], axis=1)


_HEADS = range(RET_HEADS)


def _ret_chunk_fwd(p_ref, cos, sin, intra_ref, qd_ref, st):
    def seg(i, h):
        return p_ref[:, i * RET_W + h * RET_DK:i * RET_W + (h + 1) * RET_DK]

    v = [seg(2, h) for h in _HEADS]
    qr = [_rope(seg(0, h), cos, sin) for h in _HEADS]
    kr = [_rope(seg(1, h), cos, sin) * (RET_DK ** -0.5) for h in _HEADS]
    qrb, vb = [_bf(a) for a in qr], [_bf(a) for a in v]
    sc = [_dot_nt(qrb[h], _bf(kr[h])) * intra_ref[h] for h in _HEADS]
    qs = [_dot(qrb[h], _bf(st[h])) for h in _HEADS]
    y = [_dot(_bf(sc[h]), vb[h]) + qs[h] * qd_ref[h] for h in _HEADS]
    g = [seg(3, h) for h in _HEADS]
    return dict(v=v, vb=vb, qr=qr, qrb=qrb, kr=kr, sc=sc, y=y, g=g)


def _group_norm(y):
    mu = jnp.mean(y, axis=1, keepdims=True)
    yc = y - mu
    rstd = lax.rsqrt(jnp.mean(yc * yc, axis=1, keepdims=True) + EPS)
    return yc * rstd, rstd


def _chunk_rows(sub):
    return pl.ds(sub * CHUNK, CHUNK)


def _ret_specs(steps, rev):
    cidx = (lambda c: steps - 1 - c) if rev else (lambda c: c)
    return dict(
        proj=pl.BlockSpec((RET_STEP_ROWS, 4 * RET_W), lambda c: (cidx(c), 0)),
        half=pl.BlockSpec((RET_STEP_ROWS, ROPE_HALF), lambda c: (cidx(c), 0)),
        wide=pl.BlockSpec((RET_STEP_ROWS, RET_W), lambda c: (cidx(c), 0)),
        state=pl.BlockSpec((RET_CHUNKS_PER_STEP, RET_HEADS, RET_DK, RET_DK), lambda c: (cidx(c), 0, 0, 0)),
        intra=_full_spec((RET_HEADS, CHUNK, CHUNK)),
        dec=_full_spec((RET_HEADS, CHUNK, RET_DK)),
        cd=_full_spec((RET_HEADS, 1, RET_DK)),
    )


def _retention_fwd(p_r, cos, sin, name):
    s = p_r.shape[0]
    nc = s // CHUNK
    intra_t, qd_t, kd_t, cd_t = _retention_tables()

    def body(p_ref, cos_ref, sin_ref, intra_ref, qd_ref, kd_ref, cd_ref, y_ref, st_ref, state):
        @pl.when(pl.program_id(0) == 0)
        def _():
            state[...] = jnp.zeros_like(state)

        for sub in range(RET_CHUNKS_PER_STEP):
            rows = _chunk_rows(sub)
            y_v, st_v = y_ref.at[rows], st_ref.at[sub]
            st = [state[h] for h in _HEADS]
            f = _ret_chunk_fwd(p_ref.at[rows], cos_ref[rows, :], sin_ref[rows, :], intra_ref, qd_ref, st)
            new_st = [st[h] * cd_ref[h] + _dot_tn(_bf(f["kr"][h] * kd_ref[h]), f["vb"][h]) for h in _HEADS]
            out = [_bf(_group_norm(f["y"][h])[0] * (f["g"][h] * _sigmoid(f["g"][h]))) for h in _HEADS]
            for h in _HEADS:
                st_v[h] = _bf(st[h])
                state[h] = new_st[h]
                y_v[:, h * RET_DK:(h + 1) * RET_DK] = out[h]

    steps = nc // RET_CHUNKS_PER_STEP
    sp = _ret_specs(steps, False)
    return pl.pallas_call(
        body,
        out_shape=(jax.ShapeDtypeStruct((s, RET_W), BF16),
                   jax.ShapeDtypeStruct((nc, RET_HEADS, RET_DK, RET_DK), BF16)),
        grid=(steps,),
        in_specs=[sp["proj"], sp["half"], sp["half"], sp["intra"], sp["dec"], sp["dec"], sp["cd"]],
        out_specs=(sp["wide"], sp["state"]),
        scratch_shapes=[pltpu.VMEM((RET_HEADS, RET_DK, RET_DK), F32)],
        compiler_params=_params("arbitrary"), name=name)(p_r, cos, sin, intra_t, qd_t, kd_t, cd_t)


def _retention_bwd(p_r, cos, sin, states, dy_r, name):
    s = p_r.shape[0]
    nc = s // CHUNK
    intra_t, qd_t, kd_t, cd_t = _retention_tables()

    def body(p_ref, cos_ref, sin_ref, intra_ref, qd_ref, kd_ref, cd_ref, st_ref, dy_ref, dp_ref, dstate):
        @pl.when(pl.program_id(0) == 0)
        def _():
            dstate[...] = jnp.zeros_like(dstate)

        for sub in reversed(range(RET_CHUNKS_PER_STEP)):
            rows = _chunk_rows(sub)
            chunk(p_ref.at[rows], cos_ref[rows, :], sin_ref[rows, :], intra_ref, qd_ref, kd_ref, cd_ref,
                  st_ref.at[sub], dy_ref.at[rows], dp_ref.at[rows], dstate)

    def chunk(p_ref, cos, sin, intra_ref, qd_ref, kd_ref, cd_ref, st_ref, dy_ref, dp_ref, dstate):
        st = [st_ref[h] for h in _HEADS]
        dsn = [dstate[h] for h in _HEADS]
        f = _ret_chunk_fwd(p_ref, cos, sin, intra_ref, qd_ref, st)
        vb, qrb, kr, g = f["vb"], f["qrb"], f["kr"], f["g"]
        norm = [_group_norm(f["y"][h]) for h in _HEADS]
        sg = [_sigmoid(g[h]) for h in _HEADS]
        dyr = [dy_ref[:, h * RET_DK:(h + 1) * RET_DK] for h in _HEADS]
        dyn = [dyr[h] * (g[h] * sg[h]) for h in _HEADS]
        dg = [dyr[h] * norm[h][0] * (sg[h] * (1.0 + g[h] * (1.0 - sg[h]))) for h in _HEADS]
        dy = [norm[h][1] * (dyn[h] - jnp.mean(dyn[h], axis=1, keepdims=True)
                            - norm[h][0] * jnp.mean(dyn[h] * norm[h][0], axis=1, keepdims=True)) for h in _HEADS]
        dyb, dsnb = [_bf(a) for a in dy], [_bf(a) for a in dsn]
        ds = [_bf(_dot_nt(dyb[h], vb[h]) * intra_ref[h]) for h in _HEADS]
        t = [_bf(dy[h] * qd_ref[h]) for h in _HEADS]
        dv = [_dot_tn(_bf(f["sc"][h]), dyb[h]) + _dot(_bf(kr[h] * kd_ref[h]), dsnb[h]) for h in _HEADS]
        dqr = [_dot(ds[h], _bf(kr[h])) + _dot_nt(t[h], _bf(st[h])) for h in _HEADS]
        dkr = [_dot_tn(ds[h], qrb[h]) + _dot_nt(vb[h], dsnb[h]) * kd_ref[h] for h in _HEADS]
        new_ds = [dsn[h] * cd_ref[h] + _dot_tn(qrb[h], t[h]) for h in _HEADS]
        for h in _HEADS:
            lo = h * RET_DK
            dstate[h] = new_ds[h]
            dp_ref[:, lo:lo + RET_DK] = _bf(_rope_t(dqr[h], cos, sin))
            dp_ref[:, RET_W + lo:RET_W + lo + RET_DK] = _bf(_rope_t(dkr[h], cos, sin) * (RET_DK ** -0.5))
            dp_ref[:, 2 * RET_W + lo:2 * RET_W + lo + RET_DK] = _bf(dv[h])
            dp_ref[:, 3 * RET_W + lo:3 * RET_W + lo + RET_DK] = _bf(dg[h])

    steps = nc // RET_CHUNKS_PER_STEP
    sp = _ret_specs(steps, True)
    return pl.pallas_call(
        body,
        out_shape=jax.ShapeDtypeStruct((s, 4 * RET_W), BF16),
        grid=(steps,),
        in_specs=[sp["proj"], sp["half"], sp["half"], sp["intra"], sp["dec"], sp["dec"], sp["cd"],
                  sp["state"], sp["wide"]],
        out_specs=sp["proj"],
        scratch_shapes=[pltpu.VMEM((RET_HEADS, RET_DK, RET_DK), F32)],
        compiler_params=_params("arbitrary"), name=name)(p_r, cos, sin, intra_t, qd_t, kd_t, cd_t, states, dy_r)


CONV_SLAB = 256
_CONV_SLABS = [slice(i * CONV_SLAB, (i + 1) * CONV_SLAB) for i in range(SSD_CD // CONV_SLAB)]


def _conv_taps(ext, w):
    acc = w[SSD_CONV - 1:SSD_CONV] * ext
    for j in range(SSD_CONV - 1):
        acc = acc + w[j:j + 1] * pltpu.roll(ext, SSD_CONV - 1 - j, axis=0)
    return acc


def _conv_fwd(xbc_raw, conv_w, conv_b, name):
    s = xbc_raw.shape[0]
    t8 = CONV_TILE // 8

    def body(cur_ref, prev_ref, w_ref, b_ref, o_ref, slope_ref):
        keep = (pl.program_id(0) > 0).astype(F32)
        for sl in _CONV_SLABS:
            ext = jnp.concatenate([prev_ref[:, sl] * keep, cur_ref[:, sl]], axis=0)
            u = _conv_taps(ext, w_ref[:, sl])[8:] + b_ref[:, sl]
            sg = _sigmoid(u)
            o_ref[:, sl] = u * sg
            slope_ref[:, sl] = _bf(sg * (1.0 + u * (1.0 - sg)))

    out = jax.ShapeDtypeStruct((s, SSD_CD), F32)
    return pl.pallas_call(
        body, out_shape=(out, jax.ShapeDtypeStruct((s, SSD_CD), BF16)), grid=(s // CONV_TILE,),
        in_specs=[_row_spec(SSD_CD, CONV_TILE),
                  pl.BlockSpec((8, SSD_CD), lambda i: (jnp.maximum(i * t8 - 1, 0), 0)),
                  _full_spec((SSD_CONV, SSD_CD)), _full_spec((1, SSD_CD))],
        out_specs=(_row_spec(SSD_CD, CONV_TILE), _row_spec(SSD_CD, CONV_TILE)),
        compiler_params=_params("parallel"), name=name)(xbc_raw, xbc_raw, conv_w, conv_b)


def _conv_bwd(xbc_raw, conv_w, slope, dact, name):
    s = xbc_raw.shape[0]
    nt = s // CONV_TILE
    t8 = CONV_TILE // 8
    rows = CONV_TILE + 8

    def body(cur_ref, w_ref, s_ref, snext_ref, d_ref, dnext_ref, dx_ref, dw_ref, db_ref):
        i = pl.program_id(0)

        @pl.when(i == 0)
        def _():
            dw_ref[...] = jnp.zeros_like(dw_ref)
            db_ref[...] = jnp.zeros_like(db_ref)

        keep_next = (i < nt - 1).astype(F32)
        for sl in _CONV_SLABS:
            w = w_ref[:, sl]
            cur = cur_ref[:, sl]
            duc = d_ref[:, sl].astype(F32) * s_ref[:, sl].astype(F32)
            after = (dnext_ref[:, sl].astype(F32) * snext_ref[:, sl].astype(F32))[:8] * keep_next
            du = jnp.concatenate([duc, after], axis=0)
            dx = w[SSD_CONV - 1:SSD_CONV] * duc
            for j in range(SSD_CONV - 1):
                ahead = pltpu.roll(du, rows - (SSD_CONV - 1 - j), axis=0)[:CONV_TILE]
                dx = dx + w[j:j + 1] * ahead
                dw_ref[j:j + 1, sl] += jnp.sum(cur * ahead, axis=0, keepdims=True)
            dx_ref[:, sl] = _bf(dx)
            dw_ref[SSD_CONV - 1:SSD_CONV, sl] += jnp.sum(cur * duc, axis=0, keepdims=True)
            db_ref[:, sl] += jnp.sum(duc, axis=0, keepdims=True)

    row = _row_spec(SSD_CD, CONV_TILE)
    nxt16 = pl.BlockSpec((16, SSD_CD), lambda i: (jnp.minimum((i + 1) * (t8 // 2), s // 16 - 1), 0))
    return pl.pallas_call(
        body,
        out_shape=(jax.ShapeDtypeStruct((s, SSD_CD), BF16), jax.ShapeDtypeStruct((SSD_CONV, SSD_CD), F32),
                   jax.ShapeDtypeStruct((1, SSD_CD), F32)),
        grid=(nt,),
        in_specs=[row, _full_spec((SSD_CONV, SSD_CD)), row, nxt16, row, nxt16],
        out_specs=(row, _full_spec((SSD_CONV, SSD_CD)), _full_spec((1, SSD_CD))),
        compiler_params=_params("arbitrary"), name=name)(xbc_raw, conv_w, slope, slope, dact, dact)


def _head_select():
    e = np.zeros((HEAD_PAD, SSD_W), np.float32)
    for h in range(SSD_HEADS):
        e[h, h * SSD_P:(h + 1) * SSD_P] = 1.0
    return jnp.asarray(e, BF16), jnp.asarray(e.T, BF16)


def _pad_heads(v):
    return jnp.pad(v.reshape(1, SSD_HEADS).astype(F32), ((0, 0), (0, HEAD_PAD - SSD_HEADS)))


def _ssd_masks():
    r = _iota((CHUNK, SSD_QW), 0)
    c = _iota((CHUNK, SSD_QW), 1) % SSD_P
    itile = (r == c).astype(F32)
    ctile = r >= c
    rb = _iota((SSD_QW, SSD_QW), 0) // SSD_P
    cb = _iota((SSD_QW, SSD_QW), 1) // SSD_P
    return itile, ctile, rb == cb


_TILES = [slice(i * SSD_QW, (i + 1) * SSD_QW) for i in range(SSD_W // SSD_QW)]


def _ssd_heads(dtr_ref, dtb_ref, alog_ref):
    u = dtr_ref[...] + dtb_ref[...]
    dt = _softplus(u)
    nexp = -jnp.exp(alog_ref[...])
    return u, dt, nexp, dt * nexp


def _ssd_group(xbc_ref, acol_ref, dte_ref, gs):
    acol = acol_ref[:, gs]
    alast = acol[CHUNK - 1:CHUNK, :]
    xs, dte = xbc_ref[:, gs], dte_ref[:, gs]
    return dict(xs=xs, dte=dte, xdt=xs * dte, ea=jnp.exp(acol), tail=jnp.exp(alast - acol), eal=jnp.exp(alast))


def _silu_gate(z_ref, y_ref, tl):
    zv = z_ref[:, tl]
    sz = _sigmoid(zv)
    return zv, sz, y_ref[:, tl] * (zv * sz)


def _tile4(x):
    return jnp.concatenate([x, x, x, x], axis=0)


def _fold4(x):
    return x[0:CHUNK] + x[CHUNK:2 * CHUNK] + x[2 * CHUNK:3 * CHUNK] + x[3 * CHUNK:4 * CHUNK]


def _ssd_specs(steps, rev):
    cidx = (lambda c: steps - 1 - c) if rev else (lambda c: c)
    return dict(
        xbc=pl.BlockSpec((STEP_ROWS, SSD_CD), lambda c: (cidx(c), 0)),
        dt=pl.BlockSpec((STEP_ROWS, HEAD_PAD), lambda c: (cidx(c), 0)),
        wide=pl.BlockSpec((STEP_ROWS, SSD_W), lambda c: (cidx(c), 0)),
        state=pl.BlockSpec((CHUNKS_PER_STEP, SSD_N, SSD_W), lambda c: (cidx(c), 0, 0)),
        head=_full_spec((1, HEAD_PAD)),
        roww=_full_spec((1, SSD_W)),
        e=_full_spec((HEAD_PAD, SSD_W)),
        et=_full_spec((SSD_W, HEAD_PAD)),
    )


_GROUPS = range(SSD_GROUPS)
_GROUP_LANES = [slice(g * SSD_GW, (g + 1) * SSD_GW) for g in _GROUPS]
_QUADS = [(g, slice(g * SSD_GW + q * SSD_QW, g * SSD_GW + (q + 1) * SSD_QW), slice(q * SSD_QW, (q + 1) * SSD_QW))
          for g in _GROUPS for q in range(SSD_GW // SSD_QW)]


def _ssd_bc(xbc_ref):
    bg = [_bf(xbc_ref[:, B_OFF + g * SSD_N:B_OFF + (g + 1) * SSD_N]) for g in _GROUPS]
    cg = [_bf(xbc_ref[:, C_OFF + g * SSD_N:C_OFF + (g + 1) * SSD_N]) for g in _GROUPS]
    return bg, cg


def _ssd_decay(aq, itile, ctile):
    arow = jnp.sum(aq * itile, axis=0, keepdims=True)
    return jnp.exp(jnp.where(ctile, aq - arow, -jnp.inf))


def _ssd_blockdiag(xq, bdmask):
    return jnp.where(bdmask, _tile4(_bf(xq)), jnp.zeros((), BF16))


def _ssd_fwd(xbc, dt_raw, z, dt_bias, a_log, dske, norm_w, name):
    s = xbc.shape[0]
    nc = s // CHUNK
    e_sel, _ = _head_select()

    def body(xbc_ref, dtr_ref, z_ref, dtb_ref, alog_ref, dske_ref, nw_ref, e_ref,
             yraw_ref, ys_ref, st_ref, acol_ref, dte_ref, state):
        @pl.when(pl.program_id(0) == 0)
        def _():
            state[...] = jnp.zeros_like(state)

        for sub in range(CHUNKS_PER_STEP):
            rows = _chunk_rows(sub)
            chunk(xbc_ref.at[rows], dtr_ref.at[rows], z_ref.at[rows], dtb_ref, alog_ref, dske_ref, nw_ref, e_ref,
                  yraw_ref.at[rows], ys_ref.at[rows], st_ref.at[sub], acol_ref.at[rows], dte_ref.at[rows], state)

    def chunk(xbc_ref, dtr_ref, z_ref, dtb_ref, alog_ref, dske_ref, nw_ref, e_ref,
              yraw_ref, ys_ref, st_ref, acol_ref, dte_ref, state):
        _, dt, _, a = _ssd_heads(dtr_ref, dtb_ref, alog_ref)
        tril = _bf((_iota((CHUNK, CHUNK), 0) >= _iota((CHUNK, CHUNK), 1)).astype(F32))
        ac3, dt3 = _split3(_dot_exact_r(tril, a)), _split3(dt)
        for tl in _TILES:
            e_t = e_ref[:, tl]
            acol_ref[:, tl] = _dot(ac3[0], e_t) + _dot(ac3[1], e_t) + _dot(ac3[2], e_t)
            dte_ref[:, tl] = _dot(dt3[0], e_t) + _dot(dt3[1], e_t)

        itile, ctile, bdmask = _ssd_masks()
        st_ref[...] = state[...]
        bg, cg = _ssd_bc(xbc_ref)
        for g0 in range(0, SSD_GROUPS, SSD_BLOCK_GROUPS):
            gids = range(g0, g0 + SSD_BLOCK_GROUPS)
            quads = [(g, ql, qs) for g, ql, qs in _QUADS if g in gids]
            q = {g: _ssd_group(xbc_ref, acol_ref, dte_ref, _GROUP_LANES[g]) for g in gids}
            stg = {g: state[:, _GROUP_LANES[g]] for g in gids}
            cbt = {g: _dot_nt(cg[g], _tile4(bg[g])) for g in gids}
            ys = {g: _dot(cg[g], _bf(stg[g])) for g in gids}
            dq = [_ssd_decay(acol_ref[:, ql], itile, ctile) for _, ql, _ in quads]
            xbd = [_ssd_blockdiag(q[g]["xdt"][:, qs], bdmask) for g, _, qs in quads]
            yq = [_dot(_bf(cbt[g] * dq[i]), xbd[i]) + ys[g][:, qs] * q[g]["ea"][:, qs]
                  + dske_ref[:, ql] * q[g]["xs"][:, qs] for i, (g, ql, qs) in enumerate(quads)]
            new_st = {g: stg[g] * q[g]["eal"] + _dot_tn(bg[g], _bf(q[g]["xdt"] * q[g]["tail"])) for g in gids}
            for i, (_, ql, _) in enumerate(quads):
                yraw_ref[:, ql] = yq[i]
            for g in gids:
                state[:, _GROUP_LANES[g]] = new_st[g]

        sq = jnp.zeros((CHUNK, SSD_QW), F32)
        for tl in _TILES:
            t = _silu_gate(z_ref, yraw_ref, tl)[2]
            sq = sq + t * t
        rstd = lax.rsqrt(jnp.sum(sq, axis=1, keepdims=True) * (1.0 / SSD_W) + EPS)
        for tl in _TILES:
            ys_ref[:, tl] = _bf(_silu_gate(z_ref, yraw_ref, tl)[2] * rstd * nw_ref[:, tl])

    steps = nc // CHUNKS_PER_STEP
    sp = _ssd_specs(steps, False)
    wide = jax.ShapeDtypeStruct((s, SSD_W), F32)
    return pl.pallas_call(
        body,
        out_shape=(wide, jax.ShapeDtypeStruct((s, SSD_W), BF16), jax.ShapeDtypeStruct((nc, SSD_N, SSD_W), F32),
                   wide, wide),
        grid=(steps,),
        in_specs=[sp["xbc"], sp["dt"], sp["wide"], sp["head"], sp["head"], sp["roww"], sp["roww"], sp["e"]],
        out_specs=(sp["wide"], sp["wide"], sp["state"], sp["wide"], sp["wide"]),
        scratch_shapes=[pltpu.VMEM((SSD_N, SSD_W), F32)],
        compiler_params=_params("arbitrary"), name=name,
    )(xbc, dt_raw, z, dt_bias, a_log, dske, norm_w, e_sel)


def _ssd_bwd(xbc, dt_raw, z, yraw, states, dys, acol, dte, dt_bias, a_log, dske, norm_w, name, rider):
    s = xbc.shape[0]
    nc = s // CHUNK
    steps = nc // CHUNKS_PER_STEP
    _, et_sel = _head_select()
    n_in, n_out, n_scratch = 13, 7, 5
    n_rin, n_rout = len(rider.inputs), len(rider.out_shapes)

    def body(*refs):
        ins, refs = refs[:n_in], refs[n_in:]
        r_ins, refs = refs[:n_rin], refs[n_rin:]
        outs, refs = refs[:n_out], refs[n_out:]
        r_outs, refs = refs[:n_rout], refs[n_rout:]
        ride = functools.partial(rider.run, pl.program_id(0), steps, r_ins, r_outs, refs[n_scratch],
                                 refs[n_scratch + 1])
        ride(last=False)
        compute(*ins, *outs, *refs[:n_scratch])
        ride(last=True)

    def compute(xbc_ref, dtr_ref, z_ref, yraw_ref, st_ref, dys_ref, acol_ref, dte_ref, dtb_ref, alog_ref,
                dske_ref, nw_ref, et_ref, dxbc_ref, dz_ref, ddtr_ref, ddtb_ref, dalog_ref, ddsk_ref, dnw_ref,
                dstate, dsk_acc, dy_s, dacol_s, ddte_s):
        @pl.when(pl.program_id(0) == 0)
        def _():
            dstate[...] = jnp.zeros_like(dstate)
            dsk_acc[...] = jnp.zeros_like(dsk_acc)
            ddtb_ref[...] = jnp.zeros_like(ddtb_ref)
            dalog_ref[...] = jnp.zeros_like(dalog_ref)
            dnw_ref[...] = jnp.zeros_like(dnw_ref)

        for sub in reversed(range(CHUNKS_PER_STEP)):
            rows = _chunk_rows(sub)
            chunk(xbc_ref.at[rows], dtr_ref.at[rows], z_ref.at[rows], yraw_ref.at[rows], st_ref.at[sub],
                  dys_ref.at[rows], acol_ref.at[rows], dte_ref.at[rows], dtb_ref, alog_ref, dske_ref, nw_ref, et_ref,
                  dxbc_ref.at[rows], dz_ref.at[rows], ddtr_ref.at[rows], ddtb_ref, dalog_ref, dnw_ref,
                  dstate, dsk_acc, dy_s, dacol_s, ddte_s)

        @pl.when(pl.program_id(0) == steps - 1)
        def _():
            ddsk_ref[...] = _dot_exact_l(jnp.broadcast_to(dsk_acc[...], (8, SSD_W)), et_ref[...])[0:1]

    def chunk(xbc_ref, dtr_ref, z_ref, yraw_ref, st_ref, dys_ref, acol_ref, dte_ref, dtb_ref, alog_ref,
              dske_ref, nw_ref, et_ref, dxbc_ref, dz_ref, ddtr_ref, ddtb_ref, dalog_ref, dnw_ref,
              dstate, dsk_acc, dy_s, dacol_s, ddte_s):
        itile, ctile, bdmask = _ssd_masks()
        last_row = (_iota((CHUNK, 1), 0) == CHUNK - 1).astype(F32)

        s1 = jnp.zeros((CHUNK, SSD_QW), F32)
        s2 = jnp.zeros((CHUNK, SSD_QW), F32)
        for tl in _TILES:
            t = _silu_gate(z_ref, yraw_ref, tl)[2]
            s1 = s1 + t * t
            s2 = s2 + dys_ref[:, tl] * nw_ref[:, tl] * t
        rstd = lax.rsqrt(jnp.sum(s1, axis=1, keepdims=True) * (1.0 / SSD_W) + EPS)
        back = rstd * rstd * rstd * jnp.sum(s2, axis=1, keepdims=True) * (1.0 / SSD_W)
        for tl in _TILES:
            zv, sz, t = _silu_gate(z_ref, yraw_ref, tl)
            dysv = dys_ref[:, tl]
            dt_ = rstd * (dysv * nw_ref[:, tl]) - t * back
            dnw_ref[:, tl] += jnp.sum(dysv * t * rstd, axis=0, keepdims=True)
            dz_ref[:, tl] = _bf(dt_ * yraw_ref[:, tl] * (sz * (1.0 + zv * (1.0 - sz))))
            dy_t = dt_ * (zv * sz)
            dy_s[:, tl] = dy_t
            dsk_acc[:, tl] += jnp.sum(dy_t * xbc_ref[:, tl], axis=0, keepdims=True)

        gl = _GROUP_LANES
        bg, cg = _ssd_bc(xbc_ref)
        for g0 in range(0, SSD_GROUPS, SSD_BLOCK_GROUPS):
            gids = range(g0, g0 + SSD_BLOCK_GROUPS)
            quads = [(i, g, ql, qs) for i, (g, ql, qs) in enumerate(_QUADS) if g in gids]
            q = {g: _ssd_group(xbc_ref, acol_ref, dte_ref, gl[g]) for g in gids}
            stg = {g: st_ref[:, gl[g]] for g in gids}
            dsn = {g: dstate[:, gl[g]] for g in gids}
            stgb, dsnb = {g: _bf(stg[g]) for g in gids}, {g: _bf(dsn[g]) for g in gids}
            btile = {g: _tile4(bg[g]) for g in gids}
            cbt = {g: _dot_nt(cg[g], btile[g]) for g in gids}
            dyg = {g: dy_s[:, gl[g]] for g in gids}
            eag, tailg = {g: q[g]["ea"] for g in gids}, {g: q[g]["tail"] for g in gids}
            xdtg, ealg = {g: q[g]["xdt"] for g in gids}, {g: q[g]["eal"] for g in gids}
            dys_g = {g: _bf(dyg[g] * eag[g]) for g in gids}
            ysg = {g: _dot(cg[g], stgb[g]) for g in gids}
            dc0 = {g: _dot_nt(dys_g[g], stgb[g]) for g in gids}
            dst = {g: _dot_tn(cg[g], dys_g[g]) for g in gids}
            dwt = {g: _dot(bg[g], dsnb[g]) for g in gids}
            db0 = {g: _dot_nt(_bf(xdtg[g] * tailg[g]), dsnb[g]) for g in gids}
            dtl = {g: dwt[g] * xdtg[g] * tailg[g] for g in gids}
            dal_row = {g: jnp.sum(dtl[g], axis=0, keepdims=True)
                       + jnp.sum(dsn[g] * stg[g], axis=0, keepdims=True) * ealg[g] for g in gids}
            for g in gids:
                dstate[:, gl[g]] = dst[g] + dsn[g] * ealg[g]
            dq = {i: _ssd_decay(acol_ref[:, ql], itile, ctile) for i, _, ql, _ in quads}
            mq = {i: cbt[g] * dq[i] for i, g, _, _ in quads}
            xbd = {i: _ssd_blockdiag(xdtg[g][:, qs], bdmask) for i, g, _, qs in quads}
            dyq = {i: _bf(dyg[g][:, qs]) for i, g, _, qs in quads}
            dm = {i: _dot_nt(dyq[i], xbd[i]) for i, _, _, _ in quads}
            dxdt_q = {i: _fold4(jnp.where(bdmask, _dot_tn(_bf(mq[i]), dyq[i]), 0.0)) for i, _, _, _ in quads}
            eq = {i: dm[i] * mq[i] for i, _, _, _ in quads}
            dacol_q = {i: eq[i] - itile * jnp.sum(eq[i], axis=0, keepdims=True) for i, _, _, _ in quads}
            dcbt = {i: _bf(dm[i] * dq[i]) for i, _, _, _ in quads}
            dc_q = {i: _dot(dcbt[i], btile[g]) for i, g, _, _ in quads}
            db_q = {i: _fold4(_dot_tn(dcbt[i], cg[g])) for i, g, _, _ in quads}
            for g in gids:
                dxdt_g = jnp.concatenate([dxdt_q[2 * g], dxdt_q[2 * g + 1]], axis=1) + dwt[g] * tailg[g]
                dxbc_ref[:, gl[g]] = _bf(dyg[g] * dske_ref[:, gl[g]] + dxdt_g * q[g]["dte"])
                ddte_s[:, gl[g]] = dxdt_g * q[g]["xs"]
                dacol_s[:, gl[g]] = (jnp.concatenate([dacol_q[2 * g], dacol_q[2 * g + 1]], axis=1)
                                     + dyg[g] * (ysg[g] * eag[g]) - dtl[g] + last_row * dal_row[g])
                dxbc_ref[:, B_OFF + g * SSD_N:B_OFF + (g + 1) * SSD_N] = _bf(db0[g] + db_q[2 * g] + db_q[2 * g + 1])
                dxbc_ref[:, C_OFF + g * SSD_N:C_OFF + (g + 1) * SSD_N] = _bf(dc0[g] + dc_q[2 * g] + dc_q[2 * g + 1])

        dacum = jnp.zeros((CHUNK, HEAD_PAD), F32)
        ddt = jnp.zeros((CHUNK, HEAD_PAD), F32)
        for i, tl in enumerate(_TILES):
            et_t = et_ref[i * SSD_QW:(i + 1) * SSD_QW, :]
            dacum = dacum + _dot_exact_l(dacol_s[:, tl], et_t, pieces=2)
            ddt = ddt + _dot_exact_l(ddte_s[:, tl], et_t, pieces=2)
        u, _, nexp, a = _ssd_heads(dtr_ref, dtb_ref, alog_ref)
        triu = _bf((_iota((CHUNK, CHUNK), 1) >= _iota((CHUNK, CHUNK), 0)).astype(F32))
        da = _dot_exact_r(triu, dacum)
        ddt = ddt + da * nexp
        dalog_ref[...] += jnp.sum(da * a, axis=0, keepdims=True)
        du = ddt * _sigmoid(u)
        ddtr_ref[...] = _bf(du)
        ddtb_ref[...] += jnp.sum(du, axis=0, keepdims=True)

    sp = _ssd_specs(steps, True)
    return pl.pallas_call(
        body,
        out_shape=(jax.ShapeDtypeStruct((s, SSD_CD), BF16), jax.ShapeDtypeStruct((s, SSD_W), BF16),
                   jax.ShapeDtypeStruct((s, HEAD_PAD), BF16), jax.ShapeDtypeStruct((1, HEAD_PAD), F32),
                   jax.ShapeDtypeStruct((1, HEAD_PAD), F32), jax.ShapeDtypeStruct((1, HEAD_PAD), F32),
                   jax.ShapeDtypeStruct((1, SSD_W), F32), *rider.out_shapes),
        grid=(steps,),
        in_specs=[sp["xbc"], sp["dt"], sp["wide"], sp["wide"], sp["state"], sp["wide"], sp["wide"], sp["wide"],
                  sp["head"], sp["head"], sp["roww"], sp["roww"], sp["et"], *rider.in_specs()],
        out_specs=(sp["xbc"], sp["wide"], sp["dt"], sp["head"], sp["head"], sp["head"], sp["roww"],
                   *rider.out_specs()),
        scratch_shapes=[pltpu.VMEM((SSD_N, SSD_W), F32), pltpu.VMEM((1, SSD_W), F32),
                        *[pltpu.VMEM((CHUNK, SSD_W), F32)] * 3, *rider.scratch()],
        compiler_params=_params("arbitrary"), name=name,
    )(xbc, dt_raw, z, yraw, states, dys, acol, dte, dt_bias, a_log, dske, norm_w, et_sel, *rider.inputs)


W_IN_SHARD = IN_PROJ // N_DEV
W_IN_ROW_PARTS = ((0, 1152), (1152, 832), (1984, 840))


def _device_step(x, pos_col, target, norm1_w, conv_w_shard, conv_b, dt_bias, a_log, d_skip, ssd_norm_w, norm_f_w,
                 w_in_shard, br_ret_shard, br_ssd_shard, out_shard):
    inv_freq = jnp.asarray(ROPE_THETA ** (-np.arange(ROPE_HALF, dtype=np.float64) / ROPE_HALF), F32).reshape(1, ROPE_HALF)
    dtb, alog = _pad_heads(dt_bias), _pad_heads(a_log)
    dske = jnp.repeat(d_skip.reshape(SSD_HEADS).astype(F32), SSD_P).reshape(1, SSD_W)
    my_core = lax.axis_index("c")

    h, cos, sin, w_all = _rmsnorm_rope_fwd(x, norm1_w, pos_col, inv_freq, "rmsnorm1_fwd",
                                           _gather_rider([w_in_shard], 1.0, 1.0))
    w_all = w_all.reshape(IN_PROJ, D_MODEL)
    w_dt = jnp.pad(w_all[OFF_DT:OFF_G], ((0, HEAD_PAD - SSD_HEADS), (0, 0)))
    w_g = w_all[OFF_G:]
    rows_r, rows_z, rows_xbc = (0, OFF_Z), (OFF_Z, SSD_W), (OFF_XBC, SSD_CD)
    gather = _gather_rider([br_ret_shard, br_ssd_shard, out_shard, conv_w_shard], relay_at=0.35, pass_at=0.6)
    p_r, all_ret, all_ssd, all_out, all_conv = _matmul(h, w_all, "nt", F32, "proj_ret", rider=gather, b_rows=rows_r)
    w_br_ret = all_ret.reshape(RET_W, D_MODEL)
    w_br_ssd = all_ssd.reshape(SSD_W, D_MODEL)
    w_out = all_out.reshape(D_MODEL, D_MODEL)
    conv_w = all_conv.transpose(1, 0, 2).reshape(SSD_CONV, SSD_CD)
    p_z = _matmul(h, w_all, "nt", F32, "proj_z", b_rows=rows_z)
    p_xbc = _matmul(h, w_all, "nt", F32, "proj_xbc", b_rows=rows_xbc)
    p_dt = _matmul(h, w_dt, "nt", F32, "proj_dt")
    p_g = _matmul(h, w_g, "nt", BF16, "proj_gates")
    y_r, ret_states = _retention_fwd(p_r, cos, sin, "retention_fwd")
    xbc_act, silu_slope = _conv_fwd(p_xbc, conv_w, conv_b, "conv_fwd")
    y_raw, y_s, ssd_states, acol, dte = _ssd_fwd(xbc_act, p_dt, p_z, dtb, alog, dske, ssd_norm_w, "ssd_fwd")
    pr = _matmul(y_r, w_br_ret, "nn", BF16, "branch_ret")
    ps = _matmul(y_s, w_br_ssd, "nn", BF16, "branch_ssd")
    merged = _merge_fwd(p_g, pr, ps, "merge_fwd")
    o = _matmul(merged, w_out, "nn", F32, "out_proj")
    dx2, g_norm_f, loss_acc = _final_fwd_bwd(x, o, norm_f_w, target, "final_norm_loss")

    g_w_out = _matmul(merged, dx2, "tn", BF16, "grad_w_out")
    dmerged = _matmul(dx2, w_out, "nt", F32, "d_merged")
    dpr, dps, dp_g = _merge_bwd(p_g, pr, ps, dmerged, "merge_bwd")
    g_w_br_ret = _matmul(y_r, dpr, "tn", BF16, "grad_w_br_ret")
    g_w_br_ssd = _matmul(y_s, dps, "tn", BF16, "grad_w_br_ssd")
    dy_r = _matmul(dpr, w_br_ret, "nt", F32, "d_y_ret")
    dy_s = _matmul(dps, w_br_ssd, "nt", F32, "d_y_ssd")
    scatter = _scatter_rider([g_w_out.reshape(N_DEV, -1, D_MODEL), g_w_br_ret.reshape(N_DEV, -1, D_MODEL),
                              g_w_br_ssd.reshape(N_DEV, -1, D_MODEL)])
    dxbc_act, dp_z, dp_dt, g_dtb, g_alog, g_dsk, g_ssd_norm, got_out, got_ret, got_ssd = _ssd_bwd(
        xbc_act, p_dt, p_z, y_raw, ssd_states, dy_s, acol, dte, dtb, alog, dske, ssd_norm_w, "ssd_bwd", scatter)
    dp_xbc, g_conv_w, g_conv_b = _conv_bwd(p_xbc, conv_w, silu_slope, dxbc_act, "conv_bwd")
    dp_r = _retention_bwd(p_r, cos, sin, ret_states, dy_r, "retention_bwd")
    g_w_in = jnp.concatenate([
        _matmul(dp_r, h, "tn", BF16, "grad_w_ret"),
        _matmul(dp_z, h, "tn", BF16, "grad_w_z"),
        _matmul(dp_xbc, h, "tn", BF16, "grad_w_xbc"),
        _matmul(dp_dt, h, "tn", BF16, "grad_w_dt")[:SSD_HEADS],
        _matmul(dp_g, h, "tn", BF16, "grad_w_gates"),
    ], axis=0)
    blocks = g_w_in.reshape(N_DEV, W_IN_SHARD, D_MODEL)
    dh, from_sibling = _matmul(dp_r, w_all, "nn", F32, "d_h_ret", rider=_sibling_rider(blocks), b_rows=rows_r)
    chip_sum = _chip_sum(blocks, from_sibling, my_core, "w_in_chip_sum")
    carriers = (("d_h_xbc", dp_xbc, w_all, rows_xbc), ("d_h_gates", dp_g, w_g, None), ("d_h_z", dp_z, w_all, rows_z))
    landed = []
    for (row0, rows), (nm, dp, w, b_rows) in zip(W_IN_ROW_PARTS, carriers):
        dt_pair = (dp_dt, w_dt) if nm == "d_h_z" else None
        dh, got = _matmul(dp, w, "nn", F32, nm, add=dh, rider=_chip_rider(chip_sum, row0, rows), b_rows=b_rows,
                          extra=dt_pair)
        landed.append(got)
    grad_x, g_norm1 = _rmsnorm_bwd(x, norm1_w, dh, dx2, "rmsnorm1_bwd")
    small = dict(norm1_w=g_norm1, conv_w=g_conv_w, conv_b=g_conv_b, dt_bias=g_dtb[:, :SSD_HEADS],
                 a_log=g_alog[:, :SSD_HEADS], d_skip=g_dsk[:, :SSD_HEADS], ssd_norm_w=g_ssd_norm,
                 norm_f_w=g_norm_f)
    big = dict(w_in=_sum_row_parts(landed, "w_in_sum"), w_br_ret=_sum_slots(got_ret, "w_br_ret_sum"),
               w_br_ssd=_sum_slots(got_ssd, "w_br_ssd_sum"), w_out=_sum_slots(got_out, "w_out_sum"))
    return loss_acc[0, 0], grad_x, small, big


def _all_reduce_small(vec, name):
    r, c = vec.shape

    def body(x_ref, out_ref, land, send_sems, recv_sems):
        x, y, cc = _mesh_pos()
        my_idx = 4 * x + 2 * y + cc
        land[my_idx] = x_ref[...]
        copies = []
        for k in range(1, N_DEV):
            px, py, pc = x ^ (k >> 2), y ^ ((k >> 1) & 1), cc ^ (k & 1)
            cp = pltpu.make_async_remote_copy(
                src_ref=x_ref, dst_ref=land.at[my_idx],
                send_sem=send_sems.at[k - 1], recv_sem=recv_sems.at[k - 1],
                device_id=(px, py, pc), device_id_type=MESH)
            cp.start()
            copies.append(cp)
        for k in range(1, N_DEV):
            px, py, pc = x ^ (k >> 2), y ^ ((k >> 1) & 1), cc ^ (k & 1)
            pltpu.make_async_remote_copy(
                src_ref=x_ref, dst_ref=land.at[4 * px + 2 * py + pc],
                send_sem=send_sems.at[k - 1], recv_sem=recv_sems.at[k - 1],
                device_id=(px, py, pc), device_id_type=MESH).wait_recv()
        for cp in copies:
            cp.wait_send()
        acc = land[0]
        for i in range(1, N_DEV):
            acc = acc + land[i]
        out_ref[...] = acc

    return pl.pallas_call(
        body,
        out_shape=jax.ShapeDtypeStruct((r, c), F32),
        in_specs=[pl.BlockSpec(memory_space=pltpu.VMEM)],
        out_specs=pl.BlockSpec(memory_space=pltpu.VMEM),
        scratch_shapes=[pltpu.VMEM((N_DEV, r, c), F32), pltpu.SemaphoreType.DMA((7,)),
                        pltpu.SemaphoreType.DMA((7,))],
        name=name)(vec)


_SMALL = ("norm1_w", "conv_w", "conv_b", "dt_bias", "a_log", "d_skip", "ssd_norm_w", "norm_f_w")
_SMALL_COLS = 128
_W_IN = "w_in"
_WEIGHTS = ("norm1_w", "w_in", "conv_w", "conv_b", "dt_bias", "a_log", "d_skip", "ssd_norm_w",
            "w_br_ret", "w_br_ssd", "w_out", "norm_f_w")


def _pack(parts):
    flat = jnp.concatenate([p.reshape(-1).astype(F32) for p in parts])
    rows = -(-flat.shape[0] // (8 * _SMALL_COLS)) * 8
    return jnp.pad(flat, (0, rows * _SMALL_COLS - flat.shape[0])).reshape(rows, _SMALL_COLS)


def _unpack(packed, shapes):
    flat = packed.reshape(-1)
    out, off = [], 0
    for shp in shapes:
        n = int(np.prod(shp))
        out.append(flat[off:off + n].reshape(shp))
        off += n
    return out


def kernel(x, positions, norm1_w, w_in, conv_w, conv_b, dt_bias, a_log, d_skip, ssd_norm_w, w_br_ret, w_br_ssd, w_out, norm_f_w, loss_target, m_norm1_w, m_w_in, m_conv_w, m_conv_b, m_dt_bias, m_a_log, m_d_skip, m_ssd_norm_w, m_w_br_ret, m_w_br_ssd, m_w_out, m_norm_f_w, v_norm1_w, v_w_in, v_conv_w, v_conv_b, v_dt_bias, v_a_log, v_d_skip, v_ssd_norm_w, v_w_br_ret, v_w_br_ssd, v_w_out, v_norm_f_w):
    w = dict(norm1_w=norm1_w, w_in=w_in, conv_w=conv_w, conv_b=conv_b, dt_bias=dt_bias, a_log=a_log,
             d_skip=d_skip, ssd_norm_w=ssd_norm_w, w_br_ret=w_br_ret, w_br_ssd=w_br_ssd, w_out=w_out,
             norm_f_w=norm_f_w)
    m = dict(norm1_w=m_norm1_w, w_in=m_w_in, conv_w=m_conv_w, conv_b=m_conv_b, dt_bias=m_dt_bias,
             a_log=m_a_log, d_skip=m_d_skip, ssd_norm_w=m_ssd_norm_w, w_br_ret=m_w_br_ret,
             w_br_ssd=m_w_br_ssd, w_out=m_w_out, norm_f_w=m_norm_f_w)
    v = dict(norm1_w=v_norm1_w, w_in=v_w_in, conv_w=v_conv_w, conv_b=v_conv_b, dt_bias=v_dt_bias,
             a_log=v_a_log, d_skip=v_d_skip, ssd_norm_w=v_ssd_norm_w, w_br_ret=v_w_br_ret,
             w_br_ssd=v_w_br_ssd, w_out=v_w_out, norm_f_w=v_norm_f_w)
    s = x.shape[1]
    my_idx = 4 * lax.axis_index("x") + 2 * lax.axis_index("y") + lax.axis_index("c")

    w[_W_IN], m[_W_IN], v[_W_IN] = w_in[0].T, m_w_in[0].T, v_w_in[0].T

    loss_part, grad_x, g_small, g_big = _device_step(
        x[0], positions.reshape(s, 1), loss_target[0], norm1_w, conv_w[0], conv_b, dt_bias, a_log, d_skip,
        ssd_norm_w, norm_f_w.reshape(1, D_MODEL), _cast_bf16(w[_W_IN], "cast_w_in"),
        _cast_bf16(w_br_ret[0], "cast_w_br_ret"), _cast_bf16(w_br_ssd[0], "cast_w_br_ssd"),
        _cast_bf16(w_out[0], "cast_w_out"))

    loss = lax.psum(loss_part, ("x", "y", "c"))

    small_shapes = [g_small[n].shape for n in _SMALL]
    summed = _unpack(_all_reduce_small(_pack([g_small[n] for n in _SMALL]), "allreduce_small"), small_shapes)
    grads = dict(zip(_SMALL, summed))
    conv_cols = SSD_CD // N_DEV
    grads["conv_w"] = lax.dynamic_slice_in_dim(grads["conv_w"], my_idx * conv_cols, conv_cols, axis=1)
    grads["norm_f_w"] = grads["norm_f_w"].reshape(D_MODEL)
    for n in ("norm1_w", "conv_w", "conv_b", "dt_bias", "a_log", "d_skip", "ssd_norm_w"):
        grads[n] = grads[n].reshape(w[n].shape)

    delta, new_m, new_v = {}, {}, {}
    for n in ("w_br_ret", "w_br_ssd", "w_out"):
        w[n], m[n], v[n] = w[n][0], m[n][0], v[n][0]
    for n in (_W_IN, "w_br_ret", "w_br_ssd", "w_out"):
        back = (lambda a: a.T[None]) if n == _W_IN else (lambda a: a[None])
        res = _adamw(w[n], g_big[n], m[n], v[n], "adamw_" + n)
        grads[n] = back(g_big[n])
        delta[n], new_m[n], new_v[n] = (back(a) for a in res)
    shapes = [w[n].shape for n in _SMALL]
    packed = _adamw(_pack([w[n] for n in _SMALL]), _pack([grads[n] for n in _SMALL]),
                    _pack([m[n] for n in _SMALL]), _pack([v[n] for n in _SMALL]), "adamw_small")
    for res, dst in zip(packed, (delta, new_m, new_v)):
        for n, a in zip(_SMALL, _unpack(res, shapes)):
            dst[n] = a

    return (loss, grad_x.reshape(x.shape), *[grads[n] for n in _WEIGHTS], *[delta[n] for n in _WEIGHTS],
            *[new_m[n] for n in _WEIGHTS], *[new_v[n] for n in _WEIGHTS])
```

```python
import functools

import numpy as np
import jax
import jax.numpy as jnp
from jax import lax
from jax.experimental import pallas as pl
from jax.experimental.pallas import tpu as pltpu

F32 = jnp.float32
BF16 = jnp.bfloat16

D_MODEL = 2048
CHUNK = 64
CHUNKS_PER_STEP = 2
STEP_ROWS = CHUNK * CHUNKS_PER_STEP
RET_CHUNKS_PER_STEP = 4
RET_STEP_ROWS = CHUNK * RET_CHUNKS_PER_STEP
EPS = 1e-6
N_DEV = 8

RET_HEADS = 8
RET_DK = 256
RET_W = RET_HEADS * RET_DK
ROPE_THETA = 10000.0
ROPE_HALF = RET_DK // 2

SSD_W = 4096
SSD_P = 64
SSD_HEADS = 64
SSD_GROUPS = 8
SSD_N = 128
SSD_GW = SSD_W // SSD_GROUPS
SSD_QW = 256
SSD_BLOCK_GROUPS = 2
SSD_CONV = 4
SSD_CD = SSD_W + 2 * SSD_GROUPS * SSD_N
HEAD_PAD = 128
B_OFF = SSD_W
C_OFF = SSD_W + SSD_GROUPS * SSD_N

ADAM_LR = 0.001
ADAM_B1 = 0.9
ADAM_B2 = 0.999
ADAM_EPS = 1e-08
ADAM_WD = 0.01
ADAM_STEP = 10

SPLITS = (RET_W, RET_W, RET_W, RET_W, SSD_W, SSD_CD, SSD_HEADS, D_MODEL, D_MODEL)
IN_PROJ = sum(SPLITS)
OFF_Z = 4 * RET_W
OFF_XBC = OFF_Z + SSD_W
OFF_DT = OFF_XBC + SSD_CD
OFF_G = OFF_DT + SSD_HEADS

ROW_TILE = 256
CONV_TILE = 128
MM_TILE = 1024
MM_TILE_WIDE = 2048
MM_TK = 2048
VMEM_CAP = 60 << 20

MESH = pl.DeviceIdType.MESH


def _dot(a, b):
    return lax.dot_general(a, b, (((1,), (0,)), ((), ())), preferred_element_type=F32)


def _dot_nt(a, b):
    return lax.dot_general(a, b, (((1,), (1,)), ((), ())), preferred_element_type=F32)


def _dot_tn(a, b):
    return lax.dot_general(a, b, (((0,), (0,)), ((), ())), preferred_element_type=F32)


def _bf(x):
    return x.astype(BF16)


def _split3(x):
    hi = x.astype(BF16)
    r = x - hi.astype(F32)
    mid = r.astype(BF16)
    lo = (r - mid.astype(F32)).astype(BF16)
    return hi, mid, lo


def _dot_exact_l(x, sel, pieces=3):
    hi, mid, lo = _split3(x)
    r = _dot(hi, sel) + _dot(mid, sel)
    return r + _dot(lo, sel) if pieces == 3 else r


def _dot_exact_r(sel, x):
    hi, mid, lo = _split3(x)
    return _dot(sel, hi) + _dot(sel, mid) + _dot(sel, lo)


def _sigmoid(x):
    return 1.0 / (1.0 + jnp.exp(-x))


def _softplus(x):
    return jnp.maximum(x, 0.0) + jnp.log(1.0 + jnp.exp(-jnp.abs(x)))


def _iota(shape, axis):
    return lax.broadcasted_iota(jnp.int32, shape, axis)


def _params(*sem):
    return pltpu.CompilerParams(dimension_semantics=sem)


class _Rider:
    def __init__(self, inputs, out_shapes, n_sems, phases):
        self.inputs, self.out_shapes, self.n_sems, self.phases = tuple(inputs), tuple(out_shapes), n_sems, phases

    def in_specs(self):
        return [pl.BlockSpec(memory_space=pl.ANY)] * len(self.inputs)

    def out_specs(self):
        return [pl.BlockSpec(memory_space=pl.ANY)] * len(self.out_shapes)

    def scratch(self):
        return [pltpu.SemaphoreType.DMA((self.n_sems,)), pltpu.SemaphoreType.DMA((self.n_sems,))]

    def run(self, step, n_steps, ins, outs, send_sems, recv_sems, last):
        for frac, fn in self.phases:
            if (frac >= 1.0) != last:
                continue
            at = min(int(frac * n_steps), n_steps - 1)

            @pl.when(step == at)
            def _(fn=fn):
                fn(ins, outs, send_sems, recv_sems)


def _mesh_pos():
    return lax.axis_index("x"), lax.axis_index("y"), lax.axis_index("c")


def _gather_rider(shards, relay_at, pass_at):
    n = len(shards)

    def tools(a, ins, outs, send_sems, recv_sems):
        x, y, cc = _mesh_pos()
        nbrs = [(1 - x, y), (x, 1 - y)]
        diag = (1 - x, 1 - y)
        relay_from, relay_to = (x ^ cc, y ^ (1 - cc)), (x ^ (1 - cc), y ^ cc)

        def slot(px, py, pc):
            return outs[a].at[4 * px + 2 * py + pc]

        def copy(k, block, to, src=None):
            return pltpu.make_async_remote_copy(
                src_ref=slot(*block) if src is None else src, dst_ref=slot(*block),
                send_sem=send_sems.at[8 * a + k], recv_sem=recv_sems.at[8 * a + k], device_id=to, device_id_type=MESH)

        me, sibling = (x, y, cc), (x, y, 1 - cc)
        return dict(
            mine=lambda: pltpu.make_async_copy(ins[a], slot(*me), send_sems.at[8 * a + 7]),
            first=lambda: [copy(0, me, sibling, src=ins[a])] + [copy(1 + j, me, (*chip, cc), src=ins[a])
                                                                for j, chip in enumerate(nbrs)],
            landed=lambda: [copy(1 + j, (*chip, cc), me) for j, chip in enumerate(nbrs)],
            relay=lambda: copy(3, (*relay_from, cc), (*relay_to, cc)),
            relayed_in=lambda: copy(3, (*diag, cc), me),
            passed=lambda: [copy(4 + j, (*chip, cc), sibling) for j, chip in enumerate(nbrs)],
            passed_diag=lambda: copy(6, (*diag, cc), sibling),
            from_sibling=lambda: [copy(0, sibling, me)] + [copy(4 + j, (*chip, 1 - cc), me)
                                                           for j, chip in enumerate(nbrs + [diag])])

    def start(*refs):
        for a in range(n):
            t = tools(a, *refs)
            t["mine"]().start()
            for cp in t["first"]():
                cp.start()

    def relay(*refs):
        for a in range(n):
            t = tools(a, *refs)
            for got in t["landed"]():
                got.wait_recv()
            t["relay"]().start()
            for cp in t["passed"]():
                cp.start()

    def pass_diag(*refs):
        for a in range(n):
            t = tools(a, *refs)
            t["relayed_in"]().wait_recv()
            t["passed_diag"]().start()

    def finish(*refs):
        for a in range(n):
            t = tools(a, *refs)
            for cp in t["from_sibling"]():
                cp.wait_recv()
            for cp in t["first"]() + [t["relay"]()] + t["passed"]() + [t["passed_diag"]()]:
                cp.wait_send()
            t["mine"]().wait()

    outs = [jax.ShapeDtypeStruct((N_DEV,) + s.shape, s.dtype) for s in shards]
    return _Rider(shards, outs, 8 * n, ((0.0, start), (relay_at, relay), (pass_at, pass_diag), (1.0, finish)))


def _scatter_rider(blocks):
    n = len(blocks)

    def copies(a, ins, outs, send_sems, recv_sems, with_back):
        x, y, cc = _mesh_pos()
        my_idx = 4 * x + 2 * y + cc
        mine = pltpu.make_async_copy(ins[a].at[my_idx], outs[a].at[my_idx], send_sems.at[8 * a + 7])
        out, back = [], []
        for k in range(1, N_DEV):
            px, py, pc = x ^ (k >> 2), y ^ ((k >> 1) & 1), cc ^ (k & 1)
            peer_idx = 4 * px + 2 * py + pc
            sems = dict(send_sem=send_sems.at[8 * a + k - 1], recv_sem=recv_sems.at[8 * a + k - 1],
                        device_id=(px, py, pc), device_id_type=MESH)
            out.append(pltpu.make_async_remote_copy(src_ref=ins[a].at[peer_idx], dst_ref=outs[a].at[my_idx], **sems))
            if with_back:
                back.append(pltpu.make_async_remote_copy(src_ref=ins[a].at[my_idx], dst_ref=outs[a].at[peer_idx], **sems))
        return mine, out, back

    def start(*refs):
        for a in range(n):
            mine, out, _ = copies(a, *refs, False)
            mine.start()
            for cp in out:
                cp.start()

    def finish(*refs):
        for a in range(n):
            mine, out, back = copies(a, *refs, True)
            for cp in back:
                cp.wait_recv()
            for cp in out:
                cp.wait_send()
            mine.wait()

    outs = [jax.ShapeDtypeStruct(b.shape, b.dtype) for b in blocks]
    return _Rider(blocks, outs, 8 * n, ((0.0, start), (1.0, finish)))


def _sibling_rider(blocks):
    _, r, c = blocks.shape

    def copies(ins, outs, send_sems, recv_sems):
        x, y, cc = _mesh_pos()
        return [pltpu.make_async_remote_copy(
            src_ref=ins[0].at[2 * j + 1 - cc], dst_ref=outs[0].at[j], send_sem=send_sems.at[j],
            recv_sem=recv_sems.at[j], device_id=(x, y, 1 - cc), device_id_type=MESH) for j in range(4)]

    def start(*refs):
        for cp in copies(*refs):
            cp.start()

    def finish(*refs):
        for cp in copies(*refs):
            cp.wait_recv()
        for cp in copies(*refs):
            cp.wait_send()

    return _Rider([blocks], [jax.ShapeDtypeStruct((4, r, c), blocks.dtype)], 4, ((0.0, start), (1.0, finish)))


def _chip_rider(partial, row0, rows):
    _, _, c = partial.shape

    def copies(ins, outs, send_sems, recv_sems, with_back):
        x, y, cc = _mesh_pos()
        my_chip = 2 * x + y

        def rows_of(j):
            return ins[0].at[j, pl.ds(row0, rows)]

        mine = pltpu.make_async_copy(rows_of(my_chip), outs[0].at[my_chip], send_sems.at[3])
        out, back = [], []
        for k in range(1, 4):
            px, py = x ^ (k >> 1), y ^ (k & 1)
            peer = 2 * px + py
            sems = dict(send_sem=send_sems.at[k - 1], recv_sem=recv_sems.at[k - 1], device_id=(px, py, cc),
                        device_id_type=MESH)
            out.append(pltpu.make_async_remote_copy(src_ref=rows_of(peer), dst_ref=outs[0].at[my_chip], **sems))
            if with_back:
                back.append(pltpu.make_async_remote_copy(src_ref=rows_of(my_chip), dst_ref=outs[0].at[peer], **sems))
        return mine, out, back

    def start(*refs):
        mine, out, _ = copies(*refs, False)
        mine.start()
        for cp in out:
            cp.start()

    def finish(*refs):
        mine, out, back = copies(*refs, True)
        for cp in back:
            cp.wait_recv()
        for cp in out:
            cp.wait_send()
        mine.wait()

    return _Rider([partial], [jax.ShapeDtypeStruct((4, rows, c), partial.dtype)], 4, ((0.0, start), (1.0, finish)))


def _matmul(a, b, mode, out_dtype, name, add=None, rider=None, b_rows=None, extra=None):
    b_shape = b.shape if b_rows is None else (b_rows[1], b.shape[1])
    if mode == "nn":
        (m, k), (k2, n) = a.shape, b_shape
    elif mode == "nt":
        (m, k), (n, k2) = a.shape, b_shape
    else:
        (k, m), (k2, n) = a.shape, b_shape
    assert k == k2, (a.shape, b_shape, mode)
    tm, tn, tk = min(m, MM_TILE), min(n, MM_TILE), min(k, MM_TK)
    if n % MM_TILE_WIDE == 0 and (k <= MM_TK or (add is None and jnp.dtype(out_dtype).itemsize == 2)):
        tn = MM_TILE_WIDE
    assert m % tm == 0 and n % tn == 0 and k % tk == 0, (m, n, k)
    nk = k // tk
    b_tile_rows = tn if mode == "nt" else tk
    b_off = 0 if b_rows is None else b_rows[0] // b_tile_rows
    assert b_rows is None or b_rows[0] % b_tile_rows == 0, (b_rows, b_tile_rows)
    dot = {"nn": _dot, "nt": _dot_nt, "tn": _dot_tn}[mode]

    gm, gn = m // tm, n // tn
    n_rin = len(rider.inputs) if rider else 0
    n_rout = len(rider.out_shapes) if rider else 0
    n_add = 1 if add is not None else 0

    n_extra = 2 if extra is not None else 0

    def body(*refs):
        a_ref, b_ref = refs[:2]
        add_ref = refs[2] if n_add else None
        extra_refs = refs[2 + n_add:2 + n_add + n_extra]
        first = 2 + n_add + n_extra
        r_ins = refs[first:first + n_rin]
        o_ref = refs[first + n_rin]
        r_outs = refs[first + n_rin + 1:first + n_rin + 1 + n_rout]
        scratch = refs[first + n_rin + 1 + n_rout:]
        acc_ref = scratch[0] if nk > 1 else None
        kk = pl.program_id(2)
        if rider:
            step = (pl.program_id(0) * gn + pl.program_id(1)) * nk + kk
            ride = functools.partial(rider.run, step, gm * gn * nk, r_ins, r_outs, scratch[-2], scratch[-1])
            ride(last=False)

        def finish(r):
            if extra_refs:
                r = r + _dot(_bf(extra_refs[0][...]), _bf(extra_refs[1][...]))
            if add_ref is not None:
                r = r + add_ref[...]
            o_ref[...] = r.astype(o_ref.dtype)

        if nk == 1:
            finish(dot(_bf(a_ref[...]), _bf(b_ref[...])))
        else:
            @pl.when(kk == 0)
            def _():
                acc_ref[...] = dot(_bf(a_ref[...]), _bf(b_ref[...]))

            @pl.when(jnp.logical_and(kk > 0, kk < nk - 1))
            def _():
                acc_ref[...] += dot(_bf(a_ref[...]), _bf(b_ref[...]))

            @pl.when(kk == nk - 1)
            def _():
                finish(acc_ref[...] + dot(_bf(a_ref[...]), _bf(b_ref[...])))
        if rider:
            ride(last=True)

    if mode == "nn":
        a_spec = pl.BlockSpec((tm, tk), lambda i, j, kk: (i, kk))
        b_spec = pl.BlockSpec((tk, tn), lambda i, j, kk: (kk + b_off, j))
    elif mode == "nt":
        a_spec = pl.BlockSpec((tm, tk), lambda i, j, kk: (i, kk))
        b_spec = pl.BlockSpec((tn, tk), lambda i, j, kk: (j + b_off, kk))
    else:
        a_spec = pl.BlockSpec((tk, tm), lambda i, j, kk: (kk, i))
        b_spec = pl.BlockSpec((tk, tn), lambda i, j, kk: (kk + b_off, j))
    o_spec = pl.BlockSpec((tm, tn), lambda i, j, kk: (i, j))
    in_specs = [a_spec, b_spec] + ([o_spec] if add is not None else [])
    args = (a, b) + ((add,) if add is not None else ())
    if extra is not None:
        (m2, k_extra), (k_extra2, n2) = extra[0].shape, extra[1].shape
        assert (m2, n2, k_extra) == (m, n, k_extra2), (extra[0].shape, extra[1].shape)
        in_specs += [pl.BlockSpec((tm, k_extra), lambda i, j, kk: (i, 0)),
                     pl.BlockSpec((k_extra, tn), lambda i, j, kk: (0, j))]
        args += tuple(extra)
    block_bytes = (tm * tk * a.dtype.itemsize + tk * tn * b.dtype.itemsize
                   + tm * tn * (jnp.dtype(out_dtype).itemsize + (4 if add is not None else 0)))
    vmem = min(2 * block_bytes + 2 * tm * tn * 4 + 2 * (tm + tn) * tk + (4 << 20), VMEM_CAP)
    out_shape = jax.ShapeDtypeStruct((m, n), out_dtype)
    scratch = [pltpu.VMEM((tm, tn), F32)] if nk > 1 else []
    if rider:
        in_specs = in_specs + rider.in_specs()
        args = args + rider.inputs
        out_shape = (out_shape, *rider.out_shapes)
        o_spec = (o_spec, *rider.out_specs())
        scratch = scratch + rider.scratch()
    sem = ("arbitrary",) * 3 if rider else ("parallel", "parallel", "arbitrary")
    return pl.pallas_call(
        body,
        out_shape=out_shape,
        grid=(gm, gn, nk),
        in_specs=in_specs,
        out_specs=o_spec,
        scratch_shapes=scratch,
        compiler_params=pltpu.CompilerParams(dimension_semantics=sem, vmem_limit_bytes=int(vmem)),
        name=name,
    )(*args)


def _row_spec(width, tile=ROW_TILE):
    return pl.BlockSpec((tile, width), lambda i: (i, 0))


def _full_spec(shape):
    nd = len(shape)
    return pl.BlockSpec(shape, lambda *_: (0,) * nd)


def _tiling_2d(r, c):
    if r <= ROW_TILE or r % ROW_TILE == 0:
        tr = min(r, ROW_TILE)
        return (tr, c), r // tr, (lambda i: (i, 0))
    tc = 128 if r > 4096 else 256
    assert c % tc == 0, (r, c)
    return (r, tc), c // tc, (lambda i: (0, i))


def _spec_2d(r, c):
    blk, grid, idx = _tiling_2d(r, c)
    return pl.BlockSpec(blk, idx), grid


def _cast_bf16(x, name):
    r, c = x.shape
    sp, grid = _spec_2d(r, c)

    def body(x_ref, o_ref):
        o_ref[...] = _bf(x_ref[...])

    return pl.pallas_call(
        body, out_shape=jax.ShapeDtypeStruct((r, c), BF16), grid=(grid,), in_specs=[sp], out_specs=sp,
        compiler_params=_params("parallel"), name=name)(x)


def _rmsnorm_rope_fwd(x, w, pos_col, inv_freq, to_bf16, name, rider):
    s, d = x.shape
    steps = s // ROW_TILE
    n_cast, n_rin, n_rout = len(to_bf16), len(rider.inputs), len(rider.out_shapes)

    def body(x_ref, w_ref, p_ref, f_ref, *refs):
        c_ins, refs = refs[:n_cast], refs[n_cast:]
        r_ins, refs = refs[:n_rin], refs[n_rin:]
        (o_ref, cos_ref, sin_ref), refs = refs[:3], refs[3:]
        c_outs, refs = refs[:n_cast], refs[n_cast:]
        ride = functools.partial(rider.run, pl.program_id(0), steps, r_ins, refs[:n_rout], refs[-2], refs[-1])
        ride(last=False)

        @pl.when(pl.program_id(0) == 0)
        def _():
            for c_in, c_out in zip(c_ins, c_outs):
                c_out[...] = _bf(c_in[...])

        xv = x_ref[...]
        rstd = lax.rsqrt(jnp.mean(xv * xv, axis=1, keepdims=True) + EPS)
        o_ref[...] = _bf(xv * rstd * w_ref[...])
        ang = p_ref[...].astype(F32) * f_ref[...]
        cos_ref[...] = jnp.cos(ang)
        sin_ref[...] = jnp.sin(ang)
        ride(last=True)

    table = jax.ShapeDtypeStruct((s, ROPE_HALF), F32)
    whole = [_full_spec(a.shape) for a in to_bf16]
    return pl.pallas_call(
        body,
        out_shape=(jax.ShapeDtypeStruct((s, d), BF16), table, table,
                   *[jax.ShapeDtypeStruct(a.shape, BF16) for a in to_bf16], *rider.out_shapes),
        grid=(steps,),
        in_specs=[_row_spec(d), _full_spec((1, d)), _row_spec(1), _full_spec((1, ROPE_HALF)), *whole,
                  *rider.in_specs()],
        out_specs=(_row_spec(d), _row_spec(ROPE_HALF), _row_spec(ROPE_HALF), *whole, *rider.out_specs()),
        scratch_shapes=rider.scratch(),
        compiler_params=_params("arbitrary"), name=name)(x, w, pos_col, inv_freq, *to_bf16, *rider.inputs)


def _rmsnorm_bwd(x, w, dh, dres, name):
    s, d = x.shape

    def body(x_ref, w_ref, dh_ref, dres_ref, dx_ref, dw_ref):
        @pl.when(pl.program_id(0) == 0)
        def _():
            dw_ref[...] = jnp.zeros_like(dw_ref)

        xv = x_ref[...]
        rstd = lax.rsqrt(jnp.mean(xv * xv, axis=1, keepdims=True) + EPS)
        xhat = xv * rstd
        dhv = dh_ref[...]
        dxhat = dhv * w_ref[...]
        dx = rstd * (dxhat - xhat * jnp.mean(dxhat * xhat, axis=1, keepdims=True))
        dx_ref[...] = dx + dres_ref[...]
        dw_ref[...] += jnp.sum(dhv * xhat, axis=0, keepdims=True)

    return pl.pallas_call(
        body,
        out_shape=(jax.ShapeDtypeStruct((s, d), F32), jax.ShapeDtypeStruct((1, d), F32)),
        grid=(s // ROW_TILE,),
        in_specs=[_row_spec(d), _full_spec((1, d)), _row_spec(d), _row_spec(d)],
        out_specs=(_row_spec(d), _full_spec((1, d))),
        compiler_params=_params("arbitrary"), name=name)(x, w, dh, dres)


def _merge_fwd(pg, p_r, p_s, name):
    s = pg.shape[0]

    def body(g_ref, r_ref, s_ref, o_ref):
        g = g_ref[...].astype(F32)
        o_ref[...] = _bf(_sigmoid(g[:, :D_MODEL]) * r_ref[...].astype(F32)
                         + _sigmoid(g[:, D_MODEL:]) * s_ref[...].astype(F32))

    return pl.pallas_call(
        body, out_shape=jax.ShapeDtypeStruct((s, D_MODEL), BF16), grid=(s // ROW_TILE,),
        in_specs=[_row_spec(2 * D_MODEL), _row_spec(D_MODEL), _row_spec(D_MODEL)],
        out_specs=_row_spec(D_MODEL), compiler_params=_params("parallel"), name=name)(pg, p_r, p_s)


def _merge_bwd(pg, p_r, p_s, dm, name):
    s = pg.shape[0]

    def body(g_ref, r_ref, s_ref, dm_ref, dr_ref, ds_ref, dg_ref):
        g = g_ref[...].astype(F32)
        sr, ss = _sigmoid(g[:, :D_MODEL]), _sigmoid(g[:, D_MODEL:])
        d = dm_ref[...]
        dr_ref[...] = _bf(d * sr)
        ds_ref[...] = _bf(d * ss)
        dg_ref[:, :D_MODEL] = _bf(d * r_ref[...].astype(F32) * sr * (1.0 - sr))
        dg_ref[:, D_MODEL:] = _bf(d * s_ref[...].astype(F32) * ss * (1.0 - ss))

    o = jax.ShapeDtypeStruct((s, D_MODEL), BF16)
    return pl.pallas_call(
        body, out_shape=(o, o, jax.ShapeDtypeStruct((s, 2 * D_MODEL), BF16)), grid=(s // ROW_TILE,),
        in_specs=[_row_spec(2 * D_MODEL), _row_spec(D_MODEL), _row_spec(D_MODEL), _row_spec(D_MODEL)],
        out_specs=(_row_spec(D_MODEL), _row_spec(D_MODEL), _row_spec(2 * D_MODEL)),
        compiler_params=_params("parallel"), name=name)(pg, p_r, p_s, dm)


def _final_fwd_bwd(x, o, w, target, name):
    s, d = x.shape

    def body(x_ref, o_ref, w_ref, t_ref, dx_ref, dw_ref, loss_ref):
        @pl.when(pl.program_id(0) == 0)
        def _():
            dw_ref[...] = jnp.zeros_like(dw_ref)
            loss_ref[...] = jnp.zeros_like(loss_ref)

        x2 = x_ref[...] + o_ref[...]
        rstd = lax.rsqrt(jnp.mean(x2 * x2, axis=1, keepdims=True) + EPS)
        xhat = x2 * rstd
        wv = w_ref[...]
        err = xhat * wv - t_ref[...]
        loss_ref[...] += jnp.sum(jnp.sum(err * err, axis=1, keepdims=True), axis=0, keepdims=True) * (0.5 / d)
        dy = err * (1.0 / d)
        dw_ref[...] += jnp.sum(dy * xhat, axis=0, keepdims=True)
        dxhat = dy * wv
        dx_ref[...] = rstd * (dxhat - xhat * jnp.mean(dxhat * xhat, axis=1, keepdims=True))

    return pl.pallas_call(
        body,
        out_shape=(jax.ShapeDtypeStruct((s, d), F32), jax.ShapeDtypeStruct((1, d), F32),
                   jax.ShapeDtypeStruct((8, 128), F32)),
        grid=(s // ROW_TILE,),
        in_specs=[_row_spec(d), _row_spec(d), _full_spec((1, d)), _row_spec(d)],
        out_specs=(_row_spec(d), _full_spec((1, d)), _full_spec((8, 128))),
        compiler_params=_params("arbitrary"), name=name)(x, o, w, target)


def _adamw(w, g, m, v, name):
    r, c = w.shape
    sp, grid = _spec_2d(r, c)
    c1 = 1.0 / (1.0 - ADAM_B1 ** ADAM_STEP)
    c2 = 1.0 / (1.0 - ADAM_B2 ** ADAM_STEP)

    def body(w_ref, g_ref, m_ref, v_ref, d_ref, nm_ref, nv_ref):
        gv = g_ref[...]
        nm = ADAM_B1 * m_ref[...] + (1.0 - ADAM_B1) * gv
        nv = ADAM_B2 * v_ref[...] + (1.0 - ADAM_B2) * (gv * gv)
        d_ref[...] = -ADAM_LR * ((nm * c1) / (jnp.sqrt(nv * c2) + ADAM_EPS) + ADAM_WD * w_ref[...])
        nm_ref[...] = nm
        nv_ref[...] = nv

    o = jax.ShapeDtypeStruct((r, c), F32)
    return pl.pallas_call(
        body, out_shape=(o, o, o), grid=(grid,), in_specs=[sp, sp, sp, sp], out_specs=(sp, sp, sp),
        compiler_params=_params("parallel"), name=name)(w, g, m, v)


def _sum_slots(land, name):
    n, r, c = land.shape
    blk, grid, idx = _tiling_2d(r, c)

    def body(l_ref, o_ref):
        acc = l_ref[0].astype(F32)
        for i in range(1, n):
            acc = acc + l_ref[i].astype(F32)
        o_ref[...] = acc

    return pl.pallas_call(
        body, out_shape=jax.ShapeDtypeStruct((r, c), F32), grid=(grid,),
        in_specs=[pl.BlockSpec((n, *blk), lambda i: (0, *idx(i)))], out_specs=pl.BlockSpec(blk, idx),
        compiler_params=_params("parallel"), name=name)(land)


SLAB_COLS = 256


def _chip_sum(blocks, from_sibling, my_core, name):
    _, r, c = blocks.shape

    def body(core_ref, a_ref, b_ref, o_ref):
        o_ref[...] = _bf(a_ref[...].astype(F32) + b_ref[...].astype(F32))

    def slab(chip_of):
        return pl.BlockSpec((1, r, SLAB_COLS), lambda j, i, core: (chip_of(j, core), 0, i))

    grid_spec = pltpu.PrefetchScalarGridSpec(
        num_scalar_prefetch=1, grid=(4, c // SLAB_COLS),
        in_specs=[slab(lambda j, core: 2 * j + core[0]), slab(lambda j, core: j)],
        out_specs=slab(lambda j, core: j))
    return pl.pallas_call(
        body, out_shape=jax.ShapeDtypeStruct((4, r, c), BF16), grid_spec=grid_spec,
        compiler_params=_params("parallel", "parallel"), name=name,
    )(my_core.astype(jnp.int32).reshape(1), blocks, from_sibling)


def _sum_row_parts(parts, name):
    c = parts[0].shape[2]
    rows = [p.shape[1] for p in parts]

    def body(*refs):
        o_ref, off = refs[-1], 0
        for p_ref, n in zip(refs[:-1], rows):
            acc = p_ref[0].astype(F32)
            for i in range(1, p_ref.shape[0]):
                acc = acc + p_ref[i].astype(F32)
            o_ref[off:off + n, :] = acc
            off += n

    return pl.pallas_call(
        body, out_shape=jax.ShapeDtypeStruct((sum(rows), c), F32), grid=(c // SLAB_COLS,),
        in_specs=[pl.BlockSpec((p.shape[0], p.shape[1], SLAB_COLS), lambda i: (0, 0, i)) for p in parts],
        out_specs=pl.BlockSpec((sum(rows), SLAB_COLS), lambda i: (0, i)),
        compiler_params=_params("parallel"), name=name)(*parts)


def _retention_tables():
    lg = np.log1p(-(2.0 ** (-5.0 - np.arange(RET_HEADS, dtype=np.float64))))
    idx = np.arange(CHUNK, dtype=np.float64)
    intra = np.exp(np.abs(idx[:, None] - idx[None, :])[None] * lg[:, None, None])
    qd = np.exp((idx[None, :] + 1.0) * lg[:, None])
    kd = np.exp((CHUNK - 1.0 - idx[None, :]) * lg[:, None])
    cd = np.exp(CHUNK * lg)
    ones = np.ones((1, 1, RET_DK))
    return (jnp.asarray(intra, F32), jnp.asarray(qd[:, :, None] * ones, F32),
            jnp.asarray(kd[:, :, None] * ones, F32), jnp.asarray(cd[:, None, None] * ones, F32))


def _rope(t, cos, sin):
    t1, t2 = t[:, :ROPE_HALF], t[:, ROPE_HALF:]
    return jnp.concatenate([t1 * cos - t2 * sin, t2 * cos + t1 * sin], axis=1)


def _rope_t(d, cos, sin):
    d1, d2 = d[:, :ROPE_HALF], d[:, ROPE_HALF:]
    return jnp.concatenate([d1 * cos + d2 * sin, d2 * cos - d1 * sin], axis=1)


_HEADS = range(RET_HEADS)


def _ret_chunk_fwd(p_ref, cos, sin, intra_ref, qd_ref, st):
    def seg(i, h):
        return p_ref[:, i * RET_W + h * RET_DK:i * RET_W + (h + 1) * RET_DK]

    v = [seg(2, h) for h in _HEADS]
    qr = [_rope(seg(0, h), cos, sin) for h in _HEADS]
    kr = [_rope(seg(1, h), cos, sin) * (RET_DK ** -0.5) for h in _HEADS]
    qrb, vb = [_bf(a) for a in qr], [_bf(a) for a in v]
    sc = [_dot_nt(qrb[h], _bf(kr[h])) * intra_ref[h] for h in _HEADS]
    qs = [_dot(qrb[h], _bf(st[h])) for h in _HEADS]
    y = [_dot(_bf(sc[h]), vb[h]) + qs[h] * qd_ref[h] for h in _HEADS]
    g = [seg(3, h) for h in _HEADS]
    return dict(v=v, vb=vb, qr=qr, qrb=qrb, kr=kr, sc=sc, y=y, g=g)


def _group_norm(y):
    mu = jnp.mean(y, axis=1, keepdims=True)
    yc = y - mu
    rstd = lax.rsqrt(jnp.mean(yc * yc, axis=1, keepdims=True) + EPS)
    return yc * rstd, rstd


def _chunk_rows(sub):
    return pl.ds(sub * CHUNK, CHUNK)


def _ret_specs(steps, rev):
    cidx = (lambda c: steps - 1 - c) if rev else (lambda c: c)
    return dict(
        proj=pl.BlockSpec((RET_STEP_ROWS, 4 * RET_W), lambda c: (cidx(c), 0)),
        half=pl.BlockSpec((RET_STEP_ROWS, ROPE_HALF), lambda c: (cidx(c), 0)),
        wide=pl.BlockSpec((RET_STEP_ROWS, RET_W), lambda c: (cidx(c), 0)),
        state=pl.BlockSpec((RET_CHUNKS_PER_STEP, RET_HEADS, RET_DK, RET_DK), lambda c: (cidx(c), 0, 0, 0)),
        intra=_full_spec((RET_HEADS, CHUNK, CHUNK)),
        dec=_full_spec((RET_HEADS, CHUNK, RET_DK)),
        cd=_full_spec((RET_HEADS, 1, RET_DK)),
    )


def _retention_fwd(p_r, cos, sin, name):
    s = p_r.shape[0]
    nc = s // CHUNK
    intra_t, qd_t, kd_t, cd_t = _retention_tables()

    def body(p_ref, cos_ref, sin_ref, intra_ref, qd_ref, kd_ref, cd_ref, y_ref, st_ref, state):
        @pl.when(pl.program_id(0) == 0)
        def _():
            state[...] = jnp.zeros_like(state)

        for sub in range(RET_CHUNKS_PER_STEP):
            rows = _chunk_rows(sub)
            y_v, st_v = y_ref.at[rows], st_ref.at[sub]
            st = [state[h] for h in _HEADS]
            f = _ret_chunk_fwd(p_ref.at[rows], cos_ref[rows, :], sin_ref[rows, :], intra_ref, qd_ref, st)
            new_st = [st[h] * cd_ref[h] + _dot_tn(_bf(f["kr"][h] * kd_ref[h]), f["vb"][h]) for h in _HEADS]
            out = [_bf(_group_norm(f["y"][h])[0] * (f["g"][h] * _sigmoid(f["g"][h]))) for h in _HEADS]
            for h in _HEADS:
                st_v[h] = _bf(st[h])
                state[h] = new_st[h]
                y_v[:, h * RET_DK:(h + 1) * RET_DK] = out[h]

    steps = nc // RET_CHUNKS_PER_STEP
    sp = _ret_specs(steps, False)
    return pl.pallas_call(
        body,
        out_shape=(jax.ShapeDtypeStruct((s, RET_W), BF16),
                   jax.ShapeDtypeStruct((nc, RET_HEADS, RET_DK, RET_DK), BF16)),
        grid=(steps,),
        in_specs=[sp["proj"], sp["half"], sp["half"], sp["intra"], sp["dec"], sp["dec"], sp["cd"]],
        out_specs=(sp["wide"], sp["state"]),
        scratch_shapes=[pltpu.VMEM((RET_HEADS, RET_DK, RET_DK), F32)],
        compiler_params=_params("arbitrary"), name=name)(p_r, cos, sin, intra_t, qd_t, kd_t, cd_t)


def _retention_bwd(p_r, cos, sin, states, dy_r, name):
    s = p_r.shape[0]
    nc = s // CHUNK
    intra_t, qd_t, kd_t, cd_t = _retention_tables()

    def body(p_ref, cos_ref, sin_ref, intra_ref, qd_ref, kd_ref, cd_ref, st_ref, dy_ref, dp_ref, dstate):
        @pl.when(pl.program_id(0) == 0)
        def _():
            dstate[...] = jnp.zeros_like(dstate)

        for sub in reversed(range(RET_CHUNKS_PER_STEP)):
            rows = _chunk_rows(sub)
            chunk(p_ref.at[rows], cos_ref[rows, :], sin_ref[rows, :], intra_ref, qd_ref, kd_ref, cd_ref,
                  st_ref.at[sub], dy_ref.at[rows], dp_ref.at[rows], dstate)

    def chunk(p_ref, cos, sin, intra_ref, qd_ref, kd_ref, cd_ref, st_ref, dy_ref, dp_ref, dstate):
        st = [st_ref[h] for h in _HEADS]
        dsn = [dstate[h] for h in _HEADS]
        f = _ret_chunk_fwd(p_ref, cos, sin, intra_ref, qd_ref, st)
        vb, qrb, kr, g = f["vb"], f["qrb"], f["kr"], f["g"]
        norm = [_group_norm(f["y"][h]) for h in _HEADS]
        sg = [_sigmoid(g[h]) for h in _HEADS]
        dyr = [dy_ref[:, h * RET_DK:(h + 1) * RET_DK] for h in _HEADS]
        dyn = [dyr[h] * (g[h] * sg[h]) for h in _HEADS]
        dg = [dyr[h] * norm[h][0] * (sg[h] * (1.0 + g[h] * (1.0 - sg[h]))) for h in _HEADS]
        dy = [norm[h][1] * (dyn[h] - jnp.mean(dyn[h], axis=1, keepdims=True)
                            - norm[h][0] * jnp.mean(dyn[h] * norm[h][0], axis=1, keepdims=True)) for h in _HEADS]
        dyb, dsnb = [_bf(a) for a in dy], [_bf(a) for a in dsn]
        ds = [_bf(_dot_nt(dyb[h], vb[h]) * intra_ref[h]) for h in _HEADS]
        t = [_bf(dy[h] * qd_ref[h]) for h in _HEADS]
        dv = [_dot_tn(_bf(f["sc"][h]), dyb[h]) + _dot(_bf(kr[h] * kd_ref[h]), dsnb[h]) for h in _HEADS]
        dqr = [_dot(ds[h], _bf(kr[h])) + _dot_nt(t[h], _bf(st[h])) for h in _HEADS]
        dkr = [_dot_tn(ds[h], qrb[h]) + _dot_nt(vb[h], dsnb[h]) * kd_ref[h] for h in _HEADS]
        new_ds = [dsn[h] * cd_ref[h] + _dot_tn(qrb[h], t[h]) for h in _HEADS]
        for h in _HEADS:
            lo = h * RET_DK
            dstate[h] = new_ds[h]
            dp_ref[:, lo:lo + RET_DK] = _bf(_rope_t(dqr[h], cos, sin))
            dp_ref[:, RET_W + lo:RET_W + lo + RET_DK] = _bf(_rope_t(dkr[h], cos, sin) * (RET_DK ** -0.5))
            dp_ref[:, 2 * RET_W + lo:2 * RET_W + lo + RET_DK] = _bf(dv[h])
            dp_ref[:, 3 * RET_W + lo:3 * RET_W + lo + RET_DK] = _bf(dg[h])

    steps = nc // RET_CHUNKS_PER_STEP
    sp = _ret_specs(steps, True)
    return pl.pallas_call(
        body,
        out_shape=jax.ShapeDtypeStruct((s, 4 * RET_W), BF16),
        grid=(steps,),
        in_specs=[sp["proj"], sp["half"], sp["half"], sp["intra"], sp["dec"], sp["dec"], sp["cd"],
                  sp["state"], sp["wide"]],
        out_specs=sp["proj"],
        scratch_shapes=[pltpu.VMEM((RET_HEADS, RET_DK, RET_DK), F32)],
        compiler_params=_params("arbitrary"), name=name)(p_r, cos, sin, intra_t, qd_t, kd_t, cd_t, states, dy_r)


CONV_SLAB = 256
_CONV_SLABS = [slice(i * CONV_SLAB, (i + 1) * CONV_SLAB) for i in range(SSD_CD // CONV_SLAB)]


def _conv_taps(ext, w):
    acc = w[SSD_CONV - 1:SSD_CONV] * ext
    for j in range(SSD_CONV - 1):
        acc = acc + w[j:j + 1] * pltpu.roll(ext, SSD_CONV - 1 - j, axis=0)
    return acc


def _conv_fwd(xbc_raw, conv_w, conv_b, name):
    s = xbc_raw.shape[0]
    t8 = CONV_TILE // 8

    def body(cur_ref, prev_ref, w_ref, b_ref, o_ref, slope_ref):
        keep = (pl.program_id(0) > 0).astype(F32)
        for sl in _CONV_SLABS:
            ext = jnp.concatenate([prev_ref[:, sl] * keep, cur_ref[:, sl]], axis=0)
            u = _conv_taps(ext, w_ref[:, sl])[8:] + b_ref[:, sl]
            sg = _sigmoid(u)
            o_ref[:, sl] = u * sg
            slope_ref[:, sl] = _bf(sg * (1.0 + u * (1.0 - sg)))

    out = jax.ShapeDtypeStruct((s, SSD_CD), F32)
    return pl.pallas_call(
        body, out_shape=(out, jax.ShapeDtypeStruct((s, SSD_CD), BF16)), grid=(s // CONV_TILE,),
        in_specs=[_row_spec(SSD_CD, CONV_TILE),
                  pl.BlockSpec((8, SSD_CD), lambda i: (jnp.maximum(i * t8 - 1, 0), 0)),
                  _full_spec((SSD_CONV, SSD_CD)), _full_spec((1, SSD_CD))],
        out_specs=(_row_spec(SSD_CD, CONV_TILE), _row_spec(SSD_CD, CONV_TILE)),
        compiler_params=_params("parallel"), name=name)(xbc_raw, xbc_raw, conv_w, conv_b)


def _conv_bwd(xbc_raw, conv_w, slope, dact, name):
    s = xbc_raw.shape[0]
    nt = s // CONV_TILE
    t8 = CONV_TILE // 8
    rows = CONV_TILE + 8

    def body(cur_ref, w_ref, s_ref, snext_ref, d_ref, dnext_ref, dx_ref, dw_ref, db_ref):
        i = pl.program_id(0)

        @pl.when(i == 0)
        def _():
            dw_ref[...] = jnp.zeros_like(dw_ref)
            db_ref[...] = jnp.zeros_like(db_ref)

        keep_next = (i < nt - 1).astype(F32)
        for sl in _CONV_SLABS:
            w = w_ref[:, sl]
            cur = cur_ref[:, sl]
            duc = d_ref[:, sl].astype(F32) * s_ref[:, sl].astype(F32)
            after = (dnext_ref[:, sl].astype(F32) * snext_ref[:, sl].astype(F32))[:8] * keep_next
            du = jnp.concatenate([duc, after], axis=0)
            dx = w[SSD_CONV - 1:SSD_CONV] * duc
            for j in range(SSD_CONV - 1):
                ahead = pltpu.roll(du, rows - (SSD_CONV - 1 - j), axis=0)[:CONV_TILE]
                dx = dx + w[j:j + 1] * ahead
                dw_ref[j:j + 1, sl] += jnp.sum(cur * ahead, axis=0, keepdims=True)
            dx_ref[:, sl] = _bf(dx)
            dw_ref[SSD_CONV - 1:SSD_CONV, sl] += jnp.sum(cur * duc, axis=0, keepdims=True)
            db_ref[:, sl] += jnp.sum(duc, axis=0, keepdims=True)

    row = _row_spec(SSD_CD, CONV_TILE)
    nxt16 = pl.BlockSpec((16, SSD_CD), lambda i: (jnp.minimum((i + 1) * (t8 // 2), s // 16 - 1), 0))
    return pl.pallas_call(
        body,
        out_shape=(jax.ShapeDtypeStruct((s, SSD_CD), BF16), jax.ShapeDtypeStruct((SSD_CONV, SSD_CD), F32),
                   jax.ShapeDtypeStruct((1, SSD_CD), F32)),
        grid=(nt,),
        in_specs=[row, _full_spec((SSD_CONV, SSD_CD)), row, nxt16, row, nxt16],
        out_specs=(row, _full_spec((SSD_CONV, SSD_CD)), _full_spec((1, SSD_CD))),
        compiler_params=_params("arbitrary"), name=name)(xbc_raw, conv_w, slope, slope, dact, dact)


def _head_select():
    e = np.zeros((HEAD_PAD, SSD_W), np.float32)
    for h in range(SSD_HEADS):
        e[h, h * SSD_P:(h + 1) * SSD_P] = 1.0
    return jnp.asarray(e, BF16), jnp.asarray(e.T, BF16)


def _pad_heads(v):
    return jnp.pad(v.reshape(1, SSD_HEADS).astype(F32), ((0, 0), (0, HEAD_PAD - SSD_HEADS)))


def _ssd_masks():
    r = _iota((CHUNK, SSD_QW), 0)
    c = _iota((CHUNK, SSD_QW), 1) % SSD_P
    itile = (r == c).astype(F32)
    ctile = r >= c
    rb = _iota((SSD_QW, SSD_QW), 0) // SSD_P
    cb = _iota((SSD_QW, SSD_QW), 1) // SSD_P
    return itile, ctile, rb == cb


_TILES = [slice(i * SSD_QW, (i + 1) * SSD_QW) for i in range(SSD_W // SSD_QW)]


def _ssd_heads(dtr_ref, dtb_ref, alog_ref):
    u = dtr_ref[...] + dtb_ref[...]
    dt = _softplus(u)
    nexp = -jnp.exp(alog_ref[...])
    return u, dt, nexp, dt * nexp


def _ssd_group(xbc_ref, acol_ref, dte_ref, gs):
    acol = acol_ref[:, gs]
    alast = acol[CHUNK - 1:CHUNK, :]
    xs, dte = xbc_ref[:, gs], dte_ref[:, gs]
    return dict(xs=xs, dte=dte, xdt=xs * dte, ea=jnp.exp(acol), tail=jnp.exp(alast - acol), eal=jnp.exp(alast))


def _silu_gate(z_ref, y_ref, tl):
    zv = z_ref[:, tl]
    sz = _sigmoid(zv)
    return zv, sz, y_ref[:, tl] * (zv * sz)


def _tile4(x):
    return jnp.concatenate([x, x, x, x], axis=0)


def _fold4(x):
    return x[0:CHUNK] + x[CHUNK:2 * CHUNK] + x[2 * CHUNK:3 * CHUNK] + x[3 * CHUNK:4 * CHUNK]


def _ssd_specs(steps, rev):
    cidx = (lambda c: steps - 1 - c) if rev else (lambda c: c)
    return dict(
        xbc=pl.BlockSpec((STEP_ROWS, SSD_CD), lambda c: (cidx(c), 0)),
        dt=pl.BlockSpec((STEP_ROWS, HEAD_PAD), lambda c: (cidx(c), 0)),
        wide=pl.BlockSpec((STEP_ROWS, SSD_W), lambda c: (cidx(c), 0)),
        state=pl.BlockSpec((CHUNKS_PER_STEP, SSD_N, SSD_W), lambda c: (cidx(c), 0, 0)),
        head=_full_spec((1, HEAD_PAD)),
        roww=_full_spec((1, SSD_W)),
        e=_full_spec((HEAD_PAD, SSD_W)),
        et=_full_spec((SSD_W, HEAD_PAD)),
    )


_GROUPS = range(SSD_GROUPS)
_GROUP_LANES = [slice(g * SSD_GW, (g + 1) * SSD_GW) for g in _GROUPS]
_QUADS = [(g, slice(g * SSD_GW + q * SSD_QW, g * SSD_GW + (q + 1) * SSD_QW), slice(q * SSD_QW, (q + 1) * SSD_QW))
          for g in _GROUPS for q in range(SSD_GW // SSD_QW)]


def _ssd_bc(xbc_ref):
    bg = [_bf(xbc_ref[:, B_OFF + g * SSD_N:B_OFF + (g + 1) * SSD_N]) for g in _GROUPS]
    cg = [_bf(xbc_ref[:, C_OFF + g * SSD_N:C_OFF + (g + 1) * SSD_N]) for g in _GROUPS]
    return bg, cg


def _ssd_decay(aq, itile, ctile):
    arow = jnp.sum(aq * itile, axis=0, keepdims=True)
    return jnp.exp(jnp.where(ctile, aq - arow, -jnp.inf))


def _ssd_blockdiag(xq, bdmask):
    return jnp.where(bdmask, _tile4(_bf(xq)), jnp.zeros((), BF16))


def _ssd_fwd(xbc, dt_raw, z, dt_bias, a_log, dske, norm_w, name):
    s = xbc.shape[0]
    nc = s // CHUNK
    e_sel, _ = _head_select()

    def body(xbc_ref, dtr_ref, z_ref, dtb_ref, alog_ref, dske_ref, nw_ref, e_ref,
             yraw_ref, ys_ref, st_ref, acol_ref, dte_ref, state):
        @pl.when(pl.program_id(0) == 0)
        def _():
            state[...] = jnp.zeros_like(state)

        for sub in range(CHUNKS_PER_STEP):
            rows = _chunk_rows(sub)
            chunk(xbc_ref.at[rows], dtr_ref.at[rows], z_ref.at[rows], dtb_ref, alog_ref, dske_ref, nw_ref, e_ref,
                  yraw_ref.at[rows], ys_ref.at[rows], st_ref.at[sub], acol_ref.at[rows], dte_ref.at[rows], state)

    def chunk(xbc_ref, dtr_ref, z_ref, dtb_ref, alog_ref, dske_ref, nw_ref, e_ref,
              yraw_ref, ys_ref, st_ref, acol_ref, dte_ref, state):
        _, dt, _, a = _ssd_heads(dtr_ref, dtb_ref, alog_ref)
        tril = _bf((_iota((CHUNK, CHUNK), 0) >= _iota((CHUNK, CHUNK), 1)).astype(F32))
        ac3, dt3 = _split3(_dot_exact_r(tril, a)), _split3(dt)
        for tl in _TILES:
            e_t = e_ref[:, tl]
            acol_ref[:, tl] = _dot(ac3[0], e_t) + _dot(ac3[1], e_t) + _dot(ac3[2], e_t)
            dte_ref[:, tl] = _dot(dt3[0], e_t) + _dot(dt3[1], e_t)

        itile, ctile, bdmask = _ssd_masks()
        st_ref[...] = state[...]
        bg, cg = _ssd_bc(xbc_ref)
        for g0 in range(0, SSD_GROUPS, SSD_BLOCK_GROUPS):
            gids = range(g0, g0 + SSD_BLOCK_GROUPS)
            quads = [(g, ql, qs) for g, ql, qs in _QUADS if g in gids]
            q = {g: _ssd_group(xbc_ref, acol_ref, dte_ref, _GROUP_LANES[g]) for g in gids}
            stg = {g: state[:, _GROUP_LANES[g]] for g in gids}
            cbt = {g: _dot_nt(cg[g], _tile4(bg[g])) for g in gids}
            ys = {g: _dot(cg[g], _bf(stg[g])) for g in gids}
            dq = [_ssd_decay(acol_ref[:, ql], itile, ctile) for _, ql, _ in quads]
            xbd = [_ssd_blockdiag(q[g]["xdt"][:, qs], bdmask) for g, _, qs in quads]
            yq = [_dot(_bf(cbt[g] * dq[i]), xbd[i]) + ys[g][:, qs] * q[g]["ea"][:, qs]
                  + dske_ref[:, ql] * q[g]["xs"][:, qs] for i, (g, ql, qs) in enumerate(quads)]
            new_st = {g: stg[g] * q[g]["eal"] + _dot_tn(bg[g], _bf(q[g]["xdt"] * q[g]["tail"])) for g in gids}
            for i, (_, ql, _) in enumerate(quads):
                yraw_ref[:, ql] = yq[i]
            for g in gids:
                state[:, _GROUP_LANES[g]] = new_st[g]

        sq = jnp.zeros((CHUNK, SSD_QW), F32)
        for tl in _TILES:
            t = _silu_gate(z_ref, yraw_ref, tl)[2]
            sq = sq + t * t
        rstd = lax.rsqrt(jnp.sum(sq, axis=1, keepdims=True) * (1.0 / SSD_W) + EPS)
        for tl in _TILES:
            ys_ref[:, tl] = _bf(_silu_gate(z_ref, yraw_ref, tl)[2] * rstd * nw_ref[:, tl])

    steps = nc // CHUNKS_PER_STEP
    sp = _ssd_specs(steps, False)
    wide = jax.ShapeDtypeStruct((s, SSD_W), F32)
    return pl.pallas_call(
        body,
        out_shape=(wide, jax.ShapeDtypeStruct((s, SSD_W), BF16), jax.ShapeDtypeStruct((nc, SSD_N, SSD_W), F32),
                   wide, wide),
        grid=(steps,),
        in_specs=[sp["xbc"], sp["dt"], sp["wide"], sp["head"], sp["head"], sp["roww"], sp["roww"], sp["e"]],
        out_specs=(sp["wide"], sp["wide"], sp["state"], sp["wide"], sp["wide"]),
        scratch_shapes=[pltpu.VMEM((SSD_N, SSD_W), F32)],
        compiler_params=_params("arbitrary"), name=name,
    )(xbc, dt_raw, z, dt_bias, a_log, dske, norm_w, e_sel)


def _ssd_bwd(xbc, dt_raw, z, yraw, states, dys, acol, dte, dt_bias, a_log, dske, norm_w, name, rider):
    s = xbc.shape[0]
    nc = s // CHUNK
    steps = nc // CHUNKS_PER_STEP
    _, et_sel = _head_select()
    n_in, n_out, n_scratch = 13, 7, 5
    n_rin, n_rout = len(rider.inputs), len(rider.out_shapes)

    def body(*refs):
        ins, refs = refs[:n_in], refs[n_in:]
        r_ins, refs = refs[:n_rin], refs[n_rin:]
        outs, refs = refs[:n_out], refs[n_out:]
        r_outs, refs = refs[:n_rout], refs[n_rout:]
        ride = functools.partial(rider.run, pl.program_id(0), steps, r_ins, r_outs, refs[n_scratch],
                                 refs[n_scratch + 1])
        ride(last=False)
        compute(*ins, *outs, *refs[:n_scratch])
        ride(last=True)

    def compute(xbc_ref, dtr_ref, z_ref, yraw_ref, st_ref, dys_ref, acol_ref, dte_ref, dtb_ref, alog_ref,
                dske_ref, nw_ref, et_ref, dxbc_ref, dz_ref, ddtr_ref, ddtb_ref, dalog_ref, ddsk_ref, dnw_ref,
                dstate, dsk_acc, dy_s, dacol_s, ddte_s):
        @pl.when(pl.program_id(0) == 0)
        def _():
            dstate[...] = jnp.zeros_like(dstate)
            dsk_acc[...] = jnp.zeros_like(dsk_acc)
            ddtb_ref[...] = jnp.zeros_like(ddtb_ref)
            dalog_ref[...] = jnp.zeros_like(dalog_ref)
            dnw_ref[...] = jnp.zeros_like(dnw_ref)

        for sub in reversed(range(CHUNKS_PER_STEP)):
            rows = _chunk_rows(sub)
            chunk(xbc_ref.at[rows], dtr_ref.at[rows], z_ref.at[rows], yraw_ref.at[rows], st_ref.at[sub],
                  dys_ref.at[rows], acol_ref.at[rows], dte_ref.at[rows], dtb_ref, alog_ref, dske_ref, nw_ref, et_ref,
                  dxbc_ref.at[rows], dz_ref.at[rows], ddtr_ref.at[rows], ddtb_ref, dalog_ref, dnw_ref,
                  dstate, dsk_acc, dy_s, dacol_s, ddte_s)

        @pl.when(pl.program_id(0) == steps - 1)
        def _():
            ddsk_ref[...] = _dot_exact_l(jnp.broadcast_to(dsk_acc[...], (8, SSD_W)), et_ref[...])[0:1]

    def chunk(xbc_ref, dtr_ref, z_ref, yraw_ref, st_ref, dys_ref, acol_ref, dte_ref, dtb_ref, alog_ref,
              dske_ref, nw_ref, et_ref, dxbc_ref, dz_ref, ddtr_ref, ddtb_ref, dalog_ref, dnw_ref,
              dstate, dsk_acc, dy_s, dacol_s, ddte_s):
        itile, ctile, bdmask = _ssd_masks()
        last_row = (_iota((CHUNK, 1), 0) == CHUNK - 1).astype(F32)

        s1 = jnp.zeros((CHUNK, SSD_QW), F32)
        s2 = jnp.zeros((CHUNK, SSD_QW), F32)
        for tl in _TILES:
            t = _silu_gate(z_ref, yraw_ref, tl)[2]
            s1 = s1 + t * t
            s2 = s2 + dys_ref[:, tl] * nw_ref[:, tl] * t
        rstd = lax.rsqrt(jnp.sum(s1, axis=1, keepdims=True) * (1.0 / SSD_W) + EPS)
        back = rstd * rstd * rstd * jnp.sum(s2, axis=1, keepdims=True) * (1.0 / SSD_W)
        for tl in _TILES:
            zv, sz, t = _silu_gate(z_ref, yraw_ref, tl)
            dysv = dys_ref[:, tl]
            dt_ = rstd * (dysv * nw_ref[:, tl]) - t * back
            dnw_ref[:, tl] += jnp.sum(dysv * t * rstd, axis=0, keepdims=True)
            dz_ref[:, tl] = _bf(dt_ * yraw_ref[:, tl] * (sz * (1.0 + zv * (1.0 - sz))))
            dy_t = dt_ * (zv * sz)
            dy_s[:, tl] = dy_t
            dsk_acc[:, tl] += jnp.sum(dy_t * xbc_ref[:, tl], axis=0, keepdims=True)

        gl = _GROUP_LANES
        bg, cg = _ssd_bc(xbc_ref)
        for g0 in range(0, SSD_GROUPS, SSD_BLOCK_GROUPS):
            gids = range(g0, g0 + SSD_BLOCK_GROUPS)
            quads = [(i, g, ql, qs) for i, (g, ql, qs) in enumerate(_QUADS) if g in gids]
            q = {g: _ssd_group(xbc_ref, acol_ref, dte_ref, gl[g]) for g in gids}
            stg = {g: st_ref[:, gl[g]] for g in gids}
            dsn = {g: dstate[:, gl[g]] for g in gids}
            stgb, dsnb = {g: _bf(stg[g]) for g in gids}, {g: _bf(dsn[g]) for g in gids}
            btile = {g: _tile4(bg[g]) for g in gids}
            cbt = {g: _dot_nt(cg[g], btile[g]) for g in gids}
            dyg = {g: dy_s[:, gl[g]] for g in gids}
            eag, tailg = {g: q[g]["ea"] for g in gids}, {g: q[g]["tail"] for g in gids}
            xdtg, ealg = {g: q[g]["xdt"] for g in gids}, {g: q[g]["eal"] for g in gids}
            dys_g = {g: _bf(dyg[g] * eag[g]) for g in gids}
            ysg = {g: _dot(cg[g], stgb[g]) for g in gids}
            dc0 = {g: _dot_nt(dys_g[g], stgb[g]) for g in gids}
            dst = {g: _dot_tn(cg[g], dys_g[g]) for g in gids}
            dwt = {g: _dot(bg[g], dsnb[g]) for g in gids}
            db0 = {g: _dot_nt(_bf(xdtg[g] * tailg[g]), dsnb[g]) for g in gids}
            dtl = {g: dwt[g] * xdtg[g] * tailg[g] for g in gids}
            dal_row = {g: jnp.sum(dtl[g], axis=0, keepdims=True)
                       + jnp.sum(dsn[g] * stg[g], axis=0, keepdims=True) * ealg[g] for g in gids}
            for g in gids:
                dstate[:, gl[g]] = dst[g] + dsn[g] * ealg[g]
            dq = {i: _ssd_decay(acol_ref[:, ql], itile, ctile) for i, _, ql, _ in quads}
            mq = {i: cbt[g] * dq[i] for i, g, _, _ in quads}
            xbd = {i: _ssd_blockdiag(xdtg[g][:, qs], bdmask) for i, g, _, qs in quads}
            dyq = {i: _bf(dyg[g][:, qs]) for i, g, _, qs in quads}
            dm = {i: _dot_nt(dyq[i], xbd[i]) for i, _, _, _ in quads}
            dxdt_q = {i: _fold4(jnp.where(bdmask, _dot_tn(_bf(mq[i]), dyq[i]), 0.0)) for i, _, _, _ in quads}
            eq = {i: dm[i] * mq[i] for i, _, _, _ in quads}
            dacol_q = {i: eq[i] - itile * jnp.sum(eq[i], axis=0, keepdims=True) for i, _, _, _ in quads}
            dcbt = {i: _bf(dm[i] * dq[i]) for i, _, _, _ in quads}
            dc_q = {i: _dot(dcbt[i], btile[g]) for i, g, _, _ in quads}
            db_q = {i: _fold4(_dot_tn(dcbt[i], cg[g])) for i, g, _, _ in quads}
            for g in gids:
                dxdt_g = jnp.concatenate([dxdt_q[2 * g], dxdt_q[2 * g + 1]], axis=1) + dwt[g] * tailg[g]
                dxbc_ref[:, gl[g]] = _bf(dyg[g] * dske_ref[:, gl[g]] + dxdt_g * q[g]["dte"])
                ddte_s[:, gl[g]] = dxdt_g * q[g]["xs"]
                dacol_s[:, gl[g]] = (jnp.concatenate([dacol_q[2 * g], dacol_q[2 * g + 1]], axis=1)
                                     + dyg[g] * (ysg[g] * eag[g]) - dtl[g] + last_row * dal_row[g])
                dxbc_ref[:, B_OFF + g * SSD_N:B_OFF + (g + 1) * SSD_N] = _bf(db0[g] + db_q[2 * g] + db_q[2 * g + 1])
                dxbc_ref[:, C_OFF + g * SSD_N:C_OFF + (g + 1) * SSD_N] = _bf(dc0[g] + dc_q[2 * g] + dc_q[2 * g + 1])

        dacum = jnp.zeros((CHUNK, HEAD_PAD), F32)
        ddt = jnp.zeros((CHUNK, HEAD_PAD), F32)
        for i, tl in enumerate(_TILES):
            et_t = et_ref[i * SSD_QW:(i + 1) * SSD_QW, :]
            dacum = dacum + _dot_exact_l(dacol_s[:, tl], et_t, pieces=2)
            ddt = ddt + _dot_exact_l(ddte_s[:, tl], et_t, pieces=2)
        u, _, nexp, a = _ssd_heads(dtr_ref, dtb_ref, alog_ref)
        triu = _bf((_iota((CHUNK, CHUNK), 1) >= _iota((CHUNK, CHUNK), 0)).astype(F32))
        da = _dot_exact_r(triu, dacum)
        ddt = ddt + da * nexp
        dalog_ref[...] += jnp.sum(da * a, axis=0, keepdims=True)
        du = ddt * _sigmoid(u)
        ddtr_ref[...] = _bf(du)
        ddtb_ref[...] += jnp.sum(du, axis=0, keepdims=True)

    sp = _ssd_specs(steps, True)
    return pl.pallas_call(
        body,
        out_shape=(jax.ShapeDtypeStruct((s, SSD_CD), BF16), jax.ShapeDtypeStruct((s, SSD_W), BF16),
                   jax.ShapeDtypeStruct((s, HEAD_PAD), BF16), jax.ShapeDtypeStruct((1, HEAD_PAD), F32),
                   jax.ShapeDtypeStruct((1, HEAD_PAD), F32), jax.ShapeDtypeStruct((1, HEAD_PAD), F32),
                   jax.ShapeDtypeStruct((1, SSD_W), F32), *rider.out_shapes),
        grid=(steps,),
        in_specs=[sp["xbc"], sp["dt"], sp["wide"], sp["wide"], sp["state"], sp["wide"], sp["wide"], sp["wide"],
                  sp["head"], sp["head"], sp["roww"], sp["roww"], sp["et"], *rider.in_specs()],
        out_specs=(sp["xbc"], sp["wide"], sp["dt"], sp["head"], sp["head"], sp["head"], sp["roww"],
                   *rider.out_specs()),
        scratch_shapes=[pltpu.VMEM((SSD_N, SSD_W), F32), pltpu.VMEM((1, SSD_W), F32),
                        *[pltpu.VMEM((CHUNK, SSD_W), F32)] * 3, *rider.scratch()],
        compiler_params=_params("arbitrary"), name=name,
    )(xbc, dt_raw, z, yraw, states, dys, acol, dte, dt_bias, a_log, dske, norm_w, et_sel, *rider.inputs)


W_IN_SHARD = IN_PROJ // N_DEV
W_IN_ROW_PARTS = ((0, 1152), (1152, 832), (1984, 840))


def _device_step(x, pos_col, target, norm1_w, conv_w_shard, conv_b, dt_bias, a_log, d_skip, ssd_norm_w, norm_f_w,
                 w_in_shard, br_ret_shard, br_ssd_shard, out_shard):
    inv_freq = jnp.asarray(ROPE_THETA ** (-np.arange(ROPE_HALF, dtype=np.float64) / ROPE_HALF), F32).reshape(1, ROPE_HALF)
    dtb, alog = _pad_heads(dt_bias), _pad_heads(a_log)
    dske = jnp.repeat(d_skip.reshape(SSD_HEADS).astype(F32), SSD_P).reshape(1, SSD_W)
    my_core = lax.axis_index("c")

    h, cos, sin, br_ret_shard, br_ssd_shard, out_shard, w_all = _rmsnorm_rope_fwd(
        x, norm1_w, pos_col, inv_freq, [br_ret_shard, br_ssd_shard, out_shard], "rmsnorm1_fwd",
        _gather_rider([w_in_shard], 1.0, 1.0))
    w_all = w_all.reshape(IN_PROJ, D_MODEL)
    w_dt = jnp.pad(w_all[OFF_DT:OFF_G], ((0, HEAD_PAD - SSD_HEADS), (0, 0)))
    w_g = w_all[OFF_G:]
    rows_r, rows_z, rows_xbc = (0, OFF_Z), (OFF_Z, SSD_W), (OFF_XBC, SSD_CD)
    gather = _gather_rider([br_ret_shard, br_ssd_shard, out_shard, conv_w_shard], relay_at=0.35, pass_at=0.6)
    p_r, all_ret, all_ssd, all_out, all_conv = _matmul(h, w_all, "nt", F32, "proj_ret", rider=gather, b_rows=rows_r)
    w_br_ret = all_ret.reshape(RET_W, D_MODEL)
    w_br_ssd = all_ssd.reshape(SSD_W, D_MODEL)
    w_out = all_out.reshape(D_MODEL, D_MODEL)
    conv_w = all_conv.transpose(1, 0, 2).reshape(SSD_CONV, SSD_CD)
    p_z = _matmul(h, w_all, "nt", F32, "proj_z", b_rows=rows_z)
    p_xbc = _matmul(h, w_all, "nt", F32, "proj_xbc", b_rows=rows_xbc)
    p_dt = _matmul(h, w_dt, "nt", F32, "proj_dt")
    p_g = _matmul(h, w_g, "nt", BF16, "proj_gates")
    y_r, ret_states = _retention_fwd(p_r, cos, sin, "retention_fwd")
    xbc_act, silu_slope = _conv_fwd(p_xbc, conv_w, conv_b, "conv_fwd")
    y_raw, y_s, ssd_states, acol, dte = _ssd_fwd(xbc_act, p_dt, p_z, dtb, alog, dske, ssd_norm_w, "ssd_fwd")
    pr = _matmul(y_r, w_br_ret, "nn", BF16, "branch_ret")
    ps = _matmul(y_s, w_br_ssd, "nn", BF16, "branch_ssd")
    merged = _merge_fwd(p_g, pr, ps, "merge_fwd")
    o = _matmul(merged, w_out, "nn", F32, "out_proj")
    dx2, g_norm_f, loss_acc = _final_fwd_bwd(x, o, norm_f_w, target, "final_norm_loss")

    g_w_out = _matmul(merged, dx2, "tn", BF16, "grad_w_out")
    dmerged = _matmul(dx2, w_out, "nt", F32, "d_merged")
    dpr, dps, dp_g = _merge_bwd(p_g, pr, ps, dmerged, "merge_bwd")
    g_w_br_ret = _matmul(y_r, dpr, "tn", BF16, "grad_w_br_ret")
    g_w_br_ssd = _matmul(y_s, dps, "tn", BF16, "grad_w_br_ssd")
    dy_r = _matmul(dpr, w_br_ret, "nt", F32, "d_y_ret")
    dy_s = _matmul(dps, w_br_ssd, "nt", F32, "d_y_ssd")
    scatter = _scatter_rider([g_w_out.reshape(N_DEV, -1, D_MODEL), g_w_br_ret.reshape(N_DEV, -1, D_MODEL),
                              g_w_br_ssd.reshape(N_DEV, -1, D_MODEL)])
    dxbc_act, dp_z, dp_dt, g_dtb, g_alog, g_dsk, g_ssd_norm, got_out, got_ret, got_ssd = _ssd_bwd(
        xbc_act, p_dt, p_z, y_raw, ssd_states, dy_s, acol, dte, dtb, alog, dske, ssd_norm_w, "ssd_bwd", scatter)
    dp_xbc, g_conv_w, g_conv_b = _conv_bwd(p_xbc, conv_w, silu_slope, dxbc_act, "conv_bwd")
    dp_r = _retention_bwd(p_r, cos, sin, ret_states, dy_r, "retention_bwd")
    g_w_in = jnp.concatenate([
        _matmul(dp_r, h, "tn", BF16, "grad_w_ret"),
        _matmul(dp_z, h, "tn", BF16, "grad_w_z"),
        _matmul(dp_xbc, h, "tn", BF16, "grad_w_xbc"),
        _matmul(dp_dt, h, "tn", BF16, "grad_w_dt")[:SSD_HEADS],
        _matmul(dp_g, h, "tn", BF16, "grad_w_gates"),
    ], axis=0)
    blocks = g_w_in.reshape(N_DEV, W_IN_SHARD, D_MODEL)
    dh, from_sibling = _matmul(dp_r, w_all, "nn", F32, "d_h_ret", rider=_sibling_rider(blocks), b_rows=rows_r)
    chip_sum = _chip_sum(blocks, from_sibling, my_core, "w_in_chip_sum")
    carriers = (("d_h_xbc", dp_xbc, w_all, rows_xbc), ("d_h_gates", dp_g, w_g, None), ("d_h_z", dp_z, w_all, rows_z))
    landed = []
    for (row0, rows), (nm, dp, w, b_rows) in zip(W_IN_ROW_PARTS, carriers):
        dt_pair = (dp_dt, w_dt) if nm == "d_h_z" else None
        dh, got = _matmul(dp, w, "nn", F32, nm, add=dh, rider=_chip_rider(chip_sum, row0, rows), b_rows=b_rows,
                          extra=dt_pair)
        landed.append(got)
    grad_x, g_norm1 = _rmsnorm_bwd(x, norm1_w, dh, dx2, "rmsnorm1_bwd")
    small = dict(norm1_w=g_norm1, conv_w=g_conv_w, conv_b=g_conv_b, dt_bias=g_dtb[:, :SSD_HEADS],
                 a_log=g_alog[:, :SSD_HEADS], d_skip=g_dsk[:, :SSD_HEADS], ssd_norm_w=g_ssd_norm,
                 norm_f_w=g_norm_f)
    big = dict(w_in=_sum_row_parts(landed, "w_in_sum"), w_br_ret=_sum_slots(got_ret, "w_br_ret_sum"),
               w_br_ssd=_sum_slots(got_ssd, "w_br_ssd_sum"), w_out=_sum_slots(got_out, "w_out_sum"))
    return loss_acc[0, 0], grad_x, small, big


def _all_reduce_small(vec, name):
    r, c = vec.shape

    def body(x_ref, out_ref, land, send_sems, recv_sems):
        x, y, cc = _mesh_pos()
        my_idx = 4 * x + 2 * y + cc
        land[my_idx] = x_ref[...]
        copies = []
        for k in range(1, N_DEV):
            px, py, pc = x ^ (k >> 2), y ^ ((k >> 1) & 1), cc ^ (k & 1)
            cp = pltpu.make_async_remote_copy(
                src_ref=x_ref, dst_ref=land.at[my_idx],
                send_sem=send_sems.at[k - 1], recv_sem=recv_sems.at[k - 1],
                device_id=(px, py, pc), device_id_type=MESH)
            cp.start()
            copies.append(cp)
        for k in range(1, N_DEV):
            px, py, pc = x ^ (k >> 2), y ^ ((k >> 1) & 1), cc ^ (k & 1)
            pltpu.make_async_remote_copy(
                src_ref=x_ref, dst_ref=land.at[4 * px + 2 * py + pc],
                send_sem=send_sems.at[k - 1], recv_sem=recv_sems.at[k - 1],
                device_id=(px, py, pc), device_id_type=MESH).wait_recv()
        for cp in copies:
            cp.wait_send()
        acc = land[0]
        for i in range(1, N_DEV):
            acc = acc + land[i]
        out_ref[...] = acc

    return pl.pallas_call(
        body,
        out_shape=jax.ShapeDtypeStruct((r, c), F32),
        in_specs=[pl.BlockSpec(memory_space=pltpu.VMEM)],
        out_specs=pl.BlockSpec(memory_space=pltpu.VMEM),
        scratch_shapes=[pltpu.VMEM((N_DEV, r, c), F32), pltpu.SemaphoreType.DMA((7,)),
                        pltpu.SemaphoreType.DMA((7,))],
        name=name)(vec)


_SMALL = ("norm1_w", "conv_w", "conv_b", "dt_bias", "a_log", "d_skip", "ssd_norm_w", "norm_f_w")
_SMALL_COLS = 128
_W_IN = "w_in"
_WEIGHTS = ("norm1_w", "w_in", "conv_w", "conv_b", "dt_bias", "a_log", "d_skip", "ssd_norm_w",
            "w_br_ret", "w_br_ssd", "w_out", "norm_f_w")


def _pack(parts):
    flat = jnp.concatenate([p.reshape(-1).astype(F32) for p in parts])
    rows = -(-flat.shape[0] // (8 * _SMALL_COLS)) * 8
    return jnp.pad(flat, (0, rows * _SMALL_COLS - flat.shape[0])).reshape(rows, _SMALL_COLS)


def _unpack(packed, shapes):
    flat = packed.reshape(-1)
    out, off = [], 0
    for shp in shapes:
        n = int(np.prod(shp))
        out.append(flat[off:off + n].reshape(shp))
        off += n
    return out


def kernel(x, positions, norm1_w, w_in, conv_w, conv_b, dt_bias, a_log, d_skip, ssd_norm_w, w_br_ret, w_br_ssd, w_out, norm_f_w, loss_target, m_norm1_w, m_w_in, m_conv_w, m_conv_b, m_dt_bias, m_a_log, m_d_skip, m_ssd_norm_w, m_w_br_ret, m_w_br_ssd, m_w_out, m_norm_f_w, v_norm1_w, v_w_in, v_conv_w, v_conv_b, v_dt_bias, v_a_log, v_d_skip, v_ssd_norm_w, v_w_br_ret, v_w_br_ssd, v_w_out, v_norm_f_w):
    w = dict(norm1_w=norm1_w, w_in=w_in, conv_w=conv_w, conv_b=conv_b, dt_bias=dt_bias, a_log=a_log,
             d_skip=d_skip, ssd_norm_w=ssd_norm_w, w_br_ret=w_br_ret, w_br_ssd=w_br_ssd, w_out=w_out,
             norm_f_w=norm_f_w)
    m = dict(norm1_w=m_norm1_w, w_in=m_w_in, conv_w=m_conv_w, conv_b=m_conv_b, dt_bias=m_dt_bias,
             a_log=m_a_log, d_skip=m_d_skip, ssd_norm_w=m_ssd_norm_w, w_br_ret=m_w_br_ret,
             w_br_ssd=m_w_br_ssd, w_out=m_w_out, norm_f_w=m_norm_f_w)
    v = dict(norm1_w=v_norm1_w, w_in=v_w_in, conv_w=v_conv_w, conv_b=v_conv_b, dt_bias=v_dt_bias,
             a_log=v_a_log, d_skip=v_d_skip, ssd_norm_w=v_ssd_norm_w, w_br_ret=v_w_br_ret,
             w_br_ssd=v_w_br_ssd, w_out=v_w_out, norm_f_w=v_norm_f_w)
    s = x.shape[1]
    my_idx = 4 * lax.axis_index("x") + 2 * lax.axis_index("y") + lax.axis_index("c")

    w[_W_IN], m[_W_IN], v[_W_IN] = w_in[0].T, m_w_in[0].T, v_w_in[0].T

    loss_part, grad_x, g_small, g_big = _device_step(
        x[0], positions.reshape(s, 1), loss_target[0], norm1_w, conv_w[0], conv_b, dt_bias, a_log, d_skip,
        ssd_norm_w, norm_f_w.reshape(1, D_MODEL), _cast_bf16(w[_W_IN], "cast_w_in"),
        w_br_ret[0], w_br_ssd[0], w_out[0])

    loss = lax.psum(loss_part, ("x", "y", "c"))

    small_shapes = [g_small[n].shape for n in _SMALL]
    summed = _unpack(_all_reduce_small(_pack([g_small[n] for n in _SMALL]), "allreduce_small"), small_shapes)
    grads = dict(zip(_SMALL, summed))
    conv_cols = SSD_CD // N_DEV
    grads["conv_w"] = lax.dynamic_slice_in_dim(grads["conv_w"], my_idx * conv_cols, conv_cols, axis=1)
    grads["norm_f_w"] = grads["norm_f_w"].reshape(D_MODEL)
    for n in ("norm1_w", "conv_w", "conv_b", "dt_bias", "a_log", "d_skip", "ssd_norm_w"):
        grads[n] = grads[n].reshape(w[n].shape)

    delta, new_m, new_v = {}, {}, {}
    for n in ("w_br_ret", "w_br_ssd", "w_out"):
        w[n], m[n], v[n] = w[n][0], m[n][0], v[n][0]
    for n in (_W_IN, "w_br_ret", "w_br_ssd", "w_out"):
        back = (lambda a: a.T[None]) if n == _W_IN else (lambda a: a[None])
        res = _adamw(w[n], g_big[n], m[n], v[n], "adamw_" + n)
        grads[n] = back(g_big[n])
        delta[n], new_m[n], new_v[n] = (back(a) for a in res)
    shapes = [w[n].shape for n in _SMALL]
    packed = _adamw(_pack([w[n] for n in _SMALL]), _pack([grads[n] for n in _SMALL]),
                    _pack([m[n] for n in _SMALL]), _pack([v[n] for n in _SMALL]), "adamw_small")
    for res, dst in zip(packed, (delta, new_m, new_v)):
        for n, a in zip(_SMALL, _unpack(res, shapes)):
            dst[n] = a

    return (loss, grad_x.reshape(x.shape), *[grads[n] for n in _WEIGHTS], *[delta[n] for n in _WEIGHTS],
            *[new_m[n] for n in _WEIGHTS], *[new_v[n] for n in _WEIGHTS])
```

```python
import functools

import numpy as np
import jax
import jax.numpy as jnp
from jax import lax
from jax.experimental import pallas as pl
from jax.experimental.pallas import tpu as pltpu

F32 = jnp.float32
BF16 = jnp.bfloat16

D_MODEL = 2048
CHUNK = 64
CHUNKS_PER_STEP = 2
STEP_ROWS = CHUNK * CHUNKS_PER_STEP
RET_CHUNKS_PER_STEP = 4
RET_STEP_ROWS = CHUNK * RET_CHUNKS_PER_STEP
EPS = 1e-6
N_DEV = 8

RET_HEADS = 8
RET_DK = 256
RET_W = RET_HEADS * RET_DK
ROPE_THETA = 10000.0
ROPE_HALF = RET_DK // 2

SSD_W = 4096
SSD_P = 64
SSD_HEADS = 64
SSD_GROUPS = 8
SSD_N = 128
SSD_GW = SSD_W // SSD_GROUPS
SSD_QW = 256
SSD_BLOCK_GROUPS = 1
SSD_CONV = 4
SSD_CD = SSD_W + 2 * SSD_GROUPS * SSD_N
HEAD_PAD = 128
B_OFF = SSD_W
C_OFF = SSD_W + SSD_GROUPS * SSD_N

ADAM_LR = 0.001
ADAM_B1 = 0.9
ADAM_B2 = 0.999
ADAM_EPS = 1e-08
ADAM_WD = 0.01
ADAM_STEP = 10

SPLITS = (RET_W, RET_W, RET_W, RET_W, SSD_W, SSD_CD, SSD_HEADS, D_MODEL, D_MODEL)
IN_PROJ = sum(SPLITS)
OFF_Z = 4 * RET_W
OFF_XBC = OFF_Z + SSD_W
OFF_DT = OFF_XBC + SSD_CD
OFF_G = OFF_DT + SSD_HEADS

ROW_TILE = 256
CONV_TILE = 128
MM_TILE = 1024
MM_TILE_WIDE = 2048
MM_TK = 2048
VMEM_CAP = 60 << 20

MESH = pl.DeviceIdType.MESH


def _dot(a, b):
    return lax.dot_general(a, b, (((1,), (0,)), ((), ())), preferred_element_type=F32)


def _dot_nt(a, b):
    return lax.dot_general(a, b, (((1,), (1,)), ((), ())), preferred_element_type=F32)


def _dot_tn(a, b):
    return lax.dot_general(a, b, (((0,), (0,)), ((), ())), preferred_element_type=F32)


def _bf(x):
    return x.astype(BF16)


def _split3(x):
    hi = x.astype(BF16)
    r = x - hi.astype(F32)
    mid = r.astype(BF16)
    lo = (r - mid.astype(F32)).astype(BF16)
    return hi, mid, lo


def _dot_exact_l(x, sel, pieces=3):
    hi, mid, lo = _split3(x)
    r = _dot(hi, sel) + _dot(mid, sel)
    return r + _dot(lo, sel) if pieces == 3 else r


def _dot_exact_r(sel, x):
    hi, mid, lo = _split3(x)
    return _dot(sel, hi) + _dot(sel, mid) + _dot(sel, lo)


def _sigmoid(x):
    return 1.0 / (1.0 + jnp.exp(-x))


def _softplus(x):
    return jnp.maximum(x, 0.0) + jnp.log(1.0 + jnp.exp(-jnp.abs(x)))


def _iota(shape, axis):
    return lax.broadcasted_iota(jnp.int32, shape, axis)


def _params(*sem):
    return pltpu.CompilerParams(dimension_semantics=sem)


class _Rider:
    def __init__(self, inputs, out_shapes, n_sems, phases):
        self.inputs, self.out_shapes, self.n_sems, self.phases = tuple(inputs), tuple(out_shapes), n_sems, phases

    def in_specs(self):
        return [pl.BlockSpec(memory_space=pl.ANY)] * len(self.inputs)

    def out_specs(self):
        return [pl.BlockSpec(memory_space=pl.ANY)] * len(self.out_shapes)

    def scratch(self):
        return [pltpu.SemaphoreType.DMA((self.n_sems,)), pltpu.SemaphoreType.DMA((self.n_sems,))]

    def run(self, step, n_steps, ins, outs, send_sems, recv_sems, last):
        for frac, fn in self.phases:
            if (frac >= 1.0) != last:
                continue
            at = min(int(frac * n_steps), n_steps - 1)

            @pl.when(step == at)
            def _(fn=fn):
                fn(ins, outs, send_sems, recv_sems)


def _mesh_pos():
    return lax.axis_index("x"), lax.axis_index("y"), lax.axis_index("c")


def _gather_rider(shards, relay_at, pass_at):
    n = len(shards)

    def tools(a, ins, outs, send_sems, recv_sems):
        x, y, cc = _mesh_pos()
        nbrs = [(1 - x, y), (x, 1 - y)]
        diag = (1 - x, 1 - y)
        relay_from, relay_to = (x ^ cc, y ^ (1 - cc)), (x ^ (1 - cc), y ^ cc)

        def slot(px, py, pc):
            return outs[a].at[4 * px + 2 * py + pc]

        def copy(k, block, to, src=None):
            return pltpu.make_async_remote_copy(
                src_ref=slot(*block) if src is None else src, dst_ref=slot(*block),
                send_sem=send_sems.at[8 * a + k], recv_sem=recv_sems.at[8 * a + k], device_id=to, device_id_type=MESH)

        me, sibling = (x, y, cc), (x, y, 1 - cc)
        return dict(
            mine=lambda: pltpu.make_async_copy(ins[a], slot(*me), send_sems.at[8 * a + 7]),
            first=lambda: [copy(0, me, sibling, src=ins[a])] + [copy(1 + j, me, (*chip, cc), src=ins[a])
                                                                for j, chip in enumerate(nbrs)],
            landed=lambda: [copy(1 + j, (*chip, cc), me) for j, chip in enumerate(nbrs)],
            relay=lambda: copy(3, (*relay_from, cc), (*relay_to, cc)),
            relayed_in=lambda: copy(3, (*diag, cc), me),
            passed=lambda: [copy(4 + j, (*chip, cc), sibling) for j, chip in enumerate(nbrs)],
            passed_diag=lambda: copy(6, (*diag, cc), sibling),
            from_sibling=lambda: [copy(0, sibling, me)] + [copy(4 + j, (*chip, 1 - cc), me)
                                                           for j, chip in enumerate(nbrs + [diag])])

    def start(*refs):
        for a in range(n):
            t = tools(a, *refs)
            t["mine"]().start()
            for cp in t["first"]():
                cp.start()

    def relay(*refs):
        for a in range(n):
            t = tools(a, *refs)
            for got in t["landed"]():
                got.wait_recv()
            t["relay"]().start()
            for cp in t["passed"]():
                cp.start()

    def pass_diag(*refs):
        for a in range(n):
            t = tools(a, *refs)
            t["relayed_in"]().wait_recv()
            t["passed_diag"]().start()

    def finish(*refs):
        for a in range(n):
            t = tools(a, *refs)
            for cp in t["from_sibling"]():
                cp.wait_recv()
            for cp in t["first"]() + [t["relay"]()] + t["passed"]() + [t["passed_diag"]()]:
                cp.wait_send()
            t["mine"]().wait()

    outs = [jax.ShapeDtypeStruct((N_DEV,) + s.shape, s.dtype) for s in shards]
    return _Rider(shards, outs, 8 * n, ((0.0, start), (relay_at, relay), (pass_at, pass_diag), (1.0, finish)))


def _scatter_rider(blocks):
    n = len(blocks)

    def copies(a, ins, outs, send_sems, recv_sems, with_back):
        x, y, cc = _mesh_pos()
        my_idx = 4 * x + 2 * y + cc
        mine = pltpu.make_async_copy(ins[a].at[my_idx], outs[a].at[my_idx], send_sems.at[8 * a + 7])
        out, back = [], []
        for k in range(1, N_DEV):
            px, py, pc = x ^ (k >> 2), y ^ ((k >> 1) & 1), cc ^ (k & 1)
            peer_idx = 4 * px + 2 * py + pc
            sems = dict(send_sem=send_sems.at[8 * a + k - 1], recv_sem=recv_sems.at[8 * a + k - 1],
                        device_id=(px, py, pc), device_id_type=MESH)
            out.append(pltpu.make_async_remote_copy(src_ref=ins[a].at[peer_idx], dst_ref=outs[a].at[my_idx], **sems))
            if with_back:
                back.append(pltpu.make_async_remote_copy(src_ref=ins[a].at[my_idx], dst_ref=outs[a].at[peer_idx], **sems))
        return mine, out, back

    def start(*refs):
        for a in range(n):
            mine, out, _ = copies(a, *refs, False)
            mine.start()
            for cp in out:
                cp.start()

    def finish(*refs):
        for a in range(n):
            mine, out, back = copies(a, *refs, True)
            for cp in back:
                cp.wait_recv()
            for cp in out:
                cp.wait_send()
            mine.wait()

    outs = [jax.ShapeDtypeStruct(b.shape, b.dtype) for b in blocks]
    return _Rider(blocks, outs, 8 * n, ((0.0, start), (1.0, finish)))


def _sibling_rider(blocks):
    _, r, c = blocks.shape

    def copies(ins, outs, send_sems, recv_sems):
        x, y, cc = _mesh_pos()
        return [pltpu.make_async_remote_copy(
            src_ref=ins[0].at[2 * j + 1 - cc], dst_ref=outs[0].at[j], send_sem=send_sems.at[j],
            recv_sem=recv_sems.at[j], device_id=(x, y, 1 - cc), device_id_type=MESH) for j in range(4)]

    def start(*refs):
        for cp in copies(*refs):
            cp.start()

    def finish(*refs):
        for cp in copies(*refs):
            cp.wait_recv()
        for cp in copies(*refs):
            cp.wait_send()

    return _Rider([blocks], [jax.ShapeDtypeStruct((4, r, c), blocks.dtype)], 4, ((0.0, start), (1.0, finish)))


def _chip_rider(partial, row0, rows):
    _, _, c = partial.shape

    def copies(ins, outs, send_sems, recv_sems, with_back):
        x, y, cc = _mesh_pos()
        my_chip = 2 * x + y

        def rows_of(j):
            return ins[0].at[j, pl.ds(row0, rows)]

        mine = pltpu.make_async_copy(rows_of(my_chip), outs[0].at[my_chip], send_sems.at[3])
        out, back = [], []
        for k in range(1, 4):
            px, py = x ^ (k >> 1), y ^ (k & 1)
            peer = 2 * px + py
            sems = dict(send_sem=send_sems.at[k - 1], recv_sem=recv_sems.at[k - 1], device_id=(px, py, cc),
                        device_id_type=MESH)
            out.append(pltpu.make_async_remote_copy(src_ref=rows_of(peer), dst_ref=outs[0].at[my_chip], **sems))
            if with_back:
                back.append(pltpu.make_async_remote_copy(src_ref=rows_of(my_chip), dst_ref=outs[0].at[peer], **sems))
        return mine, out, back

    def start(*refs):
        mine, out, _ = copies(*refs, False)
        mine.start()
        for cp in out:
            cp.start()

    def finish(*refs):
        mine, out, back = copies(*refs, True)
        for cp in back:
            cp.wait_recv()
        for cp in out:
            cp.wait_send()
        mine.wait()

    return _Rider([partial], [jax.ShapeDtypeStruct((4, rows, c), partial.dtype)], 4, ((0.0, start), (1.0, finish)))


def _matmul(a, b, mode, out_dtype, name, add=None, rider=None, b_rows=None, extra=None):
    b_shape = b.shape if b_rows is None else (b_rows[1], b.shape[1])
    if mode == "nn":
        (m, k), (k2, n) = a.shape, b_shape
    elif mode == "nt":
        (m, k), (n, k2) = a.shape, b_shape
    else:
        (k, m), (k2, n) = a.shape, b_shape
    assert k == k2, (a.shape, b_shape, mode)
    tm, tn, tk = min(m, MM_TILE), min(n, MM_TILE), min(k, MM_TK)
    if n % MM_TILE_WIDE == 0 and (k <= MM_TK or (add is None and jnp.dtype(out_dtype).itemsize == 2)):
        tn = MM_TILE_WIDE
    assert m % tm == 0 and n % tn == 0 and k % tk == 0, (m, n, k)
    nk = k // tk
    b_tile_rows = tn if mode == "nt" else tk
    b_off = 0 if b_rows is None else b_rows[0] // b_tile_rows
    assert b_rows is None or b_rows[0] % b_tile_rows == 0, (b_rows, b_tile_rows)
    dot = {"nn": _dot, "nt": _dot_nt, "tn": _dot_tn}[mode]

    gm, gn = m // tm, n // tn
    n_rin = len(rider.inputs) if rider else 0
    n_rout = len(rider.out_shapes) if rider else 0
    n_add = 1 if add is not None else 0

    n_extra = 2 if extra is not None else 0

    def body(*refs):
        a_ref, b_ref = refs[:2]
        add_ref = refs[2] if n_add else None
        extra_refs = refs[2 + n_add:2 + n_add + n_extra]
        first = 2 + n_add + n_extra
        r_ins = refs[first:first + n_rin]
        o_ref = refs[first + n_rin]
        r_outs = refs[first + n_rin + 1:first + n_rin + 1 + n_rout]
        scratch = refs[first + n_rin + 1 + n_rout:]
        acc_ref = scratch[0] if nk > 1 else None
        kk = pl.program_id(2)
        if rider:
            step = (pl.program_id(0) * gn + pl.program_id(1)) * nk + kk
            ride = functools.partial(rider.run, step, gm * gn * nk, r_ins, r_outs, scratch[-2], scratch[-1])
            ride(last=False)

        def finish(r):
            if extra_refs:
                r = r + _dot(_bf(extra_refs[0][...]), _bf(extra_refs[1][...]))
            if add_ref is not None:
                r = r + add_ref[...]
            o_ref[...] = r.astype(o_ref.dtype)

        if nk == 1:
            finish(dot(_bf(a_ref[...]), _bf(b_ref[...])))
        else:
            @pl.when(kk == 0)
            def _():
                acc_ref[...] = dot(_bf(a_ref[...]), _bf(b_ref[...]))

            @pl.when(jnp.logical_and(kk > 0, kk < nk - 1))
            def _():
                acc_ref[...] += dot(_bf(a_ref[...]), _bf(b_ref[...]))

            @pl.when(kk == nk - 1)
            def _():
                finish(acc_ref[...] + dot(_bf(a_ref[...]), _bf(b_ref[...])))
        if rider:
            ride(last=True)

    if mode == "nn":
        a_spec = pl.BlockSpec((tm, tk), lambda i, j, kk: (i, kk))
        b_spec = pl.BlockSpec((tk, tn), lambda i, j, kk: (kk + b_off, j))
    elif mode == "nt":
        a_spec = pl.BlockSpec((tm, tk), lambda i, j, kk: (i, kk))
        b_spec = pl.BlockSpec((tn, tk), lambda i, j, kk: (j + b_off, kk))
    else:
        a_spec = pl.BlockSpec((tk, tm), lambda i, j, kk: (kk, i))
        b_spec = pl.BlockSpec((tk, tn), lambda i, j, kk: (kk + b_off, j))
    o_spec = pl.BlockSpec((tm, tn), lambda i, j, kk: (i, j))
    in_specs = [a_spec, b_spec] + ([o_spec] if add is not None else [])
    args = (a, b) + ((add,) if add is not None else ())
    if extra is not None:
        (m2, k_extra), (k_extra2, n2) = extra[0].shape, extra[1].shape
        assert (m2, n2, k_extra) == (m, n, k_extra2), (extra[0].shape, extra[1].shape)
        in_specs += [pl.BlockSpec((tm, k_extra), lambda i, j, kk: (i, 0)),
                     pl.BlockSpec((k_extra, tn), lambda i, j, kk: (0, j))]
        args += tuple(extra)
    block_bytes = (tm * tk * a.dtype.itemsize + tk * tn * b.dtype.itemsize
                   + tm * tn * (jnp.dtype(out_dtype).itemsize + (4 if add is not None else 0)))
    vmem = min(2 * block_bytes + 2 * tm * tn * 4 + 2 * (tm + tn) * tk + (4 << 20), VMEM_CAP)
    out_shape = jax.ShapeDtypeStruct((m, n), out_dtype)
    scratch = [pltpu.VMEM((tm, tn), F32)] if nk > 1 else []
    if rider:
        in_specs = in_specs + rider.in_specs()
        args = args + rider.inputs
        out_shape = (out_shape, *rider.out_shapes)
        o_spec = (o_spec, *rider.out_specs())
        scratch = scratch + rider.scratch()
    sem = ("arbitrary",) * 3 if rider else ("parallel", "parallel", "arbitrary")
    return pl.pallas_call(
        body,
        out_shape=out_shape,
        grid=(gm, gn, nk),
        in_specs=in_specs,
        out_specs=o_spec,
        scratch_shapes=scratch,
        compiler_params=pltpu.CompilerParams(dimension_semantics=sem, vmem_limit_bytes=int(vmem)),
        name=name,
    )(*args)


def _row_spec(width, tile=ROW_TILE):
    return pl.BlockSpec((tile, width), lambda i: (i, 0))


def _full_spec(shape):
    nd = len(shape)
    return pl.BlockSpec(shape, lambda *_: (0,) * nd)


def _tiling_2d(r, c):
    if r <= ROW_TILE or r % ROW_TILE == 0:
        tr = min(r, ROW_TILE)
        return (tr, c), r // tr, (lambda i: (i, 0))
    tc = 128 if r > 4096 else 256
    assert c % tc == 0, (r, c)
    return (r, tc), c // tc, (lambda i: (0, i))


def _spec_2d(r, c):
    blk, grid, idx = _tiling_2d(r, c)
    return pl.BlockSpec(blk, idx), grid


def _cast_bf16(x, name):
    r, c = x.shape
    sp, grid = _spec_2d(r, c)

    def body(x_ref, o_ref):
        o_ref[...] = _bf(x_ref[...])

    return pl.pallas_call(
        body, out_shape=jax.ShapeDtypeStruct((r, c), BF16), grid=(grid,), in_specs=[sp], out_specs=sp,
        compiler_params=_params("parallel"), name=name)(x)


def _rmsnorm_rope_fwd(x, w, pos_col, inv_freq, to_bf16, name, rider):
    s, d = x.shape
    steps = s // ROW_TILE
    n_cast, n_rin, n_rout = len(to_bf16), len(rider.inputs), len(rider.out_shapes)

    def body(x_ref, w_ref, p_ref, f_ref, *refs):
        c_ins, refs = refs[:n_cast], refs[n_cast:]
        r_ins, refs = refs[:n_rin], refs[n_rin:]
        (o_ref, cos_ref, sin_ref), refs = refs[:3], refs[3:]
        c_outs, refs = refs[:n_cast], refs[n_cast:]
        ride = functools.partial(rider.run, pl.program_id(0), steps, r_ins, refs[:n_rout], refs[-2], refs[-1])
        ride(last=False)

        @pl.when(pl.program_id(0) == 0)
        def _():
            for c_in, c_out in zip(c_ins, c_outs):
                c_out[...] = _bf(c_in[...])

        xv = x_ref[...]
        rstd = lax.rsqrt(jnp.mean(xv * xv, axis=1, keepdims=True) + EPS)
        o_ref[...] = _bf(xv * rstd * w_ref[...])
        ang = p_ref[...].astype(F32) * f_ref[...]
        cos_ref[...] = jnp.cos(ang)
        sin_ref[...] = jnp.sin(ang)
        ride(last=True)

    table = jax.ShapeDtypeStruct((s, ROPE_HALF), F32)
    whole = [_full_spec(a.shape) for a in to_bf16]
    return pl.pallas_call(
        body,
        out_shape=(jax.ShapeDtypeStruct((s, d), BF16), table, table,
                   *[jax.ShapeDtypeStruct(a.shape, BF16) for a in to_bf16], *rider.out_shapes),
        grid=(steps,),
        in_specs=[_row_spec(d), _full_spec((1, d)), _row_spec(1), _full_spec((1, ROPE_HALF)), *whole,
                  *rider.in_specs()],
        out_specs=(_row_spec(d), _row_spec(ROPE_HALF), _row_spec(ROPE_HALF), *whole, *rider.out_specs()),
        scratch_shapes=rider.scratch(),
        compiler_params=_params("arbitrary"), name=name)(x, w, pos_col, inv_freq, *to_bf16, *rider.inputs)


def _rmsnorm_bwd(x, w, dh, dres, name):
    s, d = x.shape

    def body(x_ref, w_ref, dh_ref, dres_ref, dx_ref, dw_ref):
        @pl.when(pl.program_id(0) == 0)
        def _():
            dw_ref[...] = jnp.zeros_like(dw_ref)

        xv = x_ref[...]
        rstd = lax.rsqrt(jnp.mean(xv * xv, axis=1, keepdims=True) + EPS)
        xhat = xv * rstd
        dhv = dh_ref[...]
        dxhat = dhv * w_ref[...]
        dx = rstd * (dxhat - xhat * jnp.mean(dxhat * xhat, axis=1, keepdims=True))
        dx_ref[...] = dx + dres_ref[...]
        dw_ref[...] += jnp.sum(dhv * xhat, axis=0, keepdims=True)

    return pl.pallas_call(
        body,
        out_shape=(jax.ShapeDtypeStruct((s, d), F32), jax.ShapeDtypeStruct((1, d), F32)),
        grid=(s // ROW_TILE,),
        in_specs=[_row_spec(d), _full_spec((1, d)), _row_spec(d), _row_spec(d)],
        out_specs=(_row_spec(d), _full_spec((1, d))),
        compiler_params=_params("arbitrary"), name=name)(x, w, dh, dres)


def _merge_fwd(pg, p_r, p_s, name):
    s = pg.shape[0]

    def body(g_ref, r_ref, s_ref, o_ref):
        g = g_ref[...].astype(F32)
        o_ref[...] = _bf(_sigmoid(g[:, :D_MODEL]) * r_ref[...].astype(F32)
                         + _sigmoid(g[:, D_MODEL:]) * s_ref[...].astype(F32))

    return pl.pallas_call(
        body, out_shape=jax.ShapeDtypeStruct((s, D_MODEL), BF16), grid=(s // ROW_TILE,),
        in_specs=[_row_spec(2 * D_MODEL), _row_spec(D_MODEL), _row_spec(D_MODEL)],
        out_specs=_row_spec(D_MODEL), compiler_params=_params("parallel"), name=name)(pg, p_r, p_s)


def _merge_bwd(pg, p_r, p_s, dm, name):
    s = pg.shape[0]

    def body(g_ref, r_ref, s_ref, dm_ref, dr_ref, ds_ref, dg_ref):
        g = g_ref[...].astype(F32)
        sr, ss = _sigmoid(g[:, :D_MODEL]), _sigmoid(g[:, D_MODEL:])
        d = dm_ref[...]
        dr_ref[...] = _bf(d * sr)
        ds_ref[...] = _bf(d * ss)
        dg_ref[:, :D_MODEL] = _bf(d * r_ref[...].astype(F32) * sr * (1.0 - sr))
        dg_ref[:, D_MODEL:] = _bf(d * s_ref[...].astype(F32) * ss * (1.0 - ss))

    o = jax.ShapeDtypeStruct((s, D_MODEL), BF16)
    return pl.pallas_call(
        body, out_shape=(o, o, jax.ShapeDtypeStruct((s, 2 * D_MODEL), BF16)), grid=(s // ROW_TILE,),
        in_specs=[_row_spec(2 * D_MODEL), _row_spec(D_MODEL), _row_spec(D_MODEL), _row_spec(D_MODEL)],
        out_specs=(_row_spec(D_MODEL), _row_spec(D_MODEL), _row_spec(2 * D_MODEL)),
        compiler_params=_params("parallel"), name=name)(pg, p_r, p_s, dm)


def _final_fwd_bwd(x, o, w, target, name):
    s, d = x.shape

    def body(x_ref, o_ref, w_ref, t_ref, dx_ref, dw_ref, loss_ref):
        @pl.when(pl.program_id(0) == 0)
        def _():
            dw_ref[...] = jnp.zeros_like(dw_ref)
            loss_ref[...] = jnp.zeros_like(loss_ref)

        x2 = x_ref[...] + o_ref[...]
        rstd = lax.rsqrt(jnp.mean(x2 * x2, axis=1, keepdims=True) + EPS)
        xhat = x2 * rstd
        wv = w_ref[...]
        err = xhat * wv - t_ref[...]
        loss_ref[...] += jnp.sum(jnp.sum(err * err, axis=1, keepdims=True), axis=0, keepdims=True) * (0.5 / d)
        dy = err * (1.0 / d)
        dw_ref[...] += jnp.sum(dy * xhat, axis=0, keepdims=True)
        dxhat = dy * wv
        dx_ref[...] = rstd * (dxhat - xhat * jnp.mean(dxhat * xhat, axis=1, keepdims=True))

    return pl.pallas_call(
        body,
        out_shape=(jax.ShapeDtypeStruct((s, d), F32), jax.ShapeDtypeStruct((1, d), F32),
                   jax.ShapeDtypeStruct((8, 128), F32)),
        grid=(s // ROW_TILE,),
        in_specs=[_row_spec(d), _row_spec(d), _full_spec((1, d)), _row_spec(d)],
        out_specs=(_row_spec(d), _full_spec((1, d)), _full_spec((8, 128))),
        compiler_params=_params("arbitrary"), name=name)(x, o, w, target)


def _adamw(w, g, m, v, name):
    r, c = w.shape
    sp, grid = _spec_2d(r, c)
    c1 = 1.0 / (1.0 - ADAM_B1 ** ADAM_STEP)
    c2 = 1.0 / (1.0 - ADAM_B2 ** ADAM_STEP)

    def body(w_ref, g_ref, m_ref, v_ref, d_ref, nm_ref, nv_ref):
        gv = g_ref[...]
        nm = ADAM_B1 * m_ref[...] + (1.0 - ADAM_B1) * gv
        nv = ADAM_B2 * v_ref[...] + (1.0 - ADAM_B2) * (gv * gv)
        d_ref[...] = -ADAM_LR * ((nm * c1) / (jnp.sqrt(nv * c2) + ADAM_EPS) + ADAM_WD * w_ref[...])
        nm_ref[...] = nm
        nv_ref[...] = nv

    o = jax.ShapeDtypeStruct((r, c), F32)
    return pl.pallas_call(
        body, out_shape=(o, o, o), grid=(grid,), in_specs=[sp, sp, sp, sp], out_specs=(sp, sp, sp),
        compiler_params=_params("parallel"), name=name)(w, g, m, v)


def _sum_slots(land, name):
    n, r, c = land.shape
    blk, grid, idx = _tiling_2d(r, c)

    def body(l_ref, o_ref):
        acc = l_ref[0].astype(F32)
        for i in range(1, n):
            acc = acc + l_ref[i].astype(F32)
        o_ref[...] = acc

    return pl.pallas_call(
        body, out_shape=jax.ShapeDtypeStruct((r, c), F32), grid=(grid,),
        in_specs=[pl.BlockSpec((n, *blk), lambda i: (0, *idx(i)))], out_specs=pl.BlockSpec(blk, idx),
        compiler_params=_params("parallel"), name=name)(land)


SLAB_COLS = 256


def _chip_sum(blocks, from_sibling, my_core, name):
    _, r, c = blocks.shape

    def body(core_ref, a_ref, b_ref, o_ref):
        o_ref[...] = _bf(a_ref[...].astype(F32) + b_ref[...].astype(F32))

    def slab(chip_of):
        return pl.BlockSpec((1, r, SLAB_COLS), lambda j, i, core: (chip_of(j, core), 0, i))

    grid_spec = pltpu.PrefetchScalarGridSpec(
        num_scalar_prefetch=1, grid=(4, c // SLAB_COLS),
        in_specs=[slab(lambda j, core: 2 * j + core[0]), slab(lambda j, core: j)],
        out_specs=slab(lambda j, core: j))
    return pl.pallas_call(
        body, out_shape=jax.ShapeDtypeStruct((4, r, c), BF16), grid_spec=grid_spec,
        compiler_params=_params("parallel", "parallel"), name=name,
    )(my_core.astype(jnp.int32).reshape(1), blocks, from_sibling)


def _sum_row_parts(parts, name):
    c = parts[0].shape[2]
    rows = [p.shape[1] for p in parts]

    def body(*refs):
        o_ref, off = refs[-1], 0
        for p_ref, n in zip(refs[:-1], rows):
            acc = p_ref[0].astype(F32)
            for i in range(1, p_ref.shape[0]):
                acc = acc + p_ref[i].astype(F32)
            o_ref[off:off + n, :] = acc
            off += n

    return pl.pallas_call(
        body, out_shape=jax.ShapeDtypeStruct((sum(rows), c), F32), grid=(c // SLAB_COLS,),
        in_specs=[pl.BlockSpec((p.shape[0], p.shape[1], SLAB_COLS), lambda i: (0, 0, i)) for p in parts],
        out_specs=pl.BlockSpec((sum(rows), SLAB_COLS), lambda i: (0, i)),
        compiler_params=_params("parallel"), name=name)(*parts)


def _retention_tables():
    lg = np.log1p(-(2.0 ** (-5.0 - np.arange(RET_HEADS, dtype=np.float64))))
    idx = np.arange(CHUNK, dtype=np.float64)
    intra = np.exp(np.abs(idx[:, None] - idx[None, :])[None] * lg[:, None, None])
    qd = np.exp((idx[None, :] + 1.0) * lg[:, None])
    kd = np.exp((CHUNK - 1.0 - idx[None, :]) * lg[:, None])
    cd = np.exp(CHUNK * lg)
    ones = np.ones((1, 1, RET_DK))
    return (jnp.asarray(intra, F32), jnp.asarray(qd[:, :, None] * ones, F32),
            jnp.asarray(kd[:, :, None] * ones, F32), jnp.asarray(cd[:, None, None] * ones, F32))


def _rope(t, cos, sin):
    t1, t2 = t[:, :ROPE_HALF], t[:, ROPE_HALF:]
    return jnp.concatenate([t1 * cos - t2 * sin, t2 * cos + t1 * sin], axis=1)


def _rope_t(d, cos, sin):
    d1, d2 = d[:, :ROPE_HALF], d[:, ROPE_HALF:]
    return jnp.concatenate([d1 * cos + d2 * sin, d2 * cos - d1 * sin], axis=1)


_HEADS = range(RET_HEADS)


def _ret_chunk_fwd(p_ref, cos, sin, intra_ref, qd_ref, st):
    def seg(i, h):
        return p_ref[:, i * RET_W + h * RET_DK:i * RET_W + (h + 1) * RET_DK]

    v = [seg(2, h) for h in _HEADS]
    qr = [_rope(seg(0, h), cos, sin) for h in _HEADS]
    kr = [_rope(seg(1, h), cos, sin) * (RET_DK ** -0.5) for h in _HEADS]
    qrb, vb = [_bf(a) for a in qr], [_bf(a) for a in v]
    sc = [_dot_nt(qrb[h], _bf(kr[h])) * intra_ref[h] for h in _HEADS]
    qs = [_dot(qrb[h], _bf(st[h])) for h in _HEADS]
    y = [_dot(_bf(sc[h]), vb[h]) + qs[h] * qd_ref[h] for h in _HEADS]
    g = [seg(3, h) for h in _HEADS]
    return dict(v=v, vb=vb, qr=qr, qrb=qrb, kr=kr, sc=sc, y=y, g=g)


def _group_norm(y):
    mu = jnp.mean(y, axis=1, keepdims=True)
    yc = y - mu
    rstd = lax.rsqrt(jnp.mean(yc * yc, axis=1, keepdims=True) + EPS)
    return yc * rstd, rstd


def _chunk_rows(sub):
    return pl.ds(sub * CHUNK, CHUNK)


def _ret_specs(steps, rev):
    cidx = (lambda c: steps - 1 - c) if rev else (lambda c: c)
    return dict(
        proj=pl.BlockSpec((RET_STEP_ROWS, 4 * RET_W), lambda c: (cidx(c), 0)),
        half=pl.BlockSpec((RET_STEP_ROWS, ROPE_HALF), lambda c: (cidx(c), 0)),
        wide=pl.BlockSpec((RET_STEP_ROWS, RET_W), lambda c: (cidx(c), 0)),
        state=pl.BlockSpec((RET_CHUNKS_PER_STEP, RET_HEADS, RET_DK, RET_DK), lambda c: (cidx(c), 0, 0, 0)),
        intra=_full_spec((RET_HEADS, CHUNK, CHUNK)),
        dec=_full_spec((RET_HEADS, CHUNK, RET_DK)),
        cd=_full_spec((RET_HEADS, 1, RET_DK)),
    )


def _retention_fwd(p_r, cos, sin, name):
    s = p_r.shape[0]
    nc = s // CHUNK
    intra_t, qd_t, kd_t, cd_t = _retention_tables()

    def body(p_ref, cos_ref, sin_ref, intra_ref, qd_ref, kd_ref, cd_ref, y_ref, st_ref, state):
        @pl.when(pl.program_id(0) == 0)
        def _():
            state[...] = jnp.zeros_like(state)

        for sub in range(RET_CHUNKS_PER_STEP):
            rows = _chunk_rows(sub)
            y_v, st_v = y_ref.at[rows], st_ref.at[sub]
            st = [state[h] for h in _HEADS]
            f = _ret_chunk_fwd(p_ref.at[rows], cos_ref[rows, :], sin_ref[rows, :], intra_ref, qd_ref, st)
            new_st = [st[h] * cd_ref[h] + _dot_tn(_bf(f["kr"][h] * kd_ref[h]), f["vb"][h]) for h in _HEADS]
            out = [_bf(_group_norm(f["y"][h])[0] * (f["g"][h] * _sigmoid(f["g"][h]))) for h in _HEADS]
            for h in _HEADS:
                st_v[h] = _bf(st[h])
                state[h] = new_st[h]
                y_v[:, h * RET_DK:(h + 1) * RET_DK] = out[h]

    steps = nc // RET_CHUNKS_PER_STEP
    sp = _ret_specs(steps, False)
    return pl.pallas_call(
        body,
        out_shape=(jax.ShapeDtypeStruct((s, RET_W), BF16),
                   jax.ShapeDtypeStruct((nc, RET_HEADS, RET_DK, RET_DK), BF16)),
        grid=(steps,),
        in_specs=[sp["proj"], sp["half"], sp["half"], sp["intra"], sp["dec"], sp["dec"], sp["cd"]],
        out_specs=(sp["wide"], sp["state"]),
        scratch_shapes=[pltpu.VMEM((RET_HEADS, RET_DK, RET_DK), F32)],
        compiler_params=_params("arbitrary"), name=name)(p_r, cos, sin, intra_t, qd_t, kd_t, cd_t)


def _retention_bwd(p_r, cos, sin, states, dy_r, name):
    s = p_r.shape[0]
    nc = s // CHUNK
    intra_t, qd_t, kd_t, cd_t = _retention_tables()

    def body(p_ref, cos_ref, sin_ref, intra_ref, qd_ref, kd_ref, cd_ref, st_ref, dy_ref, dp_ref, dstate):
        @pl.when(pl.program_id(0) == 0)
        def _():
            dstate[...] = jnp.zeros_like(dstate)

        for sub in reversed(range(RET_CHUNKS_PER_STEP)):
            rows = _chunk_rows(sub)
            chunk(p_ref.at[rows], cos_ref[rows, :], sin_ref[rows, :], intra_ref, qd_ref, kd_ref, cd_ref,
                  st_ref.at[sub], dy_ref.at[rows], dp_ref.at[rows], dstate)

    def chunk(p_ref, cos, sin, intra_ref, qd_ref, kd_ref, cd_ref, st_ref, dy_ref, dp_ref, dstate):
        st = [st_ref[h] for h in _HEADS]
        dsn = [dstate[h] for h in _HEADS]
        f = _ret_chunk_fwd(p_ref, cos, sin, intra_ref, qd_ref, st)
        vb, qrb, kr, g = f["vb"], f["qrb"], f["kr"], f["g"]
        norm = [_group_norm(f["y"][h]) for h in _HEADS]
        sg = [_sigmoid(g[h]) for h in _HEADS]
        dyr = [dy_ref[:, h * RET_DK:(h + 1) * RET_DK] for h in _HEADS]
        dyn = [dyr[h] * (g[h] * sg[h]) for h in _HEADS]
        dg = [dyr[h] * norm[h][0] * (sg[h] * (1.0 + g[h] * (1.0 - sg[h]))) for h in _HEADS]
        dy = [norm[h][1] * (dyn[h] - jnp.mean(dyn[h], axis=1, keepdims=True)
                            - norm[h][0] * jnp.mean(dyn[h] * norm[h][0], axis=1, keepdims=True)) for h in _HEADS]
        dyb, dsnb = [_bf(a) for a in dy], [_bf(a) for a in dsn]
        ds = [_bf(_dot_nt(dyb[h], vb[h]) * intra_ref[h]) for h in _HEADS]
        t = [_bf(dy[h] * qd_ref[h]) for h in _HEADS]
        dv = [_dot_tn(_bf(f["sc"][h]), dyb[h]) + _dot(_bf(kr[h] * kd_ref[h]), dsnb[h]) for h in _HEADS]
        dqr = [_dot(ds[h], _bf(kr[h])) + _dot_nt(t[h], _bf(st[h])) for h in _HEADS]
        dkr = [_dot_tn(ds[h], qrb[h]) + _dot_nt(vb[h], dsnb[h]) * kd_ref[h] for h in _HEADS]
        new_ds = [dsn[h] * cd_ref[h] + _dot_tn(qrb[h], t[h]) for h in _HEADS]
        for h in _HEADS:
            lo = h * RET_DK
            dstate[h] = new_ds[h]
            dp_ref[:, lo:lo + RET_DK] = _bf(_rope_t(dqr[h], cos, sin))
            dp_ref[:, RET_W + lo:RET_W + lo + RET_DK] = _bf(_rope_t(dkr[h], cos, sin) * (RET_DK ** -0.5))
            dp_ref[:, 2 * RET_W + lo:2 * RET_W + lo + RET_DK] = _bf(dv[h])
            dp_ref[:, 3 * RET_W + lo:3 * RET_W + lo + RET_DK] = _bf(dg[h])

    steps = nc // RET_CHUNKS_PER_STEP
    sp = _ret_specs(steps, True)
    return pl.pallas_call(
        body,
        out_shape=jax.ShapeDtypeStruct((s, 4 * RET_W), BF16),
        grid=(steps,),
        in_specs=[sp["proj"], sp["half"], sp["half"], sp["intra"], sp["dec"], sp["dec"], sp["cd"],
                  sp["state"], sp["wide"]],
        out_specs=sp["proj"],
        scratch_shapes=[pltpu.VMEM((RET_HEADS, RET_DK, RET_DK), F32)],
        compiler_params=_params("arbitrary"), name=name)(p_r, cos, sin, intra_t, qd_t, kd_t, cd_t, states, dy_r)


CONV_SLAB = 256
_CONV_SLABS = [slice(i * CONV_SLAB, (i + 1) * CONV_SLAB) for i in range(SSD_CD // CONV_SLAB)]


def _conv_taps(ext, w):
    acc = w[SSD_CONV - 1:SSD_CONV] * ext
    for j in range(SSD_CONV - 1):
        acc = acc + w[j:j + 1] * pltpu.roll(ext, SSD_CONV - 1 - j, axis=0)
    return acc


def _conv_fwd(xbc_raw, conv_w, conv_b, name):
    s = xbc_raw.shape[0]
    t8 = CONV_TILE // 8

    def body(cur_ref, prev_ref, w_ref, b_ref, o_ref, slope_ref):
        keep = (pl.program_id(0) > 0).astype(F32)
        for sl in _CONV_SLABS:
            ext = jnp.concatenate([prev_ref[:, sl] * keep, cur_ref[:, sl]], axis=0)
            u = _conv_taps(ext, w_ref[:, sl])[8:] + b_ref[:, sl]
            sg = _sigmoid(u)
            o_ref[:, sl] = u * sg
            slope_ref[:, sl] = _bf(sg * (1.0 + u * (1.0 - sg)))

    out = jax.ShapeDtypeStruct((s, SSD_CD), F32)
    return pl.pallas_call(
        body, out_shape=(out, jax.ShapeDtypeStruct((s, SSD_CD), BF16)), grid=(s // CONV_TILE,),
        in_specs=[_row_spec(SSD_CD, CONV_TILE),
                  pl.BlockSpec((8, SSD_CD), lambda i: (jnp.maximum(i * t8 - 1, 0), 0)),
                  _full_spec((SSD_CONV, SSD_CD)), _full_spec((1, SSD_CD))],
        out_specs=(_row_spec(SSD_CD, CONV_TILE), _row_spec(SSD_CD, CONV_TILE)),
        compiler_params=_params("parallel"), name=name)(xbc_raw, xbc_raw, conv_w, conv_b)


def _conv_bwd(xbc_raw, conv_w, slope, dact, name):
    s = xbc_raw.shape[0]
    nt = s // CONV_TILE
    t8 = CONV_TILE // 8
    rows = CONV_TILE + 8

    def body(cur_ref, w_ref, s_ref, snext_ref, d_ref, dnext_ref, dx_ref, dw_ref, db_ref):
        i = pl.program_id(0)

        @pl.when(i == 0)
        def _():
            dw_ref[...] = jnp.zeros_like(dw_ref)
            db_ref[...] = jnp.zeros_like(db_ref)

        keep_next = (i < nt - 1).astype(F32)
        for sl in _CONV_SLABS:
            w = w_ref[:, sl]
            cur = cur_ref[:, sl]
            duc = d_ref[:, sl].astype(F32) * s_ref[:, sl].astype(F32)
            after = (dnext_ref[:, sl].astype(F32) * snext_ref[:, sl].astype(F32))[:8] * keep_next
            du = jnp.concatenate([duc, after], axis=0)
            dx = w[SSD_CONV - 1:SSD_CONV] * duc
            for j in range(SSD_CONV - 1):
                ahead = pltpu.roll(du, rows - (SSD_CONV - 1 - j), axis=0)[:CONV_TILE]
                dx = dx + w[j:j + 1] * ahead
                dw_ref[j:j + 1, sl] += jnp.sum(cur * ahead, axis=0, keepdims=True)
            dx_ref[:, sl] = _bf(dx)
            dw_ref[SSD_CONV - 1:SSD_CONV, sl] += jnp.sum(cur * duc, axis=0, keepdims=True)
            db_ref[:, sl] += jnp.sum(duc, axis=0, keepdims=True)

    row = _row_spec(SSD_CD, CONV_TILE)
    nxt16 = pl.BlockSpec((16, SSD_CD), lambda i: (jnp.minimum((i + 1) * (t8 // 2), s // 16 - 1), 0))
    return pl.pallas_call(
        body,
        out_shape=(jax.ShapeDtypeStruct((s, SSD_CD), BF16), jax.ShapeDtypeStruct((SSD_CONV, SSD_CD), F32),
                   jax.ShapeDtypeStruct((1, SSD_CD), F32)),
        grid=(nt,),
        in_specs=[row, _full_spec((SSD_CONV, SSD_CD)), row, nxt16, row, nxt16],
        out_specs=(row, _full_spec((SSD_CONV, SSD_CD)), _full_spec((1, SSD_CD))),
        compiler_params=_params("arbitrary"), name=name)(xbc_raw, conv_w, slope, slope, dact, dact)


def _head_select():
    e = np.zeros((HEAD_PAD, SSD_W), np.float32)
    for h in range(SSD_HEADS):
        e[h, h * SSD_P:(h + 1) * SSD_P] = 1.0
    return jnp.asarray(e, BF16), jnp.asarray(e.T, BF16)


def _pad_heads(v):
    return jnp.pad(v.reshape(1, SSD_HEADS).astype(F32), ((0, 0), (0, HEAD_PAD - SSD_HEADS)))


def _ssd_masks():
    r = _iota((CHUNK, SSD_QW), 0)
    c = _iota((CHUNK, SSD_QW), 1) % SSD_P
    itile = (r == c).astype(F32)
    ctile = r >= c
    rb = _iota((SSD_QW, SSD_QW), 0) // SSD_P
    cb = _iota((SSD_QW, SSD_QW), 1) // SSD_P
    return itile, ctile, rb == cb


_TILES = [slice(i * SSD_QW, (i + 1) * SSD_QW) for i in range(SSD_W // SSD_QW)]


def _ssd_heads(dtr_ref, dtb_ref, alog_ref):
    u = dtr_ref[...] + dtb_ref[...]
    dt = _softplus(u)
    nexp = -jnp.exp(alog_ref[...])
    return u, dt, nexp, dt * nexp


def _ssd_group(xbc_ref, acol_ref, dte_ref, gs):
    acol = acol_ref[:, gs]
    alast = acol[CHUNK - 1:CHUNK, :]
    xs, dte = xbc_ref[:, gs], dte_ref[:, gs]
    return dict(xs=xs, dte=dte, xdt=xs * dte, ea=jnp.exp(acol), tail=jnp.exp(alast - acol), eal=jnp.exp(alast))


def _silu_gate(z_ref, y_ref, tl):
    zv = z_ref[:, tl]
    sz = _sigmoid(zv)
    return zv, sz, y_ref[:, tl] * (zv * sz)


def _tile4(x):
    return jnp.concatenate([x, x, x, x], axis=0)


def _fold4(x):
    return x[0:CHUNK] + x[CHUNK:2 * CHUNK] + x[2 * CHUNK:3 * CHUNK] + x[3 * CHUNK:4 * CHUNK]


def _ssd_specs(steps, rev):
    cidx = (lambda c: steps - 1 - c) if rev else (lambda c: c)
    return dict(
        xbc=pl.BlockSpec((STEP_ROWS, SSD_CD), lambda c: (cidx(c), 0)),
        dt=pl.BlockSpec((STEP_ROWS, HEAD_PAD), lambda c: (cidx(c), 0)),
        wide=pl.BlockSpec((STEP_ROWS, SSD_W), lambda c: (cidx(c), 0)),
        state=pl.BlockSpec((CHUNKS_PER_STEP, SSD_N, SSD_W), lambda c: (cidx(c), 0, 0)),
        head=_full_spec((1, HEAD_PAD)),
        roww=_full_spec((1, SSD_W)),
        e=_full_spec((HEAD_PAD, SSD_W)),
        et=_full_spec((SSD_W, HEAD_PAD)),
    )


_GROUPS = range(SSD_GROUPS)
_GROUP_LANES = [slice(g * SSD_GW, (g + 1) * SSD_GW) for g in _GROUPS]
_QUADS = [(g, slice(g * SSD_GW + q * SSD_QW, g * SSD_GW + (q + 1) * SSD_QW), slice(q * SSD_QW, (q + 1) * SSD_QW))
          for g in _GROUPS for q in range(SSD_GW // SSD_QW)]


def _ssd_bc(xbc_ref):
    bg = [_bf(xbc_ref[:, B_OFF + g * SSD_N:B_OFF + (g + 1) * SSD_N]) for g in _GROUPS]
    cg = [_bf(xbc_ref[:, C_OFF + g * SSD_N:C_OFF + (g + 1) * SSD_N]) for g in _GROUPS]
    return bg, cg


def _ssd_decay(aq, itile, ctile):
    arow = jnp.sum(aq * itile, axis=0, keepdims=True)
    return jnp.exp(jnp.where(ctile, aq - arow, -jnp.inf))


def _ssd_blockdiag(xq, bdmask):
    return jnp.where(bdmask, _tile4(_bf(xq)), jnp.zeros((), BF16))


def _ssd_fwd(xbc, dt_raw, z, dt_bias, a_log, dske, norm_w, name):
    s = xbc.shape[0]
    nc = s // CHUNK
    e_sel, _ = _head_select()

    def body(xbc_ref, dtr_ref, z_ref, dtb_ref, alog_ref, dske_ref, nw_ref, e_ref,
             yraw_ref, ys_ref, st_ref, acol_ref, dte_ref, state):
        @pl.when(pl.program_id(0) == 0)
        def _():
            state[...] = jnp.zeros_like(state)

        for sub in range(CHUNKS_PER_STEP):
            rows = _chunk_rows(sub)
            chunk(xbc_ref.at[rows], dtr_ref.at[rows], z_ref.at[rows], dtb_ref, alog_ref, dske_ref, nw_ref, e_ref,
                  yraw_ref.at[rows], ys_ref.at[rows], st_ref.at[sub], acol_ref.at[rows], dte_ref.at[rows], state)

    def chunk(xbc_ref, dtr_ref, z_ref, dtb_ref, alog_ref, dske_ref, nw_ref, e_ref,
              yraw_ref, ys_ref, st_ref, acol_ref, dte_ref, state):
        _, dt, _, a = _ssd_heads(dtr_ref, dtb_ref, alog_ref)
        tril = _bf((_iota((CHUNK, CHUNK), 0) >= _iota((CHUNK, CHUNK), 1)).astype(F32))
        ac3, dt3 = _split3(_dot_exact_r(tril, a)), _split3(dt)
        for tl in _TILES:
            e_t = e_ref[:, tl]
            acol_ref[:, tl] = _dot(ac3[0], e_t) + _dot(ac3[1], e_t) + _dot(ac3[2], e_t)
            dte_ref[:, tl] = _dot(dt3[0], e_t) + _dot(dt3[1], e_t)

        itile, ctile, bdmask = _ssd_masks()
        st_ref[...] = state[...]
        bg, cg = _ssd_bc(xbc_ref)
        for g0 in range(0, SSD_GROUPS, SSD_BLOCK_GROUPS):
            gids = range(g0, g0 + SSD_BLOCK_GROUPS)
            quads = [(g, ql, qs) for g, ql, qs in _QUADS if g in gids]
            q = {g: _ssd_group(xbc_ref, acol_ref, dte_ref, _GROUP_LANES[g]) for g in gids}
            stg = {g: state[:, _GROUP_LANES[g]] for g in gids}
            cbt = {g: _dot_nt(cg[g], _tile4(bg[g])) for g in gids}
            ys = {g: _dot(cg[g], _bf(stg[g])) for g in gids}
            dq = [_ssd_decay(acol_ref[:, ql], itile, ctile) for _, ql, _ in quads]
            xbd = [_ssd_blockdiag(q[g]["xdt"][:, qs], bdmask) for g, _, qs in quads]
            yq = [_dot(_bf(cbt[g] * dq[i]), xbd[i]) + ys[g][:, qs] * q[g]["ea"][:, qs]
                  + dske_ref[:, ql] * q[g]["xs"][:, qs] for i, (g, ql, qs) in enumerate(quads)]
            new_st = {g: stg[g] * q[g]["eal"] + _dot_tn(bg[g], _bf(q[g]["xdt"] * q[g]["tail"])) for g in gids}
            for i, (_, ql, _) in enumerate(quads):
                yraw_ref[:, ql] = yq[i]
            for g in gids:
                state[:, _GROUP_LANES[g]] = new_st[g]

        sq = jnp.zeros((CHUNK, SSD_QW), F32)
        for tl in _TILES:
            t = _silu_gate(z_ref, yraw_ref, tl)[2]
            sq = sq + t * t
        rstd = lax.rsqrt(jnp.sum(sq, axis=1, keepdims=True) * (1.0 / SSD_W) + EPS)
        for tl in _TILES:
            ys_ref[:, tl] = _bf(_silu_gate(z_ref, yraw_ref, tl)[2] * rstd * nw_ref[:, tl])

    steps = nc // CHUNKS_PER_STEP
    sp = _ssd_specs(steps, False)
    wide = jax.ShapeDtypeStruct((s, SSD_W), F32)
    return pl.pallas_call(
        body,
        out_shape=(wide, jax.ShapeDtypeStruct((s, SSD_W), BF16), jax.ShapeDtypeStruct((nc, SSD_N, SSD_W), F32),
                   wide, wide),
        grid=(steps,),
        in_specs=[sp["xbc"], sp["dt"], sp["wide"], sp["head"], sp["head"], sp["roww"], sp["roww"], sp["e"]],
        out_specs=(sp["wide"], sp["wide"], sp["state"], sp["wide"], sp["wide"]),
        scratch_shapes=[pltpu.VMEM((SSD_N, SSD_W), F32)],
        compiler_params=_params("arbitrary"), name=name,
    )(xbc, dt_raw, z, dt_bias, a_log, dske, norm_w, e_sel)


def _ssd_bwd(xbc, dt_raw, z, yraw, states, dys, acol, dte, dt_bias, a_log, dske, norm_w, name, rider):
    s = xbc.shape[0]
    nc = s // CHUNK
    steps = nc // CHUNKS_PER_STEP
    _, et_sel = _head_select()
    n_in, n_out, n_scratch = 13, 7, 5
    n_rin, n_rout = len(rider.inputs), len(rider.out_shapes)

    def body(*refs):
        ins, refs = refs[:n_in], refs[n_in:]
        r_ins, refs = refs[:n_rin], refs[n_rin:]
        outs, refs = refs[:n_out], refs[n_out:]
        r_outs, refs = refs[:n_rout], refs[n_rout:]
        ride = functools.partial(rider.run, pl.program_id(0), steps, r_ins, r_outs, refs[n_scratch],
                                 refs[n_scratch + 1])
        ride(last=False)
        compute(*ins, *outs, *refs[:n_scratch])
        ride(last=True)

    def compute(xbc_ref, dtr_ref, z_ref, yraw_ref, st_ref, dys_ref, acol_ref, dte_ref, dtb_ref, alog_ref,
                dske_ref, nw_ref, et_ref, dxbc_ref, dz_ref, ddtr_ref, ddtb_ref, dalog_ref, ddsk_ref, dnw_ref,
                dstate, dsk_acc, dy_s, dacol_s, ddte_s):
        @pl.when(pl.program_id(0) == 0)
        def _():
            dstate[...] = jnp.zeros_like(dstate)
            dsk_acc[...] = jnp.zeros_like(dsk_acc)
            ddtb_ref[...] = jnp.zeros_like(ddtb_ref)
            dalog_ref[...] = jnp.zeros_like(dalog_ref)
            dnw_ref[...] = jnp.zeros_like(dnw_ref)

        for sub in reversed(range(CHUNKS_PER_STEP)):
            rows = _chunk_rows(sub)
            chunk(xbc_ref.at[rows], dtr_ref.at[rows], z_ref.at[rows], yraw_ref.at[rows], st_ref.at[sub],
                  dys_ref.at[rows], acol_ref.at[rows], dte_ref.at[rows], dtb_ref, alog_ref, dske_ref, nw_ref, et_ref,
                  dxbc_ref.at[rows], dz_ref.at[rows], ddtr_ref.at[rows], ddtb_ref, dalog_ref, dnw_ref,
                  dstate, dsk_acc, dy_s, dacol_s, ddte_s)

        @pl.when(pl.program_id(0) == steps - 1)
        def _():
            ddsk_ref[...] = _dot_exact_l(jnp.broadcast_to(dsk_acc[...], (8, SSD_W)), et_ref[...])[0:1]

    def chunk(xbc_ref, dtr_ref, z_ref, yraw_ref, st_ref, dys_ref, acol_ref, dte_ref, dtb_ref, alog_ref,
              dske_ref, nw_ref, et_ref, dxbc_ref, dz_ref, ddtr_ref, ddtb_ref, dalog_ref, dnw_ref,
              dstate, dsk_acc, dy_s, dacol_s, ddte_s):
        itile, ctile, bdmask = _ssd_masks()
        last_row = (_iota((CHUNK, 1), 0) == CHUNK - 1).astype(F32)

        s1 = jnp.zeros((CHUNK, SSD_QW), F32)
        s2 = jnp.zeros((CHUNK, SSD_QW), F32)
        for tl in _TILES:
            t = _silu_gate(z_ref, yraw_ref, tl)[2]
            s1 = s1 + t * t
            s2 = s2 + dys_ref[:, tl] * nw_ref[:, tl] * t
        rstd = lax.rsqrt(jnp.sum(s1, axis=1, keepdims=True) * (1.0 / SSD_W) + EPS)
        back = rstd * rstd * rstd * jnp.sum(s2, axis=1, keepdims=True) * (1.0 / SSD_W)
        for tl in _TILES:
            zv, sz, t = _silu_gate(z_ref, yraw_ref, tl)
            dysv = dys_ref[:, tl]
            dt_ = rstd * (dysv * nw_ref[:, tl]) - t * back
            dnw_ref[:, tl] += jnp.sum(dysv * t * rstd, axis=0, keepdims=True)
            dz_ref[:, tl] = _bf(dt_ * yraw_ref[:, tl] * (sz * (1.0 + zv * (1.0 - sz))))
            dy_t = dt_ * (zv * sz)
            dy_s[:, tl] = dy_t
            dsk_acc[:, tl] += jnp.sum(dy_t * xbc_ref[:, tl], axis=0, keepdims=True)

        gl = _GROUP_LANES
        bg, cg = _ssd_bc(xbc_ref)
        for g0 in range(0, SSD_GROUPS, SSD_BLOCK_GROUPS):
            gids = range(g0, g0 + SSD_BLOCK_GROUPS)
            quads = [(i, g, ql, qs) for i, (g, ql, qs) in enumerate(_QUADS) if g in gids]
            q = {g: _ssd_group(xbc_ref, acol_ref, dte_ref, gl[g]) for g in gids}
            stg = {g: st_ref[:, gl[g]] for g in gids}
            dsn = {g: dstate[:, gl[g]] for g in gids}
            stgb, dsnb = {g: _bf(stg[g]) for g in gids}, {g: _bf(dsn[g]) for g in gids}
            btile = {g: _tile4(bg[g]) for g in gids}
            cbt = {g: _dot_nt(cg[g], btile[g]) for g in gids}
            dyg = {g: dy_s[:, gl[g]] for g in gids}
            eag, tailg = {g: q[g]["ea"] for g in gids}, {g: q[g]["tail"] for g in gids}
            xdtg, ealg = {g: q[g]["xdt"] for g in gids}, {g: q[g]["eal"] for g in gids}
            dys_g = {g: _bf(dyg[g] * eag[g]) for g in gids}
            ysg = {g: _dot(cg[g], stgb[g]) for g in gids}
            dc0 = {g: _dot_nt(dys_g[g], stgb[g]) for g in gids}
            dst = {g: _dot_tn(cg[g], dys_g[g]) for g in gids}
            dwt = {g: _dot(bg[g], dsnb[g]) for g in gids}
            db0 = {g: _dot_nt(_bf(xdtg[g] * tailg[g]), dsnb[g]) for g in gids}
            dtl = {g: dwt[g] * xdtg[g] * tailg[g] for g in gids}
            dal_row = {g: jnp.sum(dtl[g], axis=0, keepdims=True)
                       + jnp.sum(dsn[g] * stg[g], axis=0, keepdims=True) * ealg[g] for g in gids}
            for g in gids:
                dstate[:, gl[g]] = dst[g] + dsn[g] * ealg[g]
            dq = {i: _ssd_decay(acol_ref[:, ql], itile, ctile) for i, _, ql, _ in quads}
            mq = {i: cbt[g] * dq[i] for i, g, _, _ in quads}
            xbd = {i: _ssd_blockdiag(xdtg[g][:, qs], bdmask) for i, g, _, qs in quads}
            dyq = {i: _bf(dyg[g][:, qs]) for i, g, _, qs in quads}
            dm = {i: _dot_nt(dyq[i], xbd[i]) for i, _, _, _ in quads}
            dxdt_q = {i: _fold4(jnp.where(bdmask, _dot_tn(_bf(mq[i]), dyq[i]), 0.0)) for i, _, _, _ in quads}
            eq = {i: dm[i] * mq[i] for i, _, _, _ in quads}
            dacol_q = {i: eq[i] - itile * jnp.sum(eq[i], axis=0, keepdims=True) for i, _, _, _ in quads}
            dcbt = {i: _bf(dm[i] * dq[i]) for i, _, _, _ in quads}
            dc_q = {i: _dot(dcbt[i], btile[g]) for i, g, _, _ in quads}
            db_q = {i: _fold4(_dot_tn(dcbt[i], cg[g])) for i, g, _, _ in quads}
            for g in gids:
                dxdt_g = jnp.concatenate([dxdt_q[2 * g], dxdt_q[2 * g + 1]], axis=1) + dwt[g] * tailg[g]
                dxbc_ref[:, gl[g]] = _bf(dyg[g] * dske_ref[:, gl[g]] + dxdt_g * q[g]["dte"])
                ddte_s[:, gl[g]] = dxdt_g * q[g]["xs"]
                dacol_s[:, gl[g]] = (jnp.concatenate([dacol_q[2 * g], dacol_q[2 * g + 1]], axis=1)
                                     + dyg[g] * (ysg[g] * eag[g]) - dtl[g] + last_row * dal_row[g])
                dxbc_ref[:, B_OFF + g * SSD_N:B_OFF + (g + 1) * SSD_N] = _bf(db0[g] + db_q[2 * g] + db_q[2 * g + 1])
                dxbc_ref[:, C_OFF + g * SSD_N:C_OFF + (g + 1) * SSD_N] = _bf(dc0[g] + dc_q[2 * g] + dc_q[2 * g + 1])

        dacum = jnp.zeros((CHUNK, HEAD_PAD), F32)
        ddt = jnp.zeros((CHUNK, HEAD_PAD), F32)
        for i, tl in enumerate(_TILES):
            et_t = et_ref[i * SSD_QW:(i + 1) * SSD_QW, :]
            dacum = dacum + _dot_exact_l(dacol_s[:, tl], et_t, pieces=2)
            ddt = ddt + _dot_exact_l(ddte_s[:, tl], et_t, pieces=2)
        u, _, nexp, a = _ssd_heads(dtr_ref, dtb_ref, alog_ref)
        triu = _bf((_iota((CHUNK, CHUNK), 1) >= _iota((CHUNK, CHUNK), 0)).astype(F32))
        da = _dot_exact_r(triu, dacum)
        ddt = ddt + da * nexp
        dalog_ref[...] += jnp.sum(da * a, axis=0, keepdims=True)
        du = ddt * _sigmoid(u)
        ddtr_ref[...] = _bf(du)
        ddtb_ref[...] += jnp.sum(du, axis=0, keepdims=True)

    sp = _ssd_specs(steps, True)
    return pl.pallas_call(
        body,
        out_shape=(jax.ShapeDtypeStruct((s, SSD_CD), BF16), jax.ShapeDtypeStruct((s, SSD_W), BF16),
                   jax.ShapeDtypeStruct((s, HEAD_PAD), BF16), jax.ShapeDtypeStruct((1, HEAD_PAD), F32),
                   jax.ShapeDtypeStruct((1, HEAD_PAD), F32), jax.ShapeDtypeStruct((1, HEAD_PAD), F32),
                   jax.ShapeDtypeStruct((1, SSD_W), F32), *rider.out_shapes),
        grid=(steps,),
        in_specs=[sp["xbc"], sp["dt"], sp["wide"], sp["wide"], sp["state"], sp["wide"], sp["wide"], sp["wide"],
                  sp["head"], sp["head"], sp["roww"], sp["roww"], sp["et"], *rider.in_specs()],
        out_specs=(sp["xbc"], sp["wide"], sp["dt"], sp["head"], sp["head"], sp["head"], sp["roww"],
                   *rider.out_specs()),
        scratch_shapes=[pltpu.VMEM((SSD_N, SSD_W), F32), pltpu.VMEM((1, SSD_W), F32),
                        *[pltpu.VMEM((CHUNK, SSD_W), F32)] * 3, *rider.scratch()],
        compiler_params=_params("arbitrary"), name=name,
    )(xbc, dt_raw, z, yraw, states, dys, acol, dte, dt_bias, a_log, dske, norm_w, et_sel, *rider.inputs)


W_IN_SHARD = IN_PROJ // N_DEV
W_IN_ROW_PARTS = ((0, 1152), (1152, 832), (1984, 840))


def _device_step(x, pos_col, target, norm1_w, conv_w_shard, conv_b, dt_bias, a_log, d_skip, ssd_norm_w, norm_f_w,
                 w_in_shard, br_ret_shard, br_ssd_shard, out_shard):
    inv_freq = jnp.asarray(ROPE_THETA ** (-np.arange(ROPE_HALF, dtype=np.float64) / ROPE_HALF), F32).reshape(1, ROPE_HALF)
    dtb, alog = _pad_heads(dt_bias), _pad_heads(a_log)
    dske = jnp.repeat(d_skip.reshape(SSD_HEADS).astype(F32), SSD_P).reshape(1, SSD_W)
    my_core = lax.axis_index("c")

    h, cos, sin, br_ret_shard, br_ssd_shard, out_shard, w_all = _rmsnorm_rope_fwd(
        x, norm1_w, pos_col, inv_freq, [br_ret_shard, br_ssd_shard, out_shard], "rmsnorm1_fwd",
        _gather_rider([w_in_shard], 1.0, 1.0))
    w_all = w_all.reshape(IN_PROJ, D_MODEL)
    w_dt = jnp.pad(w_all[OFF_DT:OFF_G], ((0, HEAD_PAD - SSD_HEADS), (0, 0)))
    w_g = w_all[OFF_G:]
    rows_r, rows_z, rows_xbc = (0, OFF_Z), (OFF_Z, SSD_W), (OFF_XBC, SSD_CD)
    gather = _gather_rider([br_ret_shard, br_ssd_shard, out_shard, conv_w_shard], relay_at=0.35, pass_at=0.6)
    p_r, all_ret, all_ssd, all_out, all_conv = _matmul(h, w_all, "nt", F32, "proj_ret", rider=gather, b_rows=rows_r)
    w_br_ret = all_ret.reshape(RET_W, D_MODEL)
    w_br_ssd = all_ssd.reshape(SSD_W, D_MODEL)
    w_out = all_out.reshape(D_MODEL, D_MODEL)
    conv_w = all_conv.transpose(1, 0, 2).reshape(SSD_CONV, SSD_CD)
    p_z = _matmul(h, w_all, "nt", F32, "proj_z", b_rows=rows_z)
    p_xbc = _matmul(h, w_all, "nt", F32, "proj_xbc", b_rows=rows_xbc)
    p_dt = _matmul(h, w_dt, "nt", F32, "proj_dt")
    p_g = _matmul(h, w_g, "nt", BF16, "proj_gates")
    y_r, ret_states = _retention_fwd(p_r, cos, sin, "retention_fwd")
    xbc_act, silu_slope = _conv_fwd(p_xbc, conv_w, conv_b, "conv_fwd")
    y_raw, y_s, ssd_states, acol, dte = _ssd_fwd(xbc_act, p_dt, p_z, dtb, alog, dske, ssd_norm_w, "ssd_fwd")
    pr = _matmul(y_r, w_br_ret, "nn", BF16, "branch_ret")
    ps = _matmul(y_s, w_br_ssd, "nn", BF16, "branch_ssd")
    merged = _merge_fwd(p_g, pr, ps, "merge_fwd")
    o = _matmul(merged, w_out, "nn", F32, "out_proj")
    dx2, g_norm_f, loss_acc = _final_fwd_bwd(x, o, norm_f_w, target, "final_norm_loss")

    g_w_out = _matmul(merged, dx2, "tn", BF16, "grad_w_out")
    dmerged = _matmul(dx2, w_out, "nt", F32, "d_merged")
    dpr, dps, dp_g = _merge_bwd(p_g, pr, ps, dmerged, "merge_bwd")
    g_w_br_ret = _matmul(y_r, dpr, "tn", BF16, "grad_w_br_ret")
    g_w_br_ssd = _matmul(y_s, dps, "tn", BF16, "grad_w_br_ssd")
    dy_r = _matmul(dpr, w_br_ret, "nt", F32, "d_y_ret")
    dy_s = _matmul(dps, w_br_ssd, "nt", F32, "d_y_ssd")
    scatter = _scatter_rider([g_w_out.reshape(N_DEV, -1, D_MODEL), g_w_br_ret.reshape(N_DEV, -1, D_MODEL),
                              g_w_br_ssd.reshape(N_DEV, -1, D_MODEL)])
    dxbc_act, dp_z, dp_dt, g_dtb, g_alog, g_dsk, g_ssd_norm, got_out, got_ret, got_ssd = _ssd_bwd(
        xbc_act, p_dt, p_z, y_raw, ssd_states, dy_s, acol, dte, dtb, alog, dske, ssd_norm_w, "ssd_bwd", scatter)
    dp_xbc, g_conv_w, g_conv_b = _conv_bwd(p_xbc, conv_w, silu_slope, dxbc_act, "conv_bwd")
    dp_r = _retention_bwd(p_r, cos, sin, ret_states, dy_r, "retention_bwd")
    g_w_in = jnp.concatenate([
        _matmul(dp_r, h, "tn", BF16, "grad_w_ret"),
        _matmul(dp_z, h, "tn", BF16, "grad_w_z"),
        _matmul(dp_xbc, h, "tn", BF16, "grad_w_xbc"),
        _matmul(dp_dt, h, "tn", BF16, "grad_w_dt")[:SSD_HEADS],
        _matmul(dp_g, h, "tn", BF16, "grad_w_gates"),
    ], axis=0)
    blocks = g_w_in.reshape(N_DEV, W_IN_SHARD, D_MODEL)
    dh, from_sibling = _matmul(dp_r, w_all, "nn", F32, "d_h_ret", rider=_sibling_rider(blocks), b_rows=rows_r)
    chip_sum = _chip_sum(blocks, from_sibling, my_core, "w_in_chip_sum")
    carriers = (("d_h_xbc", dp_xbc, w_all, rows_xbc), ("d_h_gates", dp_g, w_g, None), ("d_h_z", dp_z, w_all, rows_z))
    landed = []
    for (row0, rows), (nm, dp, w, b_rows) in zip(W_IN_ROW_PARTS, carriers):
        dt_pair = (dp_dt, w_dt) if nm == "d_h_z" else None
        dh, got = _matmul(dp, w, "nn", F32, nm, add=dh, rider=_chip_rider(chip_sum, row0, rows), b_rows=b_rows,
                          extra=dt_pair)
        landed.append(got)
    grad_x, g_norm1 = _rmsnorm_bwd(x, norm1_w, dh, dx2, "rmsnorm1_bwd")
    small = dict(norm1_w=g_norm1, conv_w=g_conv_w, conv_b=g_conv_b, dt_bias=g_dtb[:, :SSD_HEADS],
                 a_log=g_alog[:, :SSD_HEADS], d_skip=g_dsk[:, :SSD_HEADS], ssd_norm_w=g_ssd_norm,
                 norm_f_w=g_norm_f)
    big = dict(w_in=_sum_row_parts(landed, "w_in_sum"), w_br_ret=_sum_slots(got_ret, "w_br_ret_sum"),
               w_br_ssd=_sum_slots(got_ssd, "w_br_ssd_sum"), w_out=_sum_slots(got_out, "w_out_sum"))
    return loss_acc[0, 0], grad_x, small, big


def _all_reduce_small(vec, name):
    r, c = vec.shape

    def body(x_ref, out_ref, land, send_sems, recv_sems):
        x, y, cc = _mesh_pos()
        my_idx = 4 * x + 2 * y + cc
        land[my_idx] = x_ref[...]
        copies = []
        for k in range(1, N_DEV):
            px, py, pc = x ^ (k >> 2), y ^ ((k >> 1) & 1), cc ^ (k & 1)
            cp = pltpu.make_async_remote_copy(
                src_ref=x_ref, dst_ref=land.at[my_idx],
                send_sem=send_sems.at[k - 1], recv_sem=recv_sems.at[k - 1],
                device_id=(px, py, pc), device_id_type=MESH)
            cp.start()
            copies.append(cp)
        for k in range(1, N_DEV):
            px, py, pc = x ^ (k >> 2), y ^ ((k >> 1) & 1), cc ^ (k & 1)
            pltpu.make_async_remote_copy(
                src_ref=x_ref, dst_ref=land.at[4 * px + 2 * py + pc],
                send_sem=send_sems.at[k - 1], recv_sem=recv_sems.at[k - 1],
                device_id=(px, py, pc), device_id_type=MESH).wait_recv()
        for cp in copies:
            cp.wait_send()
        acc = land[0]
        for i in range(1, N_DEV):
            acc = acc + land[i]
        out_ref[...] = acc

    return pl.pallas_call(
        body,
        out_shape=jax.ShapeDtypeStruct((r, c), F32),
        in_specs=[pl.BlockSpec(memory_space=pltpu.VMEM)],
        out_specs=pl.BlockSpec(memory_space=pltpu.VMEM),
        scratch_shapes=[pltpu.VMEM((N_DEV, r, c), F32), pltpu.SemaphoreType.DMA((7,)),
                        pltpu.SemaphoreType.DMA((7,))],
        name=name)(vec)


_SMALL = ("norm1_w", "conv_w", "conv_b", "dt_bias", "a_log", "d_skip", "ssd_norm_w", "norm_f_w")
_SMALL_COLS = 128
_W_IN = "w_in"
_WEIGHTS = ("norm1_w", "w_in", "conv_w", "conv_b", "dt_bias", "a_log", "d_skip", "ssd_norm_w",
            "w_br_ret", "w_br_ssd", "w_out", "norm_f_w")


def _pack(parts):
    flat = jnp.concatenate([p.reshape(-1).astype(F32) for p in parts])
    rows = -(-flat.shape[0] // (8 * _SMALL_COLS)) * 8
    return jnp.pad(flat, (0, rows * _SMALL_COLS - flat.shape[0])).reshape(rows, _SMALL_COLS)


def _unpack(packed, shapes):
    flat = packed.reshape(-1)
    out, off = [], 0
    for shp in shapes:
        n = int(np.prod(shp))
        out.append(flat[off:off + n].reshape(shp))
        off += n
    return out


def kernel(x, positions, norm1_w, w_in, conv_w, conv_b, dt_bias, a_log, d_skip, ssd_norm_w, w_br_ret, w_br_ssd, w_out, norm_f_w, loss_target, m_norm1_w, m_w_in, m_conv_w, m_conv_b, m_dt_bias, m_a_log, m_d_skip, m_ssd_norm_w, m_w_br_ret, m_w_br_ssd, m_w_out, m_norm_f_w, v_norm1_w, v_w_in, v_conv_w, v_conv_b, v_dt_bias, v_a_log, v_d_skip, v_ssd_norm_w, v_w_br_ret, v_w_br_ssd, v_w_out, v_norm_f_w):
    w = dict(norm1_w=norm1_w, w_in=w_in, conv_w=conv_w, conv_b=conv_b, dt_bias=dt_bias, a_log=a_log,
             d_skip=d_skip, ssd_norm_w=ssd_norm_w, w_br_ret=w_br_ret, w_br_ssd=w_br_ssd, w_out=w_out,
             norm_f_w=norm_f_w)
    m = dict(norm1_w=m_norm1_w, w_in=m_w_in, conv_w=m_conv_w, conv_b=m_conv_b, dt_bias=m_dt_bias,
             a_log=m_a_log, d_skip=m_d_skip, ssd_norm_w=m_ssd_norm_w, w_br_ret=m_w_br_ret,
             w_br_ssd=m_w_br_ssd, w_out=m_w_out, norm_f_w=m_norm_f_w)
    v = dict(norm1_w=v_norm1_w, w_in=v_w_in, conv_w=v_conv_w, conv_b=v_conv_b, dt_bias=v_dt_bias,
             a_log=v_a_log, d_skip=v_d_skip, ssd_norm_w=v_ssd_norm_w, w_br_ret=v_w_br_ret,
             w_br_ssd=v_w_br_ssd, w_out=v_w_out, norm_f_w=v_norm_f_w)
    s = x.shape[1]
    my_idx = 4 * lax.axis_index("x") + 2 * lax.axis_index("y") + lax.axis_index("c")

    w[_W_IN], m[_W_IN], v[_W_IN] = w_in[0].T, m_w_in[0].T, v_w_in[0].T

    loss_part, grad_x, g_small, g_big = _device_step(
        x[0], positions.reshape(s, 1), loss_target[0], norm1_w, conv_w[0], conv_b, dt_bias, a_log, d_skip,
        ssd_norm_w, norm_f_w.reshape(1, D_MODEL), _cast_bf16(w[_W_IN], "cast_w_in"),
        w_br_ret[0], w_br_ssd[0], w_out[0])

    small_shapes = [(1, 1)] + [g_small[n].shape for n in _SMALL]
    summed = _unpack(_all_reduce_small(_pack([loss_part.reshape(1, 1)] + [g_small[n] for n in _SMALL]),
                                       "allreduce_small"), small_shapes)
    loss = summed[0].reshape(())
    grads = dict(zip(_SMALL, summed[1:]))
    conv_cols = SSD_CD // N_DEV
    grads["conv_w"] = lax.dynamic_slice_in_dim(grads["conv_w"], my_idx * conv_cols, conv_cols, axis=1)
    grads["norm_f_w"] = grads["norm_f_w"].reshape(D_MODEL)
    for n in ("norm1_w", "conv_w", "conv_b", "dt_bias", "a_log", "d_skip", "ssd_norm_w"):
        grads[n] = grads[n].reshape(w[n].shape)

    delta, new_m, new_v = {}, {}, {}
    for n in ("w_br_ret", "w_br_ssd", "w_out"):
        w[n], m[n], v[n] = w[n][0], m[n][0], v[n][0]
    for n in (_W_IN, "w_br_ret", "w_br_ssd", "w_out"):
        back = (lambda a: a.T[None]) if n == _W_IN else (lambda a: a[None])
        res = _adamw(w[n], g_big[n], m[n], v[n], "adamw_" + n)
        grads[n] = back(g_big[n])
        delta[n], new_m[n], new_v[n] = (back(a) for a in res)
    shapes = [w[n].shape for n in _SMALL]
    packed = _adamw(_pack([w[n] for n in _SMALL]), _pack([grads[n] for n in _SMALL]),
                    _pack([m[n] for n in _SMALL]), _pack([v[n] for n in _SMALL]), "adamw_small")
    for res, dst in zip(packed, (delta, new_m, new_v)):
        for n, a in zip(_SMALL, _unpack(res, shapes)):
            dst[n] = a

    return (loss, grad_x.reshape(x.shape), *[grads[n] for n in _WEIGHTS], *[delta[n] for n in _WEIGHTS],
            *[new_m[n] for n in _WEIGHTS], *[new_v[n] for n in _WEIGHTS])
```

```python
import functools

import numpy as np
import jax
import jax.numpy as jnp
from jax import lax
from jax.experimental import pallas as pl
from jax.experimental.pallas import tpu as pltpu

F32 = jnp.float32
BF16 = jnp.bfloat16

D_MODEL = 2048
CHUNK = 64
CHUNKS_PER_STEP = 2
STEP_ROWS = CHUNK * CHUNKS_PER_STEP
RET_CHUNKS_PER_STEP = 4
RET_STEP_ROWS = CHUNK * RET_CHUNKS_PER_STEP
EPS = 1e-6
N_DEV = 8

RET_HEADS = 8
RET_DK = 256
RET_W = RET_HEADS * RET_DK
ROPE_THETA = 10000.0
ROPE_HALF = RET_DK // 2

SSD_W = 4096
SSD_P = 64
SSD_HEADS = 64
SSD_GROUPS = 8
SSD_N = 128
SSD_GW = SSD_W // SSD_GROUPS
SSD_QW = 256
SSD_BLOCK_GROUPS = 2
SSD_CONV = 4
SSD_CD = SSD_W + 2 * SSD_GROUPS * SSD_N
HEAD_PAD = 128
B_OFF = SSD_W
C_OFF = SSD_W + SSD_GROUPS * SSD_N

ADAM_LR = 0.001
ADAM_B1 = 0.9
ADAM_B2 = 0.999
ADAM_EPS = 1e-08
ADAM_WD = 0.01
ADAM_STEP = 10

SPLITS = (RET_W, RET_W, RET_W, RET_W, SSD_W, SSD_CD, SSD_HEADS, D_MODEL, D_MODEL)
IN_PROJ = sum(SPLITS)
OFF_Z = 4 * RET_W
OFF_XBC = OFF_Z + SSD_W
OFF_DT = OFF_XBC + SSD_CD
OFF_G = OFF_DT + SSD_HEADS

ROW_TILE = 256
CONV_TILE = 128
MM_TILE = 1024
MM_TILE_WIDE = 2048
MM_TK = 2048
VMEM_CAP = 60 << 20

MESH = pl.DeviceIdType.MESH


def _dot(a, b):
    return lax.dot_general(a, b, (((1,), (0,)), ((), ())), preferred_element_type=F32)


def _dot_nt(a, b):
    return lax.dot_general(a, b, (((1,), (1,)), ((), ())), preferred_element_type=F32)


def _dot_tn(a, b):
    return lax.dot_general(a, b, (((0,), (0,)), ((), ())), preferred_element_type=F32)


def _bf(x):
    return x.astype(BF16)


def _split3(x):
    hi = x.astype(BF16)
    r = x - hi.astype(F32)
    mid = r.astype(BF16)
    lo = (r - mid.astype(F32)).astype(BF16)
    return hi, mid, lo


def _dot_exact_l(x, sel, pieces=3):
    hi, mid, lo = _split3(x)
    r = _dot(hi, sel) + _dot(mid, sel)
    return r + _dot(lo, sel) if pieces == 3 else r


def _dot_exact_r(sel, x):
    hi, mid, lo = _split3(x)
    return _dot(sel, hi) + _dot(sel, mid) + _dot(sel, lo)


def _sigmoid(x):
    return 1.0 / (1.0 + jnp.exp(-x))


def _softplus(x):
    return jnp.maximum(x, 0.0) + jnp.log(1.0 + jnp.exp(-jnp.abs(x)))


def _iota(shape, axis):
    return lax.broadcasted_iota(jnp.int32, shape, axis)


def _params(*sem):
    return pltpu.CompilerParams(dimension_semantics=sem)


class _Rider:
    def __init__(self, inputs, out_shapes, n_sems, phases):
        self.inputs, self.out_shapes, self.n_sems, self.phases = tuple(inputs), tuple(out_shapes), n_sems, phases

    def in_specs(self):
        return [pl.BlockSpec(memory_space=pl.ANY)] * len(self.inputs)

    def out_specs(self):
        return [pl.BlockSpec(memory_space=pl.ANY)] * len(self.out_shapes)

    def scratch(self):
        return [pltpu.SemaphoreType.DMA((self.n_sems,)), pltpu.SemaphoreType.DMA((self.n_sems,))]

    def run(self, step, n_steps, ins, outs, send_sems, recv_sems, last):
        for frac, fn in self.phases:
            if (frac >= 1.0) != last:
                continue
            at = min(int(frac * n_steps), n_steps - 1)

            @pl.when(step == at)
            def _(fn=fn):
                fn(ins, outs, send_sems, recv_sems)


def _mesh_pos():
    return lax.axis_index("x"), lax.axis_index("y"), lax.axis_index("c")


def _gather_rider(shards, relay_at, pass_at):
    n = len(shards)

    def tools(a, ins, outs, send_sems, recv_sems):
        x, y, cc = _mesh_pos()
        nbrs = [(1 - x, y), (x, 1 - y)]
        diag = (1 - x, 1 - y)
        relay_from, relay_to = (x ^ cc, y ^ (1 - cc)), (x ^ (1 - cc), y ^ cc)

        def slot(px, py, pc):
            return outs[a].at[4 * px + 2 * py + pc]

        def copy(k, block, to, src=None):
            return pltpu.make_async_remote_copy(
                src_ref=slot(*block) if src is None else src, dst_ref=slot(*block),
                send_sem=send_sems.at[8 * a + k], recv_sem=recv_sems.at[8 * a + k], device_id=to, device_id_type=MESH)

        me, sibling = (x, y, cc), (x, y, 1 - cc)
        return dict(
            mine=lambda: pltpu.make_async_copy(ins[a], slot(*me), send_sems.at[8 * a + 7]),
            first=lambda: [copy(0, me, sibling, src=ins[a])] + [copy(1 + j, me, (*chip, cc), src=ins[a])
                                                                for j, chip in enumerate(nbrs)],
            landed=lambda: [copy(1 + j, (*chip, cc), me) for j, chip in enumerate(nbrs)],
            relay=lambda: copy(3, (*relay_from, cc), (*relay_to, cc)),
            relayed_in=lambda: copy(3, (*diag, cc), me),
            passed=lambda: [copy(4 + j, (*chip, cc), sibling) for j, chip in enumerate(nbrs)],
            passed_diag=lambda: copy(6, (*diag, cc), sibling),
            from_sibling=lambda: [copy(0, sibling, me)] + [copy(4 + j, (*chip, 1 - cc), me)
                                                           for j, chip in enumerate(nbrs + [diag])])

    def start(*refs):
        for a in range(n):
            t = tools(a, *refs)
            t["mine"]().start()
            for cp in t["first"]():
                cp.start()

    def relay(*refs):
        for a in range(n):
            t = tools(a, *refs)
            for got in t["landed"]():
                got.wait_recv()
            t["relay"]().start()
            for cp in t["passed"]():
                cp.start()

    def pass_diag(*refs):
        for a in range(n):
            t = tools(a, *refs)
            t["relayed_in"]().wait_recv()
            t["passed_diag"]().start()

    def finish(*refs):
        for a in range(n):
            t = tools(a, *refs)
            for cp in t["from_sibling"]():
                cp.wait_recv()
            for cp in t["first"]() + [t["relay"]()] + t["passed"]() + [t["passed_diag"]()]:
                cp.wait_send()
            t["mine"]().wait()

    outs = [jax.ShapeDtypeStruct((N_DEV,) + s.shape, s.dtype) for s in shards]
    return _Rider(shards, outs, 8 * n, ((0.0, start), (relay_at, relay), (pass_at, pass_diag), (1.0, finish)))


def _scatter_rider(blocks):
    n = len(blocks)

    def copies(a, ins, outs, send_sems, recv_sems, with_back):
        x, y, cc = _mesh_pos()
        my_idx = 4 * x + 2 * y + cc
        mine = pltpu.make_async_copy(ins[a].at[my_idx], outs[a].at[my_idx], send_sems.at[8 * a + 7])
        out, back = [], []
        for k in range(1, N_DEV):
            px, py, pc = x ^ (k >> 2), y ^ ((k >> 1) & 1), cc ^ (k & 1)
            peer_idx = 4 * px + 2 * py + pc
            sems = dict(send_sem=send_sems.at[8 * a + k - 1], recv_sem=recv_sems.at[8 * a + k - 1],
                        device_id=(px, py, pc), device_id_type=MESH)
            out.append(pltpu.make_async_remote_copy(src_ref=ins[a].at[peer_idx], dst_ref=outs[a].at[my_idx], **sems))
            if with_back:
                back.append(pltpu.make_async_remote_copy(src_ref=ins[a].at[my_idx], dst_ref=outs[a].at[peer_idx], **sems))
        return mine, out, back

    def start(*refs):
        for a in range(n):
            mine, out, _ = copies(a, *refs, False)
            mine.start()
            for cp in out:
                cp.start()

    def finish(*refs):
        for a in range(n):
            mine, out, back = copies(a, *refs, True)
            for cp in back:
                cp.wait_recv()
            for cp in out:
                cp.wait_send()
            mine.wait()

    outs = [jax.ShapeDtypeStruct(b.shape, b.dtype) for b in blocks]
    return _Rider(blocks, outs, 8 * n, ((0.0, start), (1.0, finish)))


def _sibling_rider(blocks):
    _, r, c = blocks.shape

    def copies(ins, outs, send_sems, recv_sems):
        x, y, cc = _mesh_pos()
        return [pltpu.make_async_remote_copy(
            src_ref=ins[0].at[2 * j + 1 - cc], dst_ref=outs[0].at[j], send_sem=send_sems.at[j],
            recv_sem=recv_sems.at[j], device_id=(x, y, 1 - cc), device_id_type=MESH) for j in range(4)]

    def start(*refs):
        for cp in copies(*refs):
            cp.start()

    def finish(*refs):
        for cp in copies(*refs):
            cp.wait_recv()
        for cp in copies(*refs):
            cp.wait_send()

    return _Rider([blocks], [jax.ShapeDtypeStruct((4, r, c), blocks.dtype)], 4, ((0.0, start), (1.0, finish)))


def _chip_rider(partial, row0, rows):
    _, _, c = partial.shape

    def copies(ins, outs, send_sems, recv_sems, with_back):
        x, y, cc = _mesh_pos()
        my_chip = 2 * x + y

        def rows_of(j):
            return ins[0].at[j, pl.ds(row0, rows)]

        mine = pltpu.make_async_copy(rows_of(my_chip), outs[0].at[my_chip], send_sems.at[3])
        out, back = [], []
        for k in range(1, 4):
            px, py = x ^ (k >> 1), y ^ (k & 1)
            peer = 2 * px + py
            sems = dict(send_sem=send_sems.at[k - 1], recv_sem=recv_sems.at[k - 1], device_id=(px, py, cc),
                        device_id_type=MESH)
            out.append(pltpu.make_async_remote_copy(src_ref=rows_of(peer), dst_ref=outs[0].at[my_chip], **sems))
            if with_back:
                back.append(pltpu.make_async_remote_copy(src_ref=rows_of(my_chip), dst_ref=outs[0].at[peer], **sems))
        return mine, out, back

    def start(*refs):
        mine, out, _ = copies(*refs, False)
        mine.start()
        for cp in out:
            cp.start()

    def finish(*refs):
        mine, out, back = copies(*refs, True)
        for cp in back:
            cp.wait_recv()
        for cp in out:
            cp.wait_send()
        mine.wait()

    return _Rider([partial], [jax.ShapeDtypeStruct((4, rows, c), partial.dtype)], 4, ((0.0, start), (1.0, finish)))


def _matmul(a, b, mode, out_dtype, name, add=None, rider=None, b_rows=None, extra=None):
    b_shape = b.shape if b_rows is None else (b_rows[1], b.shape[1])
    if mode == "nn":
        (m, k), (k2, n) = a.shape, b_shape
    elif mode == "nt":
        (m, k), (n, k2) = a.shape, b_shape
    else:
        (k, m), (k2, n) = a.shape, b_shape
    assert k == k2, (a.shape, b_shape, mode)
    tm, tn, tk = min(m, MM_TILE), min(n, MM_TILE), min(k, MM_TK)
    if n % MM_TILE_WIDE == 0 and (k <= MM_TK or (add is None and jnp.dtype(out_dtype).itemsize == 2)):
        tn = MM_TILE_WIDE
    assert m % tm == 0 and n % tn == 0 and k % tk == 0, (m, n, k)
    nk = k // tk
    b_tile_rows = tn if mode == "nt" else tk
    b_off = 0 if b_rows is None else b_rows[0] // b_tile_rows
    assert b_rows is None or b_rows[0] % b_tile_rows == 0, (b_rows, b_tile_rows)
    dot = {"nn": _dot, "nt": _dot_nt, "tn": _dot_tn}[mode]

    gm, gn = m // tm, n // tn
    n_rin = len(rider.inputs) if rider else 0
    n_rout = len(rider.out_shapes) if rider else 0
    n_add = 1 if add is not None else 0

    n_extra = 2 if extra is not None else 0

    def body(*refs):
        a_ref, b_ref = refs[:2]
        add_ref = refs[2] if n_add else None
        extra_refs = refs[2 + n_add:2 + n_add + n_extra]
        first = 2 + n_add + n_extra
        r_ins = refs[first:first + n_rin]
        o_ref = refs[first + n_rin]
        r_outs = refs[first + n_rin + 1:first + n_rin + 1 + n_rout]
        scratch = refs[first + n_rin + 1 + n_rout:]
        acc_ref = scratch[0] if nk > 1 else None
        kk = pl.program_id(2)
        if rider:
            step = (pl.program_id(0) * gn + pl.program_id(1)) * nk + kk
            ride = functools.partial(rider.run, step, gm * gn * nk, r_ins, r_outs, scratch[-2], scratch[-1])
            ride(last=False)

        def finish(r):
            if extra_refs:
                r = r + _dot(_bf(extra_refs[0][...]), _bf(extra_refs[1][...]))
            if add_ref is not None:
                r = r + add_ref[...]
            o_ref[...] = r.astype(o_ref.dtype)

        if nk == 1:
            finish(dot(_bf(a_ref[...]), _bf(b_ref[...])))
        else:
            @pl.when(kk == 0)
            def _():
                acc_ref[...] = dot(_bf(a_ref[...]), _bf(b_ref[...]))

            @pl.when(jnp.logical_and(kk > 0, kk < nk - 1))
            def _():
                acc_ref[...] += dot(_bf(a_ref[...]), _bf(b_ref[...]))

            @pl.when(kk == nk - 1)
            def _():
                finish(acc_ref[...] + dot(_bf(a_ref[...]), _bf(b_ref[...])))
        if rider:
            ride(last=True)

    if mode == "nn":
        a_spec = pl.BlockSpec((tm, tk), lambda i, j, kk: (i, kk))
        b_spec = pl.BlockSpec((tk, tn), lambda i, j, kk: (kk + b_off, j))
    elif mode == "nt":
        a_spec = pl.BlockSpec((tm, tk), lambda i, j, kk: (i, kk))
        b_spec = pl.BlockSpec((tn, tk), lambda i, j, kk: (j + b_off, kk))
    else:
        a_spec = pl.BlockSpec((tk, tm), lambda i, j, kk: (kk, i))
        b_spec = pl.BlockSpec((tk, tn), lambda i, j, kk: (kk + b_off, j))
    o_spec = pl.BlockSpec((tm, tn), lambda i, j, kk: (i, j))
    in_specs = [a_spec, b_spec] + ([o_spec] if add is not None else [])
    args = (a, b) + ((add,) if add is not None else ())
    if extra is not None:
        (m2, k_extra), (k_extra2, n2) = extra[0].shape, extra[1].shape
        assert (m2, n2, k_extra) == (m, n, k_extra2), (extra[0].shape, extra[1].shape)
        in_specs += [pl.BlockSpec((tm, k_extra), lambda i, j, kk: (i, 0)),
                     pl.BlockSpec((k_extra, tn), lambda i, j, kk: (0, j))]
        args += tuple(extra)
    block_bytes = (tm * tk * a.dtype.itemsize + tk * tn * b.dtype.itemsize
                   + tm * tn * (jnp.dtype(out_dtype).itemsize + (4 if add is not None else 0)))
    vmem = min(2 * block_bytes + 2 * tm * tn * 4 + 2 * (tm + tn) * tk + (4 << 20), VMEM_CAP)
    out_shape = jax.ShapeDtypeStruct((m, n), out_dtype)
    scratch = [pltpu.VMEM((tm, tn), F32)] if nk > 1 else []
    if rider:
        in_specs = in_specs + rider.in_specs()
        args = args + rider.inputs
        out_shape = (out_shape, *rider.out_shapes)
        o_spec = (o_spec, *rider.out_specs())
        scratch = scratch + rider.scratch()
    sem = ("arbitrary",) * 3 if rider else ("parallel", "parallel", "arbitrary")
    return pl.pallas_call(
        body,
        out_shape=out_shape,
        grid=(gm, gn, nk),
        in_specs=in_specs,
        out_specs=o_spec,
        scratch_shapes=scratch,
        compiler_params=pltpu.CompilerParams(dimension_semantics=sem, vmem_limit_bytes=int(vmem)),
        name=name,
    )(*args)


def _row_spec(width, tile=ROW_TILE):
    return pl.BlockSpec((tile, width), lambda i: (i, 0))


def _full_spec(shape):
    nd = len(shape)
    return pl.BlockSpec(shape, lambda *_: (0,) * nd)


def _tiling_2d(r, c):
    if r <= ROW_TILE or r % ROW_TILE == 0:
        tr = min(r, ROW_TILE)
        return (tr, c), r // tr, (lambda i: (i, 0))
    tc = 128 if r > 4096 else 256
    assert c % tc == 0, (r, c)
    return (r, tc), c // tc, (lambda i: (0, i))


def _spec_2d(r, c):
    blk, grid, idx = _tiling_2d(r, c)
    return pl.BlockSpec(blk, idx), grid


def _cast_bf16(x, name):
    r, c = x.shape
    sp, grid = _spec_2d(r, c)

    def body(x_ref, o_ref):
        o_ref[...] = _bf(x_ref[...])

    return pl.pallas_call(
        body, out_shape=jax.ShapeDtypeStruct((r, c), BF16), grid=(grid,), in_specs=[sp], out_specs=sp,
        compiler_params=_params("parallel"), name=name)(x)


def _rmsnorm_rope_fwd(x, w, pos_col, inv_freq, to_bf16, name, rider):
    s, d = x.shape
    steps = s // ROW_TILE
    n_cast, n_rin, n_rout = len(to_bf16), len(rider.inputs), len(rider.out_shapes)

    def body(x_ref, w_ref, p_ref, f_ref, *refs):
        c_ins, refs = refs[:n_cast], refs[n_cast:]
        r_ins, refs = refs[:n_rin], refs[n_rin:]
        (o_ref, cos_ref, sin_ref), refs = refs[:3], refs[3:]
        c_outs, refs = refs[:n_cast], refs[n_cast:]
        ride = functools.partial(rider.run, pl.program_id(0), steps, r_ins, refs[:n_rout], refs[-2], refs[-1])
        ride(last=False)

        @pl.when(pl.program_id(0) == 0)
        def _():
            for c_in, c_out in zip(c_ins, c_outs):
                c_out[...] = _bf(c_in[...])

        xv = x_ref[...]
        rstd = lax.rsqrt(jnp.mean(xv * xv, axis=1, keepdims=True) + EPS)
        o_ref[...] = _bf(xv * rstd * w_ref[...])
        ang = p_ref[...].astype(F32) * f_ref[...]
        cos_ref[...] = jnp.cos(ang)
        sin_ref[...] = jnp.sin(ang)
        ride(last=True)

    table = jax.ShapeDtypeStruct((s, ROPE_HALF), F32)
    whole = [_full_spec(a.shape) for a in to_bf16]
    return pl.pallas_call(
        body,
        out_shape=(jax.ShapeDtypeStruct((s, d), BF16), table, table,
                   *[jax.ShapeDtypeStruct(a.shape, BF16) for a in to_bf16], *rider.out_shapes),
        grid=(steps,),
        in_specs=[_row_spec(d), _full_spec((1, d)), _row_spec(1), _full_spec((1, ROPE_HALF)), *whole,
                  *rider.in_specs()],
        out_specs=(_row_spec(d), _row_spec(ROPE_HALF), _row_spec(ROPE_HALF), *whole, *rider.out_specs()),
        scratch_shapes=rider.scratch(),
        compiler_params=_params("arbitrary"), name=name)(x, w, pos_col, inv_freq, *to_bf16, *rider.inputs)


def _rmsnorm_bwd(x, w, dh, dres, name):
    s, d = x.shape

    def body(x_ref, w_ref, dh_ref, dres_ref, dx_ref, dw_ref):
        @pl.when(pl.program_id(0) == 0)
        def _():
            dw_ref[...] = jnp.zeros_like(dw_ref)

        xv = x_ref[...]
        rstd = lax.rsqrt(jnp.mean(xv * xv, axis=1, keepdims=True) + EPS)
        xhat = xv * rstd
        dhv = dh_ref[...]
        dxhat = dhv * w_ref[...]
        dx = rstd * (dxhat - xhat * jnp.mean(dxhat * xhat, axis=1, keepdims=True))
        dx_ref[...] = dx + dres_ref[...]
        dw_ref[...] += jnp.sum(dhv * xhat, axis=0, keepdims=True)

    return pl.pallas_call(
        body,
        out_shape=(jax.ShapeDtypeStruct((s, d), F32), jax.ShapeDtypeStruct((1, d), F32)),
        grid=(s // ROW_TILE,),
        in_specs=[_row_spec(d), _full_spec((1, d)), _row_spec(d), _row_spec(d)],
        out_specs=(_row_spec(d), _full_spec((1, d))),
        compiler_params=_params("arbitrary"), name=name)(x, w, dh, dres)


def _merge_fwd(pg, p_r, p_s, name):
    s = pg.shape[0]

    def body(g_ref, r_ref, s_ref, o_ref):
        g = g_ref[...].astype(F32)
        o_ref[...] = _bf(_sigmoid(g[:, :D_MODEL]) * r_ref[...].astype(F32)
                         + _sigmoid(g[:, D_MODEL:]) * s_ref[...].astype(F32))

    return pl.pallas_call(
        body, out_shape=jax.ShapeDtypeStruct((s, D_MODEL), BF16), grid=(s // ROW_TILE,),
        in_specs=[_row_spec(2 * D_MODEL), _row_spec(D_MODEL), _row_spec(D_MODEL)],
        out_specs=_row_spec(D_MODEL), compiler_params=_params("parallel"), name=name)(pg, p_r, p_s)


def _merge_bwd(pg, p_r, p_s, dm, name):
    s = pg.shape[0]

    def body(g_ref, r_ref, s_ref, dm_ref, dr_ref, ds_ref, dg_ref):
        g = g_ref[...].astype(F32)
        sr, ss = _sigmoid(g[:, :D_MODEL]), _sigmoid(g[:, D_MODEL:])
        d = dm_ref[...]
        dr_ref[...] = _bf(d * sr)
        ds_ref[...] = _bf(d * ss)
        dg_ref[:, :D_MODEL] = _bf(d * r_ref[...].astype(F32) * sr * (1.0 - sr))
        dg_ref[:, D_MODEL:] = _bf(d * s_ref[...].astype(F32) * ss * (1.0 - ss))

    o = jax.ShapeDtypeStruct((s, D_MODEL), BF16)
    return pl.pallas_call(
        body, out_shape=(o, o, jax.ShapeDtypeStruct((s, 2 * D_MODEL), BF16)), grid=(s // ROW_TILE,),
        in_specs=[_row_spec(2 * D_MODEL), _row_spec(D_MODEL), _row_spec(D_MODEL), _row_spec(D_MODEL)],
        out_specs=(_row_spec(D_MODEL), _row_spec(D_MODEL), _row_spec(2 * D_MODEL)),
        compiler_params=_params("parallel"), name=name)(pg, p_r, p_s, dm)


def _final_fwd_bwd(x, o, w, target, name):
    s, d = x.shape

    def body(x_ref, o_ref, w_ref, t_ref, dx_ref, dw_ref, loss_ref):
        @pl.when(pl.program_id(0) == 0)
        def _():
            dw_ref[...] = jnp.zeros_like(dw_ref)
            loss_ref[...] = jnp.zeros_like(loss_ref)

        x2 = x_ref[...] + o_ref[...]
        rstd = lax.rsqrt(jnp.mean(x2 * x2, axis=1, keepdims=True) + EPS)
        xhat = x2 * rstd
        wv = w_ref[...]
        err = xhat * wv - t_ref[...]
        loss_ref[...] += jnp.sum(jnp.sum(err * err, axis=1, keepdims=True), axis=0, keepdims=True) * (0.5 / d)
        dy = err * (1.0 / d)
        dw_ref[...] += jnp.sum(dy * xhat, axis=0, keepdims=True)
        dxhat = dy * wv
        dx_ref[...] = rstd * (dxhat - xhat * jnp.mean(dxhat * xhat, axis=1, keepdims=True))

    return pl.pallas_call(
        body,
        out_shape=(jax.ShapeDtypeStruct((s, d), F32), jax.ShapeDtypeStruct((1, d), F32),
                   jax.ShapeDtypeStruct((8, 128), F32)),
        grid=(s // ROW_TILE,),
        in_specs=[_row_spec(d), _row_spec(d), _full_spec((1, d)), _row_spec(d)],
        out_specs=(_row_spec(d), _full_spec((1, d)), _full_spec((8, 128))),
        compiler_params=_params("arbitrary"), name=name)(x, o, w, target)


def _adamw(w, g, m, v, name):
    r, c = w.shape
    sp, grid = _spec_2d(r, c)
    c1 = 1.0 / (1.0 - ADAM_B1 ** ADAM_STEP)
    c2 = 1.0 / (1.0 - ADAM_B2 ** ADAM_STEP)

    def body(w_ref, g_ref, m_ref, v_ref, d_ref, nm_ref, nv_ref):
        gv = g_ref[...]
        nm = ADAM_B1 * m_ref[...] + (1.0 - ADAM_B1) * gv
        nv = ADAM_B2 * v_ref[...] + (1.0 - ADAM_B2) * (gv * gv)
        d_ref[...] = -ADAM_LR * ((nm * c1) / (jnp.sqrt(nv * c2) + ADAM_EPS) + ADAM_WD * w_ref[...])
        nm_ref[...] = nm
        nv_ref[...] = nv

    o = jax.ShapeDtypeStruct((r, c), F32)
    return pl.pallas_call(
        body, out_shape=(o, o, o), grid=(grid,), in_specs=[sp, sp, sp, sp], out_specs=(sp, sp, sp),
        compiler_params=_params("parallel"), name=name)(w, g, m, v)


def _sum_slots(land, name):
    n, r, c = land.shape
    blk, grid, idx = _tiling_2d(r, c)

    def body(l_ref, o_ref):
        acc = l_ref[0].astype(F32)
        for i in range(1, n):
            acc = acc + l_ref[i].astype(F32)
        o_ref[...] = acc

    return pl.pallas_call(
        body, out_shape=jax.ShapeDtypeStruct((r, c), F32), grid=(grid,),
        in_specs=[pl.BlockSpec((n, *blk), lambda i: (0, *idx(i)))], out_specs=pl.BlockSpec(blk, idx),
        compiler_params=_params("parallel"), name=name)(land)


SLAB_COLS = 256


def _chip_sum(blocks, from_sibling, my_core, name):
    _, r, c = blocks.shape

    def body(core_ref, a_ref, b_ref, o_ref):
        o_ref[...] = _bf(a_ref[...].astype(F32) + b_ref[...].astype(F32))

    def slab(chip_of):
        return pl.BlockSpec((1, r, SLAB_COLS), lambda j, i, core: (chip_of(j, core), 0, i))

    grid_spec = pltpu.PrefetchScalarGridSpec(
        num_scalar_prefetch=1, grid=(4, c // SLAB_COLS),
        in_specs=[slab(lambda j, core: 2 * j + core[0]), slab(lambda j, core: j)],
        out_specs=slab(lambda j, core: j))
    return pl.pallas_call(
        body, out_shape=jax.ShapeDtypeStruct((4, r, c), BF16), grid_spec=grid_spec,
        compiler_params=_params("parallel", "parallel"), name=name,
    )(my_core.astype(jnp.int32).reshape(1), blocks, from_sibling)


def _sum_row_parts(parts, name):
    c = parts[0].shape[2]
    rows = [p.shape[1] for p in parts]

    def body(*refs):
        o_ref, off = refs[-1], 0
        for p_ref, n in zip(refs[:-1], rows):
            acc = p_ref[0].astype(F32)
            for i in range(1, p_ref.shape[0]):
                acc = acc + p_ref[i].astype(F32)
            o_ref[off:off + n, :] = acc
            off += n

    return pl.pallas_call(
        body, out_shape=jax.ShapeDtypeStruct((sum(rows), c), F32), grid=(c // SLAB_COLS,),
        in_specs=[pl.BlockSpec((p.shape[0], p.shape[1], SLAB_COLS), lambda i: (0, 0, i)) for p in parts],
        out_specs=pl.BlockSpec((sum(rows), SLAB_COLS), lambda i: (0, i)),
        compiler_params=_params("parallel"), name=name)(*parts)


def _retention_tables():
    lg = np.log1p(-(2.0 ** (-5.0 - np.arange(RET_HEADS, dtype=np.float64))))
    idx = np.arange(CHUNK, dtype=np.float64)
    intra = np.exp(np.abs(idx[:, None] - idx[None, :])[None] * lg[:, None, None])
    qd = np.exp((idx[None, :] + 1.0) * lg[:, None])
    kd = np.exp((CHUNK - 1.0 - idx[None, :]) * lg[:, None])
    cd = np.exp(CHUNK * lg)
    ones = np.ones((1, 1, RET_DK))
    return (jnp.asarray(intra, F32), jnp.asarray(qd[:, :, None] * ones, F32),
            jnp.asarray(kd[:, :, None] * ones, F32), jnp.asarray(cd[:, None, None] * ones, F32))


def _rope(t, cos, sin):
    t1, t2 = t[:, :ROPE_HALF], t[:, ROPE_HALF:]
    return jnp.concatenate([t1 * cos - t2 * sin, t2 * cos + t1 * sin], axis=1)


def _rope_t(d, cos, sin):
    d1, d2 = d[:, :ROPE_HALF], d[:, ROPE_HALF:]
    return jnp.concatenate([d1 * cos + d2 * sin, d2 * cos - d1 * sin], axis=1)


_HEADS = range(RET_HEADS)


def _ret_chunk_fwd(p_ref, cos, sin, intra_ref, qd_ref, st):
    def seg(i, h):
        return p_ref[:, i * RET_W + h * RET_DK:i * RET_W + (h + 1) * RET_DK]

    v = [seg(2, h) for h in _HEADS]
    qr = [_rope(seg(0, h), cos, sin) for h in _HEADS]
    kr = [_rope(seg(1, h), cos, sin) * (RET_DK ** -0.5) for h in _HEADS]
    qrb, vb = [_bf(a) for a in qr], [_bf(a) for a in v]
    sc = [_dot_nt(qrb[h], _bf(kr[h])) * intra_ref[h] for h in _HEADS]
    qs = [_dot(qrb[h], _bf(st[h])) for h in _HEADS]
    y = [_dot(_bf(sc[h]), vb[h]) + qs[h] * qd_ref[h] for h in _HEADS]
    g = [seg(3, h) for h in _HEADS]
    return dict(v=v, vb=vb, qr=qr, qrb=qrb, kr=kr, sc=sc, y=y, g=g)


def _group_norm(y):
    mu = jnp.mean(y, axis=1, keepdims=True)
    yc = y - mu
    rstd = lax.rsqrt(jnp.mean(yc * yc, axis=1, keepdims=True) + EPS)
    return yc * rstd, rstd


def _chunk_rows(sub):
    return pl.ds(sub * CHUNK, CHUNK)


def _ret_specs(steps, rev):
    cidx = (lambda c: steps - 1 - c) if rev else (lambda c: c)
    return dict(
        proj=pl.BlockSpec((RET_STEP_ROWS, 4 * RET_W), lambda c: (cidx(c), 0)),
        half=pl.BlockSpec((RET_STEP_ROWS, ROPE_HALF), lambda c: (cidx(c), 0)),
        wide=pl.BlockSpec((RET_STEP_ROWS, RET_W), lambda c: (cidx(c), 0)),
        state=pl.BlockSpec((RET_CHUNKS_PER_STEP, RET_HEADS, RET_DK, RET_DK), lambda c: (cidx(c), 0, 0, 0)),
        intra=_full_spec((RET_HEADS, CHUNK, CHUNK)),
        dec=_full_spec((RET_HEADS, CHUNK, RET_DK)),
        cd=_full_spec((RET_HEADS, 1, RET_DK)),
    )


def _retention_fwd(p_r, cos, sin, name):
    s = p_r.shape[0]
    nc = s // CHUNK
    intra_t, qd_t, kd_t, cd_t = _retention_tables()

    def body(p_ref, cos_ref, sin_ref, intra_ref, qd_ref, kd_ref, cd_ref, y_ref, st_ref, state):
        @pl.when(pl.program_id(0) == 0)
        def _():
            state[...] = jnp.zeros_like(state)

        for sub in range(RET_CHUNKS_PER_STEP):
            rows = _chunk_rows(sub)
            y_v, st_v = y_ref.at[rows], st_ref.at[sub]
            st = [state[h] for h in _HEADS]
            f = _ret_chunk_fwd(p_ref.at[rows], cos_ref[rows, :], sin_ref[rows, :], intra_ref, qd_ref, st)
            new_st = [st[h] * cd_ref[h] + _dot_tn(_bf(f["kr"][h] * kd_ref[h]), f["vb"][h]) for h in _HEADS]
            out = [_bf(_group_norm(f["y"][h])[0] * (f["g"][h] * _sigmoid(f["g"][h]))) for h in _HEADS]
            for h in _HEADS:
                st_v[h] = _bf(st[h])
                state[h] = new_st[h]
                y_v[:, h * RET_DK:(h + 1) * RET_DK] = out[h]

    steps = nc // RET_CHUNKS_PER_STEP
    sp = _ret_specs(steps, False)
    return pl.pallas_call(
        body,
        out_shape=(jax.ShapeDtypeStruct((s, RET_W), BF16),
                   jax.ShapeDtypeStruct((nc, RET_HEADS, RET_DK, RET_DK), BF16)),
        grid=(steps,),
        in_specs=[sp["proj"], sp["half"], sp["half"], sp["intra"], sp["dec"], sp["dec"], sp["cd"]],
        out_specs=(sp["wide"], sp["state"]),
        scratch_shapes=[pltpu.VMEM((RET_HEADS, RET_DK, RET_DK), F32)],
        compiler_params=_params("arbitrary"), name=name)(p_r, cos, sin, intra_t, qd_t, kd_t, cd_t)


def _retention_bwd(p_r, cos, sin, states, dy_r, name):
    s = p_r.shape[0]
    nc = s // CHUNK
    intra_t, qd_t, kd_t, cd_t = _retention_tables()

    def body(p_ref, cos_ref, sin_ref, intra_ref, qd_ref, kd_ref, cd_ref, st_ref, dy_ref, dp_ref, dstate):
        @pl.when(pl.program_id(0) == 0)
        def _():
            dstate[...] = jnp.zeros_like(dstate)

        for sub in reversed(range(RET_CHUNKS_PER_STEP)):
            rows = _chunk_rows(sub)
            chunk(p_ref.at[rows], cos_ref[rows, :], sin_ref[rows, :], intra_ref, qd_ref, kd_ref, cd_ref,
                  st_ref.at[sub], dy_ref.at[rows], dp_ref.at[rows], dstate)

    def chunk(p_ref, cos, sin, intra_ref, qd_ref, kd_ref, cd_ref, st_ref, dy_ref, dp_ref, dstate):
        st = [st_ref[h] for h in _HEADS]
        dsn = [dstate[h] for h in _HEADS]
        f = _ret_chunk_fwd(p_ref, cos, sin, intra_ref, qd_ref, st)
        vb, qrb, kr, g = f["vb"], f["qrb"], f["kr"], f["g"]
        norm = [_group_norm(f["y"][h]) for h in _HEADS]
        sg = [_sigmoid(g[h]) for h in _HEADS]
        dyr = [dy_ref[:, h * RET_DK:(h + 1) * RET_DK] for h in _HEADS]
        dyn = [dyr[h] * (g[h] * sg[h]) for h in _HEADS]
        dg = [dyr[h] * norm[h][0] * (sg[h] * (1.0 + g[h] * (1.0 - sg[h]))) for h in _HEADS]
        dy = [norm[h][1] * (dyn[h] - jnp.mean(dyn[h], axis=1, keepdims=True)
                            - norm[h][0] * jnp.mean(dyn[h] * norm[h][0], axis=1, keepdims=True)) for h in _HEADS]
        dyb, dsnb = [_bf(a) for a in dy], [_bf(a) for a in dsn]
        ds = [_bf(_dot_nt(dyb[h], vb[h]) * intra_ref[h]) for h in _HEADS]
        t = [_bf(dy[h] * qd_ref[h]) for h in _HEADS]
        dv = [_dot_tn(_bf(f["sc"][h]), dyb[h]) + _dot(_bf(kr[h] * kd_ref[h]), dsnb[h]) for h in _HEADS]
        dqr = [_dot(ds[h], _bf(kr[h])) + _dot_nt(t[h], _bf(st[h])) for h in _HEADS]
        dkr = [_dot_tn(ds[h], qrb[h]) + _dot_nt(vb[h], dsnb[h]) * kd_ref[h] for h in _HEADS]
        new_ds = [dsn[h] * cd_ref[h] + _dot_tn(qrb[h], t[h]) for h in _HEADS]
        for h in _HEADS:
            lo = h * RET_DK
            dstate[h] = new_ds[h]
            dp_ref[:, lo:lo + RET_DK] = _bf(_rope_t(dqr[h], cos, sin))
            dp_ref[:, RET_W + lo:RET_W + lo + RET_DK] = _bf(_rope_t(dkr[h], cos, sin) * (RET_DK ** -0.5))
            dp_ref[:, 2 * RET_W + lo:2 * RET_W + lo + RET_DK] = _bf(dv[h])
            dp_ref[:, 3 * RET_W + lo:3 * RET_W + lo + RET_DK] = _bf(dg[h])

    steps = nc // RET_CHUNKS_PER_STEP
    sp = _ret_specs(steps, True)
    return pl.pallas_call(
        body,
        out_shape=jax.ShapeDtypeStruct((s, 4 * RET_W), BF16),
        grid=(steps,),
        in_specs=[sp["proj"], sp["half"], sp["half"], sp["intra"], sp["dec"], sp["dec"], sp["cd"],
                  sp["state"], sp["wide"]],
        out_specs=sp["proj"],
        scratch_shapes=[pltpu.VMEM((RET_HEADS, RET_DK, RET_DK), F32)],
        compiler_params=_params("arbitrary"), name=name)(p_r, cos, sin, intra_t, qd_t, kd_t, cd_t, states, dy_r)


CONV_SLAB = 256
_CONV_SLABS = [slice(i * CONV_SLAB, (i + 1) * CONV_SLAB) for i in range(SSD_CD // CONV_SLAB)]


def _conv_taps(ext, w):
    acc = w[SSD_CONV - 1:SSD_CONV] * ext
    for j in range(SSD_CONV - 1):
        acc = acc + w[j:j + 1] * pltpu.roll(ext, SSD_CONV - 1 - j, axis=0)
    return acc


def _conv_fwd(xbc_raw, conv_w, conv_b, name):
    s = xbc_raw.shape[0]
    t8 = CONV_TILE // 8

    def body(cur_ref, prev_ref, w_ref, b_ref, o_ref, slope_ref):
        keep = (pl.program_id(0) > 0).astype(F32)
        for sl in _CONV_SLABS:
            ext = jnp.concatenate([prev_ref[:, sl] * keep, cur_ref[:, sl]], axis=0)
            u = _conv_taps(ext, w_ref[:, sl])[8:] + b_ref[:, sl]
            sg = _sigmoid(u)
            o_ref[:, sl] = u * sg
            slope_ref[:, sl] = _bf(sg * (1.0 + u * (1.0 - sg)))

    out = jax.ShapeDtypeStruct((s, SSD_CD), F32)
    return pl.pallas_call(
        body, out_shape=(out, jax.ShapeDtypeStruct((s, SSD_CD), BF16)), grid=(s // CONV_TILE,),
        in_specs=[_row_spec(SSD_CD, CONV_TILE),
                  pl.BlockSpec((8, SSD_CD), lambda i: (jnp.maximum(i * t8 - 1, 0), 0)),
                  _full_spec((SSD_CONV, SSD_CD)), _full_spec((1, SSD_CD))],
        out_specs=(_row_spec(SSD_CD, CONV_TILE), _row_spec(SSD_CD, CONV_TILE)),
        compiler_params=_params("parallel"), name=name)(xbc_raw, xbc_raw, conv_w, conv_b)


def _conv_bwd(xbc_raw, conv_w, slope, dact, name):
    s = xbc_raw.shape[0]
    nt = s // CONV_TILE
    t8 = CONV_TILE // 8
    rows = CONV_TILE + 8

    def body(cur_ref, w_ref, s_ref, snext_ref, d_ref, dnext_ref, dx_ref, dw_ref, db_ref):
        i = pl.program_id(0)

        @pl.when(i == 0)
        def _():
            dw_ref[...] = jnp.zeros_like(dw_ref)
            db_ref[...] = jnp.zeros_like(db_ref)

        keep_next = (i < nt - 1).astype(F32)
        for sl in _CONV_SLABS:
            w = w_ref[:, sl]
            cur = cur_ref[:, sl]
            duc = d_ref[:, sl].astype(F32) * s_ref[:, sl].astype(F32)
            after = (dnext_ref[:, sl].astype(F32) * snext_ref[:, sl].astype(F32))[:8] * keep_next
            du = jnp.concatenate([duc, after], axis=0)
            dx = w[SSD_CONV - 1:SSD_CONV] * duc
            for j in range(SSD_CONV - 1):
                ahead = pltpu.roll(du, rows - (SSD_CONV - 1 - j), axis=0)[:CONV_TILE]
                dx = dx + w[j:j + 1] * ahead
                dw_ref[j:j + 1, sl] += jnp.sum(cur * ahead, axis=0, keepdims=True)
            dx_ref[:, sl] = _bf(dx)
            dw_ref[SSD_CONV - 1:SSD_CONV, sl] += jnp.sum(cur * duc, axis=0, keepdims=True)
            db_ref[:, sl] += jnp.sum(duc, axis=0, keepdims=True)

    row = _row_spec(SSD_CD, CONV_TILE)
    nxt16 = pl.BlockSpec((16, SSD_CD), lambda i: (jnp.minimum((i + 1) * (t8 // 2), s // 16 - 1), 0))
    return pl.pallas_call(
        body,
        out_shape=(jax.ShapeDtypeStruct((s, SSD_CD), BF16), jax.ShapeDtypeStruct((SSD_CONV, SSD_CD), F32),
                   jax.ShapeDtypeStruct((1, SSD_CD), F32)),
        grid=(nt,),
        in_specs=[row, _full_spec((SSD_CONV, SSD_CD)), row, nxt16, row, nxt16],
        out_specs=(row, _full_spec((SSD_CONV, SSD_CD)), _full_spec((1, SSD_CD))),
        compiler_params=_params("arbitrary"), name=name)(xbc_raw, conv_w, slope, slope, dact, dact)


def _head_select():
    e = np.zeros((HEAD_PAD, SSD_W), np.float32)
    for h in range(SSD_HEADS):
        e[h, h * SSD_P:(h + 1) * SSD_P] = 1.0
    return jnp.asarray(e, BF16), jnp.asarray(e.T, BF16)


def _pad_heads(v):
    return jnp.pad(v.reshape(1, SSD_HEADS).astype(F32), ((0, 0), (0, HEAD_PAD - SSD_HEADS)))


def _ssd_masks():
    r = _iota((CHUNK, SSD_QW), 0)
    c = _iota((CHUNK, SSD_QW), 1) % SSD_P
    itile = (r == c).astype(F32)
    ctile = r >= c
    rb = _iota((SSD_QW, SSD_QW), 0) // SSD_P
    cb = _iota((SSD_QW, SSD_QW), 1) // SSD_P
    return itile, ctile, rb == cb


_TILES = [slice(i * SSD_QW, (i + 1) * SSD_QW) for i in range(SSD_W // SSD_QW)]


def _ssd_heads(dtr_ref, dtb_ref, alog_ref):
    u = dtr_ref[...] + dtb_ref[...]
    dt = _softplus(u)
    nexp = -jnp.exp(alog_ref[...])
    return u, dt, nexp, dt * nexp


def _ssd_group(xbc_ref, acol_ref, dte_ref, gs):
    acol = acol_ref[:, gs]
    alast = acol[CHUNK - 1:CHUNK, :]
    xs, dte = xbc_ref[:, gs], dte_ref[:, gs]
    return dict(xs=xs, dte=dte, xdt=xs * dte, ea=jnp.exp(acol), tail=jnp.exp(alast - acol), eal=jnp.exp(alast))


def _silu_gate(z_ref, y_ref, tl):
    zv = z_ref[:, tl]
    sz = _sigmoid(zv)
    return zv, sz, y_ref[:, tl] * (zv * sz)


def _tile4(x):
    return jnp.concatenate([x, x, x, x], axis=0)


def _fold4(x):
    return x[0:CHUNK] + x[CHUNK:2 * CHUNK] + x[2 * CHUNK:3 * CHUNK] + x[3 * CHUNK:4 * CHUNK]


def _ssd_specs(steps, rev):
    cidx = (lambda c: steps - 1 - c) if rev else (lambda c: c)
    return dict(
        xbc=pl.BlockSpec((STEP_ROWS, SSD_CD), lambda c: (cidx(c), 0)),
        dt=pl.BlockSpec((STEP_ROWS, HEAD_PAD), lambda c: (cidx(c), 0)),
        wide=pl.BlockSpec((STEP_ROWS, SSD_W), lambda c: (cidx(c), 0)),
        state=pl.BlockSpec((CHUNKS_PER_STEP, SSD_N, SSD_W), lambda c: (cidx(c), 0, 0)),
        head=_full_spec((1, HEAD_PAD)),
        roww=_full_spec((1, SSD_W)),
        e=_full_spec((HEAD_PAD, SSD_W)),
        et=_full_spec((SSD_W, HEAD_PAD)),
    )


_GROUPS = range(SSD_GROUPS)
_GROUP_LANES = [slice(g * SSD_GW, (g + 1) * SSD_GW) for g in _GROUPS]
_QUADS = [(g, slice(g * SSD_GW + q * SSD_QW, g * SSD_GW + (q + 1) * SSD_QW), slice(q * SSD_QW, (q + 1) * SSD_QW))
          for g in _GROUPS for q in range(SSD_GW // SSD_QW)]


def _ssd_bc(xbc_ref):
    bg = [_bf(xbc_ref[:, B_OFF + g * SSD_N:B_OFF + (g + 1) * SSD_N]) for g in _GROUPS]
    cg = [_bf(xbc_ref[:, C_OFF + g * SSD_N:C_OFF + (g + 1) * SSD_N]) for g in _GROUPS]
    return bg, cg


def _ssd_decay(aq, itile, ctile):
    arow = jnp.sum(aq * itile, axis=0, keepdims=True)
    return jnp.exp(jnp.where(ctile, aq - arow, -jnp.inf))


def _ssd_blockdiag(xq, bdmask):
    return jnp.where(bdmask, _tile4(_bf(xq)), jnp.zeros((), BF16))


def _ssd_fwd(xbc, dt_raw, z, dt_bias, a_log, dske, norm_w, name):
    s = xbc.shape[0]
    nc = s // CHUNK
    e_sel, _ = _head_select()

    def body(xbc_ref, dtr_ref, z_ref, dtb_ref, alog_ref, dske_ref, nw_ref, e_ref,
             yraw_ref, ys_ref, st_ref, acol_ref, dte_ref, state):
        @pl.when(pl.program_id(0) == 0)
        def _():
            state[...] = jnp.zeros_like(state)

        for sub in range(CHUNKS_PER_STEP):
            rows = _chunk_rows(sub)
            chunk(xbc_ref.at[rows], dtr_ref.at[rows], z_ref.at[rows], dtb_ref, alog_ref, dske_ref, nw_ref, e_ref,
                  yraw_ref.at[rows], ys_ref.at[rows], st_ref.at[sub], acol_ref.at[rows], dte_ref.at[rows], state)

    def chunk(xbc_ref, dtr_ref, z_ref, dtb_ref, alog_ref, dske_ref, nw_ref, e_ref,
              yraw_ref, ys_ref, st_ref, acol_ref, dte_ref, state):
        _, dt, _, a = _ssd_heads(dtr_ref, dtb_ref, alog_ref)
        tril = _bf((_iota((CHUNK, CHUNK), 0) >= _iota((CHUNK, CHUNK), 1)).astype(F32))
        ac3, dt3 = _split3(_dot_exact_r(tril, a)), _split3(dt)
        for tl in _TILES:
            e_t = e_ref[:, tl]
            acol_ref[:, tl] = _dot(ac3[0], e_t) + _dot(ac3[1], e_t) + _dot(ac3[2], e_t)
            dte_ref[:, tl] = _dot(dt3[0], e_t) + _dot(dt3[1], e_t)

        itile, ctile, bdmask = _ssd_masks()
        st_ref[...] = state[...]
        bg, cg = _ssd_bc(xbc_ref)
        for g0 in range(0, SSD_GROUPS, SSD_BLOCK_GROUPS):
            gids = range(g0, g0 + SSD_BLOCK_GROUPS)
            quads = [(g, ql, qs) for g, ql, qs in _QUADS if g in gids]
            q = {g: _ssd_group(xbc_ref, acol_ref, dte_ref, _GROUP_LANES[g]) for g in gids}
            stg = {g: state[:, _GROUP_LANES[g]] for g in gids}
            cbt = {g: _dot_nt(cg[g], _tile4(bg[g])) for g in gids}
            ys = {g: _dot(cg[g], _bf(stg[g])) for g in gids}
            dq = [_ssd_decay(acol_ref[:, ql], itile, ctile) for _, ql, _ in quads]
            xbd = [_ssd_blockdiag(q[g]["xdt"][:, qs], bdmask) for g, _, qs in quads]
            yq = [_dot(_bf(cbt[g] * dq[i]), xbd[i]) + ys[g][:, qs] * q[g]["ea"][:, qs]
                  + dske_ref[:, ql] * q[g]["xs"][:, qs] for i, (g, ql, qs) in enumerate(quads)]
            new_st = {g: stg[g] * q[g]["eal"] + _dot_tn(bg[g], _bf(q[g]["xdt"] * q[g]["tail"])) for g in gids}
            for i, (_, ql, _) in enumerate(quads):
                yraw_ref[:, ql] = yq[i]
            for g in gids:
                state[:, _GROUP_LANES[g]] = new_st[g]

        sq = jnp.zeros((CHUNK, SSD_QW), F32)
        for tl in _TILES:
            t = _silu_gate(z_ref, yraw_ref, tl)[2]
            sq = sq + t * t
        rstd = lax.rsqrt(jnp.sum(sq, axis=1, keepdims=True) * (1.0 / SSD_W) + EPS)
        for tl in _TILES:
            ys_ref[:, tl] = _bf(_silu_gate(z_ref, yraw_ref, tl)[2] * rstd * nw_ref[:, tl])

    steps = nc // CHUNKS_PER_STEP
    sp = _ssd_specs(steps, False)
    wide = jax.ShapeDtypeStruct((s, SSD_W), F32)
    return pl.pallas_call(
        body,
        out_shape=(wide, jax.ShapeDtypeStruct((s, SSD_W), BF16), jax.ShapeDtypeStruct((nc, SSD_N, SSD_W), F32),
                   wide, wide),
        grid=(steps,),
        in_specs=[sp["xbc"], sp["dt"], sp["wide"], sp["head"], sp["head"], sp["roww"], sp["roww"], sp["e"]],
        out_specs=(sp["wide"], sp["wide"], sp["state"], sp["wide"], sp["wide"]),
        scratch_shapes=[pltpu.VMEM((SSD_N, SSD_W), F32)],
        compiler_params=_params("arbitrary"), name=name,
    )(xbc, dt_raw, z, dt_bias, a_log, dske, norm_w, e_sel)


def _ssd_bwd(xbc, dt_raw, z, yraw, states, dys, acol, dte, dt_bias, a_log, dske, norm_w, name, rider):
    s = xbc.shape[0]
    nc = s // CHUNK
    steps = nc // CHUNKS_PER_STEP
    _, et_sel = _head_select()
    n_in, n_out, n_scratch = 13, 7, 5
    n_rin, n_rout = len(rider.inputs), len(rider.out_shapes)

    def body(*refs):
        ins, refs = refs[:n_in], refs[n_in:]
        r_ins, refs = refs[:n_rin], refs[n_rin:]
        outs, refs = refs[:n_out], refs[n_out:]
        r_outs, refs = refs[:n_rout], refs[n_rout:]
        ride = functools.partial(rider.run, pl.program_id(0), steps, r_ins, r_outs, refs[n_scratch],
                                 refs[n_scratch + 1])
        ride(last=False)
        compute(*ins, *outs, *refs[:n_scratch])
        ride(last=True)

    def compute(xbc_ref, dtr_ref, z_ref, yraw_ref, st_ref, dys_ref, acol_ref, dte_ref, dtb_ref, alog_ref,
                dske_ref, nw_ref, et_ref, dxbc_ref, dz_ref, ddtr_ref, ddtb_ref, dalog_ref, ddsk_ref, dnw_ref,
                dstate, dsk_acc, dy_s, dacol_s, ddte_s):
        @pl.when(pl.program_id(0) == 0)
        def _():
            dstate[...] = jnp.zeros_like(dstate)
            dsk_acc[...] = jnp.zeros_like(dsk_acc)
            ddtb_ref[...] = jnp.zeros_like(ddtb_ref)
            dalog_ref[...] = jnp.zeros_like(dalog_ref)
            dnw_ref[...] = jnp.zeros_like(dnw_ref)

        for sub in reversed(range(CHUNKS_PER_STEP)):
            rows = _chunk_rows(sub)
            chunk(xbc_ref.at[rows], dtr_ref.at[rows], z_ref.at[rows], yraw_ref.at[rows], st_ref.at[sub],
                  dys_ref.at[rows], acol_ref.at[rows], dte_ref.at[rows], dtb_ref, alog_ref, dske_ref, nw_ref, et_ref,
                  dxbc_ref.at[rows], dz_ref.at[rows], ddtr_ref.at[rows], ddtb_ref, dalog_ref, dnw_ref,
                  dstate, dsk_acc, dy_s, dacol_s, ddte_s)

        @pl.when(pl.program_id(0) == steps - 1)
        def _():
            ddsk_ref[...] = _dot_exact_l(jnp.broadcast_to(dsk_acc[...], (8, SSD_W)), et_ref[...])[0:1]

    def chunk(xbc_ref, dtr_ref, z_ref, yraw_ref, st_ref, dys_ref, acol_ref, dte_ref, dtb_ref, alog_ref,
              dske_ref, nw_ref, et_ref, dxbc_ref, dz_ref, ddtr_ref, ddtb_ref, dalog_ref, dnw_ref,
              dstate, dsk_acc, dy_s, dacol_s, ddte_s):
        itile, ctile, bdmask = _ssd_masks()
        last_row = (_iota((CHUNK, 1), 0) == CHUNK - 1).astype(F32)

        s1 = jnp.zeros((CHUNK, SSD_QW), F32)
        s2 = jnp.zeros((CHUNK, SSD_QW), F32)
        for tl in _TILES:
            t = _silu_gate(z_ref, yraw_ref, tl)[2]
            s1 = s1 + t * t
            s2 = s2 + dys_ref[:, tl] * nw_ref[:, tl] * t
        rstd = lax.rsqrt(jnp.sum(s1, axis=1, keepdims=True) * (1.0 / SSD_W) + EPS)
        back = rstd * rstd * rstd * jnp.sum(s2, axis=1, keepdims=True) * (1.0 / SSD_W)
        for tl in _TILES:
            zv, sz, t = _silu_gate(z_ref, yraw_ref, tl)
            dysv = dys_ref[:, tl]
            dt_ = rstd * (dysv * nw_ref[:, tl]) - t * back
            dnw_ref[:, tl] += jnp.sum(dysv * t * rstd, axis=0, keepdims=True)
            dz_ref[:, tl] = _bf(dt_ * yraw_ref[:, tl] * (sz * (1.0 + zv * (1.0 - sz))))
            dy_t = dt_ * (zv * sz)
            dy_s[:, tl] = dy_t
            dsk_acc[:, tl] += jnp.sum(dy_t * xbc_ref[:, tl], axis=0, keepdims=True)

        gl = _GROUP_LANES
        bg, cg = _ssd_bc(xbc_ref)
        for g0 in range(0, SSD_GROUPS, SSD_BLOCK_GROUPS):
            gids = range(g0, g0 + SSD_BLOCK_GROUPS)
            quads = [(i, g, ql, qs) for i, (g, ql, qs) in enumerate(_QUADS) if g in gids]
            q = {g: _ssd_group(xbc_ref, acol_ref, dte_ref, gl[g]) for g in gids}
            stg = {g: st_ref[:, gl[g]] for g in gids}
            dsn = {g: dstate[:, gl[g]] for g in gids}
            stgb, dsnb = {g: _bf(stg[g]) for g in gids}, {g: _bf(dsn[g]) for g in gids}
            btile = {g: _tile4(bg[g]) for g in gids}
            cbt = {g: _dot_nt(cg[g], btile[g]) for g in gids}
            dyg = {g: dy_s[:, gl[g]] for g in gids}
            eag, tailg = {g: q[g]["ea"] for g in gids}, {g: q[g]["tail"] for g in gids}
            xdtg, ealg = {g: q[g]["xdt"] for g in gids}, {g: q[g]["eal"] for g in gids}
            dys_g = {g: _bf(dyg[g] * eag[g]) for g in gids}
            ysg = {g: _dot(cg[g], stgb[g]) for g in gids}
            dc0 = {g: _dot_nt(dys_g[g], stgb[g]) for g in gids}
            dst = {g: _dot_tn(cg[g], dys_g[g]) for g in gids}
            dwt = {g: _dot(bg[g], dsnb[g]) for g in gids}
            db0 = {g: _dot_nt(_bf(xdtg[g] * tailg[g]), dsnb[g]) for g in gids}
            dtl = {g: dwt[g] * xdtg[g] * tailg[g] for g in gids}
            dal_row = {g: jnp.sum(dtl[g], axis=0, keepdims=True)
                       + jnp.sum(dsn[g] * stg[g], axis=0, keepdims=True) * ealg[g] for g in gids}
            for g in gids:
                dstate[:, gl[g]] = dst[g] + dsn[g] * ealg[g]
            dq = {i: _ssd_decay(acol_ref[:, ql], itile, ctile) for i, _, ql, _ in quads}
            mq = {i: cbt[g] * dq[i] for i, g, _, _ in quads}
            xbd = {i: _ssd_blockdiag(xdtg[g][:, qs], bdmask) for i, g, _, qs in quads}
            dyq = {i: _bf(dyg[g][:, qs]) for i, g, _, qs in quads}
            dm = {i: _dot_nt(dyq[i], xbd[i]) for i, _, _, _ in quads}
            dxdt_q = {i: _fold4(jnp.where(bdmask, _dot_tn(_bf(mq[i]), dyq[i]), 0.0)) for i, _, _, _ in quads}
            eq = {i: dm[i] * mq[i] for i, _, _, _ in quads}
            dacol_q = {i: eq[i] - itile * jnp.sum(eq[i], axis=0, keepdims=True) for i, _, _, _ in quads}
            dcbt = {i: _bf(dm[i] * dq[i]) for i, _, _, _ in quads}
            dc_q = {i: _dot(dcbt[i], btile[g]) for i, g, _, _ in quads}
            db_q = {i: _fold4(_dot_tn(dcbt[i], cg[g])) for i, g, _, _ in quads}
            for g in gids:
                dxdt_g = jnp.concatenate([dxdt_q[2 * g], dxdt_q[2 * g + 1]], axis=1) + dwt[g] * tailg[g]
                dxbc_ref[:, gl[g]] = _bf(dyg[g] * dske_ref[:, gl[g]] + dxdt_g * q[g]["dte"])
                ddte_s[:, gl[g]] = dxdt_g * q[g]["xs"]
                dacol_s[:, gl[g]] = (jnp.concatenate([dacol_q[2 * g], dacol_q[2 * g + 1]], axis=1)
                                     + dyg[g] * (ysg[g] * eag[g]) - dtl[g] + last_row * dal_row[g])
                dxbc_ref[:, B_OFF + g * SSD_N:B_OFF + (g + 1) * SSD_N] = _bf(db0[g] + db_q[2 * g] + db_q[2 * g + 1])
                dxbc_ref[:, C_OFF + g * SSD_N:C_OFF + (g + 1) * SSD_N] = _bf(dc0[g] + dc_q[2 * g] + dc_q[2 * g + 1])

        dacum = jnp.zeros((CHUNK, HEAD_PAD), F32)
        ddt = jnp.zeros((CHUNK, HEAD_PAD), F32)
        for i, tl in enumerate(_TILES):
            et_t = et_ref[i * SSD_QW:(i + 1) * SSD_QW, :]
            dacum = dacum + _dot_exact_l(dacol_s[:, tl], et_t, pieces=2)
            ddt = ddt + _dot_exact_l(ddte_s[:, tl], et_t, pieces=2)
        u, _, nexp, a = _ssd_heads(dtr_ref, dtb_ref, alog_ref)
        triu = _bf((_iota((CHUNK, CHUNK), 1) >= _iota((CHUNK, CHUNK), 0)).astype(F32))
        da = _dot_exact_r(triu, dacum)
        ddt = ddt + da * nexp
        dalog_ref[...] += jnp.sum(da * a, axis=0, keepdims=True)
        du = ddt * _sigmoid(u)
        ddtr_ref[...] = _bf(du)
        ddtb_ref[...] += jnp.sum(du, axis=0, keepdims=True)

    sp = _ssd_specs(steps, True)
    return pl.pallas_call(
        body,
        out_shape=(jax.ShapeDtypeStruct((s, SSD_CD), BF16), jax.ShapeDtypeStruct((s, SSD_W), BF16),
                   jax.ShapeDtypeStruct((s, HEAD_PAD), BF16), jax.ShapeDtypeStruct((1, HEAD_PAD), F32),
                   jax.ShapeDtypeStruct((1, HEAD_PAD), F32), jax.ShapeDtypeStruct((1, HEAD_PAD), F32),
                   jax.ShapeDtypeStruct((1, SSD_W), F32), *rider.out_shapes),
        grid=(steps,),
        in_specs=[sp["xbc"], sp["dt"], sp["wide"], sp["wide"], sp["state"], sp["wide"], sp["wide"], sp["wide"],
                  sp["head"], sp["head"], sp["roww"], sp["roww"], sp["et"], *rider.in_specs()],
        out_specs=(sp["xbc"], sp["wide"], sp["dt"], sp["head"], sp["head"], sp["head"], sp["roww"],
                   *rider.out_specs()),
        scratch_shapes=[pltpu.VMEM((SSD_N, SSD_W), F32), pltpu.VMEM((1, SSD_W), F32),
                        *[pltpu.VMEM((CHUNK, SSD_W), F32)] * 3, *rider.scratch()],
        compiler_params=_params("arbitrary"), name=name,
    )(xbc, dt_raw, z, yraw, states, dys, acol, dte, dt_bias, a_log, dske, norm_w, et_sel, *rider.inputs)


W_IN_SHARD = IN_PROJ // N_DEV
W_IN_ROW_PARTS = ((0, 1152), (1152, 832), (1984, 840))


def _device_step(x, pos_col, target, norm1_w, conv_w_shard, conv_b, dt_bias, a_log, d_skip, ssd_norm_w, norm_f_w,
                 w_in_shard, br_ret_shard, br_ssd_shard, out_shard):
    inv_freq = jnp.asarray(ROPE_THETA ** (-np.arange(ROPE_HALF, dtype=np.float64) / ROPE_HALF), F32).reshape(1, ROPE_HALF)
    dtb, alog = _pad_heads(dt_bias), _pad_heads(a_log)
    dske = jnp.repeat(d_skip.reshape(SSD_HEADS).astype(F32), SSD_P).reshape(1, SSD_W)
    my_core = lax.axis_index("c")

    h, cos, sin, br_ret_shard, br_ssd_shard, out_shard, w_all = _rmsnorm_rope_fwd(
        x, norm1_w, pos_col, inv_freq, [br_ret_shard, br_ssd_shard, out_shard], "rmsnorm1_fwd",
        _gather_rider([w_in_shard], 1.0, 1.0))
    w_all = w_all.reshape(IN_PROJ, D_MODEL)
    w_dt = jnp.pad(w_all[OFF_DT:OFF_G], ((0, HEAD_PAD - SSD_HEADS), (0, 0)))
    w_g = w_all[OFF_G:]
    rows_r, rows_z, rows_xbc = (0, OFF_Z), (OFF_Z, SSD_W), (OFF_XBC, SSD_CD)
    gather = _gather_rider([br_ret_shard, br_ssd_shard, out_shard, conv_w_shard], relay_at=0.35, pass_at=0.6)
    p_r, all_ret, all_ssd, all_out, all_conv = _matmul(h, w_all, "nt", F32, "proj_ret", rider=gather, b_rows=rows_r)
    w_br_ret = all_ret.reshape(RET_W, D_MODEL)
    w_br_ssd = all_ssd.reshape(SSD_W, D_MODEL)
    w_out = all_out.reshape(D_MODEL, D_MODEL)
    conv_w = all_conv.transpose(1, 0, 2).reshape(SSD_CONV, SSD_CD)
    p_z = _matmul(h, w_all, "nt", F32, "proj_z", b_rows=rows_z)
    p_xbc = _matmul(h, w_all, "nt", F32, "proj_xbc", b_rows=rows_xbc)
    p_dt = _matmul(h, w_dt, "nt", F32, "proj_dt")
    p_g = _matmul(h, w_g, "nt", BF16, "proj_gates")
    y_r, ret_states = _retention_fwd(p_r, cos, sin, "retention_fwd")
    xbc_act, silu_slope = _conv_fwd(p_xbc, conv_w, conv_b, "conv_fwd")
    y_raw, y_s, ssd_states, acol, dte = _ssd_fwd(xbc_act, p_dt, p_z, dtb, alog, dske, ssd_norm_w, "ssd_fwd")
    pr = _matmul(y_r, w_br_ret, "nn", BF16, "branch_ret")
    ps = _matmul(y_s, w_br_ssd, "nn", BF16, "branch_ssd")
    merged = _merge_fwd(p_g, pr, ps, "merge_fwd")
    o = _matmul(merged, w_out, "nn", F32, "out_proj")
    dx2, g_norm_f, loss_acc = _final_fwd_bwd(x, o, norm_f_w, target, "final_norm_loss")

    g_w_out = _matmul(merged, dx2, "tn", BF16, "grad_w_out")
    dmerged = _matmul(dx2, w_out, "nt", F32, "d_merged")
    dpr, dps, dp_g = _merge_bwd(p_g, pr, ps, dmerged, "merge_bwd")
    g_w_br_ret = _matmul(y_r, dpr, "tn", BF16, "grad_w_br_ret")
    g_w_br_ssd = _matmul(y_s, dps, "tn", BF16, "grad_w_br_ssd")
    dy_r = _matmul(dpr, w_br_ret, "nt", F32, "d_y_ret")
    dy_s = _matmul(dps, w_br_ssd, "nt", F32, "d_y_ssd")
    scatter = _scatter_rider([g_w_out.reshape(N_DEV, -1, D_MODEL), g_w_br_ret.reshape(N_DEV, -1, D_MODEL),
                              g_w_br_ssd.reshape(N_DEV, -1, D_MODEL)])
    dxbc_act, dp_z, dp_dt, g_dtb, g_alog, g_dsk, g_ssd_norm, got_out, got_ret, got_ssd = _ssd_bwd(
        xbc_act, p_dt, p_z, y_raw, ssd_states, dy_s, acol, dte, dtb, alog, dske, ssd_norm_w, "ssd_bwd", scatter)
    dp_xbc, g_conv_w, g_conv_b = _conv_bwd(p_xbc, conv_w, silu_slope, dxbc_act, "conv_bwd")
    dp_r = _retention_bwd(p_r, cos, sin, ret_states, dy_r, "retention_bwd")
    g_w_in = jnp.concatenate([
        _matmul(dp_r, h, "tn", BF16, "grad_w_ret"),
        _matmul(dp_z, h, "tn", BF16, "grad_w_z"),
        _matmul(dp_xbc, h, "tn", BF16, "grad_w_xbc"),
        _matmul(dp_dt, h, "tn", BF16, "grad_w_dt")[:SSD_HEADS],
        _matmul(dp_g, h, "tn", BF16, "grad_w_gates"),
    ], axis=0)
    blocks = g_w_in.reshape(N_DEV, W_IN_SHARD, D_MODEL)
    dh, from_sibling = _matmul(dp_r, w_all, "nn", F32, "d_h_ret", rider=_sibling_rider(blocks), b_rows=rows_r)
    chip_sum = _chip_sum(blocks, from_sibling, my_core, "w_in_chip_sum")
    carriers = (("d_h_xbc", dp_xbc, w_all, rows_xbc), ("d_h_gates", dp_g, w_g, None), ("d_h_z", dp_z, w_all, rows_z))
    landed = []
    for (row0, rows), (nm, dp, w, b_rows) in zip(W_IN_ROW_PARTS, carriers):
        dt_pair = (dp_dt, w_dt) if nm == "d_h_z" else None
        dh, got = _matmul(dp, w, "nn", F32, nm, add=dh, rider=_chip_rider(chip_sum, row0, rows), b_rows=b_rows,
                          extra=dt_pair)
        landed.append(got)
    grad_x, g_norm1 = _rmsnorm_bwd(x, norm1_w, dh, dx2, "rmsnorm1_bwd")
    small = dict(norm1_w=g_norm1, conv_w=g_conv_w, conv_b=g_conv_b, dt_bias=g_dtb[:, :SSD_HEADS],
                 a_log=g_alog[:, :SSD_HEADS], d_skip=g_dsk[:, :SSD_HEADS], ssd_norm_w=g_ssd_norm,
                 norm_f_w=g_norm_f)
    big = dict(w_in=_sum_row_parts(landed, "w_in_sum"), w_br_ret=_sum_slots(got_ret, "w_br_ret_sum"),
               w_br_ssd=_sum_slots(got_ssd, "w_br_ssd_sum"), w_out=_sum_slots(got_out, "w_out_sum"))
    return loss_acc[0, 0], grad_x, small, big


def _all_reduce_small(vec, name):
    r, c = vec.shape

    def body(x_ref, out_ref, land, send_sems, recv_sems):
        x, y, cc = _mesh_pos()
        my_idx = 4 * x + 2 * y + cc
        land[my_idx] = x_ref[...]
        copies = []
        for k in range(1, N_DEV):
            px, py, pc = x ^ (k >> 2), y ^ ((k >> 1) & 1), cc ^ (k & 1)
            cp = pltpu.make_async_remote_copy(
                src_ref=x_ref, dst_ref=land.at[my_idx],
                send_sem=send_sems.at[k - 1], recv_sem=recv_sems.at[k - 1],
                device_id=(px, py, pc), device_id_type=MESH)
            cp.start()
            copies.append(cp)
        for k in range(1, N_DEV):
            px, py, pc = x ^ (k >> 2), y ^ ((k >> 1) & 1), cc ^ (k & 1)
            pltpu.make_async_remote_copy(
                src_ref=x_ref, dst_ref=land.at[4 * px + 2 * py + pc],
                send_sem=send_sems.at[k - 1], recv_sem=recv_sems.at[k - 1],
                device_id=(px, py, pc), device_id_type=MESH).wait_recv()
        for cp in copies:
            cp.wait_send()
        acc = land[0]
        for i in range(1, N_DEV):
            acc = acc + land[i]
        out_ref[...] = acc

    return pl.pallas_call(
        body,
        out_shape=jax.ShapeDtypeStruct((r, c), F32),
        in_specs=[pl.BlockSpec(memory_space=pltpu.VMEM)],
        out_specs=pl.BlockSpec(memory_space=pltpu.VMEM),
        scratch_shapes=[pltpu.VMEM((N_DEV, r, c), F32), pltpu.SemaphoreType.DMA((7,)),
                        pltpu.SemaphoreType.DMA((7,))],
        name=name)(vec)


_SMALL = ("norm1_w", "conv_w", "conv_b", "dt_bias", "a_log", "d_skip", "ssd_norm_w", "norm_f_w")
_SMALL_COLS = 128
_W_IN = "w_in"
_WEIGHTS = ("norm1_w", "w_in", "conv_w", "conv_b", "dt_bias", "a_log", "d_skip", "ssd_norm_w",
            "w_br_ret", "w_br_ssd", "w_out", "norm_f_w")


def _pack(parts):
    flat = jnp.concatenate([p.reshape(-1).astype(F32) for p in parts])
    rows = -(-flat.shape[0] // (8 * _SMALL_COLS)) * 8
    return jnp.pad(flat, (0, rows * _SMALL_COLS - flat.shape[0])).reshape(rows, _SMALL_COLS)


def _unpack(packed, shapes):
    flat = packed.reshape(-1)
    out, off = [], 0
    for shp in shapes:
        n = int(np.prod(shp))
        out.append(flat[off:off + n].reshape(shp))
        off += n
    return out


def kernel(x, positions, norm1_w, w_in, conv_w, conv_b, dt_bias, a_log, d_skip, ssd_norm_w, w_br_ret, w_br_ssd, w_out, norm_f_w, loss_target, m_norm1_w, m_w_in, m_conv_w, m_conv_b, m_dt_bias, m_a_log, m_d_skip, m_ssd_norm_w, m_w_br_ret, m_w_br_ssd, m_w_out, m_norm_f_w, v_norm1_w, v_w_in, v_conv_w, v_conv_b, v_dt_bias, v_a_log, v_d_skip, v_ssd_norm_w, v_w_br_ret, v_w_br_ssd, v_w_out, v_norm_f_w):
    w = dict(norm1_w=norm1_w, w_in=w_in, conv_w=conv_w, conv_b=conv_b, dt_bias=dt_bias, a_log=a_log,
             d_skip=d_skip, ssd_norm_w=ssd_norm_w, w_br_ret=w_br_ret, w_br_ssd=w_br_ssd, w_out=w_out,
             norm_f_w=norm_f_w)
    m = dict(norm1_w=m_norm1_w, w_in=m_w_in, conv_w=m_conv_w, conv_b=m_conv_b, dt_bias=m_dt_bias,
             a_log=m_a_log, d_skip=m_d_skip, ssd_norm_w=m_ssd_norm_w, w_br_ret=m_w_br_ret,
             w_br_ssd=m_w_br_ssd, w_out=m_w_out, norm_f_w=m_norm_f_w)
    v = dict(norm1_w=v_norm1_w, w_in=v_w_in, conv_w=v_conv_w, conv_b=v_conv_b, dt_bias=v_dt_bias,
             a_log=v_a_log, d_skip=v_d_skip, ssd_norm_w=v_ssd_norm_w, w_br_ret=v_w_br_ret,
             w_br_ssd=v_w_br_ssd, w_out=v_w_out, norm_f_w=v_norm_f_w)
    s = x.shape[1]
    my_idx = 4 * lax.axis_index("x") + 2 * lax.axis_index("y") + lax.axis_index("c")

    w[_W_IN], m[_W_IN], v[_W_IN] = w_in[0].T, m_w_in[0].T, v_w_in[0].T

    loss_part, grad_x, g_small, g_big = _device_step(
        x[0], positions.reshape(s, 1), loss_target[0], norm1_w, conv_w[0], conv_b, dt_bias, a_log, d_skip,
        ssd_norm_w, norm_f_w.reshape(1, D_MODEL), _cast_bf16(w[_W_IN], "cast_w_in"),
        w_br_ret[0], w_br_ssd[0], w_out[0])

    small_shapes = [(1, 1)] + [g_small[n].shape for n in _SMALL]
    summed = _unpack(_all_reduce_small(_pack([loss_part.reshape(1, 1)] + [g_small[n] for n in _SMALL]),
                                       "allreduce_small"), small_shapes)
    loss = summed[0].reshape(())
    grads = dict(zip(_SMALL, summed[1:]))
    conv_cols = SSD_CD // N_DEV
    grads["conv_w"] = lax.dynamic_slice_in_dim(grads["conv_w"], my_idx * conv_cols, conv_cols, axis=1)
    grads["norm_f_w"] = grads["norm_f_w"].reshape(D_MODEL)
    for n in ("norm1_w", "conv_w", "conv_b", "dt_bias", "a_log", "d_skip", "ssd_norm_w"):
        grads[n] = grads[n].reshape(w[n].shape)

    delta, new_m, new_v = {}, {}, {}
    for n in ("w_br_ret", "w_br_ssd", "w_out"):
        w[n], m[n], v[n] = w[n][0], m[n][0], v[n][0]
    for n in (_W_IN, "w_br_ret", "w_br_ssd", "w_out"):
        back = (lambda a: a.T[None]) if n == _W_IN else (lambda a: a[None])
        res = _adamw(w[n], g_big[n], m[n], v[n], "adamw_" + n)
        grads[n] = back(g_big[n])
        delta[n], new_m[n], new_v[n] = (back(a) for a in res)
    shapes = [w[n].shape for n in _SMALL]
    packed = _adamw(_pack([w[n] for n in _SMALL]), _pack([grads[n] for n in _SMALL]),
                    _pack([m[n] for n in _SMALL]), _pack([v[n] for n in _SMALL]), "adamw_small")
    for res, dst in zip(packed, (delta, new_m, new_v)):
        for n, a in zip(_SMALL, _unpack(res, shapes)):
            dst[n] = a

    return (loss, grad_x.reshape(x.shape), *[grads[n] for n in _WEIGHTS], *[delta[n] for n in _WEIGHTS],
            *[new_m[n] for n in _WEIGHTS], *[new_v[n] for n in _WEIGHTS])
```

```python
import functools

import numpy as np
import jax
import jax.numpy as jnp
from jax import lax
from jax.experimental import pallas as pl
from jax.experimental.pallas import tpu as pltpu

F32 = jnp.float32
BF16 = jnp.bfloat16

D_MODEL = 2048
CHUNK = 64
CHUNKS_PER_STEP = 2
STEP_ROWS = CHUNK * CHUNKS_PER_STEP
RET_CHUNKS_PER_STEP = 4
RET_STEP_ROWS = CHUNK * RET_CHUNKS_PER_STEP
EPS = 1e-6
N_DEV = 8

RET_HEADS = 8
RET_DK = 256
RET_W = RET_HEADS * RET_DK
ROPE_THETA = 10000.0
ROPE_HALF = RET_DK // 2

SSD_W = 4096
SSD_P = 64
SSD_HEADS = 64
SSD_GROUPS = 8
SSD_N = 128
SSD_GW = SSD_W // SSD_GROUPS
SSD_QW = 256
SSD_BLOCK_GROUPS = 2
SSD_CONV = 4
SSD_CD = SSD_W + 2 * SSD_GROUPS * SSD_N
HEAD_PAD = 128
B_OFF = SSD_W
C_OFF = SSD_W + SSD_GROUPS * SSD_N

ADAM_LR = 0.001
ADAM_B1 = 0.9
ADAM_B2 = 0.999
ADAM_EPS = 1e-08
ADAM_WD = 0.01
ADAM_STEP = 10

SPLITS = (RET_W, RET_W, RET_W, RET_W, SSD_W, SSD_CD, SSD_HEADS, D_MODEL, D_MODEL)
IN_PROJ = sum(SPLITS)
OFF_Z = 4 * RET_W
OFF_XBC = OFF_Z + SSD_W
OFF_DT = OFF_XBC + SSD_CD
OFF_G = OFF_DT + SSD_HEADS

ROW_TILE = 256
CONV_TILE = 128
MM_TILE = 1024
MM_TILE_WIDE = 2048
MM_TK = 2048
VMEM_CAP = 60 << 20

MESH = pl.DeviceIdType.MESH


def _dot(a, b):
    return lax.dot_general(a, b, (((1,), (0,)), ((), ())), preferred_element_type=F32)


def _dot_nt(a, b):
    return lax.dot_general(a, b, (((1,), (1,)), ((), ())), preferred_element_type=F32)


def _dot_tn(a, b):
    return lax.dot_general(a, b, (((0,), (0,)), ((), ())), preferred_element_type=F32)


def _bf(x):
    return x.astype(BF16)


def _split3(x):
    hi = x.astype(BF16)
    r = x - hi.astype(F32)
    mid = r.astype(BF16)
    lo = (r - mid.astype(F32)).astype(BF16)
    return hi, mid, lo


def _dot_exact_l(x, sel, pieces=3):
    hi, mid, lo = _split3(x)
    r = _dot(hi, sel) + _dot(mid, sel)
    return r + _dot(lo, sel) if pieces == 3 else r


def _dot_exact_r(sel, x):
    hi, mid, lo = _split3(x)
    return _dot(sel, hi) + _dot(sel, mid) + _dot(sel, lo)


def _sigmoid(x):
    return 1.0 / (1.0 + jnp.exp(-x))


def _softplus(x):
    return jnp.maximum(x, 0.0) + jnp.log(1.0 + jnp.exp(-jnp.abs(x)))


def _iota(shape, axis):
    return lax.broadcasted_iota(jnp.int32, shape, axis)


def _params(*sem):
    return pltpu.CompilerParams(dimension_semantics=sem)


class _Rider:
    def __init__(self, inputs, out_shapes, n_sems, phases):
        self.inputs, self.out_shapes, self.n_sems, self.phases = tuple(inputs), tuple(out_shapes), n_sems, phases

    def in_specs(self):
        return [pl.BlockSpec(memory_space=pl.ANY)] * len(self.inputs)

    def out_specs(self):
        return [pl.BlockSpec(memory_space=pl.ANY)] * len(self.out_shapes)

    def scratch(self):
        return [pltpu.SemaphoreType.DMA((self.n_sems,)), pltpu.SemaphoreType.DMA((self.n_sems,))]

    def run(self, step, n_steps, ins, outs, send_sems, recv_sems, last):
        for frac, fn in self.phases:
            if (frac >= 1.0) != last:
                continue
            at = min(int(frac * n_steps), n_steps - 1)

            @pl.when(step == at)
            def _(fn=fn):
                fn(ins, outs, send_sems, recv_sems)


def _mesh_pos():
    return lax.axis_index("x"), lax.axis_index("y"), lax.axis_index("c")


def _gather_rider(shards, relay_at, pass_at):
    n = len(shards)

    def tools(a, ins, outs, send_sems, recv_sems):
        x, y, cc = _mesh_pos()
        nbrs = [(1 - x, y), (x, 1 - y)]
        diag = (1 - x, 1 - y)
        relay_from, relay_to = (x ^ cc, y ^ (1 - cc)), (x ^ (1 - cc), y ^ cc)

        def slot(px, py, pc):
            return outs[a].at[4 * px + 2 * py + pc]

        def copy(k, block, to, src=None):
            return pltpu.make_async_remote_copy(
                src_ref=slot(*block) if src is None else src, dst_ref=slot(*block),
                send_sem=send_sems.at[8 * a + k], recv_sem=recv_sems.at[8 * a + k], device_id=to, device_id_type=MESH)

        me, sibling = (x, y, cc), (x, y, 1 - cc)
        return dict(
            mine=lambda: pltpu.make_async_copy(ins[a], slot(*me), send_sems.at[8 * a + 7]),
            first=lambda: [copy(0, me, sibling, src=ins[a])] + [copy(1 + j, me, (*chip, cc), src=ins[a])
                                                                for j, chip in enumerate(nbrs)],
            landed=lambda: [copy(1 + j, (*chip, cc), me) for j, chip in enumerate(nbrs)],
            relay=lambda: copy(3, (*relay_from, cc), (*relay_to, cc)),
            relayed_in=lambda: copy(3, (*diag, cc), me),
            passed=lambda: [copy(4 + j, (*chip, cc), sibling) for j, chip in enumerate(nbrs)],
            passed_diag=lambda: copy(6, (*diag, cc), sibling),
            from_sibling=lambda: [copy(0, sibling, me)] + [copy(4 + j, (*chip, 1 - cc), me)
                                                           for j, chip in enumerate(nbrs + [diag])])

    def start(*refs):
        for a in range(n):
            t = tools(a, *refs)
            t["mine"]().start()
            for cp in t["first"]():
                cp.start()

    def relay(*refs):
        for a in range(n):
            t = tools(a, *refs)
            for got in t["landed"]():
                got.wait_recv()
            t["relay"]().start()
            for cp in t["passed"]():
                cp.start()

    def pass_diag(*refs):
        for a in range(n):
            t = tools(a, *refs)
            t["relayed_in"]().wait_recv()
            t["passed_diag"]().start()

    def finish(*refs):
        for a in range(n):
            t = tools(a, *refs)
            for cp in t["from_sibling"]():
                cp.wait_recv()
            for cp in t["first"]() + [t["relay"]()] + t["passed"]() + [t["passed_diag"]()]:
                cp.wait_send()
            t["mine"]().wait()

    outs = [jax.ShapeDtypeStruct((N_DEV,) + s.shape, s.dtype) for s in shards]
    return _Rider(shards, outs, 8 * n, ((0.0, start), (relay_at, relay), (pass_at, pass_diag), (1.0, finish)))


def _scatter_rider(blocks):
    n = len(blocks)

    def copies(a, ins, outs, send_sems, recv_sems, with_back):
        x, y, cc = _mesh_pos()
        my_idx = 4 * x + 2 * y + cc
        mine = pltpu.make_async_copy(ins[a].at[my_idx], outs[a].at[my_idx], send_sems.at[8 * a + 7])
        out, back = [], []
        for k in range(1, N_DEV):
            px, py, pc = x ^ (k >> 2), y ^ ((k >> 1) & 1), cc ^ (k & 1)
            peer_idx = 4 * px + 2 * py + pc
            sems = dict(send_sem=send_sems.at[8 * a + k - 1], recv_sem=recv_sems.at[8 * a + k - 1],
                        device_id=(px, py, pc), device_id_type=MESH)
            out.append(pltpu.make_async_remote_copy(src_ref=ins[a].at[peer_idx], dst_ref=outs[a].at[my_idx], **sems))
            if with_back:
                back.append(pltpu.make_async_remote_copy(src_ref=ins[a].at[my_idx], dst_ref=outs[a].at[peer_idx], **sems))
        return mine, out, back

    def start(*refs):
        for a in range(n):
            mine, out, _ = copies(a, *refs, False)
            mine.start()
            for cp in out:
                cp.start()

    def finish(*refs):
        for a in range(n):
            mine, out, back = copies(a, *refs, True)
            for cp in back:
                cp.wait_recv()
            for cp in out:
                cp.wait_send()
            mine.wait()

    outs = [jax.ShapeDtypeStruct(b.shape, b.dtype) for b in blocks]
    return _Rider(blocks, outs, 8 * n, ((0.0, start), (1.0, finish)))


def _sibling_rider(blocks):
    _, r, c = blocks.shape

    def copies(ins, outs, send_sems, recv_sems):
        x, y, cc = _mesh_pos()
        return [pltpu.make_async_remote_copy(
            src_ref=ins[0].at[2 * j + 1 - cc], dst_ref=outs[0].at[j], send_sem=send_sems.at[j],
            recv_sem=recv_sems.at[j], device_id=(x, y, 1 - cc), device_id_type=MESH) for j in range(4)]

    def start(*refs):
        for cp in copies(*refs):
            cp.start()

    def finish(*refs):
        for cp in copies(*refs):
            cp.wait_recv()
        for cp in copies(*refs):
            cp.wait_send()

    return _Rider([blocks], [jax.ShapeDtypeStruct((4, r, c), blocks.dtype)], 4, ((0.0, start), (1.0, finish)))


def _chip_rider(partial, row0, rows):
    _, _, c = partial.shape

    def copies(ins, outs, send_sems, recv_sems, with_back):
        x, y, cc = _mesh_pos()
        my_chip = 2 * x + y

        def rows_of(j):
            return ins[0].at[j, pl.ds(row0, rows)]

        mine = pltpu.make_async_copy(rows_of(my_chip), outs[0].at[my_chip], send_sems.at[3])
        out, back = [], []
        for k in range(1, 4):
            px, py = x ^ (k >> 1), y ^ (k & 1)
            peer = 2 * px + py
            sems = dict(send_sem=send_sems.at[k - 1], recv_sem=recv_sems.at[k - 1], device_id=(px, py, cc),
                        device_id_type=MESH)
            out.append(pltpu.make_async_remote_copy(src_ref=rows_of(peer), dst_ref=outs[0].at[my_chip], **sems))
            if with_back:
                back.append(pltpu.make_async_remote_copy(src_ref=rows_of(my_chip), dst_ref=outs[0].at[peer], **sems))
        return mine, out, back

    def start(*refs):
        mine, out, _ = copies(*refs, False)
        mine.start()
        for cp in out:
            cp.start()

    def finish(*refs):
        mine, out, back = copies(*refs, True)
        for cp in back:
            cp.wait_recv()
        for cp in out:
            cp.wait_send()
        mine.wait()

    return _Rider([partial], [jax.ShapeDtypeStruct((4, rows, c), partial.dtype)], 4, ((0.0, start), (1.0, finish)))


def _matmul(a, b, mode, out_dtype, name, add=None, rider=None, b_rows=None, extra=None):
    b_shape = b.shape if b_rows is None else (b_rows[1], b.shape[1])
    if mode == "nn":
        (m, k), (k2, n) = a.shape, b_shape
    elif mode == "nt":
        (m, k), (n, k2) = a.shape, b_shape
    else:
        (k, m), (k2, n) = a.shape, b_shape
    assert k == k2, (a.shape, b_shape, mode)
    tm, tn, tk = min(m, MM_TILE), min(n, MM_TILE), min(k, MM_TK)
    if n % MM_TILE_WIDE == 0 and (k <= MM_TK or (add is None and jnp.dtype(out_dtype).itemsize == 2)):
        tn = MM_TILE_WIDE
    assert m % tm == 0 and n % tn == 0 and k % tk == 0, (m, n, k)
    nk = k // tk
    b_tile_rows = tn if mode == "nt" else tk
    b_off = 0 if b_rows is None else b_rows[0] // b_tile_rows
    assert b_rows is None or b_rows[0] % b_tile_rows == 0, (b_rows, b_tile_rows)
    dot = {"nn": _dot, "nt": _dot_nt, "tn": _dot_tn}[mode]

    gm, gn = m // tm, n // tn
    n_rin = len(rider.inputs) if rider else 0
    n_rout = len(rider.out_shapes) if rider else 0
    n_add = 1 if add is not None else 0

    n_extra = 2 if extra is not None else 0

    def body(*refs):
        a_ref, b_ref = refs[:2]
        add_ref = refs[2] if n_add else None
        extra_refs = refs[2 + n_add:2 + n_add + n_extra]
        first = 2 + n_add + n_extra
        r_ins = refs[first:first + n_rin]
        o_ref = refs[first + n_rin]
        r_outs = refs[first + n_rin + 1:first + n_rin + 1 + n_rout]
        scratch = refs[first + n_rin + 1 + n_rout:]
        acc_ref = scratch[0] if nk > 1 else None
        kk = pl.program_id(2)
        if rider:
            step = (pl.program_id(0) * gn + pl.program_id(1)) * nk + kk
            ride = functools.partial(rider.run, step, gm * gn * nk, r_ins, r_outs, scratch[-2], scratch[-1])
            ride(last=False)

        def finish(r):
            if extra_refs:
                r = r + _dot(_bf(extra_refs[0][...]), _bf(extra_refs[1][...]))
            if add_ref is not None:
                r = r + add_ref[...]
            o_ref[...] = r.astype(o_ref.dtype)

        if nk == 1:
            finish(dot(_bf(a_ref[...]), _bf(b_ref[...])))
        else:
            @pl.when(kk == 0)
            def _():
                acc_ref[...] = dot(_bf(a_ref[...]), _bf(b_ref[...]))

            @pl.when(jnp.logical_and(kk > 0, kk < nk - 1))
            def _():
                acc_ref[...] += dot(_bf(a_ref[...]), _bf(b_ref[...]))

            @pl.when(kk == nk - 1)
            def _():
                finish(acc_ref[...] + dot(_bf(a_ref[...]), _bf(b_ref[...])))
        if rider:
            ride(last=True)

    if mode == "nn":
        a_spec = pl.BlockSpec((tm, tk), lambda i, j, kk: (i, kk))
        b_spec = pl.BlockSpec((tk, tn), lambda i, j, kk: (kk + b_off, j))
    elif mode == "nt":
        a_spec = pl.BlockSpec((tm, tk), lambda i, j, kk: (i, kk))
        b_spec = pl.BlockSpec((tn, tk), lambda i, j, kk: (j + b_off, kk))
    else:
        a_spec = pl.BlockSpec((tk, tm), lambda i, j, kk: (kk, i))
        b_spec = pl.BlockSpec((tk, tn), lambda i, j, kk: (kk + b_off, j))
    o_spec = pl.BlockSpec((tm, tn), lambda i, j, kk: (i, j))
    in_specs = [a_spec, b_spec] + ([o_spec] if add is not None else [])
    args = (a, b) + ((add,) if add is not None else ())
    if extra is not None:
        (m2, k_extra), (k_extra2, n2) = extra[0].shape, extra[1].shape
        assert (m2, n2, k_extra) == (m, n, k_extra2), (extra[0].shape, extra[1].shape)
        in_specs += [pl.BlockSpec((tm, k_extra), lambda i, j, kk: (i, 0)),
                     pl.BlockSpec((k_extra, tn), lambda i, j, kk: (0, j))]
        args += tuple(extra)
    block_bytes = (tm * tk * a.dtype.itemsize + tk * tn * b.dtype.itemsize
                   + tm * tn * (jnp.dtype(out_dtype).itemsize + (4 if add is not None else 0)))
    vmem = min(2 * block_bytes + 2 * tm * tn * 4 + 2 * (tm + tn) * tk + (4 << 20), VMEM_CAP)
    out_shape = jax.ShapeDtypeStruct((m, n), out_dtype)
    scratch = [pltpu.VMEM((tm, tn), F32)] if nk > 1 else []
    if rider:
        in_specs = in_specs + rider.in_specs()
        args = args + rider.inputs
        out_shape = (out_shape, *rider.out_shapes)
        o_spec = (o_spec, *rider.out_specs())
        scratch = scratch + rider.scratch()
    sem = ("arbitrary",) * 3 if rider else ("parallel", "parallel", "arbitrary")
    return pl.pallas_call(
        body,
        out_shape=out_shape,
        grid=(gm, gn, nk),
        in_specs=in_specs,
        out_specs=o_spec,
        scratch_shapes=scratch,
        compiler_params=pltpu.CompilerParams(dimension_semantics=sem, vmem_limit_bytes=int(vmem)),
        name=name,
    )(*args)


def _row_spec(width, tile=ROW_TILE):
    return pl.BlockSpec((tile, width), lambda i: (i, 0))


def _full_spec(shape):
    nd = len(shape)
    return pl.BlockSpec(shape, lambda *_: (0,) * nd)


def _tiling_2d(r, c):
    if r <= ROW_TILE or r % ROW_TILE == 0:
        tr = min(r, ROW_TILE)
        return (tr, c), r // tr, (lambda i: (i, 0))
    tc = 128 if r > 4096 else 256
    assert c % tc == 0, (r, c)
    return (r, tc), c // tc, (lambda i: (0, i))


def _spec_2d(r, c):
    blk, grid, idx = _tiling_2d(r, c)
    return pl.BlockSpec(blk, idx), grid


def _cast_bf16(x, name):
    r, c = x.shape
    sp, grid = _spec_2d(r, c)

    def body(x_ref, o_ref):
        o_ref[...] = _bf(x_ref[...])

    return pl.pallas_call(
        body, out_shape=jax.ShapeDtypeStruct((r, c), BF16), grid=(grid,), in_specs=[sp], out_specs=sp,
        compiler_params=_params("parallel"), name=name)(x)


def _rmsnorm_rope_fwd(x, w, pos_col, inv_freq, to_bf16, name, rider):
    s, d = x.shape
    steps = s // ROW_TILE
    n_cast, n_rin, n_rout = len(to_bf16), len(rider.inputs), len(rider.out_shapes)

    def body(x_ref, w_ref, p_ref, f_ref, *refs):
        c_ins, refs = refs[:n_cast], refs[n_cast:]
        r_ins, refs = refs[:n_rin], refs[n_rin:]
        (o_ref, cos_ref, sin_ref), refs = refs[:3], refs[3:]
        c_outs, refs = refs[:n_cast], refs[n_cast:]
        ride = functools.partial(rider.run, pl.program_id(0), steps, r_ins, refs[:n_rout], refs[-2], refs[-1])
        ride(last=False)

        @pl.when(pl.program_id(0) == 0)
        def _():
            for c_in, c_out in zip(c_ins, c_outs):
                c_out[...] = _bf(c_in[...])

        xv = x_ref[...]
        rstd = lax.rsqrt(jnp.mean(xv * xv, axis=1, keepdims=True) + EPS)
        o_ref[...] = _bf(xv * rstd * w_ref[...])
        ang = p_ref[...].astype(F32) * f_ref[...]
        cos_ref[...] = jnp.cos(ang)
        sin_ref[...] = jnp.sin(ang)
        ride(last=True)

    table = jax.ShapeDtypeStruct((s, ROPE_HALF), F32)
    whole = [_full_spec(a.shape) for a in to_bf16]
    return pl.pallas_call(
        body,
        out_shape=(jax.ShapeDtypeStruct((s, d), BF16), table, table,
                   *[jax.ShapeDtypeStruct(a.shape, BF16) for a in to_bf16], *rider.out_shapes),
        grid=(steps,),
        in_specs=[_row_spec(d), _full_spec((1, d)), _row_spec(1), _full_spec((1, ROPE_HALF)), *whole,
                  *rider.in_specs()],
        out_specs=(_row_spec(d), _row_spec(ROPE_HALF), _row_spec(ROPE_HALF), *whole, *rider.out_specs()),
        scratch_shapes=rider.scratch(),
        compiler_params=_params("arbitrary"), name=name)(x, w, pos_col, inv_freq, *to_bf16, *rider.inputs)


def _rmsnorm_bwd(x, w, dh, dres, name):
    s, d = x.shape

    def body(x_ref, w_ref, dh_ref, dres_ref, dx_ref, dw_ref):
        @pl.when(pl.program_id(0) == 0)
        def _():
            dw_ref[...] = jnp.zeros_like(dw_ref)

        xv = x_ref[...]
        rstd = lax.rsqrt(jnp.mean(xv * xv, axis=1, keepdims=True) + EPS)
        xhat = xv * rstd
        dhv = dh_ref[...]
        dxhat = dhv * w_ref[...]
        dx = rstd * (dxhat - xhat * jnp.mean(dxhat * xhat, axis=1, keepdims=True))
        dx_ref[...] = dx + dres_ref[...]
        dw_ref[...] += jnp.sum(dhv * xhat, axis=0, keepdims=True)

    return pl.pallas_call(
        body,
        out_shape=(jax.ShapeDtypeStruct((s, d), F32), jax.ShapeDtypeStruct((1, d), F32)),
        grid=(s // ROW_TILE,),
        in_specs=[_row_spec(d), _full_spec((1, d)), _row_spec(d), _row_spec(d)],
        out_specs=(_row_spec(d), _full_spec((1, d))),
        compiler_params=_params("arbitrary"), name=name)(x, w, dh, dres)


def _merge_fwd(pg, p_r, p_s, name):
    s = pg.shape[0]

    def body(g_ref, r_ref, s_ref, o_ref):
        g = g_ref[...].astype(F32)
        o_ref[...] = _bf(_sigmoid(g[:, :D_MODEL]) * r_ref[...].astype(F32)
                         + _sigmoid(g[:, D_MODEL:]) * s_ref[...].astype(F32))

    return pl.pallas_call(
        body, out_shape=jax.ShapeDtypeStruct((s, D_MODEL), BF16), grid=(s // ROW_TILE,),
        in_specs=[_row_spec(2 * D_MODEL), _row_spec(D_MODEL), _row_spec(D_MODEL)],
        out_specs=_row_spec(D_MODEL), compiler_params=_params("parallel"), name=name)(pg, p_r, p_s)


def _merge_bwd(pg, p_r, p_s, dm, name):
    s = pg.shape[0]

    def body(g_ref, r_ref, s_ref, dm_ref, dr_ref, ds_ref, dg_ref):
        g = g_ref[...].astype(F32)
        sr, ss = _sigmoid(g[:, :D_MODEL]), _sigmoid(g[:, D_MODEL:])
        d = dm_ref[...]
        dr_ref[...] = _bf(d * sr)
        ds_ref[...] = _bf(d * ss)
        dg_ref[:, :D_MODEL] = _bf(d * r_ref[...].astype(F32) * sr * (1.0 - sr))
        dg_ref[:, D_MODEL:] = _bf(d * s_ref[...].astype(F32) * ss * (1.0 - ss))

    o = jax.ShapeDtypeStruct((s, D_MODEL), BF16)
    return pl.pallas_call(
        body, out_shape=(o, o, jax.ShapeDtypeStruct((s, 2 * D_MODEL), BF16)), grid=(s // ROW_TILE,),
        in_specs=[_row_spec(2 * D_MODEL), _row_spec(D_MODEL), _row_spec(D_MODEL), _row_spec(D_MODEL)],
        out_specs=(_row_spec(D_MODEL), _row_spec(D_MODEL), _row_spec(2 * D_MODEL)),
        compiler_params=_params("parallel"), name=name)(pg, p_r, p_s, dm)


def _final_fwd_bwd(x, o, w, target, name):
    s, d = x.shape

    def body(x_ref, o_ref, w_ref, t_ref, dx_ref, dw_ref, loss_ref):
        @pl.when(pl.program_id(0) == 0)
        def _():
            dw_ref[...] = jnp.zeros_like(dw_ref)
            loss_ref[...] = jnp.zeros_like(loss_ref)

        x2 = x_ref[...] + o_ref[...]
        rstd = lax.rsqrt(jnp.mean(x2 * x2, axis=1, keepdims=True) + EPS)
        xhat = x2 * rstd
        wv = w_ref[...]
        err = xhat * wv - t_ref[...]
        loss_ref[...] += jnp.sum(jnp.sum(err * err, axis=1, keepdims=True), axis=0, keepdims=True) * (0.5 / d)
        dy = err * (1.0 / d)
        dw_ref[...] += jnp.sum(dy * xhat, axis=0, keepdims=True)
        dxhat = dy * wv
        dx_ref[...] = rstd * (dxhat - xhat * jnp.mean(dxhat * xhat, axis=1, keepdims=True))

    return pl.pallas_call(
        body,
        out_shape=(jax.ShapeDtypeStruct((s, d), F32), jax.ShapeDtypeStruct((1, d), F32),
                   jax.ShapeDtypeStruct((8, 128), F32)),
        grid=(s // ROW_TILE,),
        in_specs=[_row_spec(d), _row_spec(d), _full_spec((1, d)), _row_spec(d)],
        out_specs=(_row_spec(d), _full_spec((1, d)), _full_spec((8, 128))),
        compiler_params=_params("arbitrary"), name=name)(x, o, w, target)


def _adam_update(w_ref, gv, m_ref, v_ref, d_ref, nm_ref, nv_ref):
    c1 = 1.0 / (1.0 - ADAM_B1 ** ADAM_STEP)
    c2 = 1.0 / (1.0 - ADAM_B2 ** ADAM_STEP)
    nm = ADAM_B1 * m_ref[...] + (1.0 - ADAM_B1) * gv
    nv = ADAM_B2 * v_ref[...] + (1.0 - ADAM_B2) * (gv * gv)
    d_ref[...] = -ADAM_LR * ((nm * c1) / (jnp.sqrt(nv * c2) + ADAM_EPS) + ADAM_WD * w_ref[...])
    nm_ref[...] = nm
    nv_ref[...] = nv


def _adamw(w, g, m, v, name):
    r, c = w.shape
    sp, grid = _spec_2d(r, c)

    def body(w_ref, g_ref, m_ref, v_ref, d_ref, nm_ref, nv_ref):
        _adam_update(w_ref, g_ref[...], m_ref, v_ref, d_ref, nm_ref, nv_ref)

    o = jax.ShapeDtypeStruct((r, c), F32)
    return pl.pallas_call(
        body, out_shape=(o, o, o), grid=(grid,), in_specs=[sp, sp, sp, sp], out_specs=(sp, sp, sp),
        compiler_params=_params("parallel"), name=name)(w, g, m, v)


def _adamw_row_parts(w, parts, m, v, name):
    r, c = w.shape
    rows = [p.shape[1] for p in parts]
    cols = SLAB_COLS // 2

    def body(w_ref, m_ref, v_ref, *refs):
        p_refs, (g_ref, d_ref, nm_ref, nv_ref) = refs[:len(parts)], refs[len(parts):]
        off = 0
        for p_ref, n in zip(p_refs, rows):
            acc = p_ref[0].astype(F32)
            for i in range(1, p_ref.shape[0]):
                acc = acc + p_ref[i].astype(F32)
            g_ref[off:off + n, :] = acc
            off += n
        _adam_update(w_ref, g_ref[...], m_ref, v_ref, d_ref, nm_ref, nv_ref)

    o = jax.ShapeDtypeStruct((r, c), F32)
    sp = pl.BlockSpec((r, cols), lambda i: (0, i))
    return pl.pallas_call(
        body, out_shape=(o, o, o, o), grid=(c // cols,),
        in_specs=[sp, sp, sp] + [pl.BlockSpec((p.shape[0], p.shape[1], cols), lambda i: (0, 0, i)) for p in parts],
        out_specs=(sp, sp, sp, sp), compiler_params=_params("parallel"), name=name)(w, m, v, *parts)


def _sum_slots(land, name):
    n, r, c = land.shape
    blk, grid, idx = _tiling_2d(r, c)

    def body(l_ref, o_ref):
        acc = l_ref[0].astype(F32)
        for i in range(1, n):
            acc = acc + l_ref[i].astype(F32)
        o_ref[...] = acc

    return pl.pallas_call(
        body, out_shape=jax.ShapeDtypeStruct((r, c), F32), grid=(grid,),
        in_specs=[pl.BlockSpec((n, *blk), lambda i: (0, *idx(i)))], out_specs=pl.BlockSpec(blk, idx),
        compiler_params=_params("parallel"), name=name)(land)


SLAB_COLS = 256


def _chip_sum(blocks, from_sibling, my_core, name):
    _, r, c = blocks.shape

    def body(core_ref, a_ref, b_ref, o_ref):
        o_ref[...] = _bf(a_ref[...].astype(F32) + b_ref[...].astype(F32))

    def slab(chip_of):
        return pl.BlockSpec((1, r, SLAB_COLS), lambda j, i, core: (chip_of(j, core), 0, i))

    grid_spec = pltpu.PrefetchScalarGridSpec(
        num_scalar_prefetch=1, grid=(4, c // SLAB_COLS),
        in_specs=[slab(lambda j, core: 2 * j + core[0]), slab(lambda j, core: j)],
        out_specs=slab(lambda j, core: j))
    return pl.pallas_call(
        body, out_shape=jax.ShapeDtypeStruct((4, r, c), BF16), grid_spec=grid_spec,
        compiler_params=_params("parallel", "parallel"), name=name,
    )(my_core.astype(jnp.int32).reshape(1), blocks, from_sibling)


def _sum_row_parts(parts, name):
    c = parts[0].shape[2]
    rows = [p.shape[1] for p in parts]

    def body(*refs):
        o_ref, off = refs[-1], 0
        for p_ref, n in zip(refs[:-1], rows):
            acc = p_ref[0].astype(F32)
            for i in range(1, p_ref.shape[0]):
                acc = acc + p_ref[i].astype(F32)
            o_ref[off:off + n, :] = acc
            off += n

    return pl.pallas_call(
        body, out_shape=jax.ShapeDtypeStruct((sum(rows), c), F32), grid=(c // SLAB_COLS,),
        in_specs=[pl.BlockSpec((p.shape[0], p.shape[1], SLAB_COLS), lambda i: (0, 0, i)) for p in parts],
        out_specs=pl.BlockSpec((sum(rows), SLAB_COLS), lambda i: (0, i)),
        compiler_params=_params("parallel"), name=name)(*parts)


def _retention_tables():
    lg = np.log1p(-(2.0 ** (-5.0 - np.arange(RET_HEADS, dtype=np.float64))))
    idx = np.arange(CHUNK, dtype=np.float64)
    intra = np.exp(np.abs(idx[:, None] - idx[None, :])[None] * lg[:, None, None])
    qd = np.exp((idx[None, :] + 1.0) * lg[:, None])
    kd = np.exp((CHUNK - 1.0 - idx[None, :]) * lg[:, None])
    cd = np.exp(CHUNK * lg)
    ones = np.ones((1, 1, RET_DK))
    return (jnp.asarray(intra, F32), jnp.asarray(qd[:, :, None] * ones, F32),
            jnp.asarray(kd[:, :, None] * ones, F32), jnp.asarray(cd[:, None, None] * ones, F32))


def _rope(t, cos, sin):
    t1, t2 = t[:, :ROPE_HALF], t[:, ROPE_HALF:]
    return jnp.concatenate([t1 * cos - t2 * sin, t2 * cos + t1 * sin], axis=1)


def _rope_t(d, cos, sin):
    d1, d2 = d[:, :ROPE_HALF], d[:, ROPE_HALF:]
    return jnp.concatenate([d1 * cos + d2 * sin, d2 * cos - d1 * sin], axis=1)


_HEADS = range(RET_HEADS)


def _ret_chunk_fwd(p_ref, cos, sin, intra_ref, qd_ref, st):
    def seg(i, h):
        return p_ref[:, i * RET_W + h * RET_DK:i * RET_W + (h + 1) * RET_DK]

    v = [seg(2, h) for h in _HEADS]
    qr = [_rope(seg(0, h), cos, sin) for h in _HEADS]
    kr = [_rope(seg(1, h), cos, sin) * (RET_DK ** -0.5) for h in _HEADS]
    qrb, vb = [_bf(a) for a in qr], [_bf(a) for a in v]
    sc = [_dot_nt(qrb[h], _bf(kr[h])) * intra_ref[h] for h in _HEADS]
    qs = [_dot(qrb[h], _bf(st[h])) for h in _HEADS]
    y = [_dot(_bf(sc[h]), vb[h]) + qs[h] * qd_ref[h] for h in _HEADS]
    g = [seg(3, h) for h in _HEADS]
    return dict(v=v, vb=vb, qr=qr, qrb=qrb, kr=kr, sc=sc, y=y, g=g)


def _group_norm(y):
    mu = jnp.mean(y, axis=1, keepdims=True)
    yc = y - mu
    rstd = lax.rsqrt(jnp.mean(yc * yc, axis=1, keepdims=True) + EPS)
    return yc * rstd, rstd


def _chunk_rows(sub):
    return pl.ds(sub * CHUNK, CHUNK)


def _ret_specs(steps, rev):
    cidx = (lambda c: steps - 1 - c) if rev else (lambda c: c)
    return dict(
        proj=pl.BlockSpec((RET_STEP_ROWS, 4 * RET_W), lambda c: (cidx(c), 0)),
        half=pl.BlockSpec((RET_STEP_ROWS, ROPE_HALF), lambda c: (cidx(c), 0)),
        wide=pl.BlockSpec((RET_STEP_ROWS, RET_W), lambda c: (cidx(c), 0)),
        state=pl.BlockSpec((RET_CHUNKS_PER_STEP, RET_HEADS, RET_DK, RET_DK), lambda c: (cidx(c), 0, 0, 0)),
        intra=_full_spec((RET_HEADS, CHUNK, CHUNK)),
        dec=_full_spec((RET_HEADS, CHUNK, RET_DK)),
        cd=_full_spec((RET_HEADS, 1, RET_DK)),
    )


def _retention_fwd(p_r, cos, sin, name):
    s = p_r.shape[0]
    nc = s // CHUNK
    intra_t, qd_t, kd_t, cd_t = _retention_tables()

    def body(p_ref, cos_ref, sin_ref, intra_ref, qd_ref, kd_ref, cd_ref, y_ref, st_ref, state):
        @pl.when(pl.program_id(0) == 0)
        def _():
            state[...] = jnp.zeros_like(state)

        for sub in range(RET_CHUNKS_PER_STEP):
            rows = _chunk_rows(sub)
            y_v, st_v = y_ref.at[rows], st_ref.at[sub]
            st = [state[h] for h in _HEADS]
            f = _ret_chunk_fwd(p_ref.at[rows], cos_ref[rows, :], sin_ref[rows, :], intra_ref, qd_ref, st)
            new_st = [st[h] * cd_ref[h] + _dot_tn(_bf(f["kr"][h] * kd_ref[h]), f["vb"][h]) for h in _HEADS]
            out = [_bf(_group_norm(f["y"][h])[0] * (f["g"][h] * _sigmoid(f["g"][h]))) for h in _HEADS]
            for h in _HEADS:
                st_v[h] = _bf(st[h])
                state[h] = new_st[h]
                y_v[:, h * RET_DK:(h + 1) * RET_DK] = out[h]

    steps = nc // RET_CHUNKS_PER_STEP
    sp = _ret_specs(steps, False)
    return pl.pallas_call(
        body,
        out_shape=(jax.ShapeDtypeStruct((s, RET_W), BF16),
                   jax.ShapeDtypeStruct((nc, RET_HEADS, RET_DK, RET_DK), BF16)),
        grid=(steps,),
        in_specs=[sp["proj"], sp["half"], sp["half"], sp["intra"], sp["dec"], sp["dec"], sp["cd"]],
        out_specs=(sp["wide"], sp["state"]),
        scratch_shapes=[pltpu.VMEM((RET_HEADS, RET_DK, RET_DK), F32)],
        compiler_params=_params("arbitrary"), name=name)(p_r, cos, sin, intra_t, qd_t, kd_t, cd_t)


def _retention_bwd(p_r, cos, sin, states, dy_r, name):
    s = p_r.shape[0]
    nc = s // CHUNK
    intra_t, qd_t, kd_t, cd_t = _retention_tables()

    def body(p_ref, cos_ref, sin_ref, intra_ref, qd_ref, kd_ref, cd_ref, st_ref, dy_ref, dp_ref, dstate):
        @pl.when(pl.program_id(0) == 0)
        def _():
            dstate[...] = jnp.zeros_like(dstate)

        for sub in reversed(range(RET_CHUNKS_PER_STEP)):
            rows = _chunk_rows(sub)
            chunk(p_ref.at[rows], cos_ref[rows, :], sin_ref[rows, :], intra_ref, qd_ref, kd_ref, cd_ref,
                  st_ref.at[sub], dy_ref.at[rows], dp_ref.at[rows], dstate)

    def chunk(p_ref, cos, sin, intra_ref, qd_ref, kd_ref, cd_ref, st_ref, dy_ref, dp_ref, dstate):
        st = [st_ref[h] for h in _HEADS]
        dsn = [dstate[h] for h in _HEADS]
        f = _ret_chunk_fwd(p_ref, cos, sin, intra_ref, qd_ref, st)
        vb, qrb, kr, g = f["vb"], f["qrb"], f["kr"], f["g"]
        norm = [_group_norm(f["y"][h]) for h in _HEADS]
        sg = [_sigmoid(g[h]) for h in _HEADS]
        dyr = [dy_ref[:, h * RET_DK:(h + 1) * RET_DK] for h in _HEADS]
        dyn = [dyr[h] * (g[h] * sg[h]) for h in _HEADS]
        dg = [dyr[h] * norm[h][0] * (sg[h] * (1.0 + g[h] * (1.0 - sg[h]))) for h in _HEADS]
        dy = [norm[h][1] * (dyn[h] - jnp.mean(dyn[h], axis=1, keepdims=True)
                            - norm[h][0] * jnp.mean(dyn[h] * norm[h][0], axis=1, keepdims=True)) for h in _HEADS]
        dyb, dsnb = [_bf(a) for a in dy], [_bf(a) for a in dsn]
        ds = [_bf(_dot_nt(dyb[h], vb[h]) * intra_ref[h]) for h in _HEADS]
        t = [_bf(dy[h] * qd_ref[h]) for h in _HEADS]
        dv = [_dot_tn(_bf(f["sc"][h]), dyb[h]) + _dot(_bf(kr[h] * kd_ref[h]), dsnb[h]) for h in _HEADS]
        dqr = [_dot(ds[h], _bf(kr[h])) + _dot_nt(t[h], _bf(st[h])) for h in _HEADS]
        dkr = [_dot_tn(ds[h], qrb[h]) + _dot_nt(vb[h], dsnb[h]) * kd_ref[h] for h in _HEADS]
        new_ds = [dsn[h] * cd_ref[h] + _dot_tn(qrb[h], t[h]) for h in _HEADS]
        for h in _HEADS:
            lo = h * RET_DK
            dstate[h] = new_ds[h]
            dp_ref[:, lo:lo + RET_DK] = _bf(_rope_t(dqr[h], cos, sin))
            dp_ref[:, RET_W + lo:RET_W + lo + RET_DK] = _bf(_rope_t(dkr[h], cos, sin) * (RET_DK ** -0.5))
            dp_ref[:, 2 * RET_W + lo:2 * RET_W + lo + RET_DK] = _bf(dv[h])
            dp_ref[:, 3 * RET_W + lo:3 * RET_W + lo + RET_DK] = _bf(dg[h])

    steps = nc // RET_CHUNKS_PER_STEP
    sp = _ret_specs(steps, True)
    return pl.pallas_call(
        body,
        out_shape=jax.ShapeDtypeStruct((s, 4 * RET_W), BF16),
        grid=(steps,),
        in_specs=[sp["proj"], sp["half"], sp["half"], sp["intra"], sp["dec"], sp["dec"], sp["cd"],
                  sp["state"], sp["wide"]],
        out_specs=sp["proj"],
        scratch_shapes=[pltpu.VMEM((RET_HEADS, RET_DK, RET_DK), F32)],
        compiler_params=_params("arbitrary"), name=name)(p_r, cos, sin, intra_t, qd_t, kd_t, cd_t, states, dy_r)


CONV_SLAB = 256
_CONV_SLABS = [slice(i * CONV_SLAB, (i + 1) * CONV_SLAB) for i in range(SSD_CD // CONV_SLAB)]


def _conv_taps(ext, w):
    acc = w[SSD_CONV - 1:SSD_CONV] * ext
    for j in range(SSD_CONV - 1):
        acc = acc + w[j:j + 1] * pltpu.roll(ext, SSD_CONV - 1 - j, axis=0)
    return acc


def _conv_fwd(xbc_raw, conv_w, conv_b, name):
    s = xbc_raw.shape[0]
    t8 = CONV_TILE // 8

    def body(cur_ref, prev_ref, w_ref, b_ref, o_ref, slope_ref):
        keep = (pl.program_id(0) > 0).astype(F32)
        for sl in _CONV_SLABS:
            ext = jnp.concatenate([prev_ref[:, sl] * keep, cur_ref[:, sl]], axis=0)
            u = _conv_taps(ext, w_ref[:, sl])[8:] + b_ref[:, sl]
            sg = _sigmoid(u)
            o_ref[:, sl] = u * sg
            slope_ref[:, sl] = _bf(sg * (1.0 + u * (1.0 - sg)))

    out = jax.ShapeDtypeStruct((s, SSD_CD), F32)
    return pl.pallas_call(
        body, out_shape=(out, jax.ShapeDtypeStruct((s, SSD_CD), BF16)), grid=(s // CONV_TILE,),
        in_specs=[_row_spec(SSD_CD, CONV_TILE),
                  pl.BlockSpec((8, SSD_CD), lambda i: (jnp.maximum(i * t8 - 1, 0), 0)),
                  _full_spec((SSD_CONV, SSD_CD)), _full_spec((1, SSD_CD))],
        out_specs=(_row_spec(SSD_CD, CONV_TILE), _row_spec(SSD_CD, CONV_TILE)),
        compiler_params=_params("parallel"), name=name)(xbc_raw, xbc_raw, conv_w, conv_b)


def _conv_bwd(xbc_raw, conv_w, slope, dact, name):
    s = xbc_raw.shape[0]
    nt = s // CONV_TILE
    t8 = CONV_TILE // 8
    rows = CONV_TILE + 8

    def body(cur_ref, w_ref, s_ref, snext_ref, d_ref, dnext_ref, dx_ref, dw_ref, db_ref):
        i = pl.program_id(0)

        @pl.when(i == 0)
        def _():
            dw_ref[...] = jnp.zeros_like(dw_ref)
            db_ref[...] = jnp.zeros_like(db_ref)

        keep_next = (i < nt - 1).astype(F32)
        for sl in _CONV_SLABS:
            w = w_ref[:, sl]
            cur = cur_ref[:, sl]
            duc = d_ref[:, sl].astype(F32) * s_ref[:, sl].astype(F32)
            after = (dnext_ref[:, sl].astype(F32) * snext_ref[:, sl].astype(F32))[:8] * keep_next
            du = jnp.concatenate([duc, after], axis=0)
            dx = w[SSD_CONV - 1:SSD_CONV] * duc
            for j in range(SSD_CONV - 1):
                ahead = pltpu.roll(du, rows - (SSD_CONV - 1 - j), axis=0)[:CONV_TILE]
                dx = dx + w[j:j + 1] * ahead
                dw_ref[j:j + 1, sl] += jnp.sum(cur * ahead, axis=0, keepdims=True)
            dx_ref[:, sl] = _bf(dx)
            dw_ref[SSD_CONV - 1:SSD_CONV, sl] += jnp.sum(cur * duc, axis=0, keepdims=True)
            db_ref[:, sl] += jnp.sum(duc, axis=0, keepdims=True)

    row = _row_spec(SSD_CD, CONV_TILE)
    nxt16 = pl.BlockSpec((16, SSD_CD), lambda i: (jnp.minimum((i + 1) * (t8 // 2), s // 16 - 1), 0))
    return pl.pallas_call(
        body,
        out_shape=(jax.ShapeDtypeStruct((s, SSD_CD), BF16), jax.ShapeDtypeStruct((SSD_CONV, SSD_CD), F32),
                   jax.ShapeDtypeStruct((1, SSD_CD), F32)),
        grid=(nt,),
        in_specs=[row, _full_spec((SSD_CONV, SSD_CD)), row, nxt16, row, nxt16],
        out_specs=(row, _full_spec((SSD_CONV, SSD_CD)), _full_spec((1, SSD_CD))),
        compiler_params=_params("arbitrary"), name=name)(xbc_raw, conv_w, slope, slope, dact, dact)


def _head_select():
    e = np.zeros((HEAD_PAD, SSD_W), np.float32)
    for h in range(SSD_HEADS):
        e[h, h * SSD_P:(h + 1) * SSD_P] = 1.0
    return jnp.asarray(e, BF16), jnp.asarray(e.T, BF16)


def _pad_heads(v):
    return jnp.pad(v.reshape(1, SSD_HEADS).astype(F32), ((0, 0), (0, HEAD_PAD - SSD_HEADS)))


def _ssd_masks():
    r = _iota((CHUNK, SSD_QW), 0)
    c = _iota((CHUNK, SSD_QW), 1) % SSD_P
    itile = (r == c).astype(F32)
    ctile = r >= c
    rb = _iota((SSD_QW, SSD_QW), 0) // SSD_P
    cb = _iota((SSD_QW, SSD_QW), 1) // SSD_P
    return itile, ctile, rb == cb


_TILES = [slice(i * SSD_QW, (i + 1) * SSD_QW) for i in range(SSD_W // SSD_QW)]


def _ssd_heads(dtr_ref, dtb_ref, alog_ref):
    u = dtr_ref[...] + dtb_ref[...]
    dt = _softplus(u)
    nexp = -jnp.exp(alog_ref[...])
    return u, dt, nexp, dt * nexp


def _ssd_group(xbc_ref, acol_ref, dte_ref, gs):
    acol = acol_ref[:, gs]
    alast = acol[CHUNK - 1:CHUNK, :]
    xs, dte = xbc_ref[:, gs], dte_ref[:, gs]
    return dict(xs=xs, dte=dte, xdt=xs * dte, ea=jnp.exp(acol), tail=jnp.exp(alast - acol), eal=jnp.exp(alast))


def _silu_gate(z_ref, y_ref, tl):
    zv = z_ref[:, tl]
    sz = _sigmoid(zv)
    return zv, sz, y_ref[:, tl] * (zv * sz)


def _tile4(x):
    return jnp.concatenate([x, x, x, x], axis=0)


def _fold4(x):
    return x[0:CHUNK] + x[CHUNK:2 * CHUNK] + x[2 * CHUNK:3 * CHUNK] + x[3 * CHUNK:4 * CHUNK]


def _ssd_specs(steps, rev):
    cidx = (lambda c: steps - 1 - c) if rev else (lambda c: c)
    return dict(
        xbc=pl.BlockSpec((STEP_ROWS, SSD_CD), lambda c: (cidx(c), 0)),
        dt=pl.BlockSpec((STEP_ROWS, HEAD_PAD), lambda c: (cidx(c), 0)),
        wide=pl.BlockSpec((STEP_ROWS, SSD_W), lambda c: (cidx(c), 0)),
        state=pl.BlockSpec((CHUNKS_PER_STEP, SSD_N, SSD_W), lambda c: (cidx(c), 0, 0)),
        head=_full_spec((1, HEAD_PAD)),
        roww=_full_spec((1, SSD_W)),
        e=_full_spec((HEAD_PAD, SSD_W)),
        et=_full_spec((SSD_W, HEAD_PAD)),
    )


_GROUPS = range(SSD_GROUPS)
_GROUP_LANES = [slice(g * SSD_GW, (g + 1) * SSD_GW) for g in _GROUPS]
_QUADS = [(g, slice(g * SSD_GW + q * SSD_QW, g * SSD_GW + (q + 1) * SSD_QW), slice(q * SSD_QW, (q + 1) * SSD_QW))
          for g in _GROUPS for q in range(SSD_GW // SSD_QW)]


def _ssd_bc(xbc_ref):
    bg = [_bf(xbc_ref[:, B_OFF + g * SSD_N:B_OFF + (g + 1) * SSD_N]) for g in _GROUPS]
    cg = [_bf(xbc_ref[:, C_OFF + g * SSD_N:C_OFF + (g + 1) * SSD_N]) for g in _GROUPS]
    return bg, cg


def _ssd_decay(aq, itile, ctile):
    arow = jnp.sum(aq * itile, axis=0, keepdims=True)
    return jnp.exp(jnp.where(ctile, aq - arow, -jnp.inf))


def _ssd_blockdiag(xq, bdmask):
    return jnp.where(bdmask, _tile4(_bf(xq)), jnp.zeros((), BF16))


def _ssd_fwd(xbc, dt_raw, z, dt_bias, a_log, dske, norm_w, name):
    s = xbc.shape[0]
    nc = s // CHUNK
    e_sel, _ = _head_select()

    def body(xbc_ref, dtr_ref, z_ref, dtb_ref, alog_ref, dske_ref, nw_ref, e_ref,
             yraw_ref, ys_ref, st_ref, acol_ref, dte_ref, state):
        @pl.when(pl.program_id(0) == 0)
        def _():
            state[...] = jnp.zeros_like(state)

        for sub in range(CHUNKS_PER_STEP):
            rows = _chunk_rows(sub)
            chunk(xbc_ref.at[rows], dtr_ref.at[rows], z_ref.at[rows], dtb_ref, alog_ref, dske_ref, nw_ref, e_ref,
                  yraw_ref.at[rows], ys_ref.at[rows], st_ref.at[sub], acol_ref.at[rows], dte_ref.at[rows], state)

    def chunk(xbc_ref, dtr_ref, z_ref, dtb_ref, alog_ref, dske_ref, nw_ref, e_ref,
              yraw_ref, ys_ref, st_ref, acol_ref, dte_ref, state):
        _, dt, _, a = _ssd_heads(dtr_ref, dtb_ref, alog_ref)
        tril = _bf((_iota((CHUNK, CHUNK), 0) >= _iota((CHUNK, CHUNK), 1)).astype(F32))
        ac3, dt3 = _split3(_dot_exact_r(tril, a)), _split3(dt)
        for tl in _TILES:
            e_t = e_ref[:, tl]
            acol_ref[:, tl] = _dot(ac3[0], e_t) + _dot(ac3[1], e_t) + _dot(ac3[2], e_t)
            dte_ref[:, tl] = _dot(dt3[0], e_t) + _dot(dt3[1], e_t)

        itile, ctile, bdmask = _ssd_masks()
        st_ref[...] = state[...]
        bg, cg = _ssd_bc(xbc_ref)
        for g0 in range(0, SSD_GROUPS, SSD_BLOCK_GROUPS):
            gids = range(g0, g0 + SSD_BLOCK_GROUPS)
            quads = [(g, ql, qs) for g, ql, qs in _QUADS if g in gids]
            q = {g: _ssd_group(xbc_ref, acol_ref, dte_ref, _GROUP_LANES[g]) for g in gids}
            stg = {g: state[:, _GROUP_LANES[g]] for g in gids}
            cbt = {g: _dot_nt(cg[g], _tile4(bg[g])) for g in gids}
            ys = {g: _dot(cg[g], _bf(stg[g])) for g in gids}
            dq = [_ssd_decay(acol_ref[:, ql], itile, ctile) for _, ql, _ in quads]
            xbd = [_ssd_blockdiag(q[g]["xdt"][:, qs], bdmask) for g, _, qs in quads]
            yq = [_dot(_bf(cbt[g] * dq[i]), xbd[i]) + ys[g][:, qs] * q[g]["ea"][:, qs]
                  + dske_ref[:, ql] * q[g]["xs"][:, qs] for i, (g, ql, qs) in enumerate(quads)]
            new_st = {g: stg[g] * q[g]["eal"] + _dot_tn(bg[g], _bf(q[g]["xdt"] * q[g]["tail"])) for g in gids}
            for i, (_, ql, _) in enumerate(quads):
                yraw_ref[:, ql] = yq[i]
            for g in gids:
                state[:, _GROUP_LANES[g]] = new_st[g]

        sq = jnp.zeros((CHUNK, SSD_QW), F32)
        for tl in _TILES:
            t = _silu_gate(z_ref, yraw_ref, tl)[2]
            sq = sq + t * t
        rstd = lax.rsqrt(jnp.sum(sq, axis=1, keepdims=True) * (1.0 / SSD_W) + EPS)
        for tl in _TILES:
            ys_ref[:, tl] = _bf(_silu_gate(z_ref, yraw_ref, tl)[2] * rstd * nw_ref[:, tl])

    steps = nc // CHUNKS_PER_STEP
    sp = _ssd_specs(steps, False)
    wide = jax.ShapeDtypeStruct((s, SSD_W), F32)
    return pl.pallas_call(
        body,
        out_shape=(wide, jax.ShapeDtypeStruct((s, SSD_W), BF16), jax.ShapeDtypeStruct((nc, SSD_N, SSD_W), F32),
                   wide, wide),
        grid=(steps,),
        in_specs=[sp["xbc"], sp["dt"], sp["wide"], sp["head"], sp["head"], sp["roww"], sp["roww"], sp["e"]],
        out_specs=(sp["wide"], sp["wide"], sp["state"], sp["wide"], sp["wide"]),
        scratch_shapes=[pltpu.VMEM((SSD_N, SSD_W), F32)],
        compiler_params=_params("arbitrary"), name=name,
    )(xbc, dt_raw, z, dt_bias, a_log, dske, norm_w, e_sel)


def _ssd_bwd(xbc, dt_raw, z, yraw, states, dys, acol, dte, dt_bias, a_log, dske, norm_w, name, rider):
    s = xbc.shape[0]
    nc = s // CHUNK
    steps = nc // CHUNKS_PER_STEP
    _, et_sel = _head_select()
    n_in, n_out, n_scratch = 13, 7, 5
    n_rin, n_rout = len(rider.inputs), len(rider.out_shapes)

    def body(*refs):
        ins, refs = refs[:n_in], refs[n_in:]
        r_ins, refs = refs[:n_rin], refs[n_rin:]
        outs, refs = refs[:n_out], refs[n_out:]
        r_outs, refs = refs[:n_rout], refs[n_rout:]
        ride = functools.partial(rider.run, pl.program_id(0), steps, r_ins, r_outs, refs[n_scratch],
                                 refs[n_scratch + 1])
        ride(last=False)
        compute(*ins, *outs, *refs[:n_scratch])
        ride(last=True)

    def compute(xbc_ref, dtr_ref, z_ref, yraw_ref, st_ref, dys_ref, acol_ref, dte_ref, dtb_ref, alog_ref,
                dske_ref, nw_ref, et_ref, dxbc_ref, dz_ref, ddtr_ref, ddtb_ref, dalog_ref, ddsk_ref, dnw_ref,
                dstate, dsk_acc, dy_s, dacol_s, ddte_s):
        @pl.when(pl.program_id(0) == 0)
        def _():
            dstate[...] = jnp.zeros_like(dstate)
            dsk_acc[...] = jnp.zeros_like(dsk_acc)
            ddtb_ref[...] = jnp.zeros_like(ddtb_ref)
            dalog_ref[...] = jnp.zeros_like(dalog_ref)
            dnw_ref[...] = jnp.zeros_like(dnw_ref)

        for sub in reversed(range(CHUNKS_PER_STEP)):
            rows = _chunk_rows(sub)
            chunk(xbc_ref.at[rows], dtr_ref.at[rows], z_ref.at[rows], yraw_ref.at[rows], st_ref.at[sub],
                  dys_ref.at[rows], acol_ref.at[rows], dte_ref.at[rows], dtb_ref, alog_ref, dske_ref, nw_ref, et_ref,
                  dxbc_ref.at[rows], dz_ref.at[rows], ddtr_ref.at[rows], ddtb_ref, dalog_ref, dnw_ref,
                  dstate, dsk_acc, dy_s, dacol_s, ddte_s)

        @pl.when(pl.program_id(0) == steps - 1)
        def _():
            ddsk_ref[...] = _dot_exact_l(jnp.broadcast_to(dsk_acc[...], (8, SSD_W)), et_ref[...])[0:1]

    def chunk(xbc_ref, dtr_ref, z_ref, yraw_ref, st_ref, dys_ref, acol_ref, dte_ref, dtb_ref, alog_ref,
              dske_ref, nw_ref, et_ref, dxbc_ref, dz_ref, ddtr_ref, ddtb_ref, dalog_ref, dnw_ref,
              dstate, dsk_acc, dy_s, dacol_s, ddte_s):
        itile, ctile, bdmask = _ssd_masks()
        last_row = (_iota((CHUNK, 1), 0) == CHUNK - 1).astype(F32)

        s1 = jnp.zeros((CHUNK, SSD_QW), F32)
        s2 = jnp.zeros((CHUNK, SSD_QW), F32)
        for tl in _TILES:
            t = _silu_gate(z_ref, yraw_ref, tl)[2]
            s1 = s1 + t * t
            s2 = s2 + dys_ref[:, tl] * nw_ref[:, tl] * t
        rstd = lax.rsqrt(jnp.sum(s1, axis=1, keepdims=True) * (1.0 / SSD_W) + EPS)
        back = rstd * rstd * rstd * jnp.sum(s2, axis=1, keepdims=True) * (1.0 / SSD_W)
        for tl in _TILES:
            zv, sz, t = _silu_gate(z_ref, yraw_ref, tl)
            dysv = dys_ref[:, tl]
            dt_ = rstd * (dysv * nw_ref[:, tl]) - t * back
            dnw_ref[:, tl] += jnp.sum(dysv * t * rstd, axis=0, keepdims=True)
            dz_ref[:, tl] = _bf(dt_ * yraw_ref[:, tl] * (sz * (1.0 + zv * (1.0 - sz))))
            dy_t = dt_ * (zv * sz)
            dy_s[:, tl] = dy_t
            dsk_acc[:, tl] += jnp.sum(dy_t * xbc_ref[:, tl], axis=0, keepdims=True)

        gl = _GROUP_LANES
        bg, cg = _ssd_bc(xbc_ref)
        for g0 in range(0, SSD_GROUPS, SSD_BLOCK_GROUPS):
            gids = range(g0, g0 + SSD_BLOCK_GROUPS)
            quads = [(i, g, ql, qs) for i, (g, ql, qs) in enumerate(_QUADS) if g in gids]
            q = {g: _ssd_group(xbc_ref, acol_ref, dte_ref, gl[g]) for g in gids}
            stg = {g: st_ref[:, gl[g]] for g in gids}
            dsn = {g: dstate[:, gl[g]] for g in gids}
            stgb, dsnb = {g: _bf(stg[g]) for g in gids}, {g: _bf(dsn[g]) for g in gids}
            btile = {g: _tile4(bg[g]) for g in gids}
            cbt = {g: _dot_nt(cg[g], btile[g]) for g in gids}
            dyg = {g: dy_s[:, gl[g]] for g in gids}
            eag, tailg = {g: q[g]["ea"] for g in gids}, {g: q[g]["tail"] for g in gids}
            xdtg, ealg = {g: q[g]["xdt"] for g in gids}, {g: q[g]["eal"] for g in gids}
            dys_g = {g: _bf(dyg[g] * eag[g]) for g in gids}
            ysg = {g: _dot(cg[g], stgb[g]) for g in gids}
            dc0 = {g: _dot_nt(dys_g[g], stgb[g]) for g in gids}
            dst = {g: _dot_tn(cg[g], dys_g[g]) for g in gids}
            dwt = {g: _dot(bg[g], dsnb[g]) for g in gids}
            db0 = {g: _dot_nt(_bf(xdtg[g] * tailg[g]), dsnb[g]) for g in gids}
            dtl = {g: dwt[g] * xdtg[g] * tailg[g] for g in gids}
            dal_row = {g: jnp.sum(dtl[g], axis=0, keepdims=True)
                       + jnp.sum(dsn[g] * stg[g], axis=0, keepdims=True) * ealg[g] for g in gids}
            for g in gids:
                dstate[:, gl[g]] = dst[g] + dsn[g] * ealg[g]
            dq = {i: _ssd_decay(acol_ref[:, ql], itile, ctile) for i, _, ql, _ in quads}
            mq = {i: cbt[g] * dq[i] for i, g, _, _ in quads}
            xbd = {i: _ssd_blockdiag(xdtg[g][:, qs], bdmask) for i, g, _, qs in quads}
            dyq = {i: _bf(dyg[g][:, qs]) for i, g, _, qs in quads}
            dm = {i: _dot_nt(dyq[i], xbd[i]) for i, _, _, _ in quads}
            dxdt_q = {i: _fold4(jnp.where(bdmask, _dot_tn(_bf(mq[i]), dyq[i]), 0.0)) for i, _, _, _ in quads}
            eq = {i: dm[i] * mq[i] for i, _, _, _ in quads}
            dacol_q = {i: eq[i] - itile * jnp.sum(eq[i], axis=0, keepdims=True) for i, _, _, _ in quads}
            dcbt = {i: _bf(dm[i] * dq[i]) for i, _, _, _ in quads}
            dc_q = {i: _dot(dcbt[i], btile[g]) for i, g, _, _ in quads}
            db_q = {i: _fold4(_dot_tn(dcbt[i], cg[g])) for i, g, _, _ in quads}
            for g in gids:
                dxdt_g = jnp.concatenate([dxdt_q[2 * g], dxdt_q[2 * g + 1]], axis=1) + dwt[g] * tailg[g]
                dxbc_ref[:, gl[g]] = _bf(dyg[g] * dske_ref[:, gl[g]] + dxdt_g * q[g]["dte"])
                ddte_s[:, gl[g]] = dxdt_g * q[g]["xs"]
                dacol_s[:, gl[g]] = (jnp.concatenate([dacol_q[2 * g], dacol_q[2 * g + 1]], axis=1)
                                     + dyg[g] * (ysg[g] * eag[g]) - dtl[g] + last_row * dal_row[g])
                dxbc_ref[:, B_OFF + g * SSD_N:B_OFF + (g + 1) * SSD_N] = _bf(db0[g] + db_q[2 * g] + db_q[2 * g + 1])
                dxbc_ref[:, C_OFF + g * SSD_N:C_OFF + (g + 1) * SSD_N] = _bf(dc0[g] + dc_q[2 * g] + dc_q[2 * g + 1])

        dacum = jnp.zeros((CHUNK, HEAD_PAD), F32)
        ddt = jnp.zeros((CHUNK, HEAD_PAD), F32)
        for i, tl in enumerate(_TILES):
            et_t = et_ref[i * SSD_QW:(i + 1) * SSD_QW, :]
            dacum = dacum + _dot_exact_l(dacol_s[:, tl], et_t, pieces=2)
            ddt = ddt + _dot_exact_l(ddte_s[:, tl], et_t, pieces=2)
        u, _, nexp, a = _ssd_heads(dtr_ref, dtb_ref, alog_ref)
        triu = _bf((_iota((CHUNK, CHUNK), 1) >= _iota((CHUNK, CHUNK), 0)).astype(F32))
        da = _dot_exact_r(triu, dacum)
        ddt = ddt + da * nexp
        dalog_ref[...] += jnp.sum(da * a, axis=0, keepdims=True)
        du = ddt * _sigmoid(u)
        ddtr_ref[...] = _bf(du)
        ddtb_ref[...] += jnp.sum(du, axis=0, keepdims=True)

    sp = _ssd_specs(steps, True)
    return pl.pallas_call(
        body,
        out_shape=(jax.ShapeDtypeStruct((s, SSD_CD), BF16), jax.ShapeDtypeStruct((s, SSD_W), BF16),
                   jax.ShapeDtypeStruct((s, HEAD_PAD), BF16), jax.ShapeDtypeStruct((1, HEAD_PAD), F32),
                   jax.ShapeDtypeStruct((1, HEAD_PAD), F32), jax.ShapeDtypeStruct((1, HEAD_PAD), F32),
                   jax.ShapeDtypeStruct((1, SSD_W), F32), *rider.out_shapes),
        grid=(steps,),
        in_specs=[sp["xbc"], sp["dt"], sp["wide"], sp["wide"], sp["state"], sp["wide"], sp["wide"], sp["wide"],
                  sp["head"], sp["head"], sp["roww"], sp["roww"], sp["et"], *rider.in_specs()],
        out_specs=(sp["xbc"], sp["wide"], sp["dt"], sp["head"], sp["head"], sp["head"], sp["roww"],
                   *rider.out_specs()),
        scratch_shapes=[pltpu.VMEM((SSD_N, SSD_W), F32), pltpu.VMEM((1, SSD_W), F32),
                        *[pltpu.VMEM((CHUNK, SSD_W), F32)] * 3, *rider.scratch()],
        compiler_params=_params("arbitrary"), name=name,
    )(xbc, dt_raw, z, yraw, states, dys, acol, dte, dt_bias, a_log, dske, norm_w, et_sel, *rider.inputs)


W_IN_SHARD = IN_PROJ // N_DEV
W_IN_ROW_PARTS = ((0, 1152), (1152, 832), (1984, 840))


def _device_step(x, pos_col, target, norm1_w, conv_w_shard, conv_b, dt_bias, a_log, d_skip, ssd_norm_w, norm_f_w,
                 w_in_shard, br_ret_shard, br_ssd_shard, out_shard):
    inv_freq = jnp.asarray(ROPE_THETA ** (-np.arange(ROPE_HALF, dtype=np.float64) / ROPE_HALF), F32).reshape(1, ROPE_HALF)
    dtb, alog = _pad_heads(dt_bias), _pad_heads(a_log)
    dske = jnp.repeat(d_skip.reshape(SSD_HEADS).astype(F32), SSD_P).reshape(1, SSD_W)
    my_core = lax.axis_index("c")

    h, cos, sin, br_ret_shard, br_ssd_shard, out_shard, w_all = _rmsnorm_rope_fwd(
        x, norm1_w, pos_col, inv_freq, [br_ret_shard, br_ssd_shard, out_shard], "rmsnorm1_fwd",
        _gather_rider([w_in_shard], 1.0, 1.0))
    w_all = w_all.reshape(IN_PROJ, D_MODEL)
    w_dt = jnp.pad(w_all[OFF_DT:OFF_G], ((0, HEAD_PAD - SSD_HEADS), (0, 0)))
    w_g = w_all[OFF_G:]
    rows_r, rows_z, rows_xbc = (0, OFF_Z), (OFF_Z, SSD_W), (OFF_XBC, SSD_CD)
    gather = _gather_rider([br_ret_shard, br_ssd_shard, out_shard, conv_w_shard], relay_at=0.35, pass_at=0.6)
    p_r, all_ret, all_ssd, all_out, all_conv = _matmul(h, w_all, "nt", F32, "proj_ret", rider=gather, b_rows=rows_r)
    w_br_ret = all_ret.reshape(RET_W, D_MODEL)
    w_br_ssd = all_ssd.reshape(SSD_W, D_MODEL)
    w_out = all_out.reshape(D_MODEL, D_MODEL)
    conv_w = all_conv.transpose(1, 0, 2).reshape(SSD_CONV, SSD_CD)
    p_z = _matmul(h, w_all, "nt", F32, "proj_z", b_rows=rows_z)
    p_xbc = _matmul(h, w_all, "nt", F32, "proj_xbc", b_rows=rows_xbc)
    p_dt = _matmul(h, w_dt, "nt", F32, "proj_dt")
    p_g = _matmul(h, w_g, "nt", BF16, "proj_gates")
    y_r, ret_states = _retention_fwd(p_r, cos, sin, "retention_fwd")
    xbc_act, silu_slope = _conv_fwd(p_xbc, conv_w, conv_b, "conv_fwd")
    y_raw, y_s, ssd_states, acol, dte = _ssd_fwd(xbc_act, p_dt, p_z, dtb, alog, dske, ssd_norm_w, "ssd_fwd")
    pr = _matmul(y_r, w_br_ret, "nn", BF16, "branch_ret")
    ps = _matmul(y_s, w_br_ssd, "nn", BF16, "branch_ssd")
    merged = _merge_fwd(p_g, pr, ps, "merge_fwd")
    o = _matmul(merged, w_out, "nn", F32, "out_proj")
    dx2, g_norm_f, loss_acc = _final_fwd_bwd(x, o, norm_f_w, target, "final_norm_loss")

    g_w_out = _matmul(merged, dx2, "tn", BF16, "grad_w_out")
    dmerged = _matmul(dx2, w_out, "nt", F32, "d_merged")
    dpr, dps, dp_g = _merge_bwd(p_g, pr, ps, dmerged, "merge_bwd")
    g_w_br_ret = _matmul(y_r, dpr, "tn", BF16, "grad_w_br_ret")
    g_w_br_ssd = _matmul(y_s, dps, "tn", BF16, "grad_w_br_ssd")
    dy_r = _matmul(dpr, w_br_ret, "nt", F32, "d_y_ret")
    dy_s = _matmul(dps, w_br_ssd, "nt", F32, "d_y_ssd")
    scatter = _scatter_rider([g_w_out.reshape(N_DEV, -1, D_MODEL), g_w_br_ret.reshape(N_DEV, -1, D_MODEL),
                              g_w_br_ssd.reshape(N_DEV, -1, D_MODEL)])
    dxbc_act, dp_z, dp_dt, g_dtb, g_alog, g_dsk, g_ssd_norm, got_out, got_ret, got_ssd = _ssd_bwd(
        xbc_act, p_dt, p_z, y_raw, ssd_states, dy_s, acol, dte, dtb, alog, dske, ssd_norm_w, "ssd_bwd", scatter)
    dp_xbc, g_conv_w, g_conv_b = _conv_bwd(p_xbc, conv_w, silu_slope, dxbc_act, "conv_bwd")
    dp_r = _retention_bwd(p_r, cos, sin, ret_states, dy_r, "retention_bwd")
    g_w_in = jnp.concatenate([
        _matmul(dp_r, h, "tn", BF16, "grad_w_ret"),
        _matmul(dp_z, h, "tn", BF16, "grad_w_z"),
        _matmul(dp_xbc, h, "tn", BF16, "grad_w_xbc"),
        _matmul(dp_dt, h, "tn", BF16, "grad_w_dt")[:SSD_HEADS],
        _matmul(dp_g, h, "tn", BF16, "grad_w_gates"),
    ], axis=0)
    blocks = g_w_in.reshape(N_DEV, W_IN_SHARD, D_MODEL)
    dh, from_sibling = _matmul(dp_r, w_all, "nn", F32, "d_h_ret", rider=_sibling_rider(blocks), b_rows=rows_r)
    chip_sum = _chip_sum(blocks, from_sibling, my_core, "w_in_chip_sum")
    carriers = (("d_h_xbc", dp_xbc, w_all, rows_xbc), ("d_h_gates", dp_g, w_g, None), ("d_h_z", dp_z, w_all, rows_z))
    landed = []
    for (row0, rows), (nm, dp, w, b_rows) in zip(W_IN_ROW_PARTS, carriers):
        dt_pair = (dp_dt, w_dt) if nm == "d_h_z" else None
        dh, got = _matmul(dp, w, "nn", F32, nm, add=dh, rider=_chip_rider(chip_sum, row0, rows), b_rows=b_rows,
                          extra=dt_pair)
        landed.append(got)
    grad_x, g_norm1 = _rmsnorm_bwd(x, norm1_w, dh, dx2, "rmsnorm1_bwd")
    small = dict(norm1_w=g_norm1, conv_w=g_conv_w, conv_b=g_conv_b, dt_bias=g_dtb[:, :SSD_HEADS],
                 a_log=g_alog[:, :SSD_HEADS], d_skip=g_dsk[:, :SSD_HEADS], ssd_norm_w=g_ssd_norm,
                 norm_f_w=g_norm_f)
    big = dict(w_in=landed, w_br_ret=_sum_slots(got_ret, "w_br_ret_sum"),
               w_br_ssd=_sum_slots(got_ssd, "w_br_ssd_sum"), w_out=_sum_slots(got_out, "w_out_sum"))
    return loss_acc[0, 0], grad_x, small, big


def _all_reduce_small(vec, name):
    r, c = vec.shape

    def body(x_ref, out_ref, land, send_sems, recv_sems):
        x, y, cc = _mesh_pos()
        my_idx = 4 * x + 2 * y + cc
        land[my_idx] = x_ref[...]
        copies = []
        for k in range(1, N_DEV):
            px, py, pc = x ^ (k >> 2), y ^ ((k >> 1) & 1), cc ^ (k & 1)
            cp = pltpu.make_async_remote_copy(
                src_ref=x_ref, dst_ref=land.at[my_idx],
                send_sem=send_sems.at[k - 1], recv_sem=recv_sems.at[k - 1],
                device_id=(px, py, pc), device_id_type=MESH)
            cp.start()
            copies.append(cp)
        for k in range(1, N_DEV):
            px, py, pc = x ^ (k >> 2), y ^ ((k >> 1) & 1), cc ^ (k & 1)
            pltpu.make_async_remote_copy(
                src_ref=x_ref, dst_ref=land.at[4 * px + 2 * py + pc],
                send_sem=send_sems.at[k - 1], recv_sem=recv_sems.at[k - 1],
                device_id=(px, py, pc), device_id_type=MESH).wait_recv()
        for cp in copies:
            cp.wait_send()
        acc = land[0]
        for i in range(1, N_DEV):
            acc = acc + land[i]
        out_ref[...] = acc

    return pl.pallas_call(
        body,
        out_shape=jax.ShapeDtypeStruct((r, c), F32),
        in_specs=[pl.BlockSpec(memory_space=pltpu.VMEM)],
        out_specs=pl.BlockSpec(memory_space=pltpu.VMEM),
        scratch_shapes=[pltpu.VMEM((N_DEV, r, c), F32), pltpu.SemaphoreType.DMA((7,)),
                        pltpu.SemaphoreType.DMA((7,))],
        name=name)(vec)


_SMALL = ("norm1_w", "conv_w", "conv_b", "dt_bias", "a_log", "d_skip", "ssd_norm_w", "norm_f_w")
_SMALL_COLS = 128
_W_IN = "w_in"
_WEIGHTS = ("norm1_w", "w_in", "conv_w", "conv_b", "dt_bias", "a_log", "d_skip", "ssd_norm_w",
            "w_br_ret", "w_br_ssd", "w_out", "norm_f_w")


def _pack(parts):
    flat = jnp.concatenate([p.reshape(-1).astype(F32) for p in parts])
    rows = -(-flat.shape[0] // (8 * _SMALL_COLS)) * 8
    return jnp.pad(flat, (0, rows * _SMALL_COLS - flat.shape[0])).reshape(rows, _SMALL_COLS)


def _unpack(packed, shapes):
    flat = packed.reshape(-1)
    out, off = [], 0
    for shp in shapes:
        n = int(np.prod(shp))
        out.append(flat[off:off + n].reshape(shp))
        off += n
    return out


def kernel(x, positions, norm1_w, w_in, conv_w, conv_b, dt_bias, a_log, d_skip, ssd_norm_w, w_br_ret, w_br_ssd, w_out, norm_f_w, loss_target, m_norm1_w, m_w_in, m_conv_w, m_conv_b, m_dt_bias, m_a_log, m_d_skip, m_ssd_norm_w, m_w_br_ret, m_w_br_ssd, m_w_out, m_norm_f_w, v_norm1_w, v_w_in, v_conv_w, v_conv_b, v_dt_bias, v_a_log, v_d_skip, v_ssd_norm_w, v_w_br_ret, v_w_br_ssd, v_w_out, v_norm_f_w):
    w = dict(norm1_w=norm1_w, w_in=w_in, conv_w=conv_w, conv_b=conv_b, dt_bias=dt_bias, a_log=a_log,
             d_skip=d_skip, ssd_norm_w=ssd_norm_w, w_br_ret=w_br_ret, w_br_ssd=w_br_ssd, w_out=w_out,
             norm_f_w=norm_f_w)
    m = dict(norm1_w=m_norm1_w, w_in=m_w_in, conv_w=m_conv_w, conv_b=m_conv_b, dt_bias=m_dt_bias,
             a_log=m_a_log, d_skip=m_d_skip, ssd_norm_w=m_ssd_norm_w, w_br_ret=m_w_br_ret,
             w_br_ssd=m_w_br_ssd, w_out=m_w_out, norm_f_w=m_norm_f_w)
    v = dict(norm1_w=v_norm1_w, w_in=v_w_in, conv_w=v_conv_w, conv_b=v_conv_b, dt_bias=v_dt_bias,
             a_log=v_a_log, d_skip=v_d_skip, ssd_norm_w=v_ssd_norm_w, w_br_ret=v_w_br_ret,
             w_br_ssd=v_w_br_ssd, w_out=v_w_out, norm_f_w=v_norm_f_w)
    s = x.shape[1]
    my_idx = 4 * lax.axis_index("x") + 2 * lax.axis_index("y") + lax.axis_index("c")

    w[_W_IN], m[_W_IN], v[_W_IN] = w_in[0].T, m_w_in[0].T, v_w_in[0].T

    loss_part, grad_x, g_small, g_big = _device_step(
        x[0], positions.reshape(s, 1), loss_target[0], norm1_w, conv_w[0], conv_b, dt_bias, a_log, d_skip,
        ssd_norm_w, norm_f_w.reshape(1, D_MODEL), _cast_bf16(w[_W_IN], "cast_w_in"),
        w_br_ret[0], w_br_ssd[0], w_out[0])

    small_shapes = [(1, 1)] + [g_small[n].shape for n in _SMALL]
    summed = _unpack(_all_reduce_small(_pack([loss_part.reshape(1, 1)] + [g_small[n] for n in _SMALL]),
                                       "allreduce_small"), small_shapes)
    loss = summed[0].reshape(())
    grads = dict(zip(_SMALL, summed[1:]))
    conv_cols = SSD_CD // N_DEV
    grads["conv_w"] = lax.dynamic_slice_in_dim(grads["conv_w"], my_idx * conv_cols, conv_cols, axis=1)
    grads["norm_f_w"] = grads["norm_f_w"].reshape(D_MODEL)
    for n in ("norm1_w", "conv_w", "conv_b", "dt_bias", "a_log", "d_skip", "ssd_norm_w"):
        grads[n] = grads[n].reshape(w[n].shape)

    delta, new_m, new_v = {}, {}, {}
    for n in ("w_br_ret", "w_br_ssd", "w_out"):
        w[n], m[n], v[n] = w[n][0], m[n][0], v[n][0]
    for n in (_W_IN, "w_br_ret", "w_br_ssd", "w_out"):
        back = (lambda a: a.T[None]) if n == _W_IN else (lambda a: a[None])
        if n == _W_IN:
            g_big[n], *res = _adamw_row_parts(w[n], g_big[n], m[n], v[n], "adamw_" + n)
        else:
            res = _adamw(w[n], g_big[n], m[n], v[n], "adamw_" + n)
        grads[n] = back(g_big[n])
        delta[n], new_m[n], new_v[n] = (back(a) for a in res)
    shapes = [w[n].shape for n in _SMALL]
    packed = _adamw(_pack([w[n] for n in _SMALL]), _pack([grads[n] for n in _SMALL]),
                    _pack([m[n] for n in _SMALL]), _pack([v[n] for n in _SMALL]), "adamw_small")
    for res, dst in zip(packed, (delta, new_m, new_v)):
        for n, a in zip(_SMALL, _unpack(res, shapes)):
            dst[n] = a

    return (loss, grad_x.reshape(x.shape), *[grads[n] for n in _WEIGHTS], *[delta[n] for n in _WEIGHTS],
            *[new_m[n] for n in _WEIGHTS], *[new_v[n] for n in _WEIGHTS])
```

```python
import functools

import numpy as np
import jax
import jax.numpy as jnp
from jax import lax
from jax.experimental import pallas as pl
from jax.experimental.pallas import tpu as pltpu

F32 = jnp.float32
BF16 = jnp.bfloat16

D_MODEL = 2048
CHUNK = 64
CHUNKS_PER_STEP = 2
STEP_ROWS = CHUNK * CHUNKS_PER_STEP
RET_CHUNKS_PER_STEP = 4
RET_STEP_ROWS = CHUNK * RET_CHUNKS_PER_STEP
EPS = 1e-6
N_DEV = 8

RET_HEADS = 8
RET_DK = 256
RET_W = RET_HEADS * RET_DK
ROPE_THETA = 10000.0
ROPE_HALF = RET_DK // 2

SSD_W = 4096
SSD_P = 64
SSD_HEADS = 64
SSD_GROUPS = 8
SSD_N = 128
SSD_GW = SSD_W // SSD_GROUPS
SSD_QW = 256
SSD_BLOCK_GROUPS = 2
SSD_CONV = 4
SSD_CD = SSD_W + 2 * SSD_GROUPS * SSD_N
HEAD_PAD = 128
B_OFF = SSD_W
C_OFF = SSD_W + SSD_GROUPS * SSD_N

ADAM_LR = 0.001
ADAM_B1 = 0.9
ADAM_B2 = 0.999
ADAM_EPS = 1e-08
ADAM_WD = 0.01
ADAM_STEP = 10

SPLITS = (RET_W, RET_W, RET_W, RET_W, SSD_W, SSD_CD, SSD_HEADS, D_MODEL, D_MODEL)
IN_PROJ = sum(SPLITS)
OFF_Z = 4 * RET_W
OFF_XBC = OFF_Z + SSD_W
OFF_DT = OFF_XBC + SSD_CD
OFF_G = OFF_DT + SSD_HEADS

ROW_TILE = 256
CONV_TILE = 128
MM_TILE = 1024
MM_TILE_WIDE = 2048
MM_TK = 2048
VMEM_CAP = 60 << 20

MESH = pl.DeviceIdType.MESH


def _dot(a, b):
    return lax.dot_general(a, b, (((1,), (0,)), ((), ())), preferred_element_type=F32)


def _dot_nt(a, b):
    return lax.dot_general(a, b, (((1,), (1,)), ((), ())), preferred_element_type=F32)


def _dot_tn(a, b):
    return lax.dot_general(a, b, (((0,), (0,)), ((), ())), preferred_element_type=F32)


def _bf(x):
    return x.astype(BF16)


def _split3(x):
    hi = x.astype(BF16)
    r = x - hi.astype(F32)
    mid = r.astype(BF16)
    lo = (r - mid.astype(F32)).astype(BF16)
    return hi, mid, lo


def _dot_exact_l(x, sel, pieces=3):
    hi, mid, lo = _split3(x)
    r = _dot(hi, sel) + _dot(mid, sel)
    return r + _dot(lo, sel) if pieces == 3 else r


def _dot_exact_r(sel, x):
    hi, mid, lo = _split3(x)
    return _dot(sel, hi) + _dot(sel, mid) + _dot(sel, lo)


def _sigmoid(x):
    return 1.0 / (1.0 + jnp.exp(-x))


def _softplus(x):
    return jnp.maximum(x, 0.0) + jnp.log(1.0 + jnp.exp(-jnp.abs(x)))


def _iota(shape, axis):
    return lax.broadcasted_iota(jnp.int32, shape, axis)


def _params(*sem):
    return pltpu.CompilerParams(dimension_semantics=sem)


class _Rider:
    def __init__(self, inputs, out_shapes, n_sems, phases):
        self.inputs, self.out_shapes, self.n_sems, self.phases = tuple(inputs), tuple(out_shapes), n_sems, phases

    def in_specs(self):
        return [pl.BlockSpec(memory_space=pl.ANY)] * len(self.inputs)

    def out_specs(self):
        return [pl.BlockSpec(memory_space=pl.ANY)] * len(self.out_shapes)

    def scratch(self):
        return [pltpu.SemaphoreType.DMA((self.n_sems,)), pltpu.SemaphoreType.DMA((self.n_sems,))]

    def run(self, step, n_steps, ins, outs, send_sems, recv_sems, last):
        for frac, fn in self.phases:
            if (frac >= 1.0) != last:
                continue
            at = min(int(frac * n_steps), n_steps - 1)

            @pl.when(step == at)
            def _(fn=fn):
                fn(ins, outs, send_sems, recv_sems)


def _mesh_pos():
    return lax.axis_index("x"), lax.axis_index("y"), lax.axis_index("c")


def _gather_rider(shards, relay_at, pass_at):
    n = len(shards)

    def tools(a, ins, outs, send_sems, recv_sems):
        x, y, cc = _mesh_pos()
        nbrs = [(1 - x, y), (x, 1 - y)]
        diag = (1 - x, 1 - y)
        relay_from, relay_to = (x ^ cc, y ^ (1 - cc)), (x ^ (1 - cc), y ^ cc)

        def slot(px, py, pc):
            return outs[a].at[4 * px + 2 * py + pc]

        def copy(k, block, to, src=None):
            return pltpu.make_async_remote_copy(
                src_ref=slot(*block) if src is None else src, dst_ref=slot(*block),
                send_sem=send_sems.at[8 * a + k], recv_sem=recv_sems.at[8 * a + k], device_id=to, device_id_type=MESH)

        me, sibling = (x, y, cc), (x, y, 1 - cc)
        return dict(
            mine=lambda: pltpu.make_async_copy(ins[a], slot(*me), send_sems.at[8 * a + 7]),
            first=lambda: [copy(0, me, sibling, src=ins[a])] + [copy(1 + j, me, (*chip, cc), src=ins[a])
                                                                for j, chip in enumerate(nbrs)],
            landed=lambda: [copy(1 + j, (*chip, cc), me) for j, chip in enumerate(nbrs)],
            relay=lambda: copy(3, (*relay_from, cc), (*relay_to, cc)),
            relayed_in=lambda: copy(3, (*diag, cc), me),
            passed=lambda: [copy(4 + j, (*chip, cc), sibling) for j, chip in enumerate(nbrs)],
            passed_diag=lambda: copy(6, (*diag, cc), sibling),
            from_sibling=lambda: [copy(0, sibling, me)] + [copy(4 + j, (*chip, 1 - cc), me)
                                                           for j, chip in enumerate(nbrs + [diag])])

    def start(*refs):
        for a in range(n):
            t = tools(a, *refs)
            t["mine"]().start()
            for cp in t["first"]():
                cp.start()

    def relay(*refs):
        for a in range(n):
            t = tools(a, *refs)
            for got in t["landed"]():
                got.wait_recv()
            t["relay"]().start()
            for cp in t["passed"]():
                cp.start()

    def pass_diag(*refs):
        for a in range(n):
            t = tools(a, *refs)
            t["relayed_in"]().wait_recv()
            t["passed_diag"]().start()

    def finish(*refs):
        for a in range(n):
            t = tools(a, *refs)
            for cp in t["from_sibling"]():
                cp.wait_recv()
            for cp in t["first"]() + [t["relay"]()] + t["passed"]() + [t["passed_diag"]()]:
                cp.wait_send()
            t["mine"]().wait()

    outs = [jax.ShapeDtypeStruct((N_DEV,) + s.shape, s.dtype) for s in shards]
    return _Rider(shards, outs, 8 * n, ((0.0, start), (relay_at, relay), (pass_at, pass_diag), (1.0, finish)))


def _scatter_rider(blocks):
    n = len(blocks)

    def copies(a, ins, outs, send_sems, recv_sems, with_back):
        x, y, cc = _mesh_pos()
        my_idx = 4 * x + 2 * y + cc
        mine = pltpu.make_async_copy(ins[a].at[my_idx], outs[a].at[my_idx], send_sems.at[8 * a + 7])
        out, back = [], []
        for k in range(1, N_DEV):
            px, py, pc = x ^ (k >> 2), y ^ ((k >> 1) & 1), cc ^ (k & 1)
            peer_idx = 4 * px + 2 * py + pc
            sems = dict(send_sem=send_sems.at[8 * a + k - 1], recv_sem=recv_sems.at[8 * a + k - 1],
                        device_id=(px, py, pc), device_id_type=MESH)
            out.append(pltpu.make_async_remote_copy(src_ref=ins[a].at[peer_idx], dst_ref=outs[a].at[my_idx], **sems))
            if with_back:
                back.append(pltpu.make_async_remote_copy(src_ref=ins[a].at[my_idx], dst_ref=outs[a].at[peer_idx], **sems))
        return mine, out, back

    def start(*refs):
        for a in range(n):
            mine, out, _ = copies(a, *refs, False)
            mine.start()
            for cp in out:
                cp.start()

    def finish(*refs):
        for a in range(n):
            mine, out, back = copies(a, *refs, True)
            for cp in back:
                cp.wait_recv()
            for cp in out:
                cp.wait_send()
            mine.wait()

    outs = [jax.ShapeDtypeStruct(b.shape, b.dtype) for b in blocks]
    return _Rider(blocks, outs, 8 * n, ((0.0, start), (1.0, finish)))


def _sibling_rider(blocks):
    _, r, c = blocks.shape

    def copies(ins, outs, send_sems, recv_sems):
        x, y, cc = _mesh_pos()
        return [pltpu.make_async_remote_copy(
            src_ref=ins[0].at[2 * j + 1 - cc], dst_ref=outs[0].at[j], send_sem=send_sems.at[j],
            recv_sem=recv_sems.at[j], device_id=(x, y, 1 - cc), device_id_type=MESH) for j in range(4)]

    def start(*refs):
        for cp in copies(*refs):
            cp.start()

    def finish(*refs):
        for cp in copies(*refs):
            cp.wait_recv()
        for cp in copies(*refs):
            cp.wait_send()

    return _Rider([blocks], [jax.ShapeDtypeStruct((4, r, c), blocks.dtype)], 4, ((0.0, start), (1.0, finish)))


def _chip_rider(partial, row0, rows):
    _, _, c = partial.shape

    def copies(ins, outs, send_sems, recv_sems, with_back):
        x, y, cc = _mesh_pos()
        my_chip = 2 * x + y

        def rows_of(j):
            return ins[0].at[j, pl.ds(row0, rows)]

        mine = pltpu.make_async_copy(rows_of(my_chip), outs[0].at[my_chip], send_sems.at[3])
        out, back = [], []
        for k in range(1, 4):
            px, py = x ^ (k >> 1), y ^ (k & 1)
            peer = 2 * px + py
            sems = dict(send_sem=send_sems.at[k - 1], recv_sem=recv_sems.at[k - 1], device_id=(px, py, cc),
                        device_id_type=MESH)
            out.append(pltpu.make_async_remote_copy(src_ref=rows_of(peer), dst_ref=outs[0].at[my_chip], **sems))
            if with_back:
                back.append(pltpu.make_async_remote_copy(src_ref=rows_of(my_chip), dst_ref=outs[0].at[peer], **sems))
        return mine, out, back

    def start(*refs):
        mine, out, _ = copies(*refs, False)
        mine.start()
        for cp in out:
            cp.start()

    def finish(*refs):
        mine, out, back = copies(*refs, True)
        for cp in back:
            cp.wait_recv()
        for cp in out:
            cp.wait_send()
        mine.wait()

    return _Rider([partial], [jax.ShapeDtypeStruct((4, rows, c), partial.dtype)], 4, ((0.0, start), (1.0, finish)))


def _matmul(a, b, mode, out_dtype, name, add=None, rider=None, b_rows=None, extra=None):
    b_shape = b.shape if b_rows is None else (b_rows[1], b.shape[1])
    if mode == "nn":
        (m, k), (k2, n) = a.shape, b_shape
    elif mode == "nt":
        (m, k), (n, k2) = a.shape, b_shape
    else:
        (k, m), (k2, n) = a.shape, b_shape
    assert k == k2, (a.shape, b_shape, mode)
    tm, tn, tk = min(m, MM_TILE), min(n, MM_TILE), min(k, MM_TK)
    if n % MM_TILE_WIDE == 0 and (k <= MM_TK or (add is None and jnp.dtype(out_dtype).itemsize == 2)):
        tn = MM_TILE_WIDE
    assert m % tm == 0 and n % tn == 0 and k % tk == 0, (m, n, k)
    nk = k // tk
    b_tile_rows = tn if mode == "nt" else tk
    b_off = 0 if b_rows is None else b_rows[0] // b_tile_rows
    assert b_rows is None or b_rows[0] % b_tile_rows == 0, (b_rows, b_tile_rows)
    dot = {"nn": _dot, "nt": _dot_nt, "tn": _dot_tn}[mode]

    gm, gn = m // tm, n // tn
    n_rin = len(rider.inputs) if rider else 0
    n_rout = len(rider.out_shapes) if rider else 0
    n_add = 1 if add is not None else 0

    n_extra = 2 if extra is not None else 0

    def body(*refs):
        a_ref, b_ref = refs[:2]
        add_ref = refs[2] if n_add else None
        extra_refs = refs[2 + n_add:2 + n_add + n_extra]
        first = 2 + n_add + n_extra
        r_ins = refs[first:first + n_rin]
        o_ref = refs[first + n_rin]
        r_outs = refs[first + n_rin + 1:first + n_rin + 1 + n_rout]
        scratch = refs[first + n_rin + 1 + n_rout:]
        acc_ref = scratch[0] if nk > 1 else None
        kk = pl.program_id(2)
        if rider:
            step = (pl.program_id(0) * gn + pl.program_id(1)) * nk + kk
            ride = functools.partial(rider.run, step, gm * gn * nk, r_ins, r_outs, scratch[-2], scratch[-1])
            ride(last=False)

        def finish(r):
            if extra_refs:
                r = r + _dot(_bf(extra_refs[0][...]), _bf(extra_refs[1][...]))
            if add_ref is not None:
                r = r + add_ref[...]
            o_ref[...] = r.astype(o_ref.dtype)

        if nk == 1:
            finish(dot(_bf(a_ref[...]), _bf(b_ref[...])))
        else:
            @pl.when(kk == 0)
            def _():
                acc_ref[...] = dot(_bf(a_ref[...]), _bf(b_ref[...]))

            @pl.when(jnp.logical_and(kk > 0, kk < nk - 1))
            def _():
                acc_ref[...] += dot(_bf(a_ref[...]), _bf(b_ref[...]))

            @pl.when(kk == nk - 1)
            def _():
                finish(acc_ref[...] + dot(_bf(a_ref[...]), _bf(b_ref[...])))
        if rider:
            ride(last=True)

    if mode == "nn":
        a_spec = pl.BlockSpec((tm, tk), lambda i, j, kk: (i, kk))
        b_spec = pl.BlockSpec((tk, tn), lambda i, j, kk: (kk + b_off, j))
    elif mode == "nt":
        a_spec = pl.BlockSpec((tm, tk), lambda i, j, kk: (i, kk))
        b_spec = pl.BlockSpec((tn, tk), lambda i, j, kk: (j + b_off, kk))
    else:
        a_spec = pl.BlockSpec((tk, tm), lambda i, j, kk: (kk, i))
        b_spec = pl.BlockSpec((tk, tn), lambda i, j, kk: (kk + b_off, j))
    o_spec = pl.BlockSpec((tm, tn), lambda i, j, kk: (i, j))
    in_specs = [a_spec, b_spec] + ([o_spec] if add is not None else [])
    args = (a, b) + ((add,) if add is not None else ())
    if extra is not None:
        (m2, k_extra), (k_extra2, n2) = extra[0].shape, extra[1].shape
        assert (m2, n2, k_extra) == (m, n, k_extra2), (extra[0].shape, extra[1].shape)
        in_specs += [pl.BlockSpec((tm, k_extra), lambda i, j, kk: (i, 0)),
                     pl.BlockSpec((k_extra, tn), lambda i, j, kk: (0, j))]
        args += tuple(extra)
    block_bytes = (tm * tk * a.dtype.itemsize + tk * tn * b.dtype.itemsize
                   + tm * tn * (jnp.dtype(out_dtype).itemsize + (4 if add is not None else 0)))
    vmem = min(2 * block_bytes + 2 * tm * tn * 4 + 2 * (tm + tn) * tk + (4 << 20), VMEM_CAP)
    out_shape = jax.ShapeDtypeStruct((m, n), out_dtype)
    scratch = [pltpu.VMEM((tm, tn), F32)] if nk > 1 else []
    if rider:
        in_specs = in_specs + rider.in_specs()
        args = args + rider.inputs
        out_shape = (out_shape, *rider.out_shapes)
        o_spec = (o_spec, *rider.out_specs())
        scratch = scratch + rider.scratch()
    sem = ("arbitrary",) * 3 if rider else ("parallel", "parallel", "arbitrary")
    return pl.pallas_call(
        body,
        out_shape=out_shape,
        grid=(gm, gn, nk),
        in_specs=in_specs,
        out_specs=o_spec,
        scratch_shapes=scratch,
        compiler_params=pltpu.CompilerParams(dimension_semantics=sem, vmem_limit_bytes=int(vmem)),
        name=name,
    )(*args)


def _row_spec(width, tile=ROW_TILE):
    return pl.BlockSpec((tile, width), lambda i: (i, 0))


def _full_spec(shape):
    nd = len(shape)
    return pl.BlockSpec(shape, lambda *_: (0,) * nd)


def _tiling_2d(r, c):
    if r <= ROW_TILE or r % ROW_TILE == 0:
        tr = min(r, ROW_TILE)
        return (tr, c), r // tr, (lambda i: (i, 0))
    tc = 128 if r > 4096 else 256
    assert c % tc == 0, (r, c)
    return (r, tc), c // tc, (lambda i: (0, i))


def _spec_2d(r, c):
    blk, grid, idx = _tiling_2d(r, c)
    return pl.BlockSpec(blk, idx), grid


def _cast_bf16(x, name):
    r, c = x.shape
    sp, grid = _spec_2d(r, c)

    def body(x_ref, o_ref):
        o_ref[...] = _bf(x_ref[...])

    return pl.pallas_call(
        body, out_shape=jax.ShapeDtypeStruct((r, c), BF16), grid=(grid,), in_specs=[sp], out_specs=sp,
        compiler_params=_params("parallel"), name=name)(x)


def _rmsnorm_rope_fwd(x, w, pos_col, inv_freq, to_bf16, name, rider):
    s, d = x.shape
    steps = s // ROW_TILE
    n_cast, n_rin, n_rout = len(to_bf16), len(rider.inputs), len(rider.out_shapes)

    def body(x_ref, w_ref, p_ref, f_ref, *refs):
        c_ins, refs = refs[:n_cast], refs[n_cast:]
        r_ins, refs = refs[:n_rin], refs[n_rin:]
        (o_ref, cos_ref, sin_ref), refs = refs[:3], refs[3:]
        c_outs, refs = refs[:n_cast], refs[n_cast:]
        ride = functools.partial(rider.run, pl.program_id(0), steps, r_ins, refs[:n_rout], refs[-2], refs[-1])
        ride(last=False)

        @pl.when(pl.program_id(0) == 0)
        def _():
            for c_in, c_out in zip(c_ins, c_outs):
                c_out[...] = _bf(c_in[...])

        xv = x_ref[...]
        rstd = lax.rsqrt(jnp.mean(xv * xv, axis=1, keepdims=True) + EPS)
        o_ref[...] = _bf(xv * rstd * w_ref[...])
        ang = p_ref[...].astype(F32) * f_ref[...]
        cos_ref[...] = jnp.cos(ang)
        sin_ref[...] = jnp.sin(ang)
        ride(last=True)

    table = jax.ShapeDtypeStruct((s, ROPE_HALF), F32)
    whole = [_full_spec(a.shape) for a in to_bf16]
    return pl.pallas_call(
        body,
        out_shape=(jax.ShapeDtypeStruct((s, d), BF16), table, table,
                   *[jax.ShapeDtypeStruct(a.shape, BF16) for a in to_bf16], *rider.out_shapes),
        grid=(steps,),
        in_specs=[_row_spec(d), _full_spec((1, d)), _row_spec(1), _full_spec((1, ROPE_HALF)), *whole,
                  *rider.in_specs()],
        out_specs=(_row_spec(d), _row_spec(ROPE_HALF), _row_spec(ROPE_HALF), *whole, *rider.out_specs()),
        scratch_shapes=rider.scratch(),
        compiler_params=_params("arbitrary"), name=name)(x, w, pos_col, inv_freq, *to_bf16, *rider.inputs)


def _rmsnorm_bwd(x, w, dh, dres, name):
    s, d = x.shape

    def body(x_ref, w_ref, dh_ref, dres_ref, dx_ref, dw_ref):
        @pl.when(pl.program_id(0) == 0)
        def _():
            dw_ref[...] = jnp.zeros_like(dw_ref)

        xv = x_ref[...]
        rstd = lax.rsqrt(jnp.mean(xv * xv, axis=1, keepdims=True) + EPS)
        xhat = xv * rstd
        dhv = dh_ref[...]
        dxhat = dhv * w_ref[...]
        dx = rstd * (dxhat - xhat * jnp.mean(dxhat * xhat, axis=1, keepdims=True))
        dx_ref[...] = dx + dres_ref[...]
        dw_ref[...] += jnp.sum(dhv * xhat, axis=0, keepdims=True)

    return pl.pallas_call(
        body,
        out_shape=(jax.ShapeDtypeStruct((s, d), F32), jax.ShapeDtypeStruct((1, d), F32)),
        grid=(s // ROW_TILE,),
        in_specs=[_row_spec(d), _full_spec((1, d)), _row_spec(d), _row_spec(d)],
        out_specs=(_row_spec(d), _full_spec((1, d))),
        compiler_params=_params("arbitrary"), name=name)(x, w, dh, dres)


def _merge_fwd(pg, p_r, p_s, name):
    s = pg.shape[0]

    def body(g_ref, r_ref, s_ref, o_ref):
        g = g_ref[...].astype(F32)
        o_ref[...] = _bf(_sigmoid(g[:, :D_MODEL]) * r_ref[...].astype(F32)
                         + _sigmoid(g[:, D_MODEL:]) * s_ref[...].astype(F32))

    return pl.pallas_call(
        body, out_shape=jax.ShapeDtypeStruct((s, D_MODEL), BF16), grid=(s // ROW_TILE,),
        in_specs=[_row_spec(2 * D_MODEL), _row_spec(D_MODEL), _row_spec(D_MODEL)],
        out_specs=_row_spec(D_MODEL), compiler_params=_params("parallel"), name=name)(pg, p_r, p_s)


def _merge_bwd(pg, p_r, p_s, dm, name):
    s = pg.shape[0]

    def body(g_ref, r_ref, s_ref, dm_ref, dr_ref, ds_ref, dg_ref):
        g = g_ref[...].astype(F32)
        sr, ss = _sigmoid(g[:, :D_MODEL]), _sigmoid(g[:, D_MODEL:])
        d = dm_ref[...]
        dr_ref[...] = _bf(d * sr)
        ds_ref[...] = _bf(d * ss)
        dg_ref[:, :D_MODEL] = _bf(d * r_ref[...].astype(F32) * sr * (1.0 - sr))
        dg_ref[:, D_MODEL:] = _bf(d * s_ref[...].astype(F32) * ss * (1.0 - ss))

    o = jax.ShapeDtypeStruct((s, D_MODEL), BF16)
    return pl.pallas_call(
        body, out_shape=(o, o, jax.ShapeDtypeStruct((s, 2 * D_MODEL), BF16)), grid=(s // ROW_TILE,),
        in_specs=[_row_spec(2 * D_MODEL), _row_spec(D_MODEL), _row_spec(D_MODEL), _row_spec(D_MODEL)],
        out_specs=(_row_spec(D_MODEL), _row_spec(D_MODEL), _row_spec(2 * D_MODEL)),
        compiler_params=_params("parallel"), name=name)(pg, p_r, p_s, dm)


def _final_fwd_bwd(x, o, w, target, name):
    s, d = x.shape

    def body(x_ref, o_ref, w_ref, t_ref, dx_ref, dw_ref, loss_ref, dxb_ref):
        @pl.when(pl.program_id(0) == 0)
        def _():
            dw_ref[...] = jnp.zeros_like(dw_ref)
            loss_ref[...] = jnp.zeros_like(loss_ref)

        x2 = x_ref[...] + o_ref[...]
        rstd = lax.rsqrt(jnp.mean(x2 * x2, axis=1, keepdims=True) + EPS)
        xhat = x2 * rstd
        wv = w_ref[...]
        err = xhat * wv - t_ref[...]
        loss_ref[...] += jnp.sum(jnp.sum(err * err, axis=1, keepdims=True), axis=0, keepdims=True) * (0.5 / d)
        dy = err * (1.0 / d)
        dw_ref[...] += jnp.sum(dy * xhat, axis=0, keepdims=True)
        dxhat = dy * wv
        dx = rstd * (dxhat - xhat * jnp.mean(dxhat * xhat, axis=1, keepdims=True))
        dx_ref[...] = dx
        dxb_ref[...] = _bf(dx)

    return pl.pallas_call(
        body,
        out_shape=(jax.ShapeDtypeStruct((s, d), F32), jax.ShapeDtypeStruct((1, d), F32),
                   jax.ShapeDtypeStruct((8, 128), F32), jax.ShapeDtypeStruct((s, d), BF16)),
        grid=(s // ROW_TILE,),
        in_specs=[_row_spec(d), _row_spec(d), _full_spec((1, d)), _row_spec(d)],
        out_specs=(_row_spec(d), _full_spec((1, d)), _full_spec((8, 128)), _row_spec(d)),
        compiler_params=_params("arbitrary"), name=name)(x, o, w, target)


def _adamw(w, g, m, v, name):
    r, c = w.shape
    sp, grid = _spec_2d(r, c)
    c1 = 1.0 / (1.0 - ADAM_B1 ** ADAM_STEP)
    c2 = 1.0 / (1.0 - ADAM_B2 ** ADAM_STEP)

    def body(w_ref, g_ref, m_ref, v_ref, d_ref, nm_ref, nv_ref):
        gv = g_ref[...]
        nm = ADAM_B1 * m_ref[...] + (1.0 - ADAM_B1) * gv
        nv = ADAM_B2 * v_ref[...] + (1.0 - ADAM_B2) * (gv * gv)
        d_ref[...] = -ADAM_LR * ((nm * c1) / (jnp.sqrt(nv * c2) + ADAM_EPS) + ADAM_WD * w_ref[...])
        nm_ref[...] = nm
        nv_ref[...] = nv

    o = jax.ShapeDtypeStruct((r, c), F32)
    return pl.pallas_call(
        body, out_shape=(o, o, o), grid=(grid,), in_specs=[sp, sp, sp, sp], out_specs=(sp, sp, sp),
        compiler_params=_params("parallel"), name=name)(w, g, m, v)


def _sum_slots(land, name):
    n, r, c = land.shape
    blk, grid, idx = _tiling_2d(r, c)

    def body(l_ref, o_ref):
        acc = l_ref[0].astype(F32)
        for i in range(1, n):
            acc = acc + l_ref[i].astype(F32)
        o_ref[...] = acc

    return pl.pallas_call(
        body, out_shape=jax.ShapeDtypeStruct((r, c), F32), grid=(grid,),
        in_specs=[pl.BlockSpec((n, *blk), lambda i: (0, *idx(i)))], out_specs=pl.BlockSpec(blk, idx),
        compiler_params=_params("parallel"), name=name)(land)


SLAB_COLS = 256


def _chip_sum(blocks, from_sibling, my_core, name):
    _, r, c = blocks.shape

    def body(core_ref, a_ref, b_ref, o_ref):
        o_ref[...] = _bf(a_ref[...].astype(F32) + b_ref[...].astype(F32))

    def slab(chip_of):
        return pl.BlockSpec((1, r, SLAB_COLS), lambda j, i, core: (chip_of(j, core), 0, i))

    grid_spec = pltpu.PrefetchScalarGridSpec(
        num_scalar_prefetch=1, grid=(4, c // SLAB_COLS),
        in_specs=[slab(lambda j, core: 2 * j + core[0]), slab(lambda j, core: j)],
        out_specs=slab(lambda j, core: j))
    return pl.pallas_call(
        body, out_shape=jax.ShapeDtypeStruct((4, r, c), BF16), grid_spec=grid_spec,
        compiler_params=_params("parallel", "parallel"), name=name,
    )(my_core.astype(jnp.int32).reshape(1), blocks, from_sibling)


def _sum_row_parts(parts, name):
    c = parts[0].shape[2]
    rows = [p.shape[1] for p in parts]

    def body(*refs):
        o_ref, off = refs[-1], 0
        for p_ref, n in zip(refs[:-1], rows):
            acc = p_ref[0].astype(F32)
            for i in range(1, p_ref.shape[0]):
                acc = acc + p_ref[i].astype(F32)
            o_ref[off:off + n, :] = acc
            off += n

    return pl.pallas_call(
        body, out_shape=jax.ShapeDtypeStruct((sum(rows), c), F32), grid=(c // SLAB_COLS,),
        in_specs=[pl.BlockSpec((p.shape[0], p.shape[1], SLAB_COLS), lambda i: (0, 0, i)) for p in parts],
        out_specs=pl.BlockSpec((sum(rows), SLAB_COLS), lambda i: (0, i)),
        compiler_params=_params("parallel"), name=name)(*parts)


def _retention_tables():
    lg = np.log1p(-(2.0 ** (-5.0 - np.arange(RET_HEADS, dtype=np.float64))))
    idx = np.arange(CHUNK, dtype=np.float64)
    intra = np.exp(np.abs(idx[:, None] - idx[None, :])[None] * lg[:, None, None])
    qd = np.exp((idx[None, :] + 1.0) * lg[:, None])
    kd = np.exp((CHUNK - 1.0 - idx[None, :]) * lg[:, None])
    cd = np.exp(CHUNK * lg)
    ones = np.ones((1, 1, RET_DK))
    return (jnp.asarray(intra, F32), jnp.asarray(qd[:, :, None] * ones, F32),
            jnp.asarray(kd[:, :, None] * ones, F32), jnp.asarray(cd[:, None, None] * ones, F32))


def _rope(t, cos, sin):
    t1, t2 = t[:, :ROPE_HALF], t[:, ROPE_HALF:]
    return jnp.concatenate([t1 * cos - t2 * sin, t2 * cos + t1 * sin], axis=1)


def _rope_t(d, cos, sin):
    d1, d2 = d[:, :ROPE_HALF], d[:, ROPE_HALF:]
    return jnp.concatenate([d1 * cos + d2 * sin, d2 * cos - d1 * sin], axis=1)


_HEADS = range(RET_HEADS)


def _ret_chunk_fwd(p_ref, cos, sin, intra_ref, qd_ref, st):
    def seg(i, h):
        return p_ref[:, i * RET_W + h * RET_DK:i * RET_W + (h + 1) * RET_DK]

    v = [seg(2, h) for h in _HEADS]
    qr = [_rope(seg(0, h), cos, sin) for h in _HEADS]
    kr = [_rope(seg(1, h), cos, sin) * (RET_DK ** -0.5) for h in _HEADS]
    qrb, vb = [_bf(a) for a in qr], [_bf(a) for a in v]
    sc = [_dot_nt(qrb[h], _bf(kr[h])) * intra_ref[h] for h in _HEADS]
    qs = [_dot(qrb[h], _bf(st[h])) for h in _HEADS]
    y = [_dot(_bf(sc[h]), vb[h]) + qs[h] * qd_ref[h] for h in _HEADS]
    g = [seg(3, h) for h in _HEADS]
    return dict(v=v, vb=vb, qr=qr, qrb=qrb, kr=kr, sc=sc, y=y, g=g)


def _group_norm(y):
    mu = jnp.mean(y, axis=1, keepdims=True)
    yc = y - mu
    rstd = lax.rsqrt(jnp.mean(yc * yc, axis=1, keepdims=True) + EPS)
    return yc * rstd, rstd


def _chunk_rows(sub):
    return pl.ds(sub * CHUNK, CHUNK)


def _ret_specs(steps, rev):
    cidx = (lambda c: steps - 1 - c) if rev else (lambda c: c)
    return dict(
        proj=pl.BlockSpec((RET_STEP_ROWS, 4 * RET_W), lambda c: (cidx(c), 0)),
        half=pl.BlockSpec((RET_STEP_ROWS, ROPE_HALF), lambda c: (cidx(c), 0)),
        wide=pl.BlockSpec((RET_STEP_ROWS, RET_W), lambda c: (cidx(c), 0)),
        state=pl.BlockSpec((RET_CHUNKS_PER_STEP, RET_HEADS, RET_DK, RET_DK), lambda c: (cidx(c), 0, 0, 0)),
        intra=_full_spec((RET_HEADS, CHUNK, CHUNK)),
        dec=_full_spec((RET_HEADS, CHUNK, RET_DK)),
        cd=_full_spec((RET_HEADS, 1, RET_DK)),
    )


def _retention_fwd(p_r, cos, sin, name):
    s = p_r.shape[0]
    nc = s // CHUNK
    intra_t, qd_t, kd_t, cd_t = _retention_tables()

    def body(p_ref, cos_ref, sin_ref, intra_ref, qd_ref, kd_ref, cd_ref, y_ref, st_ref, state):
        @pl.when(pl.program_id(0) == 0)
        def _():
            state[...] = jnp.zeros_like(state)

        for sub in range(RET_CHUNKS_PER_STEP):
            rows = _chunk_rows(sub)
            y_v, st_v = y_ref.at[rows], st_ref.at[sub]
            st = [state[h] for h in _HEADS]
            f = _ret_chunk_fwd(p_ref.at[rows], cos_ref[rows, :], sin_ref[rows, :], intra_ref, qd_ref, st)
            new_st = [st[h] * cd_ref[h] + _dot_tn(_bf(f["kr"][h] * kd_ref[h]), f["vb"][h]) for h in _HEADS]
            out = [_bf(_group_norm(f["y"][h])[0] * (f["g"][h] * _sigmoid(f["g"][h]))) for h in _HEADS]
            for h in _HEADS:
                st_v[h] = _bf(st[h])
                state[h] = new_st[h]
                y_v[:, h * RET_DK:(h + 1) * RET_DK] = out[h]

    steps = nc // RET_CHUNKS_PER_STEP
    sp = _ret_specs(steps, False)
    return pl.pallas_call(
        body,
        out_shape=(jax.ShapeDtypeStruct((s, RET_W), BF16),
                   jax.ShapeDtypeStruct((nc, RET_HEADS, RET_DK, RET_DK), BF16)),
        grid=(steps,),
        in_specs=[sp["proj"], sp["half"], sp["half"], sp["intra"], sp["dec"], sp["dec"], sp["cd"]],
        out_specs=(sp["wide"], sp["state"]),
        scratch_shapes=[pltpu.VMEM((RET_HEADS, RET_DK, RET_DK), F32)],
        compiler_params=_params("arbitrary"), name=name)(p_r, cos, sin, intra_t, qd_t, kd_t, cd_t)


def _retention_bwd(p_r, cos, sin, states, dy_r, name):
    s = p_r.shape[0]
    nc = s // CHUNK
    intra_t, qd_t, kd_t, cd_t = _retention_tables()

    def body(p_ref, cos_ref, sin_ref, intra_ref, qd_ref, kd_ref, cd_ref, st_ref, dy_ref, dp_ref, dstate):
        @pl.when(pl.program_id(0) == 0)
        def _():
            dstate[...] = jnp.zeros_like(dstate)

        for sub in reversed(range(RET_CHUNKS_PER_STEP)):
            rows = _chunk_rows(sub)
            chunk(p_ref.at[rows], cos_ref[rows, :], sin_ref[rows, :], intra_ref, qd_ref, kd_ref, cd_ref,
                  st_ref.at[sub], dy_ref.at[rows], dp_ref.at[rows], dstate)

    def chunk(p_ref, cos, sin, intra_ref, qd_ref, kd_ref, cd_ref, st_ref, dy_ref, dp_ref, dstate):
        st = [st_ref[h] for h in _HEADS]
        dsn = [dstate[h] for h in _HEADS]
        f = _ret_chunk_fwd(p_ref, cos, sin, intra_ref, qd_ref, st)
        vb, qrb, kr, g = f["vb"], f["qrb"], f["kr"], f["g"]
        norm = [_group_norm(f["y"][h]) for h in _HEADS]
        sg = [_sigmoid(g[h]) for h in _HEADS]
        dyr = [dy_ref[:, h * RET_DK:(h + 1) * RET_DK] for h in _HEADS]
        dyn = [dyr[h] * (g[h] * sg[h]) for h in _HEADS]
        dg = [dyr[h] * norm[h][0] * (sg[h] * (1.0 + g[h] * (1.0 - sg[h]))) for h in _HEADS]
        dy = [norm[h][1] * (dyn[h] - jnp.mean(dyn[h], axis=1, keepdims=True)
                            - norm[h][0] * jnp.mean(dyn[h] * norm[h][0], axis=1, keepdims=True)) for h in _HEADS]
        dyb, dsnb = [_bf(a) for a in dy], [_bf(a) for a in dsn]
        ds = [_bf(_dot_nt(dyb[h], vb[h]) * intra_ref[h]) for h in _HEADS]
        t = [_bf(dy[h] * qd_ref[h]) for h in _HEADS]
        dv = [_dot_tn(_bf(f["sc"][h]), dyb[h]) + _dot(_bf(kr[h] * kd_ref[h]), dsnb[h]) for h in _HEADS]
        dqr = [_dot(ds[h], _bf(kr[h])) + _dot_nt(t[h], _bf(st[h])) for h in _HEADS]
        dkr = [_dot_tn(ds[h], qrb[h]) + _dot_nt(vb[h], dsnb[h]) * kd_ref[h] for h in _HEADS]
        new_ds = [dsn[h] * cd_ref[h] + _dot_tn(qrb[h], t[h]) for h in _HEADS]
        for h in _HEADS:
            lo = h * RET_DK
            dstate[h] = new_ds[h]
            dp_ref[:, lo:lo + RET_DK] = _bf(_rope_t(dqr[h], cos, sin))
            dp_ref[:, RET_W + lo:RET_W + lo + RET_DK] = _bf(_rope_t(dkr[h], cos, sin) * (RET_DK ** -0.5))
            dp_ref[:, 2 * RET_W + lo:2 * RET_W + lo + RET_DK] = _bf(dv[h])
            dp_ref[:, 3 * RET_W + lo:3 * RET_W + lo + RET_DK] = _bf(dg[h])

    steps = nc // RET_CHUNKS_PER_STEP
    sp = _ret_specs(steps, True)
    return pl.pallas_call(
        body,
        out_shape=jax.ShapeDtypeStruct((s, 4 * RET_W), BF16),
        grid=(steps,),
        in_specs=[sp["proj"], sp["half"], sp["half"], sp["intra"], sp["dec"], sp["dec"], sp["cd"],
                  sp["state"], sp["wide"]],
        out_specs=sp["proj"],
        scratch_shapes=[pltpu.VMEM((RET_HEADS, RET_DK, RET_DK), F32)],
        compiler_params=_params("arbitrary"), name=name)(p_r, cos, sin, intra_t, qd_t, kd_t, cd_t, states, dy_r)


CONV_SLAB = 256
_CONV_SLABS = [slice(i * CONV_SLAB, (i + 1) * CONV_SLAB) for i in range(SSD_CD // CONV_SLAB)]


def _conv_taps(ext, w):
    acc = w[SSD_CONV - 1:SSD_CONV] * ext
    for j in range(SSD_CONV - 1):
        acc = acc + w[j:j + 1] * pltpu.roll(ext, SSD_CONV - 1 - j, axis=0)
    return acc


def _conv_fwd(xbc_raw, conv_w, conv_b, name):
    s = xbc_raw.shape[0]
    t8 = CONV_TILE // 8

    def body(cur_ref, prev_ref, w_ref, b_ref, o_ref, slope_ref):
        keep = (pl.program_id(0) > 0).astype(F32)
        for sl in _CONV_SLABS:
            ext = jnp.concatenate([prev_ref[:, sl] * keep, cur_ref[:, sl]], axis=0)
            u = _conv_taps(ext, w_ref[:, sl])[8:] + b_ref[:, sl]
            sg = _sigmoid(u)
            o_ref[:, sl] = u * sg
            slope_ref[:, sl] = _bf(sg * (1.0 + u * (1.0 - sg)))

    out = jax.ShapeDtypeStruct((s, SSD_CD), F32)
    return pl.pallas_call(
        body, out_shape=(out, jax.ShapeDtypeStruct((s, SSD_CD), BF16)), grid=(s // CONV_TILE,),
        in_specs=[_row_spec(SSD_CD, CONV_TILE),
                  pl.BlockSpec((8, SSD_CD), lambda i: (jnp.maximum(i * t8 - 1, 0), 0)),
                  _full_spec((SSD_CONV, SSD_CD)), _full_spec((1, SSD_CD))],
        out_specs=(_row_spec(SSD_CD, CONV_TILE), _row_spec(SSD_CD, CONV_TILE)),
        compiler_params=_params("parallel"), name=name)(xbc_raw, xbc_raw, conv_w, conv_b)


def _conv_bwd(xbc_raw, conv_w, slope, dact, name):
    s = xbc_raw.shape[0]
    nt = s // CONV_TILE
    t8 = CONV_TILE // 8
    rows = CONV_TILE + 8

    def body(cur_ref, w_ref, s_ref, snext_ref, d_ref, dnext_ref, dx_ref, dw_ref, db_ref):
        i = pl.program_id(0)

        @pl.when(i == 0)
        def _():
            dw_ref[...] = jnp.zeros_like(dw_ref)
            db_ref[...] = jnp.zeros_like(db_ref)

        keep_next = (i < nt - 1).astype(F32)
        for sl in _CONV_SLABS:
            w = w_ref[:, sl]
            cur = cur_ref[:, sl]
            duc = d_ref[:, sl].astype(F32) * s_ref[:, sl].astype(F32)
            after = (dnext_ref[:, sl].astype(F32) * snext_ref[:, sl].astype(F32))[:8] * keep_next
            du = jnp.concatenate([duc, after], axis=0)
            dx = w[SSD_CONV - 1:SSD_CONV] * duc
            for j in range(SSD_CONV - 1):
                ahead = pltpu.roll(du, rows - (SSD_CONV - 1 - j), axis=0)[:CONV_TILE]
                dx = dx + w[j:j + 1] * ahead
                dw_ref[j:j + 1, sl] += jnp.sum(cur * ahead, axis=0, keepdims=True)
            dx_ref[:, sl] = _bf(dx)
            dw_ref[SSD_CONV - 1:SSD_CONV, sl] += jnp.sum(cur * duc, axis=0, keepdims=True)
            db_ref[:, sl] += jnp.sum(duc, axis=0, keepdims=True)

    row = _row_spec(SSD_CD, CONV_TILE)
    nxt16 = pl.BlockSpec((16, SSD_CD), lambda i: (jnp.minimum((i + 1) * (t8 // 2), s // 16 - 1), 0))
    return pl.pallas_call(
        body,
        out_shape=(jax.ShapeDtypeStruct((s, SSD_CD), BF16), jax.ShapeDtypeStruct((SSD_CONV, SSD_CD), F32),
                   jax.ShapeDtypeStruct((1, SSD_CD), F32)),
        grid=(nt,),
        in_specs=[row, _full_spec((SSD_CONV, SSD_CD)), row, nxt16, row, nxt16],
        out_specs=(row, _full_spec((SSD_CONV, SSD_CD)), _full_spec((1, SSD_CD))),
        compiler_params=_params("arbitrary"), name=name)(xbc_raw, conv_w, slope, slope, dact, dact)


def _head_select():
    e = np.zeros((HEAD_PAD, SSD_W), np.float32)
    for h in range(SSD_HEADS):
        e[h, h * SSD_P:(h + 1) * SSD_P] = 1.0
    return jnp.asarray(e, BF16), jnp.asarray(e.T, BF16)


def _pad_heads(v):
    return jnp.pad(v.reshape(1, SSD_HEADS).astype(F32), ((0, 0), (0, HEAD_PAD - SSD_HEADS)))


def _ssd_masks():
    r = _iota((CHUNK, SSD_QW), 0)
    c = _iota((CHUNK, SSD_QW), 1) % SSD_P
    itile = (r == c).astype(F32)
    ctile = r >= c
    rb = _iota((SSD_QW, SSD_QW), 0) // SSD_P
    cb = _iota((SSD_QW, SSD_QW), 1) // SSD_P
    return itile, ctile, rb == cb


_TILES = [slice(i * SSD_QW, (i + 1) * SSD_QW) for i in range(SSD_W // SSD_QW)]


def _ssd_heads(dtr_ref, dtb_ref, alog_ref):
    u = dtr_ref[...] + dtb_ref[...]
    dt = _softplus(u)
    nexp = -jnp.exp(alog_ref[...])
    return u, dt, nexp, dt * nexp


def _ssd_group(xbc_ref, acol_ref, dte_ref, gs):
    acol = acol_ref[:, gs]
    alast = acol[CHUNK - 1:CHUNK, :]
    xs, dte = xbc_ref[:, gs], dte_ref[:, gs]
    return dict(xs=xs, dte=dte, xdt=xs * dte, ea=jnp.exp(acol), tail=jnp.exp(alast - acol), eal=jnp.exp(alast))


def _silu_gate(z_ref, y_ref, tl):
    zv = z_ref[:, tl]
    sz = _sigmoid(zv)
    return zv, sz, y_ref[:, tl] * (zv * sz)


def _tile4(x):
    return jnp.concatenate([x, x, x, x], axis=0)


def _fold4(x):
    return x[0:CHUNK] + x[CHUNK:2 * CHUNK] + x[2 * CHUNK:3 * CHUNK] + x[3 * CHUNK:4 * CHUNK]


def _ssd_specs(steps, rev):
    cidx = (lambda c: steps - 1 - c) if rev else (lambda c: c)
    return dict(
        xbc=pl.BlockSpec((STEP_ROWS, SSD_CD), lambda c: (cidx(c), 0)),
        dt=pl.BlockSpec((STEP_ROWS, HEAD_PAD), lambda c: (cidx(c), 0)),
        wide=pl.BlockSpec((STEP_ROWS, SSD_W), lambda c: (cidx(c), 0)),
        state=pl.BlockSpec((CHUNKS_PER_STEP, SSD_N, SSD_W), lambda c: (cidx(c), 0, 0)),
        head=_full_spec((1, HEAD_PAD)),
        roww=_full_spec((1, SSD_W)),
        e=_full_spec((HEAD_PAD, SSD_W)),
        et=_full_spec((SSD_W, HEAD_PAD)),
    )


_GROUPS = range(SSD_GROUPS)
_GROUP_LANES = [slice(g * SSD_GW, (g + 1) * SSD_GW) for g in _GROUPS]
_QUADS = [(g, slice(g * SSD_GW + q * SSD_QW, g * SSD_GW + (q + 1) * SSD_QW), slice(q * SSD_QW, (q + 1) * SSD_QW))
          for g in _GROUPS for q in range(SSD_GW // SSD_QW)]


def _ssd_bc(xbc_ref):
    bg = [_bf(xbc_ref[:, B_OFF + g * SSD_N:B_OFF + (g + 1) * SSD_N]) for g in _GROUPS]
    cg = [_bf(xbc_ref[:, C_OFF + g * SSD_N:C_OFF + (g + 1) * SSD_N]) for g in _GROUPS]
    return bg, cg


def _ssd_decay(aq, itile, ctile):
    arow = jnp.sum(aq * itile, axis=0, keepdims=True)
    return jnp.exp(jnp.where(ctile, aq - arow, -jnp.inf))


def _ssd_blockdiag(xq, bdmask):
    return jnp.where(bdmask, _tile4(_bf(xq)), jnp.zeros((), BF16))


def _ssd_fwd(xbc, dt_raw, z, dt_bias, a_log, dske, norm_w, name):
    s = xbc.shape[0]
    nc = s // CHUNK
    e_sel, _ = _head_select()

    def body(xbc_ref, dtr_ref, z_ref, dtb_ref, alog_ref, dske_ref, nw_ref, e_ref,
             yraw_ref, ys_ref, st_ref, acol_ref, dte_ref, state):
        @pl.when(pl.program_id(0) == 0)
        def _():
            state[...] = jnp.zeros_like(state)

        for sub in range(CHUNKS_PER_STEP):
            rows = _chunk_rows(sub)
            chunk(xbc_ref.at[rows], dtr_ref.at[rows], z_ref.at[rows], dtb_ref, alog_ref, dske_ref, nw_ref, e_ref,
                  yraw_ref.at[rows], ys_ref.at[rows], st_ref.at[sub], acol_ref.at[rows], dte_ref.at[rows], state)

    def chunk(xbc_ref, dtr_ref, z_ref, dtb_ref, alog_ref, dske_ref, nw_ref, e_ref,
              yraw_ref, ys_ref, st_ref, acol_ref, dte_ref, state):
        _, dt, _, a = _ssd_heads(dtr_ref, dtb_ref, alog_ref)
        tril = _bf((_iota((CHUNK, CHUNK), 0) >= _iota((CHUNK, CHUNK), 1)).astype(F32))
        ac3, dt3 = _split3(_dot_exact_r(tril, a)), _split3(dt)
        for tl in _TILES:
            e_t = e_ref[:, tl]
            acol_ref[:, tl] = _dot(ac3[0], e_t) + _dot(ac3[1], e_t) + _dot(ac3[2], e_t)
            dte_ref[:, tl] = _dot(dt3[0], e_t) + _dot(dt3[1], e_t)

        itile, ctile, bdmask = _ssd_masks()
        st_ref[...] = state[...]
        bg, cg = _ssd_bc(xbc_ref)
        for g0 in range(0, SSD_GROUPS, SSD_BLOCK_GROUPS):
            gids = range(g0, g0 + SSD_BLOCK_GROUPS)
            quads = [(g, ql, qs) for g, ql, qs in _QUADS if g in gids]
            q = {g: _ssd_group(xbc_ref, acol_ref, dte_ref, _GROUP_LANES[g]) for g in gids}
            stg = {g: state[:, _GROUP_LANES[g]] for g in gids}
            cbt = {g: _dot_nt(cg[g], _tile4(bg[g])) for g in gids}
            ys = {g: _dot(cg[g], _bf(stg[g])) for g in gids}
            dq = [_ssd_decay(acol_ref[:, ql], itile, ctile) for _, ql, _ in quads]
            xbd = [_ssd_blockdiag(q[g]["xdt"][:, qs], bdmask) for g, _, qs in quads]
            yq = [_dot(_bf(cbt[g] * dq[i]), xbd[i]) + ys[g][:, qs] * q[g]["ea"][:, qs]
                  + dske_ref[:, ql] * q[g]["xs"][:, qs] for i, (g, ql, qs) in enumerate(quads)]
            new_st = {g: stg[g] * q[g]["eal"] + _dot_tn(bg[g], _bf(q[g]["xdt"] * q[g]["tail"])) for g in gids}
            for i, (_, ql, _) in enumerate(quads):
                yraw_ref[:, ql] = yq[i]
            for g in gids:
                state[:, _GROUP_LANES[g]] = new_st[g]

        sq = jnp.zeros((CHUNK, SSD_QW), F32)
        for tl in _TILES:
            t = _silu_gate(z_ref, yraw_ref, tl)[2]
            sq = sq + t * t
        rstd = lax.rsqrt(jnp.sum(sq, axis=1, keepdims=True) * (1.0 / SSD_W) + EPS)
        for tl in _TILES:
            ys_ref[:, tl] = _bf(_silu_gate(z_ref, yraw_ref, tl)[2] * rstd * nw_ref[:, tl])

    steps = nc // CHUNKS_PER_STEP
    sp = _ssd_specs(steps, False)
    wide = jax.ShapeDtypeStruct((s, SSD_W), F32)
    return pl.pallas_call(
        body,
        out_shape=(wide, jax.ShapeDtypeStruct((s, SSD_W), BF16), jax.ShapeDtypeStruct((nc, SSD_N, SSD_W), F32),
                   wide, wide),
        grid=(steps,),
        in_specs=[sp["xbc"], sp["dt"], sp["wide"], sp["head"], sp["head"], sp["roww"], sp["roww"], sp["e"]],
        out_specs=(sp["wide"], sp["wide"], sp["state"], sp["wide"], sp["wide"]),
        scratch_shapes=[pltpu.VMEM((SSD_N, SSD_W), F32)],
        compiler_params=_params("arbitrary"), name=name,
    )(xbc, dt_raw, z, dt_bias, a_log, dske, norm_w, e_sel)


def _ssd_bwd(xbc, dt_raw, z, yraw, states, dys, acol, dte, dt_bias, a_log, dske, norm_w, name, rider):
    s = xbc.shape[0]
    nc = s // CHUNK
    steps = nc // CHUNKS_PER_STEP
    _, et_sel = _head_select()
    n_in, n_out, n_scratch = 13, 7, 5
    n_rin, n_rout = len(rider.inputs), len(rider.out_shapes)

    def body(*refs):
        ins, refs = refs[:n_in], refs[n_in:]
        r_ins, refs = refs[:n_rin], refs[n_rin:]
        outs, refs = refs[:n_out], refs[n_out:]
        r_outs, refs = refs[:n_rout], refs[n_rout:]
        ride = functools.partial(rider.run, pl.program_id(0), steps, r_ins, r_outs, refs[n_scratch],
                                 refs[n_scratch + 1])
        ride(last=False)
        compute(*ins, *outs, *refs[:n_scratch])
        ride(last=True)

    def compute(xbc_ref, dtr_ref, z_ref, yraw_ref, st_ref, dys_ref, acol_ref, dte_ref, dtb_ref, alog_ref,
                dske_ref, nw_ref, et_ref, dxbc_ref, dz_ref, ddtr_ref, ddtb_ref, dalog_ref, ddsk_ref, dnw_ref,
                dstate, dsk_acc, dy_s, dacol_s, ddte_s):
        @pl.when(pl.program_id(0) == 0)
        def _():
            dstate[...] = jnp.zeros_like(dstate)
            dsk_acc[...] = jnp.zeros_like(dsk_acc)
            ddtb_ref[...] = jnp.zeros_like(ddtb_ref)
            dalog_ref[...] = jnp.zeros_like(dalog_ref)
            dnw_ref[...] = jnp.zeros_like(dnw_ref)

        for sub in reversed(range(CHUNKS_PER_STEP)):
            rows = _chunk_rows(sub)
            chunk(xbc_ref.at[rows], dtr_ref.at[rows], z_ref.at[rows], yraw_ref.at[rows], st_ref.at[sub],
                  dys_ref.at[rows], acol_ref.at[rows], dte_ref.at[rows], dtb_ref, alog_ref, dske_ref, nw_ref, et_ref,
                  dxbc_ref.at[rows], dz_ref.at[rows], ddtr_ref.at[rows], ddtb_ref, dalog_ref, dnw_ref,
                  dstate, dsk_acc, dy_s, dacol_s, ddte_s)

        @pl.when(pl.program_id(0) == steps - 1)
        def _():
            ddsk_ref[...] = _dot_exact_l(jnp.broadcast_to(dsk_acc[...], (8, SSD_W)), et_ref[...])[0:1]

    def chunk(xbc_ref, dtr_ref, z_ref, yraw_ref, st_ref, dys_ref, acol_ref, dte_ref, dtb_ref, alog_ref,
              dske_ref, nw_ref, et_ref, dxbc_ref, dz_ref, ddtr_ref, ddtb_ref, dalog_ref, dnw_ref,
              dstate, dsk_acc, dy_s, dacol_s, ddte_s):
        itile, ctile, bdmask = _ssd_masks()
        last_row = (_iota((CHUNK, 1), 0) == CHUNK - 1).astype(F32)

        s1 = jnp.zeros((CHUNK, SSD_QW), F32)
        s2 = jnp.zeros((CHUNK, SSD_QW), F32)
        for tl in _TILES:
            t = _silu_gate(z_ref, yraw_ref, tl)[2]
            s1 = s1 + t * t
            s2 = s2 + dys_ref[:, tl] * nw_ref[:, tl] * t
        rstd = lax.rsqrt(jnp.sum(s1, axis=1, keepdims=True) * (1.0 / SSD_W) + EPS)
        back = rstd * rstd * rstd * jnp.sum(s2, axis=1, keepdims=True) * (1.0 / SSD_W)
        for tl in _TILES:
            zv, sz, t = _silu_gate(z_ref, yraw_ref, tl)
            dysv = dys_ref[:, tl]
            dt_ = rstd * (dysv * nw_ref[:, tl]) - t * back
            dnw_ref[:, tl] += jnp.sum(dysv * t * rstd, axis=0, keepdims=True)
            dz_ref[:, tl] = _bf(dt_ * yraw_ref[:, tl] * (sz * (1.0 + zv * (1.0 - sz))))
            dy_t = dt_ * (zv * sz)
            dy_s[:, tl] = dy_t
            dsk_acc[:, tl] += jnp.sum(dy_t * xbc_ref[:, tl], axis=0, keepdims=True)

        gl = _GROUP_LANES
        bg, cg = _ssd_bc(xbc_ref)
        for g0 in range(0, SSD_GROUPS, SSD_BLOCK_GROUPS):
            gids = range(g0, g0 + SSD_BLOCK_GROUPS)
            quads = [(i, g, ql, qs) for i, (g, ql, qs) in enumerate(_QUADS) if g in gids]
            q = {g: _ssd_group(xbc_ref, acol_ref, dte_ref, gl[g]) for g in gids}
            stg = {g: st_ref[:, gl[g]] for g in gids}
            dsn = {g: dstate[:, gl[g]] for g in gids}
            stgb, dsnb = {g: _bf(stg[g]) for g in gids}, {g: _bf(dsn[g]) for g in gids}
            btile = {g: _tile4(bg[g]) for g in gids}
            cbt = {g: _dot_nt(cg[g], btile[g]) for g in gids}
            dyg = {g: dy_s[:, gl[g]] for g in gids}
            eag, tailg = {g: q[g]["ea"] for g in gids}, {g: q[g]["tail"] for g in gids}
            xdtg, ealg = {g: q[g]["xdt"] for g in gids}, {g: q[g]["eal"] for g in gids}
            dys_g = {g: _bf(dyg[g] * eag[g]) for g in gids}
            ysg = {g: _dot(cg[g], stgb[g]) for g in gids}
            dc0 = {g: _dot_nt(dys_g[g], stgb[g]) for g in gids}
            dst = {g: _dot_tn(cg[g], dys_g[g]) for g in gids}
            dwt = {g: _dot(bg[g], dsnb[g]) for g in gids}
            db0 = {g: _dot_nt(_bf(xdtg[g] * tailg[g]), dsnb[g]) for g in gids}
            dtl = {g: dwt[g] * xdtg[g] * tailg[g] for g in gids}
            dal_row = {g: jnp.sum(dtl[g], axis=0, keepdims=True)
                       + jnp.sum(dsn[g] * stg[g], axis=0, keepdims=True) * ealg[g] for g in gids}
            for g in gids:
                dstate[:, gl[g]] = dst[g] + dsn[g] * ealg[g]
            dq = {i: _ssd_decay(acol_ref[:, ql], itile, ctile) for i, _, ql, _ in quads}
            mq = {i: cbt[g] * dq[i] for i, g, _, _ in quads}
            xbd = {i: _ssd_blockdiag(xdtg[g][:, qs], bdmask) for i, g, _, qs in quads}
            dyq = {i: _bf(dyg[g][:, qs]) for i, g, _, qs in quads}
            dm = {i: _dot_nt(dyq[i], xbd[i]) for i, _, _, _ in quads}
            dxdt_q = {i: _fold4(jnp.where(bdmask, _dot_tn(_bf(mq[i]), dyq[i]), 0.0)) for i, _, _, _ in quads}
            eq = {i: dm[i] * mq[i] for i, _, _, _ in quads}
            dacol_q = {i: eq[i] - itile * jnp.sum(eq[i], axis=0, keepdims=True) for i, _, _, _ in quads}
            dcbt = {i: _bf(dm[i] * dq[i]) for i, _, _, _ in quads}
            dc_q = {i: _dot(dcbt[i], btile[g]) for i, g, _, _ in quads}
            db_q = {i: _fold4(_dot_tn(dcbt[i], cg[g])) for i, g, _, _ in quads}
            for g in gids:
                dxdt_g = jnp.concatenate([dxdt_q[2 * g], dxdt_q[2 * g + 1]], axis=1) + dwt[g] * tailg[g]
                dxbc_ref[:, gl[g]] = _bf(dyg[g] * dske_ref[:, gl[g]] + dxdt_g * q[g]["dte"])
                ddte_s[:, gl[g]] = dxdt_g * q[g]["xs"]
                dacol_s[:, gl[g]] = (jnp.concatenate([dacol_q[2 * g], dacol_q[2 * g + 1]], axis=1)
                                     + dyg[g] * (ysg[g] * eag[g]) - dtl[g] + last_row * dal_row[g])
                dxbc_ref[:, B_OFF + g * SSD_N:B_OFF + (g + 1) * SSD_N] = _bf(db0[g] + db_q[2 * g] + db_q[2 * g + 1])
                dxbc_ref[:, C_OFF + g * SSD_N:C_OFF + (g + 1) * SSD_N] = _bf(dc0[g] + dc_q[2 * g] + dc_q[2 * g + 1])

        dacum = jnp.zeros((CHUNK, HEAD_PAD), F32)
        ddt = jnp.zeros((CHUNK, HEAD_PAD), F32)
        for i, tl in enumerate(_TILES):
            et_t = et_ref[i * SSD_QW:(i + 1) * SSD_QW, :]
            dacum = dacum + _dot_exact_l(dacol_s[:, tl], et_t, pieces=2)
            ddt = ddt + _dot_exact_l(ddte_s[:, tl], et_t, pieces=2)
        u, _, nexp, a = _ssd_heads(dtr_ref, dtb_ref, alog_ref)
        triu = _bf((_iota((CHUNK, CHUNK), 1) >= _iota((CHUNK, CHUNK), 0)).astype(F32))
        da = _dot_exact_r(triu, dacum)
        ddt = ddt + da * nexp
        dalog_ref[...] += jnp.sum(da * a, axis=0, keepdims=True)
        du = ddt * _sigmoid(u)
        ddtr_ref[...] = _bf(du)
        ddtb_ref[...] += jnp.sum(du, axis=0, keepdims=True)

    sp = _ssd_specs(steps, True)
    return pl.pallas_call(
        body,
        out_shape=(jax.ShapeDtypeStruct((s, SSD_CD), BF16), jax.ShapeDtypeStruct((s, SSD_W), BF16),
                   jax.ShapeDtypeStruct((s, HEAD_PAD), BF16), jax.ShapeDtypeStruct((1, HEAD_PAD), F32),
                   jax.ShapeDtypeStruct((1, HEAD_PAD), F32), jax.ShapeDtypeStruct((1, HEAD_PAD), F32),
                   jax.ShapeDtypeStruct((1, SSD_W), F32), *rider.out_shapes),
        grid=(steps,),
        in_specs=[sp["xbc"], sp["dt"], sp["wide"], sp["wide"], sp["state"], sp["wide"], sp["wide"], sp["wide"],
                  sp["head"], sp["head"], sp["roww"], sp["roww"], sp["et"], *rider.in_specs()],
        out_specs=(sp["xbc"], sp["wide"], sp["dt"], sp["head"], sp["head"], sp["head"], sp["roww"],
                   *rider.out_specs()),
        scratch_shapes=[pltpu.VMEM((SSD_N, SSD_W), F32), pltpu.VMEM((1, SSD_W), F32),
                        *[pltpu.VMEM((CHUNK, SSD_W), F32)] * 3, *rider.scratch()],
        compiler_params=_params("arbitrary"), name=name,
    )(xbc, dt_raw, z, yraw, states, dys, acol, dte, dt_bias, a_log, dske, norm_w, et_sel, *rider.inputs)


W_IN_SHARD = IN_PROJ // N_DEV
W_IN_ROW_PARTS = ((0, 1152), (1152, 832), (1984, 840))


def _device_step(x, pos_col, target, norm1_w, conv_w_shard, conv_b, dt_bias, a_log, d_skip, ssd_norm_w, norm_f_w,
                 w_in_shard, br_ret_shard, br_ssd_shard, out_shard):
    inv_freq = jnp.asarray(ROPE_THETA ** (-np.arange(ROPE_HALF, dtype=np.float64) / ROPE_HALF), F32).reshape(1, ROPE_HALF)
    dtb, alog = _pad_heads(dt_bias), _pad_heads(a_log)
    dske = jnp.repeat(d_skip.reshape(SSD_HEADS).astype(F32), SSD_P).reshape(1, SSD_W)
    my_core = lax.axis_index("c")

    h, cos, sin, br_ret_shard, br_ssd_shard, out_shard, w_all = _rmsnorm_rope_fwd(
        x, norm1_w, pos_col, inv_freq, [br_ret_shard, br_ssd_shard, out_shard], "rmsnorm1_fwd",
        _gather_rider([w_in_shard], 1.0, 1.0))
    w_all = w_all.reshape(IN_PROJ, D_MODEL)
    w_dt = jnp.pad(w_all[OFF_DT:OFF_G], ((0, HEAD_PAD - SSD_HEADS), (0, 0)))
    w_g = w_all[OFF_G:]
    rows_r, rows_z, rows_xbc = (0, OFF_Z), (OFF_Z, SSD_W), (OFF_XBC, SSD_CD)
    gather = _gather_rider([br_ret_shard, br_ssd_shard, out_shard, conv_w_shard], relay_at=0.35, pass_at=0.6)
    p_r, all_ret, all_ssd, all_out, all_conv = _matmul(h, w_all, "nt", F32, "proj_ret", rider=gather, b_rows=rows_r)
    w_br_ret = all_ret.reshape(RET_W, D_MODEL)
    w_br_ssd = all_ssd.reshape(SSD_W, D_MODEL)
    w_out = all_out.reshape(D_MODEL, D_MODEL)
    conv_w = all_conv.transpose(1, 0, 2).reshape(SSD_CONV, SSD_CD)
    p_z = _matmul(h, w_all, "nt", F32, "proj_z", b_rows=rows_z)
    p_xbc = _matmul(h, w_all, "nt", F32, "proj_xbc", b_rows=rows_xbc)
    p_dt = _matmul(h, w_dt, "nt", F32, "proj_dt")
    p_g = _matmul(h, w_g, "nt", BF16, "proj_gates")
    y_r, ret_states = _retention_fwd(p_r, cos, sin, "retention_fwd")
    xbc_act, silu_slope = _conv_fwd(p_xbc, conv_w, conv_b, "conv_fwd")
    y_raw, y_s, ssd_states, acol, dte = _ssd_fwd(xbc_act, p_dt, p_z, dtb, alog, dske, ssd_norm_w, "ssd_fwd")
    pr = _matmul(y_r, w_br_ret, "nn", BF16, "branch_ret")
    ps = _matmul(y_s, w_br_ssd, "nn", BF16, "branch_ssd")
    merged = _merge_fwd(p_g, pr, ps, "merge_fwd")
    o = _matmul(merged, w_out, "nn", F32, "out_proj")
    dx2, g_norm_f, loss_acc, dx2_bf = _final_fwd_bwd(x, o, norm_f_w, target, "final_norm_loss")

    g_w_out = _matmul(merged, dx2_bf, "tn", BF16, "grad_w_out")
    dmerged = _matmul(dx2_bf, w_out, "nt", F32, "d_merged")
    dpr, dps, dp_g = _merge_bwd(p_g, pr, ps, dmerged, "merge_bwd")
    g_w_br_ret = _matmul(y_r, dpr, "tn", BF16, "grad_w_br_ret")
    g_w_br_ssd = _matmul(y_s, dps, "tn", BF16, "grad_w_br_ssd")
    dy_r = _matmul(dpr, w_br_ret, "nt", F32, "d_y_ret")
    dy_s = _matmul(dps, w_br_ssd, "nt", F32, "d_y_ssd")
    scatter = _scatter_rider([g_w_out.reshape(N_DEV, -1, D_MODEL), g_w_br_ret.reshape(N_DEV, -1, D_MODEL),
                              g_w_br_ssd.reshape(N_DEV, -1, D_MODEL)])
    dxbc_act, dp_z, dp_dt, g_dtb, g_alog, g_dsk, g_ssd_norm, got_out, got_ret, got_ssd = _ssd_bwd(
        xbc_act, p_dt, p_z, y_raw, ssd_states, dy_s, acol, dte, dtb, alog, dske, ssd_norm_w, "ssd_bwd", scatter)
    dp_xbc, g_conv_w, g_conv_b = _conv_bwd(p_xbc, conv_w, silu_slope, dxbc_act, "conv_bwd")
    dp_r = _retention_bwd(p_r, cos, sin, ret_states, dy_r, "retention_bwd")
    g_w_in = jnp.concatenate([
        _matmul(dp_r, h, "tn", BF16, "grad_w_ret"),
        _matmul(dp_z, h, "tn", BF16, "grad_w_z"),
        _matmul(dp_xbc, h, "tn", BF16, "grad_w_xbc"),
        _matmul(dp_dt, h, "tn", BF16, "grad_w_dt")[:SSD_HEADS],
        _matmul(dp_g, h, "tn", BF16, "grad_w_gates"),
    ], axis=0)
    blocks = g_w_in.reshape(N_DEV, W_IN_SHARD, D_MODEL)
    dh, from_sibling = _matmul(dp_r, w_all, "nn", F32, "d_h_ret", rider=_sibling_rider(blocks), b_rows=rows_r)
    chip_sum = _chip_sum(blocks, from_sibling, my_core, "w_in_chip_sum")
    carriers = (("d_h_xbc", dp_xbc, w_all, rows_xbc), ("d_h_gates", dp_g, w_g, None), ("d_h_z", dp_z, w_all, rows_z))
    landed = []
    for (row0, rows), (nm, dp, w, b_rows) in zip(W_IN_ROW_PARTS, carriers):
        dt_pair = (dp_dt, w_dt) if nm == "d_h_z" else None
        dh, got = _matmul(dp, w, "nn", F32, nm, add=dh, rider=_chip_rider(chip_sum, row0, rows), b_rows=b_rows,
                          extra=dt_pair)
        landed.append(got)
    grad_x, g_norm1 = _rmsnorm_bwd(x, norm1_w, dh, dx2, "rmsnorm1_bwd")
    small = dict(norm1_w=g_norm1, conv_w=g_conv_w, conv_b=g_conv_b, dt_bias=g_dtb[:, :SSD_HEADS],
                 a_log=g_alog[:, :SSD_HEADS], d_skip=g_dsk[:, :SSD_HEADS], ssd_norm_w=g_ssd_norm,
                 norm_f_w=g_norm_f)
    big = dict(w_in=_sum_row_parts(landed, "w_in_sum"), w_br_ret=_sum_slots(got_ret, "w_br_ret_sum"),
               w_br_ssd=_sum_slots(got_ssd, "w_br_ssd_sum"), w_out=_sum_slots(got_out, "w_out_sum"))
    return loss_acc[0, 0], grad_x, small, big


def _all_reduce_small(vec, name):
    r, c = vec.shape

    def body(x_ref, out_ref, land, send_sems, recv_sems):
        x, y, cc = _mesh_pos()
        my_idx = 4 * x + 2 * y + cc
        land[my_idx] = x_ref[...]
        copies = []
        for k in range(1, N_DEV):
            px, py, pc = x ^ (k >> 2), y ^ ((k >> 1) & 1), cc ^ (k & 1)
            cp = pltpu.make_async_remote_copy(
                src_ref=x_ref, dst_ref=land.at[my_idx],
                send_sem=send_sems.at[k - 1], recv_sem=recv_sems.at[k - 1],
                device_id=(px, py, pc), device_id_type=MESH)
            cp.start()
            copies.append(cp)
        for k in range(1, N_DEV):
            px, py, pc = x ^ (k >> 2), y ^ ((k >> 1) & 1), cc ^ (k & 1)
            pltpu.make_async_remote_copy(
                src_ref=x_ref, dst_ref=land.at[4 * px + 2 * py + pc],
                send_sem=send_sems.at[k - 1], recv_sem=recv_sems.at[k - 1],
                device_id=(px, py, pc), device_id_type=MESH).wait_recv()
        for cp in copies:
            cp.wait_send()
        acc = land[0]
        for i in range(1, N_DEV):
            acc = acc + land[i]
        out_ref[...] = acc

    return pl.pallas_call(
        body,
        out_shape=jax.ShapeDtypeStruct((r, c), F32),
        in_specs=[pl.BlockSpec(memory_space=pltpu.VMEM)],
        out_specs=pl.BlockSpec(memory_space=pltpu.VMEM),
        scratch_shapes=[pltpu.VMEM((N_DEV, r, c), F32), pltpu.SemaphoreType.DMA((7,)),
                        pltpu.SemaphoreType.DMA((7,))],
        name=name)(vec)


_SMALL = ("norm1_w", "conv_w", "conv_b", "dt_bias", "a_log", "d_skip", "ssd_norm_w", "norm_f_w")
_SMALL_COLS = 128
_W_IN = "w_in"
_WEIGHTS = ("norm1_w", "w_in", "conv_w", "conv_b", "dt_bias", "a_log", "d_skip", "ssd_norm_w",
            "w_br_ret", "w_br_ssd", "w_out", "norm_f_w")


def _pack(parts):
    flat = jnp.concatenate([p.reshape(-1).astype(F32) for p in parts])
    rows = -(-flat.shape[0] // (8 * _SMALL_COLS)) * 8
    return jnp.pad(flat, (0, rows * _SMALL_COLS - flat.shape[0])).reshape(rows, _SMALL_COLS)


def _unpack(packed, shapes):
    flat = packed.reshape(-1)
    out, off = [], 0
    for shp in shapes:
        n = int(np.prod(shp))
        out.append(flat[off:off + n].reshape(shp))
        off += n
    return out


def kernel(x, positions, norm1_w, w_in, conv_w, conv_b, dt_bias, a_log, d_skip, ssd_norm_w, w_br_ret, w_br_ssd, w_out, norm_f_w, loss_target, m_norm1_w, m_w_in, m_conv_w, m_conv_b, m_dt_bias, m_a_log, m_d_skip, m_ssd_norm_w, m_w_br_ret, m_w_br_ssd, m_w_out, m_norm_f_w, v_norm1_w, v_w_in, v_conv_w, v_conv_b, v_dt_bias, v_a_log, v_d_skip, v_ssd_norm_w, v_w_br_ret, v_w_br_ssd, v_w_out, v_norm_f_w):
    w = dict(norm1_w=norm1_w, w_in=w_in, conv_w=conv_w, conv_b=conv_b, dt_bias=dt_bias, a_log=a_log,
             d_skip=d_skip, ssd_norm_w=ssd_norm_w, w_br_ret=w_br_ret, w_br_ssd=w_br_ssd, w_out=w_out,
             norm_f_w=norm_f_w)
    m = dict(norm1_w=m_norm1_w, w_in=m_w_in, conv_w=m_conv_w, conv_b=m_conv_b, dt_bias=m_dt_bias,
             a_log=m_a_log, d_skip=m_d_skip, ssd_norm_w=m_ssd_norm_w, w_br_ret=m_w_br_ret,
             w_br_ssd=m_w_br_ssd, w_out=m_w_out, norm_f_w=m_norm_f_w)
    v = dict(norm1_w=v_norm1_w, w_in=v_w_in, conv_w=v_conv_w, conv_b=v_conv_b, dt_bias=v_dt_bias,
             a_log=v_a_log, d_skip=v_d_skip, ssd_norm_w=v_ssd_norm_w, w_br_ret=v_w_br_ret,
             w_br_ssd=v_w_br_ssd, w_out=v_w_out, norm_f_w=v_norm_f_w)
    s = x.shape[1]
    my_idx = 4 * lax.axis_index("x") + 2 * lax.axis_index("y") + lax.axis_index("c")

    w[_W_IN], m[_W_IN], v[_W_IN] = w_in[0].T, m_w_in[0].T, v_w_in[0].T

    loss_part, grad_x, g_small, g_big = _device_step(
        x[0], positions.reshape(s, 1), loss_target[0], norm1_w, conv_w[0], conv_b, dt_bias, a_log, d_skip,
        ssd_norm_w, norm_f_w.reshape(1, D_MODEL), _cast_bf16(w[_W_IN], "cast_w_in"),
        w_br_ret[0], w_br_ssd[0], w_out[0])

    small_shapes = [(1, 1)] + [g_small[n].shape for n in _SMALL]
    summed = _unpack(_all_reduce_small(_pack([loss_part.reshape(1, 1)] + [g_small[n] for n in _SMALL]),
                                       "allreduce_small"), small_shapes)
    loss = summed[0].reshape(())
    grads = dict(zip(_SMALL, summed[1:]))
    conv_cols = SSD_CD // N_DEV
    grads["conv_w"] = lax.dynamic_slice_in_dim(grads["conv_w"], my_idx * conv_cols, conv_cols, axis=1)
    grads["norm_f_w"] = grads["norm_f_w"].reshape(D_MODEL)
    for n in ("norm1_w", "conv_w", "conv_b", "dt_bias", "a_log", "d_skip", "ssd_norm_w"):
        grads[n] = grads[n].reshape(w[n].shape)

    delta, new_m, new_v = {}, {}, {}
    for n in ("w_br_ret", "w_br_ssd", "w_out"):
        w[n], m[n], v[n] = w[n][0], m[n][0], v[n][0]
    for n in (_W_IN, "w_br_ret", "w_br_ssd", "w_out"):
        back = (lambda a: a.T[None]) if n == _W_IN else (lambda a: a[None])
        res = _adamw(w[n], g_big[n], m[n], v[n], "adamw_" + n)
        grads[n] = back(g_big[n])
        delta[n], new_m[n], new_v[n] = (back(a) for a in res)
    shapes = [w[n].shape for n in _SMALL]
    packed = _adamw(_pack([w[n] for n in _SMALL]), _pack([grads[n] for n in _SMALL]),
                    _pack([m[n] for n in _SMALL]), _pack([v[n] for n in _SMALL]), "adamw_small")
    for res, dst in zip(packed, (delta, new_m, new_v)):
        for n, a in zip(_SMALL, _unpack(res, shapes)):
            dst[n] = a

    return (loss, grad_x.reshape(x.shape), *[grads[n] for n in _WEIGHTS], *[delta[n] for n in _WEIGHTS],
            *[new_m[n] for n in _WEIGHTS], *[new_v[n] for n in _WEIGHTS])
```
